```python
import jax
import jax.numpy as jnp
from jax import lax
import numpy as np

D_MODEL = 1024
BATCH = 8
SEQ = 16384
DEPTH = 4

N_MIXERS = 2
N_ATT_LAYERS = (DEPTH + 1) // N_MIXERS
N_HGRN_LAYERS = DEPTH // N_MIXERS
HEAD_DIM = 64
N_Q_HEADS = D_MODEL // HEAD_DIM
N_KV_HEADS = N_Q_HEADS // 4
GROUP = N_Q_HEADS // N_KV_HEADS
Q_DIM = N_Q_HEADS * HEAD_DIM
KV_DIM = N_KV_HEADS * HEAD_DIM
WINDOW = 128
ATT_BLOCK = 128
ROPE_DIM = HEAD_DIM // 4
ROPE_THETA = 500000.0
HGRN_EXPAND = 128
HGRN_HEADS = D_MODEL // HGRN_EXPAND
HGRN_KEY = HGRN_EXPAND
HGRN_VAL = D_MODEL // HGRN_HEADS
HGRN_KW = HGRN_HEADS * HGRN_KEY
HGRN_VW = HGRN_HEADS * HGRN_VAL
HGRN_IN_DIM = 3 * HGRN_KW + 2 * HGRN_VW
HGRN_CHUNK = 64
D_FF = -(-8 * D_MODEL // (3 * 256)) * 256
PLE_DIM = 256
ALPHA = (2 * DEPTH) ** 0.25
BETA = (8 * DEPTH) ** -0.25
LN_EPS = 1e-5

kernel_name = 'hybrid_swa_hgrn2_deepnorm_encoder'


def layer_norm(x, g, b):
    xf = x.astype(jnp.float32)
    mu = xf.mean(-1, keepdims=True)
    var = jnp.square(xf - mu).mean(-1, keepdims=True)
    y = (xf - mu) * lax.rsqrt(var + LN_EPS) * g.astype(jnp.float32) + b.astype(jnp.float32)
    return y.astype(x.dtype)


def rope_tables(seq):
    inv = ROPE_THETA ** (-jnp.arange(0, ROPE_DIM, 2, dtype=jnp.float32) / ROPE_DIM)
    ang = jnp.arange(seq, dtype=jnp.float32)[:, None] * inv[None, :]
    return jnp.cos(ang), jnp.sin(ang)


def apply_partial_rope(x, cos, sin):
    half = ROPE_DIM // 2
    c = cos[None, :, None, :]
    s = sin[None, :, None, :]
    xr = x[..., :ROPE_DIM].astype(jnp.float32)
    x1, x2 = xr[..., :half], xr[..., half:]
    rot = jnp.concatenate([x1 * c - x2 * s, x2 * c + x1 * s], axis=-1).astype(x.dtype)
    return jnp.concatenate([rot, x[..., ROPE_DIM:]], axis=-1)


def windowed_gqa(h, w_qkv, sink, w_o, cos, sin):
    B, S, _ = h.shape
    qkv = h @ w_qkv
    q = qkv[..., :Q_DIM].reshape(B, S, N_Q_HEADS, HEAD_DIM)
    k = qkv[..., Q_DIM:Q_DIM + KV_DIM].reshape(B, S, N_KV_HEADS, HEAD_DIM)
    v = qkv[..., Q_DIM + KV_DIM:].reshape(B, S, N_KV_HEADS, HEAD_DIM)
    q = apply_partial_rope(q, cos, sin).reshape(B, S, N_KV_HEADS, GROUP, HEAD_DIM)
    k = apply_partial_rope(k, cos, sin)
    pad = ((0, 0), (ATT_BLOCK, ATT_BLOCK), (0, 0), (0, 0))
    kp = jnp.pad(k, pad)
    vp = jnp.pad(v, pad)
    n_blocks = S // ATT_BLOCK
    scale = HEAD_DIM ** -0.5
    sink_f = sink.astype(jnp.float32).reshape(N_KV_HEADS, GROUP)[None, :, :, None, None]

    def block_attn(bi):
        start = bi * ATT_BLOCK
        qb = lax.dynamic_slice_in_dim(q, start, ATT_BLOCK, axis=1).astype(jnp.float32)
        kb = lax.dynamic_slice_in_dim(kp, start, 3 * ATT_BLOCK, axis=1).astype(jnp.float32)
        vb = lax.dynamic_slice_in_dim(vp, start, 3 * ATT_BLOCK, axis=1)
        s = jnp.einsum('bqhgd,bkhd->bhgqk', qb, kb) * scale
        qpos = start + jnp.arange(ATT_BLOCK)
        kpos = start - ATT_BLOCK + jnp.arange(3 * ATT_BLOCK)
        valid = (jnp.abs(qpos[:, None] - kpos[None, :]) <= WINDOW) & (kpos >= 0)[None, :] & (kpos < S)[None, :]
        s = jnp.where(valid, s, -jnp.inf)
        m = jnp.maximum(s.max(-1, keepdims=True), sink_f)
        pr = jnp.exp(s - m)
        den = pr.sum(-1, keepdims=True) + jnp.exp(sink_f - m)
        return jnp.einsum('bhgqk,bkhd->bqhgd', (pr / den).astype(vb.dtype), vb)

    o = lax.map(block_attn, jnp.arange(n_blocks))
    o = o.transpose(1, 0, 2, 3, 4, 5).reshape(B, S, Q_DIM)
    return o @ w_o


def gated_linear_scan(q, k, v, log_f):
    B, S, H, K = q.shape
    V = v.shape[-1]
    nc = S // HGRN_CHUNK

    def to_chunks(t):
        return t.reshape(B, nc, HGRN_CHUNK, H, t.shape[-1]).transpose(1, 0, 3, 2, 4)

    lower = jnp.tril(jnp.ones((HGRN_CHUNK, HGRN_CHUNK), dtype=bool))

    def step(state, inp):
        qc, kc, vc, gc = inp
        b = jnp.cumsum(gc, axis=2)
        inter = jnp.einsum('bhtk,bhkv->bhtv', qc * jnp.exp(b), state)
        diff = b[:, :, :, None, :] - b[:, :, None, :, :]
        decay = jnp.exp(jnp.where(lower[:, :, None], diff, -jnp.inf))
        scores = jnp.einsum('bhtk,bhsk,bhtsk->bhts', qc, kc, decay)
        intra = jnp.einsum('bhts,bhsv->bhtv', scores, vc)
        b_last = b[:, :, -1:, :]
        new_state = jnp.exp(b_last[:, :, 0, :])[..., None] * state + jnp.einsum('bhsk,bhsv->bhkv', kc * jnp.exp(b_last - b), vc)
        return new_state, inter + intra

    state0 = jnp.zeros((B, H, K, V), q.dtype)
    _, out = lax.scan(step, state0, (to_chunks(q), to_chunks(k), to_chunks(v), to_chunks(log_f)))
    return out.transpose(1, 0, 3, 2, 4).reshape(B, S, H, V)


def hgrn2_bidirectional(h, w_in, lb_fwd, lb_bwd, norm_g, w_o):
    B, S, _ = h.shape
    z = h @ w_in
    o1, o2, o3, o4 = HGRN_KW, 2 * HGRN_KW, 3 * HGRN_KW, 3 * HGRN_KW + HGRN_VW
    q = jax.nn.silu(z[..., :o1].astype(jnp.float32)).reshape(B, S, HGRN_HEADS, HGRN_KEY)
    v = z[..., o3:o4].astype(jnp.float32).reshape(B, S, HGRN_HEADS, HGRN_VAL)
    gate = z[..., o4:].astype(jnp.float32).reshape(B, S, HGRN_HEADS, HGRN_VAL)

    def forget(zf, lb):
        sig = jax.nn.sigmoid(zf.astype(jnp.float32))
        f = lb + (1.0 - lb) * sig
        k = (1.0 - lb) * (1.0 - sig)
        shape = (B, S, HGRN_HEADS, HGRN_KEY)
        return jnp.log(f).reshape(shape), k.reshape(shape)

    logf_f, k_f = forget(z[..., o1:o2], lb_fwd)
    logf_b, k_b = forget(z[..., o2:o3], lb_bwd)
    out_f = gated_linear_scan(q, k_f, v, logf_f)
    flip = lambda t: jnp.flip(t, axis=1)
    out_b = flip(gated_linear_scan(flip(q), flip(k_b), flip(v), flip(logf_b)))
    o = out_f + out_b
    o = o * lax.rsqrt(jnp.mean(o * o, axis=-1, keepdims=True) + LN_EPS) * norm_g.astype(jnp.float32)
    o = o * jax.nn.silu(gate)
    return o.reshape(B, S, HGRN_VW).astype(h.dtype) @ w_o


def swiglu(x, w_in, w_out):
    gu = x @ w_in
    g, u = jnp.split(gu, 2, axis=-1)
    return (jax.nn.silu(g) * u) @ w_out


def _fwd_setup_inputs(seed: int = 0) -> dict:
    key = jax.random.key(seed)
    ks = jax.random.split(key, 17)
    f32 = jnp.float32
    D = D_MODEL

    def nrm(k, shape, scale):
        return jax.random.normal(k, shape, f32) * scale

    return {
        'x': nrm(ks[0], (BATCH, SEQ, D), 1.0),
        'p': nrm(ks[1], (DEPTH, BATCH, SEQ, PLE_DIM), 1.0),
        'att_w_qkv': nrm(ks[2], (N_ATT_LAYERS, D, Q_DIM + 2 * KV_DIM), D ** -0.5),
        'att_sink': nrm(ks[3], (N_ATT_LAYERS, N_Q_HEADS), 0.5),
        'att_w_o': nrm(ks[4], (N_ATT_LAYERS, Q_DIM, D), BETA * Q_DIM ** -0.5),
        'hgrn_w_in': nrm(ks[5], (N_HGRN_LAYERS, D, HGRN_IN_DIM), D ** -0.5),
        'hgrn_lb_logits': nrm(ks[6], (DEPTH, 2, HGRN_KW), 0.1),
        'hgrn_norm_g': 1.0 + nrm(ks[7], (N_HGRN_LAYERS, HGRN_VAL), 0.01),
        'hgrn_w_o': nrm(ks[8], (N_HGRN_LAYERS, HGRN_VW, D), BETA * HGRN_VW ** -0.5),
        'ln_mix_g': 1.0 + nrm(ks[9], (DEPTH, D), 0.01),
        'ln_mix_b': nrm(ks[10], (DEPTH, D), 0.01),
        'ffn_w_in': nrm(ks[11], (DEPTH, D, 2 * D_FF), D ** -0.5),
        'ffn_w_out': nrm(ks[12], (DEPTH, D_FF, D), BETA * D_FF ** -0.5),
        'ln_ffn_g': 1.0 + nrm(ks[13], (DEPTH, D), 0.01),
        'ln_ffn_b': nrm(ks[14], (DEPTH, D), 0.01),
        'ple_w_gate': nrm(ks[15], (DEPTH, D, D), D ** -0.5),
        'ple_w_proj': nrm(ks[16], (DEPTH, PLE_DIM, D), BETA * PLE_DIM ** -0.5),
    }


def _fwd_reference(x, p, att_w_qkv, att_sink, att_w_o, hgrn_w_in, hgrn_lb_logits, hgrn_norm_g, hgrn_w_o,
              ln_mix_g, ln_mix_b, ffn_w_in, ffn_w_out, ln_ffn_g, ln_ffn_b, ple_w_gate, ple_w_proj):
    S = x.shape[1]
    cos, sin = rope_tables(S)
    lb_sm = jax.nn.softmax(hgrn_lb_logits.astype(jnp.float32), axis=0)
    lb_all = jnp.cumsum(lb_sm, axis=0) - lb_sm[0:1]
    for i in range(DEPTH):
        j = i // N_MIXERS
        if i % N_MIXERS == 0:
            mix = windowed_gqa(x, att_w_qkv[j], att_sink[j], att_w_o[j], cos, sin)
        else:
            mix = hgrn2_bidirectional(x, hgrn_w_in[j], lb_all[i, 0], lb_all[i, 1], hgrn_norm_g[j], hgrn_w_o[j])
        x = layer_norm(ALPHA * x + mix, ln_mix_g[i], ln_mix_b[i])
        x = layer_norm(ALPHA * x + swiglu(x, ffn_w_in[i], ffn_w_out[i]), ln_ffn_g[i], ln_ffn_b[i])
        x = x + jax.nn.sigmoid(x @ ple_w_gate[i]) * (p[i] @ ple_w_proj[i])
    return x


import jax as _jax
import jax.numpy as _jnp

TWIN_FORMAT = 'train_step'
FWD_PARAMS = ['x', 'p', 'att_w_qkv', 'att_sink', 'att_w_o', 'hgrn_w_in', 'hgrn_lb_logits', 'hgrn_norm_g', 'hgrn_w_o', 'ln_mix_g', 'ln_mix_b', 'ffn_w_in', 'ffn_w_out', 'ln_ffn_g', 'ln_ffn_b', 'ple_w_gate', 'ple_w_proj']
TWIN_WEIGHTS = ['att_w_qkv', 'att_sink', 'att_w_o', 'hgrn_w_in', 'hgrn_lb_logits', 'hgrn_norm_g', 'hgrn_w_o', 'ln_mix_g', 'ln_mix_b', 'ffn_w_in', 'ffn_w_out', 'ln_ffn_g', 'ln_ffn_b', 'ple_w_gate', 'ple_w_proj']
TWIN_DIFF_INPUT = 'x'
TWIN_INPUTS = ['x', 'p', 'att_w_qkv', 'att_sink', 'att_w_o', 'hgrn_w_in', 'hgrn_lb_logits', 'hgrn_norm_g', 'hgrn_w_o', 'ln_mix_g', 'ln_mix_b', 'ffn_w_in', 'ffn_w_out', 'ln_ffn_g', 'ln_ffn_b', 'ple_w_gate', 'ple_w_proj', 'loss_target', 'm_att_w_qkv', 'm_att_sink', 'm_att_w_o', 'm_hgrn_w_in', 'm_hgrn_lb_logits', 'm_hgrn_norm_g', 'm_hgrn_w_o', 'm_ln_mix_g', 'm_ln_mix_b', 'm_ffn_w_in', 'm_ffn_w_out', 'm_ln_ffn_g', 'm_ln_ffn_b', 'm_ple_w_gate', 'm_ple_w_proj', 'v_att_w_qkv', 'v_att_sink', 'v_att_w_o', 'v_hgrn_w_in', 'v_hgrn_lb_logits', 'v_hgrn_norm_g', 'v_hgrn_w_o', 'v_ln_mix_g', 'v_ln_mix_b', 'v_ffn_w_in', 'v_ffn_w_out', 'v_ln_ffn_g', 'v_ln_ffn_b', 'v_ple_w_gate', 'v_ple_w_proj']
TWIN_OUTPUTS = ['loss', 'grad_x', 'grad_att_w_qkv', 'grad_att_sink', 'grad_att_w_o', 'grad_hgrn_w_in', 'grad_hgrn_lb_logits', 'grad_hgrn_norm_g', 'grad_hgrn_w_o', 'grad_ln_mix_g', 'grad_ln_mix_b', 'grad_ffn_w_in', 'grad_ffn_w_out', 'grad_ln_ffn_g', 'grad_ln_ffn_b', 'grad_ple_w_gate', 'grad_ple_w_proj', 'delta_att_w_qkv', 'delta_att_sink', 'delta_att_w_o', 'delta_hgrn_w_in', 'delta_hgrn_lb_logits', 'delta_hgrn_norm_g', 'delta_hgrn_w_o', 'delta_ln_mix_g', 'delta_ln_mix_b', 'delta_ffn_w_in', 'delta_ffn_w_out', 'delta_ln_ffn_g', 'delta_ln_ffn_b', 'delta_ple_w_gate', 'delta_ple_w_proj', 'new_m_att_w_qkv', 'new_m_att_sink', 'new_m_att_w_o', 'new_m_hgrn_w_in', 'new_m_hgrn_lb_logits', 'new_m_hgrn_norm_g', 'new_m_hgrn_w_o', 'new_m_ln_mix_g', 'new_m_ln_mix_b', 'new_m_ffn_w_in', 'new_m_ffn_w_out', 'new_m_ln_ffn_g', 'new_m_ln_ffn_b', 'new_m_ple_w_gate', 'new_m_ple_w_proj', 'new_v_att_w_qkv', 'new_v_att_sink', 'new_v_att_w_o', 'new_v_hgrn_w_in', 'new_v_hgrn_lb_logits', 'new_v_hgrn_norm_g', 'new_v_hgrn_w_o', 'new_v_ln_mix_g', 'new_v_ln_mix_b', 'new_v_ffn_w_in', 'new_v_ffn_w_out', 'new_v_ln_ffn_g', 'new_v_ln_ffn_b', 'new_v_ple_w_gate', 'new_v_ple_w_proj']
TWIN_LEAF_KINDS = {'loss': 'loss', 'grad_x': 'grad_x', 'grad_att_w_qkv': 'grad_w', 'grad_att_sink': 'grad_w', 'grad_att_w_o': 'grad_w', 'grad_hgrn_w_in': 'grad_w', 'grad_hgrn_lb_logits': 'grad_w', 'grad_hgrn_norm_g': 'grad_w', 'grad_hgrn_w_o': 'grad_w', 'grad_ln_mix_g': 'grad_w', 'grad_ln_mix_b': 'grad_w', 'grad_ffn_w_in': 'grad_w', 'grad_ffn_w_out': 'grad_w', 'grad_ln_ffn_g': 'grad_w', 'grad_ln_ffn_b': 'grad_w', 'grad_ple_w_gate': 'grad_w', 'grad_ple_w_proj': 'grad_w', 'delta_att_w_qkv': 'delta_w', 'delta_att_sink': 'delta_w', 'delta_att_w_o': 'delta_w', 'delta_hgrn_w_in': 'delta_w', 'delta_hgrn_lb_logits': 'delta_w', 'delta_hgrn_norm_g': 'delta_w', 'delta_hgrn_w_o': 'delta_w', 'delta_ln_mix_g': 'delta_w', 'delta_ln_mix_b': 'delta_w', 'delta_ffn_w_in': 'delta_w', 'delta_ffn_w_out': 'delta_w', 'delta_ln_ffn_g': 'delta_w', 'delta_ln_ffn_b': 'delta_w', 'delta_ple_w_gate': 'delta_w', 'delta_ple_w_proj': 'delta_w', 'new_m_att_w_qkv': 'new_m', 'new_m_att_sink': 'new_m', 'new_m_att_w_o': 'new_m', 'new_m_hgrn_w_in': 'new_m', 'new_m_hgrn_lb_logits': 'new_m', 'new_m_hgrn_norm_g': 'new_m', 'new_m_hgrn_w_o': 'new_m', 'new_m_ln_mix_g': 'new_m', 'new_m_ln_mix_b': 'new_m', 'new_m_ffn_w_in': 'new_m', 'new_m_ffn_w_out': 'new_m', 'new_m_ln_ffn_g': 'new_m', 'new_m_ln_ffn_b': 'new_m', 'new_m_ple_w_gate': 'new_m', 'new_m_ple_w_proj': 'new_m', 'new_v_att_w_qkv': 'new_v', 'new_v_att_sink': 'new_v', 'new_v_att_w_o': 'new_v', 'new_v_hgrn_w_in': 'new_v', 'new_v_hgrn_lb_logits': 'new_v', 'new_v_hgrn_norm_g': 'new_v', 'new_v_hgrn_w_o': 'new_v', 'new_v_ln_mix_g': 'new_v', 'new_v_ln_mix_b': 'new_v', 'new_v_ffn_w_in': 'new_v', 'new_v_ffn_w_out': 'new_v', 'new_v_ln_ffn_g': 'new_v', 'new_v_ln_ffn_b': 'new_v', 'new_v_ple_w_gate': 'new_v', 'new_v_ple_w_proj': 'new_v'}


def _forward(args):
    return _fwd_reference(*[args[k] for k in FWD_PARAMS])


def _output_shape():
    def fwd():
        inp = _fwd_setup_inputs(0)
        return _fwd_reference(*[inp[k] for k in FWD_PARAMS])
    out = _jax.eval_shape(fwd)
    return out.shape, out.dtype

N_MICROBATCH = 1
ADAM_LR = 0.001
ADAM_B1 = 0.9
ADAM_B2 = 0.999
ADAM_EPS = 1e-08
ADAM_WD = 0.01
ADAM_STEP = 10
PER_EXAMPLE_BATCH_AXIS = {'x': 0, 'p': 1, 'loss_target': 0}
SHARED_INPUTS = []
_WEIGHT_DTYPES = {'att_w_qkv': _jnp.float32, 'att_sink': _jnp.float32, 'att_w_o': _jnp.float32, 'hgrn_w_in': _jnp.float32, 'hgrn_lb_logits': _jnp.float32, 'hgrn_norm_g': _jnp.float32, 'hgrn_w_o': _jnp.float32, 'ln_mix_g': _jnp.float32, 'ln_mix_b': _jnp.float32, 'ffn_w_in': _jnp.float32, 'ffn_w_out': _jnp.float32, 'ln_ffn_g': _jnp.float32, 'ln_ffn_b': _jnp.float32, 'ple_w_gate': _jnp.float32, 'ple_w_proj': _jnp.float32}
MOMENT_SCALE = {'att_w_qkv': 1.708709e-02, 'att_sink': 8.514387e-04, 'att_w_o': 3.033871e-02, 'hgrn_w_in': 3.532687e-02, 'hgrn_lb_logits': 2.852819e-03, 'hgrn_norm_g': 1.642856e-01, 'hgrn_w_o': 1.302133e-01, 'ln_mix_g': 1.345082e+00, 'ln_mix_b': 1.647355e+00, 'ffn_w_in': 3.295048e-02, 'ffn_w_out': 1.280226e-01, 'ln_ffn_g': 6.421617e+01, 'ln_ffn_b': 2.012741e+00, 'ple_w_gate': 3.876494e-02, 'ple_w_proj': 3.059522e-01}


def _to_microbatches(a, axis):
    t = _jnp.moveaxis(a, axis, 0)
    t = t.reshape((N_MICROBATCH, t.shape[0] // N_MICROBATCH) + t.shape[1:])
    return _jnp.moveaxis(t, 1, axis + 1)


def setup_inputs(seed: int = 0) -> dict:
    inp = _fwd_setup_inputs(seed)
    key = _jax.random.fold_in(_jax.random.key(seed), 7919)
    shape, _ = _output_shape()
    out = dict(inp)
    out["loss_target"] = _jax.random.normal(_jax.random.fold_in(key, 0), shape, _jnp.float32)
    for i, name in enumerate(TWIN_WEIGHTS):
        w = inp[name].astype(_jnp.float32)
        if MOMENT_SCALE is None:
            s = _jnp.sqrt(_jnp.mean(_jnp.square(w)) + 1e-30)
        else:
            s = MOMENT_SCALE[name]
        km, kv = _jax.random.split(_jax.random.fold_in(key, i + 1))
        out[name] = w
        out["m_" + name] = s * _jax.random.normal(km, w.shape, _jnp.float32)
        out["v_" + name] = (s * s) * _jax.random.uniform(kv, w.shape, _jnp.float32, 0.5, 1.5)
    if N_MICROBATCH > 1:
        for name, axis in PER_EXAMPLE_BATCH_AXIS.items():
            out[name] = _to_microbatches(out[name], axis)
    return {'x': out['x'], 'p': out['p'], 'att_w_qkv': out['att_w_qkv'], 'att_sink': out['att_sink'], 'att_w_o': out['att_w_o'], 'hgrn_w_in': out['hgrn_w_in'], 'hgrn_lb_logits': out['hgrn_lb_logits'], 'hgrn_norm_g': out['hgrn_norm_g'], 'hgrn_w_o': out['hgrn_w_o'], 'ln_mix_g': out['ln_mix_g'], 'ln_mix_b': out['ln_mix_b'], 'ffn_w_in': out['ffn_w_in'], 'ffn_w_out': out['ffn_w_out'], 'ln_ffn_g': out['ln_ffn_g'], 'ln_ffn_b': out['ln_ffn_b'], 'ple_w_gate': out['ple_w_gate'], 'ple_w_proj': out['ple_w_proj'], 'loss_target': out['loss_target'], 'm_att_w_qkv': out['m_att_w_qkv'], 'm_att_sink': out['m_att_sink'], 'm_att_w_o': out['m_att_w_o'], 'm_hgrn_w_in': out['m_hgrn_w_in'], 'm_hgrn_lb_logits': out['m_hgrn_lb_logits'], 'm_hgrn_norm_g': out['m_hgrn_norm_g'], 'm_hgrn_w_o': out['m_hgrn_w_o'], 'm_ln_mix_g': out['m_ln_mix_g'], 'm_ln_mix_b': out['m_ln_mix_b'], 'm_ffn_w_in': out['m_ffn_w_in'], 'm_ffn_w_out': out['m_ffn_w_out'], 'm_ln_ffn_g': out['m_ln_ffn_g'], 'm_ln_ffn_b': out['m_ln_ffn_b'], 'm_ple_w_gate': out['m_ple_w_gate'], 'm_ple_w_proj': out['m_ple_w_proj'], 'v_att_w_qkv': out['v_att_w_qkv'], 'v_att_sink': out['v_att_sink'], 'v_att_w_o': out['v_att_w_o'], 'v_hgrn_w_in': out['v_hgrn_w_in'], 'v_hgrn_lb_logits': out['v_hgrn_lb_logits'], 'v_hgrn_norm_g': out['v_hgrn_norm_g'], 'v_hgrn_w_o': out['v_hgrn_w_o'], 'v_ln_mix_g': out['v_ln_mix_g'], 'v_ln_mix_b': out['v_ln_mix_b'], 'v_ffn_w_in': out['v_ffn_w_in'], 'v_ffn_w_out': out['v_ffn_w_out'], 'v_ln_ffn_g': out['v_ln_ffn_g'], 'v_ln_ffn_b': out['v_ln_ffn_b'], 'v_ple_w_gate': out['v_ple_w_gate'], 'v_ple_w_proj': out['v_ple_w_proj']}


def _loss(weights, diff, rest, loss_target):
    with _jax.named_scope("forward"):
        args = {**rest, TWIN_DIFF_INPUT: diff, **{k: w.astype(_WEIGHT_DTYPES[k]) for k, w in weights.items()}}
        y = _forward(args)
    with _jax.named_scope("loss_head"):
        err = _jnp.square(y.astype(_jnp.float32) - loss_target)
        return 0.5 * _jnp.sum(_jnp.mean(err, axis=-1)) if err.ndim else 0.5 * err


def _adamw(w, g, m, v):
    m = ADAM_B1 * m + (1.0 - ADAM_B1) * g
    v = ADAM_B2 * v + (1.0 - ADAM_B2) * _jnp.square(g)
    m_hat = m / (1.0 - ADAM_B1 ** ADAM_STEP)
    v_hat = v / (1.0 - ADAM_B2 ** ADAM_STEP)
    delta = -ADAM_LR * (m_hat / (_jnp.sqrt(v_hat) + ADAM_EPS) + ADAM_WD * w)
    return delta, m, v


def reference(x, p, att_w_qkv, att_sink, att_w_o, hgrn_w_in, hgrn_lb_logits, hgrn_norm_g, hgrn_w_o, ln_mix_g, ln_mix_b, ffn_w_in, ffn_w_out, ln_ffn_g, ln_ffn_b, ple_w_gate, ple_w_proj, loss_target, m_att_w_qkv, m_att_sink, m_att_w_o, m_hgrn_w_in, m_hgrn_lb_logits, m_hgrn_norm_g, m_hgrn_w_o, m_ln_mix_g, m_ln_mix_b, m_ffn_w_in, m_ffn_w_out, m_ln_ffn_g, m_ln_ffn_b, m_ple_w_gate, m_ple_w_proj, v_att_w_qkv, v_att_sink, v_att_w_o, v_hgrn_w_in, v_hgrn_lb_logits, v_hgrn_norm_g, v_hgrn_w_o, v_ln_mix_g, v_ln_mix_b, v_ffn_w_in, v_ffn_w_out, v_ln_ffn_g, v_ln_ffn_b, v_ple_w_gate, v_ple_w_proj):
    given = dict(x=x, p=p, att_w_qkv=att_w_qkv, att_sink=att_sink, att_w_o=att_w_o, hgrn_w_in=hgrn_w_in, hgrn_lb_logits=hgrn_lb_logits, hgrn_norm_g=hgrn_norm_g, hgrn_w_o=hgrn_w_o, ln_mix_g=ln_mix_g, ln_mix_b=ln_mix_b, ffn_w_in=ffn_w_in, ffn_w_out=ffn_w_out, ln_ffn_g=ln_ffn_g, ln_ffn_b=ln_ffn_b, ple_w_gate=ple_w_gate, ple_w_proj=ple_w_proj, loss_target=loss_target, m_att_w_qkv=m_att_w_qkv, m_att_sink=m_att_sink, m_att_w_o=m_att_w_o, m_hgrn_w_in=m_hgrn_w_in, m_hgrn_lb_logits=m_hgrn_lb_logits, m_hgrn_norm_g=m_hgrn_norm_g, m_hgrn_w_o=m_hgrn_w_o, m_ln_mix_g=m_ln_mix_g, m_ln_mix_b=m_ln_mix_b, m_ffn_w_in=m_ffn_w_in, m_ffn_w_out=m_ffn_w_out, m_ln_ffn_g=m_ln_ffn_g, m_ln_ffn_b=m_ln_ffn_b, m_ple_w_gate=m_ple_w_gate, m_ple_w_proj=m_ple_w_proj, v_att_w_qkv=v_att_w_qkv, v_att_sink=v_att_sink, v_att_w_o=v_att_w_o, v_hgrn_w_in=v_hgrn_w_in, v_hgrn_lb_logits=v_hgrn_lb_logits, v_hgrn_norm_g=v_hgrn_norm_g, v_hgrn_w_o=v_hgrn_w_o, v_ln_mix_g=v_ln_mix_g, v_ln_mix_b=v_ln_mix_b, v_ffn_w_in=v_ffn_w_in, v_ffn_w_out=v_ffn_w_out, v_ln_ffn_g=v_ln_ffn_g, v_ln_ffn_b=v_ln_ffn_b, v_ple_w_gate=v_ple_w_gate, v_ple_w_proj=v_ple_w_proj)
    weights = {n: given[n] for n in TWIN_WEIGHTS}
    shared = {n: given[n] for n in SHARED_INPUTS}
    per_example = {n: given[n] for n in ['x', 'p']}
    grad_fn = _jax.value_and_grad(_loss, argnums=(0, 1))

    def one_microbatch(ex, loss_target):
        ex = dict(ex)
        diff = ex.pop(TWIN_DIFF_INPUT)
        return grad_fn(weights, diff, {**shared, **ex}, loss_target)

    if N_MICROBATCH == 1:
        loss, (grad_w, grad_x) = one_microbatch(per_example, given["loss_target"])
    else:
        def body(carry, xs):
            loss_sum, grad_sum = carry
            l_k, (gw_k, gx_k) = one_microbatch(xs[0], xs[1])
            with _jax.named_scope("update"):
                return (loss_sum + l_k, _jax.tree.map(_jnp.add, grad_sum, gw_k)), gx_k

        init = (_jnp.zeros((), _jnp.float32), _jax.tree.map(_jnp.zeros_like, weights))
        (loss, grad_w), grad_x = _jax.lax.scan(body, init, (per_example, given["loss_target"]))
    with _jax.named_scope("update"):
        delta_w, new_m, new_v = {}, {}, {}
        for n in TWIN_WEIGHTS:
            delta_w[n], new_m[n], new_v[n] = _adamw(weights[n], grad_w[n], given["m_" + n], given["v_" + n])
    return (loss, grad_x, *[grad_w[n] for n in TWIN_WEIGHTS], *[delta_w[n] for n in TWIN_WEIGHTS],
            *[new_m[n] for n in TWIN_WEIGHTS], *[new_v[n] for n in TWIN_WEIGHTS])
```

```python
import functools

import jax
import jax.numpy as jnp
from jax import lax
from jax.experimental import pallas as pl
from jax.experimental.pallas import tpu as pltpu

F32 = jnp.float32
BF16 = jnp.bfloat16

D_MODEL = 1024
DEPTH = 4
HEAD_DIM = 64
N_Q_HEADS = 16
N_KV_HEADS = 4
GROUP = 4
KV_DIM = 256
ATT_BLOCK = 128
ROPE_DIM = 16
ROPE_THETA = 500000.0
HG_HEADS = 8
HG_DIM = 128
HG_CHUNK = 64
HG_SUB = 16
D_FF = 2816
FF_TILE = 1408
PLE_DIM = 256
ALPHA = (2 * DEPTH) ** 0.25
LN_EPS = 1e-5
ADAM_LR, ADAM_B1, ADAM_B2, ADAM_EPS, ADAM_WD, ADAM_STEP = 0.001, 0.9, 0.999, 1e-08, 0.01, 10

N_DEV = 8
LANES = 128
VMEM_LIMIT = 52 * 1024 * 1024
NEG = -1e30
MESH = pl.DeviceIdType.MESH
AXES = ("x", "y", "c")

BIG = (("att_w_qkv", 2), ("att_w_o", 1), ("hgrn_w_in", 2), ("hgrn_w_o", 1), ("ffn_w_in", 2), ("ffn_w_out", 1),
       ("ple_w_gate", 1), ("ple_w_proj", 2))


def _params(sem=None, vmem=VMEM_LIMIT):
    return pltpu.CompilerParams(dimension_semantics=sem, vmem_limit_bytes=vmem)


def _sigmoid(x):
    return jax.nn.sigmoid(x)


def _exchange(src, *, same_src, name):
    blk = src.shape if same_src else src.shape[1:]

    def body(src_ref, out_ref, send_sems, recv_sems, local_sem):
        x, y, c = lax.axis_index("x"), lax.axis_index("y"), lax.axis_index("c")
        me = 4 * x + 2 * y + c

        def mine(j):
            return src_ref if same_src else src_ref.at[j]

        local = pltpu.make_async_copy(mine(me), out_ref.at[me], local_sem)
        local.start()
        copies = []
        for k in range(1, N_DEV):
            px, py, pc = x ^ (k >> 2), y ^ ((k >> 1) & 1), c ^ (k & 1)
            peer = 4 * px + 2 * py + pc
            copies.append((peer, pltpu.make_async_remote_copy(
                src_ref=mine(peer), dst_ref=out_ref.at[me], send_sem=send_sems.at[k], recv_sem=recv_sems.at[k],
                device_id=(px, py, pc), device_id_type=MESH)))
        for _, cp in copies:
            cp.start()
        for k, (peer, cp) in enumerate(copies, start=1):
            cp.wait_send()
            pltpu.make_async_remote_copy(
                src_ref=mine(peer), dst_ref=out_ref.at[peer], send_sem=send_sems.at[k], recv_sem=recv_sems.at[k],
                device_id=(x, y, c), device_id_type=MESH).wait_recv()
        local.wait()

    return pl.pallas_call(
        body, name=name,
        out_shape=jax.ShapeDtypeStruct((N_DEV,) + tuple(blk), src.dtype),
        in_specs=[pl.BlockSpec(memory_space=pltpu.HBM)],
        out_specs=pl.BlockSpec(memory_space=pltpu.HBM),
        scratch_shapes=[pltpu.SemaphoreType.DMA((N_DEV,)), pltpu.SemaphoreType.DMA((N_DEV,)), pltpu.SemaphoreType.DMA],
    )(src)


def _mm(a, b, *, n, tm, tn, tk, ta=False, tb=False, out_dtype=F32, add=None, add_scale=1.0, bk_off=0, name):
    m, kdim = (a.shape[1], a.shape[0]) if ta else a.shape
    nk = kdim // tk
    dims = (((0 if ta else 1,), (1 if tb else 0,)), ((), ()))

    def body(*refs):
        if add is None:
            a_ref, b_ref, o_ref = refs[:3]
            add_ref = None
        else:
            a_ref, b_ref, add_ref, o_ref = refs[:4]
        acc_ref = refs[-1] if nk > 1 else None
        part = lax.dot_general(a_ref[...].astype(BF16), b_ref[...].astype(BF16), dims, preferred_element_type=F32)

        def finish(r):
            if add_ref is not None:
                r = r + add_scale * add_ref[...]
            o_ref[...] = r.astype(out_dtype)

        if nk == 1:
            finish(part)
        else:
            k = pl.program_id(2)

            @pl.when(k == 0)
            def _():
                acc_ref[...] = part

            @pl.when(k > 0)
            def _():
                acc_ref[...] += part

            @pl.when(k == nk - 1)
            def _():
                finish(acc_ref[...])

    a_spec = pl.BlockSpec((tk, tm), lambda i, j, k: (k, i)) if ta else pl.BlockSpec((tm, tk), lambda i, j, k: (i, k))
    b_spec = (pl.BlockSpec((tn, tk), lambda i, j, k: (j, k + bk_off)) if tb
              else pl.BlockSpec((tk, tn), lambda i, j, k: (k + bk_off, j)))
    in_specs, args = [a_spec, b_spec], [a, b]
    if add is not None:
        in_specs.append(pl.BlockSpec((tm, tn), lambda i, j, k: (i, j)))
        args.append(add)
    return pl.pallas_call(
        body, name=name, grid=(m // tm, n // tn, nk),
        out_shape=jax.ShapeDtypeStruct((m, n), out_dtype),
        in_specs=in_specs, out_specs=pl.BlockSpec((tm, tn), lambda i, j, k: (i, j)),
        scratch_shapes=[pltpu.VMEM((tm, tn), F32)] if nk > 1 else [],
        compiler_params=_params(("parallel", "parallel", "arbitrary")),
    )(*args)


def _layer_norm_rows(y, g, b):
    mu = jnp.mean(y, axis=-1, keepdims=True)
    yc = y - mu
    var = jnp.mean(yc * yc, axis=-1, keepdims=True)
    return yc * lax.rsqrt(var + LN_EPS) * g + b


def _proj_ln(a, w, res, g, b, *, tm, name):
    s, kdim = a.shape

    def body(a_ref, w_ref, res_ref, g_ref, b_ref, pre_ref, o_ref, obf_ref):
        h = jnp.dot(a_ref[...], w_ref[...], preferred_element_type=F32)
        pre = ALPHA * res_ref[...] + h
        out = _layer_norm_rows(pre, g_ref[...], b_ref[...])
        pre_ref[...] = pre
        o_ref[...] = out
        obf_ref[...] = out.astype(BF16)

    row = lambda i: (i, 0)
    fix = lambda i: (0, 0)
    return pl.pallas_call(
        body, name=name, grid=(s // tm,),
        out_shape=(jax.ShapeDtypeStruct((s, D_MODEL), F32), jax.ShapeDtypeStruct((s, D_MODEL), F32),
                   jax.ShapeDtypeStruct((s, D_MODEL), BF16)),
        in_specs=[pl.BlockSpec((tm, kdim), row), pl.BlockSpec((kdim, D_MODEL), fix), pl.BlockSpec((tm, D_MODEL), row),
                  pl.BlockSpec((1, D_MODEL), fix), pl.BlockSpec((1, D_MODEL), fix)],
        out_specs=(pl.BlockSpec((tm, D_MODEL), row),) * 3,
        compiler_params=_params(("parallel",)),
    )(a, w, res, g, b)


def _ln_bwd(dout, pre, g, *, tm, name):
    s = dout.shape[0]

    def body(do_ref, pre_ref, g_ref, dy_ref, dybf_ref, dg_ref, db_ref):
        do = do_ref[...]
        y = pre_ref[...]
        mu = jnp.mean(y, axis=-1, keepdims=True)
        yc = y - mu
        var = jnp.mean(yc * yc, axis=-1, keepdims=True)
        rstd = lax.rsqrt(var + LN_EPS)
        xhat = yc * rstd
        dxhat = do * g_ref[...]
        dy = rstd * (dxhat - jnp.mean(dxhat, axis=-1, keepdims=True)
                     - xhat * jnp.mean(dxhat * xhat, axis=-1, keepdims=True))
        dy_ref[...] = dy
        dybf_ref[...] = dy.astype(BF16)
        pg = jnp.sum(do * xhat, axis=0, keepdims=True)
        pb = jnp.sum(do, axis=0, keepdims=True)

        @pl.when(pl.program_id(0) == 0)
        def _():
            dg_ref[...] = pg
            db_ref[...] = pb

        @pl.when(pl.program_id(0) > 0)
        def _():
            dg_ref[...] += pg
            db_ref[...] += pb

    row = lambda i: (i, 0)
    fix = lambda i: (0, 0)
    return pl.pallas_call(
        body, name=name, grid=(s // tm,),
        out_shape=(jax.ShapeDtypeStruct((s, D_MODEL), F32), jax.ShapeDtypeStruct((s, D_MODEL), BF16),
                   jax.ShapeDtypeStruct((1, D_MODEL), F32), jax.ShapeDtypeStruct((1, D_MODEL), F32)),
        in_specs=[pl.BlockSpec((tm, D_MODEL), row), pl.BlockSpec((tm, D_MODEL), row), pl.BlockSpec((1, D_MODEL), fix)],
        out_specs=(pl.BlockSpec((tm, D_MODEL), row), pl.BlockSpec((tm, D_MODEL), row),
                   pl.BlockSpec((1, D_MODEL), fix), pl.BlockSpec((1, D_MODEL), fix)),
        compiler_params=_params(("arbitrary",)),
    )(dout, pre, g)


def _ffn_in(xbf, w, *, tm, tn, name):
    s = xbf.shape[0]
    nj = D_FF // tn

    def body(x_ref, wg_ref, wu_ref, g_ref, u_ref, act_ref):
        xv = x_ref[...]
        gg = jnp.dot(xv, wg_ref[...], preferred_element_type=F32)
        uu = jnp.dot(xv, wu_ref[...], preferred_element_type=F32)
        g_ref[...] = gg.astype(BF16)
        u_ref[...] = uu.astype(BF16)
        act_ref[...] = (gg * _sigmoid(gg) * uu).astype(BF16)

    out = jax.ShapeDtypeStruct((s, D_FF), BF16)
    tile = pl.BlockSpec((tm, tn), lambda j, i: (i, j))
    return pl.pallas_call(
        body, name=name, grid=(nj, s // tm),
        out_shape=(out, out, out),
        in_specs=[pl.BlockSpec((tm, D_MODEL), lambda j, i: (i, 0)), pl.BlockSpec((D_MODEL, tn), lambda j, i: (0, j)),
                  pl.BlockSpec((D_MODEL, tn), lambda j, i: (0, j + nj))],
        out_specs=(tile, tile, tile),
        compiler_params=_params(("parallel", "parallel")),
    )(xbf, w, w)


def _ffn_bwd_act(dybf, w_out, g, u, *, tm, tn, name):
    s = dybf.shape[0]

    def body(dy_ref, w_ref, g_ref, u_ref, dg_ref, du_ref):
        dact = lax.dot_general(dy_ref[...], w_ref[...], (((1,), (1,)), ((), ())), preferred_element_type=F32)
        gg = g_ref[...].astype(F32)
        uu = u_ref[...].astype(F32)
        sg = _sigmoid(gg)
        dg_ref[...] = (dact * uu * sg * (1.0 + gg * (1.0 - sg))).astype(BF16)
        du_ref[...] = (dact * gg * sg).astype(BF16)

    out = jax.ShapeDtypeStruct((s, D_FF), BF16)
    tile = pl.BlockSpec((tm, tn), lambda j, i: (i, j))
    return pl.pallas_call(
        body, name=name, grid=(D_FF // tn, s // tm),
        out_shape=(out, out),
        in_specs=[pl.BlockSpec((tm, D_MODEL), lambda j, i: (i, 0)), pl.BlockSpec((tn, D_MODEL), lambda j, i: (j, 0)),
                  tile, tile],
        out_specs=(tile, tile),
        compiler_params=_params(("parallel", "parallel")),
    )(dybf, w_out, g, u)


def _ple_fwd(x2, x2bf, p, w_gate, w_proj, *, tm, name):
    s = x2.shape[0]

    def body(x_ref, xbf_ref, p_ref, wg_ref, wp_ref, o_ref, obf_ref):
        a = jnp.dot(xbf_ref[...], wg_ref[...], preferred_element_type=F32)
        pp = jnp.dot(p_ref[...].astype(BF16), wp_ref[...], preferred_element_type=F32)
        out = x_ref[...] + _sigmoid(a) * pp
        o_ref[...] = out
        obf_ref[...] = out.astype(BF16)

    row = lambda i: (i, 0)
    fix = lambda i: (0, 0)
    return pl.pallas_call(
        body, name=name, grid=(s // tm,),
        out_shape=(jax.ShapeDtypeStruct((s, D_MODEL), F32), jax.ShapeDtypeStruct((s, D_MODEL), BF16)),
        in_specs=[pl.BlockSpec((tm, D_MODEL), row), pl.BlockSpec((tm, D_MODEL), row), pl.BlockSpec((tm, PLE_DIM), row),
                  pl.BlockSpec((D_MODEL, D_MODEL), fix), pl.BlockSpec((PLE_DIM, D_MODEL), fix)],
        out_specs=(pl.BlockSpec((tm, D_MODEL), row), pl.BlockSpec((tm, D_MODEL), row)),
        compiler_params=_params(("parallel",)),
    )(x2, x2bf, p, w_gate, w_proj)


def _ple_bwd(dx3, x2bf, p, w_gate, w_proj, *, tm, name):
    s = dx3.shape[0]

    def body(d_ref, xbf_ref, p_ref, wg_ref, wp_ref, dx_ref, da_ref, dpp_ref):
        d = d_ref[...]
        a = jnp.dot(xbf_ref[...], wg_ref[...], preferred_element_type=F32)
        pp = jnp.dot(p_ref[...].astype(BF16), wp_ref[...], preferred_element_type=F32)
        sg = _sigmoid(a)
        da = (d * pp * sg * (1.0 - sg)).astype(BF16)
        da_ref[...] = da
        dpp_ref[...] = (d * sg).astype(BF16)
        dx_ref[...] = d + lax.dot_general(da, wg_ref[...], (((1,), (1,)), ((), ())), preferred_element_type=F32)

    row = lambda i: (i, 0)
    fix = lambda i: (0, 0)
    return pl.pallas_call(
        body, name=name, grid=(s // tm,),
        out_shape=(jax.ShapeDtypeStruct((s, D_MODEL), F32), jax.ShapeDtypeStruct((s, D_MODEL), BF16),
                   jax.ShapeDtypeStruct((s, D_MODEL), BF16)),
        in_specs=[pl.BlockSpec((tm, D_MODEL), row), pl.BlockSpec((tm, D_MODEL), row), pl.BlockSpec((tm, PLE_DIM), row),
                  pl.BlockSpec((D_MODEL, D_MODEL), fix), pl.BlockSpec((PLE_DIM, D_MODEL), fix)],
        out_specs=(pl.BlockSpec((tm, D_MODEL), row),) * 3,
        compiler_params=_params(("parallel",)),
    )(dx3, x2bf, p, w_gate, w_proj)


def _loss_head(y, target, *, tm, name):
    s = y.shape[0]

    def body(y_ref, t_ref, dy_ref, loss_ref, acc_ref):
        err = y_ref[...] - t_ref[...]
        dy_ref[...] = err * (1.0 / D_MODEL)
        part = jnp.sum(err * err, axis=0, keepdims=True)

        @pl.when(pl.program_id(0) == 0)
        def _():
            acc_ref[...] = part

        @pl.when(pl.program_id(0) > 0)
        def _():
            acc_ref[...] += part

        @pl.when(pl.program_id(0) == pl.num_programs(0) - 1)
        def _():
            tot = jnp.sum(acc_ref[...], axis=1, keepdims=True) * (0.5 / D_MODEL)
            loss_ref[...] = jnp.broadcast_to(tot, (8, LANES))

    row = lambda i: (i, 0)
    return pl.pallas_call(
        body, name=name, grid=(s // tm,),
        out_shape=(jax.ShapeDtypeStruct((s, D_MODEL), F32), jax.ShapeDtypeStruct((8, LANES), F32)),
        in_specs=[pl.BlockSpec((tm, D_MODEL), row), pl.BlockSpec((tm, D_MODEL), row)],
        out_specs=(pl.BlockSpec((tm, D_MODEL), row), pl.BlockSpec((8, LANES), lambda i: (0, 0))),
        scratch_shapes=[pltpu.VMEM((1, D_MODEL), F32)],
        compiler_params=_params(("arbitrary",)),
    )(y, target)


def _rope_tables(s):
    inv = ROPE_THETA ** (-jnp.arange(0, ROPE_DIM, 2, dtype=F32) / ROPE_DIM)
    ang = jnp.arange(s, dtype=F32)[:, None] * inv[None, :]
    cos, sin = jnp.cos(ang), jnp.sin(ang)
    ones = jnp.ones((s, HEAD_DIM - ROPE_DIM), F32)
    c_head = jnp.concatenate([cos, cos, ones], axis=1)
    s_head = jnp.concatenate([-sin, sin, 0.0 * ones], axis=1)
    return jnp.concatenate([c_head, c_head], axis=1), jnp.concatenate([s_head, s_head], axis=1)


def _rope(v, cos, sin):
    n = v.shape[1] // LANES
    width = v.shape[1]
    cos_w = jnp.tile(cos, (1, n)) if n > 1 else cos
    sin_w = jnp.tile(sin, (1, n)) if n > 1 else sin
    dim = lax.broadcasted_iota(jnp.int32, (1, width), 1) % HEAD_DIM
    partner = jnp.where(dim < ROPE_DIM // 2, pltpu.roll(v, width - ROPE_DIM // 2, 1), pltpu.roll(v, ROPE_DIM // 2, 1))
    return v * cos_w + partner * sin_w


def _unrope(dv, cos, sin):
    n = dv.shape[1] // LANES
    width = dv.shape[1]
    cos_w = jnp.tile(cos, (1, n)) if n > 1 else cos
    sin_w = jnp.tile(sin, (1, n)) if n > 1 else sin
    t = dv * sin_w
    dim = lax.broadcasted_iota(jnp.int32, (1, width), 1) % HEAD_DIM
    partner = jnp.where(dim < ROPE_DIM // 2, pltpu.roll(t, width - ROPE_DIM // 2, 1),
                        jnp.where(dim < ROPE_DIM, pltpu.roll(t, ROPE_DIM // 2, 1), 0.0))
    return dv * cos_w + partner


def _att_mask(i, nb):
    rows = GROUP * ATT_BLOCK
    r = lax.broadcasted_iota(jnp.int32, (rows, 3 * ATT_BLOCK), 0) % ATT_BLOCK
    cidx = lax.broadcasted_iota(jnp.int32, (rows, 3 * ATT_BLOCK), 1)
    rel = r + ATT_BLOCK - cidx
    ok = (rel <= ATT_BLOCK) & (rel >= -ATT_BLOCK)
    ok = ok & ((cidx >= ATT_BLOCK) | (i > 0)) & ((cidx < 2 * ATT_BLOCK) | (i < nb - 1))
    return ok


def _half_mask(half):
    lane = lax.broadcasted_iota(jnp.int32, (1, LANES), 1)
    return (lane // HEAD_DIM) == half


def _stack_q(q, h):
    parts = []
    for gq in range(GROUP):
        n = GROUP * h + gq
        grp = q[:, LANES * (n // 2):LANES * (n // 2 + 1)]
        grp = jnp.where(_half_mask(n % 2), grp, 0.0)
        if n % 2 != h % 2:
            grp = pltpu.roll(grp, HEAD_DIM, 1)
        parts.append(grp)
    return jnp.concatenate(parts, axis=0)


def _unstack_q(stacked, h, acc):
    for gq in range(GROUP):
        n = GROUP * h + gq
        grp = stacked[ATT_BLOCK * gq:ATT_BLOCK * (gq + 1), :]
        grp = jnp.where(_half_mask(h % 2), grp, 0.0)
        if n % 2 != h % 2:
            grp = pltpu.roll(grp, HEAD_DIM, 1)
        acc[n // 2] = grp if acc[n // 2] is None else acc[n // 2] + grp
    return acc


def _sink_rows(sink_ref, h):
    rows = GROUP * ATT_BLOCK
    grp = lax.broadcasted_iota(jnp.int32, (rows, 1), 0) // ATT_BLOCK
    out = jnp.zeros((rows, 1), F32)
    for gq in range(GROUP):
        out = jnp.where(grp == gq, sink_ref[GROUP * h + gq], out)
    return out


def _att_probs(qs, kh, sink, valid):
    s = lax.dot_general(qs, kh, (((1,), (1,)), ((), ())), preferred_element_type=F32)
    s = jnp.where(valid, s, NEG)
    m = jnp.maximum(jnp.max(s, axis=-1, keepdims=True), sink)
    p = jnp.exp(s - m)
    es = jnp.exp(sink - m)
    den = jnp.sum(p, axis=-1, keepdims=True) + es
    inv = 1.0 / den
    return p * inv, es * inv


def _att_specs(nb):
    prev = lambda i: (jnp.maximum(i - 1, 0), 0)
    cur = lambda i: (i, 0)
    nxt = lambda i: (jnp.minimum(i + 1, nb - 1), 0)
    kv = lambda f: (lambda i: (f(i)[0], 2))
    tab = [pl.BlockSpec((ATT_BLOCK, LANES), f) for f in (cur, prev, cur, nxt)]
    z = [pl.BlockSpec((ATT_BLOCK, D_MODEL), cur)] + [pl.BlockSpec((ATT_BLOCK, 2 * KV_DIM), kv(f)) for f in (prev, cur, nxt)]
    return z, tab


def _att_load(zq_ref, kp_ref, kc_ref, kn_ref, cq_ref, sq_ref, cp_ref, sp_ref, cc_ref, sc_ref, cn_ref, sn_ref):
    q = (_rope(zq_ref[...], cq_ref[...], sq_ref[...]) * (HEAD_DIM ** -0.5))
    ks, vs = [], []
    for ref, c_ref, s_ref in ((kp_ref, cp_ref, sp_ref), (kc_ref, cc_ref, sc_ref), (kn_ref, cn_ref, sn_ref)):
        kvb = ref[...]
        ks.append(_rope(kvb[:, :KV_DIM], c_ref[...], s_ref[...]))
        vs.append(kvb[:, KV_DIM:])
    return q, jnp.concatenate(ks, axis=0).astype(BF16), jnp.concatenate(vs, axis=0).astype(BF16)


def _att_fwd(z, sink, cos, sin, *, name):
    s = z.shape[0]
    nb = s // ATT_BLOCK

    def body(zq_ref, kp_ref, kc_ref, kn_ref, cq_ref, cp_ref, cc_ref, cn_ref, sq_ref, sp_ref, sc_ref, sn_ref, sink_ref,
             o_ref):
        i = pl.program_id(0)
        q, k, v = _att_load(zq_ref, kp_ref, kc_ref, kn_ref, cq_ref, sq_ref, cp_ref, sp_ref, cc_ref, sc_ref, cn_ref, sn_ref)
        valid = _att_mask(i, nb)
        acc = [None] * (N_Q_HEADS // 2)
        for h in range(N_KV_HEADS):
            lanes = slice(LANES * (h // 2), LANES * (h // 2 + 1))
            qs = _stack_q(q, h).astype(BF16)
            prob, _ = _att_probs(qs, k[:, lanes], _sink_rows(sink_ref, h), valid)
            oh = jnp.dot(prob.astype(BF16), v[:, lanes], preferred_element_type=F32)
            acc = _unstack_q(oh, h, acc)
        o_ref[...] = jnp.concatenate(acc, axis=1).astype(BF16)

    zspecs, tab = _att_specs(nb)
    return pl.pallas_call(
        body, name=name, grid=(nb,),
        out_shape=jax.ShapeDtypeStruct((s, D_MODEL), BF16),
        in_specs=zspecs + tab + tab + [pl.BlockSpec(memory_space=pltpu.SMEM)],
        out_specs=pl.BlockSpec((ATT_BLOCK, D_MODEL), lambda i: (i, 0)),
        compiler_params=_params(("parallel",)),
    )(z, z, z, z, cos, cos, cos, cos, sin, sin, sin, sin, sink)


def _att_bwd(z, do, sink, cos, sin, *, name):
    s = z.shape[0]
    nb = s // ATT_BLOCK

    def body(zq_ref, kp_ref, kc_ref, kn_ref, cq_ref, cp_ref, cc_ref, cn_ref, sq_ref, sp_ref, sc_ref, sn_ref, sink_ref,
             do_ref, dq_ref, part_ref, dsink_ref):
        i = pl.program_id(0)
        q, k, v = _att_load(zq_ref, kp_ref, kc_ref, kn_ref, cq_ref, sq_ref, cp_ref, sp_ref, cc_ref, sc_ref, cn_ref, sn_ref)
        valid = _att_mask(i, nb)
        dout = do_ref[...].astype(F32)
        dq_acc = [None] * (N_Q_HEADS // 2)
        dk_acc = [None] * 2
        dv_acc = [None] * 2
        ds_rows = []
        for h in range(N_KV_HEADS):
            grp = h // 2
            lanes = slice(LANES * grp, LANES * (grp + 1))
            qs = _stack_q(q, h).astype(BF16)
            dos = _stack_q(dout, h).astype(BF16)
            prob, psink = _att_probs(qs, k[:, lanes], _sink_rows(sink_ref, h), valid)
            dprob = lax.dot_general(dos, v[:, lanes], (((1,), (1,)), ((), ())), preferred_element_type=F32)
            delta = jnp.sum(prob * dprob, axis=-1, keepdims=True)
            dsc = (prob * (dprob - delta)).astype(BF16)
            ds_rows.append(-psink * delta)
            dqs = jnp.dot(dsc, k[:, lanes], preferred_element_type=F32)
            dq_acc = _unstack_q(dqs, h, dq_acc)
            dkh = lax.dot_general(dsc, qs, (((0,), (0,)), ((), ())), preferred_element_type=F32)
            dvh = lax.dot_general(prob.astype(BF16), dos, (((0,), (0,)), ((), ())), preferred_element_type=F32)
            dk_acc[grp] = dkh if dk_acc[grp] is None else dk_acc[grp] + dkh
            dv_acc[grp] = dvh if dv_acc[grp] is None else dv_acc[grp] + dvh
        dq = jnp.concatenate(dq_acc, axis=1) * (HEAD_DIM ** -0.5)
        dq_ref[...] = _unrope(dq, cq_ref[...], sq_ref[...]).astype(BF16)
        part = jnp.concatenate(dk_acc + dv_acc, axis=1)
        for wdw in range(3):
            part_ref[wdw] = part[ATT_BLOCK * wdw:ATT_BLOCK * (wdw + 1), :]
        rows = []
        for h in range(N_KV_HEADS):
            for gq in range(GROUP):
                tot = jnp.sum(ds_rows[h][ATT_BLOCK * gq:ATT_BLOCK * (gq + 1), :], axis=0, keepdims=True)
                rows.append(jnp.broadcast_to(tot, (1, LANES)))
        dsink = jnp.concatenate(rows, axis=0)

        @pl.when(i == 0)
        def _():
            dsink_ref[...] = dsink

        @pl.when(i > 0)
        def _():
            dsink_ref[...] += dsink

    zspecs, tab = _att_specs(nb)
    return pl.pallas_call(
        body, name=name, grid=(nb,),
        out_shape=(jax.ShapeDtypeStruct((s, D_MODEL), BF16), jax.ShapeDtypeStruct((nb, 3, ATT_BLOCK, 2 * KV_DIM), F32),
                   jax.ShapeDtypeStruct((N_Q_HEADS, LANES), F32)),
        in_specs=zspecs + tab + tab + [pl.BlockSpec(memory_space=pltpu.SMEM), pl.BlockSpec((ATT_BLOCK, D_MODEL), lambda i: (i, 0))],
        out_specs=(pl.BlockSpec((ATT_BLOCK, D_MODEL), lambda i: (i, 0)),
                   pl.BlockSpec((None, 3, ATT_BLOCK, 2 * KV_DIM), lambda i: (i, 0, 0, 0)),
                   pl.BlockSpec((N_Q_HEADS, LANES), lambda i: (0, 0))),
        compiler_params=_params(("arbitrary",)),
    )(z, z, z, z, cos, cos, cos, cos, sin, sin, sin, sin, sink, do)


def _att_bwd_kv(part, cos, sin, *, name):
    nb = part.shape[0]

    def body(pn_ref, pc_ref, pp_ref, c_ref, s_ref, o_ref):
        j = pl.program_id(0)
        tot = pc_ref[...]
        tot = tot + jnp.where(j < nb - 1, pn_ref[...], 0.0)
        tot = tot + jnp.where(j > 0, pp_ref[...], 0.0)
        dk = _unrope(tot[:, :KV_DIM], c_ref[...], s_ref[...])
        o_ref[...] = jnp.concatenate([dk, tot[:, KV_DIM:]], axis=1).astype(BF16)

    blk = (None, None, ATT_BLOCK, 2 * KV_DIM)
    return pl.pallas_call(
        body, name=name, grid=(nb,),
        out_shape=jax.ShapeDtypeStruct((nb * ATT_BLOCK, 2 * KV_DIM), BF16),
        in_specs=[pl.BlockSpec(blk, lambda j: (jnp.minimum(j + 1, nb - 1), 0, 0, 0)),
                  pl.BlockSpec(blk, lambda j: (j, 1, 0, 0)),
                  pl.BlockSpec(blk, lambda j: (jnp.maximum(j - 1, 0), 2, 0, 0)),
                  pl.BlockSpec((ATT_BLOCK, LANES), lambda j: (j, 0)), pl.BlockSpec((ATT_BLOCK, LANES), lambda j: (j, 0))],
        out_specs=pl.BlockSpec((ATT_BLOCK, 2 * KV_DIM), lambda j: (j, 0)),
        compiler_params=_params(("parallel",)),
    )(part, part, part, cos, sin)


def _bdot(a, b, dims):
    return lax.dot_general(a.astype(BF16), b.astype(BF16), (dims, ((), ())), preferred_element_type=F32)


@jax.custom_vjp
def _dot_nn(a, b):
    return _bdot(a, b, ((1,), (0,)))


@jax.custom_vjp
def _dot_nt(a, b):
    return _bdot(a, b, ((1,), (1,)))


@jax.custom_vjp
def _dot_tn(a, b):
    return _bdot(a, b, ((0,), (0,)))


_dot_nn.defvjp(lambda a, b: (_dot_nn(a, b), (a, b)), lambda r, d: (_dot_nt(d, r[1]), _dot_tn(r[0], d)))
_dot_nt.defvjp(lambda a, b: (_dot_nt(a, b), (a, b)), lambda r, d: (_dot_nn(d, r[1]), _dot_tn(d, r[0])))
_dot_tn.defvjp(lambda a, b: (_dot_tn(a, b), (a, b)), lambda r, d: (_dot_nt(r[1], d), _dot_nn(r[0], d)))


def _split3(v):
    hi = v.astype(BF16)
    r1 = v - hi.astype(F32)
    mid = r1.astype(BF16)
    lo = (r1 - mid.astype(F32)).astype(BF16)
    return hi, mid, lo


def _exact_apply(mat, v, contract):
    mb = mat.astype(BF16)
    dims = (((contract,), (0,)), ((), ()))
    out = None
    for piece in _split3(v):
        t = lax.dot_general(mb, piece, dims, preferred_element_type=F32)
        out = t if out is None else out + t
    return out


@jax.custom_vjp
def _sel(mat, v):
    return _exact_apply(mat, v, 1)


_sel.defvjp(lambda mat, v: (_exact_apply(mat, v, 1), mat), lambda mat, d: (jnp.zeros_like(mat), _exact_apply(mat, d, 0)))


def _hg_consts(rev):
    c, sub = HG_CHUNK, HG_SUB
    r = lax.broadcasted_iota(jnp.int32, (c, c), 0)
    col = lax.broadcasted_iota(jnp.int32, (c, c), 1)
    if rev:
        r, col = c - 1 - r, c - 1 - col
    rowpos = lax.broadcasted_iota(jnp.int32, (c, HG_DIM), 0)
    if rev:
        rowpos = c - 1 - rowpos
    one = lambda cond: jnp.where(cond, 1.0, 0.0).astype(F32)
    cum = one(col <= r)
    last = one(col == c - 1)
    own = one(col == (r // sub) * sub - 1)
    before = []
    keep = []
    inside = []
    for i in range(c // sub):
        before.append(one(col == i * sub - 1))
        keep.append((r // sub == i) & (col <= r))
        inside.append(rowpos < (i + 1) * sub)
    return cum, last, own, before, keep, inside


def _hg_chunk(zq, zf, zv, lbv, s0, consts, dots):
    dot_nn, dot_nt, dot_tn, sel = dots
    cum, last, own, before, keep, inside = consts
    sig = _sigmoid(zf)
    f = lbv + (1.0 - lbv) * sig
    g = jnp.log(f)
    k = (1.0 - lbv) * (1.0 - sig)
    q = zq * _sigmoid(zq)
    b = sel(cum, g)
    b_own = sel(own, b)
    b_last = sel(last, b)
    qa = q * jnp.exp(b - b_own)
    scores = jnp.zeros((HG_CHUNK, HG_CHUNK), F32)
    for i in range(HG_CHUNK // HG_SUB):
        expo = jnp.where(inside[i], sel(before[i], b) - b, 0.0)
        kb = k * jnp.exp(expo)
        scores = scores + jnp.where(keep[i], dot_nt(qa, kb), 0.0)
    out = dot_nn(scores, zv) + dot_nn(q * jnp.exp(b), s0)
    kd = k * jnp.exp(b_last - b)
    decay = dot_tn(jnp.exp(b_last), jnp.full((HG_CHUNK, HG_DIM), 1.0 / HG_CHUNK, F32), exact=True)
    s1 = decay * s0 + dot_tn(kd, zv)
    return out, s1


def _exact_tn(a, ones):
    out = None
    for piece in _split3(a):
        t = lax.dot_general(piece, ones.astype(BF16), (((0,), (0,)), ((), ())), preferred_element_type=F32)
        out = t if out is None else out + t
    return out


@jax.custom_vjp
def _spread(a, ones):
    return _exact_tn(a, ones)


def _spread_bwd(ones, d):
    out = None
    for piece in _split3(d):
        t = lax.dot_general(ones.astype(BF16), piece, (((1,), (1,)), ((), ())), preferred_element_type=F32)
        out = t if out is None else out + t
    return out, jnp.zeros_like(ones)


_spread.defvjp(lambda a, ones: (_exact_tn(a, ones), ones), _spread_bwd)


def _hg_dots(diff):
    if diff:
        tn = lambda a, b, exact=False: _spread(a, b) if exact else _dot_tn(a, b)
        return _dot_nn, _dot_nt, tn, _sel
    tn = lambda a, b, exact=False: _exact_tn(a, b) if exact else _bdot(a, b, ((0,), (0,)))
    return (lambda a, b: _bdot(a, b, ((1,), (0,))), lambda a, b: _bdot(a, b, ((1,), (1,))), tn,
            lambda m, v: _exact_apply(m, v, 1))


def _hg_fwd(z, lb, *, rev, ts, name):
    s = z.shape[0]
    nt = s // ts
    nch = ts // HG_CHUNK
    fcol = HG_HEADS * (2 if rev else 1)

    def body(zq_ref, zf_ref, zv_ref, lb_ref, o_ref, st_ref, state_ref):
        @pl.when(pl.program_id(1) == 0)
        def _():
            state_ref[...] = jnp.zeros_like(state_ref)

        consts = _hg_consts(rev)
        dots = _hg_dots(False)
        lbv = lb_ref[...]

        def step(ci, carry):
            c = (nch - 1 - ci) if rev else ci
            rows = pl.ds(pl.multiple_of(c * HG_CHUNK, HG_CHUNK), HG_CHUNK)
            s0 = state_ref[...]
            st_ref[c] = s0
            out, s1 = _hg_chunk(zq_ref[rows, :], zf_ref[rows, :], zv_ref[rows, :], lbv, s0, consts, dots)
            o_ref[rows, :] = out
            state_ref[...] = s1
            return carry

        lax.fori_loop(0, nch, step, 0)

    trow = (lambda t: nt - 1 - t) if rev else (lambda t: t)
    col = lambda off: pl.BlockSpec((ts, HG_DIM), lambda h, t: (trow(t), off + h))
    return pl.pallas_call(
        body, name=name, grid=(HG_HEADS, nt),
        out_shape=(jax.ShapeDtypeStruct((s, D_MODEL), F32),
                   jax.ShapeDtypeStruct((HG_HEADS, s // HG_CHUNK, HG_DIM, HG_DIM), F32)),
        in_specs=[col(0), col(fcol), col(3 * HG_HEADS), pl.BlockSpec((None, 1, HG_DIM), lambda h, t: (h, 0, 0))],
        out_specs=(pl.BlockSpec((ts, HG_DIM), lambda h, t: (trow(t), h)),
                   pl.BlockSpec((None, nch, HG_DIM, HG_DIM), lambda h, t: (h, trow(t), 0, 0))),
        scratch_shapes=[pltpu.VMEM((HG_DIM, HG_DIM), F32)],
        compiler_params=_params(("parallel", "arbitrary")),
    )(z, z, z, lb)


def _hg_bwd(z, lb, states, dout, addq, addv, *, rev, ts, name):
    s = z.shape[0]
    nt = s // ts
    nch = ts // HG_CHUNK
    fcol = HG_HEADS * (2 if rev else 1)
    has_add = addq is not None

    def body(*refs):
        zq_ref, zf_ref, zv_ref, lb_ref, st_ref, do_ref = refs[:6]
        aq_ref, av_ref = (refs[6], refs[7]) if has_add else (None, None)
        dq_ref, df_ref, dv_ref, dlb_ref, dstate_ref = refs[-5:]

        @pl.when(pl.program_id(1) == 0)
        def _():
            dstate_ref[...] = jnp.zeros_like(dstate_ref)
            dlb_ref[...] = jnp.zeros_like(dlb_ref)

        consts = _hg_consts(rev)
        dots = _hg_dots(True)
        lbv = lb_ref[...]

        def step(ci, carry):
            c = ci if rev else (nch - 1 - ci)
            rows = pl.ds(pl.multiple_of(c * HG_CHUNK, HG_CHUNK), HG_CHUNK)
            fn = lambda a, b2, c2, d2, e2: _hg_chunk(a, b2, c2, d2, e2, consts, dots)
            _, pull = jax.vjp(fn, zq_ref[rows, :], zf_ref[rows, :], zv_ref[rows, :], lbv, st_ref[c])
            dq, df, dv, dlb, ds0 = pull((do_ref[rows, :], dstate_ref[...]))
            if has_add:
                dq = dq + aq_ref[rows, :]
                dv = dv + av_ref[rows, :]
            dq_ref[rows, :] = dq
            df_ref[rows, :] = df
            dv_ref[rows, :] = dv
            dlb_ref[...] += dlb
            dstate_ref[...] = ds0
            return carry

        lax.fori_loop(0, nch, step, 0)

    trow = (lambda t: t) if rev else (lambda t: nt - 1 - t)
    col = lambda off: pl.BlockSpec((ts, HG_DIM), lambda h, t: (trow(t), off + h))
    out_tile = pl.BlockSpec((ts, HG_DIM), lambda h, t: (trow(t), h))
    in_specs = [col(0), col(fcol), col(3 * HG_HEADS), pl.BlockSpec((None, 1, HG_DIM), lambda h, t: (h, 0, 0)),
                pl.BlockSpec((None, nch, HG_DIM, HG_DIM), lambda h, t: (h, trow(t), 0, 0)), out_tile]
    args = [z, z, z, lb, states, dout]
    if has_add:
        in_specs += [out_tile, out_tile]
        args += [addq, addv]
    full = jax.ShapeDtypeStruct((s, D_MODEL), F32)
    return pl.pallas_call(
        body, name=name, grid=(HG_HEADS, nt),
        out_shape=(full, full, full, jax.ShapeDtypeStruct((HG_HEADS, 1, HG_DIM), F32)),
        in_specs=in_specs,
        out_specs=(out_tile, out_tile, out_tile, pl.BlockSpec((None, 1, HG_DIM), lambda h, t: (h, 0, 0))),
        scratch_shapes=[pltpu.VMEM((HG_DIM, HG_DIM), F32)],
        compiler_params=_params(("parallel", "arbitrary")),
    )(*args)


def _hg_post(of, ob, z, norm_g, *, tm, name):
    s = of.shape[0]

    def body(of_ref, ob_ref, gate_ref, ng_ref, y_ref):
        gn = ng_ref[...]
        for h in range(HG_HEADS):
            ln = slice(HG_DIM * h, HG_DIM * (h + 1))
            o = of_ref[:, ln] + ob_ref[:, ln]
            r = lax.rsqrt(jnp.mean(o * o, axis=-1, keepdims=True) + LN_EPS)
            gt = gate_ref[:, ln]
            y_ref[:, ln] = (o * r * gn * gt * _sigmoid(gt)).astype(BF16)

    row = lambda i: (i, 0)
    return pl.pallas_call(
        body, name=name, grid=(s // tm,),
        out_shape=jax.ShapeDtypeStruct((s, D_MODEL), BF16),
        in_specs=[pl.BlockSpec((tm, D_MODEL), row), pl.BlockSpec((tm, D_MODEL), row),
                  pl.BlockSpec((tm, D_MODEL), lambda i: (i, 4)), pl.BlockSpec((1, HG_DIM), lambda i: (0, 0))],
        out_specs=pl.BlockSpec((tm, D_MODEL), row),
        compiler_params=_params(("parallel",)),
    )(of, ob, z, norm_g)


def _hg_post_bwd(dy, of, ob, z, norm_g, *, tm, name):
    s = of.shape[0]

    def body(dy_ref, of_ref, ob_ref, gate_ref, ng_ref, do_ref, dgate_ref, dng_ref):
        gn = ng_ref[...]
        tot = jnp.zeros((1, HG_DIM), F32)
        for h in range(HG_HEADS):
            ln = slice(HG_DIM * h, HG_DIM * (h + 1))
            d = dy_ref[:, ln].astype(F32)
            o = of_ref[:, ln] + ob_ref[:, ln]
            r = lax.rsqrt(jnp.mean(o * o, axis=-1, keepdims=True) + LN_EPS)
            ohat = o * r
            gt = gate_ref[:, ln]
            sg = _sigmoid(gt)
            don = d * gt * sg
            dgate_ref[:, ln] = (d * ohat * gn * sg * (1.0 + gt * (1.0 - sg))).astype(BF16)
            tot = tot + jnp.sum(don * ohat, axis=0, keepdims=True)
            dohat = don * gn
            do_ref[:, ln] = r * (dohat - ohat * jnp.mean(dohat * ohat, axis=-1, keepdims=True))

        @pl.when(pl.program_id(0) == 0)
        def _():
            dng_ref[...] = tot

        @pl.when(pl.program_id(0) > 0)
        def _():
            dng_ref[...] += tot

    row = lambda i: (i, 0)
    return pl.pallas_call(
        body, name=name, grid=(s // tm,),
        out_shape=(jax.ShapeDtypeStruct((s, D_MODEL), F32), jax.ShapeDtypeStruct((s, D_MODEL), BF16),
                   jax.ShapeDtypeStruct((1, HG_DIM), F32)),
        in_specs=[pl.BlockSpec((tm, D_MODEL), row), pl.BlockSpec((tm, D_MODEL), row), pl.BlockSpec((tm, D_MODEL), row),
                  pl.BlockSpec((tm, D_MODEL), lambda i: (i, 4)), pl.BlockSpec((1, HG_DIM), lambda i: (0, 0))],
        out_specs=(pl.BlockSpec((tm, D_MODEL), row), pl.BlockSpec((tm, D_MODEL), row),
                   pl.BlockSpec((1, HG_DIM), lambda i: (0, 0))),
        compiler_params=_params(("arbitrary",)),
    )(dy, of, ob, z, norm_g)


def _lb_fwd(logits, *, name):
    w = logits.shape[1]

    def body(l_ref, o_ref):
        lg = l_ref[...]
        e = jnp.exp(lg - jnp.max(lg, axis=0, keepdims=True))
        sm = e / jnp.sum(e, axis=0, keepdims=True)
        o_ref[0:1, :] = sm[1:2]
        o_ref[1:2, :] = sm[1:2] + sm[2:3] + sm[3:4]

    return pl.pallas_call(body, name=name, out_shape=jax.ShapeDtypeStruct((2, w), F32))(logits)


def _lb_bwd(logits, dlb, *, name):
    w = logits.shape[1]

    def body(l_ref, d_ref, o_ref):
        lg = l_ref[...]
        e = jnp.exp(lg - jnp.max(lg, axis=0, keepdims=True))
        sm = e / jnp.sum(e, axis=0, keepdims=True)
        d1, d3 = d_ref[0:1, :], d_ref[1:2, :]
        dot = sm[1:2] * (d1 + d3) + (sm[2:3] + sm[3:4]) * d3
        o_ref[0:1, :] = -sm[0:1] * dot
        o_ref[1:2, :] = sm[1:2] * (d1 + d3 - dot)
        o_ref[2:3, :] = sm[2:3] * (d3 - dot)
        o_ref[3:4, :] = sm[3:4] * (d3 - dot)

    return pl.pallas_call(body, name=name, out_shape=jax.ShapeDtypeStruct((4, w), F32))(logits, dlb)


def _adamw(w, g, m, v, *, tr, name):
    rows = w.shape[0]
    parts = g.ndim == 3
    c1 = 1.0 / (1.0 - ADAM_B1 ** ADAM_STEP)
    c2 = 1.0 / (1.0 - ADAM_B2 ** ADAM_STEP)

    def body(w_ref, g_ref, m_ref, v_ref, go_ref, d_ref, mo_ref, vo_ref):
        if parts:
            gg = g_ref[0].astype(F32)
            for i in range(1, N_DEV):
                gg = gg + g_ref[i].astype(F32)
        else:
            gg = g_ref[...]
        mm = ADAM_B1 * m_ref[...] + (1.0 - ADAM_B1) * gg
        vv = ADAM_B2 * v_ref[...] + (1.0 - ADAM_B2) * (gg * gg)
        go_ref[...] = gg
        mo_ref[...] = mm
        vo_ref[...] = vv
        d_ref[...] = -ADAM_LR * ((mm * c1) / (jnp.sqrt(vv * c2) + ADAM_EPS) + ADAM_WD * w_ref[...])

    tile = pl.BlockSpec((tr, D_MODEL), lambda i: (i, 0))
    gspec = pl.BlockSpec((N_DEV, tr, D_MODEL), lambda i: (0, i, 0)) if parts else tile
    out = jax.ShapeDtypeStruct((rows, D_MODEL), F32)
    return pl.pallas_call(
        body, name=name, grid=(rows // tr,),
        out_shape=(out, out, out, out),
        in_specs=[tile, gspec, tile, tile], out_specs=(tile, tile, tile, tile),
        compiler_params=_params(("parallel",)),
    )(w, g, m, v)


def _sum8(parts, *, name):
    def body(p_ref, o_ref):
        tot = p_ref[0]
        for i in range(1, N_DEV):
            tot = tot + p_ref[i]
        o_ref[...] = tot

    return pl.pallas_call(body, name=name, out_shape=jax.ShapeDtypeStruct(parts.shape[1:], parts.dtype))(parts)


def _rows(a):
    return a.reshape(-1, D_MODEL)


def _pack_local(shards):
    return jnp.concatenate([_rows(shards[n]) for n, _ in BIG], axis=0)


def _unpack_local(packed, like):
    out, r = {}, 0
    for n, _ in BIG:
        k = like[n].size // D_MODEL
        out[n] = packed[r:r + k].reshape(like[n].shape)
        r += k
    return out


def _unpack_gathered(gathered, like):
    out, r = {}, 0
    for n, ax in BIG:
        shp = like[n].shape
        k = like[n].size // D_MODEL
        t = gathered[:, r:r + k].reshape((N_DEV,) + shp)
        t = jnp.moveaxis(t, 0, ax)
        out[n] = t.reshape(shp[:ax] + (N_DEV * shp[ax],) + shp[ax + 1:])
        r += k
    return out


def _pack_full(grads, like):
    cols = []
    for n, ax in BIG:
        shp = like[n].shape
        t = grads[n].reshape(shp[:ax] + (N_DEV, shp[ax]) + shp[ax + 1:])
        t = jnp.moveaxis(t, ax, 0)
        cols.append(t.reshape(N_DEV, -1, D_MODEL).astype(BF16))
    return jnp.concatenate(cols, axis=1)


SMALL_ROWS = 24


def _pad_row(a):
    flat = a.reshape(1, -1)
    return jnp.pad(flat, ((0, 0), (0, D_MODEL - flat.shape[1])))


def _tile(n, pref):
    return min(n, pref)


def kernel(x, p, att_w_qkv, att_sink, att_w_o, hgrn_w_in, hgrn_lb_logits, hgrn_norm_g, hgrn_w_o, ln_mix_g, ln_mix_b, ffn_w_in, ffn_w_out, ln_ffn_g, ln_ffn_b, ple_w_gate, ple_w_proj, loss_target, m_att_w_qkv, m_att_sink, m_att_w_o, m_hgrn_w_in, m_hgrn_lb_logits, m_hgrn_norm_g, m_hgrn_w_o, m_ln_mix_g, m_ln_mix_b, m_ffn_w_in, m_ffn_w_out, m_ln_ffn_g, m_ln_ffn_b, m_ple_w_gate, m_ple_w_proj, v_att_w_qkv, v_att_sink, v_att_w_o, v_hgrn_w_in, v_hgrn_lb_logits, v_hgrn_norm_g, v_hgrn_w_o, v_ln_mix_g, v_ln_mix_b, v_ffn_w_in, v_ffn_w_out, v_ln_ffn_g, v_ln_ffn_b, v_ple_w_gate, v_ple_w_proj):
    names = ["att_w_qkv", "att_sink", "att_w_o", "hgrn_w_in", "hgrn_lb_logits", "hgrn_norm_g", "hgrn_w_o", "ln_mix_g",
             "ln_mix_b", "ffn_w_in", "ffn_w_out", "ln_ffn_g", "ln_ffn_b", "ple_w_gate", "ple_w_proj"]
    w = dict(zip(names, (att_w_qkv, att_sink, att_w_o, hgrn_w_in, hgrn_lb_logits, hgrn_norm_g, hgrn_w_o, ln_mix_g,
                         ln_mix_b, ffn_w_in, ffn_w_out, ln_ffn_g, ln_ffn_b, ple_w_gate, ple_w_proj)))
    mom = dict(zip(names, (m_att_w_qkv, m_att_sink, m_att_w_o, m_hgrn_w_in, m_hgrn_lb_logits, m_hgrn_norm_g, m_hgrn_w_o,
                           m_ln_mix_g, m_ln_mix_b, m_ffn_w_in, m_ffn_w_out, m_ln_ffn_g, m_ln_ffn_b, m_ple_w_gate,
                           m_ple_w_proj)))
    var = dict(zip(names, (v_att_w_qkv, v_att_sink, v_att_w_o, v_hgrn_w_in, v_hgrn_lb_logits, v_hgrn_norm_g, v_hgrn_w_o,
                           v_ln_mix_g, v_ln_mix_b, v_ffn_w_in, v_ffn_w_out, v_ln_ffn_g, v_ln_ffn_b, v_ple_w_gate,
                           v_ple_w_proj)))
    s = x.shape[1]
    me = 4 * lax.axis_index("x") + 2 * lax.axis_index("y") + lax.axis_index("c")
    tm = _tile(s, 512)
    tbig = _tile(s, 1024)
    ts = _tile(s // 2, 512)
    x0 = x.reshape(s, D_MODEL)
    target = loss_target.reshape(s, D_MODEL)
    pl_in = p.reshape(DEPTH, s, PLE_DIM)

    big_like = {n: w[n] for n, _ in BIG}
    w_rows = _pack_local({n: w[n] for n, _ in BIG})
    full = _unpack_gathered(_exchange(w_rows.astype(BF16), same_src=True, name="gather_weights"), big_like)
    lb_rows = jnp.pad(hgrn_lb_logits.reshape(8, HG_DIM), ((0, 0), (0, D_MODEL - HG_DIM)))
    lb_all = _exchange(lb_rows, same_src=True, name="gather_lb")[:, :, :HG_DIM]
    logits_full = jnp.moveaxis(lb_all, 0, 1).reshape(DEPTH, 2 * D_MODEL)
    lb = _lb_fwd(logits_full, name="lb_fwd")
    cos, sin = _rope_tables(s)

    saved = []
    xf, xb = x0, x0
    for i in range(DEPTH):
        j = i // 2
        sv = {"x": xf, "xb": xb}
        if i % 2 == 0:
            z = _mm(xb, full["att_w_qkv"][j], n=D_MODEL + 2 * KV_DIM, tm=tbig, tn=512, tk=D_MODEL, name="att_in")
            sink = w["att_sink"][j]
            o = _att_fwd(z, sink, cos, sin, name="att_fwd")
            w_o = full["att_w_o"][j]
        else:
            z = _mm(xb, full["hgrn_w_in"][j], n=5 * D_MODEL, tm=tbig, tn=1024, tk=D_MODEL, name="hgrn_in")
            lbl = lb[j].reshape(2, HG_HEADS, 1, HG_DIM)
            of, st_f = _hg_fwd(z, lbl[0], rev=False, ts=ts, name="hgrn_fwd_scan")
            ob, st_b = _hg_fwd(z, lbl[1], rev=True, ts=ts, name="hgrn_fwd_scan_rev")
            o = _hg_post(of, ob, z, w["hgrn_norm_g"][j].reshape(1, HG_DIM), tm=tm, name="hgrn_post")
            w_o = full["hgrn_w_o"][j]
            sv.update(of=of, ob=ob, st_f=st_f, st_b=st_b, lbl=lbl)
        sv.update(z=z, o=o)
        pre1, x1, x1b = _proj_ln(o, w_o, xf, w["ln_mix_g"][i:i + 1], w["ln_mix_b"][i:i + 1], tm=tm, name="mix_out_ln")
        gg, uu, act = _ffn_in(x1b, full["ffn_w_in"][i], tm=tm, tn=FF_TILE, name="ffn_in")
        pre2, x2, x2b = _proj_ln(act, full["ffn_w_out"][i], x1, w["ln_ffn_g"][i:i + 1], w["ln_ffn_b"][i:i + 1], tm=tm,
                                 name="ffn_out_ln")
        xf, xb = _ple_fwd(x2, x2b, pl_in[i], full["ple_w_gate"][i], full["ple_w_proj"][i], tm=tm, name="ple_fwd")
        sv.update(pre1=pre1, x1=x1, x1b=x1b, g=gg, u=uu, act=act, pre2=pre2, x2b=x2b)
        saved.append(sv)

    dx, loss_blk = _loss_head(xf, target, tm=tm, name="loss_head")
    loss = lax.psum(loss_blk[0, 0], AXES)

    gfull = {n: [None] * w[n].shape[0] for n, _ in BIG}
    small = {n: [None] * DEPTH for n in ("ln_mix_g", "ln_mix_b", "ln_ffn_g", "ln_ffn_b")}
    dlb_rows = [None] * 4
    dnorm, dsink = [None] * 2, [None] * 2
    mmw = functools.partial(_mm, ta=True, tk=tbig, out_dtype=BF16)
    for i in reversed(range(DEPTH)):
        j = i // 2
        sv = saved[i]
        dx2, da, dpp = _ple_bwd(dx, sv["x2b"], pl_in[i], full["ple_w_gate"][i], full["ple_w_proj"][i], tm=tm,
                                name="ple_bwd")
        gfull["ple_w_gate"][i] = mmw(sv["x2b"], da, n=D_MODEL, tm=D_MODEL, tn=D_MODEL, name="dw_ple_gate")
        gfull["ple_w_proj"][i] = mmw(pl_in[i], dpp, n=D_MODEL, tm=PLE_DIM, tn=D_MODEL, name="dw_ple_proj")
        dy2, dy2b, small["ln_ffn_g"][i], small["ln_ffn_b"][i] = _ln_bwd(dx2, sv["pre2"], w["ln_ffn_g"][i:i + 1], tm=tm,
                                                                         name="ln_bwd")
        dg, du = _ffn_bwd_act(dy2b, full["ffn_w_out"][i], sv["g"], sv["u"], tm=tm, tn=FF_TILE, name="ffn_bwd_act")
        gfull["ffn_w_out"][i] = mmw(sv["act"], dy2b, n=D_MODEL, tm=FF_TILE, tn=D_MODEL, name="dw_ffn_out")
        t = _mm(dg, full["ffn_w_in"][i], n=D_MODEL, tm=tbig, tn=D_MODEL, tk=FF_TILE, tb=True, add=dy2, add_scale=ALPHA,
                name="ffn_bwd_x")
        dx1 = _mm(du, full["ffn_w_in"][i], n=D_MODEL, tm=tbig, tn=D_MODEL, tk=FF_TILE, tb=True, add=t,
                  bk_off=D_FF // FF_TILE, name="ffn_bwd_x_u")
        gfull["ffn_w_in"][i] = jnp.concatenate(
            [mmw(sv["x1b"], dg, n=D_FF, tm=D_MODEL, tn=FF_TILE, name="dw_ffn_in"),
             mmw(sv["x1b"], du, n=D_FF, tm=D_MODEL, tn=FF_TILE, name="dw_ffn_in")], axis=1)
        dy1, dy1b, small["ln_mix_g"][i], small["ln_mix_b"][i] = _ln_bwd(dx1, sv["pre1"], w["ln_mix_g"][i:i + 1], tm=tm,
                                                                         name="ln_bwd")
        if i % 2 == 0:
            w_o, w_in, n_in = full["att_w_o"][j], full["att_w_qkv"][j], D_MODEL + 2 * KV_DIM
        else:
            w_o, w_in, n_in = full["hgrn_w_o"][j], full["hgrn_w_in"][j], 5 * D_MODEL
        do = _mm(dy1b, w_o, n=D_MODEL, tm=tbig, tn=D_MODEL, tk=D_MODEL, tb=True, out_dtype=BF16, name="mix_out_bwd")
        g_o = mmw(sv["o"], dy1b, n=D_MODEL, tm=D_MODEL, tn=D_MODEL, name="dw_mix_out")
        if i % 2 == 0:
            gfull["att_w_o"][j] = g_o
            dzq, part, dsk = _att_bwd(sv["z"], do, w["att_sink"][j], cos, sin, name="att_bwd")
            dzkv = _att_bwd_kv(part, cos, sin, name="att_bwd_kv")
            dz = jnp.concatenate([dzq, dzkv], axis=1)
            dsink[j] = dsk[:, 0]
        else:
            gfull["hgrn_w_o"][j] = g_o
            dsum, dgate, dnorm[j] = _hg_post_bwd(do, sv["of"], sv["ob"], sv["z"], w["hgrn_norm_g"][j].reshape(1, HG_DIM),
                                                 tm=tm, name="hgrn_post_bwd")
            dq1, df1, dv1, dlb1 = _hg_bwd(sv["z"], sv["lbl"][0], sv["st_f"], dsum, None, None, rev=False, ts=ts,
                                          name="hgrn_bwd_scan")
            dq2, df2, dv2, dlb2 = _hg_bwd(sv["z"], sv["lbl"][1], sv["st_b"], dsum, dq1, dv1, rev=True, ts=ts,
                                          name="hgrn_bwd_scan_rev")
            dz = jnp.concatenate([dq2.astype(BF16), df1.astype(BF16), df2.astype(BF16), dv2.astype(BF16), dgate], axis=1)
            dlb_rows[2 * j] = dlb1.reshape(1, D_MODEL)
            dlb_rows[2 * j + 1] = dlb2.reshape(1, D_MODEL)
        dx = _mm(dz, w_in, n=D_MODEL, tm=tbig, tn=D_MODEL, tk=512, tb=True, add=dy1, add_scale=ALPHA, name="mix_in_bwd")
        g_in = mmw(sv["xb"], dz, n=n_in, tm=D_MODEL, tn=512, name="dw_mix_in")
        gfull["att_w_qkv" if i % 2 == 0 else "hgrn_w_in"][j] = g_in
    grad_x = dx.reshape(x.shape)

    gstack = {n: jnp.stack(gfull[n]) for n, _ in BIG}
    recv = _exchange(_pack_full(gstack, big_like), same_src=False, name="exchange_grads")
    m_rows = _pack_local({n: mom[n] for n, _ in BIG})
    v_rows = _pack_local({n: var[n] for n, _ in BIG})
    outs = _adamw(w_rows, recv, m_rows, v_rows, tr=320, name="adamw_big")
    big_out = [_unpack_local(o_, big_like) for o_ in outs]

    small_rows = jnp.concatenate(
        [jnp.concatenate(small[n], axis=0) for n in ("ln_mix_g", "ln_mix_b", "ln_ffn_g", "ln_ffn_b")] + dlb_rows
        + [_pad_row(jnp.stack(dnorm)), _pad_row(jnp.stack(dsink)), jnp.zeros((2, D_MODEL), F32)], axis=0)
    small_all = _exchange(small_rows, same_src=True, name="gather_small")
    lbw, lbm, lbv = (t.reshape(4, 2 * HG_DIM) for t in (hgrn_lb_logits, mom["hgrn_lb_logits"], var["hgrn_lb_logits"]))
    summed = _sum8(small_all, name="sum_small")
    dlb_mine = lax.dynamic_slice_in_dim(summed[16:20].reshape(2, 2, HG_HEADS, HG_DIM), me, 1, axis=2)
    dlogits = _lb_bwd(lbw, dlb_mine.reshape(2, 2 * HG_DIM), name="lb_bwd")

    def small_pack(ln4, lbt, ng, sk):
        return jnp.concatenate([ln4[n] for n in ("ln_mix_g", "ln_mix_b", "ln_ffn_g", "ln_ffn_b")]
                               + [_pad_row(lbt), _pad_row(ng), _pad_row(sk), jnp.zeros((5, D_MODEL), F32)], axis=0)

    g_small = jnp.concatenate([summed[:16], _pad_row(dlogits), summed[20:22], jnp.zeros((5, D_MODEL), F32)], axis=0)
    souts = _adamw(small_pack(w, lbw, w["hgrn_norm_g"], w["att_sink"]), g_small,
                   small_pack(mom, lbm, mom["hgrn_norm_g"], mom["att_sink"]),
                   small_pack(var, lbv, var["hgrn_norm_g"], var["att_sink"]), tr=SMALL_ROWS, name="adamw_small")

    def small_unpack(t):
        out = {n: t[4 * k:4 * k + 4] for k, n in enumerate(("ln_mix_g", "ln_mix_b", "ln_ffn_g", "ln_ffn_b"))}
        out["hgrn_lb_logits"] = t[16].reshape(hgrn_lb_logits.shape)
        out["hgrn_norm_g"] = t[17, :2 * HG_DIM].reshape(hgrn_norm_g.shape)
        out["att_sink"] = t[18, :2 * N_Q_HEADS].reshape(att_sink.shape)
        return out

    result = [loss, grad_x]
    for big_t, small_t in zip(big_out, souts):
        merged = dict(big_t)
        merged.update(small_unpack(small_t))
        result += [merged[n] for n in names]
    return tuple(result)
```

```python
import functools

import jax
import jax.numpy as jnp
from jax import lax
from jax.experimental import pallas as pl
from jax.experimental.pallas import tpu as pltpu

F32 = jnp.float32
BF16 = jnp.bfloat16

D_MODEL = 1024
DEPTH = 4
HEAD_DIM = 64
N_Q_HEADS = 16
N_KV_HEADS = 4
GROUP = 4
KV_DIM = 256
ATT_BLOCK = 128
ROPE_DIM = 16
ROPE_THETA = 500000.0
HG_HEADS = 8
HG_DIM = 128
HG_CHUNK = 64
HG_SUB = 16
HG_UNROLL = 4
D_FF = 2816
FF_TILE = 1408
PLE_DIM = 256
ALPHA = (2 * DEPTH) ** 0.25
LN_EPS = 1e-5
ADAM_LR, ADAM_B1, ADAM_B2, ADAM_EPS, ADAM_WD, ADAM_STEP = 0.001, 0.9, 0.999, 1e-08, 0.01, 10

N_DEV = 8
LANES = 128
VMEM_LIMIT = 52 * 1024 * 1024
NEG = -1e30
MESH = pl.DeviceIdType.MESH
AXES = ("x", "y", "c")

BIG = (("att_w_qkv", 2), ("att_w_o", 1), ("hgrn_w_in", 2), ("hgrn_w_o", 1), ("ffn_w_in", 2), ("ffn_w_out", 1),
       ("ple_w_gate", 1), ("ple_w_proj", 2))


def _params(sem=None, vmem=VMEM_LIMIT):
    return pltpu.CompilerParams(dimension_semantics=sem, vmem_limit_bytes=vmem)


def _sigmoid(x):
    return jax.nn.sigmoid(x)


def _exchange(src, *, name):
    def body(src_ref, out_ref, send_sems, recv_sems, local_sem):
        x, y, c = lax.axis_index("x"), lax.axis_index("y"), lax.axis_index("c")
        me = 4 * x + 2 * y + c
        local = pltpu.make_async_copy(src_ref.at[me], out_ref.at[me], local_sem)
        local.start()
        copies = []
        for k in range(1, N_DEV):
            px, py, pc = x ^ (k >> 2), y ^ ((k >> 1) & 1), c ^ (k & 1)
            peer = 4 * px + 2 * py + pc
            copies.append((peer, pltpu.make_async_remote_copy(
                src_ref=src_ref.at[peer], dst_ref=out_ref.at[me], send_sem=send_sems.at[k], recv_sem=recv_sems.at[k],
                device_id=(px, py, pc), device_id_type=MESH)))
        for _, cp in copies:
            cp.start()
        for k, (peer, cp) in enumerate(copies, start=1):
            cp.wait_send()
            pltpu.make_async_remote_copy(
                src_ref=src_ref.at[peer], dst_ref=out_ref.at[peer], send_sem=send_sems.at[k], recv_sem=recv_sems.at[k],
                device_id=(x, y, c), device_id_type=MESH).wait_recv()
        local.wait()

    return pl.pallas_call(
        body, name=name,
        out_shape=jax.ShapeDtypeStruct(src.shape, src.dtype),
        in_specs=[pl.BlockSpec(memory_space=pltpu.HBM)],
        out_specs=pl.BlockSpec(memory_space=pltpu.HBM),
        scratch_shapes=[pltpu.SemaphoreType.DMA((N_DEV,)), pltpu.SemaphoreType.DMA((N_DEV,)), pltpu.SemaphoreType.DMA],
    )(src)


def _gather(src, *, name):
    def body(src_ref, out_ref, send_sems, recv_sems, local_sem):
        x, y, c = lax.axis_index("x"), lax.axis_index("y"), lax.axis_index("c")
        sibling = (x, y, 1 - c)
        chips = [(1 - x, y), (x, 1 - y), (1 - x, 1 - y)]

        def rows(px, py, pc):
            return out_ref.at[4 * px + 2 * py + pc]

        def copy(k, block, to, from_src=False):
            return pltpu.make_async_remote_copy(
                src_ref=src_ref if from_src else rows(*block), dst_ref=rows(*block), send_sem=send_sems.at[k],
                recv_sem=recv_sems.at[k], device_id=to, device_id_type=MESH)

        me = (x, y, c)
        mine = pltpu.make_async_copy(src_ref, rows(*me), local_sem)
        mine.start()
        first = [copy(0, me, sibling, from_src=True)]
        first += [copy(1 + j, me, (*chip, c), from_src=True) for j, chip in enumerate(chips)]
        for cp in first:
            cp.start()
        passed = [copy(4 + j, (*chip, c), sibling) for j, chip in enumerate(chips)]
        for j, chip in enumerate(chips):
            copy(1 + j, (*chip, c), me).wait_recv()
            passed[j].start()
        copy(0, sibling, me).wait_recv()
        for j, chip in enumerate(chips):
            copy(4 + j, (*chip, 1 - c), me).wait_recv()
        for cp in first + passed:
            cp.wait_send()
        mine.wait()

    return pl.pallas_call(
        body, name=name,
        out_shape=jax.ShapeDtypeStruct((N_DEV,) + tuple(src.shape), src.dtype),
        in_specs=[pl.BlockSpec(memory_space=pltpu.HBM)],
        out_specs=pl.BlockSpec(memory_space=pltpu.HBM),
        scratch_shapes=[pltpu.SemaphoreType.DMA((7,)), pltpu.SemaphoreType.DMA((7,)), pltpu.SemaphoreType.DMA],
    )(src)


def _mm(a, b, *, n, tm, tn, tk, ta=False, tb=False, out_dtype=F32, add=None, add_scale=1.0, bk_off=0, name):
    m, kdim = (a.shape[1], a.shape[0]) if ta else a.shape
    nk = kdim // tk
    dims = (((0 if ta else 1,), (1 if tb else 0,)), ((), ()))

    def body(*refs):
        if add is None:
            a_ref, b_ref, o_ref = refs[:3]
            add_ref = None
        else:
            a_ref, b_ref, add_ref, o_ref = refs[:4]
        acc_ref = refs[-1] if nk > 1 else None
        part = lax.dot_general(a_ref[...].astype(BF16), b_ref[...].astype(BF16), dims, preferred_element_type=F32)

        def finish(r):
            if add_ref is not None:
                r = r + add_scale * add_ref[...]
            o_ref[...] = r.astype(out_dtype)

        if nk == 1:
            finish(part)
        else:
            k = pl.program_id(2)

            @pl.when(k == 0)
            def _():
                acc_ref[...] = part

            @pl.when(k > 0)
            def _():
                acc_ref[...] += part

            @pl.when(k == nk - 1)
            def _():
                finish(acc_ref[...])

    a_spec = pl.BlockSpec((tk, tm), lambda i, j, k: (k, i)) if ta else pl.BlockSpec((tm, tk), lambda i, j, k: (i, k))
    b_spec = (pl.BlockSpec((tn, tk), lambda i, j, k: (j, k + bk_off)) if tb
              else pl.BlockSpec((tk, tn), lambda i, j, k: (k + bk_off, j)))
    in_specs, args = [a_spec, b_spec], [a, b]
    if add is not None:
        in_specs.append(pl.BlockSpec((tm, tn), lambda i, j, k: (i, j)))
        args.append(add)
    return pl.pallas_call(
        body, name=name, grid=(m // tm, n // tn, nk),
        out_shape=jax.ShapeDtypeStruct((m, n), out_dtype),
        in_specs=in_specs, out_specs=pl.BlockSpec((tm, tn), lambda i, j, k: (i, j)),
        scratch_shapes=[pltpu.VMEM((tm, tn), F32)] if nk > 1 else [],
        compiler_params=_params(("parallel", "parallel", "arbitrary")),
    )(*args)


def _layer_norm_rows(y, g, b):
    mu = jnp.mean(y, axis=-1, keepdims=True)
    yc = y - mu
    var = jnp.mean(yc * yc, axis=-1, keepdims=True)
    return yc * lax.rsqrt(var + LN_EPS) * g + b


def _proj_ln(a, w, res, g, b, *, tm, name):
    s, kdim = a.shape

    def body(a_ref, w_ref, res_ref, g_ref, b_ref, pre_ref, o_ref, obf_ref):
        h = jnp.dot(a_ref[...], w_ref[...], preferred_element_type=F32)
        pre = ALPHA * res_ref[...] + h
        out = _layer_norm_rows(pre, g_ref[...], b_ref[...])
        pre_ref[...] = pre
        o_ref[...] = out
        obf_ref[...] = out.astype(BF16)

    row = lambda i: (i, 0)
    fix = lambda i: (0, 0)
    return pl.pallas_call(
        body, name=name, grid=(s // tm,),
        out_shape=(jax.ShapeDtypeStruct((s, D_MODEL), F32), jax.ShapeDtypeStruct((s, D_MODEL), F32),
                   jax.ShapeDtypeStruct((s, D_MODEL), BF16)),
        in_specs=[pl.BlockSpec((tm, kdim), row), pl.BlockSpec((kdim, D_MODEL), fix), pl.BlockSpec((tm, D_MODEL), row),
                  pl.BlockSpec((1, D_MODEL), fix), pl.BlockSpec((1, D_MODEL), fix)],
        out_specs=(pl.BlockSpec((tm, D_MODEL), row),) * 3,
        compiler_params=_params(("parallel",)),
    )(a, w, res, g, b)


def _ln_bwd(dout, pre, g, *, tm, name):
    s = dout.shape[0]

    def body(do_ref, pre_ref, g_ref, dy_ref, dybf_ref, dg_ref, db_ref):
        do = do_ref[...]
        y = pre_ref[...]
        mu = jnp.mean(y, axis=-1, keepdims=True)
        yc = y - mu
        var = jnp.mean(yc * yc, axis=-1, keepdims=True)
        rstd = lax.rsqrt(var + LN_EPS)
        xhat = yc * rstd
        dxhat = do * g_ref[...]
        dy = rstd * (dxhat - jnp.mean(dxhat, axis=-1, keepdims=True)
                     - xhat * jnp.mean(dxhat * xhat, axis=-1, keepdims=True))
        dy_ref[...] = dy
        dybf_ref[...] = dy.astype(BF16)
        pg = jnp.sum(do * xhat, axis=0, keepdims=True)
        pb = jnp.sum(do, axis=0, keepdims=True)

        @pl.when(pl.program_id(0) == 0)
        def _():
            dg_ref[...] = pg
            db_ref[...] = pb

        @pl.when(pl.program_id(0) > 0)
        def _():
            dg_ref[...] += pg
            db_ref[...] += pb

    row = lambda i: (i, 0)
    fix = lambda i: (0, 0)
    return pl.pallas_call(
        body, name=name, grid=(s // tm,),
        out_shape=(jax.ShapeDtypeStruct((s, D_MODEL), F32), jax.ShapeDtypeStruct((s, D_MODEL), BF16),
                   jax.ShapeDtypeStruct((1, D_MODEL), F32), jax.ShapeDtypeStruct((1, D_MODEL), F32)),
        in_specs=[pl.BlockSpec((tm, D_MODEL), row), pl.BlockSpec((tm, D_MODEL), row), pl.BlockSpec((1, D_MODEL), fix)],
        out_specs=(pl.BlockSpec((tm, D_MODEL), row), pl.BlockSpec((tm, D_MODEL), row),
                   pl.BlockSpec((1, D_MODEL), fix), pl.BlockSpec((1, D_MODEL), fix)),
        compiler_params=_params(("arbitrary",)),
    )(dout, pre, g)


def _ffn_in(xbf, w, *, tm, tn, name):
    s = xbf.shape[0]
    nj = D_FF // tn

    def body(x_ref, wg_ref, wu_ref, g_ref, u_ref, act_ref):
        xv = x_ref[...]
        gg = jnp.dot(xv, wg_ref[...], preferred_element_type=F32)
        uu = jnp.dot(xv, wu_ref[...], preferred_element_type=F32)
        g_ref[...] = gg.astype(BF16)
        u_ref[...] = uu.astype(BF16)
        act_ref[...] = (gg * _sigmoid(gg) * uu).astype(BF16)

    out = jax.ShapeDtypeStruct((s, D_FF), BF16)
    tile = pl.BlockSpec((tm, tn), lambda j, i: (i, j))
    return pl.pallas_call(
        body, name=name, grid=(nj, s // tm),
        out_shape=(out, out, out),
        in_specs=[pl.BlockSpec((tm, D_MODEL), lambda j, i: (i, 0)), pl.BlockSpec((D_MODEL, tn), lambda j, i: (0, j)),
                  pl.BlockSpec((D_MODEL, tn), lambda j, i: (0, j + nj))],
        out_specs=(tile, tile, tile),
        compiler_params=_params(("parallel", "parallel")),
    )(xbf, w, w)


def _ffn_bwd_act(dybf, w_out, g, u, *, tm, tn, name):
    s = dybf.shape[0]

    def body(dy_ref, w_ref, g_ref, u_ref, dg_ref, du_ref):
        dact = lax.dot_general(dy_ref[...], w_ref[...], (((1,), (1,)), ((), ())), preferred_element_type=F32)
        gg = g_ref[...].astype(F32)
        uu = u_ref[...].astype(F32)
        sg = _sigmoid(gg)
        dg_ref[...] = (dact * uu * sg * (1.0 + gg * (1.0 - sg))).astype(BF16)
        du_ref[...] = (dact * gg * sg).astype(BF16)

    out = jax.ShapeDtypeStruct((s, D_FF), BF16)
    tile = pl.BlockSpec((tm, tn), lambda j, i: (i, j))
    return pl.pallas_call(
        body, name=name, grid=(D_FF // tn, s // tm),
        out_shape=(out, out),
        in_specs=[pl.BlockSpec((tm, D_MODEL), lambda j, i: (i, 0)), pl.BlockSpec((tn, D_MODEL), lambda j, i: (j, 0)),
                  tile, tile],
        out_specs=(tile, tile),
        compiler_params=_params(("parallel", "parallel")),
    )(dybf, w_out, g, u)


def _ple_fwd(x2, x2bf, p, w_gate, w_proj, *, tm, name):
    s = x2.shape[0]

    def body(x_ref, xbf_ref, p_ref, wg_ref, wp_ref, o_ref, obf_ref):
        a = jnp.dot(xbf_ref[...], wg_ref[...], preferred_element_type=F32)
        pp = jnp.dot(p_ref[...].astype(BF16), wp_ref[...], preferred_element_type=F32)
        out = x_ref[...] + _sigmoid(a) * pp
        o_ref[...] = out
        obf_ref[...] = out.astype(BF16)

    row = lambda i: (i, 0)
    fix = lambda i: (0, 0)
    return pl.pallas_call(
        body, name=name, grid=(s // tm,),
        out_shape=(jax.ShapeDtypeStruct((s, D_MODEL), F32), jax.ShapeDtypeStruct((s, D_MODEL), BF16)),
        in_specs=[pl.BlockSpec((tm, D_MODEL), row), pl.BlockSpec((tm, D_MODEL), row), pl.BlockSpec((tm, PLE_DIM), row),
                  pl.BlockSpec((D_MODEL, D_MODEL), fix), pl.BlockSpec((PLE_DIM, D_MODEL), fix)],
        out_specs=(pl.BlockSpec((tm, D_MODEL), row), pl.BlockSpec((tm, D_MODEL), row)),
        compiler_params=_params(("parallel",)),
    )(x2, x2bf, p, w_gate, w_proj)


def _ple_bwd(dx3, x2bf, p, w_gate, w_proj, *, tm, name):
    s = dx3.shape[0]

    def body(d_ref, xbf_ref, p_ref, wg_ref, wp_ref, dx_ref, da_ref, dpp_ref):
        d = d_ref[...]
        a = jnp.dot(xbf_ref[...], wg_ref[...], preferred_element_type=F32)
        pp = jnp.dot(p_ref[...].astype(BF16), wp_ref[...], preferred_element_type=F32)
        sg = _sigmoid(a)
        da = (d * pp * sg * (1.0 - sg)).astype(BF16)
        da_ref[...] = da
        dpp_ref[...] = (d * sg).astype(BF16)
        dx_ref[...] = d + lax.dot_general(da, wg_ref[...], (((1,), (1,)), ((), ())), preferred_element_type=F32)

    row = lambda i: (i, 0)
    fix = lambda i: (0, 0)
    return pl.pallas_call(
        body, name=name, grid=(s // tm,),
        out_shape=(jax.ShapeDtypeStruct((s, D_MODEL), F32), jax.ShapeDtypeStruct((s, D_MODEL), BF16),
                   jax.ShapeDtypeStruct((s, D_MODEL), BF16)),
        in_specs=[pl.BlockSpec((tm, D_MODEL), row), pl.BlockSpec((tm, D_MODEL), row), pl.BlockSpec((tm, PLE_DIM), row),
                  pl.BlockSpec((D_MODEL, D_MODEL), fix), pl.BlockSpec((PLE_DIM, D_MODEL), fix)],
        out_specs=(pl.BlockSpec((tm, D_MODEL), row),) * 3,
        compiler_params=_params(("parallel",)),
    )(dx3, x2bf, p, w_gate, w_proj)


def _loss_head(y, target, *, tm, name):
    s = y.shape[0]

    def body(y_ref, t_ref, dy_ref, loss_ref, acc_ref):
        err = y_ref[...] - t_ref[...]
        dy_ref[...] = err * (1.0 / D_MODEL)
        part = jnp.sum(err * err, axis=0, keepdims=True)

        @pl.when(pl.program_id(0) == 0)
        def _():
            acc_ref[...] = part

        @pl.when(pl.program_id(0) > 0)
        def _():
            acc_ref[...] += part

        @pl.when(pl.program_id(0) == pl.num_programs(0) - 1)
        def _():
            tot = jnp.sum(acc_ref[...], axis=1, keepdims=True) * (0.5 / D_MODEL)
            loss_ref[...] = jnp.broadcast_to(tot, (8, LANES))

    row = lambda i: (i, 0)
    return pl.pallas_call(
        body, name=name, grid=(s // tm,),
        out_shape=(jax.ShapeDtypeStruct((s, D_MODEL), F32), jax.ShapeDtypeStruct((8, LANES), F32)),
        in_specs=[pl.BlockSpec((tm, D_MODEL), row), pl.BlockSpec((tm, D_MODEL), row)],
        out_specs=(pl.BlockSpec((tm, D_MODEL), row), pl.BlockSpec((8, LANES), lambda i: (0, 0))),
        scratch_shapes=[pltpu.VMEM((1, D_MODEL), F32)],
        compiler_params=_params(("arbitrary",)),
    )(y, target)


def _rope_tables(s):
    inv = ROPE_THETA ** (-jnp.arange(0, ROPE_DIM, 2, dtype=F32) / ROPE_DIM)
    ang = jnp.arange(s, dtype=F32)[:, None] * inv[None, :]
    cos, sin = jnp.cos(ang), jnp.sin(ang)
    ones = jnp.ones((s, HEAD_DIM - ROPE_DIM), F32)
    c_head = jnp.concatenate([cos, cos, ones], axis=1)
    s_head = jnp.concatenate([-sin, sin, 0.0 * ones], axis=1)
    return jnp.concatenate([c_head, c_head], axis=1), jnp.concatenate([s_head, s_head], axis=1)


def _rope(v, cos, sin):
    n = v.shape[1] // LANES
    width = v.shape[1]
    cos_w = jnp.tile(cos, (1, n)) if n > 1 else cos
    sin_w = jnp.tile(sin, (1, n)) if n > 1 else sin
    dim = lax.broadcasted_iota(jnp.int32, (1, width), 1) % HEAD_DIM
    partner = jnp.where(dim < ROPE_DIM // 2, pltpu.roll(v, width - ROPE_DIM // 2, 1), pltpu.roll(v, ROPE_DIM // 2, 1))
    return v * cos_w + partner * sin_w


def _unrope(dv, cos, sin):
    n = dv.shape[1] // LANES
    width = dv.shape[1]
    cos_w = jnp.tile(cos, (1, n)) if n > 1 else cos
    sin_w = jnp.tile(sin, (1, n)) if n > 1 else sin
    t = dv * sin_w
    dim = lax.broadcasted_iota(jnp.int32, (1, width), 1) % HEAD_DIM
    partner = jnp.where(dim < ROPE_DIM // 2, pltpu.roll(t, width - ROPE_DIM // 2, 1),
                        jnp.where(dim < ROPE_DIM, pltpu.roll(t, ROPE_DIM // 2, 1), 0.0))
    return dv * cos_w + partner


def _att_mask(i, nb):
    rows = GROUP * ATT_BLOCK
    r = lax.broadcasted_iota(jnp.int32, (rows, 3 * ATT_BLOCK), 0) % ATT_BLOCK
    cidx = lax.broadcasted_iota(jnp.int32, (rows, 3 * ATT_BLOCK), 1)
    rel = r + ATT_BLOCK - cidx
    ok = (rel <= ATT_BLOCK) & (rel >= -ATT_BLOCK)
    ok = ok & ((cidx >= ATT_BLOCK) | (i > 0)) & ((cidx < 2 * ATT_BLOCK) | (i < nb - 1))
    return ok


def _half_mask(half):
    lane = lax.broadcasted_iota(jnp.int32, (1, LANES), 1)
    return (lane // HEAD_DIM) == half


def _stack_q(q, h):
    parts = []
    for gq in range(GROUP):
        n = GROUP * h + gq
        grp = q[:, LANES * (n // 2):LANES * (n // 2 + 1)]
        grp = jnp.where(_half_mask(n % 2), grp, 0.0)
        if n % 2 != h % 2:
            grp = pltpu.roll(grp, HEAD_DIM, 1)
        parts.append(grp)
    return jnp.concatenate(parts, axis=0)


def _unstack_q(stacked, h, acc):
    for gq in range(GROUP):
        n = GROUP * h + gq
        grp = stacked[ATT_BLOCK * gq:ATT_BLOCK * (gq + 1), :]
        grp = jnp.where(_half_mask(h % 2), grp, 0.0)
        if n % 2 != h % 2:
            grp = pltpu.roll(grp, HEAD_DIM, 1)
        acc[n // 2] = grp if acc[n // 2] is None else acc[n // 2] + grp
    return acc


def _sink_rows(sink_ref, h):
    rows = GROUP * ATT_BLOCK
    grp = lax.broadcasted_iota(jnp.int32, (rows, 1), 0) // ATT_BLOCK
    out = jnp.zeros((rows, 1), F32)
    for gq in range(GROUP):
        out = jnp.where(grp == gq, sink_ref[GROUP * h + gq], out)
    return out


def _att_probs(qs, kh, sink, valid):
    s = lax.dot_general(qs, kh, (((1,), (1,)), ((), ())), preferred_element_type=F32)
    s = jnp.where(valid, s, NEG)
    m = jnp.maximum(jnp.max(s, axis=-1, keepdims=True), sink)
    p = jnp.exp(s - m)
    es = jnp.exp(sink - m)
    den = jnp.sum(p, axis=-1, keepdims=True) + es
    inv = 1.0 / den
    return p * inv, es * inv


def _att_specs(nb):
    prev = lambda i: (jnp.maximum(i - 1, 0), 0)
    cur = lambda i: (i, 0)
    nxt = lambda i: (jnp.minimum(i + 1, nb - 1), 0)
    kv = lambda f: (lambda i: (f(i)[0], 2))
    tab = [pl.BlockSpec((ATT_BLOCK, LANES), f) for f in (cur, prev, cur, nxt)]
    z = [pl.BlockSpec((ATT_BLOCK, D_MODEL), cur)] + [pl.BlockSpec((ATT_BLOCK, 2 * KV_DIM), kv(f)) for f in (prev, cur, nxt)]
    return z, tab


def _att_load(zq_ref, kp_ref, kc_ref, kn_ref, cq_ref, sq_ref, cp_ref, sp_ref, cc_ref, sc_ref, cn_ref, sn_ref):
    q = (_rope(zq_ref[...], cq_ref[...], sq_ref[...]) * (HEAD_DIM ** -0.5))
    ks, vs = [], []
    for ref, c_ref, s_ref in ((kp_ref, cp_ref, sp_ref), (kc_ref, cc_ref, sc_ref), (kn_ref, cn_ref, sn_ref)):
        kvb = ref[...]
        ks.append(_rope(kvb[:, :KV_DIM], c_ref[...], s_ref[...]))
        vs.append(kvb[:, KV_DIM:])
    return q, jnp.concatenate(ks, axis=0).astype(BF16), jnp.concatenate(vs, axis=0).astype(BF16)


def _att_fwd(z, sink, cos, sin, *, name):
    s = z.shape[0]
    nb = s // ATT_BLOCK

    def body(zq_ref, kp_ref, kc_ref, kn_ref, cq_ref, cp_ref, cc_ref, cn_ref, sq_ref, sp_ref, sc_ref, sn_ref, sink_ref,
             o_ref):
        i = pl.program_id(0)
        q, k, v = _att_load(zq_ref, kp_ref, kc_ref, kn_ref, cq_ref, sq_ref, cp_ref, sp_ref, cc_ref, sc_ref, cn_ref, sn_ref)
        valid = _att_mask(i, nb)
        acc = [None] * (N_Q_HEADS // 2)
        for h in range(N_KV_HEADS):
            lanes = slice(LANES * (h // 2), LANES * (h // 2 + 1))
            qs = _stack_q(q, h).astype(BF16)
            prob, _ = _att_probs(qs, k[:, lanes], _sink_rows(sink_ref, h), valid)
            oh = jnp.dot(prob.astype(BF16), v[:, lanes], preferred_element_type=F32)
            acc = _unstack_q(oh, h, acc)
        o_ref[...] = jnp.concatenate(acc, axis=1).astype(BF16)

    zspecs, tab = _att_specs(nb)
    return pl.pallas_call(
        body, name=name, grid=(nb,),
        out_shape=jax.ShapeDtypeStruct((s, D_MODEL), BF16),
        in_specs=zspecs + tab + tab + [pl.BlockSpec(memory_space=pltpu.SMEM)],
        out_specs=pl.BlockSpec((ATT_BLOCK, D_MODEL), lambda i: (i, 0)),
        compiler_params=_params(("parallel",)),
    )(z, z, z, z, cos, cos, cos, cos, sin, sin, sin, sin, sink)


def _att_bwd(z, do, sink, cos, sin, *, name):
    s = z.shape[0]
    nb = s // ATT_BLOCK

    def body(zq_ref, kp_ref, kc_ref, kn_ref, cq_ref, cp_ref, cc_ref, cn_ref, sq_ref, sp_ref, sc_ref, sn_ref, sink_ref,
             do_ref, dq_ref, part_ref, dsink_ref):
        i = pl.program_id(0)
        q, k, v = _att_load(zq_ref, kp_ref, kc_ref, kn_ref, cq_ref, sq_ref, cp_ref, sp_ref, cc_ref, sc_ref, cn_ref, sn_ref)
        valid = _att_mask(i, nb)
        dout = do_ref[...].astype(F32)
        dq_acc = [None] * (N_Q_HEADS // 2)
        dk_acc = [None] * 2
        dv_acc = [None] * 2
        ds_rows = []
        for h in range(N_KV_HEADS):
            grp = h // 2
            lanes = slice(LANES * grp, LANES * (grp + 1))
            qs = _stack_q(q, h).astype(BF16)
            dos = _stack_q(dout, h).astype(BF16)
            prob, psink = _att_probs(qs, k[:, lanes], _sink_rows(sink_ref, h), valid)
            dprob = lax.dot_general(dos, v[:, lanes], (((1,), (1,)), ((), ())), preferred_element_type=F32)
            delta = jnp.sum(prob * dprob, axis=-1, keepdims=True)
            dsc = (prob * (dprob - delta)).astype(BF16)
            ds_rows.append(-psink * delta)
            dqs = jnp.dot(dsc, k[:, lanes], preferred_element_type=F32)
            dq_acc = _unstack_q(dqs, h, dq_acc)
            dkh = lax.dot_general(dsc, qs, (((0,), (0,)), ((), ())), preferred_element_type=F32)
            dvh = lax.dot_general(prob.astype(BF16), dos, (((0,), (0,)), ((), ())), preferred_element_type=F32)
            dk_acc[grp] = dkh if dk_acc[grp] is None else dk_acc[grp] + dkh
            dv_acc[grp] = dvh if dv_acc[grp] is None else dv_acc[grp] + dvh
        dq = jnp.concatenate(dq_acc, axis=1) * (HEAD_DIM ** -0.5)
        dq_ref[...] = _unrope(dq, cq_ref[...], sq_ref[...]).astype(BF16)
        part = jnp.concatenate(dk_acc + dv_acc, axis=1)
        for wdw in range(3):
            part_ref[wdw] = part[ATT_BLOCK * wdw:ATT_BLOCK * (wdw + 1), :]
        rows = []
        for h in range(N_KV_HEADS):
            for gq in range(GROUP):
                tot = jnp.sum(ds_rows[h][ATT_BLOCK * gq:ATT_BLOCK * (gq + 1), :], axis=0, keepdims=True)
                rows.append(jnp.broadcast_to(tot, (1, LANES)))
        dsink = jnp.concatenate(rows, axis=0)

        @pl.when(i == 0)
        def _():
            dsink_ref[...] = dsink

        @pl.when(i > 0)
        def _():
            dsink_ref[...] += dsink

    zspecs, tab = _att_specs(nb)
    return pl.pallas_call(
        body, name=name, grid=(nb,),
        out_shape=(jax.ShapeDtypeStruct((s, D_MODEL), BF16), jax.ShapeDtypeStruct((nb, 3, ATT_BLOCK, 2 * KV_DIM), F32),
                   jax.ShapeDtypeStruct((N_Q_HEADS, LANES), F32)),
        in_specs=zspecs + tab + tab + [pl.BlockSpec(memory_space=pltpu.SMEM), pl.BlockSpec((ATT_BLOCK, D_MODEL), lambda i: (i, 0))],
        out_specs=(pl.BlockSpec((ATT_BLOCK, D_MODEL), lambda i: (i, 0)),
                   pl.BlockSpec((None, 3, ATT_BLOCK, 2 * KV_DIM), lambda i: (i, 0, 0, 0)),
                   pl.BlockSpec((N_Q_HEADS, LANES), lambda i: (0, 0))),
        compiler_params=_params(("arbitrary",)),
    )(z, z, z, z, cos, cos, cos, cos, sin, sin, sin, sin, sink, do)


def _att_bwd_kv(part, cos, sin, *, name):
    nb = part.shape[0]

    def body(pn_ref, pc_ref, pp_ref, c_ref, s_ref, o_ref):
        j = pl.program_id(0)
        tot = pc_ref[...]
        tot = tot + jnp.where(j < nb - 1, pn_ref[...], 0.0)
        tot = tot + jnp.where(j > 0, pp_ref[...], 0.0)
        dk = _unrope(tot[:, :KV_DIM], c_ref[...], s_ref[...])
        o_ref[...] = jnp.concatenate([dk, tot[:, KV_DIM:]], axis=1).astype(BF16)

    blk = (None, None, ATT_BLOCK, 2 * KV_DIM)
    return pl.pallas_call(
        body, name=name, grid=(nb,),
        out_shape=jax.ShapeDtypeStruct((nb * ATT_BLOCK, 2 * KV_DIM), BF16),
        in_specs=[pl.BlockSpec(blk, lambda j: (jnp.minimum(j + 1, nb - 1), 0, 0, 0)),
                  pl.BlockSpec(blk, lambda j: (j, 1, 0, 0)),
                  pl.BlockSpec(blk, lambda j: (jnp.maximum(j - 1, 0), 2, 0, 0)),
                  pl.BlockSpec((ATT_BLOCK, LANES), lambda j: (j, 0)), pl.BlockSpec((ATT_BLOCK, LANES), lambda j: (j, 0))],
        out_specs=pl.BlockSpec((ATT_BLOCK, 2 * KV_DIM), lambda j: (j, 0)),
        compiler_params=_params(("parallel",)),
    )(part, part, part, cos, sin)


def _bdot(a, b, dims):
    return lax.dot_general(a.astype(BF16), b.astype(BF16), (dims, ((), ())), preferred_element_type=F32)


@jax.custom_vjp
def _dot_nn(a, b):
    return _bdot(a, b, ((1,), (0,)))


@jax.custom_vjp
def _dot_nt(a, b):
    return _bdot(a, b, ((1,), (1,)))


@jax.custom_vjp
def _dot_tn(a, b):
    return _bdot(a, b, ((0,), (0,)))


_dot_nn.defvjp(lambda a, b: (_dot_nn(a, b), (a, b)), lambda r, d: (_dot_nt(d, r[1]), _dot_tn(r[0], d)))
_dot_nt.defvjp(lambda a, b: (_dot_nt(a, b), (a, b)), lambda r, d: (_dot_nn(d, r[1]), _dot_tn(d, r[0])))
_dot_tn.defvjp(lambda a, b: (_dot_tn(a, b), (a, b)), lambda r, d: (_dot_nt(r[1], d), _dot_nn(r[0], d)))


def _split3(v):
    hi = v.astype(BF16)
    r1 = v - hi.astype(F32)
    mid = r1.astype(BF16)
    lo = (r1 - mid.astype(F32)).astype(BF16)
    return hi, mid, lo


def _exact_apply(mat, v, contract):
    mb = mat.astype(BF16)
    dims = (((contract,), (0,)), ((), ()))
    out = None
    for piece in _split3(v):
        t = lax.dot_general(mb, piece, dims, preferred_element_type=F32)
        out = t if out is None else out + t
    return out


@jax.custom_vjp
def _sel(mat, v):
    return _exact_apply(mat, v, 1)


_sel.defvjp(lambda mat, v: (_exact_apply(mat, v, 1), mat), lambda mat, d: (jnp.zeros_like(mat), _exact_apply(mat, d, 0)))


def _hg_consts(rev):
    c, sub = HG_CHUNK, HG_SUB
    r = lax.broadcasted_iota(jnp.int32, (c, c), 0)
    col = lax.broadcasted_iota(jnp.int32, (c, c), 1)
    rowpos = lax.broadcasted_iota(jnp.int32, (c, HG_DIM), 0)
    if rev:
        r, col, rowpos = c - 1 - r, c - 1 - col, c - 1 - rowpos
    cum = jnp.where(col <= r, 1.0, 0.0).astype(F32)
    keep = [(r // sub == i) & (col <= r) for i in range(c // sub)]
    return cum, keep, rowpos


def _pick(b, rowpos, t):
    return jnp.sum(jnp.where(rowpos == t, b, 0.0), axis=0, keepdims=True)


def _hg_local(zq, zf, zv, lbv, consts, dots):
    dot_nn, dot_nt, dot_tn, sel = dots
    cum, keep, rowpos = consts
    nsub = HG_CHUNK // HG_SUB
    sig = _sigmoid(zf)
    f = lbv + (1.0 - lbv) * sig
    g = jnp.log(f)
    k = (1.0 - lbv) * (1.0 - sig)
    q = zq * _sigmoid(zq)
    b = sel(cum, g)
    b_last = _pick(b, rowpos, HG_CHUNK - 1)
    before = [None] + [_pick(b, rowpos, i * HG_SUB - 1) for i in range(1, nsub)]
    b_own = jnp.zeros_like(b)
    for i in range(1, nsub):
        b_own = jnp.where(rowpos // HG_SUB == i, before[i], b_own)
    qa = q * jnp.exp(b - b_own)
    scores = jnp.zeros((HG_CHUNK, HG_CHUNK), F32)
    for i in range(nsub):
        ref_i = before[i] - b if i else -b
        kb = k * jnp.exp(jnp.where(rowpos < (i + 1) * HG_SUB, ref_i, 0.0))
        scores = scores + jnp.where(keep[i], dot_nt(qa, kb), 0.0)
    return dot_nn(scores, zv), q * jnp.exp(b), k * jnp.exp(b_last - b), jnp.exp(b_last)


def _hg_chunk(zq, zf, zv, lbv, st, consts, dots):
    intra, qs, kd, dec = _hg_local(zq, zf, zv, lbv, consts, dots)
    return intra + dots[1](qs, st), dec * st + dots[2](zv, kd)


def _hg_dots(diff):
    if diff:
        return _dot_nn, _dot_nt, _dot_tn, _sel
    return (lambda a, b: _bdot(a, b, ((1,), (0,))), lambda a, b: _bdot(a, b, ((1,), (1,))),
            lambda a, b: _bdot(a, b, ((0,), (0,))), lambda m, v: _exact_apply(m, v, 1))


def _chunk_loop(nch, step, init):
    per = min(HG_UNROLL, nch)

    def trip(i, carry):
        for u in range(per):
            carry = step(i * per + u, carry)
        return carry

    return lax.fori_loop(0, nch // per, trip, init)


def _hg_specs(ts, nch, trow):
    tile = pl.BlockSpec((ts, HG_DIM), lambda h, t: (trow(t), h))
    mats = pl.BlockSpec((None, nch, HG_DIM, HG_DIM), lambda h, t: (h, trow(t), 0, 0))
    vecs = pl.BlockSpec((None, nch, 1, HG_DIM), lambda h, t: (h, trow(t), 0, 0))
    return tile, mats, vecs


def _hg_prep(z, lb, *, rev, ts, name):
    s = z.shape[0]
    nt = s // ts
    nch = ts // HG_CHUNK
    fcol = HG_HEADS * (2 if rev else 1)

    def body(zq_ref, zf_ref, zv_ref, lb_ref, intra_ref, qs_ref, upd_ref, dec_ref):
        consts = _hg_consts(rev)
        dots = _hg_dots(False)
        lbv = lb_ref[...]

        def step(c, carry):
            rows = pl.ds(pl.multiple_of(c * HG_CHUNK, HG_CHUNK), HG_CHUNK)
            zv = zv_ref[rows, :]
            intra, qs, kd, dec = _hg_local(zq_ref[rows, :], zf_ref[rows, :], zv, lbv, consts, dots)
            intra_ref[rows, :] = intra
            qs_ref[rows, :] = qs.astype(BF16)
            upd_ref[c] = dots[2](zv, kd)
            dec_ref[c] = dec
            return carry

        _chunk_loop(nch, step, 0)

    col = lambda off: pl.BlockSpec((ts, HG_DIM), lambda h, t: (t, off + h))
    tile, mats, vecs = _hg_specs(ts, nch, lambda t: t)
    nchunks = s // HG_CHUNK
    return pl.pallas_call(
        body, name=name, grid=(HG_HEADS, nt),
        out_shape=(jax.ShapeDtypeStruct((s, D_MODEL), F32), jax.ShapeDtypeStruct((s, D_MODEL), BF16),
                   jax.ShapeDtypeStruct((HG_HEADS, nchunks, HG_DIM, HG_DIM), F32),
                   jax.ShapeDtypeStruct((HG_HEADS, nchunks, 1, HG_DIM), F32)),
        in_specs=[col(0), col(fcol), col(3 * HG_HEADS), pl.BlockSpec((None, 1, HG_DIM), lambda h, t: (h, 0, 0))],
        out_specs=(tile, tile, mats, vecs),
        compiler_params=_params(("parallel", "parallel")),
    )(z, z, z, lb)


def _hg_scan(intra, qs, upd, dec, *, rev, ts, name):
    s = intra.shape[0]
    nt = s // ts
    nch = ts // HG_CHUNK

    def body(intra_ref, qs_ref, upd_ref, dec_ref, o_ref, st_ref, state_ref):
        @pl.when(pl.program_id(1) == 0)
        def _():
            state_ref[...] = jnp.zeros_like(state_ref)

        def step(ci, st):
            c = (nch - 1 - ci) if rev else ci
            rows = pl.ds(pl.multiple_of(c * HG_CHUNK, HG_CHUNK), HG_CHUNK)
            st_ref[c] = st
            o_ref[rows, :] = intra_ref[rows, :] + _bdot(qs_ref[rows, :], st, ((1,), (1,)))
            return dec_ref[c] * st + upd_ref[c]

        state_ref[...] = _chunk_loop(nch, step, state_ref[...])

    tile, mats, vecs = _hg_specs(ts, nch, (lambda t: nt - 1 - t) if rev else (lambda t: t))
    return pl.pallas_call(
        body, name=name, grid=(HG_HEADS, nt),
        out_shape=(jax.ShapeDtypeStruct((s, D_MODEL), F32),
                   jax.ShapeDtypeStruct((HG_HEADS, s // HG_CHUNK, HG_DIM, HG_DIM), F32)),
        in_specs=[tile, tile, mats, vecs], out_specs=(tile, mats),
        scratch_shapes=[pltpu.VMEM((HG_DIM, HG_DIM), F32)],
        compiler_params=_params(("parallel", "arbitrary")),
    )(intra, qs, upd, dec)


def _hg_scan_bwd(qs, dout, dec, *, rev, ts, name):
    s = qs.shape[0]
    nt = s // ts
    nch = ts // HG_CHUNK

    def body(qs_ref, do_ref, dec_ref, g_ref, grad_ref):
        @pl.when(pl.program_id(1) == 0)
        def _():
            grad_ref[...] = jnp.zeros_like(grad_ref)

        def step(ci, gr):
            c = ci if rev else (nch - 1 - ci)
            rows = pl.ds(pl.multiple_of(c * HG_CHUNK, HG_CHUNK), HG_CHUNK)
            g_ref[c] = gr
            return dec_ref[c] * gr + _bdot(do_ref[rows, :], qs_ref[rows, :], ((0,), (0,)))

        grad_ref[...] = _chunk_loop(nch, step, grad_ref[...])

    tile, mats, vecs = _hg_specs(ts, nch, (lambda t: t) if rev else (lambda t: nt - 1 - t))
    return pl.pallas_call(
        body, name=name, grid=(HG_HEADS, nt),
        out_shape=jax.ShapeDtypeStruct((HG_HEADS, s // HG_CHUNK, HG_DIM, HG_DIM), F32),
        in_specs=[tile, tile, vecs], out_specs=mats,
        scratch_shapes=[pltpu.VMEM((HG_DIM, HG_DIM), F32)],
        compiler_params=_params(("parallel", "arbitrary")),
    )(qs, dout, dec)


def _hg_bwd(z, lb, states, gstates, dout, addq, addv, *, rev, ts, name):
    s = z.shape[0]
    nt = s // ts
    nch = ts // HG_CHUNK
    fcol = HG_HEADS * (2 if rev else 1)
    has_add = addq is not None

    def body(*refs):
        zq_ref, zf_ref, zv_ref, lb_ref, st_ref, g_ref, do_ref = refs[:7]
        aq_ref, av_ref = (refs[7], refs[8]) if has_add else (None, None)
        dq_ref, df_ref, dv_ref, dlb_ref = refs[-4:]
        consts = _hg_consts(rev)
        dots = _hg_dots(True)
        lbv = lb_ref[...]

        def step(c, acc):
            rows = pl.ds(pl.multiple_of(c * HG_CHUNK, HG_CHUNK), HG_CHUNK)
            fn = lambda a, b2, c2, d2, e2: _hg_chunk(a, b2, c2, d2, e2, consts, dots)
            _, pull = jax.vjp(fn, zq_ref[rows, :], zf_ref[rows, :], zv_ref[rows, :], lbv, st_ref[c])
            dq, df, dv, dlb, _ = pull((do_ref[rows, :], g_ref[c]))
            if has_add:
                dq = dq + aq_ref[rows, :]
                dv = dv + av_ref[rows, :]
            dq_ref[rows, :] = dq
            df_ref[rows, :] = df
            dv_ref[rows, :] = dv
            return acc + dlb

        dlb_blk = _chunk_loop(nch, step, jnp.zeros((1, HG_DIM), F32))

        @pl.when(pl.program_id(1) == 0)
        def _():
            dlb_ref[...] = dlb_blk

        @pl.when(pl.program_id(1) > 0)
        def _():
            dlb_ref[...] += dlb_blk

    col = lambda off: pl.BlockSpec((ts, HG_DIM), lambda h, t: (t, off + h))
    tile, mats, _ = _hg_specs(ts, nch, lambda t: t)
    in_specs = [col(0), col(fcol), col(3 * HG_HEADS), pl.BlockSpec((None, 1, HG_DIM), lambda h, t: (h, 0, 0)),
                mats, mats, tile]
    args = [z, z, z, lb, states, gstates, dout]
    if has_add:
        in_specs += [tile, tile]
        args += [addq, addv]
    full = jax.ShapeDtypeStruct((s, D_MODEL), F32)
    return pl.pallas_call(
        body, name=name, grid=(HG_HEADS, nt),
        out_shape=(full, full, full, jax.ShapeDtypeStruct((HG_HEADS, 1, HG_DIM), F32)),
        in_specs=in_specs,
        out_specs=(tile, tile, tile, pl.BlockSpec((None, 1, HG_DIM), lambda h, t: (h, 0, 0))),
        compiler_params=_params(("parallel", "arbitrary")),
    )(*args)


def _hg_post(of, ob, z, norm_g, *, tm, name):
    s = of.shape[0]

    def body(of_ref, ob_ref, gate_ref, ng_ref, y_ref):
        gn = ng_ref[...]
        for h in range(HG_HEADS):
            ln = slice(HG_DIM * h, HG_DIM * (h + 1))
            o = of_ref[:, ln] + ob_ref[:, ln]
            r = lax.rsqrt(jnp.mean(o * o, axis=-1, keepdims=True) + LN_EPS)
            gt = gate_ref[:, ln]
            y_ref[:, ln] = (o * r * gn * gt * _sigmoid(gt)).astype(BF16)

    row = lambda i: (i, 0)
    return pl.pallas_call(
        body, name=name, grid=(s // tm,),
        out_shape=jax.ShapeDtypeStruct((s, D_MODEL), BF16),
        in_specs=[pl.BlockSpec((tm, D_MODEL), row), pl.BlockSpec((tm, D_MODEL), row),
                  pl.BlockSpec((tm, D_MODEL), lambda i: (i, 4)), pl.BlockSpec((1, HG_DIM), lambda i: (0, 0))],
        out_specs=pl.BlockSpec((tm, D_MODEL), row),
        compiler_params=_params(("parallel",)),
    )(of, ob, z, norm_g)


def _hg_post_bwd(dy, of, ob, z, norm_g, *, tm, name):
    s = of.shape[0]

    def body(dy_ref, of_ref, ob_ref, gate_ref, ng_ref, do_ref, dgate_ref, dng_ref):
        gn = ng_ref[...]
        tot = jnp.zeros((1, HG_DIM), F32)
        for h in range(HG_HEADS):
            ln = slice(HG_DIM * h, HG_DIM * (h + 1))
            d = dy_ref[:, ln].astype(F32)
            o = of_ref[:, ln] + ob_ref[:, ln]
            r = lax.rsqrt(jnp.mean(o * o, axis=-1, keepdims=True) + LN_EPS)
            ohat = o * r
            gt = gate_ref[:, ln]
            sg = _sigmoid(gt)
            don = d * gt * sg
            dgate_ref[:, ln] = (d * ohat * gn * sg * (1.0 + gt * (1.0 - sg))).astype(BF16)
            tot = tot + jnp.sum(don * ohat, axis=0, keepdims=True)
            dohat = don * gn
            do_ref[:, ln] = r * (dohat - ohat * jnp.mean(dohat * ohat, axis=-1, keepdims=True))

        @pl.when(pl.program_id(0) == 0)
        def _():
            dng_ref[...] = tot

        @pl.when(pl.program_id(0) > 0)
        def _():
            dng_ref[...] += tot

    row = lambda i: (i, 0)
    return pl.pallas_call(
        body, name=name, grid=(s // tm,),
        out_shape=(jax.ShapeDtypeStruct((s, D_MODEL), F32), jax.ShapeDtypeStruct((s, D_MODEL), BF16),
                   jax.ShapeDtypeStruct((1, HG_DIM), F32)),
        in_specs=[pl.BlockSpec((tm, D_MODEL), row), pl.BlockSpec((tm, D_MODEL), row), pl.BlockSpec((tm, D_MODEL), row),
                  pl.BlockSpec((tm, D_MODEL), lambda i: (i, 4)), pl.BlockSpec((1, HG_DIM), lambda i: (0, 0))],
        out_specs=(pl.BlockSpec((tm, D_MODEL), row), pl.BlockSpec((tm, D_MODEL), row),
                   pl.BlockSpec((1, HG_DIM), lambda i: (0, 0))),
        compiler_params=_params(("arbitrary",)),
    )(dy, of, ob, z, norm_g)


def _lb_fwd(logits, *, name):
    w = logits.shape[1]

    def body(l_ref, o_ref):
        lg = l_ref[...]
        e = jnp.exp(lg - jnp.max(lg, axis=0, keepdims=True))
        sm = e / jnp.sum(e, axis=0, keepdims=True)
        o_ref[0:1, :] = sm[1:2]
        o_ref[1:2, :] = sm[1:2] + sm[2:3] + sm[3:4]

    return pl.pallas_call(body, name=name, out_shape=jax.ShapeDtypeStruct((2, w), F32))(logits)


def _lb_bwd(logits, dlb, *, name):
    w = logits.shape[1]

    def body(l_ref, d_ref, o_ref):
        lg = l_ref[...]
        e = jnp.exp(lg - jnp.max(lg, axis=0, keepdims=True))
        sm = e / jnp.sum(e, axis=0, keepdims=True)
        d1, d3 = d_ref[0:1, :], d_ref[1:2, :]
        dot = sm[1:2] * (d1 + d3) + (sm[2:3] + sm[3:4]) * d3
        o_ref[0:1, :] = -sm[0:1] * dot
        o_ref[1:2, :] = sm[1:2] * (d1 + d3 - dot)
        o_ref[2:3, :] = sm[2:3] * (d3 - dot)
        o_ref[3:4, :] = sm[3:4] * (d3 - dot)

    return pl.pallas_call(body, name=name, out_shape=jax.ShapeDtypeStruct((4, w), F32))(logits, dlb)


def _adamw(w, g, m, v, *, tr, name):
    rows = w.shape[0]
    parts = g.ndim == 3
    c1 = 1.0 / (1.0 - ADAM_B1 ** ADAM_STEP)
    c2 = 1.0 / (1.0 - ADAM_B2 ** ADAM_STEP)

    def body(w_ref, g_ref, m_ref, v_ref, go_ref, d_ref, mo_ref, vo_ref):
        if parts:
            gg = g_ref[0].astype(F32)
            for i in range(1, N_DEV):
                gg = gg + g_ref[i].astype(F32)
        else:
            gg = g_ref[...]
        mm = ADAM_B1 * m_ref[...] + (1.0 - ADAM_B1) * gg
        vv = ADAM_B2 * v_ref[...] + (1.0 - ADAM_B2) * (gg * gg)
        go_ref[...] = gg
        mo_ref[...] = mm
        vo_ref[...] = vv
        d_ref[...] = -ADAM_LR * ((mm * c1) / (jnp.sqrt(vv * c2) + ADAM_EPS) + ADAM_WD * w_ref[...])

    tile = pl.BlockSpec((tr, D_MODEL), lambda i: (i, 0))
    gspec = pl.BlockSpec((N_DEV, tr, D_MODEL), lambda i: (0, i, 0)) if parts else tile
    out = jax.ShapeDtypeStruct((rows, D_MODEL), F32)
    return pl.pallas_call(
        body, name=name, grid=(rows // tr,),
        out_shape=(out, out, out, out),
        in_specs=[tile, gspec, tile, tile], out_specs=(tile, tile, tile, tile),
        compiler_params=_params(("parallel",)),
    )(w, g, m, v)


def _sum8(parts, *, name):
    def body(p_ref, o_ref):
        tot = p_ref[0]
        for i in range(1, N_DEV):
            tot = tot + p_ref[i]
        o_ref[...] = tot

    return pl.pallas_call(body, name=name, out_shape=jax.ShapeDtypeStruct(parts.shape[1:], parts.dtype))(parts)


def _rows(a):
    return a.reshape(-1, D_MODEL)


def _pack_local(shards):
    return jnp.concatenate([_rows(shards[n]) for n, _ in BIG], axis=0)


def _unpack_local(packed, like):
    out, r = {}, 0
    for n, _ in BIG:
        k = like[n].size // D_MODEL
        out[n] = packed[r:r + k].reshape(like[n].shape)
        r += k
    return out


def _unpack_gathered(gathered, like):
    out, r = {}, 0
    for n, ax in BIG:
        shp = like[n].shape
        k = like[n].size // D_MODEL
        t = gathered[:, r:r + k].reshape((N_DEV,) + shp)
        t = jnp.moveaxis(t, 0, ax)
        out[n] = t.reshape(shp[:ax] + (N_DEV * shp[ax],) + shp[ax + 1:])
        r += k
    return out


def _pack_full(grads, like):
    cols = []
    for n, ax in BIG:
        shp = like[n].shape
        t = grads[n].reshape(shp[:ax] + (N_DEV, shp[ax]) + shp[ax + 1:])
        t = jnp.moveaxis(t, ax, 0)
        cols.append(t.reshape(N_DEV, -1, D_MODEL).astype(BF16))
    return jnp.concatenate(cols, axis=1)


SMALL_ROWS = 24


def _pad_row(a):
    flat = a.reshape(1, -1)
    return jnp.pad(flat, ((0, 0), (0, D_MODEL - flat.shape[1])))


def _tile(n, pref):
    return min(n, pref)


def kernel(x, p, att_w_qkv, att_sink, att_w_o, hgrn_w_in, hgrn_lb_logits, hgrn_norm_g, hgrn_w_o, ln_mix_g, ln_mix_b, ffn_w_in, ffn_w_out, ln_ffn_g, ln_ffn_b, ple_w_gate, ple_w_proj, loss_target, m_att_w_qkv, m_att_sink, m_att_w_o, m_hgrn_w_in, m_hgrn_lb_logits, m_hgrn_norm_g, m_hgrn_w_o, m_ln_mix_g, m_ln_mix_b, m_ffn_w_in, m_ffn_w_out, m_ln_ffn_g, m_ln_ffn_b, m_ple_w_gate, m_ple_w_proj, v_att_w_qkv, v_att_sink, v_att_w_o, v_hgrn_w_in, v_hgrn_lb_logits, v_hgrn_norm_g, v_hgrn_w_o, v_ln_mix_g, v_ln_mix_b, v_ffn_w_in, v_ffn_w_out, v_ln_ffn_g, v_ln_ffn_b, v_ple_w_gate, v_ple_w_proj):
    names = ["att_w_qkv", "att_sink", "att_w_o", "hgrn_w_in", "hgrn_lb_logits", "hgrn_norm_g", "hgrn_w_o", "ln_mix_g",
             "ln_mix_b", "ffn_w_in", "ffn_w_out", "ln_ffn_g", "ln_ffn_b", "ple_w_gate", "ple_w_proj"]
    w = dict(zip(names, (att_w_qkv, att_sink, att_w_o, hgrn_w_in, hgrn_lb_logits, hgrn_norm_g, hgrn_w_o, ln_mix_g,
                         ln_mix_b, ffn_w_in, ffn_w_out, ln_ffn_g, ln_ffn_b, ple_w_gate, ple_w_proj)))
    mom = dict(zip(names, (m_att_w_qkv, m_att_sink, m_att_w_o, m_hgrn_w_in, m_hgrn_lb_logits, m_hgrn_norm_g, m_hgrn_w_o,
                           m_ln_mix_g, m_ln_mix_b, m_ffn_w_in, m_ffn_w_out, m_ln_ffn_g, m_ln_ffn_b, m_ple_w_gate,
                           m_ple_w_proj)))
    var = dict(zip(names, (v_att_w_qkv, v_att_sink, v_att_w_o, v_hgrn_w_in, v_hgrn_lb_logits, v_hgrn_norm_g, v_hgrn_w_o,
                           v_ln_mix_g, v_ln_mix_b, v_ffn_w_in, v_ffn_w_out, v_ln_ffn_g, v_ln_ffn_b, v_ple_w_gate,
                           v_ple_w_proj)))
    s = x.shape[1]
    me = 4 * lax.axis_index("x") + 2 * lax.axis_index("y") + lax.axis_index("c")
    tm = _tile(s, 512)
    tbig = _tile(s, 1024)
    ts = _tile(s // 2, 512)
    x0 = x.reshape(s, D_MODEL)
    target = loss_target.reshape(s, D_MODEL)
    pl_in = p.reshape(DEPTH, s, PLE_DIM)

    big_like = {n: w[n] for n, _ in BIG}
    w_rows = _pack_local({n: w[n] for n, _ in BIG})
    full = _unpack_gathered(_gather(w_rows.astype(BF16), name="gather_weights"), big_like)
    lb_rows = jnp.pad(hgrn_lb_logits.reshape(8, HG_DIM), ((0, 0), (0, D_MODEL - HG_DIM)))
    lb_all = _gather(lb_rows, name="gather_lb")[:, :, :HG_DIM]
    logits_full = jnp.moveaxis(lb_all, 0, 1).reshape(DEPTH, 2 * D_MODEL)
    lb = _lb_fwd(logits_full, name="lb_fwd")
    cos, sin = _rope_tables(s)

    saved = []
    xf, xb = x0, x0
    for i in range(DEPTH):
        j = i // 2
        sv = {"x": xf, "xb": xb}
        if i % 2 == 0:
            z = _mm(xb, full["att_w_qkv"][j], n=D_MODEL + 2 * KV_DIM, tm=tbig, tn=512, tk=D_MODEL, name="att_in")
            sink = w["att_sink"][j]
            o = _att_fwd(z, sink, cos, sin, name="att_fwd")
            w_o = full["att_w_o"][j]
        else:
            z = _mm(xb, full["hgrn_w_in"][j], n=5 * D_MODEL, tm=tbig, tn=1024, tk=D_MODEL, name="hgrn_in")
            lbl = lb[j].reshape(2, HG_HEADS, 1, HG_DIM)
            in_f, qs_f, upd_f, dec_f = _hg_prep(z, lbl[0], rev=False, ts=ts, name="hgrn_prep")
            in_b, qs_b, upd_b, dec_b = _hg_prep(z, lbl[1], rev=True, ts=ts, name="hgrn_prep_rev")
            of, st_f = _hg_scan(in_f, qs_f, upd_f, dec_f, rev=False, ts=ts, name="hgrn_scan")
            ob, st_b = _hg_scan(in_b, qs_b, upd_b, dec_b, rev=True, ts=ts, name="hgrn_scan_rev")
            o = _hg_post(of, ob, z, w["hgrn_norm_g"][j].reshape(1, HG_DIM), tm=tm, name="hgrn_post")
            w_o = full["hgrn_w_o"][j]
            sv.update(of=of, ob=ob, st_f=st_f, st_b=st_b, lbl=lbl, qs_f=qs_f, qs_b=qs_b, dec_f=dec_f, dec_b=dec_b)
        sv.update(z=z, o=o)
        pre1, x1, x1b = _proj_ln(o, w_o, xf, w["ln_mix_g"][i:i + 1], w["ln_mix_b"][i:i + 1], tm=tm, name="mix_out_ln")
        gg, uu, act = _ffn_in(x1b, full["ffn_w_in"][i], tm=tm, tn=FF_TILE, name="ffn_in")
        pre2, x2, x2b = _proj_ln(act, full["ffn_w_out"][i], x1, w["ln_ffn_g"][i:i + 1], w["ln_ffn_b"][i:i + 1], tm=tm,
                                 name="ffn_out_ln")
        xf, xb = _ple_fwd(x2, x2b, pl_in[i], full["ple_w_gate"][i], full["ple_w_proj"][i], tm=tm, name="ple_fwd")
        sv.update(pre1=pre1, x1=x1, x1b=x1b, g=gg, u=uu, act=act, pre2=pre2, x2b=x2b)
        saved.append(sv)

    dx, loss_blk = _loss_head(xf, target, tm=tm, name="loss_head")
    loss = lax.psum(loss_blk[0, 0], AXES)

    gfull = {n: [None] * w[n].shape[0] for n, _ in BIG}
    small = {n: [None] * DEPTH for n in ("ln_mix_g", "ln_mix_b", "ln_ffn_g", "ln_ffn_b")}
    dlb_rows = [None] * 4
    dnorm, dsink = [None] * 2, [None] * 2
    mmw = functools.partial(_mm, ta=True, tk=tbig, out_dtype=BF16)
    for i in reversed(range(DEPTH)):
        j = i // 2
        sv = saved[i]
        dx2, da, dpp = _ple_bwd(dx, sv["x2b"], pl_in[i], full["ple_w_gate"][i], full["ple_w_proj"][i], tm=tm,
                                name="ple_bwd")
        gfull["ple_w_gate"][i] = mmw(sv["x2b"], da, n=D_MODEL, tm=D_MODEL, tn=D_MODEL, name="dw_ple_gate")
        gfull["ple_w_proj"][i] = mmw(pl_in[i], dpp, n=D_MODEL, tm=PLE_DIM, tn=D_MODEL, name="dw_ple_proj")
        dy2, dy2b, small["ln_ffn_g"][i], small["ln_ffn_b"][i] = _ln_bwd(dx2, sv["pre2"], w["ln_ffn_g"][i:i + 1], tm=tm,
                                                                         name="ln_bwd")
        dg, du = _ffn_bwd_act(dy2b, full["ffn_w_out"][i], sv["g"], sv["u"], tm=tm, tn=FF_TILE, name="ffn_bwd_act")
        gfull["ffn_w_out"][i] = mmw(sv["act"], dy2b, n=D_MODEL, tm=FF_TILE, tn=D_MODEL, name="dw_ffn_out")
        t = _mm(dg, full["ffn_w_in"][i], n=D_MODEL, tm=tbig, tn=D_MODEL, tk=FF_TILE, tb=True, add=dy2, add_scale=ALPHA,
                name="ffn_bwd_x")
        dx1 = _mm(du, full["ffn_w_in"][i], n=D_MODEL, tm=tbig, tn=D_MODEL, tk=FF_TILE, tb=True, add=t,
                  bk_off=D_FF // FF_TILE, name="ffn_bwd_x_u")
        gfull["ffn_w_in"][i] = jnp.concatenate(
            [mmw(sv["x1b"], dg, n=D_FF, tm=D_MODEL, tn=FF_TILE, name="dw_ffn_in"),
             mmw(sv["x1b"], du, n=D_FF, tm=D_MODEL, tn=FF_TILE, name="dw_ffn_in")], axis=1)
        dy1, dy1b, small["ln_mix_g"][i], small["ln_mix_b"][i] = _ln_bwd(dx1, sv["pre1"], w["ln_mix_g"][i:i + 1], tm=tm,
                                                                         name="ln_bwd")
        if i % 2 == 0:
            w_o, w_in, n_in = full["att_w_o"][j], full["att_w_qkv"][j], D_MODEL + 2 * KV_DIM
        else:
            w_o, w_in, n_in = full["hgrn_w_o"][j], full["hgrn_w_in"][j], 5 * D_MODEL
        do = _mm(dy1b, w_o, n=D_MODEL, tm=tbig, tn=D_MODEL, tk=D_MODEL, tb=True, out_dtype=BF16, name="mix_out_bwd")
        g_o = mmw(sv["o"], dy1b, n=D_MODEL, tm=D_MODEL, tn=D_MODEL, name="dw_mix_out")
        if i % 2 == 0:
            gfull["att_w_o"][j] = g_o
            dzq, part, dsk = _att_bwd(sv["z"], do, w["att_sink"][j], cos, sin, name="att_bwd")
            dzkv = _att_bwd_kv(part, cos, sin, name="att_bwd_kv")
            dz = jnp.concatenate([dzq, dzkv], axis=1)
            dsink[j] = dsk[:, 0]
        else:
            gfull["hgrn_w_o"][j] = g_o
            dsum, dgate, dnorm[j] = _hg_post_bwd(do, sv["of"], sv["ob"], sv["z"], w["hgrn_norm_g"][j].reshape(1, HG_DIM),
                                                 tm=tm, name="hgrn_post_bwd")
            g_f = _hg_scan_bwd(sv["qs_f"], dsum, sv["dec_f"], rev=False, ts=ts, name="hgrn_scan_bwd")
            g_b = _hg_scan_bwd(sv["qs_b"], dsum, sv["dec_b"], rev=True, ts=ts, name="hgrn_scan_bwd_rev")
            dq1, df1, dv1, dlb1 = _hg_bwd(sv["z"], sv["lbl"][0], sv["st_f"], g_f, dsum, None, None, rev=False, ts=ts,
                                          name="hgrn_bwd")
            dq2, df2, dv2, dlb2 = _hg_bwd(sv["z"], sv["lbl"][1], sv["st_b"], g_b, dsum, dq1, dv1, rev=True, ts=ts,
                                          name="hgrn_bwd_rev")
            dz = jnp.concatenate([dq2.astype(BF16), df1.astype(BF16), df2.astype(BF16), dv2.astype(BF16), dgate], axis=1)
            dlb_rows[2 * j] = dlb1.reshape(1, D_MODEL)
            dlb_rows[2 * j + 1] = dlb2.reshape(1, D_MODEL)
        dx = _mm(dz, w_in, n=D_MODEL, tm=tbig, tn=D_MODEL, tk=512, tb=True, add=dy1, add_scale=ALPHA, name="mix_in_bwd")
        g_in = mmw(sv["xb"], dz, n=n_in, tm=D_MODEL, tn=512, name="dw_mix_in")
        gfull["att_w_qkv" if i % 2 == 0 else "hgrn_w_in"][j] = g_in
    grad_x = dx.reshape(x.shape)

    gstack = {n: jnp.stack(gfull[n]) for n, _ in BIG}
    recv = _exchange(_pack_full(gstack, big_like), name="exchange_grads")
    m_rows = _pack_local({n: mom[n] for n, _ in BIG})
    v_rows = _pack_local({n: var[n] for n, _ in BIG})
    outs = _adamw(w_rows, recv, m_rows, v_rows, tr=320, name="adamw_big")
    big_out = [_unpack_local(o_, big_like) for o_ in outs]

    small_rows = jnp.concatenate(
        [jnp.concatenate(small[n], axis=0) for n in ("ln_mix_g", "ln_mix_b", "ln_ffn_g", "ln_ffn_b")] + dlb_rows
        + [_pad_row(jnp.stack(dnorm)), _pad_row(jnp.stack(dsink)), jnp.zeros((2, D_MODEL), F32)], axis=0)
    small_all = _gather(small_rows, name="gather_small")
    lbw, lbm, lbv = (t.reshape(4, 2 * HG_DIM) for t in (hgrn_lb_logits, mom["hgrn_lb_logits"], var["hgrn_lb_logits"]))
    summed = _sum8(small_all, name="sum_small")
    dlb_mine = lax.dynamic_slice_in_dim(summed[16:20].reshape(2, 2, HG_HEADS, HG_DIM), me, 1, axis=2)
    dlogits = _lb_bwd(lbw, dlb_mine.reshape(2, 2 * HG_DIM), name="lb_bwd")

    def small_pack(ln4, lbt, ng, sk):
        return jnp.concatenate([ln4[n] for n in ("ln_mix_g", "ln_mix_b", "ln_ffn_g", "ln_ffn_b")]
                               + [_pad_row(lbt), _pad_row(ng), _pad_row(sk), jnp.zeros((5, D_MODEL), F32)], axis=0)

    g_small = jnp.concatenate([summed[:16], _pad_row(dlogits), summed[20:22], jnp.zeros((5, D_MODEL), F32)], axis=0)
    souts = _adamw(small_pack(w, lbw, w["hgrn_norm_g"], w["att_sink"]), g_small,
                   small_pack(mom, lbm, mom["hgrn_norm_g"], mom["att_sink"]),
                   small_pack(var, lbv, var["hgrn_norm_g"], var["att_sink"]), tr=SMALL_ROWS, name="adamw_small")

    def small_unpack(t):
        out = {n: t[4 * k:4 * k + 4] for k, n in enumerate(("ln_mix_g", "ln_mix_b", "ln_ffn_g", "ln_ffn_b"))}
        out["hgrn_lb_logits"] = t[16].reshape(hgrn_lb_logits.shape)
        out["hgrn_norm_g"] = t[17, :2 * HG_DIM].reshape(hgrn_norm_g.shape)
        out["att_sink"] = t[18, :2 * N_Q_HEADS].reshape(att_sink.shape)
        return out

    result = [loss, grad_x]
    for big_t, small_t in zip(big_out, souts):
        merged = dict(big_t)
        merged.update(small_unpack(small_t))
        result += [merged[n] for n in names]
    return tuple(result)
```

```python
import functools

import jax
import jax.numpy as jnp
from jax import lax
from jax.experimental import pallas as pl
from jax.experimental.pallas import tpu as pltpu

F32 = jnp.float32
BF16 = jnp.bfloat16

D_MODEL = 1024
DEPTH = 4
HEAD_DIM = 64
N_Q_HEADS = 16
N_KV_HEADS = 4
GROUP = 4
KV_DIM = 256
ATT_BLOCK = 128
ROPE_DIM = 16
ROPE_THETA = 500000.0
HG_HEADS = 8
HG_DIM = 128
HG_CHUNK = 64
HG_SUB = 16
HG_UNROLL = 8
D_FF = 2816
FF_TILE = 1408
PLE_DIM = 256
ALPHA = (2 * DEPTH) ** 0.25
LN_EPS = 1e-5
ADAM_LR, ADAM_B1, ADAM_B2, ADAM_EPS, ADAM_WD, ADAM_STEP = 0.001, 0.9, 0.999, 1e-08, 0.01, 10

N_DEV = 8
LANES = 128
VMEM_LIMIT = 52 * 1024 * 1024
NEG = -1e30
MESH = pl.DeviceIdType.MESH
AXES = ("x", "y", "c")

BIG = (("att_w_qkv", 2), ("att_w_o", 1), ("hgrn_w_in", 2), ("hgrn_w_o", 1), ("ffn_w_in", 2), ("ffn_w_out", 1),
       ("ple_w_gate", 1), ("ple_w_proj", 2))


def _params(sem=None, vmem=VMEM_LIMIT):
    return pltpu.CompilerParams(dimension_semantics=sem, vmem_limit_bytes=vmem)


def _sigmoid(x):
    return jax.nn.sigmoid(x)


def _exchange(src, *, name):
    def body(src_ref, out_ref, send_sems, recv_sems, local_sem):
        x, y, c = lax.axis_index("x"), lax.axis_index("y"), lax.axis_index("c")
        me = 4 * x + 2 * y + c
        local = pltpu.make_async_copy(src_ref.at[me], out_ref.at[me], local_sem)
        local.start()
        copies = []
        for k in range(1, N_DEV):
            px, py, pc = x ^ (k >> 2), y ^ ((k >> 1) & 1), c ^ (k & 1)
            peer = 4 * px + 2 * py + pc
            copies.append((peer, pltpu.make_async_remote_copy(
                src_ref=src_ref.at[peer], dst_ref=out_ref.at[me], send_sem=send_sems.at[k], recv_sem=recv_sems.at[k],
                device_id=(px, py, pc), device_id_type=MESH)))
        for _, cp in copies:
            cp.start()
        for k, (peer, cp) in enumerate(copies, start=1):
            cp.wait_send()
            pltpu.make_async_remote_copy(
                src_ref=src_ref.at[peer], dst_ref=out_ref.at[peer], send_sem=send_sems.at[k], recv_sem=recv_sems.at[k],
                device_id=(x, y, c), device_id_type=MESH).wait_recv()
        local.wait()

    return pl.pallas_call(
        body, name=name,
        out_shape=jax.ShapeDtypeStruct(src.shape, src.dtype),
        in_specs=[pl.BlockSpec(memory_space=pltpu.HBM)],
        out_specs=pl.BlockSpec(memory_space=pltpu.HBM),
        scratch_shapes=[pltpu.SemaphoreType.DMA((N_DEV,)), pltpu.SemaphoreType.DMA((N_DEV,)), pltpu.SemaphoreType.DMA],
    )(src)


def _gather(src, *, name):
    def body(src_ref, out_ref, send_sems, recv_sems, local_sem):
        x, y, c = lax.axis_index("x"), lax.axis_index("y"), lax.axis_index("c")
        sibling = (x, y, 1 - c)
        chips = [(1 - x, y), (x, 1 - y), (1 - x, 1 - y)]

        def rows(px, py, pc):
            return out_ref.at[4 * px + 2 * py + pc]

        def copy(k, block, to, from_src=False):
            return pltpu.make_async_remote_copy(
                src_ref=src_ref if from_src else rows(*block), dst_ref=rows(*block), send_sem=send_sems.at[k],
                recv_sem=recv_sems.at[k], device_id=to, device_id_type=MESH)

        me = (x, y, c)
        mine = pltpu.make_async_copy(src_ref, rows(*me), local_sem)
        mine.start()
        first = [copy(0, me, sibling, from_src=True)]
        first += [copy(1 + j, me, (*chip, c), from_src=True) for j, chip in enumerate(chips)]
        for cp in first:
            cp.start()
        passed = [copy(4 + j, (*chip, c), sibling) for j, chip in enumerate(chips)]
        for j, chip in enumerate(chips):
            copy(1 + j, (*chip, c), me).wait_recv()
            passed[j].start()
        copy(0, sibling, me).wait_recv()
        for j, chip in enumerate(chips):
            copy(4 + j, (*chip, 1 - c), me).wait_recv()
        for cp in first + passed:
            cp.wait_send()
        mine.wait()

    return pl.pallas_call(
        body, name=name,
        out_shape=jax.ShapeDtypeStruct((N_DEV,) + tuple(src.shape), src.dtype),
        in_specs=[pl.BlockSpec(memory_space=pltpu.HBM)],
        out_specs=pl.BlockSpec(memory_space=pltpu.HBM),
        scratch_shapes=[pltpu.SemaphoreType.DMA((7,)), pltpu.SemaphoreType.DMA((7,)), pltpu.SemaphoreType.DMA],
    )(src)


def _mm(a, b, *, n, tm, tn, tk, ta=False, tb=False, out_dtype=F32, add=None, add_scale=1.0, bk_off=0, name):
    m, kdim = (a.shape[1], a.shape[0]) if ta else a.shape
    nk = kdim // tk
    dims = (((0 if ta else 1,), (1 if tb else 0,)), ((), ()))

    def body(*refs):
        if add is None:
            a_ref, b_ref, o_ref = refs[:3]
            add_ref = None
        else:
            a_ref, b_ref, add_ref, o_ref = refs[:4]
        acc_ref = refs[-1] if nk > 1 else None
        part = lax.dot_general(a_ref[...].astype(BF16), b_ref[...].astype(BF16), dims, preferred_element_type=F32)

        def finish(r):
            if add_ref is not None:
                r = r + add_scale * add_ref[...]
            o_ref[...] = r.astype(out_dtype)

        if nk == 1:
            finish(part)
        else:
            k = pl.program_id(2)

            @pl.when(k == 0)
            def _():
                acc_ref[...] = part

            @pl.when(k > 0)
            def _():
                acc_ref[...] += part

            @pl.when(k == nk - 1)
            def _():
                finish(acc_ref[...])

    a_spec = pl.BlockSpec((tk, tm), lambda i, j, k: (k, i)) if ta else pl.BlockSpec((tm, tk), lambda i, j, k: (i, k))
    b_spec = (pl.BlockSpec((tn, tk), lambda i, j, k: (j, k + bk_off)) if tb
              else pl.BlockSpec((tk, tn), lambda i, j, k: (k + bk_off, j)))
    in_specs, args = [a_spec, b_spec], [a, b]
    if add is not None:
        in_specs.append(pl.BlockSpec((tm, tn), lambda i, j, k: (i, j)))
        args.append(add)
    return pl.pallas_call(
        body, name=name, grid=(m // tm, n // tn, nk),
        out_shape=jax.ShapeDtypeStruct((m, n), out_dtype),
        in_specs=in_specs, out_specs=pl.BlockSpec((tm, tn), lambda i, j, k: (i, j)),
        scratch_shapes=[pltpu.VMEM((tm, tn), F32)] if nk > 1 else [],
        compiler_params=_params(("parallel", "parallel", "arbitrary")),
    )(*args)


def _layer_norm_rows(y, g, b):
    mu = jnp.mean(y, axis=-1, keepdims=True)
    yc = y - mu
    var = jnp.mean(yc * yc, axis=-1, keepdims=True)
    return yc * lax.rsqrt(var + LN_EPS) * g + b


def _proj_ln(a, w, res, g, b, *, tm, name):
    s, kdim = a.shape

    def body(a_ref, w_ref, res_ref, g_ref, b_ref, pre_ref, o_ref, obf_ref):
        h = jnp.dot(a_ref[...], w_ref[...], preferred_element_type=F32)
        pre = ALPHA * res_ref[...] + h
        out = _layer_norm_rows(pre, g_ref[...], b_ref[...])
        pre_ref[...] = pre
        o_ref[...] = out
        obf_ref[...] = out.astype(BF16)

    row = lambda i: (i, 0)
    fix = lambda i: (0, 0)
    return pl.pallas_call(
        body, name=name, grid=(s // tm,),
        out_shape=(jax.ShapeDtypeStruct((s, D_MODEL), F32), jax.ShapeDtypeStruct((s, D_MODEL), F32),
                   jax.ShapeDtypeStruct((s, D_MODEL), BF16)),
        in_specs=[pl.BlockSpec((tm, kdim), row), pl.BlockSpec((kdim, D_MODEL), fix), pl.BlockSpec((tm, D_MODEL), row),
                  pl.BlockSpec((1, D_MODEL), fix), pl.BlockSpec((1, D_MODEL), fix)],
        out_specs=(pl.BlockSpec((tm, D_MODEL), row),) * 3,
        compiler_params=_params(("parallel",)),
    )(a, w, res, g, b)


def _ln_bwd(dout, pre, g, *, tm, name):
    s = dout.shape[0]

    def body(do_ref, pre_ref, g_ref, dy_ref, dybf_ref, dg_ref, db_ref):
        do = do_ref[...]
        y = pre_ref[...]
        mu = jnp.mean(y, axis=-1, keepdims=True)
        yc = y - mu
        var = jnp.mean(yc * yc, axis=-1, keepdims=True)
        rstd = lax.rsqrt(var + LN_EPS)
        xhat = yc * rstd
        dxhat = do * g_ref[...]
        dy = rstd * (dxhat - jnp.mean(dxhat, axis=-1, keepdims=True)
                     - xhat * jnp.mean(dxhat * xhat, axis=-1, keepdims=True))
        dy_ref[...] = dy
        dybf_ref[...] = dy.astype(BF16)
        pg = jnp.sum(do * xhat, axis=0, keepdims=True)
        pb = jnp.sum(do, axis=0, keepdims=True)

        @pl.when(pl.program_id(0) == 0)
        def _():
            dg_ref[...] = pg
            db_ref[...] = pb

        @pl.when(pl.program_id(0) > 0)
        def _():
            dg_ref[...] += pg
            db_ref[...] += pb

    row = lambda i: (i, 0)
    fix = lambda i: (0, 0)
    return pl.pallas_call(
        body, name=name, grid=(s // tm,),
        out_shape=(jax.ShapeDtypeStruct((s, D_MODEL), F32), jax.ShapeDtypeStruct((s, D_MODEL), BF16),
                   jax.ShapeDtypeStruct((1, D_MODEL), F32), jax.ShapeDtypeStruct((1, D_MODEL), F32)),
        in_specs=[pl.BlockSpec((tm, D_MODEL), row), pl.BlockSpec((tm, D_MODEL), row), pl.BlockSpec((1, D_MODEL), fix)],
        out_specs=(pl.BlockSpec((tm, D_MODEL), row), pl.BlockSpec((tm, D_MODEL), row),
                   pl.BlockSpec((1, D_MODEL), fix), pl.BlockSpec((1, D_MODEL), fix)),
        compiler_params=_params(("arbitrary",)),
    )(dout, pre, g)


def _ffn_in(xbf, w, *, tm, tn, name):
    s = xbf.shape[0]
    nj = D_FF // tn

    def body(x_ref, wg_ref, wu_ref, g_ref, u_ref, act_ref):
        xv = x_ref[...]
        gg = jnp.dot(xv, wg_ref[...], preferred_element_type=F32)
        uu = jnp.dot(xv, wu_ref[...], preferred_element_type=F32)
        g_ref[...] = gg.astype(BF16)
        u_ref[...] = uu.astype(BF16)
        act_ref[...] = (gg * _sigmoid(gg) * uu).astype(BF16)

    out = jax.ShapeDtypeStruct((s, D_FF), BF16)
    tile = pl.BlockSpec((tm, tn), lambda j, i: (i, j))
    return pl.pallas_call(
        body, name=name, grid=(nj, s // tm),
        out_shape=(out, out, out),
        in_specs=[pl.BlockSpec((tm, D_MODEL), lambda j, i: (i, 0)), pl.BlockSpec((D_MODEL, tn), lambda j, i: (0, j)),
                  pl.BlockSpec((D_MODEL, tn), lambda j, i: (0, j + nj))],
        out_specs=(tile, tile, tile),
        compiler_params=_params(("parallel", "parallel")),
    )(xbf, w, w)


def _ffn_bwd_act(dybf, w_out, g, u, *, tm, tn, name):
    s = dybf.shape[0]

    def body(dy_ref, w_ref, g_ref, u_ref, dg_ref, du_ref):
        dact = lax.dot_general(dy_ref[...], w_ref[...], (((1,), (1,)), ((), ())), preferred_element_type=F32)
        gg = g_ref[...].astype(F32)
        uu = u_ref[...].astype(F32)
        sg = _sigmoid(gg)
        dg_ref[...] = (dact * uu * sg * (1.0 + gg * (1.0 - sg))).astype(BF16)
        du_ref[...] = (dact * gg * sg).astype(BF16)

    out = jax.ShapeDtypeStruct((s, D_FF), BF16)
    tile = pl.BlockSpec((tm, tn), lambda j, i: (i, j))
    return pl.pallas_call(
        body, name=name, grid=(D_FF // tn, s // tm),
        out_shape=(out, out),
        in_specs=[pl.BlockSpec((tm, D_MODEL), lambda j, i: (i, 0)), pl.BlockSpec((tn, D_MODEL), lambda j, i: (j, 0)),
                  tile, tile],
        out_specs=(tile, tile),
        compiler_params=_params(("parallel", "parallel")),
    )(dybf, w_out, g, u)


def _ple_fwd(x2, x2bf, p, w_gate, w_proj, *, tm, name):
    s = x2.shape[0]

    def body(x_ref, xbf_ref, p_ref, wg_ref, wp_ref, o_ref, obf_ref):
        a = jnp.dot(xbf_ref[...], wg_ref[...], preferred_element_type=F32)
        pp = jnp.dot(p_ref[...].astype(BF16), wp_ref[...], preferred_element_type=F32)
        out = x_ref[...] + _sigmoid(a) * pp
        o_ref[...] = out
        obf_ref[...] = out.astype(BF16)

    row = lambda i: (i, 0)
    fix = lambda i: (0, 0)
    return pl.pallas_call(
        body, name=name, grid=(s // tm,),
        out_shape=(jax.ShapeDtypeStruct((s, D_MODEL), F32), jax.ShapeDtypeStruct((s, D_MODEL), BF16)),
        in_specs=[pl.BlockSpec((tm, D_MODEL), row), pl.BlockSpec((tm, D_MODEL), row), pl.BlockSpec((tm, PLE_DIM), row),
                  pl.BlockSpec((D_MODEL, D_MODEL), fix), pl.BlockSpec((PLE_DIM, D_MODEL), fix)],
        out_specs=(pl.BlockSpec((tm, D_MODEL), row), pl.BlockSpec((tm, D_MODEL), row)),
        compiler_params=_params(("parallel",)),
    )(x2, x2bf, p, w_gate, w_proj)


def _ple_bwd(dx3, x2bf, p, w_gate, w_proj, *, tm, name):
    s = dx3.shape[0]

    def body(d_ref, xbf_ref, p_ref, wg_ref, wp_ref, dx_ref, da_ref, dpp_ref):
        d = d_ref[...]
        a = jnp.dot(xbf_ref[...], wg_ref[...], preferred_element_type=F32)
        pp = jnp.dot(p_ref[...].astype(BF16), wp_ref[...], preferred_element_type=F32)
        sg = _sigmoid(a)
        da = (d * pp * sg * (1.0 - sg)).astype(BF16)
        da_ref[...] = da
        dpp_ref[...] = (d * sg).astype(BF16)
        dx_ref[...] = d + lax.dot_general(da, wg_ref[...], (((1,), (1,)), ((), ())), preferred_element_type=F32)

    row = lambda i: (i, 0)
    fix = lambda i: (0, 0)
    return pl.pallas_call(
        body, name=name, grid=(s // tm,),
        out_shape=(jax.ShapeDtypeStruct((s, D_MODEL), F32), jax.ShapeDtypeStruct((s, D_MODEL), BF16),
                   jax.ShapeDtypeStruct((s, D_MODEL), BF16)),
        in_specs=[pl.BlockSpec((tm, D_MODEL), row), pl.BlockSpec((tm, D_MODEL), row), pl.BlockSpec((tm, PLE_DIM), row),
                  pl.BlockSpec((D_MODEL, D_MODEL), fix), pl.BlockSpec((PLE_DIM, D_MODEL), fix)],
        out_specs=(pl.BlockSpec((tm, D_MODEL), row),) * 3,
        compiler_params=_params(("parallel",)),
    )(dx3, x2bf, p, w_gate, w_proj)


def _loss_head(y, target, *, tm, name):
    s = y.shape[0]

    def body(y_ref, t_ref, dy_ref, loss_ref, acc_ref):
        err = y_ref[...] - t_ref[...]
        dy_ref[...] = err * (1.0 / D_MODEL)
        part = jnp.sum(err * err, axis=0, keepdims=True)

        @pl.when(pl.program_id(0) == 0)
        def _():
            acc_ref[...] = part

        @pl.when(pl.program_id(0) > 0)
        def _():
            acc_ref[...] += part

        @pl.when(pl.program_id(0) == pl.num_programs(0) - 1)
        def _():
            tot = jnp.sum(acc_ref[...], axis=1, keepdims=True) * (0.5 / D_MODEL)
            loss_ref[...] = jnp.broadcast_to(tot, (8, LANES))

    row = lambda i: (i, 0)
    return pl.pallas_call(
        body, name=name, grid=(s // tm,),
        out_shape=(jax.ShapeDtypeStruct((s, D_MODEL), F32), jax.ShapeDtypeStruct((8, LANES), F32)),
        in_specs=[pl.BlockSpec((tm, D_MODEL), row), pl.BlockSpec((tm, D_MODEL), row)],
        out_specs=(pl.BlockSpec((tm, D_MODEL), row), pl.BlockSpec((8, LANES), lambda i: (0, 0))),
        scratch_shapes=[pltpu.VMEM((1, D_MODEL), F32)],
        compiler_params=_params(("arbitrary",)),
    )(y, target)


def _rope_tables(s):
    inv = ROPE_THETA ** (-jnp.arange(0, ROPE_DIM, 2, dtype=F32) / ROPE_DIM)
    ang = jnp.arange(s, dtype=F32)[:, None] * inv[None, :]
    cos, sin = jnp.cos(ang), jnp.sin(ang)
    ones = jnp.ones((s, HEAD_DIM - ROPE_DIM), F32)
    c_head = jnp.concatenate([cos, cos, ones], axis=1)
    s_head = jnp.concatenate([-sin, sin, 0.0 * ones], axis=1)
    return jnp.concatenate([c_head, c_head], axis=1), jnp.concatenate([s_head, s_head], axis=1)


def _rope(v, cos, sin):
    n = v.shape[1] // LANES
    width = v.shape[1]
    cos_w = jnp.tile(cos, (1, n)) if n > 1 else cos
    sin_w = jnp.tile(sin, (1, n)) if n > 1 else sin
    dim = lax.broadcasted_iota(jnp.int32, (1, width), 1) % HEAD_DIM
    partner = jnp.where(dim < ROPE_DIM // 2, pltpu.roll(v, width - ROPE_DIM // 2, 1), pltpu.roll(v, ROPE_DIM // 2, 1))
    return v * cos_w + partner * sin_w


def _unrope(dv, cos, sin):
    n = dv.shape[1] // LANES
    width = dv.shape[1]
    cos_w = jnp.tile(cos, (1, n)) if n > 1 else cos
    sin_w = jnp.tile(sin, (1, n)) if n > 1 else sin
    t = dv * sin_w
    dim = lax.broadcasted_iota(jnp.int32, (1, width), 1) % HEAD_DIM
    partner = jnp.where(dim < ROPE_DIM // 2, pltpu.roll(t, width - ROPE_DIM // 2, 1),
                        jnp.where(dim < ROPE_DIM, pltpu.roll(t, ROPE_DIM // 2, 1), 0.0))
    return dv * cos_w + partner


def _att_mask(i, nb):
    rows = GROUP * ATT_BLOCK
    r = lax.broadcasted_iota(jnp.int32, (rows, 3 * ATT_BLOCK), 0) % ATT_BLOCK
    cidx = lax.broadcasted_iota(jnp.int32, (rows, 3 * ATT_BLOCK), 1)
    rel = r + ATT_BLOCK - cidx
    ok = (rel <= ATT_BLOCK) & (rel >= -ATT_BLOCK)
    ok = ok & ((cidx >= ATT_BLOCK) | (i > 0)) & ((cidx < 2 * ATT_BLOCK) | (i < nb - 1))
    return ok


def _half_mask(half):
    lane = lax.broadcasted_iota(jnp.int32, (1, LANES), 1)
    return (lane // HEAD_DIM) == half


def _stack_q(q, h):
    parts = []
    for gq in range(GROUP):
        n = GROUP * h + gq
        grp = q[:, LANES * (n // 2):LANES * (n // 2 + 1)]
        grp = jnp.where(_half_mask(n % 2), grp, 0.0)
        if n % 2 != h % 2:
            grp = pltpu.roll(grp, HEAD_DIM, 1)
        parts.append(grp)
    return jnp.concatenate(parts, axis=0)


def _unstack_q(stacked, h, acc):
    for gq in range(GROUP):
        n = GROUP * h + gq
        grp = stacked[ATT_BLOCK * gq:ATT_BLOCK * (gq + 1), :]
        grp = jnp.where(_half_mask(h % 2), grp, 0.0)
        if n % 2 != h % 2:
            grp = pltpu.roll(grp, HEAD_DIM, 1)
        acc[n // 2] = grp if acc[n // 2] is None else acc[n // 2] + grp
    return acc


def _sink_rows(sink_ref, h):
    rows = GROUP * ATT_BLOCK
    grp = lax.broadcasted_iota(jnp.int32, (rows, 1), 0) // ATT_BLOCK
    out = jnp.zeros((rows, 1), F32)
    for gq in range(GROUP):
        out = jnp.where(grp == gq, sink_ref[GROUP * h + gq], out)
    return out


def _att_probs(qs, kh, sink, valid):
    s = lax.dot_general(qs, kh, (((1,), (1,)), ((), ())), preferred_element_type=F32)
    s = jnp.where(valid, s, NEG)
    m = jnp.maximum(jnp.max(s, axis=-1, keepdims=True), sink)
    p = jnp.exp(s - m)
    es = jnp.exp(sink - m)
    den = jnp.sum(p, axis=-1, keepdims=True) + es
    inv = 1.0 / den
    return p * inv, es * inv


def _att_specs(nb):
    prev = lambda i: (jnp.maximum(i - 1, 0), 0)
    cur = lambda i: (i, 0)
    nxt = lambda i: (jnp.minimum(i + 1, nb - 1), 0)
    kv = lambda f: (lambda i: (f(i)[0], 2))
    tab = [pl.BlockSpec((ATT_BLOCK, LANES), f) for f in (cur, prev, cur, nxt)]
    z = [pl.BlockSpec((ATT_BLOCK, D_MODEL), cur)] + [pl.BlockSpec((ATT_BLOCK, 2 * KV_DIM), kv(f)) for f in (prev, cur, nxt)]
    return z, tab


def _att_load(zq_ref, kp_ref, kc_ref, kn_ref, cq_ref, sq_ref, cp_ref, sp_ref, cc_ref, sc_ref, cn_ref, sn_ref):
    q = (_rope(zq_ref[...], cq_ref[...], sq_ref[...]) * (HEAD_DIM ** -0.5))
    ks, vs = [], []
    for ref, c_ref, s_ref in ((kp_ref, cp_ref, sp_ref), (kc_ref, cc_ref, sc_ref), (kn_ref, cn_ref, sn_ref)):
        kvb = ref[...]
        ks.append(_rope(kvb[:, :KV_DIM], c_ref[...], s_ref[...]))
        vs.append(kvb[:, KV_DIM:])
    return q, jnp.concatenate(ks, axis=0).astype(BF16), jnp.concatenate(vs, axis=0).astype(BF16)


def _att_fwd(z, sink, cos, sin, *, name):
    s = z.shape[0]
    nb = s // ATT_BLOCK

    def body(zq_ref, kp_ref, kc_ref, kn_ref, cq_ref, cp_ref, cc_ref, cn_ref, sq_ref, sp_ref, sc_ref, sn_ref, sink_ref,
             o_ref):
        i = pl.program_id(0)
        q, k, v = _att_load(zq_ref, kp_ref, kc_ref, kn_ref, cq_ref, sq_ref, cp_ref, sp_ref, cc_ref, sc_ref, cn_ref, sn_ref)
        valid = _att_mask(i, nb)
        acc = [None] * (N_Q_HEADS // 2)
        for h in range(N_KV_HEADS):
            lanes = slice(LANES * (h // 2), LANES * (h // 2 + 1))
            qs = _stack_q(q, h).astype(BF16)
            prob, _ = _att_probs(qs, k[:, lanes], _sink_rows(sink_ref, h), valid)
            oh = jnp.dot(prob.astype(BF16), v[:, lanes], preferred_element_type=F32)
            acc = _unstack_q(oh, h, acc)
        o_ref[...] = jnp.concatenate(acc, axis=1).astype(BF16)

    zspecs, tab = _att_specs(nb)
    return pl.pallas_call(
        body, name=name, grid=(nb,),
        out_shape=jax.ShapeDtypeStruct((s, D_MODEL), BF16),
        in_specs=zspecs + tab + tab + [pl.BlockSpec(memory_space=pltpu.SMEM)],
        out_specs=pl.BlockSpec((ATT_BLOCK, D_MODEL), lambda i: (i, 0)),
        compiler_params=_params(("parallel",)),
    )(z, z, z, z, cos, cos, cos, cos, sin, sin, sin, sin, sink)


def _att_bwd(z, do, sink, cos, sin, *, name):
    s = z.shape[0]
    nb = s // ATT_BLOCK

    def body(zq_ref, kp_ref, kc_ref, kn_ref, cq_ref, cp_ref, cc_ref, cn_ref, sq_ref, sp_ref, sc_ref, sn_ref, sink_ref,
             do_ref, dq_ref, part_ref, dsink_ref):
        i = pl.program_id(0)
        q, k, v = _att_load(zq_ref, kp_ref, kc_ref, kn_ref, cq_ref, sq_ref, cp_ref, sp_ref, cc_ref, sc_ref, cn_ref, sn_ref)
        valid = _att_mask(i, nb)
        dout = do_ref[...].astype(F32)
        dq_acc = [None] * (N_Q_HEADS // 2)
        dk_acc = [None] * 2
        dv_acc = [None] * 2
        ds_rows = []
        for h in range(N_KV_HEADS):
            grp = h // 2
            lanes = slice(LANES * grp, LANES * (grp + 1))
            qs = _stack_q(q, h).astype(BF16)
            dos = _stack_q(dout, h).astype(BF16)
            prob, psink = _att_probs(qs, k[:, lanes], _sink_rows(sink_ref, h), valid)
            dprob = lax.dot_general(dos, v[:, lanes], (((1,), (1,)), ((), ())), preferred_element_type=F32)
            delta = jnp.sum(prob * dprob, axis=-1, keepdims=True)
            dsc = (prob * (dprob - delta)).astype(BF16)
            ds_rows.append(-psink * delta)
            dqs = jnp.dot(dsc, k[:, lanes], preferred_element_type=F32)
            dq_acc = _unstack_q(dqs, h, dq_acc)
            dkh = lax.dot_general(dsc, qs, (((0,), (0,)), ((), ())), preferred_element_type=F32)
            dvh = lax.dot_general(prob.astype(BF16), dos, (((0,), (0,)), ((), ())), preferred_element_type=F32)
            dk_acc[grp] = dkh if dk_acc[grp] is None else dk_acc[grp] + dkh
            dv_acc[grp] = dvh if dv_acc[grp] is None else dv_acc[grp] + dvh
        dq = jnp.concatenate(dq_acc, axis=1) * (HEAD_DIM ** -0.5)
        dq_ref[...] = _unrope(dq, cq_ref[...], sq_ref[...]).astype(BF16)
        part = jnp.concatenate(dk_acc + dv_acc, axis=1)
        for wdw in range(3):
            part_ref[wdw] = part[ATT_BLOCK * wdw:ATT_BLOCK * (wdw + 1), :]
        rows = []
        for h in range(N_KV_HEADS):
            for gq in range(GROUP):
                tot = jnp.sum(ds_rows[h][ATT_BLOCK * gq:ATT_BLOCK * (gq + 1), :], axis=0, keepdims=True)
                rows.append(jnp.broadcast_to(tot, (1, LANES)))
        dsink = jnp.concatenate(rows, axis=0)

        @pl.when(i == 0)
        def _():
            dsink_ref[...] = dsink

        @pl.when(i > 0)
        def _():
            dsink_ref[...] += dsink

    zspecs, tab = _att_specs(nb)
    return pl.pallas_call(
        body, name=name, grid=(nb,),
        out_shape=(jax.ShapeDtypeStruct((s, D_MODEL), BF16), jax.ShapeDtypeStruct((nb, 3, ATT_BLOCK, 2 * KV_DIM), F32),
                   jax.ShapeDtypeStruct((N_Q_HEADS, LANES), F32)),
        in_specs=zspecs + tab + tab + [pl.BlockSpec(memory_space=pltpu.SMEM), pl.BlockSpec((ATT_BLOCK, D_MODEL), lambda i: (i, 0))],
        out_specs=(pl.BlockSpec((ATT_BLOCK, D_MODEL), lambda i: (i, 0)),
                   pl.BlockSpec((None, 3, ATT_BLOCK, 2 * KV_DIM), lambda i: (i, 0, 0, 0)),
                   pl.BlockSpec((N_Q_HEADS, LANES), lambda i: (0, 0))),
        compiler_params=_params(("arbitrary",)),
    )(z, z, z, z, cos, cos, cos, cos, sin, sin, sin, sin, sink, do)


def _att_bwd_kv(part, cos, sin, *, name):
    nb = part.shape[0]

    def body(pn_ref, pc_ref, pp_ref, c_ref, s_ref, o_ref):
        j = pl.program_id(0)
        tot = pc_ref[...]
        tot = tot + jnp.where(j < nb - 1, pn_ref[...], 0.0)
        tot = tot + jnp.where(j > 0, pp_ref[...], 0.0)
        dk = _unrope(tot[:, :KV_DIM], c_ref[...], s_ref[...])
        o_ref[...] = jnp.concatenate([dk, tot[:, KV_DIM:]], axis=1).astype(BF16)

    blk = (None, None, ATT_BLOCK, 2 * KV_DIM)
    return pl.pallas_call(
        body, name=name, grid=(nb,),
        out_shape=jax.ShapeDtypeStruct((nb * ATT_BLOCK, 2 * KV_DIM), BF16),
        in_specs=[pl.BlockSpec(blk, lambda j: (jnp.minimum(j + 1, nb - 1), 0, 0, 0)),
                  pl.BlockSpec(blk, lambda j: (j, 1, 0, 0)),
                  pl.BlockSpec(blk, lambda j: (jnp.maximum(j - 1, 0), 2, 0, 0)),
                  pl.BlockSpec((ATT_BLOCK, LANES), lambda j: (j, 0)), pl.BlockSpec((ATT_BLOCK, LANES), lambda j: (j, 0))],
        out_specs=pl.BlockSpec((ATT_BLOCK, 2 * KV_DIM), lambda j: (j, 0)),
        compiler_params=_params(("parallel",)),
    )(part, part, part, cos, sin)


def _bdot(a, b, dims):
    return lax.dot_general(a.astype(BF16), b.astype(BF16), (dims, ((), ())), preferred_element_type=F32)


@jax.custom_vjp
def _dot_nn(a, b):
    return _bdot(a, b, ((1,), (0,)))


@jax.custom_vjp
def _dot_nt(a, b):
    return _bdot(a, b, ((1,), (1,)))


@jax.custom_vjp
def _dot_tn(a, b):
    return _bdot(a, b, ((0,), (0,)))


_dot_nn.defvjp(lambda a, b: (_dot_nn(a, b), (a, b)), lambda r, d: (_dot_nt(d, r[1]), _dot_tn(r[0], d)))
_dot_nt.defvjp(lambda a, b: (_dot_nt(a, b), (a, b)), lambda r, d: (_dot_nn(d, r[1]), _dot_tn(d, r[0])))
_dot_tn.defvjp(lambda a, b: (_dot_tn(a, b), (a, b)), lambda r, d: (_dot_nt(r[1], d), _dot_nn(r[0], d)))


def _running_sum(v, up):
    n = v.shape[0]
    rows = lax.broadcasted_iota(jnp.int32, v.shape, 0)
    sh = 1
    while sh < n:
        if up:
            v = v + jnp.where(rows < n - sh, pltpu.roll(v, n - sh, 0), 0.0)
        else:
            v = v + jnp.where(rows >= sh, pltpu.roll(v, sh, 0), 0.0)
        sh *= 2
    return v


@jax.custom_vjp
def _sum_down(v):
    return _running_sum(v, False)


@jax.custom_vjp
def _sum_up(v):
    return _running_sum(v, True)


_sum_down.defvjp(lambda v: (_running_sum(v, False), None), lambda _, d: (_sum_up(d),))
_sum_up.defvjp(lambda v: (_running_sum(v, True), None), lambda _, d: (_sum_down(d),))

N_SUB = HG_CHUNK // HG_SUB


def _fold_blocks(v):
    out = v[:HG_CHUNK]
    for i in range(1, N_SUB):
        out = out + v[HG_CHUNK * i:HG_CHUNK * (i + 1)]
    return out


@jax.custom_vjp
def _fold(v):
    return _fold_blocks(v)


_fold.defvjp(lambda v: (_fold_blocks(v), None), lambda _, d: (jnp.concatenate([d] * N_SUB, axis=0),))


def _hg_consts(rev):
    c, sub = HG_CHUNK, HG_SUB
    rowpos = lax.broadcasted_iota(jnp.int32, (c, HG_DIM), 0)
    rr = lax.broadcasted_iota(jnp.int32, (N_SUB * c, c), 0)
    key = lax.broadcasted_iota(jnp.int32, (N_SUB * c, c), 1)
    blk, qry = rr // c, rr % c
    if rev:
        rowpos, qry, key = c - 1 - rowpos, c - 1 - qry, c - 1 - key
    keep = (key // sub == blk) & (key <= qry)
    return keep, rowpos


def _pick(b, rowpos, t):
    return jnp.sum(jnp.where(rowpos == t, b, 0.0), axis=0, keepdims=True)


def _hg_local(zq, zf, zv, lbv, consts, dots):
    dot_nn, dot_nt, dot_tn, cum, fold = dots
    keep, rowpos = consts
    sig = _sigmoid(zf)
    f = lbv + (1.0 - lbv) * sig
    g = jnp.log(f)
    k = (1.0 - lbv) * (1.0 - sig)
    q = zq * _sigmoid(zq)
    b = cum(g)
    ends = [_pick(b, rowpos, (j + 1) * HG_SUB - 1) for j in range(N_SUB)]
    b_last = ends[-1]
    b_end = b_last
    for j in range(N_SUB - 1):
        b_end = jnp.where(rowpos // HG_SUB == j, ends[j], b_end)
    kc = k * jnp.exp(b_end - b)
    qbs = [q * jnp.exp(jnp.where(rowpos >= j * HG_SUB, b - ends[j], 0.0)) for j in range(N_SUB)]
    scores = fold(jnp.where(keep, dot_nt(jnp.concatenate(qbs, axis=0), kc), 0.0))
    return dot_nn(scores, zv), q * jnp.exp(b), k * jnp.exp(b_last - b), jnp.exp(b_last)


def _hg_chunk(zq, zf, zv, lbv, st, consts, dots):
    intra, qs, kd, dec = _hg_local(zq, zf, zv, lbv, consts, dots)
    return intra + dots[1](qs, st), dec * st + dots[2](zv, kd)


def _hg_dots(diff, rev):
    if diff:
        return _dot_nn, _dot_nt, _dot_tn, (_sum_up if rev else _sum_down), _fold
    return (lambda a, b: _bdot(a, b, ((1,), (0,))), lambda a, b: _bdot(a, b, ((1,), (1,))),
            lambda a, b: _bdot(a, b, ((0,), (0,))), lambda v: _running_sum(v, rev), _fold_blocks)


def _chunk_loop(nch, step, init):
    per = min(HG_UNROLL, nch)

    def trip(i, carry):
        for u in range(per):
            carry = step(i * per + u, carry)
        return carry

    return lax.fori_loop(0, nch // per, trip, init)


def _hg_specs(ts, nch, trow):
    tile = pl.BlockSpec((ts, HG_DIM), lambda h, t: (trow(t), h))
    mats = pl.BlockSpec((None, nch, HG_DIM, HG_DIM), lambda h, t: (h, trow(t), 0, 0))
    vecs = pl.BlockSpec((None, nch, 1, HG_DIM), lambda h, t: (h, trow(t), 0, 0))
    return tile, mats, vecs


def _hg_prep(z, lb, *, rev, ts, name):
    s = z.shape[0]
    nt = s // ts
    nch = ts // HG_CHUNK
    fcol = HG_HEADS * (2 if rev else 1)

    def body(zq_ref, zf_ref, zv_ref, lb_ref, intra_ref, qs_ref, upd_ref, dec_ref):
        consts = _hg_consts(rev)
        dots = _hg_dots(False, rev)
        lbv = lb_ref[...]

        def step(c, carry):
            rows = pl.ds(pl.multiple_of(c * HG_CHUNK, HG_CHUNK), HG_CHUNK)
            zv = zv_ref[rows, :]
            intra, qs, kd, dec = _hg_local(zq_ref[rows, :], zf_ref[rows, :], zv, lbv, consts, dots)
            intra_ref[rows, :] = intra
            qs_ref[rows, :] = qs.astype(BF16)
            upd_ref[c] = dots[2](zv, kd)
            dec_ref[c] = dec
            return carry

        _chunk_loop(nch, step, 0)

    col = lambda off: pl.BlockSpec((ts, HG_DIM), lambda h, t: (t, off + h))
    tile, mats, vecs = _hg_specs(ts, nch, lambda t: t)
    nchunks = s // HG_CHUNK
    return pl.pallas_call(
        body, name=name, grid=(HG_HEADS, nt),
        out_shape=(jax.ShapeDtypeStruct((s, D_MODEL), F32), jax.ShapeDtypeStruct((s, D_MODEL), BF16),
                   jax.ShapeDtypeStruct((HG_HEADS, nchunks, HG_DIM, HG_DIM), F32),
                   jax.ShapeDtypeStruct((HG_HEADS, nchunks, 1, HG_DIM), F32)),
        in_specs=[col(0), col(fcol), col(3 * HG_HEADS), pl.BlockSpec((None, 1, HG_DIM), lambda h, t: (h, 0, 0))],
        out_specs=(tile, tile, mats, vecs),
        compiler_params=_params(("parallel", "parallel")),
    )(z, z, z, lb)


def _hg_scan(intra, qs, upd, dec, *, rev, ts, name):
    s = intra.shape[0]
    nt = s // ts
    nch = ts // HG_CHUNK

    def body(intra_ref, qs_ref, upd_ref, dec_ref, o_ref, st_ref, state_ref):
        @pl.when(pl.program_id(1) == 0)
        def _():
            state_ref[...] = jnp.zeros_like(state_ref)

        def step(ci, st):
            c = (nch - 1 - ci) if rev else ci
            rows = pl.ds(pl.multiple_of(c * HG_CHUNK, HG_CHUNK), HG_CHUNK)
            st_ref[c] = st
            o_ref[rows, :] = intra_ref[rows, :] + _bdot(qs_ref[rows, :], st, ((1,), (1,)))
            return dec_ref[c] * st + upd_ref[c]

        state_ref[...] = _chunk_loop(nch, step, state_ref[...])

    tile, mats, vecs = _hg_specs(ts, nch, (lambda t: nt - 1 - t) if rev else (lambda t: t))
    return pl.pallas_call(
        body, name=name, grid=(HG_HEADS, nt),
        out_shape=(jax.ShapeDtypeStruct((s, D_MODEL), F32),
                   jax.ShapeDtypeStruct((HG_HEADS, s // HG_CHUNK, HG_DIM, HG_DIM), F32)),
        in_specs=[tile, tile, mats, vecs], out_specs=(tile, mats),
        scratch_shapes=[pltpu.VMEM((HG_DIM, HG_DIM), F32)],
        compiler_params=_params(("parallel", "arbitrary")),
    )(intra, qs, upd, dec)


def _hg_scan_bwd(qs, dout, dec, *, rev, ts, name):
    s = qs.shape[0]
    nt = s // ts
    nch = ts // HG_CHUNK

    def body(qs_ref, do_ref, dec_ref, g_ref, grad_ref):
        @pl.when(pl.program_id(1) == 0)
        def _():
            grad_ref[...] = jnp.zeros_like(grad_ref)

        def step(ci, gr):
            c = ci if rev else (nch - 1 - ci)
            rows = pl.ds(pl.multiple_of(c * HG_CHUNK, HG_CHUNK), HG_CHUNK)
            g_ref[c] = gr
            return dec_ref[c] * gr + _bdot(do_ref[rows, :], qs_ref[rows, :], ((0,), (0,)))

        grad_ref[...] = _chunk_loop(nch, step, grad_ref[...])

    tile, mats, vecs = _hg_specs(ts, nch, (lambda t: t) if rev else (lambda t: nt - 1 - t))
    return pl.pallas_call(
        body, name=name, grid=(HG_HEADS, nt),
        out_shape=jax.ShapeDtypeStruct((HG_HEADS, s // HG_CHUNK, HG_DIM, HG_DIM), F32),
        in_specs=[tile, tile, vecs], out_specs=mats,
        scratch_shapes=[pltpu.VMEM((HG_DIM, HG_DIM), F32)],
        compiler_params=_params(("parallel", "arbitrary")),
    )(qs, dout, dec)


def _hg_bwd(z, lb, states, gstates, dout, addq, addv, *, rev, ts, name):
    s = z.shape[0]
    nt = s // ts
    nch = ts // HG_CHUNK
    fcol = HG_HEADS * (2 if rev else 1)
    has_add = addq is not None

    def body(*refs):
        zq_ref, zf_ref, zv_ref, lb_ref, st_ref, g_ref, do_ref = refs[:7]
        aq_ref, av_ref = (refs[7], refs[8]) if has_add else (None, None)
        dq_ref, df_ref, dv_ref, dlb_ref = refs[-4:]
        consts = _hg_consts(rev)
        dots = _hg_dots(True, rev)
        lbv = lb_ref[...]

        def step(c, acc):
            rows = pl.ds(pl.multiple_of(c * HG_CHUNK, HG_CHUNK), HG_CHUNK)
            fn = lambda a, b2, c2, d2, e2: _hg_chunk(a, b2, c2, d2, e2, consts, dots)
            _, pull = jax.vjp(fn, zq_ref[rows, :], zf_ref[rows, :], zv_ref[rows, :], lbv, st_ref[c])
            dq, df, dv, dlb, _ = pull((do_ref[rows, :], g_ref[c]))
            if has_add:
                dq = dq + aq_ref[rows, :]
                dv = dv + av_ref[rows, :]
            dq_ref[rows, :] = dq
            df_ref[rows, :] = df
            dv_ref[rows, :] = dv
            return acc + dlb

        dlb_blk = _chunk_loop(nch, step, jnp.zeros((1, HG_DIM), F32))

        @pl.when(pl.program_id(1) == 0)
        def _():
            dlb_ref[...] = dlb_blk

        @pl.when(pl.program_id(1) > 0)
        def _():
            dlb_ref[...] += dlb_blk

    col = lambda off: pl.BlockSpec((ts, HG_DIM), lambda h, t: (t, off + h))
    tile, mats, _ = _hg_specs(ts, nch, lambda t: t)
    in_specs = [col(0), col(fcol), col(3 * HG_HEADS), pl.BlockSpec((None, 1, HG_DIM), lambda h, t: (h, 0, 0)),
                mats, mats, tile]
    args = [z, z, z, lb, states, gstates, dout]
    if has_add:
        in_specs += [tile, tile]
        args += [addq, addv]
    full = jax.ShapeDtypeStruct((s, D_MODEL), F32)
    return pl.pallas_call(
        body, name=name, grid=(HG_HEADS, nt),
        out_shape=(full, full, full, jax.ShapeDtypeStruct((HG_HEADS, 1, HG_DIM), F32)),
        in_specs=in_specs,
        out_specs=(tile, tile, tile, pl.BlockSpec((None, 1, HG_DIM), lambda h, t: (h, 0, 0))),
        compiler_params=_params(("parallel", "arbitrary")),
    )(*args)


def _hg_post(of, ob, z, norm_g, *, tm, name):
    s = of.shape[0]

    def body(of_ref, ob_ref, gate_ref, ng_ref, y_ref):
        gn = ng_ref[...]
        for h in range(HG_HEADS):
            ln = slice(HG_DIM * h, HG_DIM * (h + 1))
            o = of_ref[:, ln] + ob_ref[:, ln]
            r = lax.rsqrt(jnp.mean(o * o, axis=-1, keepdims=True) + LN_EPS)
            gt = gate_ref[:, ln]
            y_ref[:, ln] = (o * r * gn * gt * _sigmoid(gt)).astype(BF16)

    row = lambda i: (i, 0)
    return pl.pallas_call(
        body, name=name, grid=(s // tm,),
        out_shape=jax.ShapeDtypeStruct((s, D_MODEL), BF16),
        in_specs=[pl.BlockSpec((tm, D_MODEL), row), pl.BlockSpec((tm, D_MODEL), row),
                  pl.BlockSpec((tm, D_MODEL), lambda i: (i, 4)), pl.BlockSpec((1, HG_DIM), lambda i: (0, 0))],
        out_specs=pl.BlockSpec((tm, D_MODEL), row),
        compiler_params=_params(("parallel",)),
    )(of, ob, z, norm_g)


def _hg_post_bwd(dy, of, ob, z, norm_g, *, tm, name):
    s = of.shape[0]

    def body(dy_ref, of_ref, ob_ref, gate_ref, ng_ref, do_ref, dgate_ref, dng_ref):
        gn = ng_ref[...]
        tot = jnp.zeros((1, HG_DIM), F32)
        for h in range(HG_HEADS):
            ln = slice(HG_DIM * h, HG_DIM * (h + 1))
            d = dy_ref[:, ln].astype(F32)
            o = of_ref[:, ln] + ob_ref[:, ln]
            r = lax.rsqrt(jnp.mean(o * o, axis=-1, keepdims=True) + LN_EPS)
            ohat = o * r
            gt = gate_ref[:, ln]
            sg = _sigmoid(gt)
            don = d * gt * sg
            dgate_ref[:, ln] = (d * ohat * gn * sg * (1.0 + gt * (1.0 - sg))).astype(BF16)
            tot = tot + jnp.sum(don * ohat, axis=0, keepdims=True)
            dohat = don * gn
            do_ref[:, ln] = r * (dohat - ohat * jnp.mean(dohat * ohat, axis=-1, keepdims=True))

        @pl.when(pl.program_id(0) == 0)
        def _():
            dng_ref[...] = tot

        @pl.when(pl.program_id(0) > 0)
        def _():
            dng_ref[...] += tot

    row = lambda i: (i, 0)
    return pl.pallas_call(
        body, name=name, grid=(s // tm,),
        out_shape=(jax.ShapeDtypeStruct((s, D_MODEL), F32), jax.ShapeDtypeStruct((s, D_MODEL), BF16),
                   jax.ShapeDtypeStruct((1, HG_DIM), F32)),
        in_specs=[pl.BlockSpec((tm, D_MODEL), row), pl.BlockSpec((tm, D_MODEL), row), pl.BlockSpec((tm, D_MODEL), row),
                  pl.BlockSpec((tm, D_MODEL), lambda i: (i, 4)), pl.BlockSpec((1, HG_DIM), lambda i: (0, 0))],
        out_specs=(pl.BlockSpec((tm, D_MODEL), row), pl.BlockSpec((tm, D_MODEL), row),
                   pl.BlockSpec((1, HG_DIM), lambda i: (0, 0))),
        compiler_params=_params(("arbitrary",)),
    )(dy, of, ob, z, norm_g)


def _lb_fwd(logits, *, name):
    w = logits.shape[1]

    def body(l_ref, o_ref):
        lg = l_ref[...]
        e = jnp.exp(lg - jnp.max(lg, axis=0, keepdims=True))
        sm = e / jnp.sum(e, axis=0, keepdims=True)
        o_ref[0:1, :] = sm[1:2]
        o_ref[1:2, :] = sm[1:2] + sm[2:3] + sm[3:4]

    return pl.pallas_call(body, name=name, out_shape=jax.ShapeDtypeStruct((2, w), F32))(logits)


def _lb_bwd(logits, dlb, *, name):
    w = logits.shape[1]

    def body(l_ref, d_ref, o_ref):
        lg = l_ref[...]
        e = jnp.exp(lg - jnp.max(lg, axis=0, keepdims=True))
        sm = e / jnp.sum(e, axis=0, keepdims=True)
        d1, d3 = d_ref[0:1, :], d_ref[1:2, :]
        dot = sm[1:2] * (d1 + d3) + (sm[2:3] + sm[3:4]) * d3
        o_ref[0:1, :] = -sm[0:1] * dot
        o_ref[1:2, :] = sm[1:2] * (d1 + d3 - dot)
        o_ref[2:3, :] = sm[2:3] * (d3 - dot)
        o_ref[3:4, :] = sm[3:4] * (d3 - dot)

    return pl.pallas_call(body, name=name, out_shape=jax.ShapeDtypeStruct((4, w), F32))(logits, dlb)


def _adamw(w, g, m, v, *, tr, name):
    rows = w.shape[0]
    parts = g.ndim == 3
    c1 = 1.0 / (1.0 - ADAM_B1 ** ADAM_STEP)
    c2 = 1.0 / (1.0 - ADAM_B2 ** ADAM_STEP)

    def body(w_ref, g_ref, m_ref, v_ref, go_ref, d_ref, mo_ref, vo_ref):
        if parts:
            gg = g_ref[0].astype(F32)
            for i in range(1, N_DEV):
                gg = gg + g_ref[i].astype(F32)
        else:
            gg = g_ref[...]
        mm = ADAM_B1 * m_ref[...] + (1.0 - ADAM_B1) * gg
        vv = ADAM_B2 * v_ref[...] + (1.0 - ADAM_B2) * (gg * gg)
        go_ref[...] = gg
        mo_ref[...] = mm
        vo_ref[...] = vv
        d_ref[...] = -ADAM_LR * ((mm * c1) / (jnp.sqrt(vv * c2) + ADAM_EPS) + ADAM_WD * w_ref[...])

    tile = pl.BlockSpec((tr, D_MODEL), lambda i: (i, 0))
    gspec = pl.BlockSpec((N_DEV, tr, D_MODEL), lambda i: (0, i, 0)) if parts else tile
    out = jax.ShapeDtypeStruct((rows, D_MODEL), F32)
    return pl.pallas_call(
        body, name=name, grid=(rows // tr,),
        out_shape=(out, out, out, out),
        in_specs=[tile, gspec, tile, tile], out_specs=(tile, tile, tile, tile),
        compiler_params=_params(("parallel",)),
    )(w, g, m, v)


def _sum8(parts, *, name):
    def body(p_ref, o_ref):
        tot = p_ref[0]
        for i in range(1, N_DEV):
            tot = tot + p_ref[i]
        o_ref[...] = tot

    return pl.pallas_call(body, name=name, out_shape=jax.ShapeDtypeStruct(parts.shape[1:], parts.dtype))(parts)


def _rows(a):
    return a.reshape(-1, D_MODEL)


def _pack_local(shards):
    return jnp.concatenate([_rows(shards[n]) for n, _ in BIG], axis=0)


def _unpack_local(packed, like):
    out, r = {}, 0
    for n, _ in BIG:
        k = like[n].size // D_MODEL
        out[n] = packed[r:r + k].reshape(like[n].shape)
        r += k
    return out


def _unpack_gathered(gathered, like):
    out, r = {}, 0
    for n, ax in BIG:
        shp = like[n].shape
        k = like[n].size // D_MODEL
        t = gathered[:, r:r + k].reshape((N_DEV,) + shp)
        t = jnp.moveaxis(t, 0, ax)
        out[n] = t.reshape(shp[:ax] + (N_DEV * shp[ax],) + shp[ax + 1:])
        r += k
    return out


def _pack_full(grads, like):
    cols = []
    for n, ax in BIG:
        shp = like[n].shape
        t = grads[n].reshape(shp[:ax] + (N_DEV, shp[ax]) + shp[ax + 1:])
        t = jnp.moveaxis(t, ax, 0)
        cols.append(t.reshape(N_DEV, -1, D_MODEL).astype(BF16))
    return jnp.concatenate(cols, axis=1)


SMALL_ROWS = 24


def _pad_row(a):
    flat = a.reshape(1, -1)
    return jnp.pad(flat, ((0, 0), (0, D_MODEL - flat.shape[1])))


def _tile(n, pref):
    return min(n, pref)


def kernel(x, p, att_w_qkv, att_sink, att_w_o, hgrn_w_in, hgrn_lb_logits, hgrn_norm_g, hgrn_w_o, ln_mix_g, ln_mix_b, ffn_w_in, ffn_w_out, ln_ffn_g, ln_ffn_b, ple_w_gate, ple_w_proj, loss_target, m_att_w_qkv, m_att_sink, m_att_w_o, m_hgrn_w_in, m_hgrn_lb_logits, m_hgrn_norm_g, m_hgrn_w_o, m_ln_mix_g, m_ln_mix_b, m_ffn_w_in, m_ffn_w_out, m_ln_ffn_g, m_ln_ffn_b, m_ple_w_gate, m_ple_w_proj, v_att_w_qkv, v_att_sink, v_att_w_o, v_hgrn_w_in, v_hgrn_lb_logits, v_hgrn_norm_g, v_hgrn_w_o, v_ln_mix_g, v_ln_mix_b, v_ffn_w_in, v_ffn_w_out, v_ln_ffn_g, v_ln_ffn_b, v_ple_w_gate, v_ple_w_proj):
    names = ["att_w_qkv", "att_sink", "att_w_o", "hgrn_w_in", "hgrn_lb_logits", "hgrn_norm_g", "hgrn_w_o", "ln_mix_g",
             "ln_mix_b", "ffn_w_in", "ffn_w_out", "ln_ffn_g", "ln_ffn_b", "ple_w_gate", "ple_w_proj"]
    w = dict(zip(names, (att_w_qkv, att_sink, att_w_o, hgrn_w_in, hgrn_lb_logits, hgrn_norm_g, hgrn_w_o, ln_mix_g,
                         ln_mix_b, ffn_w_in, ffn_w_out, ln_ffn_g, ln_ffn_b, ple_w_gate, ple_w_proj)))
    mom = dict(zip(names, (m_att_w_qkv, m_att_sink, m_att_w_o, m_hgrn_w_in, m_hgrn_lb_logits, m_hgrn_norm_g, m_hgrn_w_o,
                           m_ln_mix_g, m_ln_mix_b, m_ffn_w_in, m_ffn_w_out, m_ln_ffn_g, m_ln_ffn_b, m_ple_w_gate,
                           m_ple_w_proj)))
    var = dict(zip(names, (v_att_w_qkv, v_att_sink, v_att_w_o, v_hgrn_w_in, v_hgrn_lb_logits, v_hgrn_norm_g, v_hgrn_w_o,
                           v_ln_mix_g, v_ln_mix_b, v_ffn_w_in, v_ffn_w_out, v_ln_ffn_g, v_ln_ffn_b, v_ple_w_gate,
                           v_ple_w_proj)))
    s = x.shape[1]
    me = 4 * lax.axis_index("x") + 2 * lax.axis_index("y") + lax.axis_index("c")
    tm = _tile(s, 512)
    tbig = _tile(s, 1024)
    ts = _tile(s // 2, 512)
    x0 = x.reshape(s, D_MODEL)
    target = loss_target.reshape(s, D_MODEL)
    pl_in = p.reshape(DEPTH, s, PLE_DIM)

    big_like = {n: w[n] for n, _ in BIG}
    w_rows = _pack_local({n: w[n] for n, _ in BIG})
    full = _unpack_gathered(_gather(w_rows.astype(BF16), name="gather_weights"), big_like)
    lb_rows = jnp.pad(hgrn_lb_logits.reshape(8, HG_DIM), ((0, 0), (0, D_MODEL - HG_DIM)))
    lb_all = _gather(lb_rows, name="gather_lb")[:, :, :HG_DIM]
    logits_full = jnp.moveaxis(lb_all, 0, 1).reshape(DEPTH, 2 * D_MODEL)
    lb = _lb_fwd(logits_full, name="lb_fwd")
    cos, sin = _rope_tables(s)

    saved = []
    xf, xb = x0, x0
    for i in range(DEPTH):
        j = i // 2
        sv = {"x": xf, "xb": xb}
        if i % 2 == 0:
            z = _mm(xb, full["att_w_qkv"][j], n=D_MODEL + 2 * KV_DIM, tm=tbig, tn=512, tk=D_MODEL, name="att_in")
            sink = w["att_sink"][j]
            o = _att_fwd(z, sink, cos, sin, name="att_fwd")
            w_o = full["att_w_o"][j]
        else:
            z = _mm(xb, full["hgrn_w_in"][j], n=5 * D_MODEL, tm=tbig, tn=1024, tk=D_MODEL, name="hgrn_in")
            lbl = lb[j].reshape(2, HG_HEADS, 1, HG_DIM)
            in_f, qs_f, upd_f, dec_f = _hg_prep(z, lbl[0], rev=False, ts=ts, name="hgrn_prep")
            in_b, qs_b, upd_b, dec_b = _hg_prep(z, lbl[1], rev=True, ts=ts, name="hgrn_prep_rev")
            of, st_f = _hg_scan(in_f, qs_f, upd_f, dec_f, rev=False, ts=ts, name="hgrn_scan")
            ob, st_b = _hg_scan(in_b, qs_b, upd_b, dec_b, rev=True, ts=ts, name="hgrn_scan_rev")
            o = _hg_post(of, ob, z, w["hgrn_norm_g"][j].reshape(1, HG_DIM), tm=tm, name="hgrn_post")
            w_o = full["hgrn_w_o"][j]
            sv.update(of=of, ob=ob, st_f=st_f, st_b=st_b, lbl=lbl, qs_f=qs_f, qs_b=qs_b, dec_f=dec_f, dec_b=dec_b)
        sv.update(z=z, o=o)
        pre1, x1, x1b = _proj_ln(o, w_o, xf, w["ln_mix_g"][i:i + 1], w["ln_mix_b"][i:i + 1], tm=tm, name="mix_out_ln")
        gg, uu, act = _ffn_in(x1b, full["ffn_w_in"][i], tm=tm, tn=FF_TILE, name="ffn_in")
        pre2, x2, x2b = _proj_ln(act, full["ffn_w_out"][i], x1, w["ln_ffn_g"][i:i + 1], w["ln_ffn_b"][i:i + 1], tm=tm,
                                 name="ffn_out_ln")
        xf, xb = _ple_fwd(x2, x2b, pl_in[i], full["ple_w_gate"][i], full["ple_w_proj"][i], tm=tm, name="ple_fwd")
        sv.update(pre1=pre1, x1=x1, x1b=x1b, g=gg, u=uu, act=act, pre2=pre2, x2b=x2b)
        saved.append(sv)

    dx, loss_blk = _loss_head(xf, target, tm=tm, name="loss_head")
    loss = lax.psum(loss_blk[0, 0], AXES)

    gfull = {n: [None] * w[n].shape[0] for n, _ in BIG}
    small = {n: [None] * DEPTH for n in ("ln_mix_g", "ln_mix_b", "ln_ffn_g", "ln_ffn_b")}
    dlb_rows = [None] * 4
    dnorm, dsink = [None] * 2, [None] * 2
    mmw = functools.partial(_mm, ta=True, tk=tbig, out_dtype=BF16)
    for i in reversed(range(DEPTH)):
        j = i // 2
        sv = saved[i]
        dx2, da, dpp = _ple_bwd(dx, sv["x2b"], pl_in[i], full["ple_w_gate"][i], full["ple_w_proj"][i], tm=tm,
                                name="ple_bwd")
        gfull["ple_w_gate"][i] = mmw(sv["x2b"], da, n=D_MODEL, tm=D_MODEL, tn=D_MODEL, name="dw_ple_gate")
        gfull["ple_w_proj"][i] = mmw(pl_in[i], dpp, n=D_MODEL, tm=PLE_DIM, tn=D_MODEL, name="dw_ple_proj")
        dy2, dy2b, small["ln_ffn_g"][i], small["ln_ffn_b"][i] = _ln_bwd(dx2, sv["pre2"], w["ln_ffn_g"][i:i + 1], tm=tm,
                                                                         name="ln_bwd")
        dg, du = _ffn_bwd_act(dy2b, full["ffn_w_out"][i], sv["g"], sv["u"], tm=tm, tn=FF_TILE, name="ffn_bwd_act")
        gfull["ffn_w_out"][i] = mmw(sv["act"], dy2b, n=D_MODEL, tm=FF_TILE, tn=D_MODEL, name="dw_ffn_out")
        t = _mm(dg, full["ffn_w_in"][i], n=D_MODEL, tm=tbig, tn=D_MODEL, tk=FF_TILE, tb=True, add=dy2, add_scale=ALPHA,
                name="ffn_bwd_x")
        dx1 = _mm(du, full["ffn_w_in"][i], n=D_MODEL, tm=tbig, tn=D_MODEL, tk=FF_TILE, tb=True, add=t,
                  bk_off=D_FF // FF_TILE, name="ffn_bwd_x_u")
        gfull["ffn_w_in"][i] = jnp.concatenate(
            [mmw(sv["x1b"], dg, n=D_FF, tm=D_MODEL, tn=FF_TILE, name="dw_ffn_in"),
             mmw(sv["x1b"], du, n=D_FF, tm=D_MODEL, tn=FF_TILE, name="dw_ffn_in")], axis=1)
        dy1, dy1b, small["ln_mix_g"][i], small["ln_mix_b"][i] = _ln_bwd(dx1, sv["pre1"], w["ln_mix_g"][i:i + 1], tm=tm,
                                                                         name="ln_bwd")
        if i % 2 == 0:
            w_o, w_in, n_in = full["att_w_o"][j], full["att_w_qkv"][j], D_MODEL + 2 * KV_DIM
        else:
            w_o, w_in, n_in = full["hgrn_w_o"][j], full["hgrn_w_in"][j], 5 * D_MODEL
        do = _mm(dy1b, w_o, n=D_MODEL, tm=tbig, tn=D_MODEL, tk=D_MODEL, tb=True, out_dtype=BF16, name="mix_out_bwd")
        g_o = mmw(sv["o"], dy1b, n=D_MODEL, tm=D_MODEL, tn=D_MODEL, name="dw_mix_out")
        if i % 2 == 0:
            gfull["att_w_o"][j] = g_o
            dzq, part, dsk = _att_bwd(sv["z"], do, w["att_sink"][j], cos, sin, name="att_bwd")
            dzkv = _att_bwd_kv(part, cos, sin, name="att_bwd_kv")
            dz = jnp.concatenate([dzq, dzkv], axis=1)
            dsink[j] = dsk[:, 0]
        else:
            gfull["hgrn_w_o"][j] = g_o
            dsum, dgate, dnorm[j] = _hg_post_bwd(do, sv["of"], sv["ob"], sv["z"], w["hgrn_norm_g"][j].reshape(1, HG_DIM),
                                                 tm=tm, name="hgrn_post_bwd")
            g_f = _hg_scan_bwd(sv["qs_f"], dsum, sv["dec_f"], rev=False, ts=ts, name="hgrn_scan_bwd")
            g_b = _hg_scan_bwd(sv["qs_b"], dsum, sv["dec_b"], rev=True, ts=ts, name="hgrn_scan_bwd_rev")
            dq1, df1, dv1, dlb1 = _hg_bwd(sv["z"], sv["lbl"][0], sv["st_f"], g_f, dsum, None, None, rev=False, ts=ts,
                                          name="hgrn_bwd")
            dq2, df2, dv2, dlb2 = _hg_bwd(sv["z"], sv["lbl"][1], sv["st_b"], g_b, dsum, dq1, dv1, rev=True, ts=ts,
                                          name="hgrn_bwd_rev")
            dz = jnp.concatenate([dq2.astype(BF16), df1.astype(BF16), df2.astype(BF16), dv2.astype(BF16), dgate], axis=1)
            dlb_rows[2 * j] = dlb1.reshape(1, D_MODEL)
            dlb_rows[2 * j + 1] = dlb2.reshape(1, D_MODEL)
        dx = _mm(dz, w_in, n=D_MODEL, tm=tbig, tn=D_MODEL, tk=512, tb=True, add=dy1, add_scale=ALPHA, name="mix_in_bwd")
        g_in = mmw(sv["xb"], dz, n=n_in, tm=D_MODEL, tn=512, name="dw_mix_in")
        gfull["att_w_qkv" if i % 2 == 0 else "hgrn_w_in"][j] = g_in
    grad_x = dx.reshape(x.shape)

    gstack = {n: jnp.stack(gfull[n]) for n, _ in BIG}
    recv = _exchange(_pack_full(gstack, big_like), name="exchange_grads")
    m_rows = _pack_local({n: mom[n] for n, _ in BIG})
    v_rows = _pack_local({n: var[n] for n, _ in BIG})
    outs = _adamw(w_rows, recv, m_rows, v_rows, tr=320, name="adamw_big")
    big_out = [_unpack_local(o_, big_like) for o_ in outs]

    small_rows = jnp.concatenate(
        [jnp.concatenate(small[n], axis=0) for n in ("ln_mix_g", "ln_mix_b", "ln_ffn_g", "ln_ffn_b")] + dlb_rows
        + [_pad_row(jnp.stack(dnorm)), _pad_row(jnp.stack(dsink)), jnp.zeros((2, D_MODEL), F32)], axis=0)
    small_all = _gather(small_rows, name="gather_small")
    lbw, lbm, lbv = (t.reshape(4, 2 * HG_DIM) for t in (hgrn_lb_logits, mom["hgrn_lb_logits"], var["hgrn_lb_logits"]))
    summed = _sum8(small_all, name="sum_small")
    dlb_mine = lax.dynamic_slice_in_dim(summed[16:20].reshape(2, 2, HG_HEADS, HG_DIM), me, 1, axis=2)
    dlogits = _lb_bwd(lbw, dlb_mine.reshape(2, 2 * HG_DIM), name="lb_bwd")

    def small_pack(ln4, lbt, ng, sk):
        return jnp.concatenate([ln4[n] for n in ("ln_mix_g", "ln_mix_b", "ln_ffn_g", "ln_ffn_b")]
                               + [_pad_row(lbt), _pad_row(ng), _pad_row(sk), jnp.zeros((5, D_MODEL), F32)], axis=0)

    g_small = jnp.concatenate([summed[:16], _pad_row(dlogits), summed[20:22], jnp.zeros((5, D_MODEL), F32)], axis=0)
    souts = _adamw(small_pack(w, lbw, w["hgrn_norm_g"], w["att_sink"]), g_small,
                   small_pack(mom, lbm, mom["hgrn_norm_g"], mom["att_sink"]),
                   small_pack(var, lbv, var["hgrn_norm_g"], var["att_sink"]), tr=SMALL_ROWS, name="adamw_small")

    def small_unpack(t):
        out = {n: t[4 * k:4 * k + 4] for k, n in enumerate(("ln_mix_g", "ln_mix_b", "ln_ffn_g", "ln_ffn_b"))}
        out["hgrn_lb_logits"] = t[16].reshape(hgrn_lb_logits.shape)
        out["hgrn_norm_g"] = t[17, :2 * HG_DIM].reshape(hgrn_norm_g.shape)
        out["att_sink"] = t[18, :2 * N_Q_HEADS].reshape(att_sink.shape)
        return out

    result = [loss, grad_x]
    for big_t, small_t in zip(big_out, souts):
        merged = dict(big_t)
        merged.update(small_unpack(small_t))
        result += [merged[n] for n in names]
    return tuple(result)
```

```python
import functools

import jax
import jax.numpy as jnp
from jax import lax
from jax.experimental import pallas as pl
from jax.experimental.pallas import tpu as pltpu

F32 = jnp.float32
BF16 = jnp.bfloat16

D_MODEL = 1024
DEPTH = 4
HEAD_DIM = 64
N_Q_HEADS = 16
N_KV_HEADS = 4
GROUP = 4
KV_DIM = 256
ATT_BLOCK = 128
ROPE_DIM = 16
ROPE_THETA = 500000.0
HG_HEADS = 8
HG_DIM = 128
HG_CHUNK = 64
HG_SUB = 16
D_FF = 2816
FF_TILE = 1408
PLE_DIM = 256
ALPHA = (2 * DEPTH) ** 0.25
LN_EPS = 1e-5
ADAM_LR, ADAM_B1, ADAM_B2, ADAM_EPS, ADAM_WD, ADAM_STEP = 0.001, 0.9, 0.999, 1e-08, 0.01, 10

N_DEV = 8
LANES = 128
VMEM_LIMIT = 52 * 1024 * 1024
NEG = -1e30
MESH = pl.DeviceIdType.MESH
AXES = ("x", "y", "c")

BIG = (("att_w_qkv", 2), ("att_w_o", 1), ("hgrn_w_in", 2), ("hgrn_w_o", 1), ("ffn_w_in", 2), ("ffn_w_out", 1),
       ("ple_w_gate", 1), ("ple_w_proj", 2))


def _params(sem=None, vmem=VMEM_LIMIT):
    return pltpu.CompilerParams(dimension_semantics=sem, vmem_limit_bytes=vmem)


def _sigmoid(x):
    return jax.nn.sigmoid(x)


def _exchange(src, *, name):
    def body(src_ref, out_ref, send_sems, recv_sems, local_sem):
        x, y, c = lax.axis_index("x"), lax.axis_index("y"), lax.axis_index("c")
        me = 4 * x + 2 * y + c
        local = pltpu.make_async_copy(src_ref.at[me], out_ref.at[me], local_sem)
        local.start()
        copies = []
        for k in range(1, N_DEV):
            px, py, pc = x ^ (k >> 2), y ^ ((k >> 1) & 1), c ^ (k & 1)
            peer = 4 * px + 2 * py + pc
            copies.append((peer, pltpu.make_async_remote_copy(
                src_ref=src_ref.at[peer], dst_ref=out_ref.at[me], send_sem=send_sems.at[k], recv_sem=recv_sems.at[k],
                device_id=(px, py, pc), device_id_type=MESH)))
        for _, cp in copies:
            cp.start()
        for k, (peer, cp) in enumerate(copies, start=1):
            cp.wait_send()
            pltpu.make_async_remote_copy(
                src_ref=src_ref.at[peer], dst_ref=out_ref.at[peer], send_sem=send_sems.at[k], recv_sem=recv_sems.at[k],
                device_id=(x, y, c), device_id_type=MESH).wait_recv()
        local.wait()

    return pl.pallas_call(
        body, name=name,
        out_shape=jax.ShapeDtypeStruct(src.shape, src.dtype),
        in_specs=[pl.BlockSpec(memory_space=pltpu.HBM)],
        out_specs=pl.BlockSpec(memory_space=pltpu.HBM),
        scratch_shapes=[pltpu.SemaphoreType.DMA((N_DEV,)), pltpu.SemaphoreType.DMA((N_DEV,)), pltpu.SemaphoreType.DMA],
    )(src)


def _gather(src, *, name):
    def body(src_ref, out_ref, send_sems, recv_sems, local_sem):
        x, y, c = lax.axis_index("x"), lax.axis_index("y"), lax.axis_index("c")
        sibling = (x, y, 1 - c)
        chips = [(1 - x, y), (x, 1 - y), (1 - x, 1 - y)]

        def rows(px, py, pc):
            return out_ref.at[4 * px + 2 * py + pc]

        def copy(k, block, to, from_src=False):
            return pltpu.make_async_remote_copy(
                src_ref=src_ref if from_src else rows(*block), dst_ref=rows(*block), send_sem=send_sems.at[k],
                recv_sem=recv_sems.at[k], device_id=to, device_id_type=MESH)

        me = (x, y, c)
        mine = pltpu.make_async_copy(src_ref, rows(*me), local_sem)
        mine.start()
        first = [copy(0, me, sibling, from_src=True)]
        first += [copy(1 + j, me, (*chip, c), from_src=True) for j, chip in enumerate(chips)]
        for cp in first:
            cp.start()
        passed = [copy(4 + j, (*chip, c), sibling) for j, chip in enumerate(chips)]
        for j, chip in enumerate(chips):
            copy(1 + j, (*chip, c), me).wait_recv()
            passed[j].start()
        copy(0, sibling, me).wait_recv()
        for j, chip in enumerate(chips):
            copy(4 + j, (*chip, 1 - c), me).wait_recv()
        for cp in first + passed:
            cp.wait_send()
        mine.wait()

    return pl.pallas_call(
        body, name=name,
        out_shape=jax.ShapeDtypeStruct((N_DEV,) + tuple(src.shape), src.dtype),
        in_specs=[pl.BlockSpec(memory_space=pltpu.HBM)],
        out_specs=pl.BlockSpec(memory_space=pltpu.HBM),
        scratch_shapes=[pltpu.SemaphoreType.DMA((7,)), pltpu.SemaphoreType.DMA((7,)), pltpu.SemaphoreType.DMA],
    )(src)


def _mm(a, b, *, n, tm, tn, tk, ta=False, tb=False, out_dtype=F32, add=None, add_scale=1.0, bk_off=0, name):
    m, kdim = (a.shape[1], a.shape[0]) if ta else a.shape
    nk = kdim // tk
    dims = (((0 if ta else 1,), (1 if tb else 0,)), ((), ()))

    def body(*refs):
        if add is None:
            a_ref, b_ref, o_ref = refs[:3]
            add_ref = None
        else:
            a_ref, b_ref, add_ref, o_ref = refs[:4]
        acc_ref = refs[-1] if nk > 1 else None
        part = lax.dot_general(a_ref[...].astype(BF16), b_ref[...].astype(BF16), dims, preferred_element_type=F32)

        def finish(r):
            if add_ref is not None:
                r = r + add_scale * add_ref[...]
            o_ref[...] = r.astype(out_dtype)

        if nk == 1:
            finish(part)
        else:
            k = pl.program_id(2)

            @pl.when(k == 0)
            def _():
                acc_ref[...] = part

            @pl.when(k > 0)
            def _():
                acc_ref[...] += part

            @pl.when(k == nk - 1)
            def _():
                finish(acc_ref[...])

    a_spec = pl.BlockSpec((tk, tm), lambda i, j, k: (k, i)) if ta else pl.BlockSpec((tm, tk), lambda i, j, k: (i, k))
    b_spec = (pl.BlockSpec((tn, tk), lambda i, j, k: (j, k + bk_off)) if tb
              else pl.BlockSpec((tk, tn), lambda i, j, k: (k + bk_off, j)))
    in_specs, args = [a_spec, b_spec], [a, b]
    if add is not None:
        in_specs.append(pl.BlockSpec((tm, tn), lambda i, j, k: (i, j)))
        args.append(add)
    return pl.pallas_call(
        body, name=name, grid=(m // tm, n // tn, nk),
        out_shape=jax.ShapeDtypeStruct((m, n), out_dtype),
        in_specs=in_specs, out_specs=pl.BlockSpec((tm, tn), lambda i, j, k: (i, j)),
        scratch_shapes=[pltpu.VMEM((tm, tn), F32)] if nk > 1 else [],
        compiler_params=_params(("parallel", "parallel", "arbitrary")),
    )(*args)


def _layer_norm_rows(y, g, b):
    mu = jnp.mean(y, axis=-1, keepdims=True)
    yc = y - mu
    var = jnp.mean(yc * yc, axis=-1, keepdims=True)
    return yc * lax.rsqrt(var + LN_EPS) * g + b


def _proj_ln(a, w, res, g, b, *, tm, name):
    s, kdim = a.shape

    def body(a_ref, w_ref, res_ref, g_ref, b_ref, pre_ref, o_ref, obf_ref):
        h = jnp.dot(a_ref[...], w_ref[...], preferred_element_type=F32)
        pre = ALPHA * res_ref[...] + h
        out = _layer_norm_rows(pre, g_ref[...], b_ref[...])
        pre_ref[...] = pre
        o_ref[...] = out
        obf_ref[...] = out.astype(BF16)

    row = lambda i: (i, 0)
    fix = lambda i: (0, 0)
    return pl.pallas_call(
        body, name=name, grid=(s // tm,),
        out_shape=(jax.ShapeDtypeStruct((s, D_MODEL), F32), jax.ShapeDtypeStruct((s, D_MODEL), F32),
                   jax.ShapeDtypeStruct((s, D_MODEL), BF16)),
        in_specs=[pl.BlockSpec((tm, kdim), row), pl.BlockSpec((kdim, D_MODEL), fix), pl.BlockSpec((tm, D_MODEL), row),
                  pl.BlockSpec((1, D_MODEL), fix), pl.BlockSpec((1, D_MODEL), fix)],
        out_specs=(pl.BlockSpec((tm, D_MODEL), row),) * 3,
        compiler_params=_params(("parallel",)),
    )(a, w, res, g, b)


def _ln_bwd(dout, pre, g, *, tm, name):
    s = dout.shape[0]

    def body(do_ref, pre_ref, g_ref, dy_ref, dybf_ref, dg_ref, db_ref):
        do = do_ref[...]
        y = pre_ref[...]
        mu = jnp.mean(y, axis=-1, keepdims=True)
        yc = y - mu
        var = jnp.mean(yc * yc, axis=-1, keepdims=True)
        rstd = lax.rsqrt(var + LN_EPS)
        xhat = yc * rstd
        dxhat = do * g_ref[...]
        dy = rstd * (dxhat - jnp.mean(dxhat, axis=-1, keepdims=True)
                     - xhat * jnp.mean(dxhat * xhat, axis=-1, keepdims=True))
        dy_ref[...] = dy
        dybf_ref[...] = dy.astype(BF16)
        pg = jnp.sum(do * xhat, axis=0, keepdims=True)
        pb = jnp.sum(do, axis=0, keepdims=True)

        @pl.when(pl.program_id(0) == 0)
        def _():
            dg_ref[...] = pg
            db_ref[...] = pb

        @pl.when(pl.program_id(0) > 0)
        def _():
            dg_ref[...] += pg
            db_ref[...] += pb

    row = lambda i: (i, 0)
    fix = lambda i: (0, 0)
    return pl.pallas_call(
        body, name=name, grid=(s // tm,),
        out_shape=(jax.ShapeDtypeStruct((s, D_MODEL), F32), jax.ShapeDtypeStruct((s, D_MODEL), BF16),
                   jax.ShapeDtypeStruct((1, D_MODEL), F32), jax.ShapeDtypeStruct((1, D_MODEL), F32)),
        in_specs=[pl.BlockSpec((tm, D_MODEL), row), pl.BlockSpec((tm, D_MODEL), row), pl.BlockSpec((1, D_MODEL), fix)],
        out_specs=(pl.BlockSpec((tm, D_MODEL), row), pl.BlockSpec((tm, D_MODEL), row),
                   pl.BlockSpec((1, D_MODEL), fix), pl.BlockSpec((1, D_MODEL), fix)),
        compiler_params=_params(("arbitrary",)),
    )(dout, pre, g)


def _ffn_in(xbf, w, *, tm, tn, name):
    s = xbf.shape[0]
    nj = D_FF // tn

    def body(x_ref, wg_ref, wu_ref, g_ref, u_ref, act_ref):
        xv = x_ref[...]
        gg = jnp.dot(xv, wg_ref[...], preferred_element_type=F32)
        uu = jnp.dot(xv, wu_ref[...], preferred_element_type=F32)
        g_ref[...] = gg.astype(BF16)
        u_ref[...] = uu.astype(BF16)
        act_ref[...] = (gg * _sigmoid(gg) * uu).astype(BF16)

    out = jax.ShapeDtypeStruct((s, D_FF), BF16)
    tile = pl.BlockSpec((tm, tn), lambda j, i: (i, j))
    return pl.pallas_call(
        body, name=name, grid=(nj, s // tm),
        out_shape=(out, out, out),
        in_specs=[pl.BlockSpec((tm, D_MODEL), lambda j, i: (i, 0)), pl.BlockSpec((D_MODEL, tn), lambda j, i: (0, j)),
                  pl.BlockSpec((D_MODEL, tn), lambda j, i: (0, j + nj))],
        out_specs=(tile, tile, tile),
        compiler_params=_params(("parallel", "parallel")),
    )(xbf, w, w)


def _ffn_bwd_act(dybf, w_out, g, u, *, tm, tn, name):
    s = dybf.shape[0]

    def body(dy_ref, w_ref, g_ref, u_ref, dg_ref, du_ref):
        dact = lax.dot_general(dy_ref[...], w_ref[...], (((1,), (1,)), ((), ())), preferred_element_type=F32)
        gg = g_ref[...].astype(F32)
        uu = u_ref[...].astype(F32)
        sg = _sigmoid(gg)
        dg_ref[...] = (dact * uu * sg * (1.0 + gg * (1.0 - sg))).astype(BF16)
        du_ref[...] = (dact * gg * sg).astype(BF16)

    out = jax.ShapeDtypeStruct((s, D_FF), BF16)
    tile = pl.BlockSpec((tm, tn), lambda j, i: (i, j))
    return pl.pallas_call(
        body, name=name, grid=(D_FF // tn, s // tm),
        out_shape=(out, out),
        in_specs=[pl.BlockSpec((tm, D_MODEL), lambda j, i: (i, 0)), pl.BlockSpec((tn, D_MODEL), lambda j, i: (j, 0)),
                  tile, tile],
        out_specs=(tile, tile),
        compiler_params=_params(("parallel", "parallel")),
    )(dybf, w_out, g, u)


def _ple_fwd(x2, x2bf, p, w_gate, w_proj, *, tm, name):
    s = x2.shape[0]

    def body(x_ref, xbf_ref, p_ref, wg_ref, wp_ref, o_ref, obf_ref):
        a = jnp.dot(xbf_ref[...], wg_ref[...], preferred_element_type=F32)
        pp = jnp.dot(p_ref[...].astype(BF16), wp_ref[...], preferred_element_type=F32)
        out = x_ref[...] + _sigmoid(a) * pp
        o_ref[...] = out
        obf_ref[...] = out.astype(BF16)

    row = lambda i: (i, 0)
    fix = lambda i: (0, 0)
    return pl.pallas_call(
        body, name=name, grid=(s // tm,),
        out_shape=(jax.ShapeDtypeStruct((s, D_MODEL), F32), jax.ShapeDtypeStruct((s, D_MODEL), BF16)),
        in_specs=[pl.BlockSpec((tm, D_MODEL), row), pl.BlockSpec((tm, D_MODEL), row), pl.BlockSpec((tm, PLE_DIM), row),
                  pl.BlockSpec((D_MODEL, D_MODEL), fix), pl.BlockSpec((PLE_DIM, D_MODEL), fix)],
        out_specs=(pl.BlockSpec((tm, D_MODEL), row), pl.BlockSpec((tm, D_MODEL), row)),
        compiler_params=_params(("parallel",)),
    )(x2, x2bf, p, w_gate, w_proj)


def _ple_bwd(dx3, x2bf, p, w_gate, w_proj, *, tm, name):
    s = dx3.shape[0]

    def body(d_ref, xbf_ref, p_ref, wg_ref, wp_ref, dx_ref, da_ref, dpp_ref):
        d = d_ref[...]
        a = jnp.dot(xbf_ref[...], wg_ref[...], preferred_element_type=F32)
        pp = jnp.dot(p_ref[...].astype(BF16), wp_ref[...], preferred_element_type=F32)
        sg = _sigmoid(a)
        da = (d * pp * sg * (1.0 - sg)).astype(BF16)
        da_ref[...] = da
        dpp_ref[...] = (d * sg).astype(BF16)
        dx_ref[...] = d + lax.dot_general(da, wg_ref[...], (((1,), (1,)), ((), ())), preferred_element_type=F32)

    row = lambda i: (i, 0)
    fix = lambda i: (0, 0)
    return pl.pallas_call(
        body, name=name, grid=(s // tm,),
        out_shape=(jax.ShapeDtypeStruct((s, D_MODEL), F32), jax.ShapeDtypeStruct((s, D_MODEL), BF16),
                   jax.ShapeDtypeStruct((s, D_MODEL), BF16)),
        in_specs=[pl.BlockSpec((tm, D_MODEL), row), pl.BlockSpec((tm, D_MODEL), row), pl.BlockSpec((tm, PLE_DIM), row),
                  pl.BlockSpec((D_MODEL, D_MODEL), fix), pl.BlockSpec((PLE_DIM, D_MODEL), fix)],
        out_specs=(pl.BlockSpec((tm, D_MODEL), row),) * 3,
        compiler_params=_params(("parallel",)),
    )(dx3, x2bf, p, w_gate, w_proj)


def _loss_head(y, target, *, tm, name):
    s = y.shape[0]

    def body(y_ref, t_ref, dy_ref, loss_ref, acc_ref):
        err = y_ref[...] - t_ref[...]
        dy_ref[...] = err * (1.0 / D_MODEL)
        part = jnp.sum(err * err, axis=0, keepdims=True)

        @pl.when(pl.program_id(0) == 0)
        def _():
            acc_ref[...] = part

        @pl.when(pl.program_id(0) > 0)
        def _():
            acc_ref[...] += part

        @pl.when(pl.program_id(0) == pl.num_programs(0) - 1)
        def _():
            tot = jnp.sum(acc_ref[...], axis=1, keepdims=True) * (0.5 / D_MODEL)
            loss_ref[...] = jnp.broadcast_to(tot, (8, LANES))

    row = lambda i: (i, 0)
    return pl.pallas_call(
        body, name=name, grid=(s // tm,),
        out_shape=(jax.ShapeDtypeStruct((s, D_MODEL), F32), jax.ShapeDtypeStruct((8, LANES), F32)),
        in_specs=[pl.BlockSpec((tm, D_MODEL), row), pl.BlockSpec((tm, D_MODEL), row)],
        out_specs=(pl.BlockSpec((tm, D_MODEL), row), pl.BlockSpec((8, LANES), lambda i: (0, 0))),
        scratch_shapes=[pltpu.VMEM((1, D_MODEL), F32)],
        compiler_params=_params(("arbitrary",)),
    )(y, target)


def _rope_tables(s):
    inv = ROPE_THETA ** (-jnp.arange(0, ROPE_DIM, 2, dtype=F32) / ROPE_DIM)
    ang = jnp.arange(s, dtype=F32)[:, None] * inv[None, :]
    cos, sin = jnp.cos(ang), jnp.sin(ang)
    ones = jnp.ones((s, HEAD_DIM - ROPE_DIM), F32)
    c_head = jnp.concatenate([cos, cos, ones], axis=1)
    s_head = jnp.concatenate([-sin, sin, 0.0 * ones], axis=1)
    return jnp.concatenate([c_head, c_head], axis=1), jnp.concatenate([s_head, s_head], axis=1)


def _rope(v, cos, sin):
    n = v.shape[1] // LANES
    width = v.shape[1]
    cos_w = jnp.tile(cos, (1, n)) if n > 1 else cos
    sin_w = jnp.tile(sin, (1, n)) if n > 1 else sin
    dim = lax.broadcasted_iota(jnp.int32, (1, width), 1) % HEAD_DIM
    partner = jnp.where(dim < ROPE_DIM // 2, pltpu.roll(v, width - ROPE_DIM // 2, 1), pltpu.roll(v, ROPE_DIM // 2, 1))
    return v * cos_w + partner * sin_w


def _unrope(dv, cos, sin):
    n = dv.shape[1] // LANES
    width = dv.shape[1]
    cos_w = jnp.tile(cos, (1, n)) if n > 1 else cos
    sin_w = jnp.tile(sin, (1, n)) if n > 1 else sin
    t = dv * sin_w
    dim = lax.broadcasted_iota(jnp.int32, (1, width), 1) % HEAD_DIM
    partner = jnp.where(dim < ROPE_DIM // 2, pltpu.roll(t, width - ROPE_DIM // 2, 1),
                        jnp.where(dim < ROPE_DIM, pltpu.roll(t, ROPE_DIM // 2, 1), 0.0))
    return dv * cos_w + partner


def _att_mask(i, nb):
    rows = GROUP * ATT_BLOCK
    r = lax.broadcasted_iota(jnp.int32, (rows, 3 * ATT_BLOCK), 0) % ATT_BLOCK
    cidx = lax.broadcasted_iota(jnp.int32, (rows, 3 * ATT_BLOCK), 1)
    rel = r + ATT_BLOCK - cidx
    ok = (rel <= ATT_BLOCK) & (rel >= -ATT_BLOCK)
    ok = ok & ((cidx >= ATT_BLOCK) | (i > 0)) & ((cidx < 2 * ATT_BLOCK) | (i < nb - 1))
    return ok


def _half_mask(half):
    lane = lax.broadcasted_iota(jnp.int32, (1, LANES), 1)
    return (lane // HEAD_DIM) == half


def _stack_q(q, h):
    parts = []
    for gq in range(GROUP):
        n = GROUP * h + gq
        grp = q[:, LANES * (n // 2):LANES * (n // 2 + 1)]
        grp = jnp.where(_half_mask(n % 2), grp, 0.0)
        if n % 2 != h % 2:
            grp = pltpu.roll(grp, HEAD_DIM, 1)
        parts.append(grp)
    return jnp.concatenate(parts, axis=0)


def _unstack_q(stacked, h, acc):
    for gq in range(GROUP):
        n = GROUP * h + gq
        grp = stacked[ATT_BLOCK * gq:ATT_BLOCK * (gq + 1), :]
        grp = jnp.where(_half_mask(h % 2), grp, 0.0)
        if n % 2 != h % 2:
            grp = pltpu.roll(grp, HEAD_DIM, 1)
        acc[n // 2] = grp if acc[n // 2] is None else acc[n // 2] + grp
    return acc


def _sink_rows(sink_ref, h):
    rows = GROUP * ATT_BLOCK
    grp = lax.broadcasted_iota(jnp.int32, (rows, 1), 0) // ATT_BLOCK
    out = jnp.zeros((rows, 1), F32)
    for gq in range(GROUP):
        out = jnp.where(grp == gq, sink_ref[GROUP * h + gq], out)
    return out


def _att_probs(qs, kh, sink, valid):
    s = lax.dot_general(qs, kh, (((1,), (1,)), ((), ())), preferred_element_type=F32)
    s = jnp.where(valid, s, NEG)
    m = jnp.maximum(jnp.max(s, axis=-1, keepdims=True), sink)
    p = jnp.exp(s - m)
    es = jnp.exp(sink - m)
    den = jnp.sum(p, axis=-1, keepdims=True) + es
    inv = 1.0 / den
    return p * inv, es * inv


def _att_specs(nb):
    prev = lambda i: (jnp.maximum(i - 1, 0), 0)
    cur = lambda i: (i, 0)
    nxt = lambda i: (jnp.minimum(i + 1, nb - 1), 0)
    kv = lambda f: (lambda i: (f(i)[0], 2))
    tab = [pl.BlockSpec((ATT_BLOCK, LANES), f) for f in (cur, prev, cur, nxt)]
    z = [pl.BlockSpec((ATT_BLOCK, D_MODEL), cur)] + [pl.BlockSpec((ATT_BLOCK, 2 * KV_DIM), kv(f)) for f in (prev, cur, nxt)]
    return z, tab


def _att_load(zq_ref, kp_ref, kc_ref, kn_ref, cq_ref, sq_ref, cp_ref, sp_ref, cc_ref, sc_ref, cn_ref, sn_ref):
    q = (_rope(zq_ref[...], cq_ref[...], sq_ref[...]) * (HEAD_DIM ** -0.5))
    ks, vs = [], []
    for ref, c_ref, s_ref in ((kp_ref, cp_ref, sp_ref), (kc_ref, cc_ref, sc_ref), (kn_ref, cn_ref, sn_ref)):
        kvb = ref[...]
        ks.append(_rope(kvb[:, :KV_DIM], c_ref[...], s_ref[...]))
        vs.append(kvb[:, KV_DIM:])
    return q, jnp.concatenate(ks, axis=0).astype(BF16), jnp.concatenate(vs, axis=0).astype(BF16)


def _att_fwd(z, sink, cos, sin, *, name):
    s = z.shape[0]
    nb = s // ATT_BLOCK

    def body(zq_ref, kp_ref, kc_ref, kn_ref, cq_ref, cp_ref, cc_ref, cn_ref, sq_ref, sp_ref, sc_ref, sn_ref, sink_ref,
             o_ref):
        i = pl.program_id(0)
        q, k, v = _att_load(zq_ref, kp_ref, kc_ref, kn_ref, cq_ref, sq_ref, cp_ref, sp_ref, cc_ref, sc_ref, cn_ref, sn_ref)
        valid = _att_mask(i, nb)
        acc = [None] * (N_Q_HEADS // 2)
        for h in range(N_KV_HEADS):
            lanes = slice(LANES * (h // 2), LANES * (h // 2 + 1))
            qs = _stack_q(q, h).astype(BF16)
            prob, _ = _att_probs(qs, k[:, lanes], _sink_rows(sink_ref, h), valid)
            oh = jnp.dot(prob.astype(BF16), v[:, lanes], preferred_element_type=F32)
            acc = _unstack_q(oh, h, acc)
        o_ref[...] = jnp.concatenate(acc, axis=1).astype(BF16)

    zspecs, tab = _att_specs(nb)
    return pl.pallas_call(
        body, name=name, grid=(nb,),
        out_shape=jax.ShapeDtypeStruct((s, D_MODEL), BF16),
        in_specs=zspecs + tab + tab + [pl.BlockSpec(memory_space=pltpu.SMEM)],
        out_specs=pl.BlockSpec((ATT_BLOCK, D_MODEL), lambda i: (i, 0)),
        compiler_params=_params(("parallel",)),
    )(z, z, z, z, cos, cos, cos, cos, sin, sin, sin, sin, sink)


def _att_bwd(z, do, sink, cos, sin, *, name):
    s = z.shape[0]
    nb = s // ATT_BLOCK

    def body(zq_ref, kp_ref, kc_ref, kn_ref, cq_ref, cp_ref, cc_ref, cn_ref, sq_ref, sp_ref, sc_ref, sn_ref, sink_ref,
             do_ref, dq_ref, part_ref, dsink_ref):
        i = pl.program_id(0)
        q, k, v = _att_load(zq_ref, kp_ref, kc_ref, kn_ref, cq_ref, sq_ref, cp_ref, sp_ref, cc_ref, sc_ref, cn_ref, sn_ref)
        valid = _att_mask(i, nb)
        dout = do_ref[...].astype(F32)
        dq_acc = [None] * (N_Q_HEADS // 2)
        dk_acc = [None] * 2
        dv_acc = [None] * 2
        ds_rows = []
        for h in range(N_KV_HEADS):
            grp = h // 2
            lanes = slice(LANES * grp, LANES * (grp + 1))
            qs = _stack_q(q, h).astype(BF16)
            dos = _stack_q(dout, h).astype(BF16)
            prob, psink = _att_probs(qs, k[:, lanes], _sink_rows(sink_ref, h), valid)
            dprob = lax.dot_general(dos, v[:, lanes], (((1,), (1,)), ((), ())), preferred_element_type=F32)
            delta = jnp.sum(prob * dprob, axis=-1, keepdims=True)
            dsc = (prob * (dprob - delta)).astype(BF16)
            ds_rows.append(-psink * delta)
            dqs = jnp.dot(dsc, k[:, lanes], preferred_element_type=F32)
            dq_acc = _unstack_q(dqs, h, dq_acc)
            dkh = lax.dot_general(dsc, qs, (((0,), (0,)), ((), ())), preferred_element_type=F32)
            dvh = lax.dot_general(prob.astype(BF16), dos, (((0,), (0,)), ((), ())), preferred_element_type=F32)
            dk_acc[grp] = dkh if dk_acc[grp] is None else dk_acc[grp] + dkh
            dv_acc[grp] = dvh if dv_acc[grp] is None else dv_acc[grp] + dvh
        dq = jnp.concatenate(dq_acc, axis=1) * (HEAD_DIM ** -0.5)
        dq_ref[...] = _unrope(dq, cq_ref[...], sq_ref[...]).astype(BF16)
        part = jnp.concatenate(dk_acc + dv_acc, axis=1)
        for wdw in range(3):
            part_ref[wdw] = part[ATT_BLOCK * wdw:ATT_BLOCK * (wdw + 1), :]
        rows = []
        for h in range(N_KV_HEADS):
            for gq in range(GROUP):
                tot = jnp.sum(ds_rows[h][ATT_BLOCK * gq:ATT_BLOCK * (gq + 1), :], axis=0, keepdims=True)
                rows.append(jnp.broadcast_to(tot, (1, LANES)))
        dsink = jnp.concatenate(rows, axis=0)

        @pl.when(i == 0)
        def _():
            dsink_ref[...] = dsink

        @pl.when(i > 0)
        def _():
            dsink_ref[...] += dsink

    zspecs, tab = _att_specs(nb)
    return pl.pallas_call(
        body, name=name, grid=(nb,),
        out_shape=(jax.ShapeDtypeStruct((s, D_MODEL), BF16), jax.ShapeDtypeStruct((nb, 3, ATT_BLOCK, 2 * KV_DIM), F32),
                   jax.ShapeDtypeStruct((N_Q_HEADS, LANES), F32)),
        in_specs=zspecs + tab + tab + [pl.BlockSpec(memory_space=pltpu.SMEM), pl.BlockSpec((ATT_BLOCK, D_MODEL), lambda i: (i, 0))],
        out_specs=(pl.BlockSpec((ATT_BLOCK, D_MODEL), lambda i: (i, 0)),
                   pl.BlockSpec((None, 3, ATT_BLOCK, 2 * KV_DIM), lambda i: (i, 0, 0, 0)),
                   pl.BlockSpec((N_Q_HEADS, LANES), lambda i: (0, 0))),
        compiler_params=_params(("arbitrary",)),
    )(z, z, z, z, cos, cos, cos, cos, sin, sin, sin, sin, sink, do)


def _att_bwd_kv(part, cos, sin, *, name):
    nb = part.shape[0]

    def body(pn_ref, pc_ref, pp_ref, c_ref, s_ref, o_ref):
        j = pl.program_id(0)
        tot = pc_ref[...]
        tot = tot + jnp.where(j < nb - 1, pn_ref[...], 0.0)
        tot = tot + jnp.where(j > 0, pp_ref[...], 0.0)
        dk = _unrope(tot[:, :KV_DIM], c_ref[...], s_ref[...])
        o_ref[...] = jnp.concatenate([dk, tot[:, KV_DIM:]], axis=1).astype(BF16)

    blk = (None, None, ATT_BLOCK, 2 * KV_DIM)
    return pl.pallas_call(
        body, name=name, grid=(nb,),
        out_shape=jax.ShapeDtypeStruct((nb * ATT_BLOCK, 2 * KV_DIM), BF16),
        in_specs=[pl.BlockSpec(blk, lambda j: (jnp.minimum(j + 1, nb - 1), 0, 0, 0)),
                  pl.BlockSpec(blk, lambda j: (j, 1, 0, 0)),
                  pl.BlockSpec(blk, lambda j: (jnp.maximum(j - 1, 0), 2, 0, 0)),
                  pl.BlockSpec((ATT_BLOCK, LANES), lambda j: (j, 0)), pl.BlockSpec((ATT_BLOCK, LANES), lambda j: (j, 0))],
        out_specs=pl.BlockSpec((ATT_BLOCK, 2 * KV_DIM), lambda j: (j, 0)),
        compiler_params=_params(("parallel",)),
    )(part, part, part, cos, sin)


def _bdot(a, b, dims):
    return lax.dot_general(a.astype(BF16), b.astype(BF16), (dims, ((), ())), preferred_element_type=F32)


@jax.custom_vjp
def _dot_nn(a, b):
    return _bdot(a, b, ((1,), (0,)))


@jax.custom_vjp
def _dot_nt(a, b):
    return _bdot(a, b, ((1,), (1,)))


@jax.custom_vjp
def _dot_tn(a, b):
    return _bdot(a, b, ((0,), (0,)))


_dot_nn.defvjp(lambda a, b: (_dot_nn(a, b), (a, b)), lambda r, d: (_dot_nt(d, r[1]), _dot_tn(r[0], d)))
_dot_nt.defvjp(lambda a, b: (_dot_nt(a, b), (a, b)), lambda r, d: (_dot_nn(d, r[1]), _dot_tn(d, r[0])))
_dot_tn.defvjp(lambda a, b: (_dot_tn(a, b), (a, b)), lambda r, d: (_dot_nt(r[1], d), _dot_nn(r[0], d)))


def _running_sum(v, up):
    n = v.shape[0]
    rows = lax.broadcasted_iota(jnp.int32, v.shape, 0)
    sh = 1
    while sh < n:
        if up:
            v = v + jnp.where(rows < n - sh, pltpu.roll(v, n - sh, 0), 0.0)
        else:
            v = v + jnp.where(rows >= sh, pltpu.roll(v, sh, 0), 0.0)
        sh *= 2
    return v


@jax.custom_vjp
def _sum_down(v):
    return _running_sum(v, False)


@jax.custom_vjp
def _sum_up(v):
    return _running_sum(v, True)


_sum_down.defvjp(lambda v: (_running_sum(v, False), None), lambda _, d: (_sum_up(d),))
_sum_up.defvjp(lambda v: (_running_sum(v, True), None), lambda _, d: (_sum_down(d),))

N_SUB = HG_CHUNK // HG_SUB


def _fold_blocks(v):
    out = v[:HG_CHUNK]
    for i in range(1, N_SUB):
        out = out + v[HG_CHUNK * i:HG_CHUNK * (i + 1)]
    return out


@jax.custom_vjp
def _fold(v):
    return _fold_blocks(v)


_fold.defvjp(lambda v: (_fold_blocks(v), None), lambda _, d: (jnp.concatenate([d] * N_SUB, axis=0),))


def _hg_consts(rev):
    c, sub = HG_CHUNK, HG_SUB
    rowpos = lax.broadcasted_iota(jnp.int32, (c, HG_DIM), 0)
    rr = lax.broadcasted_iota(jnp.int32, (N_SUB * c, c), 0)
    key = lax.broadcasted_iota(jnp.int32, (N_SUB * c, c), 1)
    blk, qry = rr // c, rr % c
    if rev:
        rowpos, qry, key = c - 1 - rowpos, c - 1 - qry, c - 1 - key
    keep = (key // sub == blk) & (key <= qry)
    return keep, rowpos


def _pick(b, rowpos, t):
    return jnp.sum(jnp.where(rowpos == t, b, 0.0), axis=0, keepdims=True)


def _hg_local(zq, zf, zv, lbv, consts, dots):
    dot_nn, dot_nt, dot_tn, cum, fold = dots
    keep, rowpos = consts
    sig = _sigmoid(zf)
    f = lbv + (1.0 - lbv) * sig
    g = jnp.log(f)
    k = (1.0 - lbv) * (1.0 - sig)
    q = zq * _sigmoid(zq)
    b = cum(g)
    ends = [_pick(b, rowpos, (j + 1) * HG_SUB - 1) for j in range(N_SUB)]
    b_last = ends[-1]
    b_end = b_last
    for j in range(N_SUB - 1):
        b_end = jnp.where(rowpos // HG_SUB == j, ends[j], b_end)
    kc = k * jnp.exp(b_end - b)
    qbs = [q * jnp.exp(jnp.where(rowpos >= j * HG_SUB, b - ends[j], 0.0)) for j in range(N_SUB)]
    scores = fold(jnp.where(keep, dot_nt(jnp.concatenate(qbs, axis=0), kc), 0.0))
    return dot_nn(scores, zv), q * jnp.exp(b), k * jnp.exp(b_last - b), jnp.exp(b_last)


def _hg_chunk(zq, zf, zv, lbv, st, consts, dots):
    intra, qs, kd, dec = _hg_local(zq, zf, zv, lbv, consts, dots)
    return intra + dots[1](qs, st), dec * st + dots[2](zv, kd)


def _hg_dots(diff, rev):
    if diff:
        return _dot_nn, _dot_nt, _dot_tn, (_sum_up if rev else _sum_down), _fold
    return (lambda a, b: _bdot(a, b, ((1,), (0,))), lambda a, b: _bdot(a, b, ((1,), (1,))),
            lambda a, b: _bdot(a, b, ((0,), (0,))), lambda v: _running_sum(v, rev), _fold_blocks)


def _hg_specs(ts, nch, trow):
    tile = pl.BlockSpec((ts, HG_DIM), lambda h, t: (trow(t), h))
    mats = pl.BlockSpec((None, nch, HG_DIM, HG_DIM), lambda h, t: (h, trow(t), 0, 0))
    vecs = pl.BlockSpec((None, nch, 1, HG_DIM), lambda h, t: (h, trow(t), 0, 0))
    return tile, mats, vecs


def _time_order(nch, rev):
    return range(nch - 1, -1, -1) if rev else range(nch)


def _chunk_rows(c):
    return pl.ds(c * HG_CHUNK, HG_CHUNK)


def _hg_fwd(z, lb, *, rev, ts, name):
    s = z.shape[0]
    nt = s // ts
    nch = ts // HG_CHUNK
    fcol = HG_HEADS * (2 if rev else 1)

    def body(zq_ref, zf_ref, zv_ref, lb_ref, o_ref, st_ref, qs_ref, dec_ref, state_ref):
        @pl.when(pl.program_id(1) == 0)
        def _():
            state_ref[...] = jnp.zeros_like(state_ref)

        consts = _hg_consts(rev)
        dots = _hg_dots(False, rev)
        lbv = lb_ref[...]
        local = {}
        for c in range(nch):
            rows = _chunk_rows(c)
            zv = zv_ref[rows, :]
            intra, qs, kd, dec = _hg_local(zq_ref[rows, :], zf_ref[rows, :], zv, lbv, consts, dots)
            qs = qs.astype(BF16)
            qs_ref[rows, :] = qs
            dec_ref[c] = dec
            local[c] = (intra, qs, dec, dots[2](zv, kd))
        st = state_ref[...]
        for c in _time_order(nch, rev):
            intra, qs, dec, upd = local[c]
            st_ref[c] = st
            o_ref[_chunk_rows(c), :] = intra + _bdot(qs, st, ((1,), (1,)))
            st = dec * st + upd
        state_ref[...] = st

    trow = (lambda t: nt - 1 - t) if rev else (lambda t: t)
    col = lambda off: pl.BlockSpec((ts, HG_DIM), lambda h, t: (trow(t), off + h))
    tile, mats, vecs = _hg_specs(ts, nch, trow)
    nchunks = s // HG_CHUNK
    return pl.pallas_call(
        body, name=name, grid=(HG_HEADS, nt),
        out_shape=(jax.ShapeDtypeStruct((s, D_MODEL), F32),
                   jax.ShapeDtypeStruct((HG_HEADS, nchunks, HG_DIM, HG_DIM), F32),
                   jax.ShapeDtypeStruct((s, D_MODEL), BF16),
                   jax.ShapeDtypeStruct((HG_HEADS, nchunks, 1, HG_DIM), F32)),
        in_specs=[col(0), col(fcol), col(3 * HG_HEADS), pl.BlockSpec((None, 1, HG_DIM), lambda h, t: (h, 0, 0))],
        out_specs=(tile, mats, tile, vecs),
        scratch_shapes=[pltpu.VMEM((HG_DIM, HG_DIM), F32)],
        compiler_params=_params(("parallel", "arbitrary")),
    )(z, z, z, lb)


def _hg_bwd(z, lb, states, qs, dec, dout, addq, addv, *, rev, ts, name):
    s = z.shape[0]
    nt = s // ts
    nch = ts // HG_CHUNK
    fcol = HG_HEADS * (2 if rev else 1)
    has_add = addq is not None

    def body(*refs):
        zq_ref, zf_ref, zv_ref, lb_ref, st_ref, qs_ref, dec_ref, do_ref = refs[:8]
        aq_ref, av_ref = (refs[8], refs[9]) if has_add else (None, None)
        dq_ref, df_ref, dv_ref, dlb_ref, grad_ref = refs[-5:]

        @pl.when(pl.program_id(1) == 0)
        def _():
            grad_ref[...] = jnp.zeros_like(grad_ref)

        consts = _hg_consts(rev)
        dots = _hg_dots(True, rev)
        lbv = lb_ref[...]
        prods = {c: _bdot(do_ref[_chunk_rows(c), :], qs_ref[_chunk_rows(c), :], ((0,), (0,))) for c in range(nch)}
        gleave = {}
        gr = grad_ref[...]
        for c in reversed(_time_order(nch, rev)):
            gleave[c] = gr
            gr = dec_ref[c] * gr + prods[c]
        grad_ref[...] = gr
        dlb_blk = jnp.zeros((1, HG_DIM), F32)
        for c in range(nch):
            rows = _chunk_rows(c)
            fn = lambda a, b2, c2, d2, e2: _hg_chunk(a, b2, c2, d2, e2, consts, dots)
            _, pull = jax.vjp(fn, zq_ref[rows, :], zf_ref[rows, :], zv_ref[rows, :], lbv, st_ref[c])
            dq, df, dv, dlb, _ = pull((do_ref[rows, :], gleave[c]))
            if has_add:
                dq = dq + aq_ref[rows, :]
                dv = dv + av_ref[rows, :]
            dq_ref[rows, :] = dq
            df_ref[rows, :] = df
            dv_ref[rows, :] = dv
            dlb_blk = dlb_blk + dlb

        @pl.when(pl.program_id(1) == 0)
        def _():
            dlb_ref[...] = dlb_blk

        @pl.when(pl.program_id(1) > 0)
        def _():
            dlb_ref[...] += dlb_blk

    trow = (lambda t: t) if rev else (lambda t: nt - 1 - t)
    col = lambda off: pl.BlockSpec((ts, HG_DIM), lambda h, t: (trow(t), off + h))
    tile, mats, vecs = _hg_specs(ts, nch, trow)
    in_specs = [col(0), col(fcol), col(3 * HG_HEADS), pl.BlockSpec((None, 1, HG_DIM), lambda h, t: (h, 0, 0)),
                mats, tile, vecs, tile]
    args = [z, z, z, lb, states, qs, dec, dout]
    if has_add:
        in_specs += [tile, tile]
        args += [addq, addv]
    full = jax.ShapeDtypeStruct((s, D_MODEL), F32)
    return pl.pallas_call(
        body, name=name, grid=(HG_HEADS, nt),
        out_shape=(full, full, full, jax.ShapeDtypeStruct((HG_HEADS, 1, HG_DIM), F32)),
        in_specs=in_specs,
        out_specs=(tile, tile, tile, pl.BlockSpec((None, 1, HG_DIM), lambda h, t: (h, 0, 0))),
        scratch_shapes=[pltpu.VMEM((HG_DIM, HG_DIM), F32)],
        compiler_params=_params(("parallel", "arbitrary")),
    )(*args)


def _hg_post(of, ob, z, norm_g, *, tm, name):
    s = of.shape[0]

    def body(of_ref, ob_ref, gate_ref, ng_ref, y_ref):
        gn = ng_ref[...]
        for h in range(HG_HEADS):
            ln = slice(HG_DIM * h, HG_DIM * (h + 1))
            o = of_ref[:, ln] + ob_ref[:, ln]
            r = lax.rsqrt(jnp.mean(o * o, axis=-1, keepdims=True) + LN_EPS)
            gt = gate_ref[:, ln]
            y_ref[:, ln] = (o * r * gn * gt * _sigmoid(gt)).astype(BF16)

    row = lambda i: (i, 0)
    return pl.pallas_call(
        body, name=name, grid=(s // tm,),
        out_shape=jax.ShapeDtypeStruct((s, D_MODEL), BF16),
        in_specs=[pl.BlockSpec((tm, D_MODEL), row), pl.BlockSpec((tm, D_MODEL), row),
                  pl.BlockSpec((tm, D_MODEL), lambda i: (i, 4)), pl.BlockSpec((1, HG_DIM), lambda i: (0, 0))],
        out_specs=pl.BlockSpec((tm, D_MODEL), row),
        compiler_params=_params(("parallel",)),
    )(of, ob, z, norm_g)


def _hg_post_bwd(dy, of, ob, z, norm_g, *, tm, name):
    s = of.shape[0]

    def body(dy_ref, of_ref, ob_ref, gate_ref, ng_ref, do_ref, dgate_ref, dng_ref):
        gn = ng_ref[...]
        tot = jnp.zeros((1, HG_DIM), F32)
        for h in range(HG_HEADS):
            ln = slice(HG_DIM * h, HG_DIM * (h + 1))
            d = dy_ref[:, ln].astype(F32)
            o = of_ref[:, ln] + ob_ref[:, ln]
            r = lax.rsqrt(jnp.mean(o * o, axis=-1, keepdims=True) + LN_EPS)
            ohat = o * r
            gt = gate_ref[:, ln]
            sg = _sigmoid(gt)
            don = d * gt * sg
            dgate_ref[:, ln] = (d * ohat * gn * sg * (1.0 + gt * (1.0 - sg))).astype(BF16)
            tot = tot + jnp.sum(don * ohat, axis=0, keepdims=True)
            dohat = don * gn
            do_ref[:, ln] = r * (dohat - ohat * jnp.mean(dohat * ohat, axis=-1, keepdims=True))

        @pl.when(pl.program_id(0) == 0)
        def _():
            dng_ref[...] = tot

        @pl.when(pl.program_id(0) > 0)
        def _():
            dng_ref[...] += tot

    row = lambda i: (i, 0)
    return pl.pallas_call(
        body, name=name, grid=(s // tm,),
        out_shape=(jax.ShapeDtypeStruct((s, D_MODEL), F32), jax.ShapeDtypeStruct((s, D_MODEL), BF16),
                   jax.ShapeDtypeStruct((1, HG_DIM), F32)),
        in_specs=[pl.BlockSpec((tm, D_MODEL), row), pl.BlockSpec((tm, D_MODEL), row), pl.BlockSpec((tm, D_MODEL), row),
                  pl.BlockSpec((tm, D_MODEL), lambda i: (i, 4)), pl.BlockSpec((1, HG_DIM), lambda i: (0, 0))],
        out_specs=(pl.BlockSpec((tm, D_MODEL), row), pl.BlockSpec((tm, D_MODEL), row),
                   pl.BlockSpec((1, HG_DIM), lambda i: (0, 0))),
        compiler_params=_params(("arbitrary",)),
    )(dy, of, ob, z, norm_g)


def _lb_fwd(logits, *, name):
    w = logits.shape[1]

    def body(l_ref, o_ref):
        lg = l_ref[...]
        e = jnp.exp(lg - jnp.max(lg, axis=0, keepdims=True))
        sm = e / jnp.sum(e, axis=0, keepdims=True)
        o_ref[0:1, :] = sm[1:2]
        o_ref[1:2, :] = sm[1:2] + sm[2:3] + sm[3:4]

    return pl.pallas_call(body, name=name, out_shape=jax.ShapeDtypeStruct((2, w), F32))(logits)


def _lb_bwd(logits, dlb, *, name):
    w = logits.shape[1]

    def body(l_ref, d_ref, o_ref):
        lg = l_ref[...]
        e = jnp.exp(lg - jnp.max(lg, axis=0, keepdims=True))
        sm = e / jnp.sum(e, axis=0, keepdims=True)
        d1, d3 = d_ref[0:1, :], d_ref[1:2, :]
        dot = sm[1:2] * (d1 + d3) + (sm[2:3] + sm[3:4]) * d3
        o_ref[0:1, :] = -sm[0:1] * dot
        o_ref[1:2, :] = sm[1:2] * (d1 + d3 - dot)
        o_ref[2:3, :] = sm[2:3] * (d3 - dot)
        o_ref[3:4, :] = sm[3:4] * (d3 - dot)

    return pl.pallas_call(body, name=name, out_shape=jax.ShapeDtypeStruct((4, w), F32))(logits, dlb)


def _adamw(w, g, m, v, *, tr, name):
    rows = w.shape[0]
    parts = g.ndim == 3
    c1 = 1.0 / (1.0 - ADAM_B1 ** ADAM_STEP)
    c2 = 1.0 / (1.0 - ADAM_B2 ** ADAM_STEP)

    def body(w_ref, g_ref, m_ref, v_ref, go_ref, d_ref, mo_ref, vo_ref):
        if parts:
            gg = g_ref[0].astype(F32)
            for i in range(1, N_DEV):
                gg = gg + g_ref[i].astype(F32)
        else:
            gg = g_ref[...]
        mm = ADAM_B1 * m_ref[...] + (1.0 - ADAM_B1) * gg
        vv = ADAM_B2 * v_ref[...] + (1.0 - ADAM_B2) * (gg * gg)
        go_ref[...] = gg
        mo_ref[...] = mm
        vo_ref[...] = vv
        d_ref[...] = -ADAM_LR * ((mm * c1) / (jnp.sqrt(vv * c2) + ADAM_EPS) + ADAM_WD * w_ref[...])

    tile = pl.BlockSpec((tr, D_MODEL), lambda i: (i, 0))
    gspec = pl.BlockSpec((N_DEV, tr, D_MODEL), lambda i: (0, i, 0)) if parts else tile
    out = jax.ShapeDtypeStruct((rows, D_MODEL), F32)
    return pl.pallas_call(
        body, name=name, grid=(rows // tr,),
        out_shape=(out, out, out, out),
        in_specs=[tile, gspec, tile, tile], out_specs=(tile, tile, tile, tile),
        compiler_params=_params(("parallel",)),
    )(w, g, m, v)


def _sum8(parts, *, name):
    def body(p_ref, o_ref):
        tot = p_ref[0]
        for i in range(1, N_DEV):
            tot = tot + p_ref[i]
        o_ref[...] = tot

    return pl.pallas_call(body, name=name, out_shape=jax.ShapeDtypeStruct(parts.shape[1:], parts.dtype))(parts)


def _rows(a):
    return a.reshape(-1, D_MODEL)


def _pack_local(shards):
    return jnp.concatenate([_rows(shards[n]) for n, _ in BIG], axis=0)


def _unpack_local(packed, like):
    out, r = {}, 0
    for n, _ in BIG:
        k = like[n].size // D_MODEL
        out[n] = packed[r:r + k].reshape(like[n].shape)
        r += k
    return out


def _unpack_gathered(gathered, like):
    out, r = {}, 0
    for n, ax in BIG:
        shp = like[n].shape
        k = like[n].size // D_MODEL
        t = gathered[:, r:r + k].reshape((N_DEV,) + shp)
        t = jnp.moveaxis(t, 0, ax)
        out[n] = t.reshape(shp[:ax] + (N_DEV * shp[ax],) + shp[ax + 1:])
        r += k
    return out


def _pack_full(grads, like):
    cols = []
    for n, ax in BIG:
        shp = like[n].shape
        t = grads[n].reshape(shp[:ax] + (N_DEV, shp[ax]) + shp[ax + 1:])
        t = jnp.moveaxis(t, ax, 0)
        cols.append(t.reshape(N_DEV, -1, D_MODEL).astype(BF16))
    return jnp.concatenate(cols, axis=1)


SMALL_ROWS = 24


def _pad_row(a):
    flat = a.reshape(1, -1)
    return jnp.pad(flat, ((0, 0), (0, D_MODEL - flat.shape[1])))


def _tile(n, pref):
    return min(n, pref)


def kernel(x, p, att_w_qkv, att_sink, att_w_o, hgrn_w_in, hgrn_lb_logits, hgrn_norm_g, hgrn_w_o, ln_mix_g, ln_mix_b, ffn_w_in, ffn_w_out, ln_ffn_g, ln_ffn_b, ple_w_gate, ple_w_proj, loss_target, m_att_w_qkv, m_att_sink, m_att_w_o, m_hgrn_w_in, m_hgrn_lb_logits, m_hgrn_norm_g, m_hgrn_w_o, m_ln_mix_g, m_ln_mix_b, m_ffn_w_in, m_ffn_w_out, m_ln_ffn_g, m_ln_ffn_b, m_ple_w_gate, m_ple_w_proj, v_att_w_qkv, v_att_sink, v_att_w_o, v_hgrn_w_in, v_hgrn_lb_logits, v_hgrn_norm_g, v_hgrn_w_o, v_ln_mix_g, v_ln_mix_b, v_ffn_w_in, v_ffn_w_out, v_ln_ffn_g, v_ln_ffn_b, v_ple_w_gate, v_ple_w_proj):
    names = ["att_w_qkv", "att_sink", "att_w_o", "hgrn_w_in", "hgrn_lb_logits", "hgrn_norm_g", "hgrn_w_o", "ln_mix_g",
             "ln_mix_b", "ffn_w_in", "ffn_w_out", "ln_ffn_g", "ln_ffn_b", "ple_w_gate", "ple_w_proj"]
    w = dict(zip(names, (att_w_qkv, att_sink, att_w_o, hgrn_w_in, hgrn_lb_logits, hgrn_norm_g, hgrn_w_o, ln_mix_g,
                         ln_mix_b, ffn_w_in, ffn_w_out, ln_ffn_g, ln_ffn_b, ple_w_gate, ple_w_proj)))
    mom = dict(zip(names, (m_att_w_qkv, m_att_sink, m_att_w_o, m_hgrn_w_in, m_hgrn_lb_logits, m_hgrn_norm_g, m_hgrn_w_o,
                           m_ln_mix_g, m_ln_mix_b, m_ffn_w_in, m_ffn_w_out, m_ln_ffn_g, m_ln_ffn_b, m_ple_w_gate,
                           m_ple_w_proj)))
    var = dict(zip(names, (v_att_w_qkv, v_att_sink, v_att_w_o, v_hgrn_w_in, v_hgrn_lb_logits, v_hgrn_norm_g, v_hgrn_w_o,
                           v_ln_mix_g, v_ln_mix_b, v_ffn_w_in, v_ffn_w_out, v_ln_ffn_g, v_ln_ffn_b, v_ple_w_gate,
                           v_ple_w_proj)))
    s = x.shape[1]
    me = 4 * lax.axis_index("x") + 2 * lax.axis_index("y") + lax.axis_index("c")
    tm = _tile(s, 512)
    tbig = _tile(s, 1024)
    ts = _tile(s // 2, 512)
    x0 = x.reshape(s, D_MODEL)
    target = loss_target.reshape(s, D_MODEL)
    pl_in = p.reshape(DEPTH, s, PLE_DIM)

    big_like = {n: w[n] for n, _ in BIG}
    w_rows = _pack_local({n: w[n] for n, _ in BIG})
    full = _unpack_gathered(_gather(w_rows.astype(BF16), name="gather_weights"), big_like)
    lb_rows = jnp.pad(hgrn_lb_logits.reshape(8, HG_DIM), ((0, 0), (0, D_MODEL - HG_DIM)))
    lb_all = _gather(lb_rows, name="gather_lb")[:, :, :HG_DIM]
    logits_full = jnp.moveaxis(lb_all, 0, 1).reshape(DEPTH, 2 * D_MODEL)
    lb = _lb_fwd(logits_full, name="lb_fwd")
    cos, sin = _rope_tables(s)

    saved = []
    xf, xb = x0, x0
    for i in range(DEPTH):
        j = i // 2
        sv = {"x": xf, "xb": xb}
        if i % 2 == 0:
            z = _mm(xb, full["att_w_qkv"][j], n=D_MODEL + 2 * KV_DIM, tm=tbig, tn=512, tk=D_MODEL, name="att_in")
            sink = w["att_sink"][j]
            o = _att_fwd(z, sink, cos, sin, name="att_fwd")
            w_o = full["att_w_o"][j]
        else:
            z = _mm(xb, full["hgrn_w_in"][j], n=5 * D_MODEL, tm=tbig, tn=1024, tk=D_MODEL, name="hgrn_in")
            lbl = lb[j].reshape(2, HG_HEADS, 1, HG_DIM)
            of, st_f, qs_f, dec_f = _hg_fwd(z, lbl[0], rev=False, ts=ts, name="hgrn_fwd")
            ob, st_b, qs_b, dec_b = _hg_fwd(z, lbl[1], rev=True, ts=ts, name="hgrn_fwd_rev")
            o = _hg_post(of, ob, z, w["hgrn_norm_g"][j].reshape(1, HG_DIM), tm=tm, name="hgrn_post")
            w_o = full["hgrn_w_o"][j]
            sv.update(of=of, ob=ob, st_f=st_f, st_b=st_b, lbl=lbl, qs_f=qs_f, qs_b=qs_b, dec_f=dec_f, dec_b=dec_b)
        sv.update(z=z, o=o)
        pre1, x1, x1b = _proj_ln(o, w_o, xf, w["ln_mix_g"][i:i + 1], w["ln_mix_b"][i:i + 1], tm=tm, name="mix_out_ln")
        gg, uu, act = _ffn_in(x1b, full["ffn_w_in"][i], tm=tm, tn=FF_TILE, name="ffn_in")
        pre2, x2, x2b = _proj_ln(act, full["ffn_w_out"][i], x1, w["ln_ffn_g"][i:i + 1], w["ln_ffn_b"][i:i + 1], tm=tm,
                                 name="ffn_out_ln")
        xf, xb = _ple_fwd(x2, x2b, pl_in[i], full["ple_w_gate"][i], full["ple_w_proj"][i], tm=tm, name="ple_fwd")
        sv.update(pre1=pre1, x1=x1, x1b=x1b, g=gg, u=uu, act=act, pre2=pre2, x2b=x2b)
        saved.append(sv)

    dx, loss_blk = _loss_head(xf, target, tm=tm, name="loss_head")
    loss = lax.psum(loss_blk[0, 0], AXES)

    gfull = {n: [None] * w[n].shape[0] for n, _ in BIG}
    small = {n: [None] * DEPTH for n in ("ln_mix_g", "ln_mix_b", "ln_ffn_g", "ln_ffn_b")}
    dlb_rows = [None] * 4
    dnorm, dsink = [None] * 2, [None] * 2
    mmw = functools.partial(_mm, ta=True, tk=_tile(s, 2048), out_dtype=BF16)
    for i in reversed(range(DEPTH)):
        j = i // 2
        sv = saved[i]
        dx2, da, dpp = _ple_bwd(dx, sv["x2b"], pl_in[i], full["ple_w_gate"][i], full["ple_w_proj"][i], tm=tm,
                                name="ple_bwd")
        gfull["ple_w_gate"][i] = mmw(sv["x2b"], da, n=D_MODEL, tm=D_MODEL, tn=D_MODEL, name="dw_ple_gate")
        gfull["ple_w_proj"][i] = mmw(pl_in[i], dpp, n=D_MODEL, tm=PLE_DIM, tn=D_MODEL, name="dw_ple_proj")
        dy2, dy2b, small["ln_ffn_g"][i], small["ln_ffn_b"][i] = _ln_bwd(dx2, sv["pre2"], w["ln_ffn_g"][i:i + 1], tm=tm,
                                                                         name="ln_bwd")
        dg, du = _ffn_bwd_act(dy2b, full["ffn_w_out"][i], sv["g"], sv["u"], tm=tm, tn=FF_TILE, name="ffn_bwd_act")
        gfull["ffn_w_out"][i] = mmw(sv["act"], dy2b, n=D_MODEL, tm=FF_TILE, tn=D_MODEL, name="dw_ffn_out")
        t = _mm(dg, full["ffn_w_in"][i], n=D_MODEL, tm=tbig, tn=D_MODEL, tk=FF_TILE, tb=True, add=dy2, add_scale=ALPHA,
                name="ffn_bwd_x")
        dx1 = _mm(du, full["ffn_w_in"][i], n=D_MODEL, tm=tbig, tn=D_MODEL, tk=FF_TILE, tb=True, add=t,
                  bk_off=D_FF // FF_TILE, name="ffn_bwd_x_u")
        gfull["ffn_w_in"][i] = jnp.concatenate(
            [mmw(sv["x1b"], dg, n=D_FF, tm=D_MODEL, tn=FF_TILE, name="dw_ffn_in"),
             mmw(sv["x1b"], du, n=D_FF, tm=D_MODEL, tn=FF_TILE, name="dw_ffn_in")], axis=1)
        dy1, dy1b, small["ln_mix_g"][i], small["ln_mix_b"][i] = _ln_bwd(dx1, sv["pre1"], w["ln_mix_g"][i:i + 1], tm=tm,
                                                                         name="ln_bwd")
        if i % 2 == 0:
            w_o, w_in, n_in = full["att_w_o"][j], full["att_w_qkv"][j], D_MODEL + 2 * KV_DIM
        else:
            w_o, w_in, n_in = full["hgrn_w_o"][j], full["hgrn_w_in"][j], 5 * D_MODEL
        do = _mm(dy1b, w_o, n=D_MODEL, tm=tbig, tn=D_MODEL, tk=D_MODEL, tb=True, out_dtype=BF16, name="mix_out_bwd")
        g_o = mmw(sv["o"], dy1b, n=D_MODEL, tm=D_MODEL, tn=D_MODEL, name="dw_mix_out")
        if i % 2 == 0:
            gfull["att_w_o"][j] = g_o
            dzq, part, dsk = _att_bwd(sv["z"], do, w["att_sink"][j], cos, sin, name="att_bwd")
            dzkv = _att_bwd_kv(part, cos, sin, name="att_bwd_kv")
            dz = jnp.concatenate([dzq, dzkv], axis=1)
            dsink[j] = dsk[:, 0]
        else:
            gfull["hgrn_w_o"][j] = g_o
            dsum, dgate, dnorm[j] = _hg_post_bwd(do, sv["of"], sv["ob"], sv["z"], w["hgrn_norm_g"][j].reshape(1, HG_DIM),
                                                 tm=tm, name="hgrn_post_bwd")
            dq1, df1, dv1, dlb1 = _hg_bwd(sv["z"], sv["lbl"][0], sv["st_f"], sv["qs_f"], sv["dec_f"], dsum, None, None,
                                          rev=False, ts=ts, name="hgrn_bwd")
            dq2, df2, dv2, dlb2 = _hg_bwd(sv["z"], sv["lbl"][1], sv["st_b"], sv["qs_b"], sv["dec_b"], dsum, dq1, dv1,
                                          rev=True, ts=ts, name="hgrn_bwd_rev")
            dz = jnp.concatenate([dq2.astype(BF16), df1.astype(BF16), df2.astype(BF16), dv2.astype(BF16), dgate], axis=1)
            dlb_rows[2 * j] = dlb1.reshape(1, D_MODEL)
            dlb_rows[2 * j + 1] = dlb2.reshape(1, D_MODEL)
        dx = _mm(dz, w_in, n=D_MODEL, tm=tbig, tn=D_MODEL, tk=min(n_in, 2560), tb=True, add=dy1, add_scale=ALPHA,
                 name="mix_in_bwd")
        g_in = mmw(sv["xb"], dz, n=n_in, tm=D_MODEL, tn=512, name="dw_mix_in")
        gfull["att_w_qkv" if i % 2 == 0 else "hgrn_w_in"][j] = g_in
    grad_x = dx.reshape(x.shape)

    gstack = {n: jnp.stack(gfull[n]) for n, _ in BIG}
    recv = _exchange(_pack_full(gstack, big_like), name="exchange_grads")
    m_rows = _pack_local({n: mom[n] for n, _ in BIG})
    v_rows = _pack_local({n: var[n] for n, _ in BIG})
    outs = _adamw(w_rows, recv, m_rows, v_rows, tr=320, name="adamw_big")
    big_out = [_unpack_local(o_, big_like) for o_ in outs]

    small_rows = jnp.concatenate(
        [jnp.concatenate(small[n], axis=0) for n in ("ln_mix_g", "ln_mix_b", "ln_ffn_g", "ln_ffn_b")] + dlb_rows
        + [_pad_row(jnp.stack(dnorm)), _pad_row(jnp.stack(dsink)), jnp.zeros((2, D_MODEL), F32)], axis=0)
    small_all = _gather(small_rows, name="gather_small")
    lbw, lbm, lbv = (t.reshape(4, 2 * HG_DIM) for t in (hgrn_lb_logits, mom["hgrn_lb_logits"], var["hgrn_lb_logits"]))
    summed = _sum8(small_all, name="sum_small")
    dlb_mine = lax.dynamic_slice_in_dim(summed[16:20].reshape(2, 2, HG_HEADS, HG_DIM), me, 1, axis=2)
    dlogits = _lb_bwd(lbw, dlb_mine.reshape(2, 2 * HG_DIM), name="lb_bwd")

    def small_pack(ln4, lbt, ng, sk):
        return jnp.concatenate([ln4[n] for n in ("ln_mix_g", "ln_mix_b", "ln_ffn_g", "ln_ffn_b")]
                               + [_pad_row(lbt), _pad_row(ng), _pad_row(sk), jnp.zeros((5, D_MODEL), F32)], axis=0)

    g_small = jnp.concatenate([summed[:16], _pad_row(dlogits), summed[20:22], jnp.zeros((5, D_MODEL), F32)], axis=0)
    souts = _adamw(small_pack(w, lbw, w["hgrn_norm_g"], w["att_sink"]), g_small,
                   small_pack(mom, lbm, mom["hgrn_norm_g"], mom["att_sink"]),
                   small_pack(var, lbv, var["hgrn_norm_g"], var["att_sink"]), tr=SMALL_ROWS, name="adamw_small")

    def small_unpack(t):
        out = {n: t[4 * k:4 * k + 4] for k, n in enumerate(("ln_mix_g", "ln_mix_b", "ln_ffn_g", "ln_ffn_b"))}
        out["hgrn_lb_logits"] = t[16].reshape(hgrn_lb_logits.shape)
        out["hgrn_norm_g"] = t[17, :2 * HG_DIM].reshape(hgrn_norm_g.shape)
        out["att_sink"] = t[18, :2 * N_Q_HEADS].reshape(att_sink.shape)
        return out

    result = [loss, grad_x]
    for big_t, small_t in zip(big_out, souts):
        merged = dict(big_t)
        merged.update(small_unpack(small_t))
        result += [merged[n] for n in names]
    return tuple(result)
```

```python
import functools

import jax
import jax.numpy as jnp
from jax import lax
from jax.experimental import pallas as pl
from jax.experimental.pallas import tpu as pltpu

F32 = jnp.float32
BF16 = jnp.bfloat16

D_MODEL = 1024
DEPTH = 4
HEAD_DIM = 64
N_Q_HEADS = 16
N_KV_HEADS = 4
GROUP = 4
KV_DIM = 256
ATT_BLOCK = 128
ROPE_DIM = 16
ROPE_THETA = 500000.0
HG_HEADS = 8
HG_DIM = 128
HG_CHUNK = 64
HG_SUB = 16
D_FF = 2816
FF_TILE = 1408
SUB_ROWS = 256
PLE_DIM = 256
ALPHA = (2 * DEPTH) ** 0.25
LN_EPS = 1e-5
ADAM_LR, ADAM_B1, ADAM_B2, ADAM_EPS, ADAM_WD, ADAM_STEP = 0.001, 0.9, 0.999, 1e-08, 0.01, 10

N_DEV = 8
LANES = 128
VMEM_LIMIT = 52 * 1024 * 1024
NEG = -1e30
MESH = pl.DeviceIdType.MESH
AXES = ("x", "y", "c")

BIG = (("att_w_qkv", 2), ("att_w_o", 1), ("hgrn_w_in", 2), ("hgrn_w_o", 1), ("ffn_w_in", 2), ("ffn_w_out", 1),
       ("ple_w_gate", 1), ("ple_w_proj", 2))


def _params(sem=None, vmem=VMEM_LIMIT):
    return pltpu.CompilerParams(dimension_semantics=sem, vmem_limit_bytes=vmem)


def _sigmoid(x):
    return jax.nn.sigmoid(x)


def _exchange(src, *, name):
    def body(src_ref, out_ref, send_sems, recv_sems, local_sem):
        x, y, c = lax.axis_index("x"), lax.axis_index("y"), lax.axis_index("c")
        me = 4 * x + 2 * y + c
        local = pltpu.make_async_copy(src_ref.at[me], out_ref.at[me], local_sem)
        local.start()
        copies = []
        for k in range(1, N_DEV):
            px, py, pc = x ^ (k >> 2), y ^ ((k >> 1) & 1), c ^ (k & 1)
            peer = 4 * px + 2 * py + pc
            copies.append((peer, pltpu.make_async_remote_copy(
                src_ref=src_ref.at[peer], dst_ref=out_ref.at[me], send_sem=send_sems.at[k], recv_sem=recv_sems.at[k],
                device_id=(px, py, pc), device_id_type=MESH)))
        for _, cp in copies:
            cp.start()
        for k, (peer, cp) in enumerate(copies, start=1):
            cp.wait_send()
            pltpu.make_async_remote_copy(
                src_ref=src_ref.at[peer], dst_ref=out_ref.at[peer], send_sem=send_sems.at[k], recv_sem=recv_sems.at[k],
                device_id=(x, y, c), device_id_type=MESH).wait_recv()
        local.wait()

    return pl.pallas_call(
        body, name=name,
        out_shape=jax.ShapeDtypeStruct(src.shape, src.dtype),
        in_specs=[pl.BlockSpec(memory_space=pltpu.HBM)],
        out_specs=pl.BlockSpec(memory_space=pltpu.HBM),
        scratch_shapes=[pltpu.SemaphoreType.DMA((N_DEV,)), pltpu.SemaphoreType.DMA((N_DEV,)), pltpu.SemaphoreType.DMA],
    )(src)


def _gather(src, *, name):
    def body(src_ref, out_ref, send_sems, recv_sems, local_sem):
        x, y, c = lax.axis_index("x"), lax.axis_index("y"), lax.axis_index("c")
        sibling = (x, y, 1 - c)
        chips = [(1 - x, y), (x, 1 - y), (1 - x, 1 - y)]

        def rows(px, py, pc):
            return out_ref.at[4 * px + 2 * py + pc]

        def copy(k, block, to, from_src=False):
            return pltpu.make_async_remote_copy(
                src_ref=src_ref if from_src else rows(*block), dst_ref=rows(*block), send_sem=send_sems.at[k],
                recv_sem=recv_sems.at[k], device_id=to, device_id_type=MESH)

        me = (x, y, c)
        mine = pltpu.make_async_copy(src_ref, rows(*me), local_sem)
        mine.start()
        first = [copy(0, me, sibling, from_src=True)]
        first += [copy(1 + j, me, (*chip, c), from_src=True) for j, chip in enumerate(chips)]
        for cp in first:
            cp.start()
        passed = [copy(4 + j, (*chip, c), sibling) for j, chip in enumerate(chips)]
        for j, chip in enumerate(chips):
            copy(1 + j, (*chip, c), me).wait_recv()
            passed[j].start()
        copy(0, sibling, me).wait_recv()
        for j, chip in enumerate(chips):
            copy(4 + j, (*chip, 1 - c), me).wait_recv()
        for cp in first + passed:
            cp.wait_send()
        mine.wait()

    return pl.pallas_call(
        body, name=name,
        out_shape=jax.ShapeDtypeStruct((N_DEV,) + tuple(src.shape), src.dtype),
        in_specs=[pl.BlockSpec(memory_space=pltpu.HBM)],
        out_specs=pl.BlockSpec(memory_space=pltpu.HBM),
        scratch_shapes=[pltpu.SemaphoreType.DMA((7,)), pltpu.SemaphoreType.DMA((7,)), pltpu.SemaphoreType.DMA],
    )(src)


def _mm(a, b, *, n, tm, tn, tk, ta=False, tb=False, out_dtype=F32, add=None, add_scale=1.0, bk_off=0, ln=None, name):
    m, kdim = (a.shape[1], a.shape[0]) if ta else a.shape
    nk = kdim // tk
    dims = (((0 if ta else 1,), (1 if tb else 0,)), ((), ()))

    def body(*refs):
        a_ref, b_ref = refs[:2]
        nxt = 2
        add_ref = None
        if add is not None:
            add_ref = refs[nxt]
            nxt += 1
        if ln is not None:
            pre_ref, g_ref = refs[nxt:nxt + 2]
            nxt += 2
        outs = refs[nxt:nxt + (4 if ln is not None else 1)]
        acc_ref = refs[-1] if nk > 1 else None
        part = lax.dot_general(a_ref[...].astype(BF16), b_ref[...].astype(BF16), dims, preferred_element_type=F32)

        def finish(r):
            if add_ref is not None:
                r = r + add_scale * add_ref[...]
            if ln is None:
                outs[0][...] = r.astype(out_dtype)
            else:
                _ln_bwd_store(r, pre_ref[...], g_ref[...], *outs, first=pl.program_id(0) == 0)

        if nk == 1:
            finish(part)
        else:
            k = pl.program_id(2)

            @pl.when(k == 0)
            def _():
                acc_ref[...] = part

            @pl.when(k > 0)
            def _():
                acc_ref[...] += part

            @pl.when(k == nk - 1)
            def _():
                finish(acc_ref[...])

    a_spec = pl.BlockSpec((tk, tm), lambda i, j, k: (k, i)) if ta else pl.BlockSpec((tm, tk), lambda i, j, k: (i, k))
    b_spec = (pl.BlockSpec((tn, tk), lambda i, j, k: (j, k + bk_off)) if tb
              else pl.BlockSpec((tk, tn), lambda i, j, k: (k + bk_off, j)))
    in_specs, args = [a_spec, b_spec], [a, b]
    tile = pl.BlockSpec((tm, tn), lambda i, j, k: (i, j))
    if add is not None:
        in_specs.append(tile)
        args.append(add)
    out_shape, out_specs = jax.ShapeDtypeStruct((m, n), out_dtype), tile
    if ln is not None:
        assert tn == n == D_MODEL
        vec = pl.BlockSpec((1, D_MODEL), lambda i, j, k: (0, 0))
        in_specs += [tile, vec]
        args += list(ln)
        out_shape = (jax.ShapeDtypeStruct((m, n), F32), jax.ShapeDtypeStruct((m, n), BF16),
                     jax.ShapeDtypeStruct((1, D_MODEL), F32), jax.ShapeDtypeStruct((1, D_MODEL), F32))
        out_specs = (tile, tile, vec, vec)
    return pl.pallas_call(
        body, name=name, grid=(m // tm, n // tn, nk),
        out_shape=out_shape, in_specs=in_specs, out_specs=out_specs,
        scratch_shapes=[pltpu.VMEM((tm, tn), F32)] if nk > 1 else [],
        compiler_params=_params(("arbitrary",) * 3 if ln is not None else ("parallel", "parallel", "arbitrary")),
    )(*args)


def _ln_bwd_rows(do, y, g):
    mu = jnp.mean(y, axis=-1, keepdims=True)
    yc = y - mu
    var = jnp.mean(yc * yc, axis=-1, keepdims=True)
    rstd = lax.rsqrt(var + LN_EPS)
    xhat = yc * rstd
    dxhat = do * g
    dy = rstd * (dxhat - jnp.mean(dxhat, axis=-1, keepdims=True) - xhat * jnp.mean(dxhat * xhat, axis=-1, keepdims=True))
    return dy, jnp.sum(do * xhat, axis=0, keepdims=True), jnp.sum(do, axis=0, keepdims=True)


def _accumulate(ref, val, first):
    @pl.when(first)
    def _():
        ref[...] = val

    @pl.when(jnp.logical_not(first))
    def _():
        ref[...] += val


def _ln_bwd_store(do, y, g, dy_ref, dybf_ref, dg_ref, db_ref, *, first):
    dy, pg, pb = _ln_bwd_rows(do, y, g)
    dy_ref[...] = dy
    dybf_ref[...] = dy.astype(BF16)
    _accumulate(dg_ref, pg, first)
    _accumulate(db_ref, pb, first)


def _layer_norm_rows(y, g, b):
    mu = jnp.mean(y, axis=-1, keepdims=True)
    yc = y - mu
    var = jnp.mean(yc * yc, axis=-1, keepdims=True)
    return yc * lax.rsqrt(var + LN_EPS) * g + b


def _proj_ln(a, w, res, g, b, *, tm, name):
    s, kdim = a.shape

    def body(a_ref, w_ref, res_ref, g_ref, b_ref, pre_ref, o_ref, obf_ref):
        for rs in _row_parts(tm):
            h = jnp.dot(a_ref[rs, :], w_ref[...], preferred_element_type=F32)
            pre = ALPHA * res_ref[rs, :] + h
            out = _layer_norm_rows(pre, g_ref[...], b_ref[...])
            pre_ref[rs, :] = pre
            o_ref[rs, :] = out
            obf_ref[rs, :] = out.astype(BF16)

    row = lambda i: (i, 0)
    fix = lambda i: (0, 0)
    return pl.pallas_call(
        body, name=name, grid=(s // tm,),
        out_shape=(jax.ShapeDtypeStruct((s, D_MODEL), F32), jax.ShapeDtypeStruct((s, D_MODEL), F32),
                   jax.ShapeDtypeStruct((s, D_MODEL), BF16)),
        in_specs=[pl.BlockSpec((tm, kdim), row), pl.BlockSpec((kdim, D_MODEL), fix), pl.BlockSpec((tm, D_MODEL), row),
                  pl.BlockSpec((1, D_MODEL), fix), pl.BlockSpec((1, D_MODEL), fix)],
        out_specs=(pl.BlockSpec((tm, D_MODEL), row),) * 3,
        compiler_params=_params(("parallel",)),
    )(a, w, res, g, b)


def _row_parts(tm):
    sub = min(tm, SUB_ROWS)
    return [pl.ds(r * sub, sub) for r in range(tm // sub)]


def _ffn_in(xbf, w, *, tm, tn, name):
    s = xbf.shape[0]
    nj = D_FF // tn

    def body(x_ref, wg_ref, wu_ref, g_ref, u_ref, act_ref):
        for rs in _row_parts(tm):
            xv = x_ref[rs, :]
            gg = jnp.dot(xv, wg_ref[...], preferred_element_type=F32)
            uu = jnp.dot(xv, wu_ref[...], preferred_element_type=F32)
            g_ref[rs, :] = gg.astype(BF16)
            u_ref[rs, :] = uu.astype(BF16)
            act_ref[rs, :] = (gg * _sigmoid(gg) * uu).astype(BF16)

    out = jax.ShapeDtypeStruct((s, D_FF), BF16)
    tile = pl.BlockSpec((tm, tn), lambda j, i: (i, j))
    return pl.pallas_call(
        body, name=name, grid=(nj, s // tm),
        out_shape=(out, out, out),
        in_specs=[pl.BlockSpec((tm, D_MODEL), lambda j, i: (i, 0)), pl.BlockSpec((D_MODEL, tn), lambda j, i: (0, j)),
                  pl.BlockSpec((D_MODEL, tn), lambda j, i: (0, j + nj))],
        out_specs=(tile, tile, tile),
        compiler_params=_params(("parallel", "parallel")),
    )(xbf, w, w)


def _ffn_bwd_act(dybf, w_out, g, u, *, tm, tn, name):
    s = dybf.shape[0]

    def body(dy_ref, w_ref, g_ref, u_ref, dg_ref, du_ref):
        for rs in _row_parts(tm):
            dact = lax.dot_general(dy_ref[rs, :], w_ref[...], (((1,), (1,)), ((), ())), preferred_element_type=F32)
            gg = g_ref[rs, :].astype(F32)
            uu = u_ref[rs, :].astype(F32)
            sg = _sigmoid(gg)
            dg_ref[rs, :] = (dact * uu * sg * (1.0 + gg * (1.0 - sg))).astype(BF16)
            du_ref[rs, :] = (dact * gg * sg).astype(BF16)

    out = jax.ShapeDtypeStruct((s, D_FF), BF16)
    tile = pl.BlockSpec((tm, tn), lambda j, i: (i, j))
    return pl.pallas_call(
        body, name=name, grid=(D_FF // tn, s // tm),
        out_shape=(out, out),
        in_specs=[pl.BlockSpec((tm, D_MODEL), lambda j, i: (i, 0)), pl.BlockSpec((tn, D_MODEL), lambda j, i: (j, 0)),
                  tile, tile],
        out_specs=(tile, tile),
        compiler_params=_params(("parallel", "parallel")),
    )(dybf, w_out, g, u)


def _ple_fwd(x2, x2bf, p, w_gate, w_proj, *, tm, name):
    s = x2.shape[0]

    def body(x_ref, xbf_ref, p_ref, wg_ref, wp_ref, o_ref, obf_ref):
        for rs in _row_parts(tm):
            a = jnp.dot(xbf_ref[rs, :], wg_ref[...], preferred_element_type=F32)
            pp = jnp.dot(p_ref[rs, :].astype(BF16), wp_ref[...], preferred_element_type=F32)
            out = x_ref[rs, :] + _sigmoid(a) * pp
            o_ref[rs, :] = out
            obf_ref[rs, :] = out.astype(BF16)

    row = lambda i: (i, 0)
    fix = lambda i: (0, 0)
    return pl.pallas_call(
        body, name=name, grid=(s // tm,),
        out_shape=(jax.ShapeDtypeStruct((s, D_MODEL), F32), jax.ShapeDtypeStruct((s, D_MODEL), BF16)),
        in_specs=[pl.BlockSpec((tm, D_MODEL), row), pl.BlockSpec((tm, D_MODEL), row), pl.BlockSpec((tm, PLE_DIM), row),
                  pl.BlockSpec((D_MODEL, D_MODEL), fix), pl.BlockSpec((PLE_DIM, D_MODEL), fix)],
        out_specs=(pl.BlockSpec((tm, D_MODEL), row), pl.BlockSpec((tm, D_MODEL), row)),
        compiler_params=_params(("parallel",)),
    )(x2, x2bf, p, w_gate, w_proj)


def _ple_ln_bwd(dx3, x2bf, p, w_gate, w_proj, pre, g, *, tm, name):
    s = dx3.shape[0]

    def body(d_ref, xbf_ref, p_ref, wg_ref, wp_ref, pre_ref, g_ref, da_ref, dpp_ref, dy_ref, dybf_ref, dg_ref, db_ref):
        pg = jnp.zeros((1, D_MODEL), F32)
        pb = jnp.zeros((1, D_MODEL), F32)
        for rs in _row_parts(tm):
            d = d_ref[rs, :]
            a = jnp.dot(xbf_ref[rs, :], wg_ref[...], preferred_element_type=F32)
            pp = jnp.dot(p_ref[rs, :].astype(BF16), wp_ref[...], preferred_element_type=F32)
            sg = _sigmoid(a)
            da = (d * pp * sg * (1.0 - sg)).astype(BF16)
            da_ref[rs, :] = da
            dpp_ref[rs, :] = (d * sg).astype(BF16)
            dx2 = d + lax.dot_general(da, wg_ref[...], (((1,), (1,)), ((), ())), preferred_element_type=F32)
            dy, qg, qb = _ln_bwd_rows(dx2, pre_ref[rs, :], g_ref[...])
            dy_ref[rs, :] = dy
            dybf_ref[rs, :] = dy.astype(BF16)
            pg, pb = pg + qg, pb + qb
        _accumulate(dg_ref, pg, pl.program_id(0) == 0)
        _accumulate(db_ref, pb, pl.program_id(0) == 0)

    row = lambda i: (i, 0)
    fix = lambda i: (0, 0)
    tile = pl.BlockSpec((tm, D_MODEL), row)
    vec = pl.BlockSpec((1, D_MODEL), fix)
    act = lambda dt: jax.ShapeDtypeStruct((s, D_MODEL), dt)
    return pl.pallas_call(
        body, name=name, grid=(s // tm,),
        out_shape=(act(BF16), act(BF16), act(F32), act(BF16), jax.ShapeDtypeStruct((1, D_MODEL), F32),
                   jax.ShapeDtypeStruct((1, D_MODEL), F32)),
        in_specs=[tile, tile, pl.BlockSpec((tm, PLE_DIM), row), pl.BlockSpec((D_MODEL, D_MODEL), fix),
                  pl.BlockSpec((PLE_DIM, D_MODEL), fix), tile, vec],
        out_specs=(tile, tile, tile, tile, vec, vec),
        compiler_params=_params(("arbitrary",)),
    )(dx3, x2bf, p, w_gate, w_proj, pre, g)


def _loss_head(y, target, *, tm, name):
    s = y.shape[0]

    def body(y_ref, t_ref, dy_ref, loss_ref, acc_ref):
        err = y_ref[...] - t_ref[...]
        dy_ref[...] = err * (1.0 / D_MODEL)
        part = jnp.sum(err * err, axis=0, keepdims=True)

        @pl.when(pl.program_id(0) == 0)
        def _():
            acc_ref[...] = part

        @pl.when(pl.program_id(0) > 0)
        def _():
            acc_ref[...] += part

        @pl.when(pl.program_id(0) == pl.num_programs(0) - 1)
        def _():
            tot = jnp.sum(acc_ref[...], axis=1, keepdims=True) * (0.5 / D_MODEL)
            loss_ref[...] = jnp.broadcast_to(tot, (8, LANES))

    row = lambda i: (i, 0)
    return pl.pallas_call(
        body, name=name, grid=(s // tm,),
        out_shape=(jax.ShapeDtypeStruct((s, D_MODEL), F32), jax.ShapeDtypeStruct((8, LANES), F32)),
        in_specs=[pl.BlockSpec((tm, D_MODEL), row), pl.BlockSpec((tm, D_MODEL), row)],
        out_specs=(pl.BlockSpec((tm, D_MODEL), row), pl.BlockSpec((8, LANES), lambda i: (0, 0))),
        scratch_shapes=[pltpu.VMEM((1, D_MODEL), F32)],
        compiler_params=_params(("arbitrary",)),
    )(y, target)


def _rope_tables(s):
    inv = ROPE_THETA ** (-jnp.arange(0, ROPE_DIM, 2, dtype=F32) / ROPE_DIM)
    ang = jnp.arange(s, dtype=F32)[:, None] * inv[None, :]
    cos, sin = jnp.cos(ang), jnp.sin(ang)
    ones = jnp.ones((s, HEAD_DIM - ROPE_DIM), F32)
    c_head = jnp.concatenate([cos, cos, ones], axis=1)
    s_head = jnp.concatenate([-sin, sin, 0.0 * ones], axis=1)
    return jnp.concatenate([c_head, c_head], axis=1), jnp.concatenate([s_head, s_head], axis=1)


def _rope(v, cos, sin):
    n = v.shape[1] // LANES
    width = v.shape[1]
    cos_w = jnp.tile(cos, (1, n)) if n > 1 else cos
    sin_w = jnp.tile(sin, (1, n)) if n > 1 else sin
    dim = lax.broadcasted_iota(jnp.int32, (1, width), 1) % HEAD_DIM
    partner = jnp.where(dim < ROPE_DIM // 2, pltpu.roll(v, width - ROPE_DIM // 2, 1), pltpu.roll(v, ROPE_DIM // 2, 1))
    return v * cos_w + partner * sin_w


def _unrope(dv, cos, sin):
    n = dv.shape[1] // LANES
    width = dv.shape[1]
    cos_w = jnp.tile(cos, (1, n)) if n > 1 else cos
    sin_w = jnp.tile(sin, (1, n)) if n > 1 else sin
    t = dv * sin_w
    dim = lax.broadcasted_iota(jnp.int32, (1, width), 1) % HEAD_DIM
    partner = jnp.where(dim < ROPE_DIM // 2, pltpu.roll(t, width - ROPE_DIM // 2, 1),
                        jnp.where(dim < ROPE_DIM, pltpu.roll(t, ROPE_DIM // 2, 1), 0.0))
    return dv * cos_w + partner


def _att_mask(i, nb):
    rows = GROUP * ATT_BLOCK
    r = lax.broadcasted_iota(jnp.int32, (rows, 3 * ATT_BLOCK), 0) % ATT_BLOCK
    cidx = lax.broadcasted_iota(jnp.int32, (rows, 3 * ATT_BLOCK), 1)
    rel = r + ATT_BLOCK - cidx
    ok = (rel <= ATT_BLOCK) & (rel >= -ATT_BLOCK)
    ok = ok & ((cidx >= ATT_BLOCK) | (i > 0)) & ((cidx < 2 * ATT_BLOCK) | (i < nb - 1))
    return ok


def _half_mask(half):
    lane = lax.broadcasted_iota(jnp.int32, (1, LANES), 1)
    return (lane // HEAD_DIM) == half


def _stack_q(q, h):
    parts = []
    for gq in range(GROUP):
        n = GROUP * h + gq
        grp = q[:, LANES * (n // 2):LANES * (n // 2 + 1)]
        grp = jnp.where(_half_mask(n % 2), grp, 0.0)
        if n % 2 != h % 2:
            grp = pltpu.roll(grp, HEAD_DIM, 1)
        parts.append(grp)
    return jnp.concatenate(parts, axis=0)


def _unstack_q(stacked, h, acc):
    for gq in range(GROUP):
        n = GROUP * h + gq
        grp = stacked[ATT_BLOCK * gq:ATT_BLOCK * (gq + 1), :]
        grp = jnp.where(_half_mask(h % 2), grp, 0.0)
        if n % 2 != h % 2:
            grp = pltpu.roll(grp, HEAD_DIM, 1)
        acc[n // 2] = grp if acc[n // 2] is None else acc[n // 2] + grp
    return acc


def _sink_rows(sink_ref, h):
    rows = GROUP * ATT_BLOCK
    grp = lax.broadcasted_iota(jnp.int32, (rows, 1), 0) // ATT_BLOCK
    out = jnp.zeros((rows, 1), F32)
    for gq in range(GROUP):
        out = jnp.where(grp == gq, sink_ref[GROUP * h + gq], out)
    return out


def _att_probs(qs, kh, sink, valid):
    s = lax.dot_general(qs, kh, (((1,), (1,)), ((), ())), preferred_element_type=F32)
    s = jnp.where(valid, s, NEG)
    m = jnp.maximum(jnp.max(s, axis=-1, keepdims=True), sink)
    p = jnp.exp(s - m)
    es = jnp.exp(sink - m)
    den = jnp.sum(p, axis=-1, keepdims=True) + es
    inv = 1.0 / den
    return p * inv, es * inv


def _att_specs(nb):
    prev = lambda i: (jnp.maximum(i - 1, 0), 0)
    cur = lambda i: (i, 0)
    nxt = lambda i: (jnp.minimum(i + 1, nb - 1), 0)
    kv = lambda f: (lambda i: (f(i)[0], 2))
    tab = [pl.BlockSpec((ATT_BLOCK, LANES), f) for f in (cur, prev, cur, nxt)]
    z = [pl.BlockSpec((ATT_BLOCK, D_MODEL), cur)] + [pl.BlockSpec((ATT_BLOCK, 2 * KV_DIM), kv(f)) for f in (prev, cur, nxt)]
    return z, tab


def _att_load(zq_ref, kp_ref, kc_ref, kn_ref, cq_ref, sq_ref, cp_ref, sp_ref, cc_ref, sc_ref, cn_ref, sn_ref):
    q = (_rope(zq_ref[...], cq_ref[...], sq_ref[...]) * (HEAD_DIM ** -0.5))
    ks, vs = [], []
    for ref, c_ref, s_ref in ((kp_ref, cp_ref, sp_ref), (kc_ref, cc_ref, sc_ref), (kn_ref, cn_ref, sn_ref)):
        kvb = ref[...]
        ks.append(_rope(kvb[:, :KV_DIM], c_ref[...], s_ref[...]))
        vs.append(kvb[:, KV_DIM:])
    return q, jnp.concatenate(ks, axis=0).astype(BF16), jnp.concatenate(vs, axis=0).astype(BF16)


def _att_fwd(z, sink, cos, sin, *, name):
    s = z.shape[0]
    nb = s // ATT_BLOCK

    def body(zq_ref, kp_ref, kc_ref, kn_ref, cq_ref, cp_ref, cc_ref, cn_ref, sq_ref, sp_ref, sc_ref, sn_ref, sink_ref,
             o_ref):
        i = pl.program_id(0)
        q, k, v = _att_load(zq_ref, kp_ref, kc_ref, kn_ref, cq_ref, sq_ref, cp_ref, sp_ref, cc_ref, sc_ref, cn_ref, sn_ref)
        valid = _att_mask(i, nb)
        acc = [None] * (N_Q_HEADS // 2)
        for h in range(N_KV_HEADS):
            lanes = slice(LANES * (h // 2), LANES * (h // 2 + 1))
            qs = _stack_q(q, h).astype(BF16)
            prob, _ = _att_probs(qs, k[:, lanes], _sink_rows(sink_ref, h), valid)
            oh = jnp.dot(prob.astype(BF16), v[:, lanes], preferred_element_type=F32)
            acc = _unstack_q(oh, h, acc)
        o_ref[...] = jnp.concatenate(acc, axis=1).astype(BF16)

    zspecs, tab = _att_specs(nb)
    return pl.pallas_call(
        body, name=name, grid=(nb,),
        out_shape=jax.ShapeDtypeStruct((s, D_MODEL), BF16),
        in_specs=zspecs + tab + tab + [pl.BlockSpec(memory_space=pltpu.SMEM)],
        out_specs=pl.BlockSpec((ATT_BLOCK, D_MODEL), lambda i: (i, 0)),
        compiler_params=_params(("parallel",)),
    )(z, z, z, z, cos, cos, cos, cos, sin, sin, sin, sin, sink)


def _att_bwd(z, do, sink, cos, sin, *, name):
    s = z.shape[0]
    nb = s // ATT_BLOCK

    def body(zq_ref, kp_ref, kc_ref, kn_ref, cq_ref, cp_ref, cc_ref, cn_ref, sq_ref, sp_ref, sc_ref, sn_ref, sink_ref,
             do_ref, dq_ref, part_ref, dsink_ref):
        i = pl.program_id(0)
        q, k, v = _att_load(zq_ref, kp_ref, kc_ref, kn_ref, cq_ref, sq_ref, cp_ref, sp_ref, cc_ref, sc_ref, cn_ref, sn_ref)
        valid = _att_mask(i, nb)
        dout = do_ref[...].astype(F32)
        dq_acc = [None] * (N_Q_HEADS // 2)
        dk_acc = [None] * 2
        dv_acc = [None] * 2
        ds_rows = []
        for h in range(N_KV_HEADS):
            grp = h // 2
            lanes = slice(LANES * grp, LANES * (grp + 1))
            qs = _stack_q(q, h).astype(BF16)
            dos = _stack_q(dout, h).astype(BF16)
            prob, psink = _att_probs(qs, k[:, lanes], _sink_rows(sink_ref, h), valid)
            dprob = lax.dot_general(dos, v[:, lanes], (((1,), (1,)), ((), ())), preferred_element_type=F32)
            delta = jnp.sum(prob * dprob, axis=-1, keepdims=True)
            dsc = (prob * (dprob - delta)).astype(BF16)
            ds_rows.append(-psink * delta)
            dqs = jnp.dot(dsc, k[:, lanes], preferred_element_type=F32)
            dq_acc = _unstack_q(dqs, h, dq_acc)
            dkh = lax.dot_general(dsc, qs, (((0,), (0,)), ((), ())), preferred_element_type=F32)
            dvh = lax.dot_general(prob.astype(BF16), dos, (((0,), (0,)), ((), ())), preferred_element_type=F32)
            dk_acc[grp] = dkh if dk_acc[grp] is None else dk_acc[grp] + dkh
            dv_acc[grp] = dvh if dv_acc[grp] is None else dv_acc[grp] + dvh
        dq = jnp.concatenate(dq_acc, axis=1) * (HEAD_DIM ** -0.5)
        dq_ref[...] = _unrope(dq, cq_ref[...], sq_ref[...]).astype(BF16)
        part = jnp.concatenate(dk_acc + dv_acc, axis=1)
        for wdw in range(3):
            part_ref[wdw] = part[ATT_BLOCK * wdw:ATT_BLOCK * (wdw + 1), :]
        rows = []
        for h in range(N_KV_HEADS):
            for gq in range(GROUP):
                tot = jnp.sum(ds_rows[h][ATT_BLOCK * gq:ATT_BLOCK * (gq + 1), :], axis=0, keepdims=True)
                rows.append(jnp.broadcast_to(tot, (1, LANES)))
        dsink = jnp.concatenate(rows, axis=0)

        @pl.when(i == 0)
        def _():
            dsink_ref[...] = dsink

        @pl.when(i > 0)
        def _():
            dsink_ref[...] += dsink

    zspecs, tab = _att_specs(nb)
    return pl.pallas_call(
        body, name=name, grid=(nb,),
        out_shape=(jax.ShapeDtypeStruct((s, D_MODEL), BF16), jax.ShapeDtypeStruct((nb, 3, ATT_BLOCK, 2 * KV_DIM), F32),
                   jax.ShapeDtypeStruct((N_Q_HEADS, LANES), F32)),
        in_specs=zspecs + tab + tab + [pl.BlockSpec(memory_space=pltpu.SMEM), pl.BlockSpec((ATT_BLOCK, D_MODEL), lambda i: (i, 0))],
        out_specs=(pl.BlockSpec((ATT_BLOCK, D_MODEL), lambda i: (i, 0)),
                   pl.BlockSpec((None, 3, ATT_BLOCK, 2 * KV_DIM), lambda i: (i, 0, 0, 0)),
                   pl.BlockSpec((N_Q_HEADS, LANES), lambda i: (0, 0))),
        compiler_params=_params(("arbitrary",)),
    )(z, z, z, z, cos, cos, cos, cos, sin, sin, sin, sin, sink, do)


def _att_bwd_kv(part, cos, sin, *, name):
    nb = part.shape[0]

    def body(pn_ref, pc_ref, pp_ref, c_ref, s_ref, o_ref):
        j = pl.program_id(0)
        tot = pc_ref[...]
        tot = tot + jnp.where(j < nb - 1, pn_ref[...], 0.0)
        tot = tot + jnp.where(j > 0, pp_ref[...], 0.0)
        dk = _unrope(tot[:, :KV_DIM], c_ref[...], s_ref[...])
        o_ref[...] = jnp.concatenate([dk, tot[:, KV_DIM:]], axis=1).astype(BF16)

    blk = (None, None, ATT_BLOCK, 2 * KV_DIM)
    return pl.pallas_call(
        body, name=name, grid=(nb,),
        out_shape=jax.ShapeDtypeStruct((nb * ATT_BLOCK, 2 * KV_DIM), BF16),
        in_specs=[pl.BlockSpec(blk, lambda j: (jnp.minimum(j + 1, nb - 1), 0, 0, 0)),
                  pl.BlockSpec(blk, lambda j: (j, 1, 0, 0)),
                  pl.BlockSpec(blk, lambda j: (jnp.maximum(j - 1, 0), 2, 0, 0)),
                  pl.BlockSpec((ATT_BLOCK, LANES), lambda j: (j, 0)), pl.BlockSpec((ATT_BLOCK, LANES), lambda j: (j, 0))],
        out_specs=pl.BlockSpec((ATT_BLOCK, 2 * KV_DIM), lambda j: (j, 0)),
        compiler_params=_params(("parallel",)),
    )(part, part, part, cos, sin)


def _bdot(a, b, dims):
    return lax.dot_general(a.astype(BF16), b.astype(BF16), (dims, ((), ())), preferred_element_type=F32)


@jax.custom_vjp
def _dot_nn(a, b):
    return _bdot(a, b, ((1,), (0,)))


@jax.custom_vjp
def _dot_nt(a, b):
    return _bdot(a, b, ((1,), (1,)))


@jax.custom_vjp
def _dot_tn(a, b):
    return _bdot(a, b, ((0,), (0,)))


_dot_nn.defvjp(lambda a, b: (_dot_nn(a, b), (a, b)), lambda r, d: (_dot_nt(d, r[1]), _dot_tn(r[0], d)))
_dot_nt.defvjp(lambda a, b: (_dot_nt(a, b), (a, b)), lambda r, d: (_dot_nn(d, r[1]), _dot_tn(d, r[0])))
_dot_tn.defvjp(lambda a, b: (_dot_tn(a, b), (a, b)), lambda r, d: (_dot_nt(r[1], d), _dot_nn(r[0], d)))


def _running_sum(v, up):
    n = v.shape[0]
    rows = lax.broadcasted_iota(jnp.int32, v.shape, 0)
    sh = 1
    while sh < n:
        if up:
            v = v + jnp.where(rows < n - sh, pltpu.roll(v, n - sh, 0), 0.0)
        else:
            v = v + jnp.where(rows >= sh, pltpu.roll(v, sh, 0), 0.0)
        sh *= 2
    return v


@jax.custom_vjp
def _sum_down(v):
    return _running_sum(v, False)


@jax.custom_vjp
def _sum_up(v):
    return _running_sum(v, True)


_sum_down.defvjp(lambda v: (_running_sum(v, False), None), lambda _, d: (_sum_up(d),))
_sum_up.defvjp(lambda v: (_running_sum(v, True), None), lambda _, d: (_sum_down(d),))

N_SUB = HG_CHUNK // HG_SUB


def _fold_blocks(v):
    out = v[:HG_CHUNK]
    for i in range(1, N_SUB):
        out = out + v[HG_CHUNK * i:HG_CHUNK * (i + 1)]
    return out


@jax.custom_vjp
def _fold(v):
    return _fold_blocks(v)


_fold.defvjp(lambda v: (_fold_blocks(v), None), lambda _, d: (jnp.concatenate([d] * N_SUB, axis=0),))


def _hg_consts(rev):
    c, sub = HG_CHUNK, HG_SUB
    rowpos = lax.broadcasted_iota(jnp.int32, (c, HG_DIM), 0)
    rr = lax.broadcasted_iota(jnp.int32, (N_SUB * c, c), 0)
    key = lax.broadcasted_iota(jnp.int32, (N_SUB * c, c), 1)
    blk, qry = rr // c, rr % c
    if rev:
        rowpos, qry, key = c - 1 - rowpos, c - 1 - qry, c - 1 - key
    keep = (key // sub == blk) & (key <= qry)
    return keep, rowpos


def _pick(b, rowpos, t):
    return jnp.sum(jnp.where(rowpos == t, b, 0.0), axis=0, keepdims=True)


def _hg_local(zq, zf, zv, lbv, consts, dots):
    dot_nn, dot_nt, dot_tn, cum, fold = dots
    keep, rowpos = consts
    sig = _sigmoid(zf)
    f = lbv + (1.0 - lbv) * sig
    g = jnp.log(f)
    k = (1.0 - lbv) * (1.0 - sig)
    q = zq * _sigmoid(zq)
    b = cum(g)
    ends = [_pick(b, rowpos, (j + 1) * HG_SUB - 1) for j in range(N_SUB)]
    b_last = ends[-1]
    b_end = b_last
    for j in range(N_SUB - 1):
        b_end = jnp.where(rowpos // HG_SUB == j, ends[j], b_end)
    kc = k * jnp.exp(b_end - b)
    qbs = [q * jnp.exp(jnp.where(rowpos >= j * HG_SUB, b - ends[j], 0.0)) for j in range(N_SUB)]
    scores = fold(jnp.where(keep, dot_nt(jnp.concatenate(qbs, axis=0), kc), 0.0))
    return dot_nn(scores, zv), q * jnp.exp(b), k * jnp.exp(b_last - b), jnp.exp(b_last)


def _hg_chunk(zq, zf, zv, lbv, st, consts, dots):
    intra, qs, kd, dec = _hg_local(zq, zf, zv, lbv, consts, dots)
    return intra + dots[1](qs, st), dec * st + dots[2](zv, kd)


def _hg_dots(diff, rev):
    if diff:
        return _dot_nn, _dot_nt, _dot_tn, (_sum_up if rev else _sum_down), _fold
    return (lambda a, b: _bdot(a, b, ((1,), (0,))), lambda a, b: _bdot(a, b, ((1,), (1,))),
            lambda a, b: _bdot(a, b, ((0,), (0,))), lambda v: _running_sum(v, rev), _fold_blocks)


def _hg_specs(ts, nch, trow):
    tile = pl.BlockSpec((ts, HG_DIM), lambda h, t: (trow(t), h))
    mats = pl.BlockSpec((None, nch, HG_DIM, HG_DIM), lambda h, t: (h, trow(t), 0, 0))
    vecs = pl.BlockSpec((None, nch, 1, HG_DIM), lambda h, t: (h, trow(t), 0, 0))
    return tile, mats, vecs


def _time_order(nch, rev):
    return range(nch - 1, -1, -1) if rev else range(nch)


def _chunk_rows(c):
    return pl.ds(c * HG_CHUNK, HG_CHUNK)


def _hg_fwd(z, lb, *, rev, ts, name):
    s = z.shape[0]
    nt = s // ts
    nch = ts // HG_CHUNK
    fcol = HG_HEADS * (2 if rev else 1)

    def body(zq_ref, zf_ref, zv_ref, lb_ref, o_ref, st_ref, qs_ref, dec_ref, state_ref):
        @pl.when(pl.program_id(1) == 0)
        def _():
            state_ref[...] = jnp.zeros_like(state_ref)

        consts = _hg_consts(rev)
        dots = _hg_dots(False, rev)
        lbv = lb_ref[...]
        local = {}
        for c in range(nch):
            rows = _chunk_rows(c)
            zv = zv_ref[rows, :]
            intra, qs, kd, dec = _hg_local(zq_ref[rows, :], zf_ref[rows, :], zv, lbv, consts, dots)
            qs = qs.astype(BF16)
            qs_ref[rows, :] = qs
            dec_ref[c] = dec
            local[c] = (intra, qs, dec, dots[2](zv, kd))
        st = state_ref[...]
        for c in _time_order(nch, rev):
            intra, qs, dec, upd = local[c]
            st_ref[c] = st
            o_ref[_chunk_rows(c), :] = intra + _bdot(qs, st, ((1,), (1,)))
            st = dec * st + upd
        state_ref[...] = st

    trow = (lambda t: nt - 1 - t) if rev else (lambda t: t)
    col = lambda off: pl.BlockSpec((ts, HG_DIM), lambda h, t: (trow(t), off + h))
    tile, mats, vecs = _hg_specs(ts, nch, trow)
    nchunks = s // HG_CHUNK
    return pl.pallas_call(
        body, name=name, grid=(HG_HEADS, nt),
        out_shape=(jax.ShapeDtypeStruct((s, D_MODEL), F32),
                   jax.ShapeDtypeStruct((HG_HEADS, nchunks, HG_DIM, HG_DIM), F32),
                   jax.ShapeDtypeStruct((s, D_MODEL), BF16),
                   jax.ShapeDtypeStruct((HG_HEADS, nchunks, 1, HG_DIM), F32)),
        in_specs=[col(0), col(fcol), col(3 * HG_HEADS), pl.BlockSpec((None, 1, HG_DIM), lambda h, t: (h, 0, 0))],
        out_specs=(tile, mats, tile, vecs),
        scratch_shapes=[pltpu.VMEM((HG_DIM, HG_DIM), F32)],
        compiler_params=_params(("parallel", "arbitrary")),
    )(z, z, z, lb)


def _hg_bwd(z, lb, states, qs, dec, dout, addq, addv, *, rev, ts, name):
    s = z.shape[0]
    nt = s // ts
    nch = ts // HG_CHUNK
    fcol = HG_HEADS * (2 if rev else 1)
    has_add = addq is not None

    def body(*refs):
        zq_ref, zf_ref, zv_ref, lb_ref, st_ref, qs_ref, dec_ref, do_ref = refs[:8]
        aq_ref, av_ref = (refs[8], refs[9]) if has_add else (None, None)
        dq_ref, df_ref, dv_ref, dlb_ref, grad_ref = refs[-5:]

        @pl.when(pl.program_id(1) == 0)
        def _():
            grad_ref[...] = jnp.zeros_like(grad_ref)

        consts = _hg_consts(rev)
        dots = _hg_dots(True, rev)
        lbv = lb_ref[...]
        prods = {c: _bdot(do_ref[_chunk_rows(c), :], qs_ref[_chunk_rows(c), :], ((0,), (0,))) for c in range(nch)}
        gleave = {}
        gr = grad_ref[...]
        for c in reversed(_time_order(nch, rev)):
            gleave[c] = gr
            gr = dec_ref[c] * gr + prods[c]
        grad_ref[...] = gr
        dlb_blk = jnp.zeros((1, HG_DIM), F32)
        for c in range(nch):
            rows = _chunk_rows(c)
            fn = lambda a, b2, c2, d2, e2: _hg_chunk(a, b2, c2, d2, e2, consts, dots)
            _, pull = jax.vjp(fn, zq_ref[rows, :], zf_ref[rows, :], zv_ref[rows, :], lbv, st_ref[c])
            dq, df, dv, dlb, _ = pull((do_ref[rows, :], gleave[c]))
            if has_add:
                dq = dq + aq_ref[rows, :]
                dv = dv + av_ref[rows, :]
            dq_ref[rows, :] = dq
            df_ref[rows, :] = df
            dv_ref[rows, :] = dv
            dlb_blk = dlb_blk + dlb

        @pl.when(pl.program_id(1) == 0)
        def _():
            dlb_ref[...] = dlb_blk

        @pl.when(pl.program_id(1) > 0)
        def _():
            dlb_ref[...] += dlb_blk

    trow = (lambda t: t) if rev else (lambda t: nt - 1 - t)
    col = lambda off: pl.BlockSpec((ts, HG_DIM), lambda h, t: (trow(t), off + h))
    tile, mats, vecs = _hg_specs(ts, nch, trow)
    in_specs = [col(0), col(fcol), col(3 * HG_HEADS), pl.BlockSpec((None, 1, HG_DIM), lambda h, t: (h, 0, 0)),
                mats, tile, vecs, tile]
    args = [z, z, z, lb, states, qs, dec, dout]
    if has_add:
        in_specs += [tile, tile]
        args += [addq, addv]
    full = jax.ShapeDtypeStruct((s, D_MODEL), F32)
    return pl.pallas_call(
        body, name=name, grid=(HG_HEADS, nt),
        out_shape=(full, full, full, jax.ShapeDtypeStruct((HG_HEADS, 1, HG_DIM), F32)),
        in_specs=in_specs,
        out_specs=(tile, tile, tile, pl.BlockSpec((None, 1, HG_DIM), lambda h, t: (h, 0, 0))),
        scratch_shapes=[pltpu.VMEM((HG_DIM, HG_DIM), F32)],
        compiler_params=_params(("parallel", "arbitrary")),
    )(*args)


def _hg_post(of, ob, z, norm_g, *, tm, name):
    s = of.shape[0]

    def body(of_ref, ob_ref, gate_ref, ng_ref, y_ref):
        gn = ng_ref[...]
        for h in range(HG_HEADS):
            ln = slice(HG_DIM * h, HG_DIM * (h + 1))
            o = of_ref[:, ln] + ob_ref[:, ln]
            r = lax.rsqrt(jnp.mean(o * o, axis=-1, keepdims=True) + LN_EPS)
            gt = gate_ref[:, ln]
            y_ref[:, ln] = (o * r * gn * gt * _sigmoid(gt)).astype(BF16)

    row = lambda i: (i, 0)
    return pl.pallas_call(
        body, name=name, grid=(s // tm,),
        out_shape=jax.ShapeDtypeStruct((s, D_MODEL), BF16),
        in_specs=[pl.BlockSpec((tm, D_MODEL), row), pl.BlockSpec((tm, D_MODEL), row),
                  pl.BlockSpec((tm, D_MODEL), lambda i: (i, 4)), pl.BlockSpec((1, HG_DIM), lambda i: (0, 0))],
        out_specs=pl.BlockSpec((tm, D_MODEL), row),
        compiler_params=_params(("parallel",)),
    )(of, ob, z, norm_g)


def _hg_post_bwd(dy, of, ob, z, norm_g, *, tm, name):
    s = of.shape[0]

    def body(dy_ref, of_ref, ob_ref, gate_ref, ng_ref, do_ref, dgate_ref, dng_ref):
        gn = ng_ref[...]
        tot = jnp.zeros((1, HG_DIM), F32)
        for h in range(HG_HEADS):
            ln = slice(HG_DIM * h, HG_DIM * (h + 1))
            d = dy_ref[:, ln].astype(F32)
            o = of_ref[:, ln] + ob_ref[:, ln]
            r = lax.rsqrt(jnp.mean(o * o, axis=-1, keepdims=True) + LN_EPS)
            ohat = o * r
            gt = gate_ref[:, ln]
            sg = _sigmoid(gt)
            don = d * gt * sg
            dgate_ref[:, ln] = (d * ohat * gn * sg * (1.0 + gt * (1.0 - sg))).astype(BF16)
            tot = tot + jnp.sum(don * ohat, axis=0, keepdims=True)
            dohat = don * gn
            do_ref[:, ln] = r * (dohat - ohat * jnp.mean(dohat * ohat, axis=-1, keepdims=True))

        @pl.when(pl.program_id(0) == 0)
        def _():
            dng_ref[...] = tot

        @pl.when(pl.program_id(0) > 0)
        def _():
            dng_ref[...] += tot

    row = lambda i: (i, 0)
    return pl.pallas_call(
        body, name=name, grid=(s // tm,),
        out_shape=(jax.ShapeDtypeStruct((s, D_MODEL), F32), jax.ShapeDtypeStruct((s, D_MODEL), BF16),
                   jax.ShapeDtypeStruct((1, HG_DIM), F32)),
        in_specs=[pl.BlockSpec((tm, D_MODEL), row), pl.BlockSpec((tm, D_MODEL), row), pl.BlockSpec((tm, D_MODEL), row),
                  pl.BlockSpec((tm, D_MODEL), lambda i: (i, 4)), pl.BlockSpec((1, HG_DIM), lambda i: (0, 0))],
        out_specs=(pl.BlockSpec((tm, D_MODEL), row), pl.BlockSpec((tm, D_MODEL), row),
                   pl.BlockSpec((1, HG_DIM), lambda i: (0, 0))),
        compiler_params=_params(("arbitrary",)),
    )(dy, of, ob, z, norm_g)


def _lb_fwd(logits, *, name):
    w = logits.shape[1]

    def body(l_ref, o_ref):
        lg = l_ref[...]
        e = jnp.exp(lg - jnp.max(lg, axis=0, keepdims=True))
        sm = e / jnp.sum(e, axis=0, keepdims=True)
        o_ref[0:1, :] = sm[1:2]
        o_ref[1:2, :] = sm[1:2] + sm[2:3] + sm[3:4]

    return pl.pallas_call(body, name=name, out_shape=jax.ShapeDtypeStruct((2, w), F32))(logits)


def _lb_bwd(logits, dlb, *, name):
    w = logits.shape[1]

    def body(l_ref, d_ref, o_ref):
        lg = l_ref[...]
        e = jnp.exp(lg - jnp.max(lg, axis=0, keepdims=True))
        sm = e / jnp.sum(e, axis=0, keepdims=True)
        d1, d3 = d_ref[0:1, :], d_ref[1:2, :]
        dot = sm[1:2] * (d1 + d3) + (sm[2:3] + sm[3:4]) * d3
        o_ref[0:1, :] = -sm[0:1] * dot
        o_ref[1:2, :] = sm[1:2] * (d1 + d3 - dot)
        o_ref[2:3, :] = sm[2:3] * (d3 - dot)
        o_ref[3:4, :] = sm[3:4] * (d3 - dot)

    return pl.pallas_call(body, name=name, out_shape=jax.ShapeDtypeStruct((4, w), F32))(logits, dlb)


def _adamw(w, g, m, v, *, tr, name):
    rows = w.shape[0]
    parts = g.ndim == 3
    c1 = 1.0 / (1.0 - ADAM_B1 ** ADAM_STEP)
    c2 = 1.0 / (1.0 - ADAM_B2 ** ADAM_STEP)

    def body(w_ref, g_ref, m_ref, v_ref, go_ref, d_ref, mo_ref, vo_ref):
        if parts:
            gg = g_ref[0].astype(F32)
            for i in range(1, N_DEV):
                gg = gg + g_ref[i].astype(F32)
        else:
            gg = g_ref[...]
        mm = ADAM_B1 * m_ref[...] + (1.0 - ADAM_B1) * gg
        vv = ADAM_B2 * v_ref[...] + (1.0 - ADAM_B2) * (gg * gg)
        go_ref[...] = gg
        mo_ref[...] = mm
        vo_ref[...] = vv
        d_ref[...] = -ADAM_LR * ((mm * c1) / (jnp.sqrt(vv * c2) + ADAM_EPS) + ADAM_WD * w_ref[...])

    tile = pl.BlockSpec((tr, D_MODEL), lambda i: (i, 0))
    gspec = pl.BlockSpec((N_DEV, tr, D_MODEL), lambda i: (0, i, 0)) if parts else tile
    out = jax.ShapeDtypeStruct((rows, D_MODEL), F32)
    return pl.pallas_call(
        body, name=name, grid=(rows // tr,),
        out_shape=(out, out, out, out),
        in_specs=[tile, gspec, tile, tile], out_specs=(tile, tile, tile, tile),
        compiler_params=_params(("parallel",)),
    )(w, g, m, v)


def _sum8(parts, *, name):
    def body(p_ref, o_ref):
        tot = p_ref[0]
        for i in range(1, N_DEV):
            tot = tot + p_ref[i]
        o_ref[...] = tot

    return pl.pallas_call(body, name=name, out_shape=jax.ShapeDtypeStruct(parts.shape[1:], parts.dtype))(parts)


def _rows(a):
    return a.reshape(-1, D_MODEL)


def _pack_local(shards):
    return jnp.concatenate([_rows(shards[n]) for n, _ in BIG], axis=0)


def _unpack_local(packed, like):
    out, r = {}, 0
    for n, _ in BIG:
        k = like[n].size // D_MODEL
        out[n] = packed[r:r + k].reshape(like[n].shape)
        r += k
    return out


def _unpack_gathered(gathered, like):
    out, r = {}, 0
    for n, ax in BIG:
        shp = like[n].shape
        k = like[n].size // D_MODEL
        t = gathered[:, r:r + k].reshape((N_DEV,) + shp)
        t = jnp.moveaxis(t, 0, ax)
        out[n] = t.reshape(shp[:ax] + (N_DEV * shp[ax],) + shp[ax + 1:])
        r += k
    return out


def _pack_full(grads, like):
    cols = []
    for n, ax in BIG:
        shp = like[n].shape
        t = grads[n].reshape(shp[:ax] + (N_DEV, shp[ax]) + shp[ax + 1:])
        t = jnp.moveaxis(t, ax, 0)
        cols.append(t.reshape(N_DEV, -1, D_MODEL).astype(BF16))
    return jnp.concatenate(cols, axis=1)


SMALL_ROWS = 24


def _pad_row(a):
    flat = a.reshape(1, -1)
    return jnp.pad(flat, ((0, 0), (0, D_MODEL - flat.shape[1])))


def _tile(n, pref):
    return min(n, pref)


def kernel(x, p, att_w_qkv, att_sink, att_w_o, hgrn_w_in, hgrn_lb_logits, hgrn_norm_g, hgrn_w_o, ln_mix_g, ln_mix_b, ffn_w_in, ffn_w_out, ln_ffn_g, ln_ffn_b, ple_w_gate, ple_w_proj, loss_target, m_att_w_qkv, m_att_sink, m_att_w_o, m_hgrn_w_in, m_hgrn_lb_logits, m_hgrn_norm_g, m_hgrn_w_o, m_ln_mix_g, m_ln_mix_b, m_ffn_w_in, m_ffn_w_out, m_ln_ffn_g, m_ln_ffn_b, m_ple_w_gate, m_ple_w_proj, v_att_w_qkv, v_att_sink, v_att_w_o, v_hgrn_w_in, v_hgrn_lb_logits, v_hgrn_norm_g, v_hgrn_w_o, v_ln_mix_g, v_ln_mix_b, v_ffn_w_in, v_ffn_w_out, v_ln_ffn_g, v_ln_ffn_b, v_ple_w_gate, v_ple_w_proj):
    names = ["att_w_qkv", "att_sink", "att_w_o", "hgrn_w_in", "hgrn_lb_logits", "hgrn_norm_g", "hgrn_w_o", "ln_mix_g",
             "ln_mix_b", "ffn_w_in", "ffn_w_out", "ln_ffn_g", "ln_ffn_b", "ple_w_gate", "ple_w_proj"]
    w = dict(zip(names, (att_w_qkv, att_sink, att_w_o, hgrn_w_in, hgrn_lb_logits, hgrn_norm_g, hgrn_w_o, ln_mix_g,
                         ln_mix_b, ffn_w_in, ffn_w_out, ln_ffn_g, ln_ffn_b, ple_w_gate, ple_w_proj)))
    mom = dict(zip(names, (m_att_w_qkv, m_att_sink, m_att_w_o, m_hgrn_w_in, m_hgrn_lb_logits, m_hgrn_norm_g, m_hgrn_w_o,
                           m_ln_mix_g, m_ln_mix_b, m_ffn_w_in, m_ffn_w_out, m_ln_ffn_g, m_ln_ffn_b, m_ple_w_gate,
                           m_ple_w_proj)))
    var = dict(zip(names, (v_att_w_qkv, v_att_sink, v_att_w_o, v_hgrn_w_in, v_hgrn_lb_logits, v_hgrn_norm_g, v_hgrn_w_o,
                           v_ln_mix_g, v_ln_mix_b, v_ffn_w_in, v_ffn_w_out, v_ln_ffn_g, v_ln_ffn_b, v_ple_w_gate,
                           v_ple_w_proj)))
    s = x.shape[1]
    me = 4 * lax.axis_index("x") + 2 * lax.axis_index("y") + lax.axis_index("c")
    tm = _tile(s, 512)
    tbig = _tile(s, 1024)
    ts = _tile(s // 2, 512)
    x0 = x.reshape(s, D_MODEL)
    target = loss_target.reshape(s, D_MODEL)
    pl_in = p.reshape(DEPTH, s, PLE_DIM)

    big_like = {n: w[n] for n, _ in BIG}
    w_rows = _pack_local({n: w[n] for n, _ in BIG})
    full = _unpack_gathered(_gather(w_rows.astype(BF16), name="gather_weights"), big_like)
    lb_rows = jnp.pad(hgrn_lb_logits.reshape(8, HG_DIM), ((0, 0), (0, D_MODEL - HG_DIM)))
    lb_all = _gather(lb_rows, name="gather_lb")[:, :, :HG_DIM]
    logits_full = jnp.moveaxis(lb_all, 0, 1).reshape(DEPTH, 2 * D_MODEL)
    lb = _lb_fwd(logits_full, name="lb_fwd")
    cos, sin = _rope_tables(s)

    saved = []
    xf, xb = x0, x0
    for i in range(DEPTH):
        j = i // 2
        sv = {"x": xf, "xb": xb}
        if i % 2 == 0:
            z = _mm(xb, full["att_w_qkv"][j], n=D_MODEL + 2 * KV_DIM, tm=tbig, tn=512, tk=D_MODEL, name="att_in")
            sink = w["att_sink"][j]
            o = _att_fwd(z, sink, cos, sin, name="att_fwd")
            w_o = full["att_w_o"][j]
        else:
            z = _mm(xb, full["hgrn_w_in"][j], n=5 * D_MODEL, tm=tbig, tn=1024, tk=D_MODEL, name="hgrn_in")
            lbl = lb[j].reshape(2, HG_HEADS, 1, HG_DIM)
            of, st_f, qs_f, dec_f = _hg_fwd(z, lbl[0], rev=False, ts=ts, name="hgrn_fwd")
            ob, st_b, qs_b, dec_b = _hg_fwd(z, lbl[1], rev=True, ts=ts, name="hgrn_fwd_rev")
            o = _hg_post(of, ob, z, w["hgrn_norm_g"][j].reshape(1, HG_DIM), tm=tm, name="hgrn_post")
            w_o = full["hgrn_w_o"][j]
            sv.update(of=of, ob=ob, st_f=st_f, st_b=st_b, lbl=lbl, qs_f=qs_f, qs_b=qs_b, dec_f=dec_f, dec_b=dec_b)
        sv.update(z=z, o=o)
        pre1, x1, x1b = _proj_ln(o, w_o, xf, w["ln_mix_g"][i:i + 1], w["ln_mix_b"][i:i + 1], tm=tm, name="mix_out_ln")
        gg, uu, act = _ffn_in(x1b, full["ffn_w_in"][i], tm=tm, tn=FF_TILE, name="ffn_in")
        pre2, x2, x2b = _proj_ln(act, full["ffn_w_out"][i], x1, w["ln_ffn_g"][i:i + 1], w["ln_ffn_b"][i:i + 1], tm=tm,
                                 name="ffn_out_ln")
        xf, xb = _ple_fwd(x2, x2b, pl_in[i], full["ple_w_gate"][i], full["ple_w_proj"][i], tm=tm, name="ple_fwd")
        sv.update(pre1=pre1, x1=x1, x1b=x1b, g=gg, u=uu, act=act, pre2=pre2, x2b=x2b)
        saved.append(sv)

    dx, loss_blk = _loss_head(xf, target, tm=tm, name="loss_head")
    loss = lax.psum(loss_blk[0, 0], AXES)

    gfull = {n: [None] * w[n].shape[0] for n, _ in BIG}
    small = {n: [None] * DEPTH for n in ("ln_mix_g", "ln_mix_b", "ln_ffn_g", "ln_ffn_b")}
    dlb_rows = [None] * 4
    dnorm, dsink = [None] * 2, [None] * 2
    mmw = functools.partial(_mm, ta=True, tk=_tile(s, 2048), out_dtype=BF16)
    for i in reversed(range(DEPTH)):
        j = i // 2
        sv = saved[i]
        da, dpp, dy2, dy2b, small["ln_ffn_g"][i], small["ln_ffn_b"][i] = _ple_ln_bwd(
            dx, sv["x2b"], pl_in[i], full["ple_w_gate"][i], full["ple_w_proj"][i], sv["pre2"], w["ln_ffn_g"][i:i + 1],
            tm=tm, name="ple_ln_bwd")
        gfull["ple_w_gate"][i] = mmw(sv["x2b"], da, n=D_MODEL, tm=D_MODEL, tn=D_MODEL, name="dw_ple_gate")
        gfull["ple_w_proj"][i] = mmw(pl_in[i], dpp, n=D_MODEL, tm=PLE_DIM, tn=D_MODEL, name="dw_ple_proj")
        dg, du = _ffn_bwd_act(dy2b, full["ffn_w_out"][i], sv["g"], sv["u"], tm=tm, tn=FF_TILE, name="ffn_bwd_act")
        gfull["ffn_w_out"][i] = mmw(sv["act"], dy2b, n=D_MODEL, tm=FF_TILE, tn=D_MODEL, name="dw_ffn_out")
        t = _mm(dg, full["ffn_w_in"][i], n=D_MODEL, tm=tbig, tn=D_MODEL, tk=FF_TILE, tb=True, add=dy2, add_scale=ALPHA,
                name="ffn_bwd_x")
        dy1, dy1b, small["ln_mix_g"][i], small["ln_mix_b"][i] = _mm(
            du, full["ffn_w_in"][i], n=D_MODEL, tm=tm, tn=D_MODEL, tk=FF_TILE, tb=True, add=t, bk_off=D_FF // FF_TILE,
            ln=(sv["pre1"], w["ln_mix_g"][i:i + 1]), name="ffn_bwd_x_u_ln")
        gfull["ffn_w_in"][i] = jnp.concatenate(
            [mmw(sv["x1b"], dg, n=D_FF, tm=D_MODEL, tn=FF_TILE, name="dw_ffn_in"),
             mmw(sv["x1b"], du, n=D_FF, tm=D_MODEL, tn=FF_TILE, name="dw_ffn_in")], axis=1)
        if i % 2 == 0:
            w_o, w_in, n_in = full["att_w_o"][j], full["att_w_qkv"][j], D_MODEL + 2 * KV_DIM
        else:
            w_o, w_in, n_in = full["hgrn_w_o"][j], full["hgrn_w_in"][j], 5 * D_MODEL
        do = _mm(dy1b, w_o, n=D_MODEL, tm=tbig, tn=D_MODEL, tk=D_MODEL, tb=True, out_dtype=BF16, name="mix_out_bwd")
        g_o = mmw(sv["o"], dy1b, n=D_MODEL, tm=D_MODEL, tn=D_MODEL, name="dw_mix_out")
        if i % 2 == 0:
            gfull["att_w_o"][j] = g_o
            dzq, part, dsk = _att_bwd(sv["z"], do, w["att_sink"][j], cos, sin, name="att_bwd")
            dzkv = _att_bwd_kv(part, cos, sin, name="att_bwd_kv")
            dz = jnp.concatenate([dzq, dzkv], axis=1)
            dsink[j] = dsk[:, 0]
        else:
            gfull["hgrn_w_o"][j] = g_o
            dsum, dgate, dnorm[j] = _hg_post_bwd(do, sv["of"], sv["ob"], sv["z"], w["hgrn_norm_g"][j].reshape(1, HG_DIM),
                                                 tm=tm, name="hgrn_post_bwd")
            dq1, df1, dv1, dlb1 = _hg_bwd(sv["z"], sv["lbl"][0], sv["st_f"], sv["qs_f"], sv["dec_f"], dsum, None, None,
                                          rev=False, ts=ts, name="hgrn_bwd")
            dq2, df2, dv2, dlb2 = _hg_bwd(sv["z"], sv["lbl"][1], sv["st_b"], sv["qs_b"], sv["dec_b"], dsum, dq1, dv1,
                                          rev=True, ts=ts, name="hgrn_bwd_rev")
            dz = jnp.concatenate([dq2.astype(BF16), df1.astype(BF16), df2.astype(BF16), dv2.astype(BF16), dgate], axis=1)
            dlb_rows[2 * j] = dlb1.reshape(1, D_MODEL)
            dlb_rows[2 * j + 1] = dlb2.reshape(1, D_MODEL)
        dx = _mm(dz, w_in, n=D_MODEL, tm=tbig, tn=D_MODEL, tk=min(n_in, 2560), tb=True, add=dy1, add_scale=ALPHA,
                 name="mix_in_bwd")
        g_in = mmw(sv["xb"], dz, n=n_in, tm=D_MODEL, tn=512, name="dw_mix_in")
        gfull["att_w_qkv" if i % 2 == 0 else "hgrn_w_in"][j] = g_in
    grad_x = dx.reshape(x.shape)

    gstack = {n: jnp.stack(gfull[n]) for n, _ in BIG}
    recv = _exchange(_pack_full(gstack, big_like), name="exchange_grads")
    m_rows = _pack_local({n: mom[n] for n, _ in BIG})
    v_rows = _pack_local({n: var[n] for n, _ in BIG})
    outs = _adamw(w_rows, recv, m_rows, v_rows, tr=320, name="adamw_big")
    big_out = [_unpack_local(o_, big_like) for o_ in outs]

    small_rows = jnp.concatenate(
        [jnp.concatenate(small[n], axis=0) for n in ("ln_mix_g", "ln_mix_b", "ln_ffn_g", "ln_ffn_b")] + dlb_rows
        + [_pad_row(jnp.stack(dnorm)), _pad_row(jnp.stack(dsink)), jnp.zeros((2, D_MODEL), F32)], axis=0)
    small_all = _gather(small_rows, name="gather_small")
    lbw, lbm, lbv = (t.reshape(4, 2 * HG_DIM) for t in (hgrn_lb_logits, mom["hgrn_lb_logits"], var["hgrn_lb_logits"]))
    summed = _sum8(small_all, name="sum_small")
    dlb_mine = lax.dynamic_slice_in_dim(summed[16:20].reshape(2, 2, HG_HEADS, HG_DIM), me, 1, axis=2)
    dlogits = _lb_bwd(lbw, dlb_mine.reshape(2, 2 * HG_DIM), name="lb_bwd")

    def small_pack(ln4, lbt, ng, sk):
        return jnp.concatenate([ln4[n] for n in ("ln_mix_g", "ln_mix_b", "ln_ffn_g", "ln_ffn_b")]
                               + [_pad_row(lbt), _pad_row(ng), _pad_row(sk), jnp.zeros((5, D_MODEL), F32)], axis=0)

    g_small = jnp.concatenate([summed[:16], _pad_row(dlogits), summed[20:22], jnp.zeros((5, D_MODEL), F32)], axis=0)
    souts = _adamw(small_pack(w, lbw, w["hgrn_norm_g"], w["att_sink"]), g_small,
                   small_pack(mom, lbm, mom["hgrn_norm_g"], mom["att_sink"]),
                   small_pack(var, lbv, var["hgrn_norm_g"], var["att_sink"]), tr=SMALL_ROWS, name="adamw_small")

    def small_unpack(t):
        out = {n: t[4 * k:4 * k + 4] for k, n in enumerate(("ln_mix_g", "ln_mix_b", "ln_ffn_g", "ln_ffn_b"))}
        out["hgrn_lb_logits"] = t[16].reshape(hgrn_lb_logits.shape)
        out["hgrn_norm_g"] = t[17, :2 * HG_DIM].reshape(hgrn_norm_g.shape)
        out["att_sink"] = t[18, :2 * N_Q_HEADS].reshape(att_sink.shape)
        return out

    result = [loss, grad_x]
    for big_t, small_t in zip(big_out, souts):
        merged = dict(big_t)
        merged.update(small_unpack(small_t))
        result += [merged[n] for n in names]
    return tuple(result)
```

```python
import functools

import jax
import jax.numpy as jnp
from jax import lax
from jax.experimental import pallas as pl
from jax.experimental.pallas import tpu as pltpu

F32 = jnp.float32
BF16 = jnp.bfloat16

D_MODEL = 1024
DEPTH = 4
HEAD_DIM = 64
N_Q_HEADS = 16
N_KV_HEADS = 4
GROUP = 4
KV_DIM = 256
ATT_BLOCK = 128
ROPE_DIM = 16
ROPE_THETA = 500000.0
HG_HEADS = 8
HG_DIM = 128
HG_CHUNK = 64
HG_SUB = 16
D_FF = 2816
FF_TILE = 1408
SUB_ROWS = 256
PLE_DIM = 256
ALPHA = (2 * DEPTH) ** 0.25
LN_EPS = 1e-5
ADAM_LR, ADAM_B1, ADAM_B2, ADAM_EPS, ADAM_WD, ADAM_STEP = 0.001, 0.9, 0.999, 1e-08, 0.01, 10

N_DEV = 8
LANES = 128
VMEM_LIMIT = 52 * 1024 * 1024
NEG = -1e30
MESH = pl.DeviceIdType.MESH
AXES = ("x", "y", "c")

BIG = (("att_w_qkv", 2), ("att_w_o", 1), ("hgrn_w_in", 2), ("hgrn_w_o", 1), ("ffn_w_in", 2), ("ffn_w_out", 1),
       ("ple_w_gate", 1), ("ple_w_proj", 2))


def _params(sem=None, vmem=VMEM_LIMIT):
    return pltpu.CompilerParams(dimension_semantics=sem, vmem_limit_bytes=vmem)


def _sigmoid(x):
    return jax.nn.sigmoid(x)


def _direct_copies(src_ref, out_ref, send_sems, recv_sems, local_sem, gather, arrivals):
    x, y, c = lax.axis_index("x"), lax.axis_index("y"), lax.axis_index("c")
    me = 4 * x + 2 * y + c
    mine = (lambda j: src_ref) if gather else (lambda j: src_ref.at[j])
    pairs = []
    for k in range(1, N_DEV):
        px, py, pc = x ^ (k >> 2), y ^ ((k >> 1) & 1), c ^ (k & 1)
        peer = 4 * px + 2 * py + pc
        send = pltpu.make_async_remote_copy(
            src_ref=mine(peer), dst_ref=out_ref.at[me], send_sem=send_sems.at[k], recv_sem=recv_sems.at[k],
            device_id=(px, py, pc), device_id_type=MESH)
        arrival = pltpu.make_async_remote_copy(
            src_ref=mine(peer), dst_ref=out_ref.at[peer], send_sem=send_sems.at[k], recv_sem=recv_sems.at[k],
            device_id=(x, y, c), device_id_type=MESH) if arrivals else None
        pairs.append((send, arrival))
    return pltpu.make_async_copy(mine(me), out_ref.at[me], local_sem), pairs


def _direct_start(*refs, gather):
    local, pairs = _direct_copies(*refs, gather, False)
    local.start()
    for send, _ in pairs:
        send.start()


def _direct_wait(*refs, gather):
    local, pairs = _direct_copies(*refs, gather, True)
    for send, arrival in pairs:
        send.wait_send()
        arrival.wait_recv()
    local.wait()


COMM_SCRATCH = [pltpu.SemaphoreType.DMA((N_DEV,)), pltpu.SemaphoreType.DMA((N_DEV,)), pltpu.SemaphoreType.DMA]


def _exchange(src, *, gather=False, name):
    def body(*refs):
        _direct_start(*refs, gather=gather)
        _direct_wait(*refs, gather=gather)

    blk = tuple(src.shape) if gather else tuple(src.shape[1:])
    return pl.pallas_call(
        body, name=name,
        out_shape=jax.ShapeDtypeStruct((N_DEV,) + blk, src.dtype),
        in_specs=[pl.BlockSpec(memory_space=pltpu.HBM)],
        out_specs=pl.BlockSpec(memory_space=pltpu.HBM),
        scratch_shapes=COMM_SCRATCH,
    )(src)


def _call(body, *, name, grid, out_shape, in_specs, out_specs, args, scratch_shapes=(), sem, comm=None, edge=None):
    out_shape, out_specs = tuple(out_shape), tuple(out_specs)
    if comm is None:
        return pl.pallas_call(body, name=name, grid=grid, out_shape=out_shape, in_specs=list(in_specs),
                              out_specs=out_specs, scratch_shapes=list(scratch_shapes),
                              compiler_params=_params(sem))(*args)
    src, gather = comm
    n_in, n_out, n_scr = len(args), len(out_shape), len(scratch_shapes)
    blk = tuple(src.shape) if gather else tuple(src.shape[1:])
    hbm = pl.BlockSpec(memory_space=pltpu.HBM)

    def carrying(*refs):
        ins, src_ref = refs[:n_in], refs[n_in]
        outs, dst_ref = refs[n_in + 1:n_in + 1 + n_out], refs[n_in + 1 + n_out]
        own = refs[n_in + 2 + n_out:n_in + 2 + n_out + n_scr]
        comm_refs = (src_ref, dst_ref) + tuple(refs[n_in + 2 + n_out + n_scr:])
        first, last = edge()

        @pl.when(first)
        def _():
            _direct_start(*comm_refs, gather=gather)

        body(*ins, *outs, *own)

        @pl.when(last)
        def _():
            _direct_wait(*comm_refs, gather=gather)

    return pl.pallas_call(
        carrying, name=name, grid=grid,
        out_shape=out_shape + (jax.ShapeDtypeStruct((N_DEV,) + blk, src.dtype),),
        in_specs=list(in_specs) + [hbm], out_specs=out_specs + (hbm,),
        scratch_shapes=list(scratch_shapes) + COMM_SCRATCH,
        compiler_params=_params(("arbitrary",) * len(grid)),
    )(*args, src)


def _gather(src, *, name):
    def body(src_ref, out_ref, send_sems, recv_sems, local_sem):
        x, y, c = lax.axis_index("x"), lax.axis_index("y"), lax.axis_index("c")
        sibling = (x, y, 1 - c)
        chips = [(1 - x, y), (x, 1 - y), (1 - x, 1 - y)]

        def rows(px, py, pc):
            return out_ref.at[4 * px + 2 * py + pc]

        def copy(k, block, to, from_src=False):
            return pltpu.make_async_remote_copy(
                src_ref=src_ref if from_src else rows(*block), dst_ref=rows(*block), send_sem=send_sems.at[k],
                recv_sem=recv_sems.at[k], device_id=to, device_id_type=MESH)

        me = (x, y, c)
        mine = pltpu.make_async_copy(src_ref, rows(*me), local_sem)
        mine.start()
        first = [copy(0, me, sibling, from_src=True)]
        first += [copy(1 + j, me, (*chip, c), from_src=True) for j, chip in enumerate(chips)]
        for cp in first:
            cp.start()
        passed = [copy(4 + j, (*chip, c), sibling) for j, chip in enumerate(chips)]
        for j, chip in enumerate(chips):
            copy(1 + j, (*chip, c), me).wait_recv()
            passed[j].start()
        copy(0, sibling, me).wait_recv()
        for j, chip in enumerate(chips):
            copy(4 + j, (*chip, 1 - c), me).wait_recv()
        for cp in first + passed:
            cp.wait_send()
        mine.wait()

    return pl.pallas_call(
        body, name=name,
        out_shape=jax.ShapeDtypeStruct((N_DEV,) + tuple(src.shape), src.dtype),
        in_specs=[pl.BlockSpec(memory_space=pltpu.HBM)],
        out_specs=pl.BlockSpec(memory_space=pltpu.HBM),
        scratch_shapes=[pltpu.SemaphoreType.DMA((7,)), pltpu.SemaphoreType.DMA((7,)), pltpu.SemaphoreType.DMA],
    )(src)


def _mm(a, b, *, n, tm, tn, tk, ta=False, tb=False, out_dtype=F32, add=None, add_scale=1.0, bk_off=0, ln=None, name):
    m, kdim = (a.shape[1], a.shape[0]) if ta else a.shape
    nk = kdim // tk
    dims = (((0 if ta else 1,), (1 if tb else 0,)), ((), ()))

    def body(*refs):
        a_ref, b_ref = refs[:2]
        nxt = 2
        add_ref = None
        if add is not None:
            add_ref = refs[nxt]
            nxt += 1
        if ln is not None:
            pre_ref, g_ref = refs[nxt:nxt + 2]
            nxt += 2
        outs = refs[nxt:nxt + (4 if ln is not None else 1)]
        acc_ref = refs[-1] if nk > 1 else None
        part = lax.dot_general(a_ref[...].astype(BF16), b_ref[...].astype(BF16), dims, preferred_element_type=F32)

        def finish(r):
            if add_ref is not None:
                r = r + add_scale * add_ref[...]
            if ln is None:
                outs[0][...] = r.astype(out_dtype)
            else:
                _ln_bwd_store(r, pre_ref[...], g_ref[...], *outs, first=pl.program_id(0) == 0)

        if nk == 1:
            finish(part)
        else:
            k = pl.program_id(2)

            @pl.when(k == 0)
            def _():
                acc_ref[...] = part

            @pl.when(k > 0)
            def _():
                acc_ref[...] += part

            @pl.when(k == nk - 1)
            def _():
                finish(acc_ref[...])

    a_spec = pl.BlockSpec((tk, tm), lambda i, j, k: (k, i)) if ta else pl.BlockSpec((tm, tk), lambda i, j, k: (i, k))
    b_spec = (pl.BlockSpec((tn, tk), lambda i, j, k: (j, k + bk_off)) if tb
              else pl.BlockSpec((tk, tn), lambda i, j, k: (k + bk_off, j)))
    in_specs, args = [a_spec, b_spec], [a, b]
    tile = pl.BlockSpec((tm, tn), lambda i, j, k: (i, j))
    if add is not None:
        in_specs.append(tile)
        args.append(add)
    out_shape, out_specs = jax.ShapeDtypeStruct((m, n), out_dtype), tile
    if ln is not None:
        assert tn == n == D_MODEL
        vec = pl.BlockSpec((1, D_MODEL), lambda i, j, k: (0, 0))
        in_specs += [tile, vec]
        args += list(ln)
        out_shape = (jax.ShapeDtypeStruct((m, n), F32), jax.ShapeDtypeStruct((m, n), BF16),
                     jax.ShapeDtypeStruct((1, D_MODEL), F32), jax.ShapeDtypeStruct((1, D_MODEL), F32))
        out_specs = (tile, tile, vec, vec)
    return pl.pallas_call(
        body, name=name, grid=(m // tm, n // tn, nk),
        out_shape=out_shape, in_specs=in_specs, out_specs=out_specs,
        scratch_shapes=[pltpu.VMEM((tm, tn), F32)] if nk > 1 else [],
        compiler_params=_params(("arbitrary",) * 3 if ln is not None else ("parallel", "parallel", "arbitrary")),
    )(*args)


def _ln_bwd_rows(do, y, g):
    mu = jnp.mean(y, axis=-1, keepdims=True)
    yc = y - mu
    var = jnp.mean(yc * yc, axis=-1, keepdims=True)
    rstd = lax.rsqrt(var + LN_EPS)
    xhat = yc * rstd
    dxhat = do * g
    dy = rstd * (dxhat - jnp.mean(dxhat, axis=-1, keepdims=True) - xhat * jnp.mean(dxhat * xhat, axis=-1, keepdims=True))
    return dy, jnp.sum(do * xhat, axis=0, keepdims=True), jnp.sum(do, axis=0, keepdims=True)


def _accumulate(ref, val, first):
    @pl.when(first)
    def _():
        ref[...] = val

    @pl.when(jnp.logical_not(first))
    def _():
        ref[...] += val


def _ln_bwd_store(do, y, g, dy_ref, dybf_ref, dg_ref, db_ref, *, first):
    dy, pg, pb = _ln_bwd_rows(do, y, g)
    dy_ref[...] = dy
    dybf_ref[...] = dy.astype(BF16)
    _accumulate(dg_ref, pg, first)
    _accumulate(db_ref, pb, first)


def _layer_norm_rows(y, g, b):
    mu = jnp.mean(y, axis=-1, keepdims=True)
    yc = y - mu
    var = jnp.mean(yc * yc, axis=-1, keepdims=True)
    return yc * lax.rsqrt(var + LN_EPS) * g + b


def _proj_ln(a, w, res, g, b, *, tm, name):
    s, kdim = a.shape

    def body(a_ref, w_ref, res_ref, g_ref, b_ref, pre_ref, o_ref, obf_ref):
        for rs in _row_parts(tm):
            h = jnp.dot(a_ref[rs, :], w_ref[...], preferred_element_type=F32)
            pre = ALPHA * res_ref[rs, :] + h
            out = _layer_norm_rows(pre, g_ref[...], b_ref[...])
            pre_ref[rs, :] = pre
            o_ref[rs, :] = out
            obf_ref[rs, :] = out.astype(BF16)

    row = lambda i: (i, 0)
    fix = lambda i: (0, 0)
    return pl.pallas_call(
        body, name=name, grid=(s // tm,),
        out_shape=(jax.ShapeDtypeStruct((s, D_MODEL), F32), jax.ShapeDtypeStruct((s, D_MODEL), F32),
                   jax.ShapeDtypeStruct((s, D_MODEL), BF16)),
        in_specs=[pl.BlockSpec((tm, kdim), row), pl.BlockSpec((kdim, D_MODEL), fix), pl.BlockSpec((tm, D_MODEL), row),
                  pl.BlockSpec((1, D_MODEL), fix), pl.BlockSpec((1, D_MODEL), fix)],
        out_specs=(pl.BlockSpec((tm, D_MODEL), row),) * 3,
        compiler_params=_params(("parallel",)),
    )(a, w, res, g, b)


def _row_parts(tm):
    sub = min(tm, SUB_ROWS)
    return [pl.ds(r * sub, sub) for r in range(tm // sub)]


def _ffn_in(xbf, w, *, tm, tn, name):
    s = xbf.shape[0]
    nj = D_FF // tn

    def body(x_ref, wg_ref, wu_ref, g_ref, u_ref, act_ref):
        for rs in _row_parts(tm):
            xv = x_ref[rs, :]
            gg = jnp.dot(xv, wg_ref[...], preferred_element_type=F32)
            uu = jnp.dot(xv, wu_ref[...], preferred_element_type=F32)
            g_ref[rs, :] = gg.astype(BF16)
            u_ref[rs, :] = uu.astype(BF16)
            act_ref[rs, :] = (gg * _sigmoid(gg) * uu).astype(BF16)

    out = jax.ShapeDtypeStruct((s, D_FF), BF16)
    tile = pl.BlockSpec((tm, tn), lambda j, i: (i, j))
    return pl.pallas_call(
        body, name=name, grid=(nj, s // tm),
        out_shape=(out, out, out),
        in_specs=[pl.BlockSpec((tm, D_MODEL), lambda j, i: (i, 0)), pl.BlockSpec((D_MODEL, tn), lambda j, i: (0, j)),
                  pl.BlockSpec((D_MODEL, tn), lambda j, i: (0, j + nj))],
        out_specs=(tile, tile, tile),
        compiler_params=_params(("parallel", "parallel")),
    )(xbf, w, w)


def _ffn_bwd_act(dybf, w_out, g, u, *, tm, tn, name):
    s = dybf.shape[0]

    def body(dy_ref, w_ref, g_ref, u_ref, dg_ref, du_ref):
        for rs in _row_parts(tm):
            dact = lax.dot_general(dy_ref[rs, :], w_ref[...], (((1,), (1,)), ((), ())), preferred_element_type=F32)
            gg = g_ref[rs, :].astype(F32)
            uu = u_ref[rs, :].astype(F32)
            sg = _sigmoid(gg)
            dg_ref[rs, :] = (dact * uu * sg * (1.0 + gg * (1.0 - sg))).astype(BF16)
            du_ref[rs, :] = (dact * gg * sg).astype(BF16)

    out = jax.ShapeDtypeStruct((s, D_FF), BF16)
    tile = pl.BlockSpec((tm, tn), lambda j, i: (i, j))
    return pl.pallas_call(
        body, name=name, grid=(D_FF // tn, s // tm),
        out_shape=(out, out),
        in_specs=[pl.BlockSpec((tm, D_MODEL), lambda j, i: (i, 0)), pl.BlockSpec((tn, D_MODEL), lambda j, i: (j, 0)),
                  tile, tile],
        out_specs=(tile, tile),
        compiler_params=_params(("parallel", "parallel")),
    )(dybf, w_out, g, u)


def _ple_fwd(x2, x2bf, p, w_gate, w_proj, *, tm, name):
    s = x2.shape[0]

    def body(x_ref, xbf_ref, p_ref, wg_ref, wp_ref, o_ref, obf_ref):
        for rs in _row_parts(tm):
            a = jnp.dot(xbf_ref[rs, :], wg_ref[...], preferred_element_type=F32)
            pp = jnp.dot(p_ref[rs, :].astype(BF16), wp_ref[...], preferred_element_type=F32)
            out = x_ref[rs, :] + _sigmoid(a) * pp
            o_ref[rs, :] = out
            obf_ref[rs, :] = out.astype(BF16)

    row = lambda i: (i, 0)
    fix = lambda i: (0, 0)
    return pl.pallas_call(
        body, name=name, grid=(s // tm,),
        out_shape=(jax.ShapeDtypeStruct((s, D_MODEL), F32), jax.ShapeDtypeStruct((s, D_MODEL), BF16)),
        in_specs=[pl.BlockSpec((tm, D_MODEL), row), pl.BlockSpec((tm, D_MODEL), row), pl.BlockSpec((tm, PLE_DIM), row),
                  pl.BlockSpec((D_MODEL, D_MODEL), fix), pl.BlockSpec((PLE_DIM, D_MODEL), fix)],
        out_specs=(pl.BlockSpec((tm, D_MODEL), row), pl.BlockSpec((tm, D_MODEL), row)),
        compiler_params=_params(("parallel",)),
    )(x2, x2bf, p, w_gate, w_proj)


def _ple_ln_bwd(dx3, x2bf, p, w_gate, w_proj, pre, g, *, tm, name):
    s = dx3.shape[0]

    def body(d_ref, xbf_ref, p_ref, wg_ref, wp_ref, pre_ref, g_ref, da_ref, dpp_ref, dy_ref, dybf_ref, dg_ref, db_ref):
        pg = jnp.zeros((1, D_MODEL), F32)
        pb = jnp.zeros((1, D_MODEL), F32)
        for rs in _row_parts(tm):
            d = d_ref[rs, :]
            a = jnp.dot(xbf_ref[rs, :], wg_ref[...], preferred_element_type=F32)
            pp = jnp.dot(p_ref[rs, :].astype(BF16), wp_ref[...], preferred_element_type=F32)
            sg = _sigmoid(a)
            da = (d * pp * sg * (1.0 - sg)).astype(BF16)
            da_ref[rs, :] = da
            dpp_ref[rs, :] = (d * sg).astype(BF16)
            dx2 = d + lax.dot_general(da, wg_ref[...], (((1,), (1,)), ((), ())), preferred_element_type=F32)
            dy, qg, qb = _ln_bwd_rows(dx2, pre_ref[rs, :], g_ref[...])
            dy_ref[rs, :] = dy
            dybf_ref[rs, :] = dy.astype(BF16)
            pg, pb = pg + qg, pb + qb
        _accumulate(dg_ref, pg, pl.program_id(0) == 0)
        _accumulate(db_ref, pb, pl.program_id(0) == 0)

    row = lambda i: (i, 0)
    fix = lambda i: (0, 0)
    tile = pl.BlockSpec((tm, D_MODEL), row)
    vec = pl.BlockSpec((1, D_MODEL), fix)
    act = lambda dt: jax.ShapeDtypeStruct((s, D_MODEL), dt)
    return pl.pallas_call(
        body, name=name, grid=(s // tm,),
        out_shape=(act(BF16), act(BF16), act(F32), act(BF16), jax.ShapeDtypeStruct((1, D_MODEL), F32),
                   jax.ShapeDtypeStruct((1, D_MODEL), F32)),
        in_specs=[tile, tile, pl.BlockSpec((tm, PLE_DIM), row), pl.BlockSpec((D_MODEL, D_MODEL), fix),
                  pl.BlockSpec((PLE_DIM, D_MODEL), fix), tile, vec],
        out_specs=(tile, tile, tile, tile, vec, vec),
        compiler_params=_params(("arbitrary",)),
    )(dx3, x2bf, p, w_gate, w_proj, pre, g)


def _loss_head(y, target, *, tm, name):
    s = y.shape[0]

    def body(y_ref, t_ref, dy_ref, loss_ref, acc_ref):
        err = y_ref[...] - t_ref[...]
        dy_ref[...] = err * (1.0 / D_MODEL)
        part = jnp.sum(err * err, axis=0, keepdims=True)

        @pl.when(pl.program_id(0) == 0)
        def _():
            acc_ref[...] = part

        @pl.when(pl.program_id(0) > 0)
        def _():
            acc_ref[...] += part

        @pl.when(pl.program_id(0) == pl.num_programs(0) - 1)
        def _():
            tot = jnp.sum(acc_ref[...], axis=1, keepdims=True) * (0.5 / D_MODEL)
            loss_ref[...] = jnp.broadcast_to(tot, (8, LANES))

    row = lambda i: (i, 0)
    return pl.pallas_call(
        body, name=name, grid=(s // tm,),
        out_shape=(jax.ShapeDtypeStruct((s, D_MODEL), F32), jax.ShapeDtypeStruct((8, LANES), F32)),
        in_specs=[pl.BlockSpec((tm, D_MODEL), row), pl.BlockSpec((tm, D_MODEL), row)],
        out_specs=(pl.BlockSpec((tm, D_MODEL), row), pl.BlockSpec((8, LANES), lambda i: (0, 0))),
        scratch_shapes=[pltpu.VMEM((1, D_MODEL), F32)],
        compiler_params=_params(("arbitrary",)),
    )(y, target)


def _rope_tables(s):
    inv = ROPE_THETA ** (-jnp.arange(0, ROPE_DIM, 2, dtype=F32) / ROPE_DIM)
    ang = jnp.arange(s, dtype=F32)[:, None] * inv[None, :]
    cos, sin = jnp.cos(ang), jnp.sin(ang)
    ones = jnp.ones((s, HEAD_DIM - ROPE_DIM), F32)
    c_head = jnp.concatenate([cos, cos, ones], axis=1)
    s_head = jnp.concatenate([-sin, sin, 0.0 * ones], axis=1)
    return jnp.concatenate([c_head, c_head], axis=1), jnp.concatenate([s_head, s_head], axis=1)


def _rope(v, cos, sin):
    n = v.shape[1] // LANES
    width = v.shape[1]
    cos_w = jnp.tile(cos, (1, n)) if n > 1 else cos
    sin_w = jnp.tile(sin, (1, n)) if n > 1 else sin
    dim = lax.broadcasted_iota(jnp.int32, (1, width), 1) % HEAD_DIM
    partner = jnp.where(dim < ROPE_DIM // 2, pltpu.roll(v, width - ROPE_DIM // 2, 1), pltpu.roll(v, ROPE_DIM // 2, 1))
    return v * cos_w + partner * sin_w


def _unrope(dv, cos, sin):
    n = dv.shape[1] // LANES
    width = dv.shape[1]
    cos_w = jnp.tile(cos, (1, n)) if n > 1 else cos
    sin_w = jnp.tile(sin, (1, n)) if n > 1 else sin
    t = dv * sin_w
    dim = lax.broadcasted_iota(jnp.int32, (1, width), 1) % HEAD_DIM
    partner = jnp.where(dim < ROPE_DIM // 2, pltpu.roll(t, width - ROPE_DIM // 2, 1),
                        jnp.where(dim < ROPE_DIM, pltpu.roll(t, ROPE_DIM // 2, 1), 0.0))
    return dv * cos_w + partner


def _att_mask(i, nb):
    rows = GROUP * ATT_BLOCK
    r = lax.broadcasted_iota(jnp.int32, (rows, 3 * ATT_BLOCK), 0) % ATT_BLOCK
    cidx = lax.broadcasted_iota(jnp.int32, (rows, 3 * ATT_BLOCK), 1)
    rel = r + ATT_BLOCK - cidx
    ok = (rel <= ATT_BLOCK) & (rel >= -ATT_BLOCK)
    ok = ok & ((cidx >= ATT_BLOCK) | (i > 0)) & ((cidx < 2 * ATT_BLOCK) | (i < nb - 1))
    return ok


def _half_mask(half):
    lane = lax.broadcasted_iota(jnp.int32, (1, LANES), 1)
    return (lane // HEAD_DIM) == half


def _stack_q(q, h):
    parts = []
    for gq in range(GROUP):
        n = GROUP * h + gq
        grp = q[:, LANES * (n // 2):LANES * (n // 2 + 1)]
        grp = jnp.where(_half_mask(n % 2), grp, 0.0)
        if n % 2 != h % 2:
            grp = pltpu.roll(grp, HEAD_DIM, 1)
        parts.append(grp)
    return jnp.concatenate(parts, axis=0)


def _unstack_q(stacked, h, acc):
    for gq in range(GROUP):
        n = GROUP * h + gq
        grp = stacked[ATT_BLOCK * gq:ATT_BLOCK * (gq + 1), :]
        grp = jnp.where(_half_mask(h % 2), grp, 0.0)
        if n % 2 != h % 2:
            grp = pltpu.roll(grp, HEAD_DIM, 1)
        acc[n // 2] = grp if acc[n // 2] is None else acc[n // 2] + grp
    return acc


def _sink_rows(sink_ref, h):
    rows = GROUP * ATT_BLOCK
    grp = lax.broadcasted_iota(jnp.int32, (rows, 1), 0) // ATT_BLOCK
    out = jnp.zeros((rows, 1), F32)
    for gq in range(GROUP):
        out = jnp.where(grp == gq, sink_ref[GROUP * h + gq], out)
    return out


def _att_probs(qs, kh, sink, valid):
    s = lax.dot_general(qs, kh, (((1,), (1,)), ((), ())), preferred_element_type=F32)
    s = jnp.where(valid, s, NEG)
    m = jnp.maximum(jnp.max(s, axis=-1, keepdims=True), sink)
    p = jnp.exp(s - m)
    es = jnp.exp(sink - m)
    den = jnp.sum(p, axis=-1, keepdims=True) + es
    inv = 1.0 / den
    return p * inv, es * inv


def _att_specs(nb):
    prev = lambda i: (jnp.maximum(i - 1, 0), 0)
    cur = lambda i: (i, 0)
    nxt = lambda i: (jnp.minimum(i + 1, nb - 1), 0)
    kv = lambda f: (lambda i: (f(i)[0], 2))
    tab = [pl.BlockSpec((ATT_BLOCK, LANES), f) for f in (cur, prev, cur, nxt)]
    z = [pl.BlockSpec((ATT_BLOCK, D_MODEL), cur)] + [pl.BlockSpec((ATT_BLOCK, 2 * KV_DIM), kv(f)) for f in (prev, cur, nxt)]
    return z, tab


def _att_load(zq_ref, kp_ref, kc_ref, kn_ref, cq_ref, sq_ref, cp_ref, sp_ref, cc_ref, sc_ref, cn_ref, sn_ref):
    q = (_rope(zq_ref[...], cq_ref[...], sq_ref[...]) * (HEAD_DIM ** -0.5))
    ks, vs = [], []
    for ref, c_ref, s_ref in ((kp_ref, cp_ref, sp_ref), (kc_ref, cc_ref, sc_ref), (kn_ref, cn_ref, sn_ref)):
        kvb = ref[...]
        ks.append(_rope(kvb[:, :KV_DIM], c_ref[...], s_ref[...]))
        vs.append(kvb[:, KV_DIM:])
    return q, jnp.concatenate(ks, axis=0).astype(BF16), jnp.concatenate(vs, axis=0).astype(BF16)


def _att_fwd(z, sink, cos, sin, *, comm=None, name):
    s = z.shape[0]
    nb = s // ATT_BLOCK

    def body(zq_ref, kp_ref, kc_ref, kn_ref, cq_ref, cp_ref, cc_ref, cn_ref, sq_ref, sp_ref, sc_ref, sn_ref, sink_ref,
             o_ref):
        i = pl.program_id(0)
        q, k, v = _att_load(zq_ref, kp_ref, kc_ref, kn_ref, cq_ref, sq_ref, cp_ref, sp_ref, cc_ref, sc_ref, cn_ref, sn_ref)
        valid = _att_mask(i, nb)
        acc = [None] * (N_Q_HEADS // 2)
        for h in range(N_KV_HEADS):
            lanes = slice(LANES * (h // 2), LANES * (h // 2 + 1))
            qs = _stack_q(q, h).astype(BF16)
            prob, _ = _att_probs(qs, k[:, lanes], _sink_rows(sink_ref, h), valid)
            oh = jnp.dot(prob.astype(BF16), v[:, lanes], preferred_element_type=F32)
            acc = _unstack_q(oh, h, acc)
        o_ref[...] = jnp.concatenate(acc, axis=1).astype(BF16)

    zspecs, tab = _att_specs(nb)
    return _call(
        body, name=name, grid=(nb,),
        out_shape=(jax.ShapeDtypeStruct((s, D_MODEL), BF16),),
        in_specs=zspecs + tab + tab + [pl.BlockSpec(memory_space=pltpu.SMEM)],
        out_specs=(pl.BlockSpec((ATT_BLOCK, D_MODEL), lambda i: (i, 0)),),
        args=(z, z, z, z, cos, cos, cos, cos, sin, sin, sin, sin, sink), sem=("parallel",), comm=comm,
        edge=lambda: (pl.program_id(0) == 0, pl.program_id(0) == nb - 1))


def _att_bwd(z, do, sink, cos, sin, *, comm=None, name):
    s = z.shape[0]
    nb = s // ATT_BLOCK

    def body(zq_ref, kp_ref, kc_ref, kn_ref, cq_ref, cp_ref, cc_ref, cn_ref, sq_ref, sp_ref, sc_ref, sn_ref, sink_ref,
             do_ref, dq_ref, part_ref, dsink_ref):
        i = pl.program_id(0)
        q, k, v = _att_load(zq_ref, kp_ref, kc_ref, kn_ref, cq_ref, sq_ref, cp_ref, sp_ref, cc_ref, sc_ref, cn_ref, sn_ref)
        valid = _att_mask(i, nb)
        dout = do_ref[...].astype(F32)
        dq_acc = [None] * (N_Q_HEADS // 2)
        dk_acc = [None] * 2
        dv_acc = [None] * 2
        ds_rows = []
        for h in range(N_KV_HEADS):
            grp = h // 2
            lanes = slice(LANES * grp, LANES * (grp + 1))
            qs = _stack_q(q, h).astype(BF16)
            dos = _stack_q(dout, h).astype(BF16)
            prob, psink = _att_probs(qs, k[:, lanes], _sink_rows(sink_ref, h), valid)
            dprob = lax.dot_general(dos, v[:, lanes], (((1,), (1,)), ((), ())), preferred_element_type=F32)
            delta = jnp.sum(prob * dprob, axis=-1, keepdims=True)
            dsc = (prob * (dprob - delta)).astype(BF16)
            ds_rows.append(-psink * delta)
            dqs = jnp.dot(dsc, k[:, lanes], preferred_element_type=F32)
            dq_acc = _unstack_q(dqs, h, dq_acc)
            dkh = lax.dot_general(dsc, qs, (((0,), (0,)), ((), ())), preferred_element_type=F32)
            dvh = lax.dot_general(prob.astype(BF16), dos, (((0,), (0,)), ((), ())), preferred_element_type=F32)
            dk_acc[grp] = dkh if dk_acc[grp] is None else dk_acc[grp] + dkh
            dv_acc[grp] = dvh if dv_acc[grp] is None else dv_acc[grp] + dvh
        dq = jnp.concatenate(dq_acc, axis=1) * (HEAD_DIM ** -0.5)
        dq_ref[...] = _unrope(dq, cq_ref[...], sq_ref[...]).astype(BF16)
        part = jnp.concatenate(dk_acc + dv_acc, axis=1)
        for wdw in range(3):
            part_ref[wdw] = part[ATT_BLOCK * wdw:ATT_BLOCK * (wdw + 1), :]
        rows = []
        for h in range(N_KV_HEADS):
            for gq in range(GROUP):
                tot = jnp.sum(ds_rows[h][ATT_BLOCK * gq:ATT_BLOCK * (gq + 1), :], axis=0, keepdims=True)
                rows.append(jnp.broadcast_to(tot, (1, LANES)))
        dsink = jnp.concatenate(rows, axis=0)

        @pl.when(i == 0)
        def _():
            dsink_ref[...] = dsink

        @pl.when(i > 0)
        def _():
            dsink_ref[...] += dsink

    zspecs, tab = _att_specs(nb)
    return _call(
        body, name=name, grid=(nb,),
        out_shape=(jax.ShapeDtypeStruct((s, D_MODEL), BF16), jax.ShapeDtypeStruct((nb, 3, ATT_BLOCK, 2 * KV_DIM), F32),
                   jax.ShapeDtypeStruct((N_Q_HEADS, LANES), F32)),
        in_specs=zspecs + tab + tab + [pl.BlockSpec(memory_space=pltpu.SMEM), pl.BlockSpec((ATT_BLOCK, D_MODEL), lambda i: (i, 0))],
        out_specs=(pl.BlockSpec((ATT_BLOCK, D_MODEL), lambda i: (i, 0)),
                   pl.BlockSpec((None, 3, ATT_BLOCK, 2 * KV_DIM), lambda i: (i, 0, 0, 0)),
                   pl.BlockSpec((N_Q_HEADS, LANES), lambda i: (0, 0))),
        args=(z, z, z, z, cos, cos, cos, cos, sin, sin, sin, sin, sink, do), sem=("arbitrary",), comm=comm,
        edge=lambda: (pl.program_id(0) == 0, pl.program_id(0) == nb - 1))


def _att_bwd_kv(part, cos, sin, *, name):
    nb = part.shape[0]

    def body(pn_ref, pc_ref, pp_ref, c_ref, s_ref, o_ref):
        j = pl.program_id(0)
        tot = pc_ref[...]
        tot = tot + jnp.where(j < nb - 1, pn_ref[...], 0.0)
        tot = tot + jnp.where(j > 0, pp_ref[...], 0.0)
        dk = _unrope(tot[:, :KV_DIM], c_ref[...], s_ref[...])
        o_ref[...] = jnp.concatenate([dk, tot[:, KV_DIM:]], axis=1).astype(BF16)

    blk = (None, None, ATT_BLOCK, 2 * KV_DIM)
    return pl.pallas_call(
        body, name=name, grid=(nb,),
        out_shape=jax.ShapeDtypeStruct((nb * ATT_BLOCK, 2 * KV_DIM), BF16),
        in_specs=[pl.BlockSpec(blk, lambda j: (jnp.minimum(j + 1, nb - 1), 0, 0, 0)),
                  pl.BlockSpec(blk, lambda j: (j, 1, 0, 0)),
                  pl.BlockSpec(blk, lambda j: (jnp.maximum(j - 1, 0), 2, 0, 0)),
                  pl.BlockSpec((ATT_BLOCK, LANES), lambda j: (j, 0)), pl.BlockSpec((ATT_BLOCK, LANES), lambda j: (j, 0))],
        out_specs=pl.BlockSpec((ATT_BLOCK, 2 * KV_DIM), lambda j: (j, 0)),
        compiler_params=_params(("parallel",)),
    )(part, part, part, cos, sin)


def _bdot(a, b, dims):
    return lax.dot_general(a.astype(BF16), b.astype(BF16), (dims, ((), ())), preferred_element_type=F32)


@jax.custom_vjp
def _dot_nn(a, b):
    return _bdot(a, b, ((1,), (0,)))


@jax.custom_vjp
def _dot_nt(a, b):
    return _bdot(a, b, ((1,), (1,)))


@jax.custom_vjp
def _dot_tn(a, b):
    return _bdot(a, b, ((0,), (0,)))


_dot_nn.defvjp(lambda a, b: (_dot_nn(a, b), (a, b)), lambda r, d: (_dot_nt(d, r[1]), _dot_tn(r[0], d)))
_dot_nt.defvjp(lambda a, b: (_dot_nt(a, b), (a, b)), lambda r, d: (_dot_nn(d, r[1]), _dot_tn(d, r[0])))
_dot_tn.defvjp(lambda a, b: (_dot_tn(a, b), (a, b)), lambda r, d: (_dot_nt(r[1], d), _dot_nn(r[0], d)))


def _running_sum(v, up):
    n = v.shape[0]
    rows = lax.broadcasted_iota(jnp.int32, v.shape, 0)
    sh = 1
    while sh < n:
        if up:
            v = v + jnp.where(rows < n - sh, pltpu.roll(v, n - sh, 0), 0.0)
        else:
            v = v + jnp.where(rows >= sh, pltpu.roll(v, sh, 0), 0.0)
        sh *= 2
    return v


@jax.custom_vjp
def _sum_down(v):
    return _running_sum(v, False)


@jax.custom_vjp
def _sum_up(v):
    return _running_sum(v, True)


_sum_down.defvjp(lambda v: (_running_sum(v, False), None), lambda _, d: (_sum_up(d),))
_sum_up.defvjp(lambda v: (_running_sum(v, True), None), lambda _, d: (_sum_down(d),))

N_SUB = HG_CHUNK // HG_SUB


def _fold_blocks(v):
    out = v[:HG_CHUNK]
    for i in range(1, N_SUB):
        out = out + v[HG_CHUNK * i:HG_CHUNK * (i + 1)]
    return out


@jax.custom_vjp
def _fold(v):
    return _fold_blocks(v)


_fold.defvjp(lambda v: (_fold_blocks(v), None), lambda _, d: (jnp.concatenate([d] * N_SUB, axis=0),))


def _hg_consts(rev):
    c, sub = HG_CHUNK, HG_SUB
    rowpos = lax.broadcasted_iota(jnp.int32, (c, HG_DIM), 0)
    rr = lax.broadcasted_iota(jnp.int32, (N_SUB * c, c), 0)
    key = lax.broadcasted_iota(jnp.int32, (N_SUB * c, c), 1)
    blk, qry = rr // c, rr % c
    if rev:
        rowpos, qry, key = c - 1 - rowpos, c - 1 - qry, c - 1 - key
    keep = (key // sub == blk) & (key <= qry)
    return keep, rowpos


def _pick(b, rowpos, t):
    return jnp.sum(jnp.where(rowpos == t, b, 0.0), axis=0, keepdims=True)


def _hg_local(zq, zf, zv, lbv, consts, dots):
    dot_nn, dot_nt, dot_tn, cum, fold = dots
    keep, rowpos = consts
    sig = _sigmoid(zf)
    f = lbv + (1.0 - lbv) * sig
    g = jnp.log(f)
    k = (1.0 - lbv) * (1.0 - sig)
    q = zq * _sigmoid(zq)
    b = cum(g)
    ends = [_pick(b, rowpos, (j + 1) * HG_SUB - 1) for j in range(N_SUB)]
    b_last = ends[-1]
    b_end = b_last
    for j in range(N_SUB - 1):
        b_end = jnp.where(rowpos // HG_SUB == j, ends[j], b_end)
    kc = k * jnp.exp(b_end - b)
    qbs = [q * jnp.exp(jnp.where(rowpos >= j * HG_SUB, b - ends[j], 0.0)) for j in range(N_SUB)]
    scores = fold(jnp.where(keep, dot_nt(jnp.concatenate(qbs, axis=0), kc), 0.0))
    return dot_nn(scores, zv), q * jnp.exp(b), k * jnp.exp(b_last - b), jnp.exp(b_last)


def _hg_chunk(zq, zf, zv, lbv, st, consts, dots):
    intra, qs, kd, dec = _hg_local(zq, zf, zv, lbv, consts, dots)
    return intra + dots[1](qs, st), dec * st + dots[2](zv, kd)


def _hg_dots(diff, rev):
    if diff:
        return _dot_nn, _dot_nt, _dot_tn, (_sum_up if rev else _sum_down), _fold
    return (lambda a, b: _bdot(a, b, ((1,), (0,))), lambda a, b: _bdot(a, b, ((1,), (1,))),
            lambda a, b: _bdot(a, b, ((0,), (0,))), lambda v: _running_sum(v, rev), _fold_blocks)


def _hg_specs(ts, nch, trow):
    tile = pl.BlockSpec((ts, HG_DIM), lambda h, t: (trow(t), h))
    mats = pl.BlockSpec((None, nch, HG_DIM, HG_DIM), lambda h, t: (h, trow(t), 0, 0))
    vecs = pl.BlockSpec((None, nch, 1, HG_DIM), lambda h, t: (h, trow(t), 0, 0))
    return tile, mats, vecs


def _time_order(nch, rev):
    return range(nch - 1, -1, -1) if rev else range(nch)


def _chunk_rows(c):
    return pl.ds(c * HG_CHUNK, HG_CHUNK)


def _hg_edge(nt):
    h, t = pl.program_id(0), pl.program_id(1)
    return (h == 0) & (t == 0), (h == HG_HEADS - 1) & (t == nt - 1)


def _hg_fwd(z, lb, *, rev, ts, comm=None, name):
    s = z.shape[0]
    nt = s // ts
    nch = ts // HG_CHUNK
    fcol = HG_HEADS * (2 if rev else 1)

    def body(zq_ref, zf_ref, zv_ref, lb_ref, o_ref, st_ref, qs_ref, dec_ref, state_ref):
        @pl.when(pl.program_id(1) == 0)
        def _():
            state_ref[...] = jnp.zeros_like(state_ref)

        consts = _hg_consts(rev)
        dots = _hg_dots(False, rev)
        lbv = lb_ref[...]
        local = {}
        for c in range(nch):
            rows = _chunk_rows(c)
            zv = zv_ref[rows, :]
            intra, qs, kd, dec = _hg_local(zq_ref[rows, :], zf_ref[rows, :], zv, lbv, consts, dots)
            qs = qs.astype(BF16)
            qs_ref[rows, :] = qs
            dec_ref[c] = dec
            local[c] = (intra, qs, dec, dots[2](zv, kd))
        st = state_ref[...]
        for c in _time_order(nch, rev):
            intra, qs, dec, upd = local[c]
            st_ref[c] = st
            o_ref[_chunk_rows(c), :] = intra + _bdot(qs, st, ((1,), (1,)))
            st = dec * st + upd
        state_ref[...] = st

    trow = (lambda t: nt - 1 - t) if rev else (lambda t: t)
    col = lambda off: pl.BlockSpec((ts, HG_DIM), lambda h, t: (trow(t), off + h))
    tile, mats, vecs = _hg_specs(ts, nch, trow)
    nchunks = s // HG_CHUNK
    return _call(
        body, name=name, grid=(HG_HEADS, nt),
        out_shape=(jax.ShapeDtypeStruct((s, D_MODEL), F32),
                   jax.ShapeDtypeStruct((HG_HEADS, nchunks, HG_DIM, HG_DIM), F32),
                   jax.ShapeDtypeStruct((s, D_MODEL), BF16),
                   jax.ShapeDtypeStruct((HG_HEADS, nchunks, 1, HG_DIM), F32)),
        in_specs=[col(0), col(fcol), col(3 * HG_HEADS), pl.BlockSpec((None, 1, HG_DIM), lambda h, t: (h, 0, 0))],
        out_specs=(tile, mats, tile, vecs), args=(z, z, z, lb),
        scratch_shapes=[pltpu.VMEM((HG_DIM, HG_DIM), F32)], sem=("parallel", "arbitrary"), comm=comm,
        edge=lambda: _hg_edge(nt))


def _hg_bwd(z, lb, states, qs, dec, dout, addq, addv, *, rev, ts, comm=None, name):
    s = z.shape[0]
    nt = s // ts
    nch = ts // HG_CHUNK
    fcol = HG_HEADS * (2 if rev else 1)
    has_add = addq is not None

    def body(*refs):
        zq_ref, zf_ref, zv_ref, lb_ref, st_ref, qs_ref, dec_ref, do_ref = refs[:8]
        aq_ref, av_ref = (refs[8], refs[9]) if has_add else (None, None)
        dq_ref, df_ref, dv_ref, dlb_ref, grad_ref = refs[-5:]

        @pl.when(pl.program_id(1) == 0)
        def _():
            grad_ref[...] = jnp.zeros_like(grad_ref)

        consts = _hg_consts(rev)
        dots = _hg_dots(True, rev)
        lbv = lb_ref[...]
        prods = {c: _bdot(do_ref[_chunk_rows(c), :], qs_ref[_chunk_rows(c), :], ((0,), (0,))) for c in range(nch)}
        gleave = {}
        gr = grad_ref[...]
        for c in reversed(_time_order(nch, rev)):
            gleave[c] = gr
            gr = dec_ref[c] * gr + prods[c]
        grad_ref[...] = gr
        dlb_blk = jnp.zeros((1, HG_DIM), F32)
        for c in range(nch):
            rows = _chunk_rows(c)
            fn = lambda a, b2, c2, d2, e2: _hg_chunk(a, b2, c2, d2, e2, consts, dots)
            _, pull = jax.vjp(fn, zq_ref[rows, :], zf_ref[rows, :], zv_ref[rows, :], lbv, st_ref[c])
            dq, df, dv, dlb, _ = pull((do_ref[rows, :], gleave[c]))
            if has_add:
                dq = dq + aq_ref[rows, :]
                dv = dv + av_ref[rows, :]
            dq_ref[rows, :] = dq
            df_ref[rows, :] = df
            dv_ref[rows, :] = dv
            dlb_blk = dlb_blk + dlb

        @pl.when(pl.program_id(1) == 0)
        def _():
            dlb_ref[...] = dlb_blk

        @pl.when(pl.program_id(1) > 0)
        def _():
            dlb_ref[...] += dlb_blk

    trow = (lambda t: t) if rev else (lambda t: nt - 1 - t)
    col = lambda off: pl.BlockSpec((ts, HG_DIM), lambda h, t: (trow(t), off + h))
    tile, mats, vecs = _hg_specs(ts, nch, trow)
    in_specs = [col(0), col(fcol), col(3 * HG_HEADS), pl.BlockSpec((None, 1, HG_DIM), lambda h, t: (h, 0, 0)),
                mats, tile, vecs, tile]
    args = [z, z, z, lb, states, qs, dec, dout]
    if has_add:
        in_specs += [tile, tile]
        args += [addq, addv]
    full = jax.ShapeDtypeStruct((s, D_MODEL), F32)
    return _call(
        body, name=name, grid=(HG_HEADS, nt),
        out_shape=(full, full, full, jax.ShapeDtypeStruct((HG_HEADS, 1, HG_DIM), F32)),
        in_specs=in_specs,
        out_specs=(tile, tile, tile, pl.BlockSpec((None, 1, HG_DIM), lambda h, t: (h, 0, 0))), args=tuple(args),
        scratch_shapes=[pltpu.VMEM((HG_DIM, HG_DIM), F32)], sem=("parallel", "arbitrary"), comm=comm,
        edge=lambda: _hg_edge(nt))


def _hg_post(of, ob, z, norm_g, *, tm, name):
    s = of.shape[0]

    def body(of_ref, ob_ref, gate_ref, ng_ref, y_ref):
        gn = ng_ref[...]
        for h in range(HG_HEADS):
            ln = slice(HG_DIM * h, HG_DIM * (h + 1))
            o = of_ref[:, ln] + ob_ref[:, ln]
            r = lax.rsqrt(jnp.mean(o * o, axis=-1, keepdims=True) + LN_EPS)
            gt = gate_ref[:, ln]
            y_ref[:, ln] = (o * r * gn * gt * _sigmoid(gt)).astype(BF16)

    row = lambda i: (i, 0)
    return pl.pallas_call(
        body, name=name, grid=(s // tm,),
        out_shape=jax.ShapeDtypeStruct((s, D_MODEL), BF16),
        in_specs=[pl.BlockSpec((tm, D_MODEL), row), pl.BlockSpec((tm, D_MODEL), row),
                  pl.BlockSpec((tm, D_MODEL), lambda i: (i, 4)), pl.BlockSpec((1, HG_DIM), lambda i: (0, 0))],
        out_specs=pl.BlockSpec((tm, D_MODEL), row),
        compiler_params=_params(("parallel",)),
    )(of, ob, z, norm_g)


def _hg_post_bwd(dy, of, ob, z, norm_g, *, tm, name):
    s = of.shape[0]

    def body(dy_ref, of_ref, ob_ref, gate_ref, ng_ref, do_ref, dgate_ref, dng_ref):
        gn = ng_ref[...]
        tot = jnp.zeros((1, HG_DIM), F32)
        for h in range(HG_HEADS):
            ln = slice(HG_DIM * h, HG_DIM * (h + 1))
            d = dy_ref[:, ln].astype(F32)
            o = of_ref[:, ln] + ob_ref[:, ln]
            r = lax.rsqrt(jnp.mean(o * o, axis=-1, keepdims=True) + LN_EPS)
            ohat = o * r
            gt = gate_ref[:, ln]
            sg = _sigmoid(gt)
            don = d * gt * sg
            dgate_ref[:, ln] = (d * ohat * gn * sg * (1.0 + gt * (1.0 - sg))).astype(BF16)
            tot = tot + jnp.sum(don * ohat, axis=0, keepdims=True)
            dohat = don * gn
            do_ref[:, ln] = r * (dohat - ohat * jnp.mean(dohat * ohat, axis=-1, keepdims=True))

        @pl.when(pl.program_id(0) == 0)
        def _():
            dng_ref[...] = tot

        @pl.when(pl.program_id(0) > 0)
        def _():
            dng_ref[...] += tot

    row = lambda i: (i, 0)
    return pl.pallas_call(
        body, name=name, grid=(s // tm,),
        out_shape=(jax.ShapeDtypeStruct((s, D_MODEL), F32), jax.ShapeDtypeStruct((s, D_MODEL), BF16),
                   jax.ShapeDtypeStruct((1, HG_DIM), F32)),
        in_specs=[pl.BlockSpec((tm, D_MODEL), row), pl.BlockSpec((tm, D_MODEL), row), pl.BlockSpec((tm, D_MODEL), row),
                  pl.BlockSpec((tm, D_MODEL), lambda i: (i, 4)), pl.BlockSpec((1, HG_DIM), lambda i: (0, 0))],
        out_specs=(pl.BlockSpec((tm, D_MODEL), row), pl.BlockSpec((tm, D_MODEL), row),
                   pl.BlockSpec((1, HG_DIM), lambda i: (0, 0))),
        compiler_params=_params(("arbitrary",)),
    )(dy, of, ob, z, norm_g)


def _lb_fwd(logits, *, name):
    w = logits.shape[1]

    def body(l_ref, o_ref):
        lg = l_ref[...]
        e = jnp.exp(lg - jnp.max(lg, axis=0, keepdims=True))
        sm = e / jnp.sum(e, axis=0, keepdims=True)
        o_ref[0:1, :] = sm[1:2]
        o_ref[1:2, :] = sm[1:2] + sm[2:3] + sm[3:4]

    return pl.pallas_call(body, name=name, out_shape=jax.ShapeDtypeStruct((2, w), F32))(logits)


def _lb_bwd(logits, dlb, *, name):
    w = logits.shape[1]

    def body(l_ref, d_ref, o_ref):
        lg = l_ref[...]
        e = jnp.exp(lg - jnp.max(lg, axis=0, keepdims=True))
        sm = e / jnp.sum(e, axis=0, keepdims=True)
        d1, d3 = d_ref[0:1, :], d_ref[1:2, :]
        dot = sm[1:2] * (d1 + d3) + (sm[2:3] + sm[3:4]) * d3
        o_ref[0:1, :] = -sm[0:1] * dot
        o_ref[1:2, :] = sm[1:2] * (d1 + d3 - dot)
        o_ref[2:3, :] = sm[2:3] * (d3 - dot)
        o_ref[3:4, :] = sm[3:4] * (d3 - dot)

    return pl.pallas_call(body, name=name, out_shape=jax.ShapeDtypeStruct((4, w), F32))(logits, dlb)


def _adamw(w, g, m, v, *, tr, name):
    rows = w.shape[0]
    parts = g.ndim == 3
    c1 = 1.0 / (1.0 - ADAM_B1 ** ADAM_STEP)
    c2 = 1.0 / (1.0 - ADAM_B2 ** ADAM_STEP)

    def body(w_ref, g_ref, m_ref, v_ref, go_ref, d_ref, mo_ref, vo_ref):
        if parts:
            gg = g_ref[0].astype(F32)
            for i in range(1, N_DEV):
                gg = gg + g_ref[i].astype(F32)
        else:
            gg = g_ref[...]
        mm = ADAM_B1 * m_ref[...] + (1.0 - ADAM_B1) * gg
        vv = ADAM_B2 * v_ref[...] + (1.0 - ADAM_B2) * (gg * gg)
        go_ref[...] = gg
        mo_ref[...] = mm
        vo_ref[...] = vv
        d_ref[...] = -ADAM_LR * ((mm * c1) / (jnp.sqrt(vv * c2) + ADAM_EPS) + ADAM_WD * w_ref[...])

    tile = pl.BlockSpec((tr, D_MODEL), lambda i: (i, 0))
    gspec = pl.BlockSpec((N_DEV, tr, D_MODEL), lambda i: (0, i, 0)) if parts else tile
    out = jax.ShapeDtypeStruct((rows, D_MODEL), F32)
    return pl.pallas_call(
        body, name=name, grid=(rows // tr,),
        out_shape=(out, out, out, out),
        in_specs=[tile, gspec, tile, tile], out_specs=(tile, tile, tile, tile),
        compiler_params=_params(("parallel",)),
    )(w, g, m, v)


def _sum8(parts, *, name):
    def body(p_ref, o_ref):
        tot = p_ref[0]
        for i in range(1, N_DEV):
            tot = tot + p_ref[i]
        o_ref[...] = tot

    return pl.pallas_call(body, name=name, out_shape=jax.ShapeDtypeStruct(parts.shape[1:], parts.dtype))(parts)


def _layer_params(i):
    j = i // 2
    mix = [("att_w_qkv", j, 1), ("att_w_o", j, 0)] if i % 2 == 0 else [("hgrn_w_in", j, 1), ("hgrn_w_o", j, 0)]
    return mix + [("ffn_w_in", i, 1), ("ffn_w_out", i, 0), ("ple_w_gate", i, 0), ("ple_w_proj", i, 1)]


def _pack_local(tree, i):
    return jnp.concatenate([tree[n][j].reshape(-1, D_MODEL) for n, j, _ in _layer_params(i)], axis=0)


def _unpack_local(packed, i, like):
    out, r = {}, 0
    for n, _, _ in _layer_params(i):
        shp = like[n].shape[1:]
        k = shp[0] * shp[1] // D_MODEL
        out[n] = packed[r:r + k].reshape(shp)
        r += k
    return out


def _unpack_gathered(gathered, i, like):
    out, r = {}, 0
    for n, _, ax in _layer_params(i):
        shp = like[n].shape[1:]
        k = shp[0] * shp[1] // D_MODEL
        t = gathered[:, r:r + k].reshape((N_DEV,) + shp)
        out[n] = (jnp.moveaxis(t, 0, 1).reshape(shp[0], N_DEV * shp[1]) if ax == 1
                  else t.reshape(N_DEV * shp[0], shp[1]))
        r += k
    return out


def _pack_full(grads, i, like):
    cols = []
    for n, _, ax in _layer_params(i):
        shp = like[n].shape[1:]
        t = (jnp.moveaxis(grads[n].reshape(shp[0], N_DEV, shp[1]), 1, 0) if ax == 1
             else grads[n].reshape(N_DEV, shp[0], shp[1]))
        cols.append(t.reshape(N_DEV, -1, D_MODEL).astype(BF16))
    return jnp.concatenate(cols, axis=1)


def _row_tile(rows):
    return max(t for t in range(16, 257, 16) if rows % t == 0)


SMALL_ROWS = 24


def _pad_row(a):
    flat = a.reshape(1, -1)
    return jnp.pad(flat, ((0, 0), (0, D_MODEL - flat.shape[1])))


def _tile(n, pref):
    return min(n, pref)


def kernel(x, p, att_w_qkv, att_sink, att_w_o, hgrn_w_in, hgrn_lb_logits, hgrn_norm_g, hgrn_w_o, ln_mix_g, ln_mix_b, ffn_w_in, ffn_w_out, ln_ffn_g, ln_ffn_b, ple_w_gate, ple_w_proj, loss_target, m_att_w_qkv, m_att_sink, m_att_w_o, m_hgrn_w_in, m_hgrn_lb_logits, m_hgrn_norm_g, m_hgrn_w_o, m_ln_mix_g, m_ln_mix_b, m_ffn_w_in, m_ffn_w_out, m_ln_ffn_g, m_ln_ffn_b, m_ple_w_gate, m_ple_w_proj, v_att_w_qkv, v_att_sink, v_att_w_o, v_hgrn_w_in, v_hgrn_lb_logits, v_hgrn_norm_g, v_hgrn_w_o, v_ln_mix_g, v_ln_mix_b, v_ffn_w_in, v_ffn_w_out, v_ln_ffn_g, v_ln_ffn_b, v_ple_w_gate, v_ple_w_proj):
    names = ["att_w_qkv", "att_sink", "att_w_o", "hgrn_w_in", "hgrn_lb_logits", "hgrn_norm_g", "hgrn_w_o", "ln_mix_g",
             "ln_mix_b", "ffn_w_in", "ffn_w_out", "ln_ffn_g", "ln_ffn_b", "ple_w_gate", "ple_w_proj"]
    w = dict(zip(names, (att_w_qkv, att_sink, att_w_o, hgrn_w_in, hgrn_lb_logits, hgrn_norm_g, hgrn_w_o, ln_mix_g,
                         ln_mix_b, ffn_w_in, ffn_w_out, ln_ffn_g, ln_ffn_b, ple_w_gate, ple_w_proj)))
    mom = dict(zip(names, (m_att_w_qkv, m_att_sink, m_att_w_o, m_hgrn_w_in, m_hgrn_lb_logits, m_hgrn_norm_g, m_hgrn_w_o,
                           m_ln_mix_g, m_ln_mix_b, m_ffn_w_in, m_ffn_w_out, m_ln_ffn_g, m_ln_ffn_b, m_ple_w_gate,
                           m_ple_w_proj)))
    var = dict(zip(names, (v_att_w_qkv, v_att_sink, v_att_w_o, v_hgrn_w_in, v_hgrn_lb_logits, v_hgrn_norm_g, v_hgrn_w_o,
                           v_ln_mix_g, v_ln_mix_b, v_ffn_w_in, v_ffn_w_out, v_ln_ffn_g, v_ln_ffn_b, v_ple_w_gate,
                           v_ple_w_proj)))
    s = x.shape[1]
    me = 4 * lax.axis_index("x") + 2 * lax.axis_index("y") + lax.axis_index("c")
    tm = _tile(s, 512)
    tbig = _tile(s, 1024)
    ts = _tile(s // 2, 512)
    x0 = x.reshape(s, D_MODEL)
    target = loss_target.reshape(s, D_MODEL)
    pl_in = p.reshape(DEPTH, s, PLE_DIM)

    w_rows = [_pack_local(w, i) for i in range(DEPTH)]
    full = _unpack_gathered(_gather(w_rows[0].astype(BF16), name="gather_weights"), 0, w)
    lb_rows = jnp.pad(hgrn_lb_logits.reshape(8, HG_DIM), ((0, 0), (0, D_MODEL - HG_DIM)))
    lb_all = _gather(lb_rows, name="gather_lb")[:, :, :HG_DIM]
    logits_full = jnp.moveaxis(lb_all, 0, 1).reshape(DEPTH, 2 * D_MODEL)
    lb = _lb_fwd(logits_full, name="lb_fwd")
    cos, sin = _rope_tables(s)

    saved = []
    xf, xb = x0, x0
    for i in range(DEPTH):
        j = i // 2
        sv = {"x": xf, "xb": xb, "w": full}
        nxt = (w_rows[i + 1].astype(BF16), True) if i + 1 < DEPTH else None
        if i % 2 == 0:
            z = _mm(xb, full["att_w_qkv"], n=D_MODEL + 2 * KV_DIM, tm=tbig, tn=512, tk=D_MODEL, name="att_in")
            o, *more = _att_fwd(z, w["att_sink"][j], cos, sin, comm=nxt, name="att_fwd")
            w_o = full["att_w_o"]
        else:
            z = _mm(xb, full["hgrn_w_in"], n=5 * D_MODEL, tm=tbig, tn=1024, tk=D_MODEL, name="hgrn_in")
            lbl = lb[j].reshape(2, HG_HEADS, 1, HG_DIM)
            of, st_f, qs_f, dec_f, *more = _hg_fwd(z, lbl[0], rev=False, ts=ts, comm=nxt, name="hgrn_fwd")
            ob, st_b, qs_b, dec_b = _hg_fwd(z, lbl[1], rev=True, ts=ts, name="hgrn_fwd_rev")
            o = _hg_post(of, ob, z, w["hgrn_norm_g"][j].reshape(1, HG_DIM), tm=tm, name="hgrn_post")
            w_o = full["hgrn_w_o"]
            sv.update(of=of, ob=ob, st_f=st_f, st_b=st_b, lbl=lbl, qs_f=qs_f, qs_b=qs_b, dec_f=dec_f, dec_b=dec_b)
        sv.update(z=z, o=o)
        pre1, x1, x1b = _proj_ln(o, w_o, xf, w["ln_mix_g"][i:i + 1], w["ln_mix_b"][i:i + 1], tm=tm, name="mix_out_ln")
        gg, uu, act = _ffn_in(x1b, full["ffn_w_in"], tm=tm, tn=FF_TILE, name="ffn_in")
        pre2, x2, x2b = _proj_ln(act, full["ffn_w_out"], x1, w["ln_ffn_g"][i:i + 1], w["ln_ffn_b"][i:i + 1], tm=tm,
                                 name="ffn_out_ln")
        xf, xb = _ple_fwd(x2, x2b, pl_in[i], full["ple_w_gate"], full["ple_w_proj"], tm=tm, name="ple_fwd")
        sv.update(pre1=pre1, x1=x1, x1b=x1b, g=gg, u=uu, act=act, pre2=pre2, x2b=x2b)
        saved.append(sv)
        if nxt is not None:
            full = _unpack_gathered(more[0], i + 1, w)

    dx, loss_blk = _loss_head(xf, target, tm=tm, name="loss_head")
    loss = lax.psum(loss_blk[0, 0], AXES)

    small = {n: [None] * DEPTH for n in ("ln_mix_g", "ln_mix_b", "ln_ffn_g", "ln_ffn_b")}
    dlb_rows = [None] * 4
    dnorm, dsink = [None] * 2, [None] * 2
    recv = [None] * DEPTH
    above = None
    mmw = functools.partial(_mm, ta=True, tk=_tile(s, 2048), out_dtype=BF16)
    for i in reversed(range(DEPTH)):
        j = i // 2
        sv = saved[i]
        full, gl = sv["w"], {}
        da, dpp, dy2, dy2b, small["ln_ffn_g"][i], small["ln_ffn_b"][i] = _ple_ln_bwd(
            dx, sv["x2b"], pl_in[i], full["ple_w_gate"], full["ple_w_proj"], sv["pre2"], w["ln_ffn_g"][i:i + 1],
            tm=tm, name="ple_ln_bwd")
        gl["ple_w_gate"] = mmw(sv["x2b"], da, n=D_MODEL, tm=D_MODEL, tn=D_MODEL, name="dw_ple_gate")
        gl["ple_w_proj"] = mmw(pl_in[i], dpp, n=D_MODEL, tm=PLE_DIM, tn=D_MODEL, name="dw_ple_proj")
        dg, du = _ffn_bwd_act(dy2b, full["ffn_w_out"], sv["g"], sv["u"], tm=tm, tn=FF_TILE, name="ffn_bwd_act")
        gl["ffn_w_out"] = mmw(sv["act"], dy2b, n=D_MODEL, tm=FF_TILE, tn=D_MODEL, name="dw_ffn_out")
        t = _mm(dg, full["ffn_w_in"], n=D_MODEL, tm=tbig, tn=D_MODEL, tk=FF_TILE, tb=True, add=dy2, add_scale=ALPHA,
                name="ffn_bwd_x")
        dy1, dy1b, small["ln_mix_g"][i], small["ln_mix_b"][i] = _mm(
            du, full["ffn_w_in"], n=D_MODEL, tm=tm, tn=D_MODEL, tk=FF_TILE, tb=True, add=t, bk_off=D_FF // FF_TILE,
            ln=(sv["pre1"], w["ln_mix_g"][i:i + 1]), name="ffn_bwd_x_u_ln")
        gl["ffn_w_in"] = jnp.concatenate(
            [mmw(sv["x1b"], dg, n=D_FF, tm=D_MODEL, tn=FF_TILE, name="dw_ffn_in"),
             mmw(sv["x1b"], du, n=D_FF, tm=D_MODEL, tn=FF_TILE, name="dw_ffn_in")], axis=1)
        n_out, n_inw, n_in = (("att_w_o", "att_w_qkv", D_MODEL + 2 * KV_DIM) if i % 2 == 0
                              else ("hgrn_w_o", "hgrn_w_in", 5 * D_MODEL))
        do = _mm(dy1b, full[n_out], n=D_MODEL, tm=tbig, tn=D_MODEL, tk=D_MODEL, tb=True, out_dtype=BF16,
                 name="mix_out_bwd")
        gl[n_out] = mmw(sv["o"], dy1b, n=D_MODEL, tm=D_MODEL, tn=D_MODEL, name="dw_mix_out")
        comm = (above, False) if above is not None else None
        if i % 2 == 0:
            dzq, part, dsk, *more = _att_bwd(sv["z"], do, w["att_sink"][j], cos, sin, comm=comm, name="att_bwd")
            dzkv = _att_bwd_kv(part, cos, sin, name="att_bwd_kv")
            dz = jnp.concatenate([dzq, dzkv], axis=1)
            dsink[j] = dsk[:, 0]
        else:
            dsum, dgate, dnorm[j] = _hg_post_bwd(do, sv["of"], sv["ob"], sv["z"], w["hgrn_norm_g"][j].reshape(1, HG_DIM),
                                                 tm=tm, name="hgrn_post_bwd")
            dq1, df1, dv1, dlb1, *more = _hg_bwd(sv["z"], sv["lbl"][0], sv["st_f"], sv["qs_f"], sv["dec_f"], dsum, None,
                                                 None, rev=False, ts=ts, comm=comm, name="hgrn_bwd")
            dq2, df2, dv2, dlb2 = _hg_bwd(sv["z"], sv["lbl"][1], sv["st_b"], sv["qs_b"], sv["dec_b"], dsum, dq1, dv1,
                                          rev=True, ts=ts, name="hgrn_bwd_rev")
            dz = jnp.concatenate([dq2.astype(BF16), df1.astype(BF16), df2.astype(BF16), dv2.astype(BF16), dgate], axis=1)
            dlb_rows[2 * j] = dlb1.reshape(1, D_MODEL)
            dlb_rows[2 * j + 1] = dlb2.reshape(1, D_MODEL)
        if comm is not None:
            recv[i + 1] = more[0]
        dx = _mm(dz, full[n_inw], n=D_MODEL, tm=tbig, tn=D_MODEL, tk=min(n_in, 2560), tb=True, add=dy1, add_scale=ALPHA,
                 name="mix_in_bwd")
        gl[n_inw] = mmw(sv["xb"], dz, n=n_in, tm=D_MODEL, tn=512, name="dw_mix_in")
        above = _pack_full(gl, i, w)
    grad_x = dx.reshape(x.shape)
    recv[0] = _exchange(above, name="exchange_grads")

    big_out = [{n: [None] * w[n].shape[0] for n, _ in BIG} for _ in range(4)]
    for i in range(DEPTH):
        outs = _adamw(w_rows[i], recv[i], _pack_local(mom, i), _pack_local(var, i), tr=_row_tile(w_rows[i].shape[0]),
                      name="adamw_big")
        for kind, packed in enumerate(outs):
            for (n, j, _), piece in zip(_layer_params(i), _unpack_local(packed, i, w).values()):
                big_out[kind][n][j] = piece
    big_out = [{n: jnp.stack(v) for n, v in kind.items()} for kind in big_out]

    small_rows = jnp.concatenate(
        [jnp.concatenate(small[n], axis=0) for n in ("ln_mix_g", "ln_mix_b", "ln_ffn_g", "ln_ffn_b")] + dlb_rows
        + [_pad_row(jnp.stack(dnorm)), _pad_row(jnp.stack(dsink)), jnp.zeros((2, D_MODEL), F32)], axis=0)
    small_all = _gather(small_rows, name="gather_small")
    lbw, lbm, lbv = (t.reshape(4, 2 * HG_DIM) for t in (hgrn_lb_logits, mom["hgrn_lb_logits"], var["hgrn_lb_logits"]))
    summed = _sum8(small_all, name="sum_small")
    dlb_mine = lax.dynamic_slice_in_dim(summed[16:20].reshape(2, 2, HG_HEADS, HG_DIM), me, 1, axis=2)
    dlogits = _lb_bwd(lbw, dlb_mine.reshape(2, 2 * HG_DIM), name="lb_bwd")

    def small_pack(ln4, lbt, ng, sk):
        return jnp.concatenate([ln4[n] for n in ("ln_mix_g", "ln_mix_b", "ln_ffn_g", "ln_ffn_b")]
                               + [_pad_row(lbt), _pad_row(ng), _pad_row(sk), jnp.zeros((5, D_MODEL), F32)], axis=0)

    g_small = jnp.concatenate([summed[:16], _pad_row(dlogits), summed[20:22], jnp.zeros((5, D_MODEL), F32)], axis=0)
    souts = _adamw(small_pack(w, lbw, w["hgrn_norm_g"], w["att_sink"]), g_small,
                   small_pack(mom, lbm, mom["hgrn_norm_g"], mom["att_sink"]),
                   small_pack(var, lbv, var["hgrn_norm_g"], var["att_sink"]), tr=SMALL_ROWS, name="adamw_small")

    def small_unpack(t):
        out = {n: t[4 * k:4 * k + 4] for k, n in enumerate(("ln_mix_g", "ln_mix_b", "ln_ffn_g", "ln_ffn_b"))}
        out["hgrn_lb_logits"] = t[16].reshape(hgrn_lb_logits.shape)
        out["hgrn_norm_g"] = t[17, :2 * HG_DIM].reshape(hgrn_norm_g.shape)
        out["att_sink"] = t[18, :2 * N_Q_HEADS].reshape(att_sink.shape)
        return out

    result = [loss, grad_x]
    for big_t, small_t in zip(big_out, souts):
        merged = dict(big_t)
        merged.update(small_unpack(small_t))
        result += [merged[n] for n in names]
    return tuple(result)
```

```python
import functools
import math

import jax
import jax.numpy as jnp
from jax import lax
from jax.experimental import pallas as pl
from jax.experimental.pallas import tpu as pltpu

F32 = jnp.float32
BF16 = jnp.bfloat16

D_MODEL = 1024
DEPTH = 4
HEAD_DIM = 64
N_Q_HEADS = 16
N_KV_HEADS = 4
GROUP = 4
KV_DIM = 256
ATT_BLOCK = 128
ROPE_DIM = 16
ROPE_THETA = 500000.0
HG_HEADS = 8
HG_DIM = 128
HG_CHUNK = 64
HG_SUB = 16
D_FF = 2816
FF_TILE = 1408
SUB_ROWS = 256
PLE_DIM = 256
ALPHA = (2 * DEPTH) ** 0.25
LN_EPS = 1e-5
ADAM_LR, ADAM_B1, ADAM_B2, ADAM_EPS, ADAM_WD, ADAM_STEP = 0.001, 0.9, 0.999, 1e-08, 0.01, 10

N_DEV = 8
LANES = 128
VMEM_LIMIT = 52 * 1024 * 1024
NEG = -1e30
MESH = pl.DeviceIdType.MESH
AXES = ("x", "y", "c")

BIG = (("att_w_qkv", 2), ("att_w_o", 1), ("hgrn_w_in", 2), ("hgrn_w_o", 1), ("ffn_w_in", 2), ("ffn_w_out", 1),
       ("ple_w_gate", 1), ("ple_w_proj", 2))


def _params(sem=None, vmem=VMEM_LIMIT):
    return pltpu.CompilerParams(dimension_semantics=sem, vmem_limit_bytes=vmem)


def _sigmoid(x):
    return jax.nn.sigmoid(x)


def _direct_copies(src_ref, out_ref, send_sems, recv_sems, local_sem, gather, arrivals):
    x, y, c = lax.axis_index("x"), lax.axis_index("y"), lax.axis_index("c")
    me = 4 * x + 2 * y + c
    mine = (lambda j: src_ref) if gather else (lambda j: src_ref.at[j])
    pairs = []
    for k in range(1, N_DEV):
        px, py, pc = x ^ (k >> 2), y ^ ((k >> 1) & 1), c ^ (k & 1)
        peer = 4 * px + 2 * py + pc
        send = pltpu.make_async_remote_copy(
            src_ref=mine(peer), dst_ref=out_ref.at[me], send_sem=send_sems.at[k], recv_sem=recv_sems.at[k],
            device_id=(px, py, pc), device_id_type=MESH)
        arrival = pltpu.make_async_remote_copy(
            src_ref=mine(peer), dst_ref=out_ref.at[peer], send_sem=send_sems.at[k], recv_sem=recv_sems.at[k],
            device_id=(x, y, c), device_id_type=MESH) if arrivals else None
        pairs.append((send, arrival))
    return pltpu.make_async_copy(mine(me), out_ref.at[me], local_sem), pairs


def _direct_start(*refs, gather):
    local, pairs = _direct_copies(*refs, gather, False)
    local.start()
    for send, _ in pairs:
        send.start()


def _direct_wait(*refs, gather):
    local, pairs = _direct_copies(*refs, gather, True)
    for send, arrival in pairs:
        send.wait_send()
        arrival.wait_recv()
    local.wait()


COMM_SCRATCH = [pltpu.SemaphoreType.DMA((N_DEV,)), pltpu.SemaphoreType.DMA((N_DEV,)), pltpu.SemaphoreType.DMA]


def _exchange(src, *, gather=False, name):
    def body(*refs):
        _direct_start(*refs, gather=gather)
        _direct_wait(*refs, gather=gather)

    blk = tuple(src.shape) if gather else tuple(src.shape[1:])
    return pl.pallas_call(
        body, name=name,
        out_shape=jax.ShapeDtypeStruct((N_DEV,) + blk, src.dtype),
        in_specs=[pl.BlockSpec(memory_space=pltpu.HBM)],
        out_specs=pl.BlockSpec(memory_space=pltpu.HBM),
        scratch_shapes=COMM_SCRATCH,
    )(src)


def _call(body, *, name, grid, out_shape, in_specs, out_specs, args, scratch_shapes=(), sem, comm=None, edge=None):
    out_shape, out_specs = tuple(out_shape), tuple(out_specs)
    if comm is None:
        return pl.pallas_call(body, name=name, grid=grid, out_shape=out_shape, in_specs=list(in_specs),
                              out_specs=out_specs, scratch_shapes=list(scratch_shapes),
                              compiler_params=_params(sem))(*args)
    src, gather = comm
    n_in, n_out, n_scr = len(args), len(out_shape), len(scratch_shapes)
    blk = tuple(src.shape) if gather else tuple(src.shape[1:])
    hbm = pl.BlockSpec(memory_space=pltpu.HBM)

    def carrying(*refs):
        ins, src_ref = refs[:n_in], refs[n_in]
        outs, dst_ref = refs[n_in + 1:n_in + 1 + n_out], refs[n_in + 1 + n_out]
        own = refs[n_in + 2 + n_out:n_in + 2 + n_out + n_scr]
        comm_refs = (src_ref, dst_ref) + tuple(refs[n_in + 2 + n_out + n_scr:])
        first, last = edge()

        @pl.when(first)
        def _():
            _direct_start(*comm_refs, gather=gather)

        body(*ins, *outs, *own)

        @pl.when(last)
        def _():
            _direct_wait(*comm_refs, gather=gather)

    return pl.pallas_call(
        carrying, name=name, grid=grid,
        out_shape=out_shape + (jax.ShapeDtypeStruct((N_DEV,) + blk, src.dtype),),
        in_specs=list(in_specs) + [hbm], out_specs=out_specs + (hbm,),
        scratch_shapes=list(scratch_shapes) + COMM_SCRATCH,
        compiler_params=_params(("arbitrary",) * len(grid)),
    )(*args, src)


def _gather(src, *, name):
    def body(src_ref, out_ref, send_sems, recv_sems, local_sem):
        x, y, c = lax.axis_index("x"), lax.axis_index("y"), lax.axis_index("c")
        sibling = (x, y, 1 - c)
        chips = [(1 - x, y), (x, 1 - y), (1 - x, 1 - y)]

        def rows(px, py, pc):
            return out_ref.at[4 * px + 2 * py + pc]

        def copy(k, block, to, from_src=False):
            return pltpu.make_async_remote_copy(
                src_ref=src_ref if from_src else rows(*block), dst_ref=rows(*block), send_sem=send_sems.at[k],
                recv_sem=recv_sems.at[k], device_id=to, device_id_type=MESH)

        me = (x, y, c)
        mine = pltpu.make_async_copy(src_ref, rows(*me), local_sem)
        mine.start()
        first = [copy(0, me, sibling, from_src=True)]
        first += [copy(1 + j, me, (*chip, c), from_src=True) for j, chip in enumerate(chips)]
        for cp in first:
            cp.start()
        passed = [copy(4 + j, (*chip, c), sibling) for j, chip in enumerate(chips)]
        for j, chip in enumerate(chips):
            copy(1 + j, (*chip, c), me).wait_recv()
            passed[j].start()
        copy(0, sibling, me).wait_recv()
        for j, chip in enumerate(chips):
            copy(4 + j, (*chip, 1 - c), me).wait_recv()
        for cp in first + passed:
            cp.wait_send()
        mine.wait()

    return pl.pallas_call(
        body, name=name,
        out_shape=jax.ShapeDtypeStruct((N_DEV,) + tuple(src.shape), src.dtype),
        in_specs=[pl.BlockSpec(memory_space=pltpu.HBM)],
        out_specs=pl.BlockSpec(memory_space=pltpu.HBM),
        scratch_shapes=[pltpu.SemaphoreType.DMA((7,)), pltpu.SemaphoreType.DMA((7,)), pltpu.SemaphoreType.DMA],
    )(src)


def _mm(a, b, *, n, tm, tn, tk, ta=False, tb=False, out_dtype=F32, add=None, add_scale=1.0, bk_off=0, ln=None, name):
    m, kdim = (a.shape[1], a.shape[0]) if ta else a.shape
    nk = kdim // tk
    dims = (((0 if ta else 1,), (1 if tb else 0,)), ((), ()))

    def body(*refs):
        a_ref, b_ref = refs[:2]
        nxt = 2
        add_ref = None
        if add is not None:
            add_ref = refs[nxt]
            nxt += 1
        if ln is not None:
            pre_ref, g_ref = refs[nxt:nxt + 2]
            nxt += 2
        outs = refs[nxt:nxt + (4 if ln is not None else 1)]
        acc_ref = refs[-1] if nk > 1 else None
        part = lax.dot_general(a_ref[...].astype(BF16), b_ref[...].astype(BF16), dims, preferred_element_type=F32)

        def finish(r):
            if add_ref is not None:
                r = r + add_scale * add_ref[...]
            if ln is None:
                outs[0][...] = r.astype(out_dtype)
            else:
                _ln_bwd_store(r, pre_ref[...], g_ref[...], *outs, first=pl.program_id(0) == 0)

        if nk == 1:
            finish(part)
        else:
            k = pl.program_id(2)

            @pl.when(k == 0)
            def _():
                acc_ref[...] = part

            @pl.when(k > 0)
            def _():
                acc_ref[...] += part

            @pl.when(k == nk - 1)
            def _():
                finish(acc_ref[...])

    a_spec = pl.BlockSpec((tk, tm), lambda i, j, k: (k, i)) if ta else pl.BlockSpec((tm, tk), lambda i, j, k: (i, k))
    b_spec = (pl.BlockSpec((tn, tk), lambda i, j, k: (j, k + bk_off)) if tb
              else pl.BlockSpec((tk, tn), lambda i, j, k: (k + bk_off, j)))
    in_specs, args = [a_spec, b_spec], [a, b]
    tile = pl.BlockSpec((tm, tn), lambda i, j, k: (i, j))
    if add is not None:
        in_specs.append(tile)
        args.append(add)
    out_shape, out_specs = jax.ShapeDtypeStruct((m, n), out_dtype), tile
    if ln is not None:
        assert tn == n == D_MODEL
        vec = pl.BlockSpec((1, D_MODEL), lambda i, j, k: (0, 0))
        in_specs += [tile, vec]
        args += list(ln)
        out_shape = (jax.ShapeDtypeStruct((m, n), F32), jax.ShapeDtypeStruct((m, n), BF16),
                     jax.ShapeDtypeStruct((1, D_MODEL), F32), jax.ShapeDtypeStruct((1, D_MODEL), F32))
        out_specs = (tile, tile, vec, vec)
    return pl.pallas_call(
        body, name=name, grid=(m // tm, n // tn, nk),
        out_shape=out_shape, in_specs=in_specs, out_specs=out_specs,
        scratch_shapes=[pltpu.VMEM((tm, tn), F32)] if nk > 1 else [],
        compiler_params=_params(("arbitrary",) * 3 if ln is not None else ("parallel", "parallel", "arbitrary")),
    )(*args)


def _ln_bwd_rows(do, y, g):
    mu = jnp.mean(y, axis=-1, keepdims=True)
    yc = y - mu
    var = jnp.mean(yc * yc, axis=-1, keepdims=True)
    rstd = lax.rsqrt(var + LN_EPS)
    xhat = yc * rstd
    dxhat = do * g
    dy = rstd * (dxhat - jnp.mean(dxhat, axis=-1, keepdims=True) - xhat * jnp.mean(dxhat * xhat, axis=-1, keepdims=True))
    return dy, jnp.sum(do * xhat, axis=0, keepdims=True), jnp.sum(do, axis=0, keepdims=True)


def _accumulate(ref, val, first):
    @pl.when(first)
    def _():
        ref[...] = val

    @pl.when(jnp.logical_not(first))
    def _():
        ref[...] += val


def _ln_bwd_store(do, y, g, dy_ref, dybf_ref, dg_ref, db_ref, *, first):
    dy, pg, pb = _ln_bwd_rows(do, y, g)
    dy_ref[...] = dy
    dybf_ref[...] = dy.astype(BF16)
    _accumulate(dg_ref, pg, first)
    _accumulate(db_ref, pb, first)


def _layer_norm_rows(y, g, b):
    mu = jnp.mean(y, axis=-1, keepdims=True)
    yc = y - mu
    var = jnp.mean(yc * yc, axis=-1, keepdims=True)
    return yc * lax.rsqrt(var + LN_EPS) * g + b


def _proj_ln(a, w, res, g, b, *, tm, name):
    s, kdim = a.shape

    def body(a_ref, w_ref, res_ref, g_ref, b_ref, pre_ref, o_ref, obf_ref):
        for rs in _row_parts(tm):
            h = jnp.dot(a_ref[rs, :], w_ref[...], preferred_element_type=F32)
            pre = ALPHA * res_ref[rs, :] + h
            out = _layer_norm_rows(pre, g_ref[...], b_ref[...])
            pre_ref[rs, :] = pre
            o_ref[rs, :] = out
            obf_ref[rs, :] = out.astype(BF16)

    row = lambda i: (i, 0)
    fix = lambda i: (0, 0)
    return pl.pallas_call(
        body, name=name, grid=(s // tm,),
        out_shape=(jax.ShapeDtypeStruct((s, D_MODEL), F32), jax.ShapeDtypeStruct((s, D_MODEL), F32),
                   jax.ShapeDtypeStruct((s, D_MODEL), BF16)),
        in_specs=[pl.BlockSpec((tm, kdim), row), pl.BlockSpec((kdim, D_MODEL), fix), pl.BlockSpec((tm, D_MODEL), row),
                  pl.BlockSpec((1, D_MODEL), fix), pl.BlockSpec((1, D_MODEL), fix)],
        out_specs=(pl.BlockSpec((tm, D_MODEL), row),) * 3,
        compiler_params=_params(("parallel",)),
    )(a, w, res, g, b)


def _row_parts(tm):
    sub = min(tm, SUB_ROWS)
    return [pl.ds(r * sub, sub) for r in range(tm // sub)]


def _ffn_in(xbf, w, *, tm, tn, name):
    s = xbf.shape[0]
    nj = D_FF // tn

    def body(x_ref, wg_ref, wu_ref, g_ref, u_ref, act_ref):
        for rs in _row_parts(tm):
            xv = x_ref[rs, :]
            gg = jnp.dot(xv, wg_ref[...], preferred_element_type=F32)
            uu = jnp.dot(xv, wu_ref[...], preferred_element_type=F32)
            g_ref[rs, :] = gg.astype(BF16)
            u_ref[rs, :] = uu.astype(BF16)
            act_ref[rs, :] = (gg * _sigmoid(gg) * uu).astype(BF16)

    out = jax.ShapeDtypeStruct((s, D_FF), BF16)
    tile = pl.BlockSpec((tm, tn), lambda j, i: (i, j))
    return pl.pallas_call(
        body, name=name, grid=(nj, s // tm),
        out_shape=(out, out, out),
        in_specs=[pl.BlockSpec((tm, D_MODEL), lambda j, i: (i, 0)), pl.BlockSpec((D_MODEL, tn), lambda j, i: (0, j)),
                  pl.BlockSpec((D_MODEL, tn), lambda j, i: (0, j + nj))],
        out_specs=(tile, tile, tile),
        compiler_params=_params(("parallel", "parallel")),
    )(xbf, w, w)


def _ffn_bwd_act(dybf, w_out, g, u, *, tm, tn, name):
    s = dybf.shape[0]

    def body(dy_ref, w_ref, g_ref, u_ref, dg_ref, du_ref):
        for rs in _row_parts(tm):
            dact = lax.dot_general(dy_ref[rs, :], w_ref[...], (((1,), (1,)), ((), ())), preferred_element_type=F32)
            gg = g_ref[rs, :].astype(F32)
            uu = u_ref[rs, :].astype(F32)
            sg = _sigmoid(gg)
            dg_ref[rs, :] = (dact * uu * sg * (1.0 + gg * (1.0 - sg))).astype(BF16)
            du_ref[rs, :] = (dact * gg * sg).astype(BF16)

    out = jax.ShapeDtypeStruct((s, D_FF), BF16)
    tile = pl.BlockSpec((tm, tn), lambda j, i: (i, j))
    return pl.pallas_call(
        body, name=name, grid=(D_FF // tn, s // tm),
        out_shape=(out, out),
        in_specs=[pl.BlockSpec((tm, D_MODEL), lambda j, i: (i, 0)), pl.BlockSpec((tn, D_MODEL), lambda j, i: (j, 0)),
                  tile, tile],
        out_specs=(tile, tile),
        compiler_params=_params(("parallel", "parallel")),
    )(dybf, w_out, g, u)


def _ffn_out_ple(act, w_out, res, g, b, p, w_gate, w_proj, *, tm, name):
    s = act.shape[0]

    def body(a_ref, w_ref, res_ref, g_ref, b_ref, p_ref, wg_ref, wp_ref, pre_ref, x2bf_ref, o_ref, obf_ref):
        for rs in _row_parts(tm):
            pre = ALPHA * res_ref[rs, :] + jnp.dot(a_ref[rs, :], w_ref[...], preferred_element_type=F32)
            x2 = _layer_norm_rows(pre, g_ref[...], b_ref[...])
            x2bf = x2.astype(BF16)
            pre_ref[rs, :] = pre
            x2bf_ref[rs, :] = x2bf
            gate = jnp.dot(x2bf, wg_ref[...], preferred_element_type=F32)
            pp = jnp.dot(p_ref[rs, :].astype(BF16), wp_ref[...], preferred_element_type=F32)
            out = x2 + _sigmoid(gate) * pp
            o_ref[rs, :] = out
            obf_ref[rs, :] = out.astype(BF16)

    row = lambda i: (i, 0)
    fix = lambda i: (0, 0)
    tile = pl.BlockSpec((tm, D_MODEL), row)
    vec = pl.BlockSpec((1, D_MODEL), fix)
    act_t = lambda dt: jax.ShapeDtypeStruct((s, D_MODEL), dt)
    return pl.pallas_call(
        body, name=name, grid=(s // tm,),
        out_shape=(act_t(F32), act_t(BF16), act_t(F32), act_t(BF16)),
        in_specs=[pl.BlockSpec((tm, D_FF), row), pl.BlockSpec((D_FF, D_MODEL), fix), tile, vec, vec,
                  pl.BlockSpec((tm, PLE_DIM), row), pl.BlockSpec((D_MODEL, D_MODEL), fix),
                  pl.BlockSpec((PLE_DIM, D_MODEL), fix)],
        out_specs=(tile, tile, tile, tile),
        compiler_params=_params(("parallel",)),
    )(act, w_out, res, g, b, p, w_gate, w_proj)


def _ple_ln_bwd(dx3, x2bf, p, w_gate, w_proj, pre, g, *, tm, name):
    s = dx3.shape[0]

    def body(d_ref, xbf_ref, p_ref, wg_ref, wp_ref, pre_ref, g_ref, da_ref, dpp_ref, dy_ref, dybf_ref, dg_ref, db_ref):
        pg = jnp.zeros((1, D_MODEL), F32)
        pb = jnp.zeros((1, D_MODEL), F32)
        for rs in _row_parts(tm):
            d = d_ref[rs, :]
            a = jnp.dot(xbf_ref[rs, :], wg_ref[...], preferred_element_type=F32)
            pp = jnp.dot(p_ref[rs, :].astype(BF16), wp_ref[...], preferred_element_type=F32)
            sg = _sigmoid(a)
            da = (d * pp * sg * (1.0 - sg)).astype(BF16)
            da_ref[rs, :] = da
            dpp_ref[rs, :] = (d * sg).astype(BF16)
            dx2 = d + lax.dot_general(da, wg_ref[...], (((1,), (1,)), ((), ())), preferred_element_type=F32)
            dy, qg, qb = _ln_bwd_rows(dx2, pre_ref[rs, :], g_ref[...])
            dy_ref[rs, :] = dy
            dybf_ref[rs, :] = dy.astype(BF16)
            pg, pb = pg + qg, pb + qb
        _accumulate(dg_ref, pg, pl.program_id(0) == 0)
        _accumulate(db_ref, pb, pl.program_id(0) == 0)

    row = lambda i: (i, 0)
    fix = lambda i: (0, 0)
    tile = pl.BlockSpec((tm, D_MODEL), row)
    vec = pl.BlockSpec((1, D_MODEL), fix)
    act = lambda dt: jax.ShapeDtypeStruct((s, D_MODEL), dt)
    return pl.pallas_call(
        body, name=name, grid=(s // tm,),
        out_shape=(act(BF16), act(BF16), act(F32), act(BF16), jax.ShapeDtypeStruct((1, D_MODEL), F32),
                   jax.ShapeDtypeStruct((1, D_MODEL), F32)),
        in_specs=[tile, tile, pl.BlockSpec((tm, PLE_DIM), row), pl.BlockSpec((D_MODEL, D_MODEL), fix),
                  pl.BlockSpec((PLE_DIM, D_MODEL), fix), tile, vec],
        out_specs=(tile, tile, tile, tile, vec, vec),
        compiler_params=_params(("arbitrary",)),
    )(dx3, x2bf, p, w_gate, w_proj, pre, g)


def _loss_head(y, target, *, tm, name):
    s = y.shape[0]

    def body(y_ref, t_ref, dy_ref, loss_ref, acc_ref):
        err = y_ref[...] - t_ref[...]
        dy_ref[...] = err * (1.0 / D_MODEL)
        part = jnp.sum(err * err, axis=0, keepdims=True)

        @pl.when(pl.program_id(0) == 0)
        def _():
            acc_ref[...] = part

        @pl.when(pl.program_id(0) > 0)
        def _():
            acc_ref[...] += part

        @pl.when(pl.program_id(0) == pl.num_programs(0) - 1)
        def _():
            tot = jnp.sum(acc_ref[...], axis=1, keepdims=True) * (0.5 / D_MODEL)
            loss_ref[...] = jnp.broadcast_to(tot, (8, LANES))

    row = lambda i: (i, 0)
    return pl.pallas_call(
        body, name=name, grid=(s // tm,),
        out_shape=(jax.ShapeDtypeStruct((s, D_MODEL), F32), jax.ShapeDtypeStruct((8, LANES), F32)),
        in_specs=[pl.BlockSpec((tm, D_MODEL), row), pl.BlockSpec((tm, D_MODEL), row)],
        out_specs=(pl.BlockSpec((tm, D_MODEL), row), pl.BlockSpec((8, LANES), lambda i: (0, 0))),
        scratch_shapes=[pltpu.VMEM((1, D_MODEL), F32)],
        compiler_params=_params(("arbitrary",)),
    )(y, target)


def _rope_tables(s):
    inv = ROPE_THETA ** (-jnp.arange(0, ROPE_DIM, 2, dtype=F32) / ROPE_DIM)
    ang = jnp.arange(s, dtype=F32)[:, None] * inv[None, :]
    cos, sin = jnp.cos(ang), jnp.sin(ang)
    ones = jnp.ones((s, HEAD_DIM - ROPE_DIM), F32)
    c_head = jnp.concatenate([cos, cos, ones], axis=1)
    s_head = jnp.concatenate([-sin, sin, 0.0 * ones], axis=1)
    return jnp.concatenate([c_head, c_head], axis=1), jnp.concatenate([s_head, s_head], axis=1)


def _rope(v, cos, sin):
    n = v.shape[1] // LANES
    width = v.shape[1]
    cos_w = jnp.tile(cos, (1, n)) if n > 1 else cos
    sin_w = jnp.tile(sin, (1, n)) if n > 1 else sin
    dim = lax.broadcasted_iota(jnp.int32, (1, width), 1) % HEAD_DIM
    partner = jnp.where(dim < ROPE_DIM // 2, pltpu.roll(v, width - ROPE_DIM // 2, 1), pltpu.roll(v, ROPE_DIM // 2, 1))
    return v * cos_w + partner * sin_w


def _unrope(dv, cos, sin):
    n = dv.shape[1] // LANES
    width = dv.shape[1]
    cos_w = jnp.tile(cos, (1, n)) if n > 1 else cos
    sin_w = jnp.tile(sin, (1, n)) if n > 1 else sin
    t = dv * sin_w
    dim = lax.broadcasted_iota(jnp.int32, (1, width), 1) % HEAD_DIM
    partner = jnp.where(dim < ROPE_DIM // 2, pltpu.roll(t, width - ROPE_DIM // 2, 1),
                        jnp.where(dim < ROPE_DIM, pltpu.roll(t, ROPE_DIM // 2, 1), 0.0))
    return dv * cos_w + partner


def _att_mask(i, nb):
    rows = GROUP * ATT_BLOCK
    r = lax.broadcasted_iota(jnp.int32, (rows, 3 * ATT_BLOCK), 0) % ATT_BLOCK
    cidx = lax.broadcasted_iota(jnp.int32, (rows, 3 * ATT_BLOCK), 1)
    rel = r + ATT_BLOCK - cidx
    ok = (rel <= ATT_BLOCK) & (rel >= -ATT_BLOCK)
    ok = ok & ((cidx >= ATT_BLOCK) | (i > 0)) & ((cidx < 2 * ATT_BLOCK) | (i < nb - 1))
    return ok


def _att_mask_t(i, nb):
    cols = GROUP * ATT_BLOCK
    cidx = lax.broadcasted_iota(jnp.int32, (3 * ATT_BLOCK, cols), 0)
    r = lax.broadcasted_iota(jnp.int32, (3 * ATT_BLOCK, cols), 1) % ATT_BLOCK
    rel = r + ATT_BLOCK - cidx
    ok = (rel <= ATT_BLOCK) & (rel >= -ATT_BLOCK)
    return ok & ((cidx >= ATT_BLOCK) | (i > 0)) & ((cidx < 2 * ATT_BLOCK) | (i < nb - 1))


def _sink_lanes(sink_ref, h):
    cols = GROUP * ATT_BLOCK
    grp = lax.broadcasted_iota(jnp.int32, (1, cols), 1) // ATT_BLOCK
    out = jnp.zeros((1, cols), F32)
    for gq in range(GROUP):
        out = jnp.where(grp == gq, sink_ref[GROUP * h + gq], out)
    return out


def _half_mask(half):
    lane = lax.broadcasted_iota(jnp.int32, (1, LANES), 1)
    return (lane // HEAD_DIM) == half


def _stack_q(q, h):
    parts = []
    for gq in range(GROUP):
        n = GROUP * h + gq
        grp = q[:, LANES * (n // 2):LANES * (n // 2 + 1)]
        grp = jnp.where(_half_mask(n % 2), grp, 0.0)
        if n % 2 != h % 2:
            grp = pltpu.roll(grp, HEAD_DIM, 1)
        parts.append(grp)
    return jnp.concatenate(parts, axis=0)


def _unstack_q(stacked, h, acc):
    for gq in range(GROUP):
        n = GROUP * h + gq
        grp = stacked[ATT_BLOCK * gq:ATT_BLOCK * (gq + 1), :]
        grp = jnp.where(_half_mask(h % 2), grp, 0.0)
        if n % 2 != h % 2:
            grp = pltpu.roll(grp, HEAD_DIM, 1)
        acc[n // 2] = grp if acc[n // 2] is None else acc[n // 2] + grp
    return acc


def _sink_rows(sink_ref, h):
    rows = GROUP * ATT_BLOCK
    grp = lax.broadcasted_iota(jnp.int32, (rows, 1), 0) // ATT_BLOCK
    out = jnp.zeros((rows, 1), F32)
    for gq in range(GROUP):
        out = jnp.where(grp == gq, sink_ref[GROUP * h + gq], out)
    return out


def _att_probs(qs, kh, sink, valid):
    s = lax.dot_general(qs, kh, (((1,), (1,)), ((), ())), preferred_element_type=F32)
    s = jnp.where(valid, s, NEG)
    m = jnp.maximum(jnp.max(s, axis=-1, keepdims=True), sink)
    p = jnp.exp(s - m)
    es = jnp.exp(sink - m)
    den = jnp.sum(p, axis=-1, keepdims=True) + es
    inv = 1.0 / den
    return p * inv, es * inv


def _att_specs(nb):
    prev = lambda i: (jnp.maximum(i - 1, 0), 0)
    cur = lambda i: (i, 0)
    nxt = lambda i: (jnp.minimum(i + 1, nb - 1), 0)
    kv = lambda f: (lambda i: (f(i)[0], 2))
    tab = [pl.BlockSpec((ATT_BLOCK, LANES), f) for f in (cur, prev, cur, nxt)]
    z = [pl.BlockSpec((ATT_BLOCK, D_MODEL), cur)] + [pl.BlockSpec((ATT_BLOCK, 2 * KV_DIM), kv(f)) for f in (prev, cur, nxt)]
    return z, tab


def _att_load(zq_ref, kp_ref, kc_ref, kn_ref, cq_ref, sq_ref, cp_ref, sp_ref, cc_ref, sc_ref, cn_ref, sn_ref):
    q = (_rope(zq_ref[...], cq_ref[...], sq_ref[...]) * (HEAD_DIM ** -0.5))
    ks, vs = [], []
    for ref, c_ref, s_ref in ((kp_ref, cp_ref, sp_ref), (kc_ref, cc_ref, sc_ref), (kn_ref, cn_ref, sn_ref)):
        kvb = ref[...]
        ks.append(_rope(kvb[:, :KV_DIM], c_ref[...], s_ref[...]))
        vs.append(kvb[:, KV_DIM:])
    return q, jnp.concatenate(ks, axis=0).astype(BF16), jnp.concatenate(vs, axis=0).astype(BF16)


def _att_fwd(z, sink, cos, sin, *, comm=None, name):
    s = z.shape[0]
    nb = s // ATT_BLOCK

    def body(zq_ref, kp_ref, kc_ref, kn_ref, cq_ref, cp_ref, cc_ref, cn_ref, sq_ref, sp_ref, sc_ref, sn_ref, sink_ref,
             o_ref):
        i = pl.program_id(0)
        q, k, v = _att_load(zq_ref, kp_ref, kc_ref, kn_ref, cq_ref, sq_ref, cp_ref, sp_ref, cc_ref, sc_ref, cn_ref, sn_ref)
        valid = _att_mask(i, nb)
        acc = [None] * (N_Q_HEADS // 2)
        for h in range(N_KV_HEADS):
            lanes = slice(LANES * (h // 2), LANES * (h // 2 + 1))
            qs = _stack_q(q, h).astype(BF16)
            prob, _ = _att_probs(qs, k[:, lanes], _sink_rows(sink_ref, h), valid)
            oh = jnp.dot(prob.astype(BF16), v[:, lanes], preferred_element_type=F32)
            acc = _unstack_q(oh, h, acc)
        o_ref[...] = jnp.concatenate(acc, axis=1).astype(BF16)

    zspecs, tab = _att_specs(nb)
    return _call(
        body, name=name, grid=(nb,),
        out_shape=(jax.ShapeDtypeStruct((s, D_MODEL), BF16),),
        in_specs=zspecs + tab + tab + [pl.BlockSpec(memory_space=pltpu.SMEM)],
        out_specs=(pl.BlockSpec((ATT_BLOCK, D_MODEL), lambda i: (i, 0)),),
        args=(z, z, z, z, cos, cos, cos, cos, sin, sin, sin, sin, sink), sem=("parallel",), comm=comm,
        edge=lambda: (pl.program_id(0) == 0, pl.program_id(0) == nb - 1))


def _att_bwd(z, do, sink, cos, sin, *, comm=None, name):
    s = z.shape[0]
    nb = s // ATT_BLOCK

    def body(zq_ref, kp_ref, kc_ref, kn_ref, cq_ref, cp_ref, cc_ref, cn_ref, sq_ref, sp_ref, sc_ref, sn_ref, sink_ref,
             do_ref, dq_ref, part_ref, dsink_ref):
        i = pl.program_id(0)
        q, k, v = _att_load(zq_ref, kp_ref, kc_ref, kn_ref, cq_ref, sq_ref, cp_ref, sp_ref, cc_ref, sc_ref, cn_ref, sn_ref)
        valid = _att_mask_t(i, nb)
        dout = do_ref[...].astype(F32)
        dq_acc = [None] * (N_Q_HEADS // 2)
        dk_acc = [None] * 2
        dv_acc = [None] * 2
        rows = []
        nt = (((1,), (1,)), ((), ()))
        for h in range(N_KV_HEADS):
            grp = h // 2
            lanes = slice(LANES * grp, LANES * (grp + 1))
            qs = _stack_q(q, h).astype(BF16)
            dos = _stack_q(dout, h).astype(BF16)
            sink = _sink_lanes(sink_ref, h)
            sc = jnp.where(valid, lax.dot_general(k[:, lanes], qs, nt, preferred_element_type=F32), NEG)
            m = jnp.maximum(jnp.max(sc, axis=0, keepdims=True), sink)
            p = jnp.exp(sc - m)
            es = jnp.exp(sink - m)
            inv = 1.0 / (jnp.sum(p, axis=0, keepdims=True) + es)
            prob = p * inv
            dprob = lax.dot_general(v[:, lanes], dos, nt, preferred_element_type=F32)
            delta = jnp.sum(prob * dprob, axis=0, keepdims=True)
            dsc = (prob * (dprob - delta)).astype(BF16)
            dsk = -(es * inv) * delta
            for gq in range(GROUP):
                tot = jnp.sum(dsk[:, ATT_BLOCK * gq:ATT_BLOCK * (gq + 1)], axis=1, keepdims=True)
                rows.append(jnp.broadcast_to(tot, (1, LANES)))
            dqs = lax.dot_general(dsc, k[:, lanes], (((0,), (0,)), ((), ())), preferred_element_type=F32)
            dq_acc = _unstack_q(dqs, h, dq_acc)
            dkh = jnp.dot(dsc, qs, preferred_element_type=F32)
            dvh = jnp.dot(prob.astype(BF16), dos, preferred_element_type=F32)
            dk_acc[grp] = dkh if dk_acc[grp] is None else dk_acc[grp] + dkh
            dv_acc[grp] = dvh if dv_acc[grp] is None else dv_acc[grp] + dvh
        dq = jnp.concatenate(dq_acc, axis=1) * (HEAD_DIM ** -0.5)
        dq_ref[...] = _unrope(dq, cq_ref[...], sq_ref[...]).astype(BF16)
        part = jnp.concatenate(dk_acc + dv_acc, axis=1)
        for wdw in range(3):
            part_ref[wdw] = part[ATT_BLOCK * wdw:ATT_BLOCK * (wdw + 1), :]
        dsink = jnp.concatenate(rows, axis=0)

        @pl.when(i == 0)
        def _():
            dsink_ref[...] = dsink

        @pl.when(i > 0)
        def _():
            dsink_ref[...] += dsink

    zspecs, tab = _att_specs(nb)
    return _call(
        body, name=name, grid=(nb,),
        out_shape=(jax.ShapeDtypeStruct((s, D_MODEL), BF16), jax.ShapeDtypeStruct((nb, 3, ATT_BLOCK, 2 * KV_DIM), F32),
                   jax.ShapeDtypeStruct((N_Q_HEADS, LANES), F32)),
        in_specs=zspecs + tab + tab + [pl.BlockSpec(memory_space=pltpu.SMEM), pl.BlockSpec((ATT_BLOCK, D_MODEL), lambda i: (i, 0))],
        out_specs=(pl.BlockSpec((ATT_BLOCK, D_MODEL), lambda i: (i, 0)),
                   pl.BlockSpec((None, 3, ATT_BLOCK, 2 * KV_DIM), lambda i: (i, 0, 0, 0)),
                   pl.BlockSpec((N_Q_HEADS, LANES), lambda i: (0, 0))),
        args=(z, z, z, z, cos, cos, cos, cos, sin, sin, sin, sin, sink, do), sem=("arbitrary",), comm=comm,
        edge=lambda: (pl.program_id(0) == 0, pl.program_id(0) == nb - 1))


def _att_bwd_kv(part, cos, sin, *, name):
    nb = part.shape[0]

    def body(pn_ref, pc_ref, pp_ref, c_ref, s_ref, o_ref):
        j = pl.program_id(0)
        tot = pc_ref[...]
        tot = tot + jnp.where(j < nb - 1, pn_ref[...], 0.0)
        tot = tot + jnp.where(j > 0, pp_ref[...], 0.0)
        dk = _unrope(tot[:, :KV_DIM], c_ref[...], s_ref[...])
        o_ref[...] = jnp.concatenate([dk, tot[:, KV_DIM:]], axis=1).astype(BF16)

    blk = (None, None, ATT_BLOCK, 2 * KV_DIM)
    return pl.pallas_call(
        body, name=name, grid=(nb,),
        out_shape=jax.ShapeDtypeStruct((nb * ATT_BLOCK, 2 * KV_DIM), BF16),
        in_specs=[pl.BlockSpec(blk, lambda j: (jnp.minimum(j + 1, nb - 1), 0, 0, 0)),
                  pl.BlockSpec(blk, lambda j: (j, 1, 0, 0)),
                  pl.BlockSpec(blk, lambda j: (jnp.maximum(j - 1, 0), 2, 0, 0)),
                  pl.BlockSpec((ATT_BLOCK, LANES), lambda j: (j, 0)), pl.BlockSpec((ATT_BLOCK, LANES), lambda j: (j, 0))],
        out_specs=pl.BlockSpec((ATT_BLOCK, 2 * KV_DIM), lambda j: (j, 0)),
        compiler_params=_params(("parallel",)),
    )(part, part, part, cos, sin)


def _bdot(a, b, dims):
    return lax.dot_general(a.astype(BF16), b.astype(BF16), (dims, ((), ())), preferred_element_type=F32)


@jax.custom_vjp
def _dot_nn(a, b):
    return _bdot(a, b, ((1,), (0,)))


@jax.custom_vjp
def _dot_nt(a, b):
    return _bdot(a, b, ((1,), (1,)))


@jax.custom_vjp
def _dot_tn(a, b):
    return _bdot(a, b, ((0,), (0,)))


_dot_nn.defvjp(lambda a, b: (_dot_nn(a, b), (a, b)), lambda r, d: (_dot_nt(d, r[1]), _dot_tn(r[0], d)))
_dot_nt.defvjp(lambda a, b: (_dot_nt(a, b), (a, b)), lambda r, d: (_dot_nn(d, r[1]), _dot_tn(d, r[0])))
_dot_tn.defvjp(lambda a, b: (_dot_tn(a, b), (a, b)), lambda r, d: (_dot_nt(r[1], d), _dot_nn(r[0], d)))


def _running_sum(v, up):
    n = v.shape[0]
    rows = lax.broadcasted_iota(jnp.int32, v.shape, 0)
    sh = 1
    while sh < n:
        if up:
            v = v + jnp.where(rows < n - sh, pltpu.roll(v, n - sh, 0), 0.0)
        else:
            v = v + jnp.where(rows >= sh, pltpu.roll(v, sh, 0), 0.0)
        sh *= 2
    return v


@jax.custom_vjp
def _sum_down(v):
    return _running_sum(v, False)


@jax.custom_vjp
def _sum_up(v):
    return _running_sum(v, True)


_sum_down.defvjp(lambda v: (_running_sum(v, False), None), lambda _, d: (_sum_up(d),))
_sum_up.defvjp(lambda v: (_running_sum(v, True), None), lambda _, d: (_sum_down(d),))

N_SUB = HG_CHUNK // HG_SUB


def _fold_blocks(v):
    out = v[:HG_CHUNK]
    for i in range(1, N_SUB):
        out = out + v[HG_CHUNK * i:HG_CHUNK * (i + 1)]
    return out


@jax.custom_vjp
def _fold(v):
    return _fold_blocks(v)


_fold.defvjp(lambda v: (_fold_blocks(v), None), lambda _, d: (jnp.concatenate([d] * N_SUB, axis=0),))


def _hg_consts(rev):
    c, sub = HG_CHUNK, HG_SUB
    rowpos = lax.broadcasted_iota(jnp.int32, (c, HG_DIM), 0)
    rr = lax.broadcasted_iota(jnp.int32, (N_SUB * c, c), 0)
    key = lax.broadcasted_iota(jnp.int32, (N_SUB * c, c), 1)
    blk, qry = rr // c, rr % c
    if rev:
        rowpos, qry, key = c - 1 - rowpos, c - 1 - qry, c - 1 - key
    keep = (key // sub == blk) & (key <= qry)
    return keep, rowpos


def _pick(b, rowpos, t):
    return jnp.sum(jnp.where(rowpos == t, b, 0.0), axis=0, keepdims=True)


def _hg_local(zq, zf, zv, lbv, consts, dots):
    dot_nn, dot_nt, dot_tn, cum, fold = dots
    keep, rowpos = consts
    sig = _sigmoid(zf)
    f = lbv + (1.0 - lbv) * sig
    g = jnp.log(f)
    k = (1.0 - lbv) * (1.0 - sig)
    q = zq * _sigmoid(zq)
    b = cum(g)
    ends = [_pick(b, rowpos, (j + 1) * HG_SUB - 1) for j in range(N_SUB)]
    b_last = ends[-1]
    b_end = b_last
    for j in range(N_SUB - 1):
        b_end = jnp.where(rowpos // HG_SUB == j, ends[j], b_end)
    kc = k * jnp.exp(b_end - b)
    qbs = [q * jnp.exp(jnp.where(rowpos >= j * HG_SUB, b - ends[j], 0.0)) for j in range(N_SUB)]
    scores = fold(jnp.where(keep, dot_nt(jnp.concatenate(qbs, axis=0), kc), 0.0))
    return dot_nn(scores, zv), q * jnp.exp(b), k * jnp.exp(b_last - b), jnp.exp(b_last)


def _hg_chunk(zq, zf, zv, lbv, st, consts, dots):
    intra, qs, kd, dec = _hg_local(zq, zf, zv, lbv, consts, dots)
    return intra + dots[1](qs, st), dec * st + dots[2](zv, kd)


def _hg_dots(diff, rev):
    if diff:
        return _dot_nn, _dot_nt, _dot_tn, (_sum_up if rev else _sum_down), _fold
    return (lambda a, b: _bdot(a, b, ((1,), (0,))), lambda a, b: _bdot(a, b, ((1,), (1,))),
            lambda a, b: _bdot(a, b, ((0,), (0,))), lambda v: _running_sum(v, rev), _fold_blocks)


def _hg_specs(ts, nch, trow):
    tile = pl.BlockSpec((ts, HG_DIM), lambda h, t: (trow(t), h))
    mats = pl.BlockSpec((None, nch, HG_DIM, HG_DIM), lambda h, t: (h, trow(t), 0, 0))
    vecs = pl.BlockSpec((None, nch, 1, HG_DIM), lambda h, t: (h, trow(t), 0, 0))
    return tile, mats, vecs


def _time_order(nch, rev):
    return range(nch - 1, -1, -1) if rev else range(nch)


def _chunk_rows(c):
    return pl.ds(c * HG_CHUNK, HG_CHUNK)


def _hg_edge(nt):
    h, t = pl.program_id(0), pl.program_id(1)
    return (h == 0) & (t == 0), (h == HG_HEADS - 1) & (t == nt - 1)


def _hg_fwd(z, lb, *, rev, ts, comm=None, name):
    s = z.shape[0]
    nt = s // ts
    nch = ts // HG_CHUNK
    fcol = HG_HEADS * (2 if rev else 1)

    def body(zq_ref, zf_ref, zv_ref, lb_ref, o_ref, st_ref, qs_ref, dec_ref, state_ref):
        @pl.when(pl.program_id(1) == 0)
        def _():
            state_ref[...] = jnp.zeros_like(state_ref)

        consts = _hg_consts(rev)
        dots = _hg_dots(False, rev)
        lbv = lb_ref[...]
        local = {}
        for c in range(nch):
            rows = _chunk_rows(c)
            zv = zv_ref[rows, :]
            intra, qs, kd, dec = _hg_local(zq_ref[rows, :], zf_ref[rows, :], zv, lbv, consts, dots)
            qs = qs.astype(BF16)
            qs_ref[rows, :] = qs
            dec_ref[c] = dec
            local[c] = (intra, qs, dec, dots[2](zv, kd))
        st = state_ref[...]
        for c in _time_order(nch, rev):
            intra, qs, dec, upd = local[c]
            st_ref[c] = st
            o_ref[_chunk_rows(c), :] = intra + _bdot(qs, st, ((1,), (1,)))
            st = dec * st + upd
        state_ref[...] = st

    trow = (lambda t: nt - 1 - t) if rev else (lambda t: t)
    col = lambda off: pl.BlockSpec((ts, HG_DIM), lambda h, t: (trow(t), off + h))
    tile, mats, vecs = _hg_specs(ts, nch, trow)
    nchunks = s // HG_CHUNK
    return _call(
        body, name=name, grid=(HG_HEADS, nt),
        out_shape=(jax.ShapeDtypeStruct((s, D_MODEL), F32),
                   jax.ShapeDtypeStruct((HG_HEADS, nchunks, HG_DIM, HG_DIM), F32),
                   jax.ShapeDtypeStruct((s, D_MODEL), BF16),
                   jax.ShapeDtypeStruct((HG_HEADS, nchunks, 1, HG_DIM), F32)),
        in_specs=[col(0), col(fcol), col(3 * HG_HEADS), pl.BlockSpec((None, 1, HG_DIM), lambda h, t: (h, 0, 0))],
        out_specs=(tile, mats, tile, vecs), args=(z, z, z, lb),
        scratch_shapes=[pltpu.VMEM((HG_DIM, HG_DIM), F32)], sem=("parallel", "arbitrary"), comm=comm,
        edge=lambda: _hg_edge(nt))


def _hg_bwd(z, lb, states, qs, dec, dout, addq, addv, *, rev, ts, comm=None, name):
    s = z.shape[0]
    nt = s // ts
    nch = ts // HG_CHUNK
    fcol = HG_HEADS * (2 if rev else 1)
    has_add = addq is not None

    def body(*refs):
        zq_ref, zf_ref, zv_ref, lb_ref, st_ref, qs_ref, dec_ref, do_ref = refs[:8]
        aq_ref, av_ref = (refs[8], refs[9]) if has_add else (None, None)
        dq_ref, df_ref, dv_ref, dlb_ref, grad_ref = refs[-5:]

        @pl.when(pl.program_id(1) == 0)
        def _():
            grad_ref[...] = jnp.zeros_like(grad_ref)

        consts = _hg_consts(rev)
        dots = _hg_dots(True, rev)
        lbv = lb_ref[...]
        prods = {c: _bdot(do_ref[_chunk_rows(c), :], qs_ref[_chunk_rows(c), :], ((0,), (0,))) for c in range(nch)}
        gleave = {}
        gr = grad_ref[...]
        for c in reversed(_time_order(nch, rev)):
            gleave[c] = gr
            gr = dec_ref[c] * gr + prods[c]
        grad_ref[...] = gr
        dlb_blk = jnp.zeros((1, HG_DIM), F32)
        for c in range(nch):
            rows = _chunk_rows(c)
            fn = lambda a, b2, c2, d2, e2: _hg_chunk(a, b2, c2, d2, e2, consts, dots)
            _, pull = jax.vjp(fn, zq_ref[rows, :], zf_ref[rows, :], zv_ref[rows, :], lbv, st_ref[c])
            dq, df, dv, dlb, _ = pull((do_ref[rows, :], gleave[c]))
            if has_add:
                dq = dq + aq_ref[rows, :]
                dv = dv + av_ref[rows, :]
            dq_ref[rows, :] = dq.astype(dq_ref.dtype)
            df_ref[rows, :] = df.astype(BF16)
            dv_ref[rows, :] = dv.astype(dv_ref.dtype)
            dlb_blk = dlb_blk + dlb

        @pl.when(pl.program_id(1) == 0)
        def _():
            dlb_ref[...] = dlb_blk

        @pl.when(pl.program_id(1) > 0)
        def _():
            dlb_ref[...] += dlb_blk

    trow = (lambda t: t) if rev else (lambda t: nt - 1 - t)
    col = lambda off: pl.BlockSpec((ts, HG_DIM), lambda h, t: (trow(t), off + h))
    tile, mats, vecs = _hg_specs(ts, nch, trow)
    in_specs = [col(0), col(fcol), col(3 * HG_HEADS), pl.BlockSpec((None, 1, HG_DIM), lambda h, t: (h, 0, 0)),
                mats, tile, vecs, tile]
    args = [z, z, z, lb, states, qs, dec, dout]
    if has_add:
        in_specs += [tile, tile]
        args += [addq, addv]
    act = lambda dt: jax.ShapeDtypeStruct((s, D_MODEL), dt)
    sums = BF16 if has_add else F32
    return _call(
        body, name=name, grid=(HG_HEADS, nt),
        out_shape=(act(sums), act(BF16), act(sums), jax.ShapeDtypeStruct((HG_HEADS, 1, HG_DIM), F32)),
        in_specs=in_specs,
        out_specs=(tile, tile, tile, pl.BlockSpec((None, 1, HG_DIM), lambda h, t: (h, 0, 0))), args=tuple(args),
        scratch_shapes=[pltpu.VMEM((HG_DIM, HG_DIM), F32)], sem=("parallel", "arbitrary"), comm=comm,
        edge=lambda: _hg_edge(nt))


def _hg_post(of, ob, z, norm_g, *, tm, name):
    s = of.shape[0]

    def body(of_ref, ob_ref, gate_ref, ng_ref, y_ref):
        gn = ng_ref[...]
        for h in range(HG_HEADS):
            ln = slice(HG_DIM * h, HG_DIM * (h + 1))
            o = of_ref[:, ln] + ob_ref[:, ln]
            r = lax.rsqrt(jnp.mean(o * o, axis=-1, keepdims=True) + LN_EPS)
            gt = gate_ref[:, ln]
            y_ref[:, ln] = (o * r * gn * gt * _sigmoid(gt)).astype(BF16)

    row = lambda i: (i, 0)
    return pl.pallas_call(
        body, name=name, grid=(s // tm,),
        out_shape=jax.ShapeDtypeStruct((s, D_MODEL), BF16),
        in_specs=[pl.BlockSpec((tm, D_MODEL), row), pl.BlockSpec((tm, D_MODEL), row),
                  pl.BlockSpec((tm, D_MODEL), lambda i: (i, 4)), pl.BlockSpec((1, HG_DIM), lambda i: (0, 0))],
        out_specs=pl.BlockSpec((tm, D_MODEL), row),
        compiler_params=_params(("parallel",)),
    )(of, ob, z, norm_g)


def _hg_post_bwd(dy, of, ob, z, norm_g, *, tm, name):
    s = of.shape[0]

    def body(dy_ref, of_ref, ob_ref, gate_ref, ng_ref, do_ref, dgate_ref, dng_ref):
        gn = ng_ref[...]
        tot = jnp.zeros((1, HG_DIM), F32)
        for h in range(HG_HEADS):
            ln = slice(HG_DIM * h, HG_DIM * (h + 1))
            d = dy_ref[:, ln].astype(F32)
            o = of_ref[:, ln] + ob_ref[:, ln]
            r = lax.rsqrt(jnp.mean(o * o, axis=-1, keepdims=True) + LN_EPS)
            ohat = o * r
            gt = gate_ref[:, ln]
            sg = _sigmoid(gt)
            don = d * gt * sg
            dgate_ref[:, ln] = (d * ohat * gn * sg * (1.0 + gt * (1.0 - sg))).astype(BF16)
            tot = tot + jnp.sum(don * ohat, axis=0, keepdims=True)
            dohat = don * gn
            do_ref[:, ln] = r * (dohat - ohat * jnp.mean(dohat * ohat, axis=-1, keepdims=True))

        @pl.when(pl.program_id(0) == 0)
        def _():
            dng_ref[...] = tot

        @pl.when(pl.program_id(0) > 0)
        def _():
            dng_ref[...] += tot

    row = lambda i: (i, 0)
    return pl.pallas_call(
        body, name=name, grid=(s // tm,),
        out_shape=(jax.ShapeDtypeStruct((s, D_MODEL), F32), jax.ShapeDtypeStruct((s, D_MODEL), BF16),
                   jax.ShapeDtypeStruct((1, HG_DIM), F32)),
        in_specs=[pl.BlockSpec((tm, D_MODEL), row), pl.BlockSpec((tm, D_MODEL), row), pl.BlockSpec((tm, D_MODEL), row),
                  pl.BlockSpec((tm, D_MODEL), lambda i: (i, 4)), pl.BlockSpec((1, HG_DIM), lambda i: (0, 0))],
        out_specs=(pl.BlockSpec((tm, D_MODEL), row), pl.BlockSpec((tm, D_MODEL), row),
                   pl.BlockSpec((1, HG_DIM), lambda i: (0, 0))),
        compiler_params=_params(("arbitrary",)),
    )(dy, of, ob, z, norm_g)


def _lb_fwd(logits, *, name):
    w = logits.shape[1]

    def body(l_ref, o_ref):
        lg = l_ref[...]
        e = jnp.exp(lg - jnp.max(lg, axis=0, keepdims=True))
        sm = e / jnp.sum(e, axis=0, keepdims=True)
        o_ref[0:1, :] = sm[1:2]
        o_ref[1:2, :] = sm[1:2] + sm[2:3] + sm[3:4]

    return pl.pallas_call(body, name=name, out_shape=jax.ShapeDtypeStruct((2, w), F32))(logits)


def _lb_bwd(logits, dlb, *, name):
    w = logits.shape[1]

    def body(l_ref, d_ref, o_ref):
        lg = l_ref[...]
        e = jnp.exp(lg - jnp.max(lg, axis=0, keepdims=True))
        sm = e / jnp.sum(e, axis=0, keepdims=True)
        d1, d3 = d_ref[0:1, :], d_ref[1:2, :]
        dot = sm[1:2] * (d1 + d3) + (sm[2:3] + sm[3:4]) * d3
        o_ref[0:1, :] = -sm[0:1] * dot
        o_ref[1:2, :] = sm[1:2] * (d1 + d3 - dot)
        o_ref[2:3, :] = sm[2:3] * (d3 - dot)
        o_ref[3:4, :] = sm[3:4] * (d3 - dot)

    return pl.pallas_call(body, name=name, out_shape=jax.ShapeDtypeStruct((4, w), F32))(logits, dlb)


def _adamw(w, g, m, v, *, tr, g_off=0, name):
    rows = w.shape[0]
    parts = g.ndim == 3
    c1 = 1.0 / (1.0 - ADAM_B1 ** ADAM_STEP)
    c2 = 1.0 / (1.0 - ADAM_B2 ** ADAM_STEP)

    def body(w_ref, g_ref, m_ref, v_ref, go_ref, d_ref, mo_ref, vo_ref):
        if parts:
            gg = g_ref[0].astype(F32)
            for i in range(1, N_DEV):
                gg = gg + g_ref[i].astype(F32)
        else:
            gg = g_ref[...]
        mm = ADAM_B1 * m_ref[...] + (1.0 - ADAM_B1) * gg
        vv = ADAM_B2 * v_ref[...] + (1.0 - ADAM_B2) * (gg * gg)
        go_ref[...] = gg
        mo_ref[...] = mm
        vo_ref[...] = vv
        d_ref[...] = -ADAM_LR * ((mm * c1) / (jnp.sqrt(vv * c2) + ADAM_EPS) + ADAM_WD * w_ref[...])

    tile = pl.BlockSpec((tr, D_MODEL), lambda i: (i, 0))
    gspec = pl.BlockSpec((N_DEV, tr, D_MODEL), lambda i: (0, i + g_off // tr, 0)) if parts else tile
    out = jax.ShapeDtypeStruct((rows, D_MODEL), F32)
    return pl.pallas_call(
        body, name=name, grid=(rows // tr,),
        out_shape=(out, out, out, out),
        in_specs=[tile, gspec, tile, tile], out_specs=(tile, tile, tile, tile),
        compiler_params=_params(("parallel",)),
    )(w, g, m, v)


def _sum8(parts, *, name):
    def body(p_ref, o_ref):
        tot = p_ref[0]
        for i in range(1, N_DEV):
            tot = tot + p_ref[i]
        o_ref[...] = tot

    return pl.pallas_call(body, name=name, out_shape=jax.ShapeDtypeStruct(parts.shape[1:], parts.dtype))(parts)


def _layer_params(i):
    j = i // 2
    mix = [("att_w_qkv", j, 1), ("att_w_o", j, 0)] if i % 2 == 0 else [("hgrn_w_in", j, 1), ("hgrn_w_o", j, 0)]
    return mix + [("ffn_w_in", i, 1), ("ffn_w_out", i, 0), ("ple_w_gate", i, 0), ("ple_w_proj", i, 1)]


def _pack_local(tree, params):
    return jnp.concatenate([tree[n][j].reshape(-1, D_MODEL) for n, j, _ in params], axis=0)


def _unpack_local(packed, params, like):
    out, r = {}, 0
    for n, _, _ in params:
        shp = like[n].shape[1:]
        k = shp[0] * shp[1] // D_MODEL
        out[n] = packed[r:r + k].reshape(shp)
        r += k
    return out


def _unpack_gathered(gathered, i, like):
    out, r = {}, 0
    for n, _, ax in _layer_params(i):
        shp = like[n].shape[1:]
        k = shp[0] * shp[1] // D_MODEL
        t = gathered[:, r:r + k].reshape((N_DEV,) + shp)
        out[n] = (jnp.moveaxis(t, 0, 1).reshape(shp[0], N_DEV * shp[1]) if ax == 1
                  else t.reshape(N_DEV * shp[0], shp[1]))
        r += k
    return out


def _pack_full(grads, params, like):
    cols = []
    for n, _, ax in params:
        shp = like[n].shape[1:]
        t = (jnp.moveaxis(grads[n].reshape(shp[0], N_DEV, shp[1]), 1, 0) if ax == 1
             else grads[n].reshape(N_DEV, shp[0], shp[1]))
        cols.append(t.reshape(N_DEV, -1, D_MODEL).astype(BF16))
    return jnp.concatenate(cols, axis=1)


def _row_tile(rows):
    return max(t for t in range(16, 257, 16) if rows % t == 0)


SMALL_ROWS = 24


def _pad_row(a):
    flat = a.reshape(1, -1)
    return jnp.pad(flat, ((0, 0), (0, D_MODEL - flat.shape[1])))


def _tile(n, pref):
    return min(n, pref)


def kernel(x, p, att_w_qkv, att_sink, att_w_o, hgrn_w_in, hgrn_lb_logits, hgrn_norm_g, hgrn_w_o, ln_mix_g, ln_mix_b, ffn_w_in, ffn_w_out, ln_ffn_g, ln_ffn_b, ple_w_gate, ple_w_proj, loss_target, m_att_w_qkv, m_att_sink, m_att_w_o, m_hgrn_w_in, m_hgrn_lb_logits, m_hgrn_norm_g, m_hgrn_w_o, m_ln_mix_g, m_ln_mix_b, m_ffn_w_in, m_ffn_w_out, m_ln_ffn_g, m_ln_ffn_b, m_ple_w_gate, m_ple_w_proj, v_att_w_qkv, v_att_sink, v_att_w_o, v_hgrn_w_in, v_hgrn_lb_logits, v_hgrn_norm_g, v_hgrn_w_o, v_ln_mix_g, v_ln_mix_b, v_ffn_w_in, v_ffn_w_out, v_ln_ffn_g, v_ln_ffn_b, v_ple_w_gate, v_ple_w_proj):
    names = ["att_w_qkv", "att_sink", "att_w_o", "hgrn_w_in", "hgrn_lb_logits", "hgrn_norm_g", "hgrn_w_o", "ln_mix_g",
             "ln_mix_b", "ffn_w_in", "ffn_w_out", "ln_ffn_g", "ln_ffn_b", "ple_w_gate", "ple_w_proj"]
    w = dict(zip(names, (att_w_qkv, att_sink, att_w_o, hgrn_w_in, hgrn_lb_logits, hgrn_norm_g, hgrn_w_o, ln_mix_g,
                         ln_mix_b, ffn_w_in, ffn_w_out, ln_ffn_g, ln_ffn_b, ple_w_gate, ple_w_proj)))
    mom = dict(zip(names, (m_att_w_qkv, m_att_sink, m_att_w_o, m_hgrn_w_in, m_hgrn_lb_logits, m_hgrn_norm_g, m_hgrn_w_o,
                           m_ln_mix_g, m_ln_mix_b, m_ffn_w_in, m_ffn_w_out, m_ln_ffn_g, m_ln_ffn_b, m_ple_w_gate,
                           m_ple_w_proj)))
    var = dict(zip(names, (v_att_w_qkv, v_att_sink, v_att_w_o, v_hgrn_w_in, v_hgrn_lb_logits, v_hgrn_norm_g, v_hgrn_w_o,
                           v_ln_mix_g, v_ln_mix_b, v_ffn_w_in, v_ffn_w_out, v_ln_ffn_g, v_ln_ffn_b, v_ple_w_gate,
                           v_ple_w_proj)))
    s = x.shape[1]
    me = 4 * lax.axis_index("x") + 2 * lax.axis_index("y") + lax.axis_index("c")
    tm = _tile(s, 512)
    tbig = _tile(s, 1024)
    ts = _tile(s // 2, 512)
    x0 = x.reshape(s, D_MODEL)
    target = loss_target.reshape(s, D_MODEL)
    pl_in = p.reshape(DEPTH, s, PLE_DIM)

    w_rows = [_pack_local(w, _layer_params(i)).astype(BF16) for i in range(DEPTH)]
    full = _unpack_gathered(_gather(w_rows[0], name="gather_weights"), 0, w)
    lb_rows = jnp.pad(hgrn_lb_logits.reshape(8, HG_DIM), ((0, 0), (0, D_MODEL - HG_DIM)))
    lb_all = _gather(lb_rows, name="gather_lb")[:, :, :HG_DIM]
    logits_full = jnp.moveaxis(lb_all, 0, 1).reshape(DEPTH, 2 * D_MODEL)
    lb = _lb_fwd(logits_full, name="lb_fwd")
    cos, sin = _rope_tables(s)

    saved = []
    xf, xb = x0, x0
    for i in range(DEPTH):
        j = i // 2
        sv = {"x": xf, "xb": xb, "w": full}
        nxt = (w_rows[i + 1], True) if i + 1 < DEPTH else None
        if i % 2 == 0:
            z = _mm(xb, full["att_w_qkv"], n=D_MODEL + 2 * KV_DIM, tm=tbig, tn=512, tk=D_MODEL, name="att_in")
            o, *more = _att_fwd(z, w["att_sink"][j], cos, sin, comm=nxt, name="att_fwd")
            w_o = full["att_w_o"]
        else:
            z = _mm(xb, full["hgrn_w_in"], n=5 * D_MODEL, tm=tbig, tn=1024, tk=D_MODEL, name="hgrn_in")
            lbl = lb[j].reshape(2, HG_HEADS, 1, HG_DIM)
            of, st_f, qs_f, dec_f, *more = _hg_fwd(z, lbl[0], rev=False, ts=ts, comm=nxt, name="hgrn_fwd")
            ob, st_b, qs_b, dec_b = _hg_fwd(z, lbl[1], rev=True, ts=ts, name="hgrn_fwd_rev")
            o = _hg_post(of, ob, z, w["hgrn_norm_g"][j].reshape(1, HG_DIM), tm=tm, name="hgrn_post")
            w_o = full["hgrn_w_o"]
            sv.update(of=of, ob=ob, st_f=st_f, st_b=st_b, lbl=lbl, qs_f=qs_f, qs_b=qs_b, dec_f=dec_f, dec_b=dec_b)
        sv.update(z=z, o=o)
        pre1, x1, x1b = _proj_ln(o, w_o, xf, w["ln_mix_g"][i:i + 1], w["ln_mix_b"][i:i + 1], tm=tm, name="mix_out_ln")
        gg, uu, act = _ffn_in(x1b, full["ffn_w_in"], tm=tm, tn=FF_TILE, name="ffn_in")
        pre2, x2b, xf, xb = _ffn_out_ple(act, full["ffn_w_out"], x1, w["ln_ffn_g"][i:i + 1], w["ln_ffn_b"][i:i + 1],
                                         pl_in[i], full["ple_w_gate"], full["ple_w_proj"], tm=tm, name="ffn_out_ple")
        sv.update(pre1=pre1, x1=x1, x1b=x1b, g=gg, u=uu, act=act, pre2=pre2, x2b=x2b)
        saved.append(sv)
        if nxt is not None:
            full = _unpack_gathered(more[0], i + 1, w)

    dx, loss_blk = _loss_head(xf, target, tm=tm, name="loss_head")
    loss = lax.psum(loss_blk[0, 0], AXES)

    small = {n: [None] * DEPTH for n in ("ln_mix_g", "ln_mix_b", "ln_ffn_g", "ln_ffn_b")}
    dlb_rows = [None] * 4
    dnorm, dsink = [None] * 2, [None] * 2
    recv_late, recv_early = [None] * DEPTH, [None] * DEPTH
    above = None
    mmw = functools.partial(_mm, ta=True, tk=_tile(s, 2048), out_dtype=BF16)
    for i in reversed(range(DEPTH)):
        j = i // 2
        sv = saved[i]
        full, gl = sv["w"], {}
        da, dpp, dy2, dy2b, small["ln_ffn_g"][i], small["ln_ffn_b"][i] = _ple_ln_bwd(
            dx, sv["x2b"], pl_in[i], full["ple_w_gate"], full["ple_w_proj"], sv["pre2"], w["ln_ffn_g"][i:i + 1],
            tm=tm, name="ple_ln_bwd")
        gl["ple_w_gate"] = mmw(sv["x2b"], da, n=D_MODEL, tm=D_MODEL, tn=D_MODEL, name="dw_ple_gate")
        gl["ple_w_proj"] = mmw(pl_in[i], dpp, n=D_MODEL, tm=PLE_DIM, tn=D_MODEL, name="dw_ple_proj")
        dg, du = _ffn_bwd_act(dy2b, full["ffn_w_out"], sv["g"], sv["u"], tm=tm, tn=FF_TILE, name="ffn_bwd_act")
        gl["ffn_w_out"] = mmw(sv["act"], dy2b, n=D_MODEL, tm=FF_TILE, tn=D_MODEL, name="dw_ffn_out")
        t = _mm(dg, full["ffn_w_in"], n=D_MODEL, tm=tbig, tn=D_MODEL, tk=FF_TILE, tb=True, add=dy2, add_scale=ALPHA,
                name="ffn_bwd_x")
        dy1, dy1b, small["ln_mix_g"][i], small["ln_mix_b"][i] = _mm(
            du, full["ffn_w_in"], n=D_MODEL, tm=tm, tn=D_MODEL, tk=FF_TILE, tb=True, add=t, bk_off=D_FF // FF_TILE,
            ln=(sv["pre1"], w["ln_mix_g"][i:i + 1]), name="ffn_bwd_x_u_ln")
        gl["ffn_w_in"] = jnp.concatenate(
            [mmw(sv["x1b"], dg, n=D_FF, tm=D_MODEL, tn=FF_TILE, name="dw_ffn_in"),
             mmw(sv["x1b"], du, n=D_FF, tm=D_MODEL, tn=FF_TILE, name="dw_ffn_in")], axis=1)
        n_out, n_inw, n_in = (("att_w_o", "att_w_qkv", D_MODEL + 2 * KV_DIM) if i % 2 == 0
                              else ("hgrn_w_o", "hgrn_w_in", 5 * D_MODEL))
        do = _mm(dy1b, full[n_out], n=D_MODEL, tm=tbig, tn=D_MODEL, tk=D_MODEL, tb=True, out_dtype=BF16,
                 name="mix_out_bwd")
        gl[n_out] = mmw(sv["o"], dy1b, n=D_MODEL, tm=D_MODEL, tn=D_MODEL, name="dw_mix_out")
        early = _pack_full(gl, _layer_params(i)[1:], w)
        comm = (early if above is None else jnp.concatenate([above, early], axis=1), False)
        if i % 2 == 0:
            dzq, part, dsk, *more = _att_bwd(sv["z"], do, w["att_sink"][j], cos, sin, comm=comm, name="att_bwd")
            dzkv = _att_bwd_kv(part, cos, sin, name="att_bwd_kv")
            dz = jnp.concatenate([dzq, dzkv], axis=1)
            dsink[j] = dsk[:, 0]
        else:
            dsum, dgate, dnorm[j] = _hg_post_bwd(do, sv["of"], sv["ob"], sv["z"], w["hgrn_norm_g"][j].reshape(1, HG_DIM),
                                                 tm=tm, name="hgrn_post_bwd")
            dq1, df1, dv1, dlb1, *more = _hg_bwd(sv["z"], sv["lbl"][0], sv["st_f"], sv["qs_f"], sv["dec_f"], dsum, None,
                                                 None, rev=False, ts=ts, comm=comm, name="hgrn_bwd")
            dq2, df2, dv2, dlb2 = _hg_bwd(sv["z"], sv["lbl"][1], sv["st_b"], sv["qs_b"], sv["dec_b"], dsum, dq1, dv1,
                                          rev=True, ts=ts, name="hgrn_bwd_rev")
            dz = jnp.concatenate([dq2, df1, df2, dv2, dgate], axis=1)
            dlb_rows[2 * j] = dlb1.reshape(1, D_MODEL)
            dlb_rows[2 * j + 1] = dlb2.reshape(1, D_MODEL)
        if above is not None:
            recv_late[i + 1] = (more[0], 0)
        recv_early[i] = (more[0], 0 if above is None else above.shape[1])
        dx = _mm(dz, full[n_inw], n=D_MODEL, tm=tbig, tn=D_MODEL, tk=min(n_in, 2560), tb=True, add=dy1, add_scale=ALPHA,
                 name="mix_in_bwd")
        gl[n_inw] = mmw(sv["xb"], dz, n=n_in, tm=D_MODEL, tn=512, name="dw_mix_in")
        above = _pack_full(gl, _layer_params(i)[:1], w)
    grad_x = dx.reshape(x.shape)
    recv_late[0] = (_exchange(above, name="exchange_grads"), 0)

    big_out = [{n: [None] * w[n].shape[0] for n, _ in BIG} for _ in range(4)]
    for i in range(DEPTH):
        for params, (got, off) in ((_layer_params(i)[:1], recv_late[i]), (_layer_params(i)[1:], recv_early[i])):
            w_part = _pack_local(w, params)
            outs = _adamw(w_part, got, _pack_local(mom, params), _pack_local(var, params),
                          tr=_row_tile(math.gcd(w_part.shape[0], off)), g_off=off, name="adamw_big")
            for kind, packed in enumerate(outs):
                for (n, j, _), piece in zip(params, _unpack_local(packed, params, w).values()):
                    big_out[kind][n][j] = piece
    big_out = [{n: jnp.stack(v) for n, v in kind.items()} for kind in big_out]

    small_rows = jnp.concatenate(
        [jnp.concatenate(small[n], axis=0) for n in ("ln_mix_g", "ln_mix_b", "ln_ffn_g", "ln_ffn_b")] + dlb_rows
        + [_pad_row(jnp.stack(dnorm)), _pad_row(jnp.stack(dsink)), jnp.zeros((2, D_MODEL), F32)], axis=0)
    small_all = _gather(small_rows, name="gather_small")
    lbw, lbm, lbv = (t.reshape(4, 2 * HG_DIM) for t in (hgrn_lb_logits, mom["hgrn_lb_logits"], var["hgrn_lb_logits"]))
    summed = _sum8(small_all, name="sum_small")
    dlb_mine = lax.dynamic_slice_in_dim(summed[16:20].reshape(2, 2, HG_HEADS, HG_DIM), me, 1, axis=2)
    dlogits = _lb_bwd(lbw, dlb_mine.reshape(2, 2 * HG_DIM), name="lb_bwd")

    def small_pack(ln4, lbt, ng, sk):
        return jnp.concatenate([ln4[n] for n in ("ln_mix_g", "ln_mix_b", "ln_ffn_g", "ln_ffn_b")]
                               + [_pad_row(lbt), _pad_row(ng), _pad_row(sk), jnp.zeros((5, D_MODEL), F32)], axis=0)

    g_small = jnp.concatenate([summed[:16], _pad_row(dlogits), summed[20:22], jnp.zeros((5, D_MODEL), F32)], axis=0)
    souts = _adamw(small_pack(w, lbw, w["hgrn_norm_g"], w["att_sink"]), g_small,
                   small_pack(mom, lbm, mom["hgrn_norm_g"], mom["att_sink"]),
                   small_pack(var, lbv, var["hgrn_norm_g"], var["att_sink"]), tr=SMALL_ROWS, name="adamw_small")

    def small_unpack(t):
        out = {n: t[4 * k:4 * k + 4] for k, n in enumerate(("ln_mix_g", "ln_mix_b", "ln_ffn_g", "ln_ffn_b"))}
        out["hgrn_lb_logits"] = t[16].reshape(hgrn_lb_logits.shape)
        out["hgrn_norm_g"] = t[17, :2 * HG_DIM].reshape(hgrn_norm_g.shape)
        out["att_sink"] = t[18, :2 * N_Q_HEADS].reshape(att_sink.shape)
        return out

    result = [loss, grad_x]
    for big_t, small_t in zip(big_out, souts):
        merged = dict(big_t)
        merged.update(small_unpack(small_t))
        result += [merged[n] for n in names]
    return tuple(result)
```

```python
import functools
import math

import jax
import jax.numpy as jnp
from jax import lax
from jax.experimental import pallas as pl
from jax.experimental.pallas import tpu as pltpu

F32 = jnp.float32
BF16 = jnp.bfloat16

D_MODEL = 1024
DEPTH = 4
HEAD_DIM = 64
N_Q_HEADS = 16
N_KV_HEADS = 4
GROUP = 4
KV_DIM = 256
ATT_BLOCK = 128
ROPE_DIM = 16
ROPE_THETA = 500000.0
HG_HEADS = 8
HG_DIM = 128
HG_CHUNK = 64
HG_SUB = 16
D_FF = 2816
FF_TILE = 1408
SUB_ROWS = 256
PLE_DIM = 256
ALPHA = (2 * DEPTH) ** 0.25
LN_EPS = 1e-5
ADAM_LR, ADAM_B1, ADAM_B2, ADAM_EPS, ADAM_WD, ADAM_STEP = 0.001, 0.9, 0.999, 1e-08, 0.01, 10

N_DEV = 8
LANES = 128
VMEM_LIMIT = 52 * 1024 * 1024
NEG = -1e30
MESH = pl.DeviceIdType.MESH
AXES = ("x", "y", "c")

BIG = (("att_w_qkv", 2), ("att_w_o", 1), ("hgrn_w_in", 2), ("hgrn_w_o", 1), ("ffn_w_in", 2), ("ffn_w_out", 1),
       ("ple_w_gate", 1), ("ple_w_proj", 2))


def _params(sem=None, vmem=VMEM_LIMIT):
    return pltpu.CompilerParams(dimension_semantics=sem, vmem_limit_bytes=vmem)


def _sigmoid(x):
    return jax.nn.sigmoid(x)


def _direct_copies(src_ref, out_ref, send_sems, recv_sems, local_sem, gather, arrivals):
    x, y, c = lax.axis_index("x"), lax.axis_index("y"), lax.axis_index("c")
    me = 4 * x + 2 * y + c
    mine = (lambda j: src_ref) if gather else (lambda j: src_ref.at[j])
    pairs = []
    for k in range(1, N_DEV):
        px, py, pc = x ^ (k >> 2), y ^ ((k >> 1) & 1), c ^ (k & 1)
        peer = 4 * px + 2 * py + pc
        send = pltpu.make_async_remote_copy(
            src_ref=mine(peer), dst_ref=out_ref.at[me], send_sem=send_sems.at[k], recv_sem=recv_sems.at[k],
            device_id=(px, py, pc), device_id_type=MESH)
        arrival = pltpu.make_async_remote_copy(
            src_ref=mine(peer), dst_ref=out_ref.at[peer], send_sem=send_sems.at[k], recv_sem=recv_sems.at[k],
            device_id=(x, y, c), device_id_type=MESH) if arrivals else None
        pairs.append((send, arrival))
    return pltpu.make_async_copy(mine(me), out_ref.at[me], local_sem), pairs


def _direct_start(*refs, gather):
    local, pairs = _direct_copies(*refs, gather, False)
    local.start()
    for send, _ in pairs:
        send.start()


def _direct_wait(*refs, gather):
    local, pairs = _direct_copies(*refs, gather, True)
    for send, arrival in pairs:
        send.wait_send()
        arrival.wait_recv()
    local.wait()


COMM_SCRATCH = [pltpu.SemaphoreType.DMA((N_DEV,)), pltpu.SemaphoreType.DMA((N_DEV,)), pltpu.SemaphoreType.DMA]


def _exchange(src, *, gather=False, name):
    def body(*refs):
        _direct_start(*refs, gather=gather)
        _direct_wait(*refs, gather=gather)

    blk = tuple(src.shape) if gather else tuple(src.shape[1:])
    return pl.pallas_call(
        body, name=name,
        out_shape=jax.ShapeDtypeStruct((N_DEV,) + blk, src.dtype),
        in_specs=[pl.BlockSpec(memory_space=pltpu.HBM)],
        out_specs=pl.BlockSpec(memory_space=pltpu.HBM),
        scratch_shapes=COMM_SCRATCH,
    )(src)


def _call(body, *, name, grid, out_shape, in_specs, out_specs, args, scratch_shapes=(), sem, comm=None, edge=None):
    out_shape, out_specs = tuple(out_shape), tuple(out_specs)
    if comm is None:
        return pl.pallas_call(body, name=name, grid=grid, out_shape=out_shape, in_specs=list(in_specs),
                              out_specs=out_specs, scratch_shapes=list(scratch_shapes),
                              compiler_params=_params(sem))(*args)
    src, gather = comm
    n_in, n_out, n_scr = len(args), len(out_shape), len(scratch_shapes)
    blk = tuple(src.shape) if gather else tuple(src.shape[1:])
    hbm = pl.BlockSpec(memory_space=pltpu.HBM)

    def carrying(*refs):
        ins, src_ref = refs[:n_in], refs[n_in]
        outs, dst_ref = refs[n_in + 1:n_in + 1 + n_out], refs[n_in + 1 + n_out]
        own = refs[n_in + 2 + n_out:n_in + 2 + n_out + n_scr]
        comm_refs = (src_ref, dst_ref) + tuple(refs[n_in + 2 + n_out + n_scr:])
        first, last = edge()

        @pl.when(first)
        def _():
            _direct_start(*comm_refs, gather=gather)

        body(*ins, *outs, *own)

        @pl.when(last)
        def _():
            _direct_wait(*comm_refs, gather=gather)

    return pl.pallas_call(
        carrying, name=name, grid=grid,
        out_shape=out_shape + (jax.ShapeDtypeStruct((N_DEV,) + blk, src.dtype),),
        in_specs=list(in_specs) + [hbm], out_specs=out_specs + (hbm,),
        scratch_shapes=list(scratch_shapes) + COMM_SCRATCH,
        compiler_params=_params(("arbitrary",) * len(grid)),
    )(*args, src)


def _gather(src, *, name):
    def body(src_ref, out_ref, send_sems, recv_sems, local_sem):
        x, y, c = lax.axis_index("x"), lax.axis_index("y"), lax.axis_index("c")
        sibling = (x, y, 1 - c)
        chips = [(1 - x, y), (x, 1 - y), (1 - x, 1 - y)]

        def rows(px, py, pc):
            return out_ref.at[4 * px + 2 * py + pc]

        def copy(k, block, to, from_src=False):
            return pltpu.make_async_remote_copy(
                src_ref=src_ref if from_src else rows(*block), dst_ref=rows(*block), send_sem=send_sems.at[k],
                recv_sem=recv_sems.at[k], device_id=to, device_id_type=MESH)

        me = (x, y, c)
        mine = pltpu.make_async_copy(src_ref, rows(*me), local_sem)
        mine.start()
        first = [copy(0, me, sibling, from_src=True)]
        first += [copy(1 + j, me, (*chip, c), from_src=True) for j, chip in enumerate(chips)]
        for cp in first:
            cp.start()
        passed = [copy(4 + j, (*chip, c), sibling) for j, chip in enumerate(chips)]
        for j, chip in enumerate(chips):
            copy(1 + j, (*chip, c), me).wait_recv()
            passed[j].start()
        copy(0, sibling, me).wait_recv()
        for j, chip in enumerate(chips):
            copy(4 + j, (*chip, 1 - c), me).wait_recv()
        for cp in first + passed:
            cp.wait_send()
        mine.wait()

    return pl.pallas_call(
        body, name=name,
        out_shape=jax.ShapeDtypeStruct((N_DEV,) + tuple(src.shape), src.dtype),
        in_specs=[pl.BlockSpec(memory_space=pltpu.HBM)],
        out_specs=pl.BlockSpec(memory_space=pltpu.HBM),
        scratch_shapes=[pltpu.SemaphoreType.DMA((7,)), pltpu.SemaphoreType.DMA((7,)), pltpu.SemaphoreType.DMA],
    )(src)


def _mm(a, b, *, n, tm, tn, tk, ta=False, tb=False, out_dtype=F32, add=None, add_scale=1.0, a2=None, ln=None, name):
    m, kdim = (a.shape[1], a.shape[0]) if ta else a.shape
    nk1 = kdim // tk
    nk = nk1 + (a2.shape[1] // tk if a2 is not None else 0)
    dims = (((0 if ta else 1,), (1 if tb else 0,)), ((), ()))

    def body(*refs):
        a_ref, b_ref = refs[:2]
        nxt = 2
        a2_ref = None
        if a2 is not None:
            a2_ref = refs[nxt]
            nxt += 1
        add_ref = None
        if add is not None:
            add_ref = refs[nxt]
            nxt += 1
        if ln is not None:
            pre_ref, g_ref = refs[nxt:nxt + 2]
            nxt += 2
        outs = refs[nxt:nxt + (4 if ln is not None else 1)]
        acc_ref = refs[-1] if nk > 1 else None

        def product(lhs_ref):
            return lax.dot_general(lhs_ref[...].astype(BF16), b_ref[...].astype(BF16), dims, preferred_element_type=F32)

        def finish(r):
            if add_ref is not None:
                r = r + add_scale * add_ref[...]
            if ln is None:
                outs[0][...] = r.astype(out_dtype)
            else:
                _ln_bwd_store(r, pre_ref[...], g_ref[...], *outs, first=pl.program_id(0) == 0)

        if nk == 1:
            finish(product(a_ref))
        else:
            k = pl.program_id(2)

            @pl.when(k == 0)
            def _():
                acc_ref[...] = product(a_ref)

            @pl.when((k > 0) & (k < nk1))
            def _():
                acc_ref[...] += product(a_ref)

            if a2_ref is not None:
                @pl.when(k >= nk1)
                def _():
                    acc_ref[...] += product(a2_ref)

            @pl.when(k == nk - 1)
            def _():
                finish(acc_ref[...])

    a_spec = (pl.BlockSpec((tk, tm), lambda i, j, k: (k, i)) if ta
              else pl.BlockSpec((tm, tk), lambda i, j, k: (i, jnp.minimum(k, nk1 - 1))))
    b_spec = pl.BlockSpec((tn, tk), lambda i, j, k: (j, k)) if tb else pl.BlockSpec((tk, tn), lambda i, j, k: (k, j))
    in_specs, args = [a_spec, b_spec], [a, b]
    if a2 is not None:
        assert not ta
        in_specs.append(pl.BlockSpec((tm, tk), lambda i, j, k: (i, jnp.maximum(k - nk1, 0))))
        args.append(a2)
    tile = pl.BlockSpec((tm, tn), lambda i, j, k: (i, j))
    if add is not None:
        in_specs.append(tile)
        args.append(add)
    out_shape, out_specs = jax.ShapeDtypeStruct((m, n), out_dtype), tile
    if ln is not None:
        assert tn == n == D_MODEL
        vec = pl.BlockSpec((1, D_MODEL), lambda i, j, k: (0, 0))
        in_specs += [tile, vec]
        args += list(ln)
        out_shape = (jax.ShapeDtypeStruct((m, n), F32), jax.ShapeDtypeStruct((m, n), BF16),
                     jax.ShapeDtypeStruct((1, D_MODEL), F32), jax.ShapeDtypeStruct((1, D_MODEL), F32))
        out_specs = (tile, tile, vec, vec)
    return pl.pallas_call(
        body, name=name, grid=(m // tm, n // tn, nk),
        out_shape=out_shape, in_specs=in_specs, out_specs=out_specs,
        scratch_shapes=[pltpu.VMEM((tm, tn), F32)] if nk > 1 else [],
        compiler_params=_params(("arbitrary",) * 3 if ln is not None else ("parallel", "parallel", "arbitrary")),
    )(*args)


def _ln_bwd_rows(do, y, g):
    mu = jnp.mean(y, axis=-1, keepdims=True)
    yc = y - mu
    var = jnp.mean(yc * yc, axis=-1, keepdims=True)
    rstd = lax.rsqrt(var + LN_EPS)
    xhat = yc * rstd
    dxhat = do * g
    dy = rstd * (dxhat - jnp.mean(dxhat, axis=-1, keepdims=True) - xhat * jnp.mean(dxhat * xhat, axis=-1, keepdims=True))
    return dy, jnp.sum(do * xhat, axis=0, keepdims=True), jnp.sum(do, axis=0, keepdims=True)


def _accumulate(ref, val, first):
    @pl.when(first)
    def _():
        ref[...] = val

    @pl.when(jnp.logical_not(first))
    def _():
        ref[...] += val


def _ln_bwd_store(do, y, g, dy_ref, dybf_ref, dg_ref, db_ref, *, first):
    dy, pg, pb = _ln_bwd_rows(do, y, g)
    dy_ref[...] = dy
    dybf_ref[...] = dy.astype(BF16)
    _accumulate(dg_ref, pg, first)
    _accumulate(db_ref, pb, first)


def _layer_norm_rows(y, g, b):
    mu = jnp.mean(y, axis=-1, keepdims=True)
    yc = y - mu
    var = jnp.mean(yc * yc, axis=-1, keepdims=True)
    return yc * lax.rsqrt(var + LN_EPS) * g + b


def _proj_ln(a, w, res, g, b, *, tm, name):
    s, kdim = a.shape

    def body(a_ref, w_ref, res_ref, g_ref, b_ref, pre_ref, o_ref, obf_ref):
        for rs in _row_parts(tm):
            h = jnp.dot(a_ref[rs, :], w_ref[...], preferred_element_type=F32)
            pre = ALPHA * res_ref[rs, :] + h
            out = _layer_norm_rows(pre, g_ref[...], b_ref[...])
            pre_ref[rs, :] = pre
            o_ref[rs, :] = out
            obf_ref[rs, :] = out.astype(BF16)

    row = lambda i: (i, 0)
    fix = lambda i: (0, 0)
    return pl.pallas_call(
        body, name=name, grid=(s // tm,),
        out_shape=(jax.ShapeDtypeStruct((s, D_MODEL), F32), jax.ShapeDtypeStruct((s, D_MODEL), F32),
                   jax.ShapeDtypeStruct((s, D_MODEL), BF16)),
        in_specs=[pl.BlockSpec((tm, kdim), row), pl.BlockSpec((kdim, D_MODEL), fix), pl.BlockSpec((tm, D_MODEL), row),
                  pl.BlockSpec((1, D_MODEL), fix), pl.BlockSpec((1, D_MODEL), fix)],
        out_specs=(pl.BlockSpec((tm, D_MODEL), row),) * 3,
        compiler_params=_params(("parallel",)),
    )(a, w, res, g, b)


def _row_parts(tm):
    sub = min(tm, SUB_ROWS)
    return [pl.ds(r * sub, sub) for r in range(tm // sub)]


def _ffn_in(xbf, w, *, tm, tn, name):
    s = xbf.shape[0]
    nj = D_FF // tn

    def body(x_ref, wg_ref, wu_ref, g_ref, u_ref, act_ref):
        for rs in _row_parts(tm):
            xv = x_ref[rs, :]
            gg = jnp.dot(xv, wg_ref[...], preferred_element_type=F32)
            uu = jnp.dot(xv, wu_ref[...], preferred_element_type=F32)
            g_ref[rs, :] = gg.astype(BF16)
            u_ref[rs, :] = uu.astype(BF16)
            act_ref[rs, :] = (gg * _sigmoid(gg) * uu).astype(BF16)

    out = jax.ShapeDtypeStruct((s, D_FF), BF16)
    tile = pl.BlockSpec((tm, tn), lambda j, i: (i, j))
    return pl.pallas_call(
        body, name=name, grid=(nj, s // tm),
        out_shape=(out, out, out),
        in_specs=[pl.BlockSpec((tm, D_MODEL), lambda j, i: (i, 0)), pl.BlockSpec((D_MODEL, tn), lambda j, i: (0, j)),
                  pl.BlockSpec((D_MODEL, tn), lambda j, i: (0, j + nj))],
        out_specs=(tile, tile, tile),
        compiler_params=_params(("parallel", "parallel")),
    )(xbf, w, w)


def _ffn_bwd_act(dybf, w_out, g, u, *, tm, tn, name):
    s = dybf.shape[0]

    def body(dy_ref, w_ref, g_ref, u_ref, dg_ref, du_ref):
        for rs in _row_parts(tm):
            dact = lax.dot_general(dy_ref[rs, :], w_ref[...], (((1,), (1,)), ((), ())), preferred_element_type=F32)
            gg = g_ref[rs, :].astype(F32)
            uu = u_ref[rs, :].astype(F32)
            sg = _sigmoid(gg)
            dg_ref[rs, :] = (dact * uu * sg * (1.0 + gg * (1.0 - sg))).astype(BF16)
            du_ref[rs, :] = (dact * gg * sg).astype(BF16)

    out = jax.ShapeDtypeStruct((s, D_FF), BF16)
    tile = pl.BlockSpec((tm, tn), lambda j, i: (i, j))
    return pl.pallas_call(
        body, name=name, grid=(D_FF // tn, s // tm),
        out_shape=(out, out),
        in_specs=[pl.BlockSpec((tm, D_MODEL), lambda j, i: (i, 0)), pl.BlockSpec((tn, D_MODEL), lambda j, i: (j, 0)),
                  tile, tile],
        out_specs=(tile, tile),
        compiler_params=_params(("parallel", "parallel")),
    )(dybf, w_out, g, u)


def _ffn_out_ple(act, w_out, res, g, b, p, w_gate, w_proj, *, tm, name):
    s = act.shape[0]

    def body(a_ref, w_ref, res_ref, g_ref, b_ref, p_ref, wg_ref, wp_ref, pre_ref, x2bf_ref, o_ref, obf_ref):
        for rs in _row_parts(tm):
            pre = ALPHA * res_ref[rs, :] + jnp.dot(a_ref[rs, :], w_ref[...], preferred_element_type=F32)
            x2 = _layer_norm_rows(pre, g_ref[...], b_ref[...])
            x2bf = x2.astype(BF16)
            pre_ref[rs, :] = pre
            x2bf_ref[rs, :] = x2bf
            gate = jnp.dot(x2bf, wg_ref[...], preferred_element_type=F32)
            pp = jnp.dot(p_ref[rs, :].astype(BF16), wp_ref[...], preferred_element_type=F32)
            out = x2 + _sigmoid(gate) * pp
            o_ref[rs, :] = out
            obf_ref[rs, :] = out.astype(BF16)

    row = lambda i: (i, 0)
    fix = lambda i: (0, 0)
    tile = pl.BlockSpec((tm, D_MODEL), row)
    vec = pl.BlockSpec((1, D_MODEL), fix)
    act_t = lambda dt: jax.ShapeDtypeStruct((s, D_MODEL), dt)
    return pl.pallas_call(
        body, name=name, grid=(s // tm,),
        out_shape=(act_t(F32), act_t(BF16), act_t(F32), act_t(BF16)),
        in_specs=[pl.BlockSpec((tm, D_FF), row), pl.BlockSpec((D_FF, D_MODEL), fix), tile, vec, vec,
                  pl.BlockSpec((tm, PLE_DIM), row), pl.BlockSpec((D_MODEL, D_MODEL), fix),
                  pl.BlockSpec((PLE_DIM, D_MODEL), fix)],
        out_specs=(tile, tile, tile, tile),
        compiler_params=_params(("parallel",)),
    )(act, w_out, res, g, b, p, w_gate, w_proj)


def _ple_ln_bwd(dx3, x2bf, p, w_gate, w_proj, pre, g, *, tm, name):
    s = dx3.shape[0]

    def body(d_ref, xbf_ref, p_ref, wg_ref, wp_ref, pre_ref, g_ref, da_ref, dpp_ref, dy_ref, dybf_ref, dg_ref, db_ref):
        pg = jnp.zeros((1, D_MODEL), F32)
        pb = jnp.zeros((1, D_MODEL), F32)
        for rs in _row_parts(tm):
            d = d_ref[rs, :]
            a = jnp.dot(xbf_ref[rs, :], wg_ref[...], preferred_element_type=F32)
            pp = jnp.dot(p_ref[rs, :].astype(BF16), wp_ref[...], preferred_element_type=F32)
            sg = _sigmoid(a)
            da = (d * pp * sg * (1.0 - sg)).astype(BF16)
            da_ref[rs, :] = da
            dpp_ref[rs, :] = (d * sg).astype(BF16)
            dx2 = d + lax.dot_general(da, wg_ref[...], (((1,), (1,)), ((), ())), preferred_element_type=F32)
            dy, qg, qb = _ln_bwd_rows(dx2, pre_ref[rs, :], g_ref[...])
            dy_ref[rs, :] = dy
            dybf_ref[rs, :] = dy.astype(BF16)
            pg, pb = pg + qg, pb + qb
        _accumulate(dg_ref, pg, pl.program_id(0) == 0)
        _accumulate(db_ref, pb, pl.program_id(0) == 0)

    row = lambda i: (i, 0)
    fix = lambda i: (0, 0)
    tile = pl.BlockSpec((tm, D_MODEL), row)
    vec = pl.BlockSpec((1, D_MODEL), fix)
    act = lambda dt: jax.ShapeDtypeStruct((s, D_MODEL), dt)
    return pl.pallas_call(
        body, name=name, grid=(s // tm,),
        out_shape=(act(BF16), act(BF16), act(F32), act(BF16), jax.ShapeDtypeStruct((1, D_MODEL), F32),
                   jax.ShapeDtypeStruct((1, D_MODEL), F32)),
        in_specs=[tile, tile, pl.BlockSpec((tm, PLE_DIM), row), pl.BlockSpec((D_MODEL, D_MODEL), fix),
                  pl.BlockSpec((PLE_DIM, D_MODEL), fix), tile, vec],
        out_specs=(tile, tile, tile, tile, vec, vec),
        compiler_params=_params(("arbitrary",)),
    )(dx3, x2bf, p, w_gate, w_proj, pre, g)


def _loss_head(y, target, *, tm, name):
    s = y.shape[0]

    def body(y_ref, t_ref, dy_ref, loss_ref, acc_ref):
        err = y_ref[...] - t_ref[...]
        dy_ref[...] = err * (1.0 / D_MODEL)
        part = jnp.sum(err * err, axis=0, keepdims=True)

        @pl.when(pl.program_id(0) == 0)
        def _():
            acc_ref[...] = part

        @pl.when(pl.program_id(0) > 0)
        def _():
            acc_ref[...] += part

        @pl.when(pl.program_id(0) == pl.num_programs(0) - 1)
        def _():
            tot = jnp.sum(acc_ref[...], axis=1, keepdims=True) * (0.5 / D_MODEL)
            loss_ref[...] = jnp.broadcast_to(tot, (8, LANES))

    row = lambda i: (i, 0)
    return pl.pallas_call(
        body, name=name, grid=(s // tm,),
        out_shape=(jax.ShapeDtypeStruct((s, D_MODEL), F32), jax.ShapeDtypeStruct((8, LANES), F32)),
        in_specs=[pl.BlockSpec((tm, D_MODEL), row), pl.BlockSpec((tm, D_MODEL), row)],
        out_specs=(pl.BlockSpec((tm, D_MODEL), row), pl.BlockSpec((8, LANES), lambda i: (0, 0))),
        scratch_shapes=[pltpu.VMEM((1, D_MODEL), F32)],
        compiler_params=_params(("arbitrary",)),
    )(y, target)


def _rope_tables(s):
    inv = ROPE_THETA ** (-jnp.arange(0, ROPE_DIM, 2, dtype=F32) / ROPE_DIM)
    ang = jnp.arange(s, dtype=F32)[:, None] * inv[None, :]
    cos, sin = jnp.cos(ang), jnp.sin(ang)
    ones = jnp.ones((s, HEAD_DIM - ROPE_DIM), F32)
    c_head = jnp.concatenate([cos, cos, ones], axis=1)
    s_head = jnp.concatenate([-sin, sin, 0.0 * ones], axis=1)
    return jnp.concatenate([c_head, c_head], axis=1), jnp.concatenate([s_head, s_head], axis=1)


def _rope(v, cos, sin):
    n = v.shape[1] // LANES
    width = v.shape[1]
    cos_w = jnp.tile(cos, (1, n)) if n > 1 else cos
    sin_w = jnp.tile(sin, (1, n)) if n > 1 else sin
    dim = lax.broadcasted_iota(jnp.int32, (1, width), 1) % HEAD_DIM
    partner = jnp.where(dim < ROPE_DIM // 2, pltpu.roll(v, width - ROPE_DIM // 2, 1), pltpu.roll(v, ROPE_DIM // 2, 1))
    return v * cos_w + partner * sin_w


def _unrope(dv, cos, sin):
    n = dv.shape[1] // LANES
    width = dv.shape[1]
    cos_w = jnp.tile(cos, (1, n)) if n > 1 else cos
    sin_w = jnp.tile(sin, (1, n)) if n > 1 else sin
    t = dv * sin_w
    dim = lax.broadcasted_iota(jnp.int32, (1, width), 1) % HEAD_DIM
    partner = jnp.where(dim < ROPE_DIM // 2, pltpu.roll(t, width - ROPE_DIM // 2, 1),
                        jnp.where(dim < ROPE_DIM, pltpu.roll(t, ROPE_DIM // 2, 1), 0.0))
    return dv * cos_w + partner


def _att_mask(i, nb):
    rows = GROUP * ATT_BLOCK
    r = lax.broadcasted_iota(jnp.int32, (rows, 3 * ATT_BLOCK), 0) % ATT_BLOCK
    cidx = lax.broadcasted_iota(jnp.int32, (rows, 3 * ATT_BLOCK), 1)
    rel = r + ATT_BLOCK - cidx
    ok = (rel <= ATT_BLOCK) & (rel >= -ATT_BLOCK)
    ok = ok & ((cidx >= ATT_BLOCK) | (i > 0)) & ((cidx < 2 * ATT_BLOCK) | (i < nb - 1))
    return ok


def _att_mask_t(i, nb):
    cols = GROUP * ATT_BLOCK
    cidx = lax.broadcasted_iota(jnp.int32, (3 * ATT_BLOCK, cols), 0)
    r = lax.broadcasted_iota(jnp.int32, (3 * ATT_BLOCK, cols), 1) % ATT_BLOCK
    rel = r + ATT_BLOCK - cidx
    ok = (rel <= ATT_BLOCK) & (rel >= -ATT_BLOCK)
    return ok & ((cidx >= ATT_BLOCK) | (i > 0)) & ((cidx < 2 * ATT_BLOCK) | (i < nb - 1))


def _sink_lanes(sink_ref, h):
    cols = GROUP * ATT_BLOCK
    grp = lax.broadcasted_iota(jnp.int32, (1, cols), 1) // ATT_BLOCK
    out = jnp.zeros((1, cols), F32)
    for gq in range(GROUP):
        out = jnp.where(grp == gq, sink_ref[GROUP * h + gq], out)
    return out


def _half_mask(half):
    lane = lax.broadcasted_iota(jnp.int32, (1, LANES), 1)
    return (lane // HEAD_DIM) == half


def _stack_q(q, h):
    parts = []
    for gq in range(GROUP):
        n = GROUP * h + gq
        grp = q[:, LANES * (n // 2):LANES * (n // 2 + 1)]
        grp = jnp.where(_half_mask(n % 2), grp, 0.0)
        if n % 2 != h % 2:
            grp = pltpu.roll(grp, HEAD_DIM, 1)
        parts.append(grp)
    return jnp.concatenate(parts, axis=0)


def _unstack_q(stacked, h, acc):
    for gq in range(GROUP):
        n = GROUP * h + gq
        grp = stacked[ATT_BLOCK * gq:ATT_BLOCK * (gq + 1), :]
        grp = jnp.where(_half_mask(h % 2), grp, 0.0)
        if n % 2 != h % 2:
            grp = pltpu.roll(grp, HEAD_DIM, 1)
        acc[n // 2] = grp if acc[n // 2] is None else acc[n // 2] + grp
    return acc


def _sink_rows(sink_ref, h):
    rows = GROUP * ATT_BLOCK
    grp = lax.broadcasted_iota(jnp.int32, (rows, 1), 0) // ATT_BLOCK
    out = jnp.zeros((rows, 1), F32)
    for gq in range(GROUP):
        out = jnp.where(grp == gq, sink_ref[GROUP * h + gq], out)
    return out


def _att_probs(qs, kh, sink, valid):
    s = lax.dot_general(qs, kh, (((1,), (1,)), ((), ())), preferred_element_type=F32)
    s = jnp.where(valid, s, NEG)
    m = jnp.maximum(jnp.max(s, axis=-1, keepdims=True), sink)
    p = jnp.exp(s - m)
    es = jnp.exp(sink - m)
    den = jnp.sum(p, axis=-1, keepdims=True) + es
    inv = 1.0 / den
    return p * inv, es * inv


def _att_specs(nb):
    prev = lambda i: (jnp.maximum(i - 1, 0), 0)
    cur = lambda i: (i, 0)
    nxt = lambda i: (jnp.minimum(i + 1, nb - 1), 0)
    kv = lambda f: (lambda i: (f(i)[0], 2))
    tab = [pl.BlockSpec((ATT_BLOCK, LANES), f) for f in (cur, prev, cur, nxt)]
    z = [pl.BlockSpec((ATT_BLOCK, D_MODEL), cur)] + [pl.BlockSpec((ATT_BLOCK, 2 * KV_DIM), kv(f)) for f in (prev, cur, nxt)]
    return z, tab


def _att_load(zq_ref, kp_ref, kc_ref, kn_ref, cq_ref, sq_ref, cp_ref, sp_ref, cc_ref, sc_ref, cn_ref, sn_ref):
    q = (_rope(zq_ref[...], cq_ref[...], sq_ref[...]) * (HEAD_DIM ** -0.5))
    ks, vs = [], []
    for ref, c_ref, s_ref in ((kp_ref, cp_ref, sp_ref), (kc_ref, cc_ref, sc_ref), (kn_ref, cn_ref, sn_ref)):
        kvb = ref[...]
        ks.append(_rope(kvb[:, :KV_DIM], c_ref[...], s_ref[...]))
        vs.append(kvb[:, KV_DIM:])
    return q, jnp.concatenate(ks, axis=0).astype(BF16), jnp.concatenate(vs, axis=0).astype(BF16)


def _att_fwd(z, sink, cos, sin, *, comm=None, name):
    s = z.shape[0]
    nb = s // ATT_BLOCK

    def body(zq_ref, kp_ref, kc_ref, kn_ref, cq_ref, cp_ref, cc_ref, cn_ref, sq_ref, sp_ref, sc_ref, sn_ref, sink_ref,
             o_ref):
        i = pl.program_id(0)
        q, k, v = _att_load(zq_ref, kp_ref, kc_ref, kn_ref, cq_ref, sq_ref, cp_ref, sp_ref, cc_ref, sc_ref, cn_ref, sn_ref)
        valid = _att_mask(i, nb)
        acc = [None] * (N_Q_HEADS // 2)
        for h in range(N_KV_HEADS):
            lanes = slice(LANES * (h // 2), LANES * (h // 2 + 1))
            qs = _stack_q(q, h).astype(BF16)
            prob, _ = _att_probs(qs, k[:, lanes], _sink_rows(sink_ref, h), valid)
            oh = jnp.dot(prob.astype(BF16), v[:, lanes], preferred_element_type=F32)
            acc = _unstack_q(oh, h, acc)
        o_ref[...] = jnp.concatenate(acc, axis=1).astype(BF16)

    zspecs, tab = _att_specs(nb)
    return _call(
        body, name=name, grid=(nb,),
        out_shape=(jax.ShapeDtypeStruct((s, D_MODEL), BF16),),
        in_specs=zspecs + tab + tab + [pl.BlockSpec(memory_space=pltpu.SMEM)],
        out_specs=(pl.BlockSpec((ATT_BLOCK, D_MODEL), lambda i: (i, 0)),),
        args=(z, z, z, z, cos, cos, cos, cos, sin, sin, sin, sin, sink), sem=("parallel",), comm=comm,
        edge=lambda: (pl.program_id(0) == 0, pl.program_id(0) == nb - 1))


def _att_bwd(z, do, sink, cos, sin, *, comm=None, name):
    s = z.shape[0]
    nb = s // ATT_BLOCK

    def body(zq_ref, kp_ref, kc_ref, kn_ref, cq_ref, cp_ref, cc_ref, cn_ref, sq_ref, sp_ref, sc_ref, sn_ref, sink_ref,
             do_ref, dq_ref, part_ref, dsink_ref):
        i = pl.program_id(0)
        q, k, v = _att_load(zq_ref, kp_ref, kc_ref, kn_ref, cq_ref, sq_ref, cp_ref, sp_ref, cc_ref, sc_ref, cn_ref, sn_ref)
        valid = _att_mask_t(i, nb)
        dout = do_ref[...].astype(F32)
        dq_acc = [None] * (N_Q_HEADS // 2)
        dk_acc = [None] * 2
        dv_acc = [None] * 2
        rows = []
        nt = (((1,), (1,)), ((), ()))
        for h in range(N_KV_HEADS):
            grp = h // 2
            lanes = slice(LANES * grp, LANES * (grp + 1))
            qs = _stack_q(q, h).astype(BF16)
            dos = _stack_q(dout, h).astype(BF16)
            sink = _sink_lanes(sink_ref, h)
            sc = jnp.where(valid, lax.dot_general(k[:, lanes], qs, nt, preferred_element_type=F32), NEG)
            m = jnp.maximum(jnp.max(sc, axis=0, keepdims=True), sink)
            p = jnp.exp(sc - m)
            es = jnp.exp(sink - m)
            inv = 1.0 / (jnp.sum(p, axis=0, keepdims=True) + es)
            prob = p * inv
            dprob = lax.dot_general(v[:, lanes], dos, nt, preferred_element_type=F32)
            delta = jnp.sum(prob * dprob, axis=0, keepdims=True)
            dsc = (prob * (dprob - delta)).astype(BF16)
            dsk = -(es * inv) * delta
            for gq in range(GROUP):
                tot = jnp.sum(dsk[:, ATT_BLOCK * gq:ATT_BLOCK * (gq + 1)], axis=1, keepdims=True)
                rows.append(jnp.broadcast_to(tot, (1, LANES)))
            dqs = lax.dot_general(dsc, k[:, lanes], (((0,), (0,)), ((), ())), preferred_element_type=F32)
            dq_acc = _unstack_q(dqs, h, dq_acc)
            dkh = jnp.dot(dsc, qs, preferred_element_type=F32)
            dvh = jnp.dot(prob.astype(BF16), dos, preferred_element_type=F32)
            dk_acc[grp] = dkh if dk_acc[grp] is None else dk_acc[grp] + dkh
            dv_acc[grp] = dvh if dv_acc[grp] is None else dv_acc[grp] + dvh
        dq = jnp.concatenate(dq_acc, axis=1) * (HEAD_DIM ** -0.5)
        dq_ref[...] = _unrope(dq, cq_ref[...], sq_ref[...]).astype(BF16)
        part = jnp.concatenate(dk_acc + dv_acc, axis=1)
        for wdw in range(3):
            part_ref[wdw] = part[ATT_BLOCK * wdw:ATT_BLOCK * (wdw + 1), :]
        dsink = jnp.concatenate(rows, axis=0)

        @pl.when(i == 0)
        def _():
            dsink_ref[...] = dsink

        @pl.when(i > 0)
        def _():
            dsink_ref[...] += dsink

    zspecs, tab = _att_specs(nb)
    return _call(
        body, name=name, grid=(nb,),
        out_shape=(jax.ShapeDtypeStruct((s, D_MODEL), BF16), jax.ShapeDtypeStruct((nb, 3, ATT_BLOCK, 2 * KV_DIM), F32),
                   jax.ShapeDtypeStruct((N_Q_HEADS, LANES), F32)),
        in_specs=zspecs + tab + tab + [pl.BlockSpec(memory_space=pltpu.SMEM), pl.BlockSpec((ATT_BLOCK, D_MODEL), lambda i: (i, 0))],
        out_specs=(pl.BlockSpec((ATT_BLOCK, D_MODEL), lambda i: (i, 0)),
                   pl.BlockSpec((None, 3, ATT_BLOCK, 2 * KV_DIM), lambda i: (i, 0, 0, 0)),
                   pl.BlockSpec((N_Q_HEADS, LANES), lambda i: (0, 0))),
        args=(z, z, z, z, cos, cos, cos, cos, sin, sin, sin, sin, sink, do), sem=("arbitrary",), comm=comm,
        edge=lambda: (pl.program_id(0) == 0, pl.program_id(0) == nb - 1))


def _att_bwd_kv(part, cos, sin, *, name):
    nb = part.shape[0]

    def body(pn_ref, pc_ref, pp_ref, c_ref, s_ref, o_ref):
        j = pl.program_id(0)
        tot = pc_ref[...]
        tot = tot + jnp.where(j < nb - 1, pn_ref[...], 0.0)
        tot = tot + jnp.where(j > 0, pp_ref[...], 0.0)
        dk = _unrope(tot[:, :KV_DIM], c_ref[...], s_ref[...])
        o_ref[...] = jnp.concatenate([dk, tot[:, KV_DIM:]], axis=1).astype(BF16)

    blk = (None, None, ATT_BLOCK, 2 * KV_DIM)
    return pl.pallas_call(
        body, name=name, grid=(nb,),
        out_shape=jax.ShapeDtypeStruct((nb * ATT_BLOCK, 2 * KV_DIM), BF16),
        in_specs=[pl.BlockSpec(blk, lambda j: (jnp.minimum(j + 1, nb - 1), 0, 0, 0)),
                  pl.BlockSpec(blk, lambda j: (j, 1, 0, 0)),
                  pl.BlockSpec(blk, lambda j: (jnp.maximum(j - 1, 0), 2, 0, 0)),
                  pl.BlockSpec((ATT_BLOCK, LANES), lambda j: (j, 0)), pl.BlockSpec((ATT_BLOCK, LANES), lambda j: (j, 0))],
        out_specs=pl.BlockSpec((ATT_BLOCK, 2 * KV_DIM), lambda j: (j, 0)),
        compiler_params=_params(("parallel",)),
    )(part, part, part, cos, sin)


def _bdot(a, b, dims):
    return lax.dot_general(a.astype(BF16), b.astype(BF16), (dims, ((), ())), preferred_element_type=F32)


@jax.custom_vjp
def _dot_nn(a, b):
    return _bdot(a, b, ((1,), (0,)))


@jax.custom_vjp
def _dot_nt(a, b):
    return _bdot(a, b, ((1,), (1,)))


@jax.custom_vjp
def _dot_tn(a, b):
    return _bdot(a, b, ((0,), (0,)))


_dot_nn.defvjp(lambda a, b: (_dot_nn(a, b), (a, b)), lambda r, d: (_dot_nt(d, r[1]), _dot_tn(r[0], d)))
_dot_nt.defvjp(lambda a, b: (_dot_nt(a, b), (a, b)), lambda r, d: (_dot_nn(d, r[1]), _dot_tn(d, r[0])))
_dot_tn.defvjp(lambda a, b: (_dot_tn(a, b), (a, b)), lambda r, d: (_dot_nt(r[1], d), _dot_nn(r[0], d)))


def _running_sum(v, up):
    n = v.shape[0]
    rows = lax.broadcasted_iota(jnp.int32, v.shape, 0)
    sh = 1
    while sh < n:
        if up:
            v = v + jnp.where(rows < n - sh, pltpu.roll(v, n - sh, 0), 0.0)
        else:
            v = v + jnp.where(rows >= sh, pltpu.roll(v, sh, 0), 0.0)
        sh *= 2
    return v


@jax.custom_vjp
def _sum_down(v):
    return _running_sum(v, False)


@jax.custom_vjp
def _sum_up(v):
    return _running_sum(v, True)


_sum_down.defvjp(lambda v: (_running_sum(v, False), None), lambda _, d: (_sum_up(d),))
_sum_up.defvjp(lambda v: (_running_sum(v, True), None), lambda _, d: (_sum_down(d),))

N_SUB = HG_CHUNK // HG_SUB


def _fold_blocks(v):
    out = v[:HG_CHUNK]
    for i in range(1, N_SUB):
        out = out + v[HG_CHUNK * i:HG_CHUNK * (i + 1)]
    return out


@jax.custom_vjp
def _fold(v):
    return _fold_blocks(v)


_fold.defvjp(lambda v: (_fold_blocks(v), None), lambda _, d: (jnp.concatenate([d] * N_SUB, axis=0),))


def _hg_consts(rev):
    c, sub = HG_CHUNK, HG_SUB
    rowpos = lax.broadcasted_iota(jnp.int32, (c, HG_DIM), 0)
    rr = lax.broadcasted_iota(jnp.int32, (N_SUB * c, c), 0)
    key = lax.broadcasted_iota(jnp.int32, (N_SUB * c, c), 1)
    blk, qry = rr // c, rr % c
    if rev:
        rowpos, qry, key = c - 1 - rowpos, c - 1 - qry, c - 1 - key
    keep = (key // sub == blk) & (key <= qry)
    return keep, rowpos


def _pick(b, rowpos, t):
    return jnp.sum(jnp.where(rowpos == t, b, 0.0), axis=0, keepdims=True)


def _hg_local(zq, zf, zv, lbv, consts, dots):
    dot_nn, dot_nt, dot_tn, cum, fold = dots
    keep, rowpos = consts
    sig = _sigmoid(zf)
    f = lbv + (1.0 - lbv) * sig
    g = jnp.log(f)
    k = (1.0 - lbv) * (1.0 - sig)
    q = zq * _sigmoid(zq)
    b = cum(g)
    ends = [_pick(b, rowpos, (j + 1) * HG_SUB - 1) for j in range(N_SUB)]
    b_last = ends[-1]
    b_end = b_last
    for j in range(N_SUB - 1):
        b_end = jnp.where(rowpos // HG_SUB == j, ends[j], b_end)
    kc = k * jnp.exp(b_end - b)
    qbs = [q * jnp.exp(jnp.where(rowpos >= j * HG_SUB, b - ends[j], 0.0)) for j in range(N_SUB)]
    scores = fold(jnp.where(keep, dot_nt(jnp.concatenate(qbs, axis=0), kc), 0.0))
    return dot_nn(scores, zv), q * jnp.exp(b), k * jnp.exp(b_last - b), jnp.exp(b_last)


def _hg_chunk(zq, zf, zv, lbv, st, consts, dots):
    intra, qs, kd, dec = _hg_local(zq, zf, zv, lbv, consts, dots)
    return intra + dots[1](qs, st), dec * st + dots[2](zv, kd)


def _hg_dots(diff, rev):
    if diff:
        return _dot_nn, _dot_nt, _dot_tn, (_sum_up if rev else _sum_down), _fold
    return (lambda a, b: _bdot(a, b, ((1,), (0,))), lambda a, b: _bdot(a, b, ((1,), (1,))),
            lambda a, b: _bdot(a, b, ((0,), (0,))), lambda v: _running_sum(v, rev), _fold_blocks)


def _hg_specs(ts, nch, trow):
    tile = pl.BlockSpec((ts, HG_DIM), lambda h, t: (trow(t), h))
    mats = pl.BlockSpec((None, nch, HG_DIM, HG_DIM), lambda h, t: (h, trow(t), 0, 0))
    vecs = pl.BlockSpec((None, nch, 1, HG_DIM), lambda h, t: (h, trow(t), 0, 0))
    return tile, mats, vecs


def _time_order(nch, rev):
    return range(nch - 1, -1, -1) if rev else range(nch)


def _chunk_rows(c):
    return pl.ds(c * HG_CHUNK, HG_CHUNK)


def _hg_edge(nt):
    h, t = pl.program_id(0), pl.program_id(1)
    return (h == 0) & (t == 0), (h == HG_HEADS - 1) & (t == nt - 1)


def _hg_fwd(z, lb, *, rev, ts, comm=None, name):
    s = z.shape[0]
    nt = s // ts
    nch = ts // HG_CHUNK
    fcol = HG_HEADS * (2 if rev else 1)

    def body(zq_ref, zf_ref, zv_ref, lb_ref, o_ref, st_ref, qs_ref, dec_ref, state_ref):
        @pl.when(pl.program_id(1) == 0)
        def _():
            state_ref[...] = jnp.zeros_like(state_ref)

        consts = _hg_consts(rev)
        dots = _hg_dots(False, rev)
        lbv = lb_ref[...]
        local = {}
        for c in range(nch):
            rows = _chunk_rows(c)
            zv = zv_ref[rows, :]
            intra, qs, kd, dec = _hg_local(zq_ref[rows, :], zf_ref[rows, :], zv, lbv, consts, dots)
            qs = qs.astype(BF16)
            qs_ref[rows, :] = qs
            dec_ref[c] = dec
            local[c] = (intra, qs, dec, dots[2](zv, kd))
        st = state_ref[...]
        for c in _time_order(nch, rev):
            intra, qs, dec, upd = local[c]
            st_ref[c] = st.astype(BF16)
            o_ref[_chunk_rows(c), :] = intra + _bdot(qs, st, ((1,), (1,)))
            st = dec * st + upd
        state_ref[...] = st

    trow = (lambda t: nt - 1 - t) if rev else (lambda t: t)
    col = lambda off: pl.BlockSpec((ts, HG_DIM), lambda h, t: (trow(t), off + h))
    tile, mats, vecs = _hg_specs(ts, nch, trow)
    nchunks = s // HG_CHUNK
    return _call(
        body, name=name, grid=(HG_HEADS, nt),
        out_shape=(jax.ShapeDtypeStruct((s, D_MODEL), F32),
                   jax.ShapeDtypeStruct((HG_HEADS, nchunks, HG_DIM, HG_DIM), BF16),
                   jax.ShapeDtypeStruct((s, D_MODEL), BF16),
                   jax.ShapeDtypeStruct((HG_HEADS, nchunks, 1, HG_DIM), F32)),
        in_specs=[col(0), col(fcol), col(3 * HG_HEADS), pl.BlockSpec((None, 1, HG_DIM), lambda h, t: (h, 0, 0))],
        out_specs=(tile, mats, tile, vecs), args=(z, z, z, lb),
        scratch_shapes=[pltpu.VMEM((HG_DIM, HG_DIM), F32)], sem=("parallel", "arbitrary"), comm=comm,
        edge=lambda: _hg_edge(nt))


def _hg_bwd(z, lb, states, qs, dec, dout, addq, addv, *, rev, ts, comm=None, name):
    s = z.shape[0]
    nt = s // ts
    nch = ts // HG_CHUNK
    fcol = HG_HEADS * (2 if rev else 1)
    has_add = addq is not None

    def body(*refs):
        zq_ref, zf_ref, zv_ref, lb_ref, st_ref, qs_ref, dec_ref, do_ref = refs[:8]
        aq_ref, av_ref = (refs[8], refs[9]) if has_add else (None, None)
        dq_ref, df_ref, dv_ref, dlb_ref, grad_ref = refs[-5:]

        @pl.when(pl.program_id(1) == 0)
        def _():
            grad_ref[...] = jnp.zeros_like(grad_ref)

        consts = _hg_consts(rev)
        dots = _hg_dots(True, rev)
        lbv = lb_ref[...]
        prods = {c: _bdot(do_ref[_chunk_rows(c), :], qs_ref[_chunk_rows(c), :], ((0,), (0,))) for c in range(nch)}
        gleave = {}
        gr = grad_ref[...]
        for c in reversed(_time_order(nch, rev)):
            gleave[c] = gr
            gr = dec_ref[c] * gr + prods[c]
        grad_ref[...] = gr
        dlb_blk = jnp.zeros((1, HG_DIM), F32)
        for c in range(nch):
            rows = _chunk_rows(c)
            fn = lambda a, b2, c2, d2, e2: _hg_chunk(a, b2, c2, d2, e2, consts, dots)
            _, pull = jax.vjp(fn, zq_ref[rows, :], zf_ref[rows, :], zv_ref[rows, :], lbv, st_ref[c].astype(F32))
            dq, df, dv, dlb, _ = pull((do_ref[rows, :], gleave[c]))
            if has_add:
                dq = dq + aq_ref[rows, :]
                dv = dv + av_ref[rows, :]
            dq_ref[rows, :] = dq.astype(dq_ref.dtype)
            df_ref[rows, :] = df.astype(BF16)
            dv_ref[rows, :] = dv.astype(dv_ref.dtype)
            dlb_blk = dlb_blk + dlb

        @pl.when(pl.program_id(1) == 0)
        def _():
            dlb_ref[...] = dlb_blk

        @pl.when(pl.program_id(1) > 0)
        def _():
            dlb_ref[...] += dlb_blk

    trow = (lambda t: t) if rev else (lambda t: nt - 1 - t)
    col = lambda off: pl.BlockSpec((ts, HG_DIM), lambda h, t: (trow(t), off + h))
    tile, mats, vecs = _hg_specs(ts, nch, trow)
    in_specs = [col(0), col(fcol), col(3 * HG_HEADS), pl.BlockSpec((None, 1, HG_DIM), lambda h, t: (h, 0, 0)),
                mats, tile, vecs, tile]
    args = [z, z, z, lb, states, qs, dec, dout]
    if has_add:
        in_specs += [tile, tile]
        args += [addq, addv]
    act = lambda dt: jax.ShapeDtypeStruct((s, D_MODEL), dt)
    sums = BF16 if has_add else F32
    return _call(
        body, name=name, grid=(HG_HEADS, nt),
        out_shape=(act(sums), act(BF16), act(sums), jax.ShapeDtypeStruct((HG_HEADS, 1, HG_DIM), F32)),
        in_specs=in_specs,
        out_specs=(tile, tile, tile, pl.BlockSpec((None, 1, HG_DIM), lambda h, t: (h, 0, 0))), args=tuple(args),
        scratch_shapes=[pltpu.VMEM((HG_DIM, HG_DIM), F32)], sem=("parallel", "arbitrary"), comm=comm,
        edge=lambda: _hg_edge(nt))


def _hg_post(of, ob, z, norm_g, *, tm, name):
    s = of.shape[0]

    def body(of_ref, ob_ref, gate_ref, ng_ref, y_ref):
        gn = ng_ref[...]
        for h in range(HG_HEADS):
            ln = slice(HG_DIM * h, HG_DIM * (h + 1))
            o = of_ref[:, ln] + ob_ref[:, ln]
            r = lax.rsqrt(jnp.mean(o * o, axis=-1, keepdims=True) + LN_EPS)
            gt = gate_ref[:, ln]
            y_ref[:, ln] = (o * r * gn * gt * _sigmoid(gt)).astype(BF16)

    row = lambda i: (i, 0)
    return pl.pallas_call(
        body, name=name, grid=(s // tm,),
        out_shape=jax.ShapeDtypeStruct((s, D_MODEL), BF16),
        in_specs=[pl.BlockSpec((tm, D_MODEL), row), pl.BlockSpec((tm, D_MODEL), row),
                  pl.BlockSpec((tm, D_MODEL), lambda i: (i, 4)), pl.BlockSpec((1, HG_DIM), lambda i: (0, 0))],
        out_specs=pl.BlockSpec((tm, D_MODEL), row),
        compiler_params=_params(("parallel",)),
    )(of, ob, z, norm_g)


def _hg_post_bwd(dy, of, ob, z, norm_g, *, tm, name):
    s = of.shape[0]

    def body(dy_ref, of_ref, ob_ref, gate_ref, ng_ref, do_ref, dgate_ref, dng_ref):
        gn = ng_ref[...]
        tot = jnp.zeros((1, HG_DIM), F32)
        for h in range(HG_HEADS):
            ln = slice(HG_DIM * h, HG_DIM * (h + 1))
            d = dy_ref[:, ln].astype(F32)
            o = of_ref[:, ln] + ob_ref[:, ln]
            r = lax.rsqrt(jnp.mean(o * o, axis=-1, keepdims=True) + LN_EPS)
            ohat = o * r
            gt = gate_ref[:, ln]
            sg = _sigmoid(gt)
            don = d * gt * sg
            dgate_ref[:, ln] = (d * ohat * gn * sg * (1.0 + gt * (1.0 - sg))).astype(BF16)
            tot = tot + jnp.sum(don * ohat, axis=0, keepdims=True)
            dohat = don * gn
            do_ref[:, ln] = r * (dohat - ohat * jnp.mean(dohat * ohat, axis=-1, keepdims=True))

        @pl.when(pl.program_id(0) == 0)
        def _():
            dng_ref[...] = tot

        @pl.when(pl.program_id(0) > 0)
        def _():
            dng_ref[...] += tot

    row = lambda i: (i, 0)
    return pl.pallas_call(
        body, name=name, grid=(s // tm,),
        out_shape=(jax.ShapeDtypeStruct((s, D_MODEL), F32), jax.ShapeDtypeStruct((s, D_MODEL), BF16),
                   jax.ShapeDtypeStruct((1, HG_DIM), F32)),
        in_specs=[pl.BlockSpec((tm, D_MODEL), row), pl.BlockSpec((tm, D_MODEL), row), pl.BlockSpec((tm, D_MODEL), row),
                  pl.BlockSpec((tm, D_MODEL), lambda i: (i, 4)), pl.BlockSpec((1, HG_DIM), lambda i: (0, 0))],
        out_specs=(pl.BlockSpec((tm, D_MODEL), row), pl.BlockSpec((tm, D_MODEL), row),
                   pl.BlockSpec((1, HG_DIM), lambda i: (0, 0))),
        compiler_params=_params(("arbitrary",)),
    )(dy, of, ob, z, norm_g)


def _lb_fwd(logits, *, name):
    w = logits.shape[1]

    def body(l_ref, o_ref):
        lg = l_ref[...]
        e = jnp.exp(lg - jnp.max(lg, axis=0, keepdims=True))
        sm = e / jnp.sum(e, axis=0, keepdims=True)
        o_ref[0:1, :] = sm[1:2]
        o_ref[1:2, :] = sm[1:2] + sm[2:3] + sm[3:4]

    return pl.pallas_call(body, name=name, out_shape=jax.ShapeDtypeStruct((2, w), F32))(logits)


def _lb_bwd(logits, dlb, *, name):
    w = logits.shape[1]

    def body(l_ref, d_ref, o_ref):
        lg = l_ref[...]
        e = jnp.exp(lg - jnp.max(lg, axis=0, keepdims=True))
        sm = e / jnp.sum(e, axis=0, keepdims=True)
        d1, d3 = d_ref[0:1, :], d_ref[1:2, :]
        dot = sm[1:2] * (d1 + d3) + (sm[2:3] + sm[3:4]) * d3
        o_ref[0:1, :] = -sm[0:1] * dot
        o_ref[1:2, :] = sm[1:2] * (d1 + d3 - dot)
        o_ref[2:3, :] = sm[2:3] * (d3 - dot)
        o_ref[3:4, :] = sm[3:4] * (d3 - dot)

    return pl.pallas_call(body, name=name, out_shape=jax.ShapeDtypeStruct((4, w), F32))(logits, dlb)


def _adamw(w, g, m, v, *, tr, g_off=0, name):
    rows = w.shape[0]
    parts = g.ndim == 3
    c1 = 1.0 / (1.0 - ADAM_B1 ** ADAM_STEP)
    c2 = 1.0 / (1.0 - ADAM_B2 ** ADAM_STEP)

    def body(w_ref, g_ref, m_ref, v_ref, go_ref, d_ref, mo_ref, vo_ref):
        if parts:
            gg = g_ref[0].astype(F32)
            for i in range(1, N_DEV):
                gg = gg + g_ref[i].astype(F32)
        else:
            gg = g_ref[...]
        mm = ADAM_B1 * m_ref[...] + (1.0 - ADAM_B1) * gg
        vv = ADAM_B2 * v_ref[...] + (1.0 - ADAM_B2) * (gg * gg)
        go_ref[...] = gg
        mo_ref[...] = mm
        vo_ref[...] = vv
        d_ref[...] = -ADAM_LR * ((mm * c1) / (jnp.sqrt(vv * c2) + ADAM_EPS) + ADAM_WD * w_ref[...])

    tile = pl.BlockSpec((tr, D_MODEL), lambda i: (i, 0))
    gspec = pl.BlockSpec((N_DEV, tr, D_MODEL), lambda i: (0, i + g_off // tr, 0)) if parts else tile
    out = jax.ShapeDtypeStruct((rows, D_MODEL), F32)
    return pl.pallas_call(
        body, name=name, grid=(rows // tr,),
        out_shape=(out, out, out, out),
        in_specs=[tile, gspec, tile, tile], out_specs=(tile, tile, tile, tile),
        compiler_params=_params(("parallel",)),
    )(w, g, m, v)


def _sum8(parts, *, name):
    def body(p_ref, o_ref):
        tot = p_ref[0]
        for i in range(1, N_DEV):
            tot = tot + p_ref[i]
        o_ref[...] = tot

    return pl.pallas_call(body, name=name, out_shape=jax.ShapeDtypeStruct(parts.shape[1:], parts.dtype))(parts)


def _layer_params(i):
    j = i // 2
    mix = [("att_w_qkv", j, 1), ("att_w_o", j, 0)] if i % 2 == 0 else [("hgrn_w_in", j, 1), ("hgrn_w_o", j, 0)]
    return mix + [("ffn_w_in", i, 1), ("ffn_w_out", i, 0), ("ple_w_gate", i, 0), ("ple_w_proj", i, 1)]


def _pack_local(tree, params):
    return jnp.concatenate([tree[n][j].reshape(-1, D_MODEL) for n, j, _ in params], axis=0)


def _unpack_local(packed, params, like):
    out, r = {}, 0
    for n, _, _ in params:
        shp = like[n].shape[1:]
        k = shp[0] * shp[1] // D_MODEL
        out[n] = packed[r:r + k].reshape(shp)
        r += k
    return out


def _unpack_gathered(gathered, i, like):
    out, r = {}, 0
    for n, _, ax in _layer_params(i):
        shp = like[n].shape[1:]
        k = shp[0] * shp[1] // D_MODEL
        t = gathered[:, r:r + k].reshape((N_DEV,) + shp)
        out[n] = (jnp.moveaxis(t, 0, 1).reshape(shp[0], N_DEV * shp[1]) if ax == 1
                  else t.reshape(N_DEV * shp[0], shp[1]))
        r += k
    return out


def _pack_full(grads, params, like):
    cols = []
    for n, _, ax in params:
        shp = like[n].shape[1:]
        t = (jnp.moveaxis(grads[n].reshape(shp[0], N_DEV, shp[1]), 1, 0) if ax == 1
             else grads[n].reshape(N_DEV, shp[0], shp[1]))
        cols.append(t.reshape(N_DEV, -1, D_MODEL).astype(BF16))
    return jnp.concatenate(cols, axis=1)


def _row_tile(rows):
    return max(t for t in range(16, 257, 16) if rows % t == 0)


SMALL_ROWS = 24


def _pad_row(a):
    flat = a.reshape(1, -1)
    return jnp.pad(flat, ((0, 0), (0, D_MODEL - flat.shape[1])))


def _tile(n, pref):
    return min(n, pref)


def kernel(x, p, att_w_qkv, att_sink, att_w_o, hgrn_w_in, hgrn_lb_logits, hgrn_norm_g, hgrn_w_o, ln_mix_g, ln_mix_b, ffn_w_in, ffn_w_out, ln_ffn_g, ln_ffn_b, ple_w_gate, ple_w_proj, loss_target, m_att_w_qkv, m_att_sink, m_att_w_o, m_hgrn_w_in, m_hgrn_lb_logits, m_hgrn_norm_g, m_hgrn_w_o, m_ln_mix_g, m_ln_mix_b, m_ffn_w_in, m_ffn_w_out, m_ln_ffn_g, m_ln_ffn_b, m_ple_w_gate, m_ple_w_proj, v_att_w_qkv, v_att_sink, v_att_w_o, v_hgrn_w_in, v_hgrn_lb_logits, v_hgrn_norm_g, v_hgrn_w_o, v_ln_mix_g, v_ln_mix_b, v_ffn_w_in, v_ffn_w_out, v_ln_ffn_g, v_ln_ffn_b, v_ple_w_gate, v_ple_w_proj):
    names = ["att_w_qkv", "att_sink", "att_w_o", "hgrn_w_in", "hgrn_lb_logits", "hgrn_norm_g", "hgrn_w_o", "ln_mix_g",
             "ln_mix_b", "ffn_w_in", "ffn_w_out", "ln_ffn_g", "ln_ffn_b", "ple_w_gate", "ple_w_proj"]
    w = dict(zip(names, (att_w_qkv, att_sink, att_w_o, hgrn_w_in, hgrn_lb_logits, hgrn_norm_g, hgrn_w_o, ln_mix_g,
                         ln_mix_b, ffn_w_in, ffn_w_out, ln_ffn_g, ln_ffn_b, ple_w_gate, ple_w_proj)))
    mom = dict(zip(names, (m_att_w_qkv, m_att_sink, m_att_w_o, m_hgrn_w_in, m_hgrn_lb_logits, m_hgrn_norm_g, m_hgrn_w_o,
                           m_ln_mix_g, m_ln_mix_b, m_ffn_w_in, m_ffn_w_out, m_ln_ffn_g, m_ln_ffn_b, m_ple_w_gate,
                           m_ple_w_proj)))
    var = dict(zip(names, (v_att_w_qkv, v_att_sink, v_att_w_o, v_hgrn_w_in, v_hgrn_lb_logits, v_hgrn_norm_g, v_hgrn_w_o,
                           v_ln_mix_g, v_ln_mix_b, v_ffn_w_in, v_ffn_w_out, v_ln_ffn_g, v_ln_ffn_b, v_ple_w_gate,
                           v_ple_w_proj)))
    s = x.shape[1]
    me = 4 * lax.axis_index("x") + 2 * lax.axis_index("y") + lax.axis_index("c")
    tm = _tile(s, 512)
    tbig = _tile(s, 1024)
    ts = _tile(s // 2, 512)
    x0 = x.reshape(s, D_MODEL)
    target = loss_target.reshape(s, D_MODEL)
    pl_in = p.reshape(DEPTH, s, PLE_DIM)

    w_rows = [_pack_local(w, _layer_params(i)).astype(BF16) for i in range(DEPTH)]
    full = _unpack_gathered(_gather(w_rows[0], name="gather_weights"), 0, w)
    lb_rows = jnp.pad(hgrn_lb_logits.reshape(8, HG_DIM), ((0, 0), (0, D_MODEL - HG_DIM)))
    lb_all = _gather(lb_rows, name="gather_lb")[:, :, :HG_DIM]
    logits_full = jnp.moveaxis(lb_all, 0, 1).reshape(DEPTH, 2 * D_MODEL)
    lb = _lb_fwd(logits_full, name="lb_fwd")
    cos, sin = _rope_tables(s)

    saved = []
    xf, xb = x0, x0
    for i in range(DEPTH):
        j = i // 2
        sv = {"x": xf, "xb": xb, "w": full}
        nxt = (w_rows[i + 1], True) if i + 1 < DEPTH else None
        if i % 2 == 0:
            z = _mm(xb, full["att_w_qkv"], n=D_MODEL + 2 * KV_DIM, tm=tbig, tn=512, tk=D_MODEL, name="att_in")
            o, *more = _att_fwd(z, w["att_sink"][j], cos, sin, comm=nxt, name="att_fwd")
            w_o = full["att_w_o"]
        else:
            z = _mm(xb, full["hgrn_w_in"], n=5 * D_MODEL, tm=tbig, tn=1024, tk=D_MODEL, name="hgrn_in")
            lbl = lb[j].reshape(2, HG_HEADS, 1, HG_DIM)
            of, st_f, qs_f, dec_f, *more = _hg_fwd(z, lbl[0], rev=False, ts=ts, comm=nxt, name="hgrn_fwd")
            ob, st_b, qs_b, dec_b = _hg_fwd(z, lbl[1], rev=True, ts=ts, name="hgrn_fwd_rev")
            o = _hg_post(of, ob, z, w["hgrn_norm_g"][j].reshape(1, HG_DIM), tm=tm, name="hgrn_post")
            w_o = full["hgrn_w_o"]
            sv.update(of=of, ob=ob, st_f=st_f, st_b=st_b, lbl=lbl, qs_f=qs_f, qs_b=qs_b, dec_f=dec_f, dec_b=dec_b)
        sv.update(z=z, o=o)
        pre1, x1, x1b = _proj_ln(o, w_o, xf, w["ln_mix_g"][i:i + 1], w["ln_mix_b"][i:i + 1], tm=tm, name="mix_out_ln")
        gg, uu, act = _ffn_in(x1b, full["ffn_w_in"], tm=tm, tn=FF_TILE, name="ffn_in")
        pre2, x2b, xf, xb = _ffn_out_ple(act, full["ffn_w_out"], x1, w["ln_ffn_g"][i:i + 1], w["ln_ffn_b"][i:i + 1],
                                         pl_in[i], full["ple_w_gate"], full["ple_w_proj"], tm=tm, name="ffn_out_ple")
        sv.update(pre1=pre1, x1=x1, x1b=x1b, g=gg, u=uu, act=act, pre2=pre2, x2b=x2b)
        saved.append(sv)
        if nxt is not None:
            full = _unpack_gathered(more[0], i + 1, w)

    dx, loss_blk = _loss_head(xf, target, tm=tm, name="loss_head")
    loss = lax.psum(loss_blk[0, 0], AXES)

    small = {n: [None] * DEPTH for n in ("ln_mix_g", "ln_mix_b", "ln_ffn_g", "ln_ffn_b")}
    dlb_rows = [None] * 4
    dnorm, dsink = [None] * 2, [None] * 2
    recv_late, recv_early = [None] * DEPTH, [None] * DEPTH
    above = None
    mmw = functools.partial(_mm, ta=True, tk=_tile(s, 2048), out_dtype=BF16)
    for i in reversed(range(DEPTH)):
        j = i // 2
        sv = saved[i]
        full, gl = sv["w"], {}
        da, dpp, dy2, dy2b, small["ln_ffn_g"][i], small["ln_ffn_b"][i] = _ple_ln_bwd(
            dx, sv["x2b"], pl_in[i], full["ple_w_gate"], full["ple_w_proj"], sv["pre2"], w["ln_ffn_g"][i:i + 1],
            tm=tm, name="ple_ln_bwd")
        gl["ple_w_gate"] = mmw(sv["x2b"], da, n=D_MODEL, tm=D_MODEL, tn=D_MODEL, name="dw_ple_gate")
        gl["ple_w_proj"] = mmw(pl_in[i], dpp, n=D_MODEL, tm=PLE_DIM, tn=D_MODEL, name="dw_ple_proj")
        dg, du = _ffn_bwd_act(dy2b, full["ffn_w_out"], sv["g"], sv["u"], tm=tm, tn=FF_TILE, name="ffn_bwd_act")
        gl["ffn_w_out"] = mmw(sv["act"], dy2b, n=D_MODEL, tm=FF_TILE, tn=D_MODEL, name="dw_ffn_out")
        dy1, dy1b, small["ln_mix_g"][i], small["ln_mix_b"][i] = _mm(
            dg, full["ffn_w_in"], a2=du, n=D_MODEL, tm=tm, tn=D_MODEL, tk=FF_TILE, tb=True, add=dy2, add_scale=ALPHA,
            ln=(sv["pre1"], w["ln_mix_g"][i:i + 1]), name="ffn_bwd_x_ln")
        gl["ffn_w_in"] = jnp.concatenate(
            [mmw(sv["x1b"], dg, n=D_FF, tm=D_MODEL, tn=FF_TILE, name="dw_ffn_in"),
             mmw(sv["x1b"], du, n=D_FF, tm=D_MODEL, tn=FF_TILE, name="dw_ffn_in")], axis=1)
        n_out, n_inw, n_in = (("att_w_o", "att_w_qkv", D_MODEL + 2 * KV_DIM) if i % 2 == 0
                              else ("hgrn_w_o", "hgrn_w_in", 5 * D_MODEL))
        do = _mm(dy1b, full[n_out], n=D_MODEL, tm=tbig, tn=D_MODEL, tk=D_MODEL, tb=True, out_dtype=BF16,
                 name="mix_out_bwd")
        gl[n_out] = mmw(sv["o"], dy1b, n=D_MODEL, tm=D_MODEL, tn=D_MODEL, name="dw_mix_out")
        early = _pack_full(gl, _layer_params(i)[1:], w)
        comm = (early if above is None else jnp.concatenate([above, early], axis=1), False)
        if i % 2 == 0:
            dzq, part, dsk, *more = _att_bwd(sv["z"], do, w["att_sink"][j], cos, sin, comm=comm, name="att_bwd")
            dzkv = _att_bwd_kv(part, cos, sin, name="att_bwd_kv")
            dz = jnp.concatenate([dzq, dzkv], axis=1)
            dsink[j] = dsk[:, 0]
        else:
            dsum, dgate, dnorm[j] = _hg_post_bwd(do, sv["of"], sv["ob"], sv["z"], w["hgrn_norm_g"][j].reshape(1, HG_DIM),
                                                 tm=tm, name="hgrn_post_bwd")
            dq1, df1, dv1, dlb1, *more = _hg_bwd(sv["z"], sv["lbl"][0], sv["st_f"], sv["qs_f"], sv["dec_f"], dsum, None,
                                                 None, rev=False, ts=ts, comm=comm, name="hgrn_bwd")
            dq2, df2, dv2, dlb2 = _hg_bwd(sv["z"], sv["lbl"][1], sv["st_b"], sv["qs_b"], sv["dec_b"], dsum, dq1, dv1,
                                          rev=True, ts=ts, name="hgrn_bwd_rev")
            dz = jnp.concatenate([dq2, df1, df2, dv2, dgate], axis=1)
            dlb_rows[2 * j] = dlb1.reshape(1, D_MODEL)
            dlb_rows[2 * j + 1] = dlb2.reshape(1, D_MODEL)
        if above is not None:
            recv_late[i + 1] = (more[0], 0)
        recv_early[i] = (more[0], 0 if above is None else above.shape[1])
        dx = _mm(dz, full[n_inw], n=D_MODEL, tm=tbig, tn=D_MODEL, tk=min(n_in, 2560), tb=True, add=dy1, add_scale=ALPHA,
                 name="mix_in_bwd")
        gl[n_inw] = mmw(sv["xb"], dz, n=n_in, tm=D_MODEL, tn=512, name="dw_mix_in")
        above = _pack_full(gl, _layer_params(i)[:1], w)
    grad_x = dx.reshape(x.shape)
    recv_late[0] = (_exchange(above, name="exchange_grads"), 0)

    big_out = [{n: [None] * w[n].shape[0] for n, _ in BIG} for _ in range(4)]
    for i in range(DEPTH):
        for params, (got, off) in ((_layer_params(i)[:1], recv_late[i]), (_layer_params(i)[1:], recv_early[i])):
            w_part = _pack_local(w, params)
            outs = _adamw(w_part, got, _pack_local(mom, params), _pack_local(var, params),
                          tr=_row_tile(math.gcd(w_part.shape[0], off)), g_off=off, name="adamw_big")
            for kind, packed in enumerate(outs):
                for (n, j, _), piece in zip(params, _unpack_local(packed, params, w).values()):
                    big_out[kind][n][j] = piece
    big_out = [{n: jnp.stack(v) for n, v in kind.items()} for kind in big_out]

    small_rows = jnp.concatenate(
        [jnp.concatenate(small[n], axis=0) for n in ("ln_mix_g", "ln_mix_b", "ln_ffn_g", "ln_ffn_b")] + dlb_rows
        + [_pad_row(jnp.stack(dnorm)), _pad_row(jnp.stack(dsink)), jnp.zeros((2, D_MODEL), F32)], axis=0)
    small_all = _gather(small_rows, name="gather_small")
    lbw, lbm, lbv = (t.reshape(4, 2 * HG_DIM) for t in (hgrn_lb_logits, mom["hgrn_lb_logits"], var["hgrn_lb_logits"]))
    summed = _sum8(small_all, name="sum_small")
    dlb_mine = lax.dynamic_slice_in_dim(summed[16:20].reshape(2, 2, HG_HEADS, HG_DIM), me, 1, axis=2)
    dlogits = _lb_bwd(lbw, dlb_mine.reshape(2, 2 * HG_DIM), name="lb_bwd")

    def small_pack(ln4, lbt, ng, sk):
        return jnp.concatenate([ln4[n] for n in ("ln_mix_g", "ln_mix_b", "ln_ffn_g", "ln_ffn_b")]
                               + [_pad_row(lbt), _pad_row(ng), _pad_row(sk), jnp.zeros((5, D_MODEL), F32)], axis=0)

    g_small = jnp.concatenate([summed[:16], _pad_row(dlogits), summed[20:22], jnp.zeros((5, D_MODEL), F32)], axis=0)
    souts = _adamw(small_pack(w, lbw, w["hgrn_norm_g"], w["att_sink"]), g_small,
                   small_pack(mom, lbm, mom["hgrn_norm_g"], mom["att_sink"]),
                   small_pack(var, lbv, var["hgrn_norm_g"], var["att_sink"]), tr=SMALL_ROWS, name="adamw_small")

    def small_unpack(t):
        out = {n: t[4 * k:4 * k + 4] for k, n in enumerate(("ln_mix_g", "ln_mix_b", "ln_ffn_g", "ln_ffn_b"))}
        out["hgrn_lb_logits"] = t[16].reshape(hgrn_lb_logits.shape)
        out["hgrn_norm_g"] = t[17, :2 * HG_DIM].reshape(hgrn_norm_g.shape)
        out["att_sink"] = t[18, :2 * N_Q_HEADS].reshape(att_sink.shape)
        return out

    result = [loss, grad_x]
    for big_t, small_t in zip(big_out, souts):
        merged = dict(big_t)
        merged.update(small_unpack(small_t))
        result += [merged[n] for n in names]
    return tuple(result)
```

```python
import functools
import math

import jax
import jax.numpy as jnp
from jax import lax
from jax.experimental import pallas as pl
from jax.experimental.pallas import tpu as pltpu

F32 = jnp.float32
BF16 = jnp.bfloat16

D_MODEL = 1024
DEPTH = 4
HEAD_DIM = 64
N_Q_HEADS = 16
N_KV_HEADS = 4
GROUP = 4
KV_DIM = 256
ATT_BLOCK = 128
ROPE_DIM = 16
ROPE_THETA = 500000.0
HG_HEADS = 8
HG_DIM = 128
HG_CHUNK = 64
HG_SUB = 16
D_FF = 2816
FF_TILE = 1408
SUB_ROWS = 256
PLE_DIM = 256
ALPHA = (2 * DEPTH) ** 0.25
LN_EPS = 1e-5
ADAM_LR, ADAM_B1, ADAM_B2, ADAM_EPS, ADAM_WD, ADAM_STEP = 0.001, 0.9, 0.999, 1e-08, 0.01, 10

N_DEV = 8
LANES = 128
VMEM_LIMIT = 52 * 1024 * 1024
NEG = -1e30
MESH = pl.DeviceIdType.MESH
AXES = ("x", "y", "c")

BIG = (("att_w_qkv", 2), ("att_w_o", 1), ("hgrn_w_in", 2), ("hgrn_w_o", 1), ("ffn_w_in", 2), ("ffn_w_out", 1),
       ("ple_w_gate", 1), ("ple_w_proj", 2))


def _params(sem=None, vmem=VMEM_LIMIT):
    return pltpu.CompilerParams(dimension_semantics=sem, vmem_limit_bytes=vmem)


def _sigmoid(x):
    return jax.nn.sigmoid(x)


def _direct_copies(src_ref, out_ref, send_sems, recv_sems, local_sem, gather, arrivals):
    x, y, c = lax.axis_index("x"), lax.axis_index("y"), lax.axis_index("c")
    me = 4 * x + 2 * y + c
    mine = (lambda j: src_ref) if gather else (lambda j: src_ref.at[j])
    pairs = []
    for k in range(1, N_DEV):
        px, py, pc = x ^ (k >> 2), y ^ ((k >> 1) & 1), c ^ (k & 1)
        peer = 4 * px + 2 * py + pc
        send = pltpu.make_async_remote_copy(
            src_ref=mine(peer), dst_ref=out_ref.at[me], send_sem=send_sems.at[k], recv_sem=recv_sems.at[k],
            device_id=(px, py, pc), device_id_type=MESH)
        arrival = pltpu.make_async_remote_copy(
            src_ref=mine(peer), dst_ref=out_ref.at[peer], send_sem=send_sems.at[k], recv_sem=recv_sems.at[k],
            device_id=(x, y, c), device_id_type=MESH) if arrivals else None
        pairs.append((send, arrival))
    return pltpu.make_async_copy(mine(me), out_ref.at[me], local_sem), pairs


def _direct_start(*refs, gather):
    local, pairs = _direct_copies(*refs, gather, False)
    local.start()
    for send, _ in pairs:
        send.start()


def _direct_wait(*refs, gather):
    local, pairs = _direct_copies(*refs, gather, True)
    for send, arrival in pairs:
        send.wait_send()
        arrival.wait_recv()
    local.wait()


COMM_SCRATCH = [pltpu.SemaphoreType.DMA((N_DEV,)), pltpu.SemaphoreType.DMA((N_DEV,)), pltpu.SemaphoreType.DMA]


def _exchange(src, *, gather=False, name):
    def body(*refs):
        _direct_start(*refs, gather=gather)
        _direct_wait(*refs, gather=gather)

    blk = tuple(src.shape) if gather else tuple(src.shape[1:])
    return pl.pallas_call(
        body, name=name,
        out_shape=jax.ShapeDtypeStruct((N_DEV,) + blk, src.dtype),
        in_specs=[pl.BlockSpec(memory_space=pltpu.HBM)],
        out_specs=pl.BlockSpec(memory_space=pltpu.HBM),
        scratch_shapes=COMM_SCRATCH,
    )(src)


def _call(body, *, name, grid, out_shape, in_specs, out_specs, args, scratch_shapes=(), sem, comm=None, edge=None):
    out_shape, out_specs = tuple(out_shape), tuple(out_specs)
    if comm is None:
        return pl.pallas_call(body, name=name, grid=grid, out_shape=out_shape, in_specs=list(in_specs),
                              out_specs=out_specs, scratch_shapes=list(scratch_shapes),
                              compiler_params=_params(sem))(*args)
    src, gather = comm
    n_in, n_out, n_scr = len(args), len(out_shape), len(scratch_shapes)
    blk = tuple(src.shape) if gather else tuple(src.shape[1:])
    hbm = pl.BlockSpec(memory_space=pltpu.HBM)

    def carrying(*refs):
        ins, src_ref = refs[:n_in], refs[n_in]
        outs, dst_ref = refs[n_in + 1:n_in + 1 + n_out], refs[n_in + 1 + n_out]
        own = refs[n_in + 2 + n_out:n_in + 2 + n_out + n_scr]
        comm_refs = (src_ref, dst_ref) + tuple(refs[n_in + 2 + n_out + n_scr:])
        first, last = edge()

        @pl.when(first)
        def _():
            _direct_start(*comm_refs, gather=gather)

        body(*ins, *outs, *own)

        @pl.when(last)
        def _():
            _direct_wait(*comm_refs, gather=gather)

    return pl.pallas_call(
        carrying, name=name, grid=grid,
        out_shape=out_shape + (jax.ShapeDtypeStruct((N_DEV,) + blk, src.dtype),),
        in_specs=list(in_specs) + [hbm], out_specs=out_specs + (hbm,),
        scratch_shapes=list(scratch_shapes) + COMM_SCRATCH,
        compiler_params=_params(("arbitrary",) * len(grid)),
    )(*args, src)


def _gather(src, *, name):
    def body(src_ref, out_ref, send_sems, recv_sems, local_sem):
        x, y, c = lax.axis_index("x"), lax.axis_index("y"), lax.axis_index("c")
        sibling = (x, y, 1 - c)
        chips = [(1 - x, y), (x, 1 - y), (1 - x, 1 - y)]

        def rows(px, py, pc):
            return out_ref.at[4 * px + 2 * py + pc]

        def copy(k, block, to, from_src=False):
            return pltpu.make_async_remote_copy(
                src_ref=src_ref if from_src else rows(*block), dst_ref=rows(*block), send_sem=send_sems.at[k],
                recv_sem=recv_sems.at[k], device_id=to, device_id_type=MESH)

        me = (x, y, c)
        mine = pltpu.make_async_copy(src_ref, rows(*me), local_sem)
        mine.start()
        first = [copy(0, me, sibling, from_src=True)]
        first += [copy(1 + j, me, (*chip, c), from_src=True) for j, chip in enumerate(chips)]
        for cp in first:
            cp.start()
        passed = [copy(4 + j, (*chip, c), sibling) for j, chip in enumerate(chips)]
        for j, chip in enumerate(chips):
            copy(1 + j, (*chip, c), me).wait_recv()
            passed[j].start()
        copy(0, sibling, me).wait_recv()
        for j, chip in enumerate(chips):
            copy(4 + j, (*chip, 1 - c), me).wait_recv()
        for cp in first + passed:
            cp.wait_send()
        mine.wait()

    return pl.pallas_call(
        body, name=name,
        out_shape=jax.ShapeDtypeStruct((N_DEV,) + tuple(src.shape), src.dtype),
        in_specs=[pl.BlockSpec(memory_space=pltpu.HBM)],
        out_specs=pl.BlockSpec(memory_space=pltpu.HBM),
        scratch_shapes=[pltpu.SemaphoreType.DMA((7,)), pltpu.SemaphoreType.DMA((7,)), pltpu.SemaphoreType.DMA],
    )(src)


def _mm(a, b, *, tm, tn, tk, ta=False, tb=False, out_dtype=F32, add=None, add_scale=1.0, ln=None, name):
    a_list = list(a) if isinstance(a, (list, tuple)) else [a]
    b_list = list(b) if isinstance(b, (list, tuple)) else [b]
    assert not (ta and len(a_list) > 1) and not (tb and len(b_list) > 1)
    m = a_list[0].shape[1] if ta else a_list[0].shape[0]
    koff = [0]
    for piece in a_list:
        koff.append(koff[-1] + (piece.shape[0] if ta else piece.shape[1]) // tk)
    joff = [0]
    for piece in b_list:
        joff.append(joff[-1] + (piece.shape[0] if tb else piece.shape[1]) // tn)
    nk, n = koff[-1], joff[-1] * tn
    dims = (((0 if ta else 1,), (1 if tb else 0,)), ((), ()))

    def within(idx, off, p, count):
        return None if count == 1 else (idx >= off[p]) & (idx < off[p + 1])

    def body(*refs):
        a_refs, b_refs = refs[:len(a_list)], refs[len(a_list):len(a_list) + len(b_list)]
        nxt = len(a_list) + len(b_list)
        add_ref = None
        if add is not None:
            add_ref = refs[nxt]
            nxt += 1
        if ln is not None:
            pre_ref, g_ref = refs[nxt:nxt + 2]
            nxt += 2
        outs = refs[nxt:nxt + (4 if ln is not None else 1)]
        acc_ref = refs[-1] if nk > 1 else None
        j, k = pl.program_id(1), pl.program_id(2)

        def finish(r):
            if add_ref is not None:
                r = r + add_scale * add_ref[...]
            if ln is None:
                outs[0][...] = r.astype(out_dtype)
            else:
                _ln_bwd_store(r, pre_ref[...], g_ref[...], *outs, first=pl.program_id(0) == 0)

        for pa, a_ref in enumerate(a_refs):
            for pb, b_ref in enumerate(b_refs):
                def step(a_ref=a_ref, b_ref=b_ref, pa=pa):
                    part = lax.dot_general(a_ref[...].astype(BF16), b_ref[...].astype(BF16), dims,
                                           preferred_element_type=F32)
                    if nk == 1:
                        finish(part)
                    elif pa == 0:
                        _accumulate(acc_ref, part, k == 0)
                    else:
                        acc_ref[...] += part

                conds = [c for c in (within(k, koff, pa, len(a_list)), within(j, joff, pb, len(b_list))) if c is not None]
                if conds:
                    pl.when(functools.reduce(jnp.logical_and, conds))(step)
                else:
                    step()
        if nk > 1:
            @pl.when(k == nk - 1)
            def _():
                finish(acc_ref[...])

    def a_spec(p):
        kk = lambda k: jnp.clip(k - koff[p], 0, koff[p + 1] - koff[p] - 1)
        return (pl.BlockSpec((tk, tm), lambda i, j, k: (kk(k), i)) if ta
                else pl.BlockSpec((tm, tk), lambda i, j, k: (i, kk(k))))

    def b_spec(p):
        jj = lambda j: jnp.clip(j - joff[p], 0, joff[p + 1] - joff[p] - 1)
        kk = (lambda j, k: k) if len(b_list) == 1 else (lambda j, k: jnp.where((j >= joff[p]) & (j < joff[p + 1]), k, 0))
        return (pl.BlockSpec((tn, tk), lambda i, j, k: (jj(j), kk(j, k))) if tb
                else pl.BlockSpec((tk, tn), lambda i, j, k: (kk(j, k), jj(j))))

    in_specs = [a_spec(p) for p in range(len(a_list))] + [b_spec(p) for p in range(len(b_list))]
    args = a_list + b_list
    tile = pl.BlockSpec((tm, tn), lambda i, j, k: (i, j))
    if add is not None:
        in_specs.append(tile)
        args.append(add)
    out_shape, out_specs = jax.ShapeDtypeStruct((m, n), out_dtype), tile
    if ln is not None:
        assert tn == n == D_MODEL
        vec = pl.BlockSpec((1, D_MODEL), lambda i, j, k: (0, 0))
        in_specs += [tile, vec]
        args += list(ln)
        out_shape = (jax.ShapeDtypeStruct((m, n), F32), jax.ShapeDtypeStruct((m, n), BF16),
                     jax.ShapeDtypeStruct((1, D_MODEL), F32), jax.ShapeDtypeStruct((1, D_MODEL), F32))
        out_specs = (tile, tile, vec, vec)
    return pl.pallas_call(
        body, name=name, grid=(m // tm, n // tn, nk),
        out_shape=out_shape, in_specs=in_specs, out_specs=out_specs,
        scratch_shapes=[pltpu.VMEM((tm, tn), F32)] if nk > 1 else [],
        compiler_params=_params(("arbitrary",) * 3 if ln is not None else ("parallel", "parallel", "arbitrary")),
    )(*args)


def _ln_bwd_rows(do, y, g):
    mu = jnp.mean(y, axis=-1, keepdims=True)
    yc = y - mu
    var = jnp.mean(yc * yc, axis=-1, keepdims=True)
    rstd = lax.rsqrt(var + LN_EPS)
    xhat = yc * rstd
    dxhat = do * g
    dy = rstd * (dxhat - jnp.mean(dxhat, axis=-1, keepdims=True) - xhat * jnp.mean(dxhat * xhat, axis=-1, keepdims=True))
    return dy, jnp.sum(do * xhat, axis=0, keepdims=True), jnp.sum(do, axis=0, keepdims=True)


def _accumulate(ref, val, first):
    @pl.when(first)
    def _():
        ref[...] = val

    @pl.when(jnp.logical_not(first))
    def _():
        ref[...] += val


def _ln_bwd_store(do, y, g, dy_ref, dybf_ref, dg_ref, db_ref, *, first):
    dy, pg, pb = _ln_bwd_rows(do, y, g)
    dy_ref[...] = dy
    dybf_ref[...] = dy.astype(BF16)
    _accumulate(dg_ref, pg, first)
    _accumulate(db_ref, pb, first)


def _layer_norm_rows(y, g, b):
    mu = jnp.mean(y, axis=-1, keepdims=True)
    yc = y - mu
    var = jnp.mean(yc * yc, axis=-1, keepdims=True)
    return yc * lax.rsqrt(var + LN_EPS) * g + b


def _proj_ln(a, w, res, g, b, *, tm, name):
    s, kdim = a.shape

    def body(a_ref, w_ref, res_ref, g_ref, b_ref, pre_ref, o_ref, obf_ref):
        for rs in _row_parts(tm):
            h = jnp.dot(a_ref[rs, :], w_ref[...], preferred_element_type=F32)
            pre = ALPHA * res_ref[rs, :] + h
            out = _layer_norm_rows(pre, g_ref[...], b_ref[...])
            pre_ref[rs, :] = pre
            o_ref[rs, :] = out
            obf_ref[rs, :] = out.astype(BF16)

    row = lambda i: (i, 0)
    fix = lambda i: (0, 0)
    return pl.pallas_call(
        body, name=name, grid=(s // tm,),
        out_shape=(jax.ShapeDtypeStruct((s, D_MODEL), F32), jax.ShapeDtypeStruct((s, D_MODEL), F32),
                   jax.ShapeDtypeStruct((s, D_MODEL), BF16)),
        in_specs=[pl.BlockSpec((tm, kdim), row), pl.BlockSpec((kdim, D_MODEL), fix), pl.BlockSpec((tm, D_MODEL), row),
                  pl.BlockSpec((1, D_MODEL), fix), pl.BlockSpec((1, D_MODEL), fix)],
        out_specs=(pl.BlockSpec((tm, D_MODEL), row),) * 3,
        compiler_params=_params(("parallel",)),
    )(a, w, res, g, b)


def _row_parts(tm):
    sub = min(tm, SUB_ROWS)
    return [pl.ds(r * sub, sub) for r in range(tm // sub)]


def _ffn_in(xbf, w, *, tm, tn, name):
    s = xbf.shape[0]
    nj = D_FF // tn

    def body(x_ref, wg_ref, wu_ref, g_ref, u_ref, act_ref):
        for rs in _row_parts(tm):
            xv = x_ref[rs, :]
            gg = jnp.dot(xv, wg_ref[...], preferred_element_type=F32)
            uu = jnp.dot(xv, wu_ref[...], preferred_element_type=F32)
            g_ref[rs, :] = gg.astype(BF16)
            u_ref[rs, :] = uu.astype(BF16)
            act_ref[rs, :] = (gg * _sigmoid(gg) * uu).astype(BF16)

    out = jax.ShapeDtypeStruct((s, D_FF), BF16)
    tile = pl.BlockSpec((tm, tn), lambda j, i: (i, j))
    return pl.pallas_call(
        body, name=name, grid=(nj, s // tm),
        out_shape=(out, out, out),
        in_specs=[pl.BlockSpec((tm, D_MODEL), lambda j, i: (i, 0)), pl.BlockSpec((D_MODEL, tn), lambda j, i: (0, j)),
                  pl.BlockSpec((D_MODEL, tn), lambda j, i: (0, j + nj))],
        out_specs=(tile, tile, tile),
        compiler_params=_params(("parallel", "parallel")),
    )(xbf, w, w)


def _ffn_bwd_act(dybf, w_out, g, u, *, tm, tn, name):
    s = dybf.shape[0]

    def body(dy_ref, w_ref, g_ref, u_ref, dg_ref, du_ref):
        for rs in _row_parts(tm):
            dact = lax.dot_general(dy_ref[rs, :], w_ref[...], (((1,), (1,)), ((), ())), preferred_element_type=F32)
            gg = g_ref[rs, :].astype(F32)
            uu = u_ref[rs, :].astype(F32)
            sg = _sigmoid(gg)
            dg_ref[rs, :] = (dact * uu * sg * (1.0 + gg * (1.0 - sg))).astype(BF16)
            du_ref[rs, :] = (dact * gg * sg).astype(BF16)

    out = jax.ShapeDtypeStruct((s, D_FF), BF16)
    tile = pl.BlockSpec((tm, tn), lambda j, i: (i, j))
    return pl.pallas_call(
        body, name=name, grid=(D_FF // tn, s // tm),
        out_shape=(out, out),
        in_specs=[pl.BlockSpec((tm, D_MODEL), lambda j, i: (i, 0)), pl.BlockSpec((tn, D_MODEL), lambda j, i: (j, 0)),
                  tile, tile],
        out_specs=(tile, tile),
        compiler_params=_params(("parallel", "parallel")),
    )(dybf, w_out, g, u)


def _ffn_out_ple(act, w_out, res, g, b, p, w_gate, w_proj, *, tm, name):
    s = act.shape[0]

    def body(a_ref, w_ref, res_ref, g_ref, b_ref, p_ref, wg_ref, wp_ref, pre_ref, x2bf_ref, o_ref, obf_ref):
        for rs in _row_parts(tm):
            pre = ALPHA * res_ref[rs, :] + jnp.dot(a_ref[rs, :], w_ref[...], preferred_element_type=F32)
            x2 = _layer_norm_rows(pre, g_ref[...], b_ref[...])
            x2bf = x2.astype(BF16)
            pre_ref[rs, :] = pre
            x2bf_ref[rs, :] = x2bf
            gate = jnp.dot(x2bf, wg_ref[...], preferred_element_type=F32)
            pp = jnp.dot(p_ref[rs, :].astype(BF16), wp_ref[...], preferred_element_type=F32)
            out = x2 + _sigmoid(gate) * pp
            o_ref[rs, :] = out
            obf_ref[rs, :] = out.astype(BF16)

    row = lambda i: (i, 0)
    fix = lambda i: (0, 0)
    tile = pl.BlockSpec((tm, D_MODEL), row)
    vec = pl.BlockSpec((1, D_MODEL), fix)
    act_t = lambda dt: jax.ShapeDtypeStruct((s, D_MODEL), dt)
    return pl.pallas_call(
        body, name=name, grid=(s // tm,),
        out_shape=(act_t(F32), act_t(BF16), act_t(F32), act_t(BF16)),
        in_specs=[pl.BlockSpec((tm, D_FF), row), pl.BlockSpec((D_FF, D_MODEL), fix), tile, vec, vec,
                  pl.BlockSpec((tm, PLE_DIM), row), pl.BlockSpec((D_MODEL, D_MODEL), fix),
                  pl.BlockSpec((PLE_DIM, D_MODEL), fix)],
        out_specs=(tile, tile, tile, tile),
        compiler_params=_params(("parallel",)),
    )(act, w_out, res, g, b, p, w_gate, w_proj)


def _ple_ln_bwd(dx3, x2bf, p, w_gate, w_proj, pre, g, *, tm, name):
    s = dx3.shape[0]

    def body(d_ref, xbf_ref, p_ref, wg_ref, wp_ref, pre_ref, g_ref, da_ref, dpp_ref, dy_ref, dybf_ref, dg_ref, db_ref):
        pg = jnp.zeros((1, D_MODEL), F32)
        pb = jnp.zeros((1, D_MODEL), F32)
        for rs in _row_parts(tm):
            d = d_ref[rs, :]
            a = jnp.dot(xbf_ref[rs, :], wg_ref[...], preferred_element_type=F32)
            pp = jnp.dot(p_ref[rs, :].astype(BF16), wp_ref[...], preferred_element_type=F32)
            sg = _sigmoid(a)
            da = (d * pp * sg * (1.0 - sg)).astype(BF16)
            da_ref[rs, :] = da
            dpp_ref[rs, :] = (d * sg).astype(BF16)
            dx2 = d + lax.dot_general(da, wg_ref[...], (((1,), (1,)), ((), ())), preferred_element_type=F32)
            dy, qg, qb = _ln_bwd_rows(dx2, pre_ref[rs, :], g_ref[...])
            dy_ref[rs, :] = dy
            dybf_ref[rs, :] = dy.astype(BF16)
            pg, pb = pg + qg, pb + qb
        _accumulate(dg_ref, pg, pl.program_id(0) == 0)
        _accumulate(db_ref, pb, pl.program_id(0) == 0)

    row = lambda i: (i, 0)
    fix = lambda i: (0, 0)
    tile = pl.BlockSpec((tm, D_MODEL), row)
    vec = pl.BlockSpec((1, D_MODEL), fix)
    act = lambda dt: jax.ShapeDtypeStruct((s, D_MODEL), dt)
    return pl.pallas_call(
        body, name=name, grid=(s // tm,),
        out_shape=(act(BF16), act(BF16), act(F32), act(BF16), jax.ShapeDtypeStruct((1, D_MODEL), F32),
                   jax.ShapeDtypeStruct((1, D_MODEL), F32)),
        in_specs=[tile, tile, pl.BlockSpec((tm, PLE_DIM), row), pl.BlockSpec((D_MODEL, D_MODEL), fix),
                  pl.BlockSpec((PLE_DIM, D_MODEL), fix), tile, vec],
        out_specs=(tile, tile, tile, tile, vec, vec),
        compiler_params=_params(("arbitrary",)),
    )(dx3, x2bf, p, w_gate, w_proj, pre, g)


def _loss_head(y, target, *, tm, name):
    s = y.shape[0]

    def body(y_ref, t_ref, dy_ref, loss_ref, acc_ref):
        err = y_ref[...] - t_ref[...]
        dy_ref[...] = err * (1.0 / D_MODEL)
        part = jnp.sum(err * err, axis=0, keepdims=True)

        @pl.when(pl.program_id(0) == 0)
        def _():
            acc_ref[...] = part

        @pl.when(pl.program_id(0) > 0)
        def _():
            acc_ref[...] += part

        @pl.when(pl.program_id(0) == pl.num_programs(0) - 1)
        def _():
            tot = jnp.sum(acc_ref[...], axis=1, keepdims=True) * (0.5 / D_MODEL)
            loss_ref[...] = jnp.broadcast_to(tot, (8, LANES))

    row = lambda i: (i, 0)
    return pl.pallas_call(
        body, name=name, grid=(s // tm,),
        out_shape=(jax.ShapeDtypeStruct((s, D_MODEL), F32), jax.ShapeDtypeStruct((8, LANES), F32)),
        in_specs=[pl.BlockSpec((tm, D_MODEL), row), pl.BlockSpec((tm, D_MODEL), row)],
        out_specs=(pl.BlockSpec((tm, D_MODEL), row), pl.BlockSpec((8, LANES), lambda i: (0, 0))),
        scratch_shapes=[pltpu.VMEM((1, D_MODEL), F32)],
        compiler_params=_params(("arbitrary",)),
    )(y, target)


def _rope_tables(s):
    inv = ROPE_THETA ** (-jnp.arange(0, ROPE_DIM, 2, dtype=F32) / ROPE_DIM)
    ang = jnp.arange(s, dtype=F32)[:, None] * inv[None, :]
    cos, sin = jnp.cos(ang), jnp.sin(ang)
    ones = jnp.ones((s, HEAD_DIM - ROPE_DIM), F32)
    c_head = jnp.concatenate([cos, cos, ones], axis=1)
    s_head = jnp.concatenate([-sin, sin, 0.0 * ones], axis=1)
    return jnp.concatenate([c_head, c_head], axis=1), jnp.concatenate([s_head, s_head], axis=1)


def _rope(v, cos, sin):
    n = v.shape[1] // LANES
    width = v.shape[1]
    cos_w = jnp.tile(cos, (1, n)) if n > 1 else cos
    sin_w = jnp.tile(sin, (1, n)) if n > 1 else sin
    dim = lax.broadcasted_iota(jnp.int32, (1, width), 1) % HEAD_DIM
    partner = jnp.where(dim < ROPE_DIM // 2, pltpu.roll(v, width - ROPE_DIM // 2, 1), pltpu.roll(v, ROPE_DIM // 2, 1))
    return v * cos_w + partner * sin_w


def _unrope(dv, cos, sin):
    n = dv.shape[1] // LANES
    width = dv.shape[1]
    cos_w = jnp.tile(cos, (1, n)) if n > 1 else cos
    sin_w = jnp.tile(sin, (1, n)) if n > 1 else sin
    t = dv * sin_w
    dim = lax.broadcasted_iota(jnp.int32, (1, width), 1) % HEAD_DIM
    partner = jnp.where(dim < ROPE_DIM // 2, pltpu.roll(t, width - ROPE_DIM // 2, 1),
                        jnp.where(dim < ROPE_DIM, pltpu.roll(t, ROPE_DIM // 2, 1), 0.0))
    return dv * cos_w + partner


def _att_mask(i, nb):
    rows = GROUP * ATT_BLOCK
    r = lax.broadcasted_iota(jnp.int32, (rows, 3 * ATT_BLOCK), 0) % ATT_BLOCK
    cidx = lax.broadcasted_iota(jnp.int32, (rows, 3 * ATT_BLOCK), 1)
    rel = r + ATT_BLOCK - cidx
    ok = (rel <= ATT_BLOCK) & (rel >= -ATT_BLOCK)
    ok = ok & ((cidx >= ATT_BLOCK) | (i > 0)) & ((cidx < 2 * ATT_BLOCK) | (i < nb - 1))
    return ok


def _att_mask_t(i, nb):
    cols = GROUP * ATT_BLOCK
    cidx = lax.broadcasted_iota(jnp.int32, (3 * ATT_BLOCK, cols), 0)
    r = lax.broadcasted_iota(jnp.int32, (3 * ATT_BLOCK, cols), 1) % ATT_BLOCK
    rel = r + ATT_BLOCK - cidx
    ok = (rel <= ATT_BLOCK) & (rel >= -ATT_BLOCK)
    return ok & ((cidx >= ATT_BLOCK) | (i > 0)) & ((cidx < 2 * ATT_BLOCK) | (i < nb - 1))


def _sink_lanes(sink_ref, h):
    cols = GROUP * ATT_BLOCK
    grp = lax.broadcasted_iota(jnp.int32, (1, cols), 1) // ATT_BLOCK
    out = jnp.zeros((1, cols), F32)
    for gq in range(GROUP):
        out = jnp.where(grp == gq, sink_ref[GROUP * h + gq], out)
    return out


def _half_mask(half):
    lane = lax.broadcasted_iota(jnp.int32, (1, LANES), 1)
    return (lane // HEAD_DIM) == half


def _stack_q(q, h):
    parts = []
    for gq in range(GROUP):
        n = GROUP * h + gq
        grp = q[:, LANES * (n // 2):LANES * (n // 2 + 1)]
        grp = jnp.where(_half_mask(n % 2), grp, 0.0)
        if n % 2 != h % 2:
            grp = pltpu.roll(grp, HEAD_DIM, 1)
        parts.append(grp)
    return jnp.concatenate(parts, axis=0)


def _unstack_q(stacked, h, acc):
    for gq in range(GROUP):
        n = GROUP * h + gq
        grp = stacked[ATT_BLOCK * gq:ATT_BLOCK * (gq + 1), :]
        grp = jnp.where(_half_mask(h % 2), grp, 0.0)
        if n % 2 != h % 2:
            grp = pltpu.roll(grp, HEAD_DIM, 1)
        acc[n // 2] = grp if acc[n // 2] is None else acc[n // 2] + grp
    return acc


def _sink_rows(sink_ref, h):
    rows = GROUP * ATT_BLOCK
    grp = lax.broadcasted_iota(jnp.int32, (rows, 1), 0) // ATT_BLOCK
    out = jnp.zeros((rows, 1), F32)
    for gq in range(GROUP):
        out = jnp.where(grp == gq, sink_ref[GROUP * h + gq], out)
    return out


def _att_probs(qs, kh, sink, valid):
    s = lax.dot_general(qs, kh, (((1,), (1,)), ((), ())), preferred_element_type=F32)
    s = jnp.where(valid, s, NEG)
    m = jnp.maximum(jnp.max(s, axis=-1, keepdims=True), sink)
    p = jnp.exp(s - m)
    es = jnp.exp(sink - m)
    den = jnp.sum(p, axis=-1, keepdims=True) + es
    inv = 1.0 / den
    return p * inv, es * inv


def _att_specs(nb):
    prev = lambda i: (jnp.maximum(i - 1, 0), 0)
    cur = lambda i: (i, 0)
    nxt = lambda i: (jnp.minimum(i + 1, nb - 1), 0)
    kv = lambda f: (lambda i: (f(i)[0], 2))
    tab = [pl.BlockSpec((ATT_BLOCK, LANES), f) for f in (cur, prev, cur, nxt)]
    z = [pl.BlockSpec((ATT_BLOCK, D_MODEL), cur)] + [pl.BlockSpec((ATT_BLOCK, 2 * KV_DIM), kv(f)) for f in (prev, cur, nxt)]
    return z, tab


def _att_load(zq_ref, kp_ref, kc_ref, kn_ref, cq_ref, sq_ref, cp_ref, sp_ref, cc_ref, sc_ref, cn_ref, sn_ref):
    q = (_rope(zq_ref[...], cq_ref[...], sq_ref[...]) * (HEAD_DIM ** -0.5))
    ks, vs = [], []
    for ref, c_ref, s_ref in ((kp_ref, cp_ref, sp_ref), (kc_ref, cc_ref, sc_ref), (kn_ref, cn_ref, sn_ref)):
        kvb = ref[...]
        ks.append(_rope(kvb[:, :KV_DIM], c_ref[...], s_ref[...]))
        vs.append(kvb[:, KV_DIM:])
    return q, jnp.concatenate(ks, axis=0).astype(BF16), jnp.concatenate(vs, axis=0).astype(BF16)


def _att_fwd(z, sink, cos, sin, *, comm=None, name):
    s = z.shape[0]
    nb = s // ATT_BLOCK

    def body(zq_ref, kp_ref, kc_ref, kn_ref, cq_ref, cp_ref, cc_ref, cn_ref, sq_ref, sp_ref, sc_ref, sn_ref, sink_ref,
             o_ref):
        i = pl.program_id(0)
        q, k, v = _att_load(zq_ref, kp_ref, kc_ref, kn_ref, cq_ref, sq_ref, cp_ref, sp_ref, cc_ref, sc_ref, cn_ref, sn_ref)
        valid = _att_mask(i, nb)
        acc = [None] * (N_Q_HEADS // 2)
        for h in range(N_KV_HEADS):
            lanes = slice(LANES * (h // 2), LANES * (h // 2 + 1))
            qs = _stack_q(q, h).astype(BF16)
            prob, _ = _att_probs(qs, k[:, lanes], _sink_rows(sink_ref, h), valid)
            oh = jnp.dot(prob.astype(BF16), v[:, lanes], preferred_element_type=F32)
            acc = _unstack_q(oh, h, acc)
        o_ref[...] = jnp.concatenate(acc, axis=1).astype(BF16)

    zspecs, tab = _att_specs(nb)
    return _call(
        body, name=name, grid=(nb,),
        out_shape=(jax.ShapeDtypeStruct((s, D_MODEL), BF16),),
        in_specs=zspecs + tab + tab + [pl.BlockSpec(memory_space=pltpu.SMEM)],
        out_specs=(pl.BlockSpec((ATT_BLOCK, D_MODEL), lambda i: (i, 0)),),
        args=(z, z, z, z, cos, cos, cos, cos, sin, sin, sin, sin, sink), sem=("parallel",), comm=comm,
        edge=lambda: (pl.program_id(0) == 0, pl.program_id(0) == nb - 1))


def _att_bwd(z, do, sink, cos, sin, *, comm=None, name):
    s = z.shape[0]
    nb = s // ATT_BLOCK

    def body(zq_ref, kp_ref, kc_ref, kn_ref, cq_ref, cp_ref, cc_ref, cn_ref, sq_ref, sp_ref, sc_ref, sn_ref, sink_ref,
             do_ref, dq_ref, part_ref, dsink_ref):
        i = pl.program_id(0)
        q, k, v = _att_load(zq_ref, kp_ref, kc_ref, kn_ref, cq_ref, sq_ref, cp_ref, sp_ref, cc_ref, sc_ref, cn_ref, sn_ref)
        valid = _att_mask_t(i, nb)
        dout = do_ref[...].astype(F32)
        dq_acc = [None] * (N_Q_HEADS // 2)
        dk_acc = [None] * 2
        dv_acc = [None] * 2
        rows = []
        nt = (((1,), (1,)), ((), ()))
        for h in range(N_KV_HEADS):
            grp = h // 2
            lanes = slice(LANES * grp, LANES * (grp + 1))
            qs = _stack_q(q, h).astype(BF16)
            dos = _stack_q(dout, h).astype(BF16)
            sink = _sink_lanes(sink_ref, h)
            sc = jnp.where(valid, lax.dot_general(k[:, lanes], qs, nt, preferred_element_type=F32), NEG)
            m = jnp.maximum(jnp.max(sc, axis=0, keepdims=True), sink)
            p = jnp.exp(sc - m)
            es = jnp.exp(sink - m)
            inv = 1.0 / (jnp.sum(p, axis=0, keepdims=True) + es)
            prob = p * inv
            dprob = lax.dot_general(v[:, lanes], dos, nt, preferred_element_type=F32)
            delta = jnp.sum(prob * dprob, axis=0, keepdims=True)
            dsc = (prob * (dprob - delta)).astype(BF16)
            dsk = -(es * inv) * delta
            for gq in range(GROUP):
                tot = jnp.sum(dsk[:, ATT_BLOCK * gq:ATT_BLOCK * (gq + 1)], axis=1, keepdims=True)
                rows.append(jnp.broadcast_to(tot, (1, LANES)))
            dqs = lax.dot_general(dsc, k[:, lanes], (((0,), (0,)), ((), ())), preferred_element_type=F32)
            dq_acc = _unstack_q(dqs, h, dq_acc)
            dkh = jnp.dot(dsc, qs, preferred_element_type=F32)
            dvh = jnp.dot(prob.astype(BF16), dos, preferred_element_type=F32)
            dk_acc[grp] = dkh if dk_acc[grp] is None else dk_acc[grp] + dkh
            dv_acc[grp] = dvh if dv_acc[grp] is None else dv_acc[grp] + dvh
        dq = jnp.concatenate(dq_acc, axis=1) * (HEAD_DIM ** -0.5)
        dq_ref[...] = _unrope(dq, cq_ref[...], sq_ref[...]).astype(BF16)
        part = jnp.concatenate(dk_acc + dv_acc, axis=1)
        for wdw in range(3):
            part_ref[wdw] = part[ATT_BLOCK * wdw:ATT_BLOCK * (wdw + 1), :]
        dsink = jnp.concatenate(rows, axis=0)

        @pl.when(i == 0)
        def _():
            dsink_ref[...] = dsink

        @pl.when(i > 0)
        def _():
            dsink_ref[...] += dsink

    zspecs, tab = _att_specs(nb)
    return _call(
        body, name=name, grid=(nb,),
        out_shape=(jax.ShapeDtypeStruct((s, D_MODEL), BF16), jax.ShapeDtypeStruct((nb, 3, ATT_BLOCK, 2 * KV_DIM), F32),
                   jax.ShapeDtypeStruct((N_Q_HEADS, LANES), F32)),
        in_specs=zspecs + tab + tab + [pl.BlockSpec(memory_space=pltpu.SMEM), pl.BlockSpec((ATT_BLOCK, D_MODEL), lambda i: (i, 0))],
        out_specs=(pl.BlockSpec((ATT_BLOCK, D_MODEL), lambda i: (i, 0)),
                   pl.BlockSpec((None, 3, ATT_BLOCK, 2 * KV_DIM), lambda i: (i, 0, 0, 0)),
                   pl.BlockSpec((N_Q_HEADS, LANES), lambda i: (0, 0))),
        args=(z, z, z, z, cos, cos, cos, cos, sin, sin, sin, sin, sink, do), sem=("arbitrary",), comm=comm,
        edge=lambda: (pl.program_id(0) == 0, pl.program_id(0) == nb - 1))


def _att_bwd_kv(part, cos, sin, *, name):
    nb = part.shape[0]

    def body(pn_ref, pc_ref, pp_ref, c_ref, s_ref, o_ref):
        j = pl.program_id(0)
        tot = pc_ref[...]
        tot = tot + jnp.where(j < nb - 1, pn_ref[...], 0.0)
        tot = tot + jnp.where(j > 0, pp_ref[...], 0.0)
        dk = _unrope(tot[:, :KV_DIM], c_ref[...], s_ref[...])
        o_ref[...] = jnp.concatenate([dk, tot[:, KV_DIM:]], axis=1).astype(BF16)

    blk = (None, None, ATT_BLOCK, 2 * KV_DIM)
    return pl.pallas_call(
        body, name=name, grid=(nb,),
        out_shape=jax.ShapeDtypeStruct((nb * ATT_BLOCK, 2 * KV_DIM), BF16),
        in_specs=[pl.BlockSpec(blk, lambda j: (jnp.minimum(j + 1, nb - 1), 0, 0, 0)),
                  pl.BlockSpec(blk, lambda j: (j, 1, 0, 0)),
                  pl.BlockSpec(blk, lambda j: (jnp.maximum(j - 1, 0), 2, 0, 0)),
                  pl.BlockSpec((ATT_BLOCK, LANES), lambda j: (j, 0)), pl.BlockSpec((ATT_BLOCK, LANES), lambda j: (j, 0))],
        out_specs=pl.BlockSpec((ATT_BLOCK, 2 * KV_DIM), lambda j: (j, 0)),
        compiler_params=_params(("parallel",)),
    )(part, part, part, cos, sin)


def _bdot(a, b, dims):
    return lax.dot_general(a.astype(BF16), b.astype(BF16), (dims, ((), ())), preferred_element_type=F32)


@jax.custom_vjp
def _dot_nn(a, b):
    return _bdot(a, b, ((1,), (0,)))


@jax.custom_vjp
def _dot_nt(a, b):
    return _bdot(a, b, ((1,), (1,)))


@jax.custom_vjp
def _dot_tn(a, b):
    return _bdot(a, b, ((0,), (0,)))


_dot_nn.defvjp(lambda a, b: (_dot_nn(a, b), (a, b)), lambda r, d: (_dot_nt(d, r[1]), _dot_tn(r[0], d)))
_dot_nt.defvjp(lambda a, b: (_dot_nt(a, b), (a, b)), lambda r, d: (_dot_nn(d, r[1]), _dot_tn(d, r[0])))
_dot_tn.defvjp(lambda a, b: (_dot_tn(a, b), (a, b)), lambda r, d: (_dot_nt(r[1], d), _dot_nn(r[0], d)))


def _running_sum(v, up):
    n = v.shape[0]
    rows = lax.broadcasted_iota(jnp.int32, v.shape, 0)
    sh = 1
    while sh < n:
        if up:
            v = v + jnp.where(rows < n - sh, pltpu.roll(v, n - sh, 0), 0.0)
        else:
            v = v + jnp.where(rows >= sh, pltpu.roll(v, sh, 0), 0.0)
        sh *= 2
    return v


@jax.custom_vjp
def _sum_down(v):
    return _running_sum(v, False)


@jax.custom_vjp
def _sum_up(v):
    return _running_sum(v, True)


_sum_down.defvjp(lambda v: (_running_sum(v, False), None), lambda _, d: (_sum_up(d),))
_sum_up.defvjp(lambda v: (_running_sum(v, True), None), lambda _, d: (_sum_down(d),))

N_SUB = HG_CHUNK // HG_SUB


def _fold_blocks(v):
    out = v[:HG_CHUNK]
    for i in range(1, N_SUB):
        out = out + v[HG_CHUNK * i:HG_CHUNK * (i + 1)]
    return out


@jax.custom_vjp
def _fold(v):
    return _fold_blocks(v)


_fold.defvjp(lambda v: (_fold_blocks(v), None), lambda _, d: (jnp.concatenate([d] * N_SUB, axis=0),))


def _hg_consts(rev):
    c, sub = HG_CHUNK, HG_SUB
    rowpos = lax.broadcasted_iota(jnp.int32, (c, HG_DIM), 0)
    rr = lax.broadcasted_iota(jnp.int32, (N_SUB * c, c), 0)
    key = lax.broadcasted_iota(jnp.int32, (N_SUB * c, c), 1)
    blk, qry = rr // c, rr % c
    if rev:
        rowpos, qry, key = c - 1 - rowpos, c - 1 - qry, c - 1 - key
    keep = (key // sub == blk) & (key <= qry)
    return keep, rowpos


def _pick(b, rowpos, t):
    return jnp.sum(jnp.where(rowpos == t, b, 0.0), axis=0, keepdims=True)


def _hg_local(zq, zf, zv, lbv, consts, dots):
    dot_nn, dot_nt, dot_tn, cum, fold = dots
    keep, rowpos = consts
    sig = _sigmoid(zf)
    f = lbv + (1.0 - lbv) * sig
    g = jnp.log(f)
    k = (1.0 - lbv) * (1.0 - sig)
    q = zq * _sigmoid(zq)
    b = cum(g)
    ends = [_pick(b, rowpos, (j + 1) * HG_SUB - 1) for j in range(N_SUB)]
    b_last = ends[-1]
    b_end = b_last
    for j in range(N_SUB - 1):
        b_end = jnp.where(rowpos // HG_SUB == j, ends[j], b_end)
    kc = k * jnp.exp(b_end - b)
    qbs = [q * jnp.exp(jnp.where(rowpos >= j * HG_SUB, b - ends[j], 0.0)) for j in range(N_SUB)]
    scores = fold(jnp.where(keep, dot_nt(jnp.concatenate(qbs, axis=0), kc), 0.0))
    return dot_nn(scores, zv), q * jnp.exp(b), k * jnp.exp(b_last - b), jnp.exp(b_last)


def _hg_chunk(zq, zf, zv, lbv, st, consts, dots):
    intra, qs, kd, dec = _hg_local(zq, zf, zv, lbv, consts, dots)
    return intra + dots[1](qs, st), dec * st + dots[2](zv, kd)


def _hg_dots(diff, rev):
    if diff:
        return _dot_nn, _dot_nt, _dot_tn, (_sum_up if rev else _sum_down), _fold
    return (lambda a, b: _bdot(a, b, ((1,), (0,))), lambda a, b: _bdot(a, b, ((1,), (1,))),
            lambda a, b: _bdot(a, b, ((0,), (0,))), lambda v: _running_sum(v, rev), _fold_blocks)


def _hg_specs(ts, nch, trow):
    tile = pl.BlockSpec((ts, HG_DIM), lambda h, t: (trow(t), h))
    mats = pl.BlockSpec((None, nch, HG_DIM, HG_DIM), lambda h, t: (h, trow(t), 0, 0))
    vecs = pl.BlockSpec((None, nch, 1, HG_DIM), lambda h, t: (h, trow(t), 0, 0))
    return tile, mats, vecs


def _time_order(nch, rev):
    return range(nch - 1, -1, -1) if rev else range(nch)


def _chunk_rows(c):
    return pl.ds(c * HG_CHUNK, HG_CHUNK)


def _hg_edge(nt):
    h, t = pl.program_id(0), pl.program_id(1)
    return (h == 0) & (t == 0), (h == HG_HEADS - 1) & (t == nt - 1)


def _hg_fwd(z, lb, *, rev, ts, comm=None, name):
    s = z.shape[0]
    nt = s // ts
    nch = ts // HG_CHUNK
    fcol = HG_HEADS * (2 if rev else 1)

    def body(zq_ref, zf_ref, zv_ref, lb_ref, o_ref, st_ref, qs_ref, dec_ref, state_ref):
        @pl.when(pl.program_id(1) == 0)
        def _():
            state_ref[...] = jnp.zeros_like(state_ref)

        consts = _hg_consts(rev)
        dots = _hg_dots(False, rev)
        lbv = lb_ref[...]
        local = {}
        for c in range(nch):
            rows = _chunk_rows(c)
            zv = zv_ref[rows, :]
            intra, qs, kd, dec = _hg_local(zq_ref[rows, :], zf_ref[rows, :], zv, lbv, consts, dots)
            qs = qs.astype(BF16)
            qs_ref[rows, :] = qs
            dec_ref[c] = dec
            local[c] = (intra, qs, dec, dots[2](zv, kd))
        st = state_ref[...]
        for c in _time_order(nch, rev):
            intra, qs, dec, upd = local[c]
            st_ref[c] = st.astype(BF16)
            o_ref[_chunk_rows(c), :] = intra + _bdot(qs, st, ((1,), (1,)))
            st = dec * st + upd
        state_ref[...] = st

    trow = (lambda t: nt - 1 - t) if rev else (lambda t: t)
    col = lambda off: pl.BlockSpec((ts, HG_DIM), lambda h, t: (trow(t), off + h))
    tile, mats, vecs = _hg_specs(ts, nch, trow)
    nchunks = s // HG_CHUNK
    return _call(
        body, name=name, grid=(HG_HEADS, nt),
        out_shape=(jax.ShapeDtypeStruct((s, D_MODEL), F32),
                   jax.ShapeDtypeStruct((HG_HEADS, nchunks, HG_DIM, HG_DIM), BF16),
                   jax.ShapeDtypeStruct((s, D_MODEL), BF16),
                   jax.ShapeDtypeStruct((HG_HEADS, nchunks, 1, HG_DIM), F32)),
        in_specs=[col(0), col(fcol), col(3 * HG_HEADS), pl.BlockSpec((None, 1, HG_DIM), lambda h, t: (h, 0, 0))],
        out_specs=(tile, mats, tile, vecs), args=(z, z, z, lb),
        scratch_shapes=[pltpu.VMEM((HG_DIM, HG_DIM), F32)], sem=("parallel", "arbitrary"), comm=comm,
        edge=lambda: _hg_edge(nt))


def _hg_bwd(z, lb, states, qs, dec, dout, addq, addv, *, rev, ts, comm=None, name):
    s = z.shape[0]
    nt = s // ts
    nch = ts // HG_CHUNK
    fcol = HG_HEADS * (2 if rev else 1)
    has_add = addq is not None

    def body(*refs):
        zq_ref, zf_ref, zv_ref, lb_ref, st_ref, qs_ref, dec_ref, do_ref = refs[:8]
        aq_ref, av_ref = (refs[8], refs[9]) if has_add else (None, None)
        dq_ref, df_ref, dv_ref, dlb_ref, grad_ref = refs[-5:]

        @pl.when(pl.program_id(1) == 0)
        def _():
            grad_ref[...] = jnp.zeros_like(grad_ref)

        consts = _hg_consts(rev)
        dots = _hg_dots(True, rev)
        lbv = lb_ref[...]
        prods = {c: _bdot(do_ref[_chunk_rows(c), :], qs_ref[_chunk_rows(c), :], ((0,), (0,))) for c in range(nch)}
        gleave = {}
        gr = grad_ref[...]
        for c in reversed(_time_order(nch, rev)):
            gleave[c] = gr
            gr = dec_ref[c] * gr + prods[c]
        grad_ref[...] = gr
        dlb_blk = jnp.zeros((1, HG_DIM), F32)
        for c in range(nch):
            rows = _chunk_rows(c)
            fn = lambda a, b2, c2, d2, e2: _hg_chunk(a, b2, c2, d2, e2, consts, dots)
            _, pull = jax.vjp(fn, zq_ref[rows, :], zf_ref[rows, :], zv_ref[rows, :], lbv, st_ref[c].astype(F32))
            dq, df, dv, dlb, _ = pull((do_ref[rows, :], gleave[c]))
            if has_add:
                dq = dq + aq_ref[rows, :]
                dv = dv + av_ref[rows, :]
            dq_ref[rows, :] = dq.astype(dq_ref.dtype)
            df_ref[rows, :] = df.astype(BF16)
            dv_ref[rows, :] = dv.astype(dv_ref.dtype)
            dlb_blk = dlb_blk + dlb

        @pl.when(pl.program_id(1) == 0)
        def _():
            dlb_ref[...] = dlb_blk

        @pl.when(pl.program_id(1) > 0)
        def _():
            dlb_ref[...] += dlb_blk

    trow = (lambda t: t) if rev else (lambda t: nt - 1 - t)
    col = lambda off: pl.BlockSpec((ts, HG_DIM), lambda h, t: (trow(t), off + h))
    tile, mats, vecs = _hg_specs(ts, nch, trow)
    in_specs = [col(0), col(fcol), col(3 * HG_HEADS), pl.BlockSpec((None, 1, HG_DIM), lambda h, t: (h, 0, 0)),
                mats, tile, vecs, tile]
    args = [z, z, z, lb, states, qs, dec, dout]
    if has_add:
        in_specs += [tile, tile]
        args += [addq, addv]
    act = lambda dt: jax.ShapeDtypeStruct((s, D_MODEL), dt)
    sums = BF16 if has_add else F32
    return _call(
        body, name=name, grid=(HG_HEADS, nt),
        out_shape=(act(sums), act(BF16), act(sums), jax.ShapeDtypeStruct((HG_HEADS, 1, HG_DIM), F32)),
        in_specs=in_specs,
        out_specs=(tile, tile, tile, pl.BlockSpec((None, 1, HG_DIM), lambda h, t: (h, 0, 0))), args=tuple(args),
        scratch_shapes=[pltpu.VMEM((HG_DIM, HG_DIM), F32)], sem=("parallel", "arbitrary"), comm=comm,
        edge=lambda: _hg_edge(nt))


def _hg_post(of, ob, z, norm_g, *, tm, name):
    s = of.shape[0]

    def body(of_ref, ob_ref, gate_ref, ng_ref, y_ref):
        gn = ng_ref[...]
        for h in range(HG_HEADS):
            ln = slice(HG_DIM * h, HG_DIM * (h + 1))
            o = of_ref[:, ln] + ob_ref[:, ln]
            r = lax.rsqrt(jnp.mean(o * o, axis=-1, keepdims=True) + LN_EPS)
            gt = gate_ref[:, ln]
            y_ref[:, ln] = (o * r * gn * gt * _sigmoid(gt)).astype(BF16)

    row = lambda i: (i, 0)
    return pl.pallas_call(
        body, name=name, grid=(s // tm,),
        out_shape=jax.ShapeDtypeStruct((s, D_MODEL), BF16),
        in_specs=[pl.BlockSpec((tm, D_MODEL), row), pl.BlockSpec((tm, D_MODEL), row),
                  pl.BlockSpec((tm, D_MODEL), lambda i: (i, 4)), pl.BlockSpec((1, HG_DIM), lambda i: (0, 0))],
        out_specs=pl.BlockSpec((tm, D_MODEL), row),
        compiler_params=_params(("parallel",)),
    )(of, ob, z, norm_g)


def _hg_post_bwd(dy, of, ob, z, norm_g, *, tm, name):
    s = of.shape[0]

    def body(dy_ref, of_ref, ob_ref, gate_ref, ng_ref, do_ref, dgate_ref, dng_ref):
        gn = ng_ref[...]
        tot = jnp.zeros((1, HG_DIM), F32)
        for h in range(HG_HEADS):
            ln = slice(HG_DIM * h, HG_DIM * (h + 1))
            d = dy_ref[:, ln].astype(F32)
            o = of_ref[:, ln] + ob_ref[:, ln]
            r = lax.rsqrt(jnp.mean(o * o, axis=-1, keepdims=True) + LN_EPS)
            ohat = o * r
            gt = gate_ref[:, ln]
            sg = _sigmoid(gt)
            don = d * gt * sg
            dgate_ref[:, ln] = (d * ohat * gn * sg * (1.0 + gt * (1.0 - sg))).astype(BF16)
            tot = tot + jnp.sum(don * ohat, axis=0, keepdims=True)
            dohat = don * gn
            do_ref[:, ln] = r * (dohat - ohat * jnp.mean(dohat * ohat, axis=-1, keepdims=True))

        @pl.when(pl.program_id(0) == 0)
        def _():
            dng_ref[...] = tot

        @pl.when(pl.program_id(0) > 0)
        def _():
            dng_ref[...] += tot

    row = lambda i: (i, 0)
    return pl.pallas_call(
        body, name=name, grid=(s // tm,),
        out_shape=(jax.ShapeDtypeStruct((s, D_MODEL), F32), jax.ShapeDtypeStruct((s, D_MODEL), BF16),
                   jax.ShapeDtypeStruct((1, HG_DIM), F32)),
        in_specs=[pl.BlockSpec((tm, D_MODEL), row), pl.BlockSpec((tm, D_MODEL), row), pl.BlockSpec((tm, D_MODEL), row),
                  pl.BlockSpec((tm, D_MODEL), lambda i: (i, 4)), pl.BlockSpec((1, HG_DIM), lambda i: (0, 0))],
        out_specs=(pl.BlockSpec((tm, D_MODEL), row), pl.BlockSpec((tm, D_MODEL), row),
                   pl.BlockSpec((1, HG_DIM), lambda i: (0, 0))),
        compiler_params=_params(("arbitrary",)),
    )(dy, of, ob, z, norm_g)


def _lb_fwd(logits, *, name):
    w = logits.shape[1]

    def body(l_ref, o_ref):
        lg = l_ref[...]
        e = jnp.exp(lg - jnp.max(lg, axis=0, keepdims=True))
        sm = e / jnp.sum(e, axis=0, keepdims=True)
        o_ref[0:1, :] = sm[1:2]
        o_ref[1:2, :] = sm[1:2] + sm[2:3] + sm[3:4]

    return pl.pallas_call(body, name=name, out_shape=jax.ShapeDtypeStruct((2, w), F32))(logits)


def _lb_bwd(logits, dlb, *, name):
    w = logits.shape[1]

    def body(l_ref, d_ref, o_ref):
        lg = l_ref[...]
        e = jnp.exp(lg - jnp.max(lg, axis=0, keepdims=True))
        sm = e / jnp.sum(e, axis=0, keepdims=True)
        d1, d3 = d_ref[0:1, :], d_ref[1:2, :]
        dot = sm[1:2] * (d1 + d3) + (sm[2:3] + sm[3:4]) * d3
        o_ref[0:1, :] = -sm[0:1] * dot
        o_ref[1:2, :] = sm[1:2] * (d1 + d3 - dot)
        o_ref[2:3, :] = sm[2:3] * (d3 - dot)
        o_ref[3:4, :] = sm[3:4] * (d3 - dot)

    return pl.pallas_call(body, name=name, out_shape=jax.ShapeDtypeStruct((4, w), F32))(logits, dlb)


def _adamw(w, g, m, v, *, tr, g_off=0, name):
    rows = w.shape[0]
    parts = g.ndim == 3
    c1 = 1.0 / (1.0 - ADAM_B1 ** ADAM_STEP)
    c2 = 1.0 / (1.0 - ADAM_B2 ** ADAM_STEP)

    def body(w_ref, g_ref, m_ref, v_ref, go_ref, d_ref, mo_ref, vo_ref):
        if parts:
            gg = g_ref[0].astype(F32)
            for i in range(1, N_DEV):
                gg = gg + g_ref[i].astype(F32)
        else:
            gg = g_ref[...]
        mm = ADAM_B1 * m_ref[...] + (1.0 - ADAM_B1) * gg
        vv = ADAM_B2 * v_ref[...] + (1.0 - ADAM_B2) * (gg * gg)
        go_ref[...] = gg
        mo_ref[...] = mm
        vo_ref[...] = vv
        d_ref[...] = -ADAM_LR * ((mm * c1) / (jnp.sqrt(vv * c2) + ADAM_EPS) + ADAM_WD * w_ref[...])

    tile = pl.BlockSpec((tr, D_MODEL), lambda i: (i, 0))
    gspec = pl.BlockSpec((N_DEV, tr, D_MODEL), lambda i: (0, i + g_off // tr, 0)) if parts else tile
    out = jax.ShapeDtypeStruct((rows, D_MODEL), F32)
    return pl.pallas_call(
        body, name=name, grid=(rows // tr,),
        out_shape=(out, out, out, out),
        in_specs=[tile, gspec, tile, tile], out_specs=(tile, tile, tile, tile),
        compiler_params=_params(("parallel",)),
    )(w, g, m, v)


def _sum8(parts, *, name):
    def body(p_ref, o_ref):
        tot = p_ref[0]
        for i in range(1, N_DEV):
            tot = tot + p_ref[i]
        o_ref[...] = tot

    return pl.pallas_call(body, name=name, out_shape=jax.ShapeDtypeStruct(parts.shape[1:], parts.dtype))(parts)


def _layer_params(i):
    j = i // 2
    mix = [("att_w_qkv", j, 1), ("att_w_o", j, 0)] if i % 2 == 0 else [("hgrn_w_in", j, 1), ("hgrn_w_o", j, 0)]
    return mix + [("ffn_w_in", i, 1), ("ffn_w_out", i, 0), ("ple_w_gate", i, 0), ("ple_w_proj", i, 1)]


def _pack_local(tree, params):
    return jnp.concatenate([tree[n][j].reshape(-1, D_MODEL) for n, j, _ in params], axis=0)


def _unpack_local(packed, params, like):
    out, r = {}, 0
    for n, _, _ in params:
        shp = like[n].shape[1:]
        k = shp[0] * shp[1] // D_MODEL
        out[n] = packed[r:r + k].reshape(shp)
        r += k
    return out


def _unpack_gathered(gathered, i, like):
    out, r = {}, 0
    for n, _, ax in _layer_params(i):
        shp = like[n].shape[1:]
        k = shp[0] * shp[1] // D_MODEL
        t = gathered[:, r:r + k].reshape((N_DEV,) + shp)
        out[n] = (jnp.moveaxis(t, 0, 1).reshape(shp[0], N_DEV * shp[1]) if ax == 1
                  else t.reshape(N_DEV * shp[0], shp[1]))
        r += k
    return out


def _pack_full(grads, params, like):
    cols = []
    for n, _, ax in params:
        shp = like[n].shape[1:]
        t = (jnp.moveaxis(grads[n].reshape(shp[0], N_DEV, shp[1]), 1, 0) if ax == 1
             else grads[n].reshape(N_DEV, shp[0], shp[1]))
        cols.append(t.reshape(N_DEV, -1, D_MODEL).astype(BF16))
    return jnp.concatenate(cols, axis=1)


def _row_tile(rows):
    return max(t for t in range(16, 257, 16) if rows % t == 0)


SMALL_ROWS = 24


def _pad_row(a):
    flat = a.reshape(1, -1)
    return jnp.pad(flat, ((0, 0), (0, D_MODEL - flat.shape[1])))


def _tile(n, pref):
    return min(n, pref)


def kernel(x, p, att_w_qkv, att_sink, att_w_o, hgrn_w_in, hgrn_lb_logits, hgrn_norm_g, hgrn_w_o, ln_mix_g, ln_mix_b, ffn_w_in, ffn_w_out, ln_ffn_g, ln_ffn_b, ple_w_gate, ple_w_proj, loss_target, m_att_w_qkv, m_att_sink, m_att_w_o, m_hgrn_w_in, m_hgrn_lb_logits, m_hgrn_norm_g, m_hgrn_w_o, m_ln_mix_g, m_ln_mix_b, m_ffn_w_in, m_ffn_w_out, m_ln_ffn_g, m_ln_ffn_b, m_ple_w_gate, m_ple_w_proj, v_att_w_qkv, v_att_sink, v_att_w_o, v_hgrn_w_in, v_hgrn_lb_logits, v_hgrn_norm_g, v_hgrn_w_o, v_ln_mix_g, v_ln_mix_b, v_ffn_w_in, v_ffn_w_out, v_ln_ffn_g, v_ln_ffn_b, v_ple_w_gate, v_ple_w_proj):
    names = ["att_w_qkv", "att_sink", "att_w_o", "hgrn_w_in", "hgrn_lb_logits", "hgrn_norm_g", "hgrn_w_o", "ln_mix_g",
             "ln_mix_b", "ffn_w_in", "ffn_w_out", "ln_ffn_g", "ln_ffn_b", "ple_w_gate", "ple_w_proj"]
    w = dict(zip(names, (att_w_qkv, att_sink, att_w_o, hgrn_w_in, hgrn_lb_logits, hgrn_norm_g, hgrn_w_o, ln_mix_g,
                         ln_mix_b, ffn_w_in, ffn_w_out, ln_ffn_g, ln_ffn_b, ple_w_gate, ple_w_proj)))
    mom = dict(zip(names, (m_att_w_qkv, m_att_sink, m_att_w_o, m_hgrn_w_in, m_hgrn_lb_logits, m_hgrn_norm_g, m_hgrn_w_o,
                           m_ln_mix_g, m_ln_mix_b, m_ffn_w_in, m_ffn_w_out, m_ln_ffn_g, m_ln_ffn_b, m_ple_w_gate,
                           m_ple_w_proj)))
    var = dict(zip(names, (v_att_w_qkv, v_att_sink, v_att_w_o, v_hgrn_w_in, v_hgrn_lb_logits, v_hgrn_norm_g, v_hgrn_w_o,
                           v_ln_mix_g, v_ln_mix_b, v_ffn_w_in, v_ffn_w_out, v_ln_ffn_g, v_ln_ffn_b, v_ple_w_gate,
                           v_ple_w_proj)))
    s = x.shape[1]
    me = 4 * lax.axis_index("x") + 2 * lax.axis_index("y") + lax.axis_index("c")
    tm = _tile(s, 512)
    tbig = _tile(s, 1024)
    ts = _tile(s // 2, 512)
    x0 = x.reshape(s, D_MODEL)
    target = loss_target.reshape(s, D_MODEL)
    pl_in = p.reshape(DEPTH, s, PLE_DIM)

    w_rows = [_pack_local(w, _layer_params(i)).astype(BF16) for i in range(DEPTH)]
    full = _unpack_gathered(_gather(w_rows[0], name="gather_weights"), 0, w)
    lb_rows = jnp.pad(hgrn_lb_logits.reshape(8, HG_DIM), ((0, 0), (0, D_MODEL - HG_DIM)))
    lb_all = _gather(lb_rows, name="gather_lb")[:, :, :HG_DIM]
    logits_full = jnp.moveaxis(lb_all, 0, 1).reshape(DEPTH, 2 * D_MODEL)
    lb = _lb_fwd(logits_full, name="lb_fwd")
    cos, sin = _rope_tables(s)

    saved = []
    xf, xb = x0, x0
    for i in range(DEPTH):
        j = i // 2
        sv = {"x": xf, "xb": xb, "w": full}
        nxt = (w_rows[i + 1], True) if i + 1 < DEPTH else None
        if i % 2 == 0:
            z = _mm(xb, full["att_w_qkv"], tm=tbig, tn=512, tk=D_MODEL, name="att_in")
            o, *more = _att_fwd(z, w["att_sink"][j], cos, sin, comm=nxt, name="att_fwd")
            w_o = full["att_w_o"]
        else:
            z = _mm(xb, full["hgrn_w_in"], tm=tbig, tn=1024, tk=D_MODEL, name="hgrn_in")
            lbl = lb[j].reshape(2, HG_HEADS, 1, HG_DIM)
            of, st_f, qs_f, dec_f, *more = _hg_fwd(z, lbl[0], rev=False, ts=ts, comm=nxt, name="hgrn_fwd")
            ob, st_b, qs_b, dec_b = _hg_fwd(z, lbl[1], rev=True, ts=ts, name="hgrn_fwd_rev")
            o = _hg_post(of, ob, z, w["hgrn_norm_g"][j].reshape(1, HG_DIM), tm=tm, name="hgrn_post")
            w_o = full["hgrn_w_o"]
            sv.update(of=of, ob=ob, st_f=st_f, st_b=st_b, lbl=lbl, qs_f=qs_f, qs_b=qs_b, dec_f=dec_f, dec_b=dec_b)
        sv.update(z=z, o=o)
        pre1, x1, x1b = _proj_ln(o, w_o, xf, w["ln_mix_g"][i:i + 1], w["ln_mix_b"][i:i + 1], tm=tm, name="mix_out_ln")
        gg, uu, act = _ffn_in(x1b, full["ffn_w_in"], tm=tm, tn=FF_TILE, name="ffn_in")
        pre2, x2b, xf, xb = _ffn_out_ple(act, full["ffn_w_out"], x1, w["ln_ffn_g"][i:i + 1], w["ln_ffn_b"][i:i + 1],
                                         pl_in[i], full["ple_w_gate"], full["ple_w_proj"], tm=tm, name="ffn_out_ple")
        sv.update(pre1=pre1, x1=x1, x1b=x1b, g=gg, u=uu, act=act, pre2=pre2, x2b=x2b)
        saved.append(sv)
        if nxt is not None:
            full = _unpack_gathered(more[0], i + 1, w)

    dx, loss_blk = _loss_head(xf, target, tm=tm, name="loss_head")
    loss = lax.psum(loss_blk[0, 0], AXES)

    small = {n: [None] * DEPTH for n in ("ln_mix_g", "ln_mix_b", "ln_ffn_g", "ln_ffn_b")}
    dlb_rows = [None] * 4
    dnorm, dsink = [None] * 2, [None] * 2
    recv_late, recv_early = [None] * DEPTH, [None] * DEPTH
    above = None
    mmw = functools.partial(_mm, ta=True, tk=_tile(s, 2048), out_dtype=BF16)
    for i in reversed(range(DEPTH)):
        j = i // 2
        sv = saved[i]
        full, gl = sv["w"], {}
        da, dpp, dy2, dy2b, small["ln_ffn_g"][i], small["ln_ffn_b"][i] = _ple_ln_bwd(
            dx, sv["x2b"], pl_in[i], full["ple_w_gate"], full["ple_w_proj"], sv["pre2"], w["ln_ffn_g"][i:i + 1],
            tm=tm, name="ple_ln_bwd")
        gl["ple_w_gate"] = mmw(sv["x2b"], da, tm=D_MODEL, tn=D_MODEL, name="dw_ple_gate")
        gl["ple_w_proj"] = mmw(pl_in[i], dpp, tm=PLE_DIM, tn=D_MODEL, name="dw_ple_proj")
        dg, du = _ffn_bwd_act(dy2b, full["ffn_w_out"], sv["g"], sv["u"], tm=tm, tn=FF_TILE, name="ffn_bwd_act")
        gl["ffn_w_out"] = mmw(sv["act"], dy2b, tm=FF_TILE, tn=D_MODEL, name="dw_ffn_out")
        dy1, dy1b, small["ln_mix_g"][i], small["ln_mix_b"][i] = _mm(
            [dg, du], full["ffn_w_in"], tm=tm, tn=D_MODEL, tk=FF_TILE, tb=True, add=dy2, add_scale=ALPHA,
            ln=(sv["pre1"], w["ln_mix_g"][i:i + 1]), name="ffn_bwd_x_ln")
        gl["ffn_w_in"] = mmw(sv["x1b"], [dg, du], tm=D_MODEL, tn=FF_TILE, name="dw_ffn_in")
        n_out, n_inw = ("att_w_o", "att_w_qkv") if i % 2 == 0 else ("hgrn_w_o", "hgrn_w_in")
        do = _mm(dy1b, full[n_out], tm=tbig, tn=D_MODEL, tk=D_MODEL, tb=True, out_dtype=BF16, name="mix_out_bwd")
        gl[n_out] = mmw(sv["o"], dy1b, tm=D_MODEL, tn=D_MODEL, name="dw_mix_out")
        early = _pack_full(gl, _layer_params(i)[1:], w)
        comm = (early if above is None else jnp.concatenate([above, early], axis=1), False)
        if i % 2 == 0:
            dzq, part, dsk, *more = _att_bwd(sv["z"], do, w["att_sink"][j], cos, sin, comm=comm, name="att_bwd")
            dz = [dzq, _att_bwd_kv(part, cos, sin, name="att_bwd_kv")]
            dsink[j] = dsk[:, 0]
        else:
            dsum, dgate, dnorm[j] = _hg_post_bwd(do, sv["of"], sv["ob"], sv["z"], w["hgrn_norm_g"][j].reshape(1, HG_DIM),
                                                 tm=tm, name="hgrn_post_bwd")
            dq1, df1, dv1, dlb1, *more = _hg_bwd(sv["z"], sv["lbl"][0], sv["st_f"], sv["qs_f"], sv["dec_f"], dsum, None,
                                                 None, rev=False, ts=ts, comm=comm, name="hgrn_bwd")
            dq2, df2, dv2, dlb2 = _hg_bwd(sv["z"], sv["lbl"][1], sv["st_b"], sv["qs_b"], sv["dec_b"], dsum, dq1, dv1,
                                          rev=True, ts=ts, name="hgrn_bwd_rev")
            dz = [dq2, df1, df2, dv2, dgate]
            dlb_rows[2 * j] = dlb1.reshape(1, D_MODEL)
            dlb_rows[2 * j + 1] = dlb2.reshape(1, D_MODEL)
        if above is not None:
            recv_late[i + 1] = (more[0], 0)
        recv_early[i] = (more[0], 0 if above is None else above.shape[1])
        dx = _mm(dz, full[n_inw], tm=tbig, tn=D_MODEL, tk=512 if i % 2 == 0 else D_MODEL, tb=True, add=dy1,
                 add_scale=ALPHA, name="mix_in_bwd")
        gl[n_inw] = mmw(sv["xb"], dz, tm=D_MODEL, tn=512, name="dw_mix_in")
        above = _pack_full(gl, _layer_params(i)[:1], w)
    grad_x = dx.reshape(x.shape)
    recv_late[0] = (_exchange(above, name="exchange_grads"), 0)

    big_out = [{n: [None] * w[n].shape[0] for n, _ in BIG} for _ in range(4)]
    for i in range(DEPTH):
        for params, (got, off) in ((_layer_params(i)[:1], recv_late[i]), (_layer_params(i)[1:], recv_early[i])):
            w_part = _pack_local(w, params)
            outs = _adamw(w_part, got, _pack_local(mom, params), _pack_local(var, params),
                          tr=_row_tile(math.gcd(w_part.shape[0], off)), g_off=off, name="adamw_big")
            for kind, packed in enumerate(outs):
                for (n, j, _), piece in zip(params, _unpack_local(packed, params, w).values()):
                    big_out[kind][n][j] = piece
    big_out = [{n: jnp.stack(v) for n, v in kind.items()} for kind in big_out]

    small_rows = jnp.concatenate(
        [jnp.concatenate(small[n], axis=0) for n in ("ln_mix_g", "ln_mix_b", "ln_ffn_g", "ln_ffn_b")] + dlb_rows
        + [_pad_row(jnp.stack(dnorm)), _pad_row(jnp.stack(dsink)), jnp.zeros((2, D_MODEL), F32)], axis=0)
    small_all = _gather(small_rows, name="gather_small")
    lbw, lbm, lbv = (t.reshape(4, 2 * HG_DIM) for t in (hgrn_lb_logits, mom["hgrn_lb_logits"], var["hgrn_lb_logits"]))
    summed = _sum8(small_all, name="sum_small")
    dlb_mine = lax.dynamic_slice_in_dim(summed[16:20].reshape(2, 2, HG_HEADS, HG_DIM), me, 1, axis=2)
    dlogits = _lb_bwd(lbw, dlb_mine.reshape(2, 2 * HG_DIM), name="lb_bwd")

    def small_pack(ln4, lbt, ng, sk):
        return jnp.concatenate([ln4[n] for n in ("ln_mix_g", "ln_mix_b", "ln_ffn_g", "ln_ffn_b")]
                               + [_pad_row(lbt), _pad_row(ng), _pad_row(sk), jnp.zeros((5, D_MODEL), F32)], axis=0)

    g_small = jnp.concatenate([summed[:16], _pad_row(dlogits), summed[20:22], jnp.zeros((5, D_MODEL), F32)], axis=0)
    souts = _adamw(small_pack(w, lbw, w["hgrn_norm_g"], w["att_sink"]), g_small,
                   small_pack(mom, lbm, mom["hgrn_norm_g"], mom["att_sink"]),
                   small_pack(var, lbv, var["hgrn_norm_g"], var["att_sink"]), tr=SMALL_ROWS, name="adamw_small")

    def small_unpack(t):
        out = {n: t[4 * k:4 * k + 4] for k, n in enumerate(("ln_mix_g", "ln_mix_b", "ln_ffn_g", "ln_ffn_b"))}
        out["hgrn_lb_logits"] = t[16].reshape(hgrn_lb_logits.shape)
        out["hgrn_norm_g"] = t[17, :2 * HG_DIM].reshape(hgrn_norm_g.shape)
        out["att_sink"] = t[18, :2 * N_Q_HEADS].reshape(att_sink.shape)
        return out

    result = [loss, grad_x]
    for big_t, small_t in zip(big_out, souts):
        merged = dict(big_t)
        merged.update(small_unpack(small_t))
        result += [merged[n] for n in names]
    return tuple(result)
```

```python
import functools
import math

import jax
import jax.numpy as jnp
from jax import lax
from jax.experimental import pallas as pl
from jax.experimental.pallas import tpu as pltpu

F32 = jnp.float32
BF16 = jnp.bfloat16

D_MODEL = 1024
DEPTH = 4
HEAD_DIM = 64
N_Q_HEADS = 16
N_KV_HEADS = 4
GROUP = 4
KV_DIM = 256
ATT_BLOCK = 128
ROPE_DIM = 16
ROPE_THETA = 500000.0
HG_HEADS = 8
HG_DIM = 128
HG_CHUNK = 64
HG_SUB = 16
D_FF = 2816
FF_TILE = 1408
SUB_ROWS = 256
PLE_DIM = 256
ALPHA = (2 * DEPTH) ** 0.25
LN_EPS = 1e-5
ADAM_LR, ADAM_B1, ADAM_B2, ADAM_EPS, ADAM_WD, ADAM_STEP = 0.001, 0.9, 0.999, 1e-08, 0.01, 10

N_DEV = 8
LANES = 128
VMEM_LIMIT = 52 * 1024 * 1024
NEG = -1e30
MESH = pl.DeviceIdType.MESH
AXES = ("x", "y", "c")

BIG = (("att_w_qkv", 2), ("att_w_o", 1), ("hgrn_w_in", 2), ("hgrn_w_o", 1), ("ffn_w_in", 2), ("ffn_w_out", 1),
       ("ple_w_gate", 1), ("ple_w_proj", 2))


def _params(sem=None, vmem=VMEM_LIMIT):
    return pltpu.CompilerParams(dimension_semantics=sem, vmem_limit_bytes=vmem)


def _sigmoid(x):
    return jax.nn.sigmoid(x)


def _direct_copies(src_ref, out_ref, send_sems, recv_sems, local_sem, gather, arrivals):
    x, y, c = lax.axis_index("x"), lax.axis_index("y"), lax.axis_index("c")
    me = 4 * x + 2 * y + c
    mine = (lambda j: src_ref) if gather else (lambda j: src_ref.at[j])
    pairs = []
    for k in range(1, N_DEV):
        px, py, pc = x ^ (k >> 2), y ^ ((k >> 1) & 1), c ^ (k & 1)
        peer = 4 * px + 2 * py + pc
        send = pltpu.make_async_remote_copy(
            src_ref=mine(peer), dst_ref=out_ref.at[me], send_sem=send_sems.at[k], recv_sem=recv_sems.at[k],
            device_id=(px, py, pc), device_id_type=MESH)
        arrival = pltpu.make_async_remote_copy(
            src_ref=mine(peer), dst_ref=out_ref.at[peer], send_sem=send_sems.at[k], recv_sem=recv_sems.at[k],
            device_id=(x, y, c), device_id_type=MESH) if arrivals else None
        pairs.append((send, arrival))
    return pltpu.make_async_copy(mine(me), out_ref.at[me], local_sem), pairs


def _direct_start(*refs, gather):
    local, pairs = _direct_copies(*refs, gather, False)
    local.start()
    for send, _ in pairs:
        send.start()


def _direct_wait(*refs, gather):
    local, pairs = _direct_copies(*refs, gather, True)
    for send, arrival in pairs:
        send.wait_send()
        arrival.wait_recv()
    local.wait()


COMM_SCRATCH = [pltpu.SemaphoreType.DMA((N_DEV,)), pltpu.SemaphoreType.DMA((N_DEV,)), pltpu.SemaphoreType.DMA]


def _exchange(src, *, gather=False, name):
    def body(*refs):
        _direct_start(*refs, gather=gather)
        _direct_wait(*refs, gather=gather)

    blk = tuple(src.shape) if gather else tuple(src.shape[1:])
    return pl.pallas_call(
        body, name=name,
        out_shape=jax.ShapeDtypeStruct((N_DEV,) + blk, src.dtype),
        in_specs=[pl.BlockSpec(memory_space=pltpu.HBM)],
        out_specs=pl.BlockSpec(memory_space=pltpu.HBM),
        scratch_shapes=COMM_SCRATCH,
    )(src)


def _call(body, *, name, grid, out_shape, in_specs, out_specs, args, scratch_shapes=(), sem, comm=None, edge=None):
    out_shape, out_specs = tuple(out_shape), tuple(out_specs)
    if comm is None:
        return pl.pallas_call(body, name=name, grid=grid, out_shape=out_shape, in_specs=list(in_specs),
                              out_specs=out_specs, scratch_shapes=list(scratch_shapes),
                              compiler_params=_params(sem))(*args)
    src, gather = comm
    n_in, n_out, n_scr = len(args), len(out_shape), len(scratch_shapes)
    blk = tuple(src.shape) if gather else tuple(src.shape[1:])
    hbm = pl.BlockSpec(memory_space=pltpu.HBM)

    def carrying(*refs):
        ins, src_ref = refs[:n_in], refs[n_in]
        outs, dst_ref = refs[n_in + 1:n_in + 1 + n_out], refs[n_in + 1 + n_out]
        own = refs[n_in + 2 + n_out:n_in + 2 + n_out + n_scr]
        comm_refs = (src_ref, dst_ref) + tuple(refs[n_in + 2 + n_out + n_scr:])
        first, last = edge()

        @pl.when(first)
        def _():
            _direct_start(*comm_refs, gather=gather)

        body(*ins, *outs, *own)

        @pl.when(last)
        def _():
            _direct_wait(*comm_refs, gather=gather)

    return pl.pallas_call(
        carrying, name=name, grid=grid,
        out_shape=out_shape + (jax.ShapeDtypeStruct((N_DEV,) + blk, src.dtype),),
        in_specs=list(in_specs) + [hbm], out_specs=out_specs + (hbm,),
        scratch_shapes=list(scratch_shapes) + COMM_SCRATCH,
        compiler_params=_params(("arbitrary",) * len(grid)),
    )(*args, src)


def _gather(src, *, name):
    def body(src_ref, out_ref, send_sems, recv_sems, local_sem):
        x, y, c = lax.axis_index("x"), lax.axis_index("y"), lax.axis_index("c")
        sibling = (x, y, 1 - c)
        chips = [(1 - x, y), (x, 1 - y), (1 - x, 1 - y)]

        def rows(px, py, pc):
            return out_ref.at[4 * px + 2 * py + pc]

        def copy(k, block, to, from_src=False):
            return pltpu.make_async_remote_copy(
                src_ref=src_ref if from_src else rows(*block), dst_ref=rows(*block), send_sem=send_sems.at[k],
                recv_sem=recv_sems.at[k], device_id=to, device_id_type=MESH)

        me = (x, y, c)
        mine = pltpu.make_async_copy(src_ref, rows(*me), local_sem)
        mine.start()
        first = [copy(0, me, sibling, from_src=True)]
        first += [copy(1 + j, me, (*chip, c), from_src=True) for j, chip in enumerate(chips)]
        for cp in first:
            cp.start()
        passed = [copy(4 + j, (*chip, c), sibling) for j, chip in enumerate(chips)]
        for j, chip in enumerate(chips):
            copy(1 + j, (*chip, c), me).wait_recv()
            passed[j].start()
        copy(0, sibling, me).wait_recv()
        for j, chip in enumerate(chips):
            copy(4 + j, (*chip, 1 - c), me).wait_recv()
        for cp in first + passed:
            cp.wait_send()
        mine.wait()

    return pl.pallas_call(
        body, name=name,
        out_shape=jax.ShapeDtypeStruct((N_DEV,) + tuple(src.shape), src.dtype),
        in_specs=[pl.BlockSpec(memory_space=pltpu.HBM)],
        out_specs=pl.BlockSpec(memory_space=pltpu.HBM),
        scratch_shapes=[pltpu.SemaphoreType.DMA((7,)), pltpu.SemaphoreType.DMA((7,)), pltpu.SemaphoreType.DMA],
    )(src)


def _mm(a, b, *, tm, tn, tk, ta=False, tb=False, out_dtype=F32, add=None, add_scale=1.0, ln=None, name):
    a_list = list(a) if isinstance(a, (list, tuple)) else [a]
    b_list = list(b) if isinstance(b, (list, tuple)) else [b]
    assert not (ta and len(a_list) > 1) and not (tb and len(b_list) > 1)
    m = a_list[0].shape[1] if ta else a_list[0].shape[0]
    koff = [0]
    for piece in a_list:
        koff.append(koff[-1] + (piece.shape[0] if ta else piece.shape[1]) // tk)
    joff = [0]
    for piece in b_list:
        joff.append(joff[-1] + (piece.shape[0] if tb else piece.shape[1]) // tn)
    nk, n = koff[-1], joff[-1] * tn
    dims = (((0 if ta else 1,), (1 if tb else 0,)), ((), ()))

    def within(idx, off, p, count):
        return None if count == 1 else (idx >= off[p]) & (idx < off[p + 1])

    def body(*refs):
        a_refs, b_refs = refs[:len(a_list)], refs[len(a_list):len(a_list) + len(b_list)]
        nxt = len(a_list) + len(b_list)
        add_ref = None
        if add is not None:
            add_ref = refs[nxt]
            nxt += 1
        if ln is not None:
            pre_ref, g_ref = refs[nxt:nxt + 2]
            nxt += 2
        outs = refs[nxt:nxt + (4 if ln is not None else 1)]
        acc_ref = refs[-1] if nk > 1 else None
        j, k = pl.program_id(1), pl.program_id(2)

        def finish(r):
            if add_ref is not None:
                r = r + add_scale * add_ref[...]
            if ln is None:
                outs[0][...] = r.astype(out_dtype)
            else:
                _ln_bwd_store(r, pre_ref[...], g_ref[...], *outs, first=pl.program_id(0) == 0)

        for pa, a_ref in enumerate(a_refs):
            for pb, b_ref in enumerate(b_refs):
                def step(a_ref=a_ref, b_ref=b_ref, pa=pa):
                    part = lax.dot_general(a_ref[...].astype(BF16), b_ref[...].astype(BF16), dims,
                                           preferred_element_type=F32)
                    if nk == 1:
                        finish(part)
                    elif pa == 0:
                        _accumulate(acc_ref, part, k == 0)
                    else:
                        acc_ref[...] += part

                conds = [c for c in (within(k, koff, pa, len(a_list)), within(j, joff, pb, len(b_list))) if c is not None]
                if conds:
                    pl.when(functools.reduce(jnp.logical_and, conds))(step)
                else:
                    step()
        if nk > 1:
            @pl.when(k == nk - 1)
            def _():
                finish(acc_ref[...])

    def a_spec(p):
        kk = lambda k: jnp.clip(k - koff[p], 0, koff[p + 1] - koff[p] - 1)
        return (pl.BlockSpec((tk, tm), lambda i, j, k: (kk(k), i)) if ta
                else pl.BlockSpec((tm, tk), lambda i, j, k: (i, kk(k))))

    def b_spec(p):
        jj = lambda j: jnp.clip(j - joff[p], 0, joff[p + 1] - joff[p] - 1)
        kk = (lambda j, k: k) if len(b_list) == 1 else (lambda j, k: jnp.where((j >= joff[p]) & (j < joff[p + 1]), k, 0))
        return (pl.BlockSpec((tn, tk), lambda i, j, k: (jj(j), kk(j, k))) if tb
                else pl.BlockSpec((tk, tn), lambda i, j, k: (kk(j, k), jj(j))))

    in_specs = [a_spec(p) for p in range(len(a_list))] + [b_spec(p) for p in range(len(b_list))]
    args = a_list + b_list
    tile = pl.BlockSpec((tm, tn), lambda i, j, k: (i, j))
    if add is not None:
        in_specs.append(tile)
        args.append(add)
    out_shape, out_specs = jax.ShapeDtypeStruct((m, n), out_dtype), tile
    if ln is not None:
        assert tn == n == D_MODEL
        vec = pl.BlockSpec((1, D_MODEL), lambda i, j, k: (0, 0))
        in_specs += [tile, vec]
        args += list(ln)
        out_shape = (jax.ShapeDtypeStruct((m, n), F32), jax.ShapeDtypeStruct((m, n), BF16),
                     jax.ShapeDtypeStruct((1, D_MODEL), F32), jax.ShapeDtypeStruct((1, D_MODEL), F32))
        out_specs = (tile, tile, vec, vec)
    return pl.pallas_call(
        body, name=name, grid=(m // tm, n // tn, nk),
        out_shape=out_shape, in_specs=in_specs, out_specs=out_specs,
        scratch_shapes=[pltpu.VMEM((tm, tn), F32)] if nk > 1 else [],
        compiler_params=_params(("arbitrary",) * 3 if ln is not None else ("parallel", "parallel", "arbitrary")),
    )(*args)


def _dx_from_pieces(pieces, w, add, *, tm, name):
    s = pieces[0].shape[0]
    widths = [p.shape[1] for p in pieces]

    def body(*refs):
        p_refs, (w_ref, add_ref, o_ref) = refs[:len(pieces)], refs[len(pieces):]
        for rs in _row_parts(tm):
            r = ALPHA * add_ref[rs, :]
            off = 0
            for p_ref, width in zip(p_refs, widths):
                r = r + lax.dot_general(p_ref[rs, :], w_ref[:, off:off + width], (((1,), (1,)), ((), ())),
                                        preferred_element_type=F32)
                off += width
            o_ref[rs, :] = r

    row = lambda i: (i, 0)
    return pl.pallas_call(
        body, name=name, grid=(s // tm,),
        out_shape=jax.ShapeDtypeStruct((s, D_MODEL), F32),
        in_specs=[pl.BlockSpec((tm, width), row) for width in widths]
        + [pl.BlockSpec((D_MODEL, sum(widths)), lambda i: (0, 0)), pl.BlockSpec((tm, D_MODEL), row)],
        out_specs=pl.BlockSpec((tm, D_MODEL), row),
        compiler_params=_params(("parallel",)),
    )(*pieces, w, add)


def _ln_bwd_rows(do, y, g):
    mu = jnp.mean(y, axis=-1, keepdims=True)
    yc = y - mu
    var = jnp.mean(yc * yc, axis=-1, keepdims=True)
    rstd = lax.rsqrt(var + LN_EPS)
    xhat = yc * rstd
    dxhat = do * g
    dy = rstd * (dxhat - jnp.mean(dxhat, axis=-1, keepdims=True) - xhat * jnp.mean(dxhat * xhat, axis=-1, keepdims=True))
    return dy, jnp.sum(do * xhat, axis=0, keepdims=True), jnp.sum(do, axis=0, keepdims=True)


def _accumulate(ref, val, first):
    @pl.when(first)
    def _():
        ref[...] = val

    @pl.when(jnp.logical_not(first))
    def _():
        ref[...] += val


def _ln_bwd_store(do, y, g, dy_ref, dybf_ref, dg_ref, db_ref, *, first):
    dy, pg, pb = _ln_bwd_rows(do, y, g)
    dy_ref[...] = dy
    dybf_ref[...] = dy.astype(BF16)
    _accumulate(dg_ref, pg, first)
    _accumulate(db_ref, pb, first)


def _layer_norm_rows(y, g, b):
    mu = jnp.mean(y, axis=-1, keepdims=True)
    yc = y - mu
    var = jnp.mean(yc * yc, axis=-1, keepdims=True)
    return yc * lax.rsqrt(var + LN_EPS) * g + b


def _proj_ln(a, w, res, g, b, *, tm, name):
    s, kdim = a.shape

    def body(a_ref, w_ref, res_ref, g_ref, b_ref, pre_ref, o_ref, obf_ref):
        for rs in _row_parts(tm):
            h = jnp.dot(a_ref[rs, :], w_ref[...], preferred_element_type=F32)
            pre = ALPHA * res_ref[rs, :] + h
            out = _layer_norm_rows(pre, g_ref[...], b_ref[...])
            pre_ref[rs, :] = pre
            o_ref[rs, :] = out
            obf_ref[rs, :] = out.astype(BF16)

    row = lambda i: (i, 0)
    fix = lambda i: (0, 0)
    return pl.pallas_call(
        body, name=name, grid=(s // tm,),
        out_shape=(jax.ShapeDtypeStruct((s, D_MODEL), F32), jax.ShapeDtypeStruct((s, D_MODEL), F32),
                   jax.ShapeDtypeStruct((s, D_MODEL), BF16)),
        in_specs=[pl.BlockSpec((tm, kdim), row), pl.BlockSpec((kdim, D_MODEL), fix), pl.BlockSpec((tm, D_MODEL), row),
                  pl.BlockSpec((1, D_MODEL), fix), pl.BlockSpec((1, D_MODEL), fix)],
        out_specs=(pl.BlockSpec((tm, D_MODEL), row),) * 3,
        compiler_params=_params(("parallel",)),
    )(a, w, res, g, b)


def _row_parts(tm):
    sub = min(tm, SUB_ROWS)
    return [pl.ds(r * sub, sub) for r in range(tm // sub)]


def _ffn_in(xbf, w, *, tm, tn, name):
    s = xbf.shape[0]
    nj = D_FF // tn

    def body(x_ref, wg_ref, wu_ref, g_ref, u_ref, act_ref):
        for rs in _row_parts(tm):
            xv = x_ref[rs, :]
            gg = jnp.dot(xv, wg_ref[...], preferred_element_type=F32)
            uu = jnp.dot(xv, wu_ref[...], preferred_element_type=F32)
            g_ref[rs, :] = gg.astype(BF16)
            u_ref[rs, :] = uu.astype(BF16)
            act_ref[rs, :] = (gg * _sigmoid(gg) * uu).astype(BF16)

    out = jax.ShapeDtypeStruct((s, D_FF), BF16)
    tile = pl.BlockSpec((tm, tn), lambda j, i: (i, j))
    return pl.pallas_call(
        body, name=name, grid=(nj, s // tm),
        out_shape=(out, out, out),
        in_specs=[pl.BlockSpec((tm, D_MODEL), lambda j, i: (i, 0)), pl.BlockSpec((D_MODEL, tn), lambda j, i: (0, j)),
                  pl.BlockSpec((D_MODEL, tn), lambda j, i: (0, j + nj))],
        out_specs=(tile, tile, tile),
        compiler_params=_params(("parallel", "parallel")),
    )(xbf, w, w)


def _ffn_bwd_act(dybf, w_out, g, u, *, tm, tn, name):
    s = dybf.shape[0]

    def body(dy_ref, w_ref, g_ref, u_ref, dg_ref, du_ref):
        for rs in _row_parts(tm):
            dact = lax.dot_general(dy_ref[rs, :], w_ref[...], (((1,), (1,)), ((), ())), preferred_element_type=F32)
            gg = g_ref[rs, :].astype(F32)
            uu = u_ref[rs, :].astype(F32)
            sg = _sigmoid(gg)
            dg_ref[rs, :] = (dact * uu * sg * (1.0 + gg * (1.0 - sg))).astype(BF16)
            du_ref[rs, :] = (dact * gg * sg).astype(BF16)

    out = jax.ShapeDtypeStruct((s, D_FF), BF16)
    tile = pl.BlockSpec((tm, tn), lambda j, i: (i, j))
    return pl.pallas_call(
        body, name=name, grid=(D_FF // tn, s // tm),
        out_shape=(out, out),
        in_specs=[pl.BlockSpec((tm, D_MODEL), lambda j, i: (i, 0)), pl.BlockSpec((tn, D_MODEL), lambda j, i: (j, 0)),
                  tile, tile],
        out_specs=(tile, tile),
        compiler_params=_params(("parallel", "parallel")),
    )(dybf, w_out, g, u)


def _ffn_out_ple(act, w_out, res, g, b, p, w_gate, w_proj, *, tm, name):
    s = act.shape[0]

    def body(a_ref, w_ref, res_ref, g_ref, b_ref, p_ref, wg_ref, wp_ref, pre_ref, x2bf_ref, o_ref, obf_ref):
        for rs in _row_parts(tm):
            pre = ALPHA * res_ref[rs, :] + jnp.dot(a_ref[rs, :], w_ref[...], preferred_element_type=F32)
            x2 = _layer_norm_rows(pre, g_ref[...], b_ref[...])
            x2bf = x2.astype(BF16)
            pre_ref[rs, :] = pre
            x2bf_ref[rs, :] = x2bf
            gate = jnp.dot(x2bf, wg_ref[...], preferred_element_type=F32)
            pp = jnp.dot(p_ref[rs, :].astype(BF16), wp_ref[...], preferred_element_type=F32)
            out = x2 + _sigmoid(gate) * pp
            o_ref[rs, :] = out
            obf_ref[rs, :] = out.astype(BF16)

    row = lambda i: (i, 0)
    fix = lambda i: (0, 0)
    tile = pl.BlockSpec((tm, D_MODEL), row)
    vec = pl.BlockSpec((1, D_MODEL), fix)
    act_t = lambda dt: jax.ShapeDtypeStruct((s, D_MODEL), dt)
    return pl.pallas_call(
        body, name=name, grid=(s // tm,),
        out_shape=(act_t(F32), act_t(BF16), act_t(F32), act_t(BF16)),
        in_specs=[pl.BlockSpec((tm, D_FF), row), pl.BlockSpec((D_FF, D_MODEL), fix), tile, vec, vec,
                  pl.BlockSpec((tm, PLE_DIM), row), pl.BlockSpec((D_MODEL, D_MODEL), fix),
                  pl.BlockSpec((PLE_DIM, D_MODEL), fix)],
        out_specs=(tile, tile, tile, tile),
        compiler_params=_params(("parallel",)),
    )(act, w_out, res, g, b, p, w_gate, w_proj)


def _ple_ln_bwd(dx3, x2bf, p, w_gate, w_proj, pre, g, *, tm, name):
    s = dx3.shape[0]

    def body(d_ref, xbf_ref, p_ref, wg_ref, wp_ref, pre_ref, g_ref, da_ref, dpp_ref, dy_ref, dybf_ref, dg_ref, db_ref):
        pg = jnp.zeros((1, D_MODEL), F32)
        pb = jnp.zeros((1, D_MODEL), F32)
        for rs in _row_parts(tm):
            d = d_ref[rs, :]
            a = jnp.dot(xbf_ref[rs, :], wg_ref[...], preferred_element_type=F32)
            pp = jnp.dot(p_ref[rs, :].astype(BF16), wp_ref[...], preferred_element_type=F32)
            sg = _sigmoid(a)
            da = (d * pp * sg * (1.0 - sg)).astype(BF16)
            da_ref[rs, :] = da
            dpp_ref[rs, :] = (d * sg).astype(BF16)
            dx2 = d + lax.dot_general(da, wg_ref[...], (((1,), (1,)), ((), ())), preferred_element_type=F32)
            dy, qg, qb = _ln_bwd_rows(dx2, pre_ref[rs, :], g_ref[...])
            dy_ref[rs, :] = dy
            dybf_ref[rs, :] = dy.astype(BF16)
            pg, pb = pg + qg, pb + qb
        _accumulate(dg_ref, pg, pl.program_id(0) == 0)
        _accumulate(db_ref, pb, pl.program_id(0) == 0)

    row = lambda i: (i, 0)
    fix = lambda i: (0, 0)
    tile = pl.BlockSpec((tm, D_MODEL), row)
    vec = pl.BlockSpec((1, D_MODEL), fix)
    act = lambda dt: jax.ShapeDtypeStruct((s, D_MODEL), dt)
    return pl.pallas_call(
        body, name=name, grid=(s // tm,),
        out_shape=(act(BF16), act(BF16), act(F32), act(BF16), jax.ShapeDtypeStruct((1, D_MODEL), F32),
                   jax.ShapeDtypeStruct((1, D_MODEL), F32)),
        in_specs=[tile, tile, pl.BlockSpec((tm, PLE_DIM), row), pl.BlockSpec((D_MODEL, D_MODEL), fix),
                  pl.BlockSpec((PLE_DIM, D_MODEL), fix), tile, vec],
        out_specs=(tile, tile, tile, tile, vec, vec),
        compiler_params=_params(("arbitrary",)),
    )(dx3, x2bf, p, w_gate, w_proj, pre, g)


def _loss_head(y, target, *, tm, name):
    s = y.shape[0]

    def body(y_ref, t_ref, dy_ref, loss_ref, acc_ref):
        err = y_ref[...] - t_ref[...]
        dy_ref[...] = err * (1.0 / D_MODEL)
        part = jnp.sum(err * err, axis=0, keepdims=True)

        @pl.when(pl.program_id(0) == 0)
        def _():
            acc_ref[...] = part

        @pl.when(pl.program_id(0) > 0)
        def _():
            acc_ref[...] += part

        @pl.when(pl.program_id(0) == pl.num_programs(0) - 1)
        def _():
            tot = jnp.sum(acc_ref[...], axis=1, keepdims=True) * (0.5 / D_MODEL)
            loss_ref[...] = jnp.broadcast_to(tot, (8, LANES))

    row = lambda i: (i, 0)
    return pl.pallas_call(
        body, name=name, grid=(s // tm,),
        out_shape=(jax.ShapeDtypeStruct((s, D_MODEL), F32), jax.ShapeDtypeStruct((8, LANES), F32)),
        in_specs=[pl.BlockSpec((tm, D_MODEL), row), pl.BlockSpec((tm, D_MODEL), row)],
        out_specs=(pl.BlockSpec((tm, D_MODEL), row), pl.BlockSpec((8, LANES), lambda i: (0, 0))),
        scratch_shapes=[pltpu.VMEM((1, D_MODEL), F32)],
        compiler_params=_params(("arbitrary",)),
    )(y, target)


def _rope_tables(s):
    inv = ROPE_THETA ** (-jnp.arange(0, ROPE_DIM, 2, dtype=F32) / ROPE_DIM)
    ang = jnp.arange(s, dtype=F32)[:, None] * inv[None, :]
    cos, sin = jnp.cos(ang), jnp.sin(ang)
    ones = jnp.ones((s, HEAD_DIM - ROPE_DIM), F32)
    c_head = jnp.concatenate([cos, cos, ones], axis=1)
    s_head = jnp.concatenate([-sin, sin, 0.0 * ones], axis=1)
    return jnp.concatenate([c_head, c_head], axis=1), jnp.concatenate([s_head, s_head], axis=1)


def _rope(v, cos, sin):
    n = v.shape[1] // LANES
    width = v.shape[1]
    cos_w = jnp.tile(cos, (1, n)) if n > 1 else cos
    sin_w = jnp.tile(sin, (1, n)) if n > 1 else sin
    dim = lax.broadcasted_iota(jnp.int32, (1, width), 1) % HEAD_DIM
    partner = jnp.where(dim < ROPE_DIM // 2, pltpu.roll(v, width - ROPE_DIM // 2, 1), pltpu.roll(v, ROPE_DIM // 2, 1))
    return v * cos_w + partner * sin_w


def _unrope(dv, cos, sin):
    n = dv.shape[1] // LANES
    width = dv.shape[1]
    cos_w = jnp.tile(cos, (1, n)) if n > 1 else cos
    sin_w = jnp.tile(sin, (1, n)) if n > 1 else sin
    t = dv * sin_w
    dim = lax.broadcasted_iota(jnp.int32, (1, width), 1) % HEAD_DIM
    partner = jnp.where(dim < ROPE_DIM // 2, pltpu.roll(t, width - ROPE_DIM // 2, 1),
                        jnp.where(dim < ROPE_DIM, pltpu.roll(t, ROPE_DIM // 2, 1), 0.0))
    return dv * cos_w + partner


def _att_mask(i, nb):
    rows = GROUP * ATT_BLOCK
    r = lax.broadcasted_iota(jnp.int32, (rows, 3 * ATT_BLOCK), 0) % ATT_BLOCK
    cidx = lax.broadcasted_iota(jnp.int32, (rows, 3 * ATT_BLOCK), 1)
    rel = r + ATT_BLOCK - cidx
    ok = (rel <= ATT_BLOCK) & (rel >= -ATT_BLOCK)
    ok = ok & ((cidx >= ATT_BLOCK) | (i > 0)) & ((cidx < 2 * ATT_BLOCK) | (i < nb - 1))
    return ok


def _att_mask_t(i, nb):
    cols = GROUP * ATT_BLOCK
    cidx = lax.broadcasted_iota(jnp.int32, (3 * ATT_BLOCK, cols), 0)
    r = lax.broadcasted_iota(jnp.int32, (3 * ATT_BLOCK, cols), 1) % ATT_BLOCK
    rel = r + ATT_BLOCK - cidx
    ok = (rel <= ATT_BLOCK) & (rel >= -ATT_BLOCK)
    return ok & ((cidx >= ATT_BLOCK) | (i > 0)) & ((cidx < 2 * ATT_BLOCK) | (i < nb - 1))


def _sink_lanes(sink_ref, h):
    cols = GROUP * ATT_BLOCK
    grp = lax.broadcasted_iota(jnp.int32, (1, cols), 1) // ATT_BLOCK
    out = jnp.zeros((1, cols), F32)
    for gq in range(GROUP):
        out = jnp.where(grp == gq, sink_ref[GROUP * h + gq], out)
    return out


def _half_mask(half):
    lane = lax.broadcasted_iota(jnp.int32, (1, LANES), 1)
    return (lane // HEAD_DIM) == half


def _stack_q(q, h):
    parts = []
    for gq in range(GROUP):
        n = GROUP * h + gq
        grp = q[:, LANES * (n // 2):LANES * (n // 2 + 1)]
        grp = jnp.where(_half_mask(n % 2), grp, 0.0)
        if n % 2 != h % 2:
            grp = pltpu.roll(grp, HEAD_DIM, 1)
        parts.append(grp)
    return jnp.concatenate(parts, axis=0)


def _unstack_q(stacked, h, acc):
    for gq in range(GROUP):
        n = GROUP * h + gq
        grp = stacked[ATT_BLOCK * gq:ATT_BLOCK * (gq + 1), :]
        grp = jnp.where(_half_mask(h % 2), grp, 0.0)
        if n % 2 != h % 2:
            grp = pltpu.roll(grp, HEAD_DIM, 1)
        acc[n // 2] = grp if acc[n // 2] is None else acc[n // 2] + grp
    return acc


def _sink_rows(sink_ref, h):
    rows = GROUP * ATT_BLOCK
    grp = lax.broadcasted_iota(jnp.int32, (rows, 1), 0) // ATT_BLOCK
    out = jnp.zeros((rows, 1), F32)
    for gq in range(GROUP):
        out = jnp.where(grp == gq, sink_ref[GROUP * h + gq], out)
    return out


def _att_probs(qs, kh, sink, valid):
    s = lax.dot_general(qs, kh, (((1,), (1,)), ((), ())), preferred_element_type=F32)
    s = jnp.where(valid, s, NEG)
    m = jnp.maximum(jnp.max(s, axis=-1, keepdims=True), sink)
    p = jnp.exp(s - m)
    es = jnp.exp(sink - m)
    den = jnp.sum(p, axis=-1, keepdims=True) + es
    inv = 1.0 / den
    return p * inv, es * inv


def _att_specs(nb):
    prev = lambda i: (jnp.maximum(i - 1, 0), 0)
    cur = lambda i: (i, 0)
    nxt = lambda i: (jnp.minimum(i + 1, nb - 1), 0)
    kv = lambda f: (lambda i: (f(i)[0], 2))
    tab = [pl.BlockSpec((ATT_BLOCK, LANES), f) for f in (cur, prev, cur, nxt)]
    z = [pl.BlockSpec((ATT_BLOCK, D_MODEL), cur)] + [pl.BlockSpec((ATT_BLOCK, 2 * KV_DIM), kv(f)) for f in (prev, cur, nxt)]
    return z, tab


def _att_load(zq_ref, kp_ref, kc_ref, kn_ref, cq_ref, sq_ref, cp_ref, sp_ref, cc_ref, sc_ref, cn_ref, sn_ref):
    q = (_rope(zq_ref[...], cq_ref[...], sq_ref[...]) * (HEAD_DIM ** -0.5))
    ks, vs = [], []
    for ref, c_ref, s_ref in ((kp_ref, cp_ref, sp_ref), (kc_ref, cc_ref, sc_ref), (kn_ref, cn_ref, sn_ref)):
        kvb = ref[...]
        ks.append(_rope(kvb[:, :KV_DIM], c_ref[...], s_ref[...]))
        vs.append(kvb[:, KV_DIM:])
    return q, jnp.concatenate(ks, axis=0).astype(BF16), jnp.concatenate(vs, axis=0).astype(BF16)


def _att_fwd(z, sink, cos, sin, *, comm=None, name):
    s = z.shape[0]
    nb = s // ATT_BLOCK

    def body(zq_ref, kp_ref, kc_ref, kn_ref, cq_ref, cp_ref, cc_ref, cn_ref, sq_ref, sp_ref, sc_ref, sn_ref, sink_ref,
             o_ref):
        i = pl.program_id(0)
        q, k, v = _att_load(zq_ref, kp_ref, kc_ref, kn_ref, cq_ref, sq_ref, cp_ref, sp_ref, cc_ref, sc_ref, cn_ref, sn_ref)
        valid = _att_mask(i, nb)
        acc = [None] * (N_Q_HEADS // 2)
        for h in range(N_KV_HEADS):
            lanes = slice(LANES * (h // 2), LANES * (h // 2 + 1))
            qs = _stack_q(q, h).astype(BF16)
            prob, _ = _att_probs(qs, k[:, lanes], _sink_rows(sink_ref, h), valid)
            oh = jnp.dot(prob.astype(BF16), v[:, lanes], preferred_element_type=F32)
            acc = _unstack_q(oh, h, acc)
        o_ref[...] = jnp.concatenate(acc, axis=1).astype(BF16)

    zspecs, tab = _att_specs(nb)
    return _call(
        body, name=name, grid=(nb,),
        out_shape=(jax.ShapeDtypeStruct((s, D_MODEL), BF16),),
        in_specs=zspecs + tab + tab + [pl.BlockSpec(memory_space=pltpu.SMEM)],
        out_specs=(pl.BlockSpec((ATT_BLOCK, D_MODEL), lambda i: (i, 0)),),
        args=(z, z, z, z, cos, cos, cos, cos, sin, sin, sin, sin, sink), sem=("parallel",), comm=comm,
        edge=lambda: (pl.program_id(0) == 0, pl.program_id(0) == nb - 1))


def _att_bwd(z, do, sink, cos, sin, *, comm=None, name):
    s = z.shape[0]
    nb = s // ATT_BLOCK

    def body(zq_ref, kp_ref, kc_ref, kn_ref, cq_ref, cp_ref, cc_ref, cn_ref, sq_ref, sp_ref, sc_ref, sn_ref, sink_ref,
             do_ref, dq_ref, part_ref, dsink_ref):
        i = pl.program_id(0)
        q, k, v = _att_load(zq_ref, kp_ref, kc_ref, kn_ref, cq_ref, sq_ref, cp_ref, sp_ref, cc_ref, sc_ref, cn_ref, sn_ref)
        valid = _att_mask_t(i, nb)
        dout = do_ref[...].astype(F32)
        dq_acc = [None] * (N_Q_HEADS // 2)
        dk_acc = [None] * 2
        dv_acc = [None] * 2
        rows = []
        nt = (((1,), (1,)), ((), ()))
        for h in range(N_KV_HEADS):
            grp = h // 2
            lanes = slice(LANES * grp, LANES * (grp + 1))
            qs = _stack_q(q, h).astype(BF16)
            dos = _stack_q(dout, h).astype(BF16)
            sink = _sink_lanes(sink_ref, h)
            sc = jnp.where(valid, lax.dot_general(k[:, lanes], qs, nt, preferred_element_type=F32), NEG)
            m = jnp.maximum(jnp.max(sc, axis=0, keepdims=True), sink)
            p = jnp.exp(sc - m)
            es = jnp.exp(sink - m)
            inv = 1.0 / (jnp.sum(p, axis=0, keepdims=True) + es)
            prob = p * inv
            dprob = lax.dot_general(v[:, lanes], dos, nt, preferred_element_type=F32)
            delta = jnp.sum(prob * dprob, axis=0, keepdims=True)
            dsc = (prob * (dprob - delta)).astype(BF16)
            dsk = -(es * inv) * delta
            for gq in range(GROUP):
                tot = jnp.sum(dsk[:, ATT_BLOCK * gq:ATT_BLOCK * (gq + 1)], axis=1, keepdims=True)
                rows.append(jnp.broadcast_to(tot, (1, LANES)))
            dqs = lax.dot_general(dsc, k[:, lanes], (((0,), (0,)), ((), ())), preferred_element_type=F32)
            dq_acc = _unstack_q(dqs, h, dq_acc)
            dkh = jnp.dot(dsc, qs, preferred_element_type=F32)
            dvh = jnp.dot(prob.astype(BF16), dos, preferred_element_type=F32)
            dk_acc[grp] = dkh if dk_acc[grp] is None else dk_acc[grp] + dkh
            dv_acc[grp] = dvh if dv_acc[grp] is None else dv_acc[grp] + dvh
        dq = jnp.concatenate(dq_acc, axis=1) * (HEAD_DIM ** -0.5)
        dq_ref[...] = _unrope(dq, cq_ref[...], sq_ref[...]).astype(BF16)
        part = jnp.concatenate(dk_acc + dv_acc, axis=1)
        for wdw in range(3):
            part_ref[wdw] = part[ATT_BLOCK * wdw:ATT_BLOCK * (wdw + 1), :]
        dsink = jnp.concatenate(rows, axis=0)

        @pl.when(i == 0)
        def _():
            dsink_ref[...] = dsink

        @pl.when(i > 0)
        def _():
            dsink_ref[...] += dsink

    zspecs, tab = _att_specs(nb)
    return _call(
        body, name=name, grid=(nb,),
        out_shape=(jax.ShapeDtypeStruct((s, D_MODEL), BF16), jax.ShapeDtypeStruct((nb, 3, ATT_BLOCK, 2 * KV_DIM), F32),
                   jax.ShapeDtypeStruct((N_Q_HEADS, LANES), F32)),
        in_specs=zspecs + tab + tab + [pl.BlockSpec(memory_space=pltpu.SMEM), pl.BlockSpec((ATT_BLOCK, D_MODEL), lambda i: (i, 0))],
        out_specs=(pl.BlockSpec((ATT_BLOCK, D_MODEL), lambda i: (i, 0)),
                   pl.BlockSpec((None, 3, ATT_BLOCK, 2 * KV_DIM), lambda i: (i, 0, 0, 0)),
                   pl.BlockSpec((N_Q_HEADS, LANES), lambda i: (0, 0))),
        args=(z, z, z, z, cos, cos, cos, cos, sin, sin, sin, sin, sink, do), sem=("arbitrary",), comm=comm,
        edge=lambda: (pl.program_id(0) == 0, pl.program_id(0) == nb - 1))


def _att_bwd_kv(part, cos, sin, *, name):
    nb = part.shape[0]

    def body(pn_ref, pc_ref, pp_ref, c_ref, s_ref, o_ref):
        j = pl.program_id(0)
        tot = pc_ref[...]
        tot = tot + jnp.where(j < nb - 1, pn_ref[...], 0.0)
        tot = tot + jnp.where(j > 0, pp_ref[...], 0.0)
        dk = _unrope(tot[:, :KV_DIM], c_ref[...], s_ref[...])
        o_ref[...] = jnp.concatenate([dk, tot[:, KV_DIM:]], axis=1).astype(BF16)

    blk = (None, None, ATT_BLOCK, 2 * KV_DIM)
    return pl.pallas_call(
        body, name=name, grid=(nb,),
        out_shape=jax.ShapeDtypeStruct((nb * ATT_BLOCK, 2 * KV_DIM), BF16),
        in_specs=[pl.BlockSpec(blk, lambda j: (jnp.minimum(j + 1, nb - 1), 0, 0, 0)),
                  pl.BlockSpec(blk, lambda j: (j, 1, 0, 0)),
                  pl.BlockSpec(blk, lambda j: (jnp.maximum(j - 1, 0), 2, 0, 0)),
                  pl.BlockSpec((ATT_BLOCK, LANES), lambda j: (j, 0)), pl.BlockSpec((ATT_BLOCK, LANES), lambda j: (j, 0))],
        out_specs=pl.BlockSpec((ATT_BLOCK, 2 * KV_DIM), lambda j: (j, 0)),
        compiler_params=_params(("parallel",)),
    )(part, part, part, cos, sin)


def _bdot(a, b, dims):
    return lax.dot_general(a.astype(BF16), b.astype(BF16), (dims, ((), ())), preferred_element_type=F32)


@jax.custom_vjp
def _dot_nn(a, b):
    return _bdot(a, b, ((1,), (0,)))


@jax.custom_vjp
def _dot_nt(a, b):
    return _bdot(a, b, ((1,), (1,)))


@jax.custom_vjp
def _dot_tn(a, b):
    return _bdot(a, b, ((0,), (0,)))


_dot_nn.defvjp(lambda a, b: (_dot_nn(a, b), (a, b)), lambda r, d: (_dot_nt(d, r[1]), _dot_tn(r[0], d)))
_dot_nt.defvjp(lambda a, b: (_dot_nt(a, b), (a, b)), lambda r, d: (_dot_nn(d, r[1]), _dot_tn(d, r[0])))
_dot_tn.defvjp(lambda a, b: (_dot_tn(a, b), (a, b)), lambda r, d: (_dot_nt(r[1], d), _dot_nn(r[0], d)))


def _running_sum(v, up):
    n = v.shape[0]
    rows = lax.broadcasted_iota(jnp.int32, v.shape, 0)
    sh = 1
    while sh < n:
        if up:
            v = v + jnp.where(rows < n - sh, pltpu.roll(v, n - sh, 0), 0.0)
        else:
            v = v + jnp.where(rows >= sh, pltpu.roll(v, sh, 0), 0.0)
        sh *= 2
    return v


@jax.custom_vjp
def _sum_down(v):
    return _running_sum(v, False)


@jax.custom_vjp
def _sum_up(v):
    return _running_sum(v, True)


_sum_down.defvjp(lambda v: (_running_sum(v, False), None), lambda _, d: (_sum_up(d),))
_sum_up.defvjp(lambda v: (_running_sum(v, True), None), lambda _, d: (_sum_down(d),))

N_SUB = HG_CHUNK // HG_SUB


def _fold_blocks(v):
    out = v[:HG_CHUNK]
    for i in range(1, N_SUB):
        out = out + v[HG_CHUNK * i:HG_CHUNK * (i + 1)]
    return out


@jax.custom_vjp
def _fold(v):
    return _fold_blocks(v)


_fold.defvjp(lambda v: (_fold_blocks(v), None), lambda _, d: (jnp.concatenate([d] * N_SUB, axis=0),))


def _hg_consts(rev):
    c, sub = HG_CHUNK, HG_SUB
    rowpos = lax.broadcasted_iota(jnp.int32, (c, HG_DIM), 0)
    rr = lax.broadcasted_iota(jnp.int32, (N_SUB * c, c), 0)
    key = lax.broadcasted_iota(jnp.int32, (N_SUB * c, c), 1)
    blk, qry = rr // c, rr % c
    if rev:
        rowpos, qry, key = c - 1 - rowpos, c - 1 - qry, c - 1 - key
    keep = (key // sub == blk) & (key <= qry)
    return keep, rowpos


def _pick(b, rowpos, t):
    return jnp.sum(jnp.where(rowpos == t, b, 0.0), axis=0, keepdims=True)


def _hg_local(zq, zf, zv, lbv, consts, dots):
    dot_nn, dot_nt, dot_tn, cum, fold = dots
    keep, rowpos = consts
    sig = _sigmoid(zf)
    f = lbv + (1.0 - lbv) * sig
    g = jnp.log(f)
    k = (1.0 - lbv) * (1.0 - sig)
    q = zq * _sigmoid(zq)
    b = cum(g)
    ends = [_pick(b, rowpos, (j + 1) * HG_SUB - 1) for j in range(N_SUB)]
    b_last = ends[-1]
    b_end = b_last
    for j in range(N_SUB - 1):
        b_end = jnp.where(rowpos // HG_SUB == j, ends[j], b_end)
    kc = k * jnp.exp(b_end - b)
    qbs = [q * jnp.exp(jnp.where(rowpos >= j * HG_SUB, b - ends[j], 0.0)) for j in range(N_SUB)]
    scores = fold(jnp.where(keep, dot_nt(jnp.concatenate(qbs, axis=0), kc), 0.0))
    return dot_nn(scores, zv), q * jnp.exp(b), k * jnp.exp(b_last - b), jnp.exp(b_last)


def _hg_chunk(zq, zf, zv, lbv, st, consts, dots):
    intra, qs, kd, dec = _hg_local(zq, zf, zv, lbv, consts, dots)
    return intra + dots[1](qs, st), dec * st + dots[2](zv, kd)


def _hg_dots(diff, rev):
    if diff:
        return _dot_nn, _dot_nt, _dot_tn, (_sum_up if rev else _sum_down), _fold
    return (lambda a, b: _bdot(a, b, ((1,), (0,))), lambda a, b: _bdot(a, b, ((1,), (1,))),
            lambda a, b: _bdot(a, b, ((0,), (0,))), lambda v: _running_sum(v, rev), _fold_blocks)


def _hg_specs(ts, nch, trow):
    tile = pl.BlockSpec((ts, HG_DIM), lambda h, t: (trow(t), h))
    mats = pl.BlockSpec((None, nch, HG_DIM, HG_DIM), lambda h, t: (h, trow(t), 0, 0))
    vecs = pl.BlockSpec((None, nch, 1, HG_DIM), lambda h, t: (h, trow(t), 0, 0))
    return tile, mats, vecs


def _time_order(nch, rev):
    return range(nch - 1, -1, -1) if rev else range(nch)


def _chunk_rows(c):
    return pl.ds(c * HG_CHUNK, HG_CHUNK)


def _hg_edge(nt):
    h, t = pl.program_id(0), pl.program_id(1)
    return (h == 0) & (t == 0), (h == HG_HEADS - 1) & (t == nt - 1)


def _hg_fwd(z, lb, *, rev, ts, comm=None, name):
    s = z.shape[0]
    nt = s // ts
    nch = ts // HG_CHUNK
    fcol = HG_HEADS * (2 if rev else 1)

    def body(zq_ref, zf_ref, zv_ref, lb_ref, o_ref, st_ref, qs_ref, dec_ref, state_ref):
        @pl.when(pl.program_id(1) == 0)
        def _():
            state_ref[...] = jnp.zeros_like(state_ref)

        consts = _hg_consts(rev)
        dots = _hg_dots(False, rev)
        lbv = lb_ref[...]
        local = {}
        for c in range(nch):
            rows = _chunk_rows(c)
            zv = zv_ref[rows, :]
            intra, qs, kd, dec = _hg_local(zq_ref[rows, :], zf_ref[rows, :], zv, lbv, consts, dots)
            qs = qs.astype(BF16)
            qs_ref[rows, :] = qs
            dec_ref[c] = dec
            local[c] = (intra, qs, dec, dots[2](zv, kd))
        st = state_ref[...]
        for c in _time_order(nch, rev):
            intra, qs, dec, upd = local[c]
            st_ref[c] = st.astype(BF16)
            o_ref[_chunk_rows(c), :] = intra + _bdot(qs, st, ((1,), (1,)))
            st = dec * st + upd
        state_ref[...] = st

    trow = (lambda t: nt - 1 - t) if rev else (lambda t: t)
    col = lambda off: pl.BlockSpec((ts, HG_DIM), lambda h, t: (trow(t), off + h))
    tile, mats, vecs = _hg_specs(ts, nch, trow)
    nchunks = s // HG_CHUNK
    return _call(
        body, name=name, grid=(HG_HEADS, nt),
        out_shape=(jax.ShapeDtypeStruct((s, D_MODEL), F32),
                   jax.ShapeDtypeStruct((HG_HEADS, nchunks, HG_DIM, HG_DIM), BF16),
                   jax.ShapeDtypeStruct((s, D_MODEL), BF16),
                   jax.ShapeDtypeStruct((HG_HEADS, nchunks, 1, HG_DIM), F32)),
        in_specs=[col(0), col(fcol), col(3 * HG_HEADS), pl.BlockSpec((None, 1, HG_DIM), lambda h, t: (h, 0, 0))],
        out_specs=(tile, mats, tile, vecs), args=(z, z, z, lb),
        scratch_shapes=[pltpu.VMEM((HG_DIM, HG_DIM), F32)], sem=("parallel", "arbitrary"), comm=comm,
        edge=lambda: _hg_edge(nt))


def _hg_bwd(z, lb, states, qs, dec, dout, addq, addv, *, rev, ts, comm=None, name):
    s = z.shape[0]
    nt = s // ts
    nch = ts // HG_CHUNK
    fcol = HG_HEADS * (2 if rev else 1)
    has_add = addq is not None

    def body(*refs):
        zq_ref, zf_ref, zv_ref, lb_ref, st_ref, qs_ref, dec_ref, do_ref = refs[:8]
        aq_ref, av_ref = (refs[8], refs[9]) if has_add else (None, None)
        dq_ref, df_ref, dv_ref, dlb_ref, grad_ref = refs[-5:]

        @pl.when(pl.program_id(1) == 0)
        def _():
            grad_ref[...] = jnp.zeros_like(grad_ref)

        consts = _hg_consts(rev)
        dots = _hg_dots(True, rev)
        lbv = lb_ref[...]
        prods = {c: _bdot(do_ref[_chunk_rows(c), :], qs_ref[_chunk_rows(c), :], ((0,), (0,))) for c in range(nch)}
        gleave = {}
        gr = grad_ref[...]
        for c in reversed(_time_order(nch, rev)):
            gleave[c] = gr
            gr = dec_ref[c] * gr + prods[c]
        grad_ref[...] = gr
        dlb_blk = jnp.zeros((1, HG_DIM), F32)
        for c in range(nch):
            rows = _chunk_rows(c)
            fn = lambda a, b2, c2, d2, e2: _hg_chunk(a, b2, c2, d2, e2, consts, dots)
            _, pull = jax.vjp(fn, zq_ref[rows, :], zf_ref[rows, :], zv_ref[rows, :], lbv, st_ref[c].astype(F32))
            dq, df, dv, dlb, _ = pull((do_ref[rows, :], gleave[c]))
            if has_add:
                dq = dq + aq_ref[rows, :]
                dv = dv + av_ref[rows, :]
            dq_ref[rows, :] = dq.astype(dq_ref.dtype)
            df_ref[rows, :] = df.astype(BF16)
            dv_ref[rows, :] = dv.astype(dv_ref.dtype)
            dlb_blk = dlb_blk + dlb

        @pl.when(pl.program_id(1) == 0)
        def _():
            dlb_ref[...] = dlb_blk

        @pl.when(pl.program_id(1) > 0)
        def _():
            dlb_ref[...] += dlb_blk

    trow = (lambda t: t) if rev else (lambda t: nt - 1 - t)
    col = lambda off: pl.BlockSpec((ts, HG_DIM), lambda h, t: (trow(t), off + h))
    tile, mats, vecs = _hg_specs(ts, nch, trow)
    in_specs = [col(0), col(fcol), col(3 * HG_HEADS), pl.BlockSpec((None, 1, HG_DIM), lambda h, t: (h, 0, 0)),
                mats, tile, vecs, tile]
    args = [z, z, z, lb, states, qs, dec, dout]
    if has_add:
        in_specs += [tile, tile]
        args += [addq, addv]
    act = lambda dt: jax.ShapeDtypeStruct((s, D_MODEL), dt)
    sums = BF16 if has_add else F32
    return _call(
        body, name=name, grid=(HG_HEADS, nt),
        out_shape=(act(sums), act(BF16), act(sums), jax.ShapeDtypeStruct((HG_HEADS, 1, HG_DIM), F32)),
        in_specs=in_specs,
        out_specs=(tile, tile, tile, pl.BlockSpec((None, 1, HG_DIM), lambda h, t: (h, 0, 0))), args=tuple(args),
        scratch_shapes=[pltpu.VMEM((HG_DIM, HG_DIM), F32)], sem=("parallel", "arbitrary"), comm=comm,
        edge=lambda: _hg_edge(nt))


def _hg_post(of, ob, z, norm_g, *, tm, name):
    s = of.shape[0]

    def body(of_ref, ob_ref, gate_ref, ng_ref, y_ref):
        gn = ng_ref[...]
        for h in range(HG_HEADS):
            ln = slice(HG_DIM * h, HG_DIM * (h + 1))
            o = of_ref[:, ln] + ob_ref[:, ln]
            r = lax.rsqrt(jnp.mean(o * o, axis=-1, keepdims=True) + LN_EPS)
            gt = gate_ref[:, ln]
            y_ref[:, ln] = (o * r * gn * gt * _sigmoid(gt)).astype(BF16)

    row = lambda i: (i, 0)
    return pl.pallas_call(
        body, name=name, grid=(s // tm,),
        out_shape=jax.ShapeDtypeStruct((s, D_MODEL), BF16),
        in_specs=[pl.BlockSpec((tm, D_MODEL), row), pl.BlockSpec((tm, D_MODEL), row),
                  pl.BlockSpec((tm, D_MODEL), lambda i: (i, 4)), pl.BlockSpec((1, HG_DIM), lambda i: (0, 0))],
        out_specs=pl.BlockSpec((tm, D_MODEL), row),
        compiler_params=_params(("parallel",)),
    )(of, ob, z, norm_g)


def _hg_post_bwd(dy, of, ob, z, norm_g, *, tm, name):
    s = of.shape[0]

    def body(dy_ref, of_ref, ob_ref, gate_ref, ng_ref, do_ref, dgate_ref, dng_ref):
        gn = ng_ref[...]
        tot = jnp.zeros((1, HG_DIM), F32)
        for h in range(HG_HEADS):
            ln = slice(HG_DIM * h, HG_DIM * (h + 1))
            d = dy_ref[:, ln].astype(F32)
            o = of_ref[:, ln] + ob_ref[:, ln]
            r = lax.rsqrt(jnp.mean(o * o, axis=-1, keepdims=True) + LN_EPS)
            ohat = o * r
            gt = gate_ref[:, ln]
            sg = _sigmoid(gt)
            don = d * gt * sg
            dgate_ref[:, ln] = (d * ohat * gn * sg * (1.0 + gt * (1.0 - sg))).astype(BF16)
            tot = tot + jnp.sum(don * ohat, axis=0, keepdims=True)
            dohat = don * gn
            do_ref[:, ln] = r * (dohat - ohat * jnp.mean(dohat * ohat, axis=-1, keepdims=True))

        @pl.when(pl.program_id(0) == 0)
        def _():
            dng_ref[...] = tot

        @pl.when(pl.program_id(0) > 0)
        def _():
            dng_ref[...] += tot

    row = lambda i: (i, 0)
    return pl.pallas_call(
        body, name=name, grid=(s // tm,),
        out_shape=(jax.ShapeDtypeStruct((s, D_MODEL), F32), jax.ShapeDtypeStruct((s, D_MODEL), BF16),
                   jax.ShapeDtypeStruct((1, HG_DIM), F32)),
        in_specs=[pl.BlockSpec((tm, D_MODEL), row), pl.BlockSpec((tm, D_MODEL), row), pl.BlockSpec((tm, D_MODEL), row),
                  pl.BlockSpec((tm, D_MODEL), lambda i: (i, 4)), pl.BlockSpec((1, HG_DIM), lambda i: (0, 0))],
        out_specs=(pl.BlockSpec((tm, D_MODEL), row), pl.BlockSpec((tm, D_MODEL), row),
                   pl.BlockSpec((1, HG_DIM), lambda i: (0, 0))),
        compiler_params=_params(("arbitrary",)),
    )(dy, of, ob, z, norm_g)


def _lb_fwd(logits, *, name):
    w = logits.shape[1]

    def body(l_ref, o_ref):
        lg = l_ref[...]
        e = jnp.exp(lg - jnp.max(lg, axis=0, keepdims=True))
        sm = e / jnp.sum(e, axis=0, keepdims=True)
        o_ref[0:1, :] = sm[1:2]
        o_ref[1:2, :] = sm[1:2] + sm[2:3] + sm[3:4]

    return pl.pallas_call(body, name=name, out_shape=jax.ShapeDtypeStruct((2, w), F32))(logits)


def _lb_bwd(logits, dlb, *, name):
    w = logits.shape[1]

    def body(l_ref, d_ref, o_ref):
        lg = l_ref[...]
        e = jnp.exp(lg - jnp.max(lg, axis=0, keepdims=True))
        sm = e / jnp.sum(e, axis=0, keepdims=True)
        d1, d3 = d_ref[0:1, :], d_ref[1:2, :]
        dot = sm[1:2] * (d1 + d3) + (sm[2:3] + sm[3:4]) * d3
        o_ref[0:1, :] = -sm[0:1] * dot
        o_ref[1:2, :] = sm[1:2] * (d1 + d3 - dot)
        o_ref[2:3, :] = sm[2:3] * (d3 - dot)
        o_ref[3:4, :] = sm[3:4] * (d3 - dot)

    return pl.pallas_call(body, name=name, out_shape=jax.ShapeDtypeStruct((4, w), F32))(logits, dlb)


def _adamw(w, g, m, v, *, tr, g_off=0, name):
    rows = w.shape[0]
    parts = g.ndim == 3
    c1 = 1.0 / (1.0 - ADAM_B1 ** ADAM_STEP)
    c2 = 1.0 / (1.0 - ADAM_B2 ** ADAM_STEP)

    def body(w_ref, g_ref, m_ref, v_ref, go_ref, d_ref, mo_ref, vo_ref):
        if parts:
            gg = g_ref[0].astype(F32)
            for i in range(1, N_DEV):
                gg = gg + g_ref[i].astype(F32)
        else:
            gg = g_ref[...]
        mm = ADAM_B1 * m_ref[...] + (1.0 - ADAM_B1) * gg
        vv = ADAM_B2 * v_ref[...] + (1.0 - ADAM_B2) * (gg * gg)
        go_ref[...] = gg
        mo_ref[...] = mm
        vo_ref[...] = vv
        d_ref[...] = -ADAM_LR * ((mm * c1) / (jnp.sqrt(vv * c2) + ADAM_EPS) + ADAM_WD * w_ref[...])

    tile = pl.BlockSpec((tr, D_MODEL), lambda i: (i, 0))
    gspec = pl.BlockSpec((N_DEV, tr, D_MODEL), lambda i: (0, i + g_off // tr, 0)) if parts else tile
    out = jax.ShapeDtypeStruct((rows, D_MODEL), F32)
    return pl.pallas_call(
        body, name=name, grid=(rows // tr,),
        out_shape=(out, out, out, out),
        in_specs=[tile, gspec, tile, tile], out_specs=(tile, tile, tile, tile),
        compiler_params=_params(("parallel",)),
    )(w, g, m, v)


def _sum8(parts, *, name):
    def body(p_ref, o_ref):
        tot = p_ref[0]
        for i in range(1, N_DEV):
            tot = tot + p_ref[i]
        o_ref[...] = tot

    return pl.pallas_call(body, name=name, out_shape=jax.ShapeDtypeStruct(parts.shape[1:], parts.dtype))(parts)


def _layer_params(i):
    j = i // 2
    mix = [("att_w_qkv", j, 1), ("att_w_o", j, 0)] if i % 2 == 0 else [("hgrn_w_in", j, 1), ("hgrn_w_o", j, 0)]
    return mix + [("ffn_w_in", i, 1), ("ffn_w_out", i, 0), ("ple_w_gate", i, 0), ("ple_w_proj", i, 1)]


def _pack_local(tree, params):
    return jnp.concatenate([tree[n][j].reshape(-1, D_MODEL) for n, j, _ in params], axis=0)


def _unpack_local(packed, params, like):
    out, r = {}, 0
    for n, _, _ in params:
        shp = like[n].shape[1:]
        k = shp[0] * shp[1] // D_MODEL
        out[n] = packed[r:r + k].reshape(shp)
        r += k
    return out


def _unpack_gathered(gathered, i, like):
    out, r = {}, 0
    for n, _, ax in _layer_params(i):
        shp = like[n].shape[1:]
        k = shp[0] * shp[1] // D_MODEL
        t = gathered[:, r:r + k].reshape((N_DEV,) + shp)
        out[n] = (jnp.moveaxis(t, 0, 1).reshape(shp[0], N_DEV * shp[1]) if ax == 1
                  else t.reshape(N_DEV * shp[0], shp[1]))
        r += k
    return out


def _pack_full(grads, params, like):
    cols = []
    for n, _, ax in params:
        shp = like[n].shape[1:]
        t = (jnp.moveaxis(grads[n].reshape(shp[0], N_DEV, shp[1]), 1, 0) if ax == 1
             else grads[n].reshape(N_DEV, shp[0], shp[1]))
        cols.append(t.reshape(N_DEV, -1, D_MODEL).astype(BF16))
    return jnp.concatenate(cols, axis=1)


def _row_tile(rows):
    return max(t for t in range(16, 257, 16) if rows % t == 0)


SMALL_ROWS = 24


def _pad_row(a):
    flat = a.reshape(1, -1)
    return jnp.pad(flat, ((0, 0), (0, D_MODEL - flat.shape[1])))


def _tile(n, pref):
    return min(n, pref)


def kernel(x, p, att_w_qkv, att_sink, att_w_o, hgrn_w_in, hgrn_lb_logits, hgrn_norm_g, hgrn_w_o, ln_mix_g, ln_mix_b, ffn_w_in, ffn_w_out, ln_ffn_g, ln_ffn_b, ple_w_gate, ple_w_proj, loss_target, m_att_w_qkv, m_att_sink, m_att_w_o, m_hgrn_w_in, m_hgrn_lb_logits, m_hgrn_norm_g, m_hgrn_w_o, m_ln_mix_g, m_ln_mix_b, m_ffn_w_in, m_ffn_w_out, m_ln_ffn_g, m_ln_ffn_b, m_ple_w_gate, m_ple_w_proj, v_att_w_qkv, v_att_sink, v_att_w_o, v_hgrn_w_in, v_hgrn_lb_logits, v_hgrn_norm_g, v_hgrn_w_o, v_ln_mix_g, v_ln_mix_b, v_ffn_w_in, v_ffn_w_out, v_ln_ffn_g, v_ln_ffn_b, v_ple_w_gate, v_ple_w_proj):
    names = ["att_w_qkv", "att_sink", "att_w_o", "hgrn_w_in", "hgrn_lb_logits", "hgrn_norm_g", "hgrn_w_o", "ln_mix_g",
             "ln_mix_b", "ffn_w_in", "ffn_w_out", "ln_ffn_g", "ln_ffn_b", "ple_w_gate", "ple_w_proj"]
    w = dict(zip(names, (att_w_qkv, att_sink, att_w_o, hgrn_w_in, hgrn_lb_logits, hgrn_norm_g, hgrn_w_o, ln_mix_g,
                         ln_mix_b, ffn_w_in, ffn_w_out, ln_ffn_g, ln_ffn_b, ple_w_gate, ple_w_proj)))
    mom = dict(zip(names, (m_att_w_qkv, m_att_sink, m_att_w_o, m_hgrn_w_in, m_hgrn_lb_logits, m_hgrn_norm_g, m_hgrn_w_o,
                           m_ln_mix_g, m_ln_mix_b, m_ffn_w_in, m_ffn_w_out, m_ln_ffn_g, m_ln_ffn_b, m_ple_w_gate,
                           m_ple_w_proj)))
    var = dict(zip(names, (v_att_w_qkv, v_att_sink, v_att_w_o, v_hgrn_w_in, v_hgrn_lb_logits, v_hgrn_norm_g, v_hgrn_w_o,
                           v_ln_mix_g, v_ln_mix_b, v_ffn_w_in, v_ffn_w_out, v_ln_ffn_g, v_ln_ffn_b, v_ple_w_gate,
                           v_ple_w_proj)))
    s = x.shape[1]
    me = 4 * lax.axis_index("x") + 2 * lax.axis_index("y") + lax.axis_index("c")
    tm = _tile(s, 512)
    tbig = _tile(s, 1024)
    ts = _tile(s // 2, 512)
    x0 = x.reshape(s, D_MODEL)
    target = loss_target.reshape(s, D_MODEL)
    pl_in = p.reshape(DEPTH, s, PLE_DIM)

    w_rows = [_pack_local(w, _layer_params(i)).astype(BF16) for i in range(DEPTH)]
    full = _unpack_gathered(_gather(w_rows[0], name="gather_weights"), 0, w)
    lb_rows = jnp.pad(hgrn_lb_logits.reshape(8, HG_DIM), ((0, 0), (0, D_MODEL - HG_DIM)))
    lb_all = _gather(lb_rows, name="gather_lb")[:, :, :HG_DIM]
    logits_full = jnp.moveaxis(lb_all, 0, 1).reshape(DEPTH, 2 * D_MODEL)
    lb = _lb_fwd(logits_full, name="lb_fwd")
    cos, sin = _rope_tables(s)

    saved = []
    xf, xb = x0, x0
    for i in range(DEPTH):
        j = i // 2
        sv = {"x": xf, "xb": xb, "w": full}
        nxt = (w_rows[i + 1], True) if i + 1 < DEPTH else None
        if i % 2 == 0:
            z = _mm(xb, full["att_w_qkv"], tm=tbig, tn=512, tk=D_MODEL, name="att_in")
            o, *more = _att_fwd(z, w["att_sink"][j], cos, sin, comm=nxt, name="att_fwd")
            w_o = full["att_w_o"]
        else:
            z = _mm(xb, full["hgrn_w_in"], tm=tbig, tn=1024, tk=D_MODEL, name="hgrn_in")
            lbl = lb[j].reshape(2, HG_HEADS, 1, HG_DIM)
            of, st_f, qs_f, dec_f, *more = _hg_fwd(z, lbl[0], rev=False, ts=ts, comm=nxt, name="hgrn_fwd")
            ob, st_b, qs_b, dec_b = _hg_fwd(z, lbl[1], rev=True, ts=ts, name="hgrn_fwd_rev")
            o = _hg_post(of, ob, z, w["hgrn_norm_g"][j].reshape(1, HG_DIM), tm=tm, name="hgrn_post")
            w_o = full["hgrn_w_o"]
            sv.update(of=of, ob=ob, st_f=st_f, st_b=st_b, lbl=lbl, qs_f=qs_f, qs_b=qs_b, dec_f=dec_f, dec_b=dec_b)
        sv.update(z=z, o=o)
        pre1, x1, x1b = _proj_ln(o, w_o, xf, w["ln_mix_g"][i:i + 1], w["ln_mix_b"][i:i + 1], tm=tm, name="mix_out_ln")
        gg, uu, act = _ffn_in(x1b, full["ffn_w_in"], tm=tm, tn=FF_TILE, name="ffn_in")
        pre2, x2b, xf, xb = _ffn_out_ple(act, full["ffn_w_out"], x1, w["ln_ffn_g"][i:i + 1], w["ln_ffn_b"][i:i + 1],
                                         pl_in[i], full["ple_w_gate"], full["ple_w_proj"], tm=tm, name="ffn_out_ple")
        sv.update(pre1=pre1, x1=x1, x1b=x1b, g=gg, u=uu, act=act, pre2=pre2, x2b=x2b)
        saved.append(sv)
        if nxt is not None:
            full = _unpack_gathered(more[0], i + 1, w)

    dx, loss_blk = _loss_head(xf, target, tm=tm, name="loss_head")
    loss = lax.psum(loss_blk[0, 0], AXES)

    small = {n: [None] * DEPTH for n in ("ln_mix_g", "ln_mix_b", "ln_ffn_g", "ln_ffn_b")}
    dlb_rows = [None] * 4
    dnorm, dsink = [None] * 2, [None] * 2
    recv_late, recv_early = [None] * DEPTH, [None] * DEPTH
    above = None
    mmw = functools.partial(_mm, ta=True, tk=_tile(s, 2048), out_dtype=BF16)
    for i in reversed(range(DEPTH)):
        j = i // 2
        sv = saved[i]
        full, gl = sv["w"], {}
        da, dpp, dy2, dy2b, small["ln_ffn_g"][i], small["ln_ffn_b"][i] = _ple_ln_bwd(
            dx, sv["x2b"], pl_in[i], full["ple_w_gate"], full["ple_w_proj"], sv["pre2"], w["ln_ffn_g"][i:i + 1],
            tm=tm, name="ple_ln_bwd")
        gl["ple_w_gate"] = mmw(sv["x2b"], da, tm=D_MODEL, tn=D_MODEL, name="dw_ple_gate")
        gl["ple_w_proj"] = mmw(pl_in[i], dpp, tm=PLE_DIM, tn=D_MODEL, name="dw_ple_proj")
        dg, du = _ffn_bwd_act(dy2b, full["ffn_w_out"], sv["g"], sv["u"], tm=tm, tn=FF_TILE, name="ffn_bwd_act")
        gl["ffn_w_out"] = mmw(sv["act"], dy2b, tm=FF_TILE, tn=D_MODEL, name="dw_ffn_out")
        dy1, dy1b, small["ln_mix_g"][i], small["ln_mix_b"][i] = _mm(
            [dg, du], full["ffn_w_in"], tm=tm, tn=D_MODEL, tk=FF_TILE, tb=True, add=dy2, add_scale=ALPHA,
            ln=(sv["pre1"], w["ln_mix_g"][i:i + 1]), name="ffn_bwd_x_ln")
        gl["ffn_w_in"] = mmw(sv["x1b"], [dg, du], tm=D_MODEL, tn=FF_TILE, name="dw_ffn_in")
        n_out, n_inw = ("att_w_o", "att_w_qkv") if i % 2 == 0 else ("hgrn_w_o", "hgrn_w_in")
        do = _mm(dy1b, full[n_out], tm=tbig, tn=D_MODEL, tk=D_MODEL, tb=True, out_dtype=BF16, name="mix_out_bwd")
        gl[n_out] = mmw(sv["o"], dy1b, tm=D_MODEL, tn=D_MODEL, name="dw_mix_out")
        early = _pack_full(gl, _layer_params(i)[1:], w)
        comm = (early if above is None else jnp.concatenate([above, early], axis=1), False)
        if i % 2 == 0:
            dzq, part, dsk, *more = _att_bwd(sv["z"], do, w["att_sink"][j], cos, sin, comm=comm, name="att_bwd")
            dz = [dzq, _att_bwd_kv(part, cos, sin, name="att_bwd_kv")]
            dsink[j] = dsk[:, 0]
        else:
            dsum, dgate, dnorm[j] = _hg_post_bwd(do, sv["of"], sv["ob"], sv["z"], w["hgrn_norm_g"][j].reshape(1, HG_DIM),
                                                 tm=tm, name="hgrn_post_bwd")
            dq1, df1, dv1, dlb1, *more = _hg_bwd(sv["z"], sv["lbl"][0], sv["st_f"], sv["qs_f"], sv["dec_f"], dsum, None,
                                                 None, rev=False, ts=ts, comm=comm, name="hgrn_bwd")
            dq2, df2, dv2, dlb2 = _hg_bwd(sv["z"], sv["lbl"][1], sv["st_b"], sv["qs_b"], sv["dec_b"], dsum, dq1, dv1,
                                          rev=True, ts=ts, name="hgrn_bwd_rev")
            dz = [dq2, df1, df2, dv2, dgate]
            dlb_rows[2 * j] = dlb1.reshape(1, D_MODEL)
            dlb_rows[2 * j + 1] = dlb2.reshape(1, D_MODEL)
        if above is not None:
            recv_late[i + 1] = (more[0], 0)
        recv_early[i] = (more[0], 0 if above is None else above.shape[1])
        dx = _dx_from_pieces(dz, full[n_inw], dy1, tm=tm, name="mix_in_bwd")
        gl[n_inw] = mmw(sv["xb"], dz, tm=D_MODEL, tn=512, name="dw_mix_in")
        above = _pack_full(gl, _layer_params(i)[:1], w)
    grad_x = dx.reshape(x.shape)
    recv_late[0] = (_exchange(above, name="exchange_grads"), 0)

    big_out = [{n: [None] * w[n].shape[0] for n, _ in BIG} for _ in range(4)]
    for i in range(DEPTH):
        for params, (got, off) in ((_layer_params(i)[:1], recv_late[i]), (_layer_params(i)[1:], recv_early[i])):
            w_part = _pack_local(w, params)
            outs = _adamw(w_part, got, _pack_local(mom, params), _pack_local(var, params),
                          tr=_row_tile(math.gcd(w_part.shape[0], off)), g_off=off, name="adamw_big")
            for kind, packed in enumerate(outs):
                for (n, j, _), piece in zip(params, _unpack_local(packed, params, w).values()):
                    big_out[kind][n][j] = piece
    big_out = [{n: jnp.stack(v) for n, v in kind.items()} for kind in big_out]

    small_rows = jnp.concatenate(
        [jnp.concatenate(small[n], axis=0) for n in ("ln_mix_g", "ln_mix_b", "ln_ffn_g", "ln_ffn_b")] + dlb_rows
        + [_pad_row(jnp.stack(dnorm)), _pad_row(jnp.stack(dsink)), jnp.zeros((2, D_MODEL), F32)], axis=0)
    small_all = _gather(small_rows, name="gather_small")
    lbw, lbm, lbv = (t.reshape(4, 2 * HG_DIM) for t in (hgrn_lb_logits, mom["hgrn_lb_logits"], var["hgrn_lb_logits"]))
    summed = _sum8(small_all, name="sum_small")
    dlb_mine = lax.dynamic_slice_in_dim(summed[16:20].reshape(2, 2, HG_HEADS, HG_DIM), me, 1, axis=2)
    dlogits = _lb_bwd(lbw, dlb_mine.reshape(2, 2 * HG_DIM), name="lb_bwd")

    def small_pack(ln4, lbt, ng, sk):
        return jnp.concatenate([ln4[n] for n in ("ln_mix_g", "ln_mix_b", "ln_ffn_g", "ln_ffn_b")]
                               + [_pad_row(lbt), _pad_row(ng), _pad_row(sk), jnp.zeros((5, D_MODEL), F32)], axis=0)

    g_small = jnp.concatenate([summed[:16], _pad_row(dlogits), summed[20:22], jnp.zeros((5, D_MODEL), F32)], axis=0)
    souts = _adamw(small_pack(w, lbw, w["hgrn_norm_g"], w["att_sink"]), g_small,
                   small_pack(mom, lbm, mom["hgrn_norm_g"], mom["att_sink"]),
                   small_pack(var, lbv, var["hgrn_norm_g"], var["att_sink"]), tr=SMALL_ROWS, name="adamw_small")

    def small_unpack(t):
        out = {n: t[4 * k:4 * k + 4] for k, n in enumerate(("ln_mix_g", "ln_mix_b", "ln_ffn_g", "ln_ffn_b"))}
        out["hgrn_lb_logits"] = t[16].reshape(hgrn_lb_logits.shape)
        out["hgrn_norm_g"] = t[17, :2 * HG_DIM].reshape(hgrn_norm_g.shape)
        out["att_sink"] = t[18, :2 * N_Q_HEADS].reshape(att_sink.shape)
        return out

    result = [loss, grad_x]
    for big_t, small_t in zip(big_out, souts):
        merged = dict(big_t)
        merged.update(small_unpack(small_t))
        result += [merged[n] for n in names]
    return tuple(result)
```

```python
import functools
import math

import jax
import jax.numpy as jnp
from jax import lax
from jax.experimental import pallas as pl
from jax.experimental.pallas import tpu as pltpu

F32 = jnp.float32
BF16 = jnp.bfloat16

D_MODEL = 1024
DEPTH = 4
HEAD_DIM = 64
N_Q_HEADS = 16
N_KV_HEADS = 4
GROUP = 4
KV_DIM = 256
ATT_BLOCK = 128
ROPE_DIM = 16
ROPE_THETA = 500000.0
HG_HEADS = 8
HG_DIM = 128
HG_CHUNK = 64
HG_SUB = 16
D_FF = 2816
FF_TILE = 1408
SUB_ROWS = 256
PLE_DIM = 256
ALPHA = (2 * DEPTH) ** 0.25
LN_EPS = 1e-5
ADAM_LR, ADAM_B1, ADAM_B2, ADAM_EPS, ADAM_WD, ADAM_STEP = 0.001, 0.9, 0.999, 1e-08, 0.01, 10

N_DEV = 8
LANES = 128
VMEM_LIMIT = 52 * 1024 * 1024
NEG = -1e30
MESH = pl.DeviceIdType.MESH
AXES = ("x", "y", "c")

BIG = (("att_w_qkv", 2), ("att_w_o", 1), ("hgrn_w_in", 2), ("hgrn_w_o", 1), ("ffn_w_in", 2), ("ffn_w_out", 1),
       ("ple_w_gate", 1), ("ple_w_proj", 2))


def _params(sem=None, vmem=VMEM_LIMIT):
    return pltpu.CompilerParams(dimension_semantics=sem, vmem_limit_bytes=vmem)


def _sigmoid(x):
    return jax.nn.sigmoid(x)


def _direct_copies(src_ref, out_ref, send_sems, recv_sems, local_sem, gather, arrivals):
    x, y, c = lax.axis_index("x"), lax.axis_index("y"), lax.axis_index("c")
    me = 4 * x + 2 * y + c
    mine = (lambda j: src_ref) if gather else (lambda j: src_ref.at[j])
    pairs = []
    for k in range(1, N_DEV):
        px, py, pc = x ^ (k >> 2), y ^ ((k >> 1) & 1), c ^ (k & 1)
        peer = 4 * px + 2 * py + pc
        send = pltpu.make_async_remote_copy(
            src_ref=mine(peer), dst_ref=out_ref.at[me], send_sem=send_sems.at[k], recv_sem=recv_sems.at[k],
            device_id=(px, py, pc), device_id_type=MESH)
        arrival = pltpu.make_async_remote_copy(
            src_ref=mine(peer), dst_ref=out_ref.at[peer], send_sem=send_sems.at[k], recv_sem=recv_sems.at[k],
            device_id=(x, y, c), device_id_type=MESH) if arrivals else None
        pairs.append((send, arrival))
    return pltpu.make_async_copy(mine(me), out_ref.at[me], local_sem), pairs


def _direct_start(*refs, gather):
    local, pairs = _direct_copies(*refs, gather, False)
    local.start()
    for send, _ in pairs:
        send.start()


def _direct_wait(*refs, gather):
    local, pairs = _direct_copies(*refs, gather, True)
    for send, arrival in pairs:
        send.wait_send()
        arrival.wait_recv()
    local.wait()


COMM_SCRATCH = [pltpu.SemaphoreType.DMA((N_DEV,)), pltpu.SemaphoreType.DMA((N_DEV,)), pltpu.SemaphoreType.DMA]


def _exchange(src, *, gather=False, name):
    def body(*refs):
        _direct_start(*refs, gather=gather)
        _direct_wait(*refs, gather=gather)

    blk = tuple(src.shape) if gather else tuple(src.shape[1:])
    return pl.pallas_call(
        body, name=name,
        out_shape=jax.ShapeDtypeStruct((N_DEV,) + blk, src.dtype),
        in_specs=[pl.BlockSpec(memory_space=pltpu.HBM)],
        out_specs=pl.BlockSpec(memory_space=pltpu.HBM),
        scratch_shapes=COMM_SCRATCH,
    )(src)


def _call(body, *, name, grid, out_shape, in_specs, out_specs, args, scratch_shapes=(), sem, comm=None, edge=None):
    out_shape, out_specs = tuple(out_shape), tuple(out_specs)
    if comm is None:
        return pl.pallas_call(body, name=name, grid=grid, out_shape=out_shape, in_specs=list(in_specs),
                              out_specs=out_specs, scratch_shapes=list(scratch_shapes),
                              compiler_params=_params(sem))(*args)
    src, gather = comm
    n_in, n_out, n_scr = len(args), len(out_shape), len(scratch_shapes)
    blk = tuple(src.shape) if gather else tuple(src.shape[1:])
    hbm = pl.BlockSpec(memory_space=pltpu.HBM)

    def carrying(*refs):
        ins, src_ref = refs[:n_in], refs[n_in]
        outs, dst_ref = refs[n_in + 1:n_in + 1 + n_out], refs[n_in + 1 + n_out]
        own = refs[n_in + 2 + n_out:n_in + 2 + n_out + n_scr]
        comm_refs = (src_ref, dst_ref) + tuple(refs[n_in + 2 + n_out + n_scr:])
        first, last = edge()

        @pl.when(first)
        def _():
            _direct_start(*comm_refs, gather=gather)

        body(*ins, *outs, *own)

        @pl.when(last)
        def _():
            _direct_wait(*comm_refs, gather=gather)

    return pl.pallas_call(
        carrying, name=name, grid=grid,
        out_shape=out_shape + (jax.ShapeDtypeStruct((N_DEV,) + blk, src.dtype),),
        in_specs=list(in_specs) + [hbm], out_specs=out_specs + (hbm,),
        scratch_shapes=list(scratch_shapes) + COMM_SCRATCH,
        compiler_params=_params(("arbitrary",) * len(grid)),
    )(*args, src)


def _gather(src, *, name):
    def body(src_ref, out_ref, send_sems, recv_sems, local_sem):
        x, y, c = lax.axis_index("x"), lax.axis_index("y"), lax.axis_index("c")
        sibling = (x, y, 1 - c)
        chips = [(1 - x, y), (x, 1 - y), (1 - x, 1 - y)]

        def rows(px, py, pc):
            return out_ref.at[4 * px + 2 * py + pc]

        def copy(k, block, to, from_src=False):
            return pltpu.make_async_remote_copy(
                src_ref=src_ref if from_src else rows(*block), dst_ref=rows(*block), send_sem=send_sems.at[k],
                recv_sem=recv_sems.at[k], device_id=to, device_id_type=MESH)

        me = (x, y, c)
        mine = pltpu.make_async_copy(src_ref, rows(*me), local_sem)
        mine.start()
        first = [copy(0, me, sibling, from_src=True)]
        first += [copy(1 + j, me, (*chip, c), from_src=True) for j, chip in enumerate(chips)]
        for cp in first:
            cp.start()
        passed = [copy(4 + j, (*chip, c), sibling) for j, chip in enumerate(chips)]
        for j, chip in enumerate(chips):
            copy(1 + j, (*chip, c), me).wait_recv()
            passed[j].start()
        copy(0, sibling, me).wait_recv()
        for j, chip in enumerate(chips):
            copy(4 + j, (*chip, 1 - c), me).wait_recv()
        for cp in first + passed:
            cp.wait_send()
        mine.wait()

    return pl.pallas_call(
        body, name=name,
        out_shape=jax.ShapeDtypeStruct((N_DEV,) + tuple(src.shape), src.dtype),
        in_specs=[pl.BlockSpec(memory_space=pltpu.HBM)],
        out_specs=pl.BlockSpec(memory_space=pltpu.HBM),
        scratch_shapes=[pltpu.SemaphoreType.DMA((7,)), pltpu.SemaphoreType.DMA((7,)), pltpu.SemaphoreType.DMA],
    )(src)


def _mm(a, b, *, tm, tn, tk, ta=False, tb=False, out_dtype=F32, name):
    b_list = list(b) if isinstance(b, (list, tuple)) else [b]
    assert not (tb and len(b_list) > 1)
    m, kdim = (a.shape[1], a.shape[0]) if ta else a.shape
    joff = [0]
    for piece in b_list:
        joff.append(joff[-1] + (piece.shape[0] if tb else piece.shape[1]) // tn)
    nk, n = kdim // tk, joff[-1] * tn
    dims = (((0 if ta else 1,), (1 if tb else 0,)), ((), ()))

    def mine(j, p):
        return (j >= joff[p]) & (j < joff[p + 1])

    def body(*refs):
        a_ref, b_refs, o_ref = refs[0], refs[1:1 + len(b_list)], refs[1 + len(b_list)]
        acc_ref = refs[-1] if nk > 1 else None
        j, k = pl.program_id(1), pl.program_id(2)
        for p, b_ref in enumerate(b_refs):
            def step(b_ref=b_ref):
                part = lax.dot_general(a_ref[...].astype(BF16), b_ref[...].astype(BF16), dims,
                                       preferred_element_type=F32)
                if nk == 1:
                    o_ref[...] = part.astype(out_dtype)
                else:
                    _accumulate(acc_ref, part, k == 0)

            if len(b_list) == 1:
                step()
            else:
                pl.when(mine(j, p))(step)
        if nk > 1:
            @pl.when(k == nk - 1)
            def _():
                o_ref[...] = acc_ref[...].astype(out_dtype)

    def b_spec(p):
        jj = lambda j: jnp.clip(j - joff[p], 0, joff[p + 1] - joff[p] - 1)
        kk = (lambda j, k: k) if len(b_list) == 1 else (lambda j, k: jnp.where(mine(j, p), k, 0))
        return (pl.BlockSpec((tn, tk), lambda i, j, k: (jj(j), kk(j, k))) if tb
                else pl.BlockSpec((tk, tn), lambda i, j, k: (kk(j, k), jj(j))))

    a_spec = pl.BlockSpec((tk, tm), lambda i, j, k: (k, i)) if ta else pl.BlockSpec((tm, tk), lambda i, j, k: (i, k))
    return pl.pallas_call(
        body, name=name, grid=(m // tm, n // tn, nk),
        out_shape=jax.ShapeDtypeStruct((m, n), out_dtype),
        in_specs=[a_spec] + [b_spec(p) for p in range(len(b_list))],
        out_specs=pl.BlockSpec((tm, tn), lambda i, j, k: (i, j)),
        scratch_shapes=[pltpu.VMEM((tm, tn), F32)] if nk > 1 else [],
        compiler_params=_params(("parallel", "parallel", "arbitrary")),
    )(a, *b_list)


def _dx_from_pieces(pieces, w, add, *, tm, ln=None, name):
    s = pieces[0].shape[0]
    widths = [p.shape[1] for p in pieces]

    def body(*refs):
        p_refs, (w_ref, add_ref) = refs[:len(pieces)], refs[len(pieces):len(pieces) + 2]
        rest = refs[len(pieces) + 2:]
        pg = jnp.zeros((1, D_MODEL), F32)
        pb = jnp.zeros((1, D_MODEL), F32)
        for rs in _row_parts(tm):
            r = ALPHA * add_ref[rs, :]
            off = 0
            for p_ref, width in zip(p_refs, widths):
                r = r + lax.dot_general(p_ref[rs, :], w_ref[:, off:off + width], (((1,), (1,)), ((), ())),
                                        preferred_element_type=F32)
                off += width
            if ln is None:
                rest[0][rs, :] = r
            else:
                dy, qg, qb = _ln_bwd_rows(r, rest[0][rs, :], rest[1][...])
                rest[2][rs, :] = dy
                rest[3][rs, :] = dy.astype(BF16)
                pg, pb = pg + qg, pb + qb
        if ln is not None:
            _accumulate(rest[4], pg, pl.program_id(0) == 0)
            _accumulate(rest[5], pb, pl.program_id(0) == 0)

    row = lambda i: (i, 0)
    tile = pl.BlockSpec((tm, D_MODEL), row)
    vec = pl.BlockSpec((1, D_MODEL), lambda i: (0, 0))
    in_specs = ([pl.BlockSpec((tm, width), row) for width in widths]
                + [pl.BlockSpec((D_MODEL, sum(widths)), lambda i: (0, 0)), tile])
    args = list(pieces) + [w, add]
    out_shape, out_specs = jax.ShapeDtypeStruct((s, D_MODEL), F32), tile
    if ln is not None:
        in_specs += [tile, vec]
        args += list(ln)
        out_shape = (jax.ShapeDtypeStruct((s, D_MODEL), F32), jax.ShapeDtypeStruct((s, D_MODEL), BF16),
                     jax.ShapeDtypeStruct((1, D_MODEL), F32), jax.ShapeDtypeStruct((1, D_MODEL), F32))
        out_specs = (tile, tile, vec, vec)
    return pl.pallas_call(
        body, name=name, grid=(s // tm,), out_shape=out_shape, in_specs=in_specs, out_specs=out_specs,
        compiler_params=_params(("arbitrary",) if ln is not None else ("parallel",)),
    )(*args)


def _ln_bwd_rows(do, y, g):
    mu = jnp.mean(y, axis=-1, keepdims=True)
    yc = y - mu
    var = jnp.mean(yc * yc, axis=-1, keepdims=True)
    rstd = lax.rsqrt(var + LN_EPS)
    xhat = yc * rstd
    dxhat = do * g
    dy = rstd * (dxhat - jnp.mean(dxhat, axis=-1, keepdims=True) - xhat * jnp.mean(dxhat * xhat, axis=-1, keepdims=True))
    return dy, jnp.sum(do * xhat, axis=0, keepdims=True), jnp.sum(do, axis=0, keepdims=True)


def _accumulate(ref, val, first):
    @pl.when(first)
    def _():
        ref[...] = val

    @pl.when(jnp.logical_not(first))
    def _():
        ref[...] += val


def _layer_norm_rows(y, g, b):
    mu = jnp.mean(y, axis=-1, keepdims=True)
    yc = y - mu
    var = jnp.mean(yc * yc, axis=-1, keepdims=True)
    return yc * lax.rsqrt(var + LN_EPS) * g + b


def _proj_ln(a, w, res, g, b, *, tm, name):
    s, kdim = a.shape

    def body(a_ref, w_ref, res_ref, g_ref, b_ref, pre_ref, o_ref, obf_ref):
        for rs in _row_parts(tm):
            h = jnp.dot(a_ref[rs, :], w_ref[...], preferred_element_type=F32)
            pre = ALPHA * res_ref[rs, :] + h
            out = _layer_norm_rows(pre, g_ref[...], b_ref[...])
            pre_ref[rs, :] = pre
            o_ref[rs, :] = out
            obf_ref[rs, :] = out.astype(BF16)

    row = lambda i: (i, 0)
    fix = lambda i: (0, 0)
    return pl.pallas_call(
        body, name=name, grid=(s // tm,),
        out_shape=(jax.ShapeDtypeStruct((s, D_MODEL), F32), jax.ShapeDtypeStruct((s, D_MODEL), F32),
                   jax.ShapeDtypeStruct((s, D_MODEL), BF16)),
        in_specs=[pl.BlockSpec((tm, kdim), row), pl.BlockSpec((kdim, D_MODEL), fix), pl.BlockSpec((tm, D_MODEL), row),
                  pl.BlockSpec((1, D_MODEL), fix), pl.BlockSpec((1, D_MODEL), fix)],
        out_specs=(pl.BlockSpec((tm, D_MODEL), row),) * 3,
        compiler_params=_params(("parallel",)),
    )(a, w, res, g, b)


def _row_parts(tm):
    sub = min(tm, SUB_ROWS)
    return [pl.ds(r * sub, sub) for r in range(tm // sub)]


def _ffn_in(xbf, w, *, tm, tn, name):
    s = xbf.shape[0]
    nj = D_FF // tn

    def body(x_ref, wg_ref, wu_ref, g_ref, u_ref, act_ref):
        for rs in _row_parts(tm):
            xv = x_ref[rs, :]
            gg = jnp.dot(xv, wg_ref[...], preferred_element_type=F32)
            uu = jnp.dot(xv, wu_ref[...], preferred_element_type=F32)
            g_ref[rs, :] = gg.astype(BF16)
            u_ref[rs, :] = uu.astype(BF16)
            act_ref[rs, :] = (gg * _sigmoid(gg) * uu).astype(BF16)

    out = jax.ShapeDtypeStruct((s, D_FF), BF16)
    tile = pl.BlockSpec((tm, tn), lambda j, i: (i, j))
    return pl.pallas_call(
        body, name=name, grid=(nj, s // tm),
        out_shape=(out, out, out),
        in_specs=[pl.BlockSpec((tm, D_MODEL), lambda j, i: (i, 0)), pl.BlockSpec((D_MODEL, tn), lambda j, i: (0, j)),
                  pl.BlockSpec((D_MODEL, tn), lambda j, i: (0, j + nj))],
        out_specs=(tile, tile, tile),
        compiler_params=_params(("parallel", "parallel")),
    )(xbf, w, w)


def _ffn_bwd_act(dybf, w_out, g, u, *, tm, tn, name):
    s = dybf.shape[0]

    def body(dy_ref, w_ref, g_ref, u_ref, dg_ref, du_ref):
        for rs in _row_parts(tm):
            dact = lax.dot_general(dy_ref[rs, :], w_ref[...], (((1,), (1,)), ((), ())), preferred_element_type=F32)
            gg = g_ref[rs, :].astype(F32)
            uu = u_ref[rs, :].astype(F32)
            sg = _sigmoid(gg)
            dg_ref[rs, :] = (dact * uu * sg * (1.0 + gg * (1.0 - sg))).astype(BF16)
            du_ref[rs, :] = (dact * gg * sg).astype(BF16)

    out = jax.ShapeDtypeStruct((s, D_FF), BF16)
    tile = pl.BlockSpec((tm, tn), lambda j, i: (i, j))
    return pl.pallas_call(
        body, name=name, grid=(D_FF // tn, s // tm),
        out_shape=(out, out),
        in_specs=[pl.BlockSpec((tm, D_MODEL), lambda j, i: (i, 0)), pl.BlockSpec((tn, D_MODEL), lambda j, i: (j, 0)),
                  tile, tile],
        out_specs=(tile, tile),
        compiler_params=_params(("parallel", "parallel")),
    )(dybf, w_out, g, u)


def _ffn_out_ple(act, w_out, res, g, b, p, w_gate, w_proj, *, tm, name):
    s = act.shape[0]

    def body(a_ref, w_ref, res_ref, g_ref, b_ref, p_ref, wg_ref, wp_ref, pre_ref, x2bf_ref, o_ref, obf_ref):
        for rs in _row_parts(tm):
            pre = ALPHA * res_ref[rs, :] + jnp.dot(a_ref[rs, :], w_ref[...], preferred_element_type=F32)
            x2 = _layer_norm_rows(pre, g_ref[...], b_ref[...])
            x2bf = x2.astype(BF16)
            pre_ref[rs, :] = pre
            x2bf_ref[rs, :] = x2bf
            gate = jnp.dot(x2bf, wg_ref[...], preferred_element_type=F32)
            pp = jnp.dot(p_ref[rs, :].astype(BF16), wp_ref[...], preferred_element_type=F32)
            out = x2 + _sigmoid(gate) * pp
            o_ref[rs, :] = out
            obf_ref[rs, :] = out.astype(BF16)

    row = lambda i: (i, 0)
    fix = lambda i: (0, 0)
    tile = pl.BlockSpec((tm, D_MODEL), row)
    vec = pl.BlockSpec((1, D_MODEL), fix)
    act_t = lambda dt: jax.ShapeDtypeStruct((s, D_MODEL), dt)
    return pl.pallas_call(
        body, name=name, grid=(s // tm,),
        out_shape=(act_t(F32), act_t(BF16), act_t(F32), act_t(BF16)),
        in_specs=[pl.BlockSpec((tm, D_FF), row), pl.BlockSpec((D_FF, D_MODEL), fix), tile, vec, vec,
                  pl.BlockSpec((tm, PLE_DIM), row), pl.BlockSpec((D_MODEL, D_MODEL), fix),
                  pl.BlockSpec((PLE_DIM, D_MODEL), fix)],
        out_specs=(tile, tile, tile, tile),
        compiler_params=_params(("parallel",)),
    )(act, w_out, res, g, b, p, w_gate, w_proj)


def _ple_ln_bwd(dx3, x2bf, p, w_gate, w_proj, pre, g, *, tm, name):
    s = dx3.shape[0]

    def body(d_ref, xbf_ref, p_ref, wg_ref, wp_ref, pre_ref, g_ref, da_ref, dpp_ref, dy_ref, dybf_ref, dg_ref, db_ref):
        pg = jnp.zeros((1, D_MODEL), F32)
        pb = jnp.zeros((1, D_MODEL), F32)
        for rs in _row_parts(tm):
            d = d_ref[rs, :]
            a = jnp.dot(xbf_ref[rs, :], wg_ref[...], preferred_element_type=F32)
            pp = jnp.dot(p_ref[rs, :].astype(BF16), wp_ref[...], preferred_element_type=F32)
            sg = _sigmoid(a)
            da = (d * pp * sg * (1.0 - sg)).astype(BF16)
            da_ref[rs, :] = da
            dpp_ref[rs, :] = (d * sg).astype(BF16)
            dx2 = d + lax.dot_general(da, wg_ref[...], (((1,), (1,)), ((), ())), preferred_element_type=F32)
            dy, qg, qb = _ln_bwd_rows(dx2, pre_ref[rs, :], g_ref[...])
            dy_ref[rs, :] = dy
            dybf_ref[rs, :] = dy.astype(BF16)
            pg, pb = pg + qg, pb + qb
        _accumulate(dg_ref, pg, pl.program_id(0) == 0)
        _accumulate(db_ref, pb, pl.program_id(0) == 0)

    row = lambda i: (i, 0)
    fix = lambda i: (0, 0)
    tile = pl.BlockSpec((tm, D_MODEL), row)
    vec = pl.BlockSpec((1, D_MODEL), fix)
    act = lambda dt: jax.ShapeDtypeStruct((s, D_MODEL), dt)
    return pl.pallas_call(
        body, name=name, grid=(s // tm,),
        out_shape=(act(BF16), act(BF16), act(F32), act(BF16), jax.ShapeDtypeStruct((1, D_MODEL), F32),
                   jax.ShapeDtypeStruct((1, D_MODEL), F32)),
        in_specs=[tile, tile, pl.BlockSpec((tm, PLE_DIM), row), pl.BlockSpec((D_MODEL, D_MODEL), fix),
                  pl.BlockSpec((PLE_DIM, D_MODEL), fix), tile, vec],
        out_specs=(tile, tile, tile, tile, vec, vec),
        compiler_params=_params(("arbitrary",)),
    )(dx3, x2bf, p, w_gate, w_proj, pre, g)


def _loss_head(y, target, *, tm, name):
    s = y.shape[0]

    def body(y_ref, t_ref, dy_ref, loss_ref, acc_ref):
        err = y_ref[...] - t_ref[...]
        dy_ref[...] = err * (1.0 / D_MODEL)
        part = jnp.sum(err * err, axis=0, keepdims=True)

        @pl.when(pl.program_id(0) == 0)
        def _():
            acc_ref[...] = part

        @pl.when(pl.program_id(0) > 0)
        def _():
            acc_ref[...] += part

        @pl.when(pl.program_id(0) == pl.num_programs(0) - 1)
        def _():
            tot = jnp.sum(acc_ref[...], axis=1, keepdims=True) * (0.5 / D_MODEL)
            loss_ref[...] = jnp.broadcast_to(tot, (8, LANES))

    row = lambda i: (i, 0)
    return pl.pallas_call(
        body, name=name, grid=(s // tm,),
        out_shape=(jax.ShapeDtypeStruct((s, D_MODEL), F32), jax.ShapeDtypeStruct((8, LANES), F32)),
        in_specs=[pl.BlockSpec((tm, D_MODEL), row), pl.BlockSpec((tm, D_MODEL), row)],
        out_specs=(pl.BlockSpec((tm, D_MODEL), row), pl.BlockSpec((8, LANES), lambda i: (0, 0))),
        scratch_shapes=[pltpu.VMEM((1, D_MODEL), F32)],
        compiler_params=_params(("arbitrary",)),
    )(y, target)


def _rope_tables(s):
    inv = ROPE_THETA ** (-jnp.arange(0, ROPE_DIM, 2, dtype=F32) / ROPE_DIM)
    ang = jnp.arange(s, dtype=F32)[:, None] * inv[None, :]
    cos, sin = jnp.cos(ang), jnp.sin(ang)
    ones = jnp.ones((s, HEAD_DIM - ROPE_DIM), F32)
    c_head = jnp.concatenate([cos, cos, ones], axis=1)
    s_head = jnp.concatenate([-sin, sin, 0.0 * ones], axis=1)
    return jnp.concatenate([c_head, c_head], axis=1), jnp.concatenate([s_head, s_head], axis=1)


def _rope(v, cos, sin):
    n = v.shape[1] // LANES
    width = v.shape[1]
    cos_w = jnp.tile(cos, (1, n)) if n > 1 else cos
    sin_w = jnp.tile(sin, (1, n)) if n > 1 else sin
    dim = lax.broadcasted_iota(jnp.int32, (1, width), 1) % HEAD_DIM
    partner = jnp.where(dim < ROPE_DIM // 2, pltpu.roll(v, width - ROPE_DIM // 2, 1), pltpu.roll(v, ROPE_DIM // 2, 1))
    return v * cos_w + partner * sin_w


def _unrope(dv, cos, sin):
    n = dv.shape[1] // LANES
    width = dv.shape[1]
    cos_w = jnp.tile(cos, (1, n)) if n > 1 else cos
    sin_w = jnp.tile(sin, (1, n)) if n > 1 else sin
    t = dv * sin_w
    dim = lax.broadcasted_iota(jnp.int32, (1, width), 1) % HEAD_DIM
    partner = jnp.where(dim < ROPE_DIM // 2, pltpu.roll(t, width - ROPE_DIM // 2, 1),
                        jnp.where(dim < ROPE_DIM, pltpu.roll(t, ROPE_DIM // 2, 1), 0.0))
    return dv * cos_w + partner


def _att_mask(i, nb):
    rows = GROUP * ATT_BLOCK
    r = lax.broadcasted_iota(jnp.int32, (rows, 3 * ATT_BLOCK), 0) % ATT_BLOCK
    cidx = lax.broadcasted_iota(jnp.int32, (rows, 3 * ATT_BLOCK), 1)
    rel = r + ATT_BLOCK - cidx
    ok = (rel <= ATT_BLOCK) & (rel >= -ATT_BLOCK)
    ok = ok & ((cidx >= ATT_BLOCK) | (i > 0)) & ((cidx < 2 * ATT_BLOCK) | (i < nb - 1))
    return ok


def _att_mask_t(i, nb):
    cols = GROUP * ATT_BLOCK
    cidx = lax.broadcasted_iota(jnp.int32, (3 * ATT_BLOCK, cols), 0)
    r = lax.broadcasted_iota(jnp.int32, (3 * ATT_BLOCK, cols), 1) % ATT_BLOCK
    rel = r + ATT_BLOCK - cidx
    ok = (rel <= ATT_BLOCK) & (rel >= -ATT_BLOCK)
    return ok & ((cidx >= ATT_BLOCK) | (i > 0)) & ((cidx < 2 * ATT_BLOCK) | (i < nb - 1))


def _sink_lanes(sink_ref, h):
    cols = GROUP * ATT_BLOCK
    grp = lax.broadcasted_iota(jnp.int32, (1, cols), 1) // ATT_BLOCK
    out = jnp.zeros((1, cols), F32)
    for gq in range(GROUP):
        out = jnp.where(grp == gq, sink_ref[GROUP * h + gq], out)
    return out


def _half_mask(half):
    lane = lax.broadcasted_iota(jnp.int32, (1, LANES), 1)
    return (lane // HEAD_DIM) == half


def _stack_q(q, h):
    parts = []
    for gq in range(GROUP):
        n = GROUP * h + gq
        grp = q[:, LANES * (n // 2):LANES * (n // 2 + 1)]
        grp = jnp.where(_half_mask(n % 2), grp, 0.0)
        if n % 2 != h % 2:
            grp = pltpu.roll(grp, HEAD_DIM, 1)
        parts.append(grp)
    return jnp.concatenate(parts, axis=0)


def _unstack_q(stacked, h, acc):
    for gq in range(GROUP):
        n = GROUP * h + gq
        grp = stacked[ATT_BLOCK * gq:ATT_BLOCK * (gq + 1), :]
        grp = jnp.where(_half_mask(h % 2), grp, 0.0)
        if n % 2 != h % 2:
            grp = pltpu.roll(grp, HEAD_DIM, 1)
        acc[n // 2] = grp if acc[n // 2] is None else acc[n // 2] + grp
    return acc


def _sink_rows(sink_ref, h):
    rows = GROUP * ATT_BLOCK
    grp = lax.broadcasted_iota(jnp.int32, (rows, 1), 0) // ATT_BLOCK
    out = jnp.zeros((rows, 1), F32)
    for gq in range(GROUP):
        out = jnp.where(grp == gq, sink_ref[GROUP * h + gq], out)
    return out


def _att_probs(qs, kh, sink, valid):
    s = lax.dot_general(qs, kh, (((1,), (1,)), ((), ())), preferred_element_type=F32)
    s = jnp.where(valid, s, NEG)
    m = jnp.maximum(jnp.max(s, axis=-1, keepdims=True), sink)
    p = jnp.exp(s - m)
    es = jnp.exp(sink - m)
    den = jnp.sum(p, axis=-1, keepdims=True) + es
    inv = 1.0 / den
    return p * inv, es * inv


def _att_specs(nb):
    prev = lambda i: (jnp.maximum(i - 1, 0), 0)
    cur = lambda i: (i, 0)
    nxt = lambda i: (jnp.minimum(i + 1, nb - 1), 0)
    kv = lambda f: (lambda i: (f(i)[0], 2))
    tab = [pl.BlockSpec((ATT_BLOCK, LANES), f) for f in (cur, prev, cur, nxt)]
    z = [pl.BlockSpec((ATT_BLOCK, D_MODEL), cur)] + [pl.BlockSpec((ATT_BLOCK, 2 * KV_DIM), kv(f)) for f in (prev, cur, nxt)]
    return z, tab


def _att_load(zq_ref, kp_ref, kc_ref, kn_ref, cq_ref, sq_ref, cp_ref, sp_ref, cc_ref, sc_ref, cn_ref, sn_ref):
    q = (_rope(zq_ref[...], cq_ref[...], sq_ref[...]) * (HEAD_DIM ** -0.5))
    ks, vs = [], []
    for ref, c_ref, s_ref in ((kp_ref, cp_ref, sp_ref), (kc_ref, cc_ref, sc_ref), (kn_ref, cn_ref, sn_ref)):
        kvb = ref[...]
        ks.append(_rope(kvb[:, :KV_DIM], c_ref[...], s_ref[...]))
        vs.append(kvb[:, KV_DIM:])
    return q, jnp.concatenate(ks, axis=0).astype(BF16), jnp.concatenate(vs, axis=0).astype(BF16)


def _att_fwd(z, sink, cos, sin, *, comm=None, name):
    s = z.shape[0]
    nb = s // ATT_BLOCK

    def body(zq_ref, kp_ref, kc_ref, kn_ref, cq_ref, cp_ref, cc_ref, cn_ref, sq_ref, sp_ref, sc_ref, sn_ref, sink_ref,
             o_ref):
        i = pl.program_id(0)
        q, k, v = _att_load(zq_ref, kp_ref, kc_ref, kn_ref, cq_ref, sq_ref, cp_ref, sp_ref, cc_ref, sc_ref, cn_ref, sn_ref)
        valid = _att_mask(i, nb)
        acc = [None] * (N_Q_HEADS // 2)
        for h in range(N_KV_HEADS):
            lanes = slice(LANES * (h // 2), LANES * (h // 2 + 1))
            qs = _stack_q(q, h).astype(BF16)
            prob, _ = _att_probs(qs, k[:, lanes], _sink_rows(sink_ref, h), valid)
            oh = jnp.dot(prob.astype(BF16), v[:, lanes], preferred_element_type=F32)
            acc = _unstack_q(oh, h, acc)
        o_ref[...] = jnp.concatenate(acc, axis=1).astype(BF16)

    zspecs, tab = _att_specs(nb)
    return _call(
        body, name=name, grid=(nb,),
        out_shape=(jax.ShapeDtypeStruct((s, D_MODEL), BF16),),
        in_specs=zspecs + tab + tab + [pl.BlockSpec(memory_space=pltpu.SMEM)],
        out_specs=(pl.BlockSpec((ATT_BLOCK, D_MODEL), lambda i: (i, 0)),),
        args=(z, z, z, z, cos, cos, cos, cos, sin, sin, sin, sin, sink), sem=("parallel",), comm=comm,
        edge=lambda: (pl.program_id(0) == 0, pl.program_id(0) == nb - 1))


def _att_bwd(z, do, sink, cos, sin, *, comm=None, name):
    s = z.shape[0]
    nb = s // ATT_BLOCK

    def body(zq_ref, kp_ref, kc_ref, kn_ref, cq_ref, cp_ref, cc_ref, cn_ref, sq_ref, sp_ref, sc_ref, sn_ref, sink_ref,
             do_ref, dq_ref, part_ref, dsink_ref):
        i = pl.program_id(0)
        q, k, v = _att_load(zq_ref, kp_ref, kc_ref, kn_ref, cq_ref, sq_ref, cp_ref, sp_ref, cc_ref, sc_ref, cn_ref, sn_ref)
        valid = _att_mask_t(i, nb)
        dout = do_ref[...].astype(F32)
        dq_acc = [None] * (N_Q_HEADS // 2)
        dk_acc = [None] * 2
        dv_acc = [None] * 2
        rows = []
        nt = (((1,), (1,)), ((), ()))
        for h in range(N_KV_HEADS):
            grp = h // 2
            lanes = slice(LANES * grp, LANES * (grp + 1))
            qs = _stack_q(q, h).astype(BF16)
            dos = _stack_q(dout, h).astype(BF16)
            sink = _sink_lanes(sink_ref, h)
            sc = jnp.where(valid, lax.dot_general(k[:, lanes], qs, nt, preferred_element_type=F32), NEG)
            m = jnp.maximum(jnp.max(sc, axis=0, keepdims=True), sink)
            p = jnp.exp(sc - m)
            es = jnp.exp(sink - m)
            inv = 1.0 / (jnp.sum(p, axis=0, keepdims=True) + es)
            prob = p * inv
            dprob = lax.dot_general(v[:, lanes], dos, nt, preferred_element_type=F32)
            delta = jnp.sum(prob * dprob, axis=0, keepdims=True)
            dsc = (prob * (dprob - delta)).astype(BF16)
            dsk = -(es * inv) * delta
            for gq in range(GROUP):
                tot = jnp.sum(dsk[:, ATT_BLOCK * gq:ATT_BLOCK * (gq + 1)], axis=1, keepdims=True)
                rows.append(jnp.broadcast_to(tot, (1, LANES)))
            dqs = lax.dot_general(dsc, k[:, lanes], (((0,), (0,)), ((), ())), preferred_element_type=F32)
            dq_acc = _unstack_q(dqs, h, dq_acc)
            dkh = jnp.dot(dsc, qs, preferred_element_type=F32)
            dvh = jnp.dot(prob.astype(BF16), dos, preferred_element_type=F32)
            dk_acc[grp] = dkh if dk_acc[grp] is None else dk_acc[grp] + dkh
            dv_acc[grp] = dvh if dv_acc[grp] is None else dv_acc[grp] + dvh
        dq = jnp.concatenate(dq_acc, axis=1) * (HEAD_DIM ** -0.5)
        dq_ref[...] = _unrope(dq, cq_ref[...], sq_ref[...]).astype(BF16)
        part = jnp.concatenate(dk_acc + dv_acc, axis=1)
        for wdw in range(3):
            part_ref[wdw] = part[ATT_BLOCK * wdw:ATT_BLOCK * (wdw + 1), :]
        dsink = jnp.concatenate(rows, axis=0)

        @pl.when(i == 0)
        def _():
            dsink_ref[...] = dsink

        @pl.when(i > 0)
        def _():
            dsink_ref[...] += dsink

    zspecs, tab = _att_specs(nb)
    return _call(
        body, name=name, grid=(nb,),
        out_shape=(jax.ShapeDtypeStruct((s, D_MODEL), BF16), jax.ShapeDtypeStruct((nb, 3, ATT_BLOCK, 2 * KV_DIM), F32),
                   jax.ShapeDtypeStruct((N_Q_HEADS, LANES), F32)),
        in_specs=zspecs + tab + tab + [pl.BlockSpec(memory_space=pltpu.SMEM), pl.BlockSpec((ATT_BLOCK, D_MODEL), lambda i: (i, 0))],
        out_specs=(pl.BlockSpec((ATT_BLOCK, D_MODEL), lambda i: (i, 0)),
                   pl.BlockSpec((None, 3, ATT_BLOCK, 2 * KV_DIM), lambda i: (i, 0, 0, 0)),
                   pl.BlockSpec((N_Q_HEADS, LANES), lambda i: (0, 0))),
        args=(z, z, z, z, cos, cos, cos, cos, sin, sin, sin, sin, sink, do), sem=("arbitrary",), comm=comm,
        edge=lambda: (pl.program_id(0) == 0, pl.program_id(0) == nb - 1))


def _att_bwd_kv(part, cos, sin, *, name):
    nb = part.shape[0]

    def body(pn_ref, pc_ref, pp_ref, c_ref, s_ref, o_ref):
        j = pl.program_id(0)
        tot = pc_ref[...]
        tot = tot + jnp.where(j < nb - 1, pn_ref[...], 0.0)
        tot = tot + jnp.where(j > 0, pp_ref[...], 0.0)
        dk = _unrope(tot[:, :KV_DIM], c_ref[...], s_ref[...])
        o_ref[...] = jnp.concatenate([dk, tot[:, KV_DIM:]], axis=1).astype(BF16)

    blk = (None, None, ATT_BLOCK, 2 * KV_DIM)
    return pl.pallas_call(
        body, name=name, grid=(nb,),
        out_shape=jax.ShapeDtypeStruct((nb * ATT_BLOCK, 2 * KV_DIM), BF16),
        in_specs=[pl.BlockSpec(blk, lambda j: (jnp.minimum(j + 1, nb - 1), 0, 0, 0)),
                  pl.BlockSpec(blk, lambda j: (j, 1, 0, 0)),
                  pl.BlockSpec(blk, lambda j: (jnp.maximum(j - 1, 0), 2, 0, 0)),
                  pl.BlockSpec((ATT_BLOCK, LANES), lambda j: (j, 0)), pl.BlockSpec((ATT_BLOCK, LANES), lambda j: (j, 0))],
        out_specs=pl.BlockSpec((ATT_BLOCK, 2 * KV_DIM), lambda j: (j, 0)),
        compiler_params=_params(("parallel",)),
    )(part, part, part, cos, sin)


def _bdot(a, b, dims):
    return lax.dot_general(a.astype(BF16), b.astype(BF16), (dims, ((), ())), preferred_element_type=F32)


@jax.custom_vjp
def _dot_nn(a, b):
    return _bdot(a, b, ((1,), (0,)))


@jax.custom_vjp
def _dot_nt(a, b):
    return _bdot(a, b, ((1,), (1,)))


@jax.custom_vjp
def _dot_tn(a, b):
    return _bdot(a, b, ((0,), (0,)))


_dot_nn.defvjp(lambda a, b: (_dot_nn(a, b), (a, b)), lambda r, d: (_dot_nt(d, r[1]), _dot_tn(r[0], d)))
_dot_nt.defvjp(lambda a, b: (_dot_nt(a, b), (a, b)), lambda r, d: (_dot_nn(d, r[1]), _dot_tn(d, r[0])))
_dot_tn.defvjp(lambda a, b: (_dot_tn(a, b), (a, b)), lambda r, d: (_dot_nt(r[1], d), _dot_nn(r[0], d)))


def _running_sum(v, up):
    n = v.shape[0]
    rows = lax.broadcasted_iota(jnp.int32, v.shape, 0)
    sh = 1
    while sh < n:
        if up:
            v = v + jnp.where(rows < n - sh, pltpu.roll(v, n - sh, 0), 0.0)
        else:
            v = v + jnp.where(rows >= sh, pltpu.roll(v, sh, 0), 0.0)
        sh *= 2
    return v


@jax.custom_vjp
def _sum_down(v):
    return _running_sum(v, False)


@jax.custom_vjp
def _sum_up(v):
    return _running_sum(v, True)


_sum_down.defvjp(lambda v: (_running_sum(v, False), None), lambda _, d: (_sum_up(d),))
_sum_up.defvjp(lambda v: (_running_sum(v, True), None), lambda _, d: (_sum_down(d),))

N_SUB = HG_CHUNK // HG_SUB


def _fold_blocks(v):
    out = v[:HG_CHUNK]
    for i in range(1, N_SUB):
        out = out + v[HG_CHUNK * i:HG_CHUNK * (i + 1)]
    return out


@jax.custom_vjp
def _fold(v):
    return _fold_blocks(v)


_fold.defvjp(lambda v: (_fold_blocks(v), None), lambda _, d: (jnp.concatenate([d] * N_SUB, axis=0),))


def _hg_consts(rev):
    c, sub = HG_CHUNK, HG_SUB
    rowpos = lax.broadcasted_iota(jnp.int32, (c, HG_DIM), 0)
    rr = lax.broadcasted_iota(jnp.int32, (N_SUB * c, c), 0)
    key = lax.broadcasted_iota(jnp.int32, (N_SUB * c, c), 1)
    blk, qry = rr // c, rr % c
    if rev:
        rowpos, qry, key = c - 1 - rowpos, c - 1 - qry, c - 1 - key
    keep = (key // sub == blk) & (key <= qry)
    return keep, rowpos


def _pick(b, rowpos, t):
    return jnp.sum(jnp.where(rowpos == t, b, 0.0), axis=0, keepdims=True)


def _hg_local(zq, zf, zv, lbv, consts, dots):
    dot_nn, dot_nt, dot_tn, cum, fold = dots
    keep, rowpos = consts
    sig = _sigmoid(zf)
    f = lbv + (1.0 - lbv) * sig
    g = jnp.log(f)
    k = (1.0 - lbv) * (1.0 - sig)
    q = zq * _sigmoid(zq)
    b = cum(g)
    ends = [_pick(b, rowpos, (j + 1) * HG_SUB - 1) for j in range(N_SUB)]
    b_last = ends[-1]
    b_end = b_last
    for j in range(N_SUB - 1):
        b_end = jnp.where(rowpos // HG_SUB == j, ends[j], b_end)
    kc = k * jnp.exp(b_end - b)
    qbs = [q * jnp.exp(jnp.where(rowpos >= j * HG_SUB, b - ends[j], 0.0)) for j in range(N_SUB)]
    scores = fold(jnp.where(keep, dot_nt(jnp.concatenate(qbs, axis=0), kc), 0.0))
    return dot_nn(scores, zv), q * jnp.exp(b), k * jnp.exp(b_last - b), jnp.exp(b_last)


def _hg_chunk(zq, zf, zv, lbv, st, consts, dots):
    intra, qs, kd, dec = _hg_local(zq, zf, zv, lbv, consts, dots)
    return intra + dots[1](qs, st), dec * st + dots[2](zv, kd)


def _hg_dots(diff, rev):
    if diff:
        return _dot_nn, _dot_nt, _dot_tn, (_sum_up if rev else _sum_down), _fold
    return (lambda a, b: _bdot(a, b, ((1,), (0,))), lambda a, b: _bdot(a, b, ((1,), (1,))),
            lambda a, b: _bdot(a, b, ((0,), (0,))), lambda v: _running_sum(v, rev), _fold_blocks)


def _hg_specs(ts, nch, trow):
    tile = pl.BlockSpec((ts, HG_DIM), lambda h, t: (trow(t), h))
    mats = pl.BlockSpec((None, nch, HG_DIM, HG_DIM), lambda h, t: (h, trow(t), 0, 0))
    vecs = pl.BlockSpec((None, nch, 1, HG_DIM), lambda h, t: (h, trow(t), 0, 0))
    return tile, mats, vecs


def _time_order(nch, rev):
    return range(nch - 1, -1, -1) if rev else range(nch)


def _chunk_rows(c):
    return pl.ds(c * HG_CHUNK, HG_CHUNK)


def _hg_edge(nt):
    h, t = pl.program_id(0), pl.program_id(1)
    return (h == 0) & (t == 0), (h == HG_HEADS - 1) & (t == nt - 1)


def _hg_fwd(z, lb, *, rev, ts, comm=None, name):
    s = z.shape[0]
    nt = s // ts
    nch = ts // HG_CHUNK
    fcol = HG_HEADS * (2 if rev else 1)

    def body(zq_ref, zf_ref, zv_ref, lb_ref, o_ref, st_ref, qs_ref, dec_ref, state_ref):
        @pl.when(pl.program_id(1) == 0)
        def _():
            state_ref[...] = jnp.zeros_like(state_ref)

        consts = _hg_consts(rev)
        dots = _hg_dots(False, rev)
        lbv = lb_ref[...]
        local = {}
        for c in range(nch):
            rows = _chunk_rows(c)
            zv = zv_ref[rows, :]
            intra, qs, kd, dec = _hg_local(zq_ref[rows, :], zf_ref[rows, :], zv, lbv, consts, dots)
            qs = qs.astype(BF16)
            qs_ref[rows, :] = qs
            dec_ref[c] = dec
            local[c] = (intra, qs, dec, dots[2](zv, kd))
        st = state_ref[...]
        for c in _time_order(nch, rev):
            intra, qs, dec, upd = local[c]
            st_ref[c] = st.astype(BF16)
            o_ref[_chunk_rows(c), :] = intra + _bdot(qs, st, ((1,), (1,)))
            st = dec * st + upd
        state_ref[...] = st

    trow = (lambda t: nt - 1 - t) if rev else (lambda t: t)
    col = lambda off: pl.BlockSpec((ts, HG_DIM), lambda h, t: (trow(t), off + h))
    tile, mats, vecs = _hg_specs(ts, nch, trow)
    nchunks = s // HG_CHUNK
    return _call(
        body, name=name, grid=(HG_HEADS, nt),
        out_shape=(jax.ShapeDtypeStruct((s, D_MODEL), F32),
                   jax.ShapeDtypeStruct((HG_HEADS, nchunks, HG_DIM, HG_DIM), BF16),
                   jax.ShapeDtypeStruct((s, D_MODEL), BF16),
                   jax.ShapeDtypeStruct((HG_HEADS, nchunks, 1, HG_DIM), F32)),
        in_specs=[col(0), col(fcol), col(3 * HG_HEADS), pl.BlockSpec((None, 1, HG_DIM), lambda h, t: (h, 0, 0))],
        out_specs=(tile, mats, tile, vecs), args=(z, z, z, lb),
        scratch_shapes=[pltpu.VMEM((HG_DIM, HG_DIM), F32)], sem=("parallel", "arbitrary"), comm=comm,
        edge=lambda: _hg_edge(nt))


def _hg_bwd(z, lb, states, qs, dec, dout, addq, addv, *, rev, ts, comm=None, name):
    s = z.shape[0]
    nt = s // ts
    nch = ts // HG_CHUNK
    fcol = HG_HEADS * (2 if rev else 1)
    has_add = addq is not None

    def body(*refs):
        zq_ref, zf_ref, zv_ref, lb_ref, st_ref, qs_ref, dec_ref, do_ref = refs[:8]
        aq_ref, av_ref = (refs[8], refs[9]) if has_add else (None, None)
        dq_ref, df_ref, dv_ref, dlb_ref, grad_ref = refs[-5:]

        @pl.when(pl.program_id(1) == 0)
        def _():
            grad_ref[...] = jnp.zeros_like(grad_ref)

        consts = _hg_consts(rev)
        dots = _hg_dots(True, rev)
        lbv = lb_ref[...]
        prods = {c: _bdot(do_ref[_chunk_rows(c), :], qs_ref[_chunk_rows(c), :], ((0,), (0,))) for c in range(nch)}
        gleave = {}
        gr = grad_ref[...]
        for c in reversed(_time_order(nch, rev)):
            gleave[c] = gr
            gr = dec_ref[c] * gr + prods[c]
        grad_ref[...] = gr
        dlb_blk = jnp.zeros((1, HG_DIM), F32)
        for c in range(nch):
            rows = _chunk_rows(c)
            fn = lambda a, b2, c2, d2, e2: _hg_chunk(a, b2, c2, d2, e2, consts, dots)
            _, pull = jax.vjp(fn, zq_ref[rows, :], zf_ref[rows, :], zv_ref[rows, :], lbv, st_ref[c].astype(F32))
            dq, df, dv, dlb, _ = pull((do_ref[rows, :], gleave[c]))
            if has_add:
                dq = dq + aq_ref[rows, :]
                dv = dv + av_ref[rows, :]
            dq_ref[rows, :] = dq.astype(dq_ref.dtype)
            df_ref[rows, :] = df.astype(BF16)
            dv_ref[rows, :] = dv.astype(dv_ref.dtype)
            dlb_blk = dlb_blk + dlb

        @pl.when(pl.program_id(1) == 0)
        def _():
            dlb_ref[...] = dlb_blk

        @pl.when(pl.program_id(1) > 0)
        def _():
            dlb_ref[...] += dlb_blk

    trow = (lambda t: t) if rev else (lambda t: nt - 1 - t)
    col = lambda off: pl.BlockSpec((ts, HG_DIM), lambda h, t: (trow(t), off + h))
    tile, mats, vecs = _hg_specs(ts, nch, trow)
    in_specs = [col(0), col(fcol), col(3 * HG_HEADS), pl.BlockSpec((None, 1, HG_DIM), lambda h, t: (h, 0, 0)),
                mats, tile, vecs, tile]
    args = [z, z, z, lb, states, qs, dec, dout]
    if has_add:
        in_specs += [tile, tile]
        args += [addq, addv]
    act = lambda dt: jax.ShapeDtypeStruct((s, D_MODEL), dt)
    sums = BF16 if has_add else F32
    return _call(
        body, name=name, grid=(HG_HEADS, nt),
        out_shape=(act(sums), act(BF16), act(sums), jax.ShapeDtypeStruct((HG_HEADS, 1, HG_DIM), F32)),
        in_specs=in_specs,
        out_specs=(tile, tile, tile, pl.BlockSpec((None, 1, HG_DIM), lambda h, t: (h, 0, 0))), args=tuple(args),
        scratch_shapes=[pltpu.VMEM((HG_DIM, HG_DIM), F32)], sem=("parallel", "arbitrary"), comm=comm,
        edge=lambda: _hg_edge(nt))


def _hg_post(of, ob, z, norm_g, *, tm, name):
    s = of.shape[0]

    def body(of_ref, ob_ref, gate_ref, ng_ref, y_ref):
        gn = ng_ref[...]
        for h in range(HG_HEADS):
            ln = slice(HG_DIM * h, HG_DIM * (h + 1))
            o = of_ref[:, ln] + ob_ref[:, ln]
            r = lax.rsqrt(jnp.mean(o * o, axis=-1, keepdims=True) + LN_EPS)
            gt = gate_ref[:, ln]
            y_ref[:, ln] = (o * r * gn * gt * _sigmoid(gt)).astype(BF16)

    row = lambda i: (i, 0)
    return pl.pallas_call(
        body, name=name, grid=(s // tm,),
        out_shape=jax.ShapeDtypeStruct((s, D_MODEL), BF16),
        in_specs=[pl.BlockSpec((tm, D_MODEL), row), pl.BlockSpec((tm, D_MODEL), row),
                  pl.BlockSpec((tm, D_MODEL), lambda i: (i, 4)), pl.BlockSpec((1, HG_DIM), lambda i: (0, 0))],
        out_specs=pl.BlockSpec((tm, D_MODEL), row),
        compiler_params=_params(("parallel",)),
    )(of, ob, z, norm_g)


def _hg_post_bwd(dy, of, ob, z, norm_g, *, tm, name):
    s = of.shape[0]

    def body(dy_ref, of_ref, ob_ref, gate_ref, ng_ref, do_ref, dgate_ref, dng_ref):
        gn = ng_ref[...]
        tot = jnp.zeros((1, HG_DIM), F32)
        for h in range(HG_HEADS):
            ln = slice(HG_DIM * h, HG_DIM * (h + 1))
            d = dy_ref[:, ln].astype(F32)
            o = of_ref[:, ln] + ob_ref[:, ln]
            r = lax.rsqrt(jnp.mean(o * o, axis=-1, keepdims=True) + LN_EPS)
            ohat = o * r
            gt = gate_ref[:, ln]
            sg = _sigmoid(gt)
            don = d * gt * sg
            dgate_ref[:, ln] = (d * ohat * gn * sg * (1.0 + gt * (1.0 - sg))).astype(BF16)
            tot = tot + jnp.sum(don * ohat, axis=0, keepdims=True)
            dohat = don * gn
            do_ref[:, ln] = r * (dohat - ohat * jnp.mean(dohat * ohat, axis=-1, keepdims=True))

        @pl.when(pl.program_id(0) == 0)
        def _():
            dng_ref[...] = tot

        @pl.when(pl.program_id(0) > 0)
        def _():
            dng_ref[...] += tot

    row = lambda i: (i, 0)
    return pl.pallas_call(
        body, name=name, grid=(s // tm,),
        out_shape=(jax.ShapeDtypeStruct((s, D_MODEL), F32), jax.ShapeDtypeStruct((s, D_MODEL), BF16),
                   jax.ShapeDtypeStruct((1, HG_DIM), F32)),
        in_specs=[pl.BlockSpec((tm, D_MODEL), row), pl.BlockSpec((tm, D_MODEL), row), pl.BlockSpec((tm, D_MODEL), row),
                  pl.BlockSpec((tm, D_MODEL), lambda i: (i, 4)), pl.BlockSpec((1, HG_DIM), lambda i: (0, 0))],
        out_specs=(pl.BlockSpec((tm, D_MODEL), row), pl.BlockSpec((tm, D_MODEL), row),
                   pl.BlockSpec((1, HG_DIM), lambda i: (0, 0))),
        compiler_params=_params(("arbitrary",)),
    )(dy, of, ob, z, norm_g)


def _lb_fwd(logits, *, name):
    w = logits.shape[1]

    def body(l_ref, o_ref):
        lg = l_ref[...]
        e = jnp.exp(lg - jnp.max(lg, axis=0, keepdims=True))
        sm = e / jnp.sum(e, axis=0, keepdims=True)
        o_ref[0:1, :] = sm[1:2]
        o_ref[1:2, :] = sm[1:2] + sm[2:3] + sm[3:4]

    return pl.pallas_call(body, name=name, out_shape=jax.ShapeDtypeStruct((2, w), F32))(logits)


def _lb_bwd(logits, dlb, *, name):
    w = logits.shape[1]

    def body(l_ref, d_ref, o_ref):
        lg = l_ref[...]
        e = jnp.exp(lg - jnp.max(lg, axis=0, keepdims=True))
        sm = e / jnp.sum(e, axis=0, keepdims=True)
        d1, d3 = d_ref[0:1, :], d_ref[1:2, :]
        dot = sm[1:2] * (d1 + d3) + (sm[2:3] + sm[3:4]) * d3
        o_ref[0:1, :] = -sm[0:1] * dot
        o_ref[1:2, :] = sm[1:2] * (d1 + d3 - dot)
        o_ref[2:3, :] = sm[2:3] * (d3 - dot)
        o_ref[3:4, :] = sm[3:4] * (d3 - dot)

    return pl.pallas_call(body, name=name, out_shape=jax.ShapeDtypeStruct((4, w), F32))(logits, dlb)


def _adamw(w, g, m, v, *, tr, g_off=0, name):
    rows = w.shape[0]
    parts = g.ndim == 3
    c1 = 1.0 / (1.0 - ADAM_B1 ** ADAM_STEP)
    c2 = 1.0 / (1.0 - ADAM_B2 ** ADAM_STEP)

    def body(w_ref, g_ref, m_ref, v_ref, go_ref, d_ref, mo_ref, vo_ref):
        if parts:
            gg = g_ref[0].astype(F32)
            for i in range(1, N_DEV):
                gg = gg + g_ref[i].astype(F32)
        else:
            gg = g_ref[...]
        mm = ADAM_B1 * m_ref[...] + (1.0 - ADAM_B1) * gg
        vv = ADAM_B2 * v_ref[...] + (1.0 - ADAM_B2) * (gg * gg)
        go_ref[...] = gg
        mo_ref[...] = mm
        vo_ref[...] = vv
        d_ref[...] = -ADAM_LR * ((mm * c1) / (jnp.sqrt(vv * c2) + ADAM_EPS) + ADAM_WD * w_ref[...])

    tile = pl.BlockSpec((tr, D_MODEL), lambda i: (i, 0))
    gspec = pl.BlockSpec((N_DEV, tr, D_MODEL), lambda i: (0, i + g_off // tr, 0)) if parts else tile
    out = jax.ShapeDtypeStruct((rows, D_MODEL), F32)
    return pl.pallas_call(
        body, name=name, grid=(rows // tr,),
        out_shape=(out, out, out, out),
        in_specs=[tile, gspec, tile, tile], out_specs=(tile, tile, tile, tile),
        compiler_params=_params(("parallel",)),
    )(w, g, m, v)


def _sum8(parts, *, name):
    def body(p_ref, o_ref):
        tot = p_ref[0]
        for i in range(1, N_DEV):
            tot = tot + p_ref[i]
        o_ref[...] = tot

    return pl.pallas_call(body, name=name, out_shape=jax.ShapeDtypeStruct(parts.shape[1:], parts.dtype))(parts)


def _layer_params(i):
    j = i // 2
    mix = [("att_w_qkv", j, 1), ("att_w_o", j, 0)] if i % 2 == 0 else [("hgrn_w_in", j, 1), ("hgrn_w_o", j, 0)]
    return mix + [("ffn_w_in", i, 1), ("ffn_w_out", i, 0), ("ple_w_gate", i, 0), ("ple_w_proj", i, 1)]


def _pack_local(tree, params):
    return jnp.concatenate([tree[n][j].reshape(-1, D_MODEL) for n, j, _ in params], axis=0)


def _unpack_local(packed, params, like):
    out, r = {}, 0
    for n, _, _ in params:
        shp = like[n].shape[1:]
        k = shp[0] * shp[1] // D_MODEL
        out[n] = packed[r:r + k].reshape(shp)
        r += k
    return out


def _unpack_gathered(gathered, i, like):
    out, r = {}, 0
    for n, _, ax in _layer_params(i):
        shp = like[n].shape[1:]
        k = shp[0] * shp[1] // D_MODEL
        t = gathered[:, r:r + k].reshape((N_DEV,) + shp)
        out[n] = (jnp.moveaxis(t, 0, 1).reshape(shp[0], N_DEV * shp[1]) if ax == 1
                  else t.reshape(N_DEV * shp[0], shp[1]))
        r += k
    return out


def _pack_full(grads, params, like):
    cols = []
    for n, _, ax in params:
        shp = like[n].shape[1:]
        t = (jnp.moveaxis(grads[n].reshape(shp[0], N_DEV, shp[1]), 1, 0) if ax == 1
             else grads[n].reshape(N_DEV, shp[0], shp[1]))
        cols.append(t.reshape(N_DEV, -1, D_MODEL).astype(BF16))
    return jnp.concatenate(cols, axis=1)


def _row_tile(rows):
    return max(t for t in range(16, 257, 16) if rows % t == 0)


SMALL_ROWS = 24


def _pad_row(a):
    flat = a.reshape(1, -1)
    return jnp.pad(flat, ((0, 0), (0, D_MODEL - flat.shape[1])))


def _tile(n, pref):
    return min(n, pref)


def kernel(x, p, att_w_qkv, att_sink, att_w_o, hgrn_w_in, hgrn_lb_logits, hgrn_norm_g, hgrn_w_o, ln_mix_g, ln_mix_b, ffn_w_in, ffn_w_out, ln_ffn_g, ln_ffn_b, ple_w_gate, ple_w_proj, loss_target, m_att_w_qkv, m_att_sink, m_att_w_o, m_hgrn_w_in, m_hgrn_lb_logits, m_hgrn_norm_g, m_hgrn_w_o, m_ln_mix_g, m_ln_mix_b, m_ffn_w_in, m_ffn_w_out, m_ln_ffn_g, m_ln_ffn_b, m_ple_w_gate, m_ple_w_proj, v_att_w_qkv, v_att_sink, v_att_w_o, v_hgrn_w_in, v_hgrn_lb_logits, v_hgrn_norm_g, v_hgrn_w_o, v_ln_mix_g, v_ln_mix_b, v_ffn_w_in, v_ffn_w_out, v_ln_ffn_g, v_ln_ffn_b, v_ple_w_gate, v_ple_w_proj):
    names = ["att_w_qkv", "att_sink", "att_w_o", "hgrn_w_in", "hgrn_lb_logits", "hgrn_norm_g", "hgrn_w_o", "ln_mix_g",
             "ln_mix_b", "ffn_w_in", "ffn_w_out", "ln_ffn_g", "ln_ffn_b", "ple_w_gate", "ple_w_proj"]
    w = dict(zip(names, (att_w_qkv, att_sink, att_w_o, hgrn_w_in, hgrn_lb_logits, hgrn_norm_g, hgrn_w_o, ln_mix_g,
                         ln_mix_b, ffn_w_in, ffn_w_out, ln_ffn_g, ln_ffn_b, ple_w_gate, ple_w_proj)))
    mom = dict(zip(names, (m_att_w_qkv, m_att_sink, m_att_w_o, m_hgrn_w_in, m_hgrn_lb_logits, m_hgrn_norm_g, m_hgrn_w_o,
                           m_ln_mix_g, m_ln_mix_b, m_ffn_w_in, m_ffn_w_out, m_ln_ffn_g, m_ln_ffn_b, m_ple_w_gate,
                           m_ple_w_proj)))
    var = dict(zip(names, (v_att_w_qkv, v_att_sink, v_att_w_o, v_hgrn_w_in, v_hgrn_lb_logits, v_hgrn_norm_g, v_hgrn_w_o,
                           v_ln_mix_g, v_ln_mix_b, v_ffn_w_in, v_ffn_w_out, v_ln_ffn_g, v_ln_ffn_b, v_ple_w_gate,
                           v_ple_w_proj)))
    s = x.shape[1]
    me = 4 * lax.axis_index("x") + 2 * lax.axis_index("y") + lax.axis_index("c")
    tm = _tile(s, 512)
    tbig = _tile(s, 1024)
    ts = _tile(s // 2, 512)
    x0 = x.reshape(s, D_MODEL)
    target = loss_target.reshape(s, D_MODEL)
    pl_in = p.reshape(DEPTH, s, PLE_DIM)

    w_rows = [_pack_local(w, _layer_params(i)).astype(BF16) for i in range(DEPTH)]
    full = _unpack_gathered(_gather(w_rows[0], name="gather_weights"), 0, w)
    lb_rows = jnp.pad(hgrn_lb_logits.reshape(8, HG_DIM), ((0, 0), (0, D_MODEL - HG_DIM)))
    lb_all = _gather(lb_rows, name="gather_lb")[:, :, :HG_DIM]
    logits_full = jnp.moveaxis(lb_all, 0, 1).reshape(DEPTH, 2 * D_MODEL)
    lb = _lb_fwd(logits_full, name="lb_fwd")
    cos, sin = _rope_tables(s)

    saved = []
    xf, xb = x0, x0
    for i in range(DEPTH):
        j = i // 2
        sv = {"x": xf, "xb": xb, "w": full}
        nxt = (w_rows[i + 1], True) if i + 1 < DEPTH else None
        if i % 2 == 0:
            z = _mm(xb, full["att_w_qkv"], tm=tbig, tn=512, tk=D_MODEL, name="att_in")
            o, *more = _att_fwd(z, w["att_sink"][j], cos, sin, comm=nxt, name="att_fwd")
            w_o = full["att_w_o"]
        else:
            z = _mm(xb, full["hgrn_w_in"], tm=tbig, tn=1024, tk=D_MODEL, name="hgrn_in")
            lbl = lb[j].reshape(2, HG_HEADS, 1, HG_DIM)
            of, st_f, qs_f, dec_f, *more = _hg_fwd(z, lbl[0], rev=False, ts=ts, comm=nxt, name="hgrn_fwd")
            ob, st_b, qs_b, dec_b = _hg_fwd(z, lbl[1], rev=True, ts=ts, name="hgrn_fwd_rev")
            o = _hg_post(of, ob, z, w["hgrn_norm_g"][j].reshape(1, HG_DIM), tm=tm, name="hgrn_post")
            w_o = full["hgrn_w_o"]
            sv.update(of=of, ob=ob, st_f=st_f, st_b=st_b, lbl=lbl, qs_f=qs_f, qs_b=qs_b, dec_f=dec_f, dec_b=dec_b)
        sv.update(z=z, o=o)
        pre1, x1, x1b = _proj_ln(o, w_o, xf, w["ln_mix_g"][i:i + 1], w["ln_mix_b"][i:i + 1], tm=tm, name="mix_out_ln")
        gg, uu, act = _ffn_in(x1b, full["ffn_w_in"], tm=tm, tn=FF_TILE, name="ffn_in")
        pre2, x2b, xf, xb = _ffn_out_ple(act, full["ffn_w_out"], x1, w["ln_ffn_g"][i:i + 1], w["ln_ffn_b"][i:i + 1],
                                         pl_in[i], full["ple_w_gate"], full["ple_w_proj"], tm=tm, name="ffn_out_ple")
        sv.update(pre1=pre1, x1=x1, x1b=x1b, g=gg, u=uu, act=act, pre2=pre2, x2b=x2b)
        saved.append(sv)
        if nxt is not None:
            full = _unpack_gathered(more[0], i + 1, w)

    dx, loss_blk = _loss_head(xf, target, tm=tm, name="loss_head")
    loss = lax.psum(loss_blk[0, 0], AXES)

    small = {n: [None] * DEPTH for n in ("ln_mix_g", "ln_mix_b", "ln_ffn_g", "ln_ffn_b")}
    dlb_rows = [None] * 4
    dnorm, dsink = [None] * 2, [None] * 2
    recv_late, recv_early = [None] * DEPTH, [None] * DEPTH
    above = None
    mmw = functools.partial(_mm, ta=True, tk=_tile(s, 2048), out_dtype=BF16)
    for i in reversed(range(DEPTH)):
        j = i // 2
        sv = saved[i]
        full, gl = sv["w"], {}
        da, dpp, dy2, dy2b, small["ln_ffn_g"][i], small["ln_ffn_b"][i] = _ple_ln_bwd(
            dx, sv["x2b"], pl_in[i], full["ple_w_gate"], full["ple_w_proj"], sv["pre2"], w["ln_ffn_g"][i:i + 1],
            tm=tm, name="ple_ln_bwd")
        gl["ple_w_gate"] = mmw(sv["x2b"], da, tm=D_MODEL, tn=D_MODEL, name="dw_ple_gate")
        gl["ple_w_proj"] = mmw(pl_in[i], dpp, tm=PLE_DIM, tn=D_MODEL, name="dw_ple_proj")
        dg, du = _ffn_bwd_act(dy2b, full["ffn_w_out"], sv["g"], sv["u"], tm=_tile(s, SUB_ROWS), tn=D_FF,
                              name="ffn_bwd_act")
        gl["ffn_w_out"] = mmw(sv["act"], dy2b, tm=FF_TILE, tn=D_MODEL, name="dw_ffn_out")
        dy1, dy1b, small["ln_mix_g"][i], small["ln_mix_b"][i] = _dx_from_pieces(
            [dg, du], full["ffn_w_in"], dy2, tm=_tile(s, SUB_ROWS), ln=(sv["pre1"], w["ln_mix_g"][i:i + 1]),
            name="ffn_bwd_x_ln")
        gl["ffn_w_in"] = mmw(sv["x1b"], [dg, du], tm=D_MODEL, tn=FF_TILE, name="dw_ffn_in")
        n_out, n_inw = ("att_w_o", "att_w_qkv") if i % 2 == 0 else ("hgrn_w_o", "hgrn_w_in")
        do = _mm(dy1b, full[n_out], tm=tbig, tn=D_MODEL, tk=D_MODEL, tb=True, out_dtype=BF16, name="mix_out_bwd")
        gl[n_out] = mmw(sv["o"], dy1b, tm=D_MODEL, tn=D_MODEL, name="dw_mix_out")
        early = _pack_full(gl, _layer_params(i)[1:], w)
        comm = (early if above is None else jnp.concatenate([above, early], axis=1), False)
        if i % 2 == 0:
            dzq, part, dsk, *more = _att_bwd(sv["z"], do, w["att_sink"][j], cos, sin, comm=comm, name="att_bwd")
            dz = [dzq, _att_bwd_kv(part, cos, sin, name="att_bwd_kv")]
            dsink[j] = dsk[:, 0]
        else:
            dsum, dgate, dnorm[j] = _hg_post_bwd(do, sv["of"], sv["ob"], sv["z"], w["hgrn_norm_g"][j].reshape(1, HG_DIM),
                                                 tm=tm, name="hgrn_post_bwd")
            dq1, df1, dv1, dlb1, *more = _hg_bwd(sv["z"], sv["lbl"][0], sv["st_f"], sv["qs_f"], sv["dec_f"], dsum, None,
                                                 None, rev=False, ts=ts, comm=comm, name="hgrn_bwd")
            dq2, df2, dv2, dlb2 = _hg_bwd(sv["z"], sv["lbl"][1], sv["st_b"], sv["qs_b"], sv["dec_b"], dsum, dq1, dv1,
                                          rev=True, ts=ts, name="hgrn_bwd_rev")
            dz = [dq2, df1, df2, dv2, dgate]
            dlb_rows[2 * j] = dlb1.reshape(1, D_MODEL)
            dlb_rows[2 * j + 1] = dlb2.reshape(1, D_MODEL)
        if above is not None:
            recv_late[i + 1] = (more[0], 0)
        recv_early[i] = (more[0], 0 if above is None else above.shape[1])
        dx = _dx_from_pieces(dz, full[n_inw], dy1, tm=tm, name="mix_in_bwd")
        gl[n_inw] = mmw(sv["xb"], dz, tm=D_MODEL, tn=512, name="dw_mix_in")
        above = _pack_full(gl, _layer_params(i)[:1], w)
    grad_x = dx.reshape(x.shape)
    recv_late[0] = (_exchange(above, name="exchange_grads"), 0)

    big_out = [{n: [None] * w[n].shape[0] for n, _ in BIG} for _ in range(4)]
    for i in range(DEPTH):
        for params, (got, off) in ((_layer_params(i)[:1], recv_late[i]), (_layer_params(i)[1:], recv_early[i])):
            w_part = _pack_local(w, params)
            outs = _adamw(w_part, got, _pack_local(mom, params), _pack_local(var, params),
                          tr=_row_tile(math.gcd(w_part.shape[0], off)), g_off=off, name="adamw_big")
            for kind, packed in enumerate(outs):
                for (n, j, _), piece in zip(params, _unpack_local(packed, params, w).values()):
                    big_out[kind][n][j] = piece
    big_out = [{n: jnp.stack(v) for n, v in kind.items()} for kind in big_out]

    small_rows = jnp.concatenate(
        [jnp.concatenate(small[n], axis=0) for n in ("ln_mix_g", "ln_mix_b", "ln_ffn_g", "ln_ffn_b")] + dlb_rows
        + [_pad_row(jnp.stack(dnorm)), _pad_row(jnp.stack(dsink)), jnp.zeros((2, D_MODEL), F32)], axis=0)
    small_all = _gather(small_rows, name="gather_small")
    lbw, lbm, lbv = (t.reshape(4, 2 * HG_DIM) for t in (hgrn_lb_logits, mom["hgrn_lb_logits"], var["hgrn_lb_logits"]))
    summed = _sum8(small_all, name="sum_small")
    dlb_mine = lax.dynamic_slice_in_dim(summed[16:20].reshape(2, 2, HG_HEADS, HG_DIM), me, 1, axis=2)
    dlogits = _lb_bwd(lbw, dlb_mine.reshape(2, 2 * HG_DIM), name="lb_bwd")

    def small_pack(ln4, lbt, ng, sk):
        return jnp.concatenate([ln4[n] for n in ("ln_mix_g", "ln_mix_b", "ln_ffn_g", "ln_ffn_b")]
                               + [_pad_row(lbt), _pad_row(ng), _pad_row(sk), jnp.zeros((5, D_MODEL), F32)], axis=0)

    g_small = jnp.concatenate([summed[:16], _pad_row(dlogits), summed[20:22], jnp.zeros((5, D_MODEL), F32)], axis=0)
    souts = _adamw(small_pack(w, lbw, w["hgrn_norm_g"], w["att_sink"]), g_small,
                   small_pack(mom, lbm, mom["hgrn_norm_g"], mom["att_sink"]),
                   small_pack(var, lbv, var["hgrn_norm_g"], var["att_sink"]), tr=SMALL_ROWS, name="adamw_small")

    def small_unpack(t):
        out = {n: t[4 * k:4 * k + 4] for k, n in enumerate(("ln_mix_g", "ln_mix_b", "ln_ffn_g", "ln_ffn_b"))}
        out["hgrn_lb_logits"] = t[16].reshape(hgrn_lb_logits.shape)
        out["hgrn_norm_g"] = t[17, :2 * HG_DIM].reshape(hgrn_norm_g.shape)
        out["att_sink"] = t[18, :2 * N_Q_HEADS].reshape(att_sink.shape)
        return out

    result = [loss, grad_x]
    for big_t, small_t in zip(big_out, souts):
        merged = dict(big_t)
        merged.update(small_unpack(small_t))
        result += [merged[n] for n in names]
    return tuple(result)
```

```python
import functools
import math

import jax
import jax.numpy as jnp
from jax import lax
from jax.experimental import pallas as pl
from jax.experimental.pallas import tpu as pltpu

F32 = jnp.float32
BF16 = jnp.bfloat16

D_MODEL = 1024
DEPTH = 4
HEAD_DIM = 64
N_Q_HEADS = 16
N_KV_HEADS = 4
GROUP = 4
KV_DIM = 256
ATT_BLOCK = 128
ROPE_DIM = 16
ROPE_THETA = 500000.0
HG_HEADS = 8
HG_DIM = 128
HG_CHUNK = 64
HG_SUB = 16
D_FF = 2816
FF_TILE = 1408
SUB_ROWS = 256
PLE_DIM = 256
ALPHA = (2 * DEPTH) ** 0.25
LN_EPS = 1e-5
ADAM_LR, ADAM_B1, ADAM_B2, ADAM_EPS, ADAM_WD, ADAM_STEP = 0.001, 0.9, 0.999, 1e-08, 0.01, 10

N_DEV = 8
LANES = 128
VMEM_LIMIT = 52 * 1024 * 1024
NEG = -1e30
MESH = pl.DeviceIdType.MESH
AXES = ("x", "y", "c")

BIG = (("att_w_qkv", 2), ("att_w_o", 1), ("hgrn_w_in", 2), ("hgrn_w_o", 1), ("ffn_w_in", 2), ("ffn_w_out", 1),
       ("ple_w_gate", 1), ("ple_w_proj", 2))


def _params(sem=None, vmem=VMEM_LIMIT):
    return pltpu.CompilerParams(dimension_semantics=sem, vmem_limit_bytes=vmem)


def _sigmoid(x):
    return jax.nn.sigmoid(x)


def _direct_copies(src_ref, out_ref, send_sems, recv_sems, local_sem, gather, arrivals):
    x, y, c = lax.axis_index("x"), lax.axis_index("y"), lax.axis_index("c")
    me = 4 * x + 2 * y + c
    mine = (lambda j: src_ref) if gather else (lambda j: src_ref.at[j])
    pairs = []
    for k in range(1, N_DEV):
        px, py, pc = x ^ (k >> 2), y ^ ((k >> 1) & 1), c ^ (k & 1)
        peer = 4 * px + 2 * py + pc
        send = pltpu.make_async_remote_copy(
            src_ref=mine(peer), dst_ref=out_ref.at[me], send_sem=send_sems.at[k], recv_sem=recv_sems.at[k],
            device_id=(px, py, pc), device_id_type=MESH)
        arrival = pltpu.make_async_remote_copy(
            src_ref=mine(peer), dst_ref=out_ref.at[peer], send_sem=send_sems.at[k], recv_sem=recv_sems.at[k],
            device_id=(x, y, c), device_id_type=MESH) if arrivals else None
        pairs.append((send, arrival))
    return pltpu.make_async_copy(mine(me), out_ref.at[me], local_sem), pairs


def _direct_start(*refs, gather):
    local, pairs = _direct_copies(*refs, gather, False)
    local.start()
    for send, _ in pairs:
        send.start()


def _direct_wait(*refs, gather):
    local, pairs = _direct_copies(*refs, gather, True)
    for send, arrival in pairs:
        send.wait_send()
        arrival.wait_recv()
    local.wait()


COMM_SCRATCH = [pltpu.SemaphoreType.DMA((N_DEV,)), pltpu.SemaphoreType.DMA((N_DEV,)), pltpu.SemaphoreType.DMA]


def _exchange(src, *, gather=False, name):
    def body(*refs):
        _direct_start(*refs, gather=gather)
        _direct_wait(*refs, gather=gather)

    blk = tuple(src.shape) if gather else tuple(src.shape[1:])
    return pl.pallas_call(
        body, name=name,
        out_shape=jax.ShapeDtypeStruct((N_DEV,) + blk, src.dtype),
        in_specs=[pl.BlockSpec(memory_space=pltpu.HBM)],
        out_specs=pl.BlockSpec(memory_space=pltpu.HBM),
        scratch_shapes=COMM_SCRATCH,
    )(src)


def _call(body, *, name, grid, out_shape, in_specs, out_specs, args, scratch_shapes=(), sem, comm=None, edge=None):
    out_shape, out_specs = tuple(out_shape), tuple(out_specs)
    if comm is None:
        return pl.pallas_call(body, name=name, grid=grid, out_shape=out_shape, in_specs=list(in_specs),
                              out_specs=out_specs, scratch_shapes=list(scratch_shapes),
                              compiler_params=_params(sem))(*args)
    src, gather = comm
    n_in, n_out, n_scr = len(args), len(out_shape), len(scratch_shapes)
    blk = tuple(src.shape) if gather else tuple(src.shape[1:])
    hbm = pl.BlockSpec(memory_space=pltpu.HBM)

    def carrying(*refs):
        ins, src_ref = refs[:n_in], refs[n_in]
        outs, dst_ref = refs[n_in + 1:n_in + 1 + n_out], refs[n_in + 1 + n_out]
        own = refs[n_in + 2 + n_out:n_in + 2 + n_out + n_scr]
        comm_refs = (src_ref, dst_ref) + tuple(refs[n_in + 2 + n_out + n_scr:])
        first, last = edge()

        @pl.when(first)
        def _():
            _direct_start(*comm_refs, gather=gather)

        body(*ins, *outs, *own)

        @pl.when(last)
        def _():
            _direct_wait(*comm_refs, gather=gather)

    return pl.pallas_call(
        carrying, name=name, grid=grid,
        out_shape=out_shape + (jax.ShapeDtypeStruct((N_DEV,) + blk, src.dtype),),
        in_specs=list(in_specs) + [hbm], out_specs=out_specs + (hbm,),
        scratch_shapes=list(scratch_shapes) + COMM_SCRATCH,
        compiler_params=_params(("arbitrary",) * len(grid)),
    )(*args, src)


def _gather(src, *, name):
    def body(src_ref, out_ref, send_sems, recv_sems, local_sem):
        x, y, c = lax.axis_index("x"), lax.axis_index("y"), lax.axis_index("c")
        sibling = (x, y, 1 - c)
        chips = [(1 - x, y), (x, 1 - y), (1 - x, 1 - y)]

        def rows(px, py, pc):
            return out_ref.at[4 * px + 2 * py + pc]

        def copy(k, block, to, from_src=False):
            return pltpu.make_async_remote_copy(
                src_ref=src_ref if from_src else rows(*block), dst_ref=rows(*block), send_sem=send_sems.at[k],
                recv_sem=recv_sems.at[k], device_id=to, device_id_type=MESH)

        me = (x, y, c)
        mine = pltpu.make_async_copy(src_ref, rows(*me), local_sem)
        mine.start()
        first = [copy(0, me, sibling, from_src=True)]
        first += [copy(1 + j, me, (*chip, c), from_src=True) for j, chip in enumerate(chips)]
        for cp in first:
            cp.start()
        passed = [copy(4 + j, (*chip, c), sibling) for j, chip in enumerate(chips)]
        for j, chip in enumerate(chips):
            copy(1 + j, (*chip, c), me).wait_recv()
            passed[j].start()
        copy(0, sibling, me).wait_recv()
        for j, chip in enumerate(chips):
            copy(4 + j, (*chip, 1 - c), me).wait_recv()
        for cp in first + passed:
            cp.wait_send()
        mine.wait()

    return pl.pallas_call(
        body, name=name,
        out_shape=jax.ShapeDtypeStruct((N_DEV,) + tuple(src.shape), src.dtype),
        in_specs=[pl.BlockSpec(memory_space=pltpu.HBM)],
        out_specs=pl.BlockSpec(memory_space=pltpu.HBM),
        scratch_shapes=[pltpu.SemaphoreType.DMA((7,)), pltpu.SemaphoreType.DMA((7,)), pltpu.SemaphoreType.DMA],
    )(src)


def _mm(a, b, *, tm, tn, tk, ta=False, tb=False, out_dtype=F32, name):
    b_list = list(b) if isinstance(b, (list, tuple)) else [b]
    assert not (tb and len(b_list) > 1)
    m, kdim = (a.shape[1], a.shape[0]) if ta else a.shape
    joff = [0]
    for piece in b_list:
        joff.append(joff[-1] + (piece.shape[0] if tb else piece.shape[1]) // tn)
    nk, n = kdim // tk, joff[-1] * tn
    dims = (((0 if ta else 1,), (1 if tb else 0,)), ((), ()))

    def mine(j, p):
        return (j >= joff[p]) & (j < joff[p + 1])

    def body(*refs):
        a_ref, b_refs, o_ref = refs[0], refs[1:1 + len(b_list)], refs[1 + len(b_list)]
        acc_ref = refs[-1] if nk > 1 else None
        j, k = pl.program_id(1), pl.program_id(2)
        for p, b_ref in enumerate(b_refs):
            def step(b_ref=b_ref):
                part = lax.dot_general(a_ref[...].astype(BF16), b_ref[...].astype(BF16), dims,
                                       preferred_element_type=F32)
                if nk == 1:
                    o_ref[...] = part.astype(out_dtype)
                else:
                    _accumulate(acc_ref, part, k == 0)

            if len(b_list) == 1:
                step()
            else:
                pl.when(mine(j, p))(step)
        if nk > 1:
            @pl.when(k == nk - 1)
            def _():
                o_ref[...] = acc_ref[...].astype(out_dtype)

    def b_spec(p):
        jj = lambda j: jnp.clip(j - joff[p], 0, joff[p + 1] - joff[p] - 1)
        kk = (lambda j, k: k) if len(b_list) == 1 else (lambda j, k: jnp.where(mine(j, p), k, 0))
        return (pl.BlockSpec((tn, tk), lambda i, j, k: (jj(j), kk(j, k))) if tb
                else pl.BlockSpec((tk, tn), lambda i, j, k: (kk(j, k), jj(j))))

    a_spec = pl.BlockSpec((tk, tm), lambda i, j, k: (k, i)) if ta else pl.BlockSpec((tm, tk), lambda i, j, k: (i, k))
    return pl.pallas_call(
        body, name=name, grid=(m // tm, n // tn, nk),
        out_shape=jax.ShapeDtypeStruct((m, n), out_dtype),
        in_specs=[a_spec] + [b_spec(p) for p in range(len(b_list))],
        out_specs=pl.BlockSpec((tm, tn), lambda i, j, k: (i, j)),
        scratch_shapes=[pltpu.VMEM((tm, tn), F32)] if nk > 1 else [],
        compiler_params=_params(("parallel", "parallel", "arbitrary")),
    )(a, *b_list)


def _dx_from_pieces(pieces, w, add, *, tm, ln=None, name):
    s = pieces[0].shape[0]
    widths = [p.shape[1] for p in pieces]

    def body(*refs):
        p_refs, (w_ref, add_ref) = refs[:len(pieces)], refs[len(pieces):len(pieces) + 2]
        rest = refs[len(pieces) + 2:]
        pg = jnp.zeros((1, D_MODEL), F32)
        pb = jnp.zeros((1, D_MODEL), F32)
        for rs in _row_parts(tm):
            r = ALPHA * add_ref[rs, :]
            off = 0
            for p_ref, width in zip(p_refs, widths):
                r = r + lax.dot_general(p_ref[rs, :], w_ref[:, off:off + width], (((1,), (1,)), ((), ())),
                                        preferred_element_type=F32)
                off += width
            if ln is None:
                rest[0][rs, :] = r
            else:
                dy, qg, qb = _ln_bwd_rows(r, rest[0][rs, :], rest[1][...])
                rest[2][rs, :] = dy
                rest[3][rs, :] = dy.astype(BF16)
                pg, pb = pg + qg, pb + qb
        if ln is not None:
            _accumulate(rest[4], pg, pl.program_id(0) == 0)
            _accumulate(rest[5], pb, pl.program_id(0) == 0)

    row = lambda i: (i, 0)
    tile = pl.BlockSpec((tm, D_MODEL), row)
    vec = pl.BlockSpec((1, D_MODEL), lambda i: (0, 0))
    in_specs = ([pl.BlockSpec((tm, width), row) for width in widths]
                + [pl.BlockSpec((D_MODEL, sum(widths)), lambda i: (0, 0)), tile])
    args = list(pieces) + [w, add]
    out_shape, out_specs = jax.ShapeDtypeStruct((s, D_MODEL), F32), tile
    if ln is not None:
        in_specs += [tile, vec]
        args += list(ln)
        out_shape = (jax.ShapeDtypeStruct((s, D_MODEL), F32), jax.ShapeDtypeStruct((s, D_MODEL), BF16),
                     jax.ShapeDtypeStruct((1, D_MODEL), F32), jax.ShapeDtypeStruct((1, D_MODEL), F32))
        out_specs = (tile, tile, vec, vec)
    return pl.pallas_call(
        body, name=name, grid=(s // tm,), out_shape=out_shape, in_specs=in_specs, out_specs=out_specs,
        compiler_params=_params(("arbitrary",) if ln is not None else ("parallel",)),
    )(*args)


def _ln_bwd_rows(do, y, g):
    mu = jnp.mean(y, axis=-1, keepdims=True)
    yc = y - mu
    var = jnp.mean(yc * yc, axis=-1, keepdims=True)
    rstd = lax.rsqrt(var + LN_EPS)
    xhat = yc * rstd
    dxhat = do * g
    dy = rstd * (dxhat - jnp.mean(dxhat, axis=-1, keepdims=True) - xhat * jnp.mean(dxhat * xhat, axis=-1, keepdims=True))
    return dy, jnp.sum(do * xhat, axis=0, keepdims=True), jnp.sum(do, axis=0, keepdims=True)


def _accumulate(ref, val, first):
    @pl.when(first)
    def _():
        ref[...] = val

    @pl.when(jnp.logical_not(first))
    def _():
        ref[...] += val


def _layer_norm_rows(y, g, b):
    mu = jnp.mean(y, axis=-1, keepdims=True)
    yc = y - mu
    var = jnp.mean(yc * yc, axis=-1, keepdims=True)
    return yc * lax.rsqrt(var + LN_EPS) * g + b


def _proj_ln(a, w, res, g, b, *, tm, name):
    s, kdim = a.shape

    def body(a_ref, w_ref, res_ref, g_ref, b_ref, pre_ref, o_ref, obf_ref):
        for rs in _row_parts(tm):
            h = jnp.dot(a_ref[rs, :], w_ref[...], preferred_element_type=F32)
            pre = ALPHA * res_ref[rs, :] + h
            out = _layer_norm_rows(pre, g_ref[...], b_ref[...])
            pre_ref[rs, :] = pre
            o_ref[rs, :] = out
            obf_ref[rs, :] = out.astype(BF16)

    row = lambda i: (i, 0)
    fix = lambda i: (0, 0)
    return pl.pallas_call(
        body, name=name, grid=(s // tm,),
        out_shape=(jax.ShapeDtypeStruct((s, D_MODEL), F32), jax.ShapeDtypeStruct((s, D_MODEL), F32),
                   jax.ShapeDtypeStruct((s, D_MODEL), BF16)),
        in_specs=[pl.BlockSpec((tm, kdim), row), pl.BlockSpec((kdim, D_MODEL), fix), pl.BlockSpec((tm, D_MODEL), row),
                  pl.BlockSpec((1, D_MODEL), fix), pl.BlockSpec((1, D_MODEL), fix)],
        out_specs=(pl.BlockSpec((tm, D_MODEL), row),) * 3,
        compiler_params=_params(("parallel",)),
    )(a, w, res, g, b)


def _row_parts(tm):
    sub = min(tm, SUB_ROWS)
    return [pl.ds(r * sub, sub) for r in range(tm // sub)]


def _ffn_in(xbf, w, *, tm, tn, name):
    s = xbf.shape[0]
    nj = D_FF // tn

    def body(x_ref, wg_ref, wu_ref, g_ref, u_ref, act_ref):
        for rs in _row_parts(tm):
            xv = x_ref[rs, :]
            gg = jnp.dot(xv, wg_ref[...], preferred_element_type=F32)
            uu = jnp.dot(xv, wu_ref[...], preferred_element_type=F32)
            g_ref[rs, :] = gg.astype(BF16)
            u_ref[rs, :] = uu.astype(BF16)
            act_ref[rs, :] = (gg * _sigmoid(gg) * uu).astype(BF16)

    out = jax.ShapeDtypeStruct((s, D_FF), BF16)
    tile = pl.BlockSpec((tm, tn), lambda j, i: (i, j))
    return pl.pallas_call(
        body, name=name, grid=(nj, s // tm),
        out_shape=(out, out, out),
        in_specs=[pl.BlockSpec((tm, D_MODEL), lambda j, i: (i, 0)), pl.BlockSpec((D_MODEL, tn), lambda j, i: (0, j)),
                  pl.BlockSpec((D_MODEL, tn), lambda j, i: (0, j + nj))],
        out_specs=(tile, tile, tile),
        compiler_params=_params(("parallel", "parallel")),
    )(xbf, w, w)


def _ffn_bwd_act(dybf, w_out, g, u, *, tm, tn, name):
    s = dybf.shape[0]

    def body(dy_ref, w_ref, g_ref, u_ref, dg_ref, du_ref):
        for rs in _row_parts(tm):
            dact = lax.dot_general(dy_ref[rs, :], w_ref[...], (((1,), (1,)), ((), ())), preferred_element_type=F32)
            gg = g_ref[rs, :].astype(F32)
            uu = u_ref[rs, :].astype(F32)
            sg = _sigmoid(gg)
            dg_ref[rs, :] = (dact * uu * sg * (1.0 + gg * (1.0 - sg))).astype(BF16)
            du_ref[rs, :] = (dact * gg * sg).astype(BF16)

    out = jax.ShapeDtypeStruct((s, D_FF), BF16)
    tile = pl.BlockSpec((tm, tn), lambda j, i: (i, j))
    return pl.pallas_call(
        body, name=name, grid=(D_FF // tn, s // tm),
        out_shape=(out, out),
        in_specs=[pl.BlockSpec((tm, D_MODEL), lambda j, i: (i, 0)), pl.BlockSpec((tn, D_MODEL), lambda j, i: (j, 0)),
                  tile, tile],
        out_specs=(tile, tile),
        compiler_params=_params(("parallel", "parallel")),
    )(dybf, w_out, g, u)


def _ffn_out_ple(act, w_out, res, g, b, p, w_gate, w_proj, *, tm, name):
    s = act.shape[0]

    def body(a_ref, w_ref, res_ref, g_ref, b_ref, p_ref, wg_ref, wp_ref, pre_ref, x2bf_ref, o_ref, obf_ref):
        for rs in _row_parts(tm):
            pre = ALPHA * res_ref[rs, :] + jnp.dot(a_ref[rs, :], w_ref[...], preferred_element_type=F32)
            x2 = _layer_norm_rows(pre, g_ref[...], b_ref[...])
            x2bf = x2.astype(BF16)
            pre_ref[rs, :] = pre
            x2bf_ref[rs, :] = x2bf
            gate = jnp.dot(x2bf, wg_ref[...], preferred_element_type=F32)
            pp = jnp.dot(p_ref[rs, :].astype(BF16), wp_ref[...], preferred_element_type=F32)
            out = x2 + _sigmoid(gate) * pp
            o_ref[rs, :] = out
            obf_ref[rs, :] = out.astype(BF16)

    row = lambda i: (i, 0)
    fix = lambda i: (0, 0)
    tile = pl.BlockSpec((tm, D_MODEL), row)
    vec = pl.BlockSpec((1, D_MODEL), fix)
    act_t = lambda dt: jax.ShapeDtypeStruct((s, D_MODEL), dt)
    return pl.pallas_call(
        body, name=name, grid=(s // tm,),
        out_shape=(act_t(F32), act_t(BF16), act_t(F32), act_t(BF16)),
        in_specs=[pl.BlockSpec((tm, D_FF), row), pl.BlockSpec((D_FF, D_MODEL), fix), tile, vec, vec,
                  pl.BlockSpec((tm, PLE_DIM), row), pl.BlockSpec((D_MODEL, D_MODEL), fix),
                  pl.BlockSpec((PLE_DIM, D_MODEL), fix)],
        out_specs=(tile, tile, tile, tile),
        compiler_params=_params(("parallel",)),
    )(act, w_out, res, g, b, p, w_gate, w_proj)


def _ple_ln_bwd(dx3, x2bf, p, w_gate, w_proj, pre, g, *, tm, name):
    s = dx3.shape[0]

    def body(d_ref, xbf_ref, p_ref, wg_ref, wp_ref, pre_ref, g_ref, da_ref, dpp_ref, dy_ref, dybf_ref, dg_ref, db_ref):
        pg = jnp.zeros((1, D_MODEL), F32)
        pb = jnp.zeros((1, D_MODEL), F32)
        for rs in _row_parts(tm):
            d = d_ref[rs, :]
            a = jnp.dot(xbf_ref[rs, :], wg_ref[...], preferred_element_type=F32)
            pp = jnp.dot(p_ref[rs, :].astype(BF16), wp_ref[...], preferred_element_type=F32)
            sg = _sigmoid(a)
            da = (d * pp * sg * (1.0 - sg)).astype(BF16)
            da_ref[rs, :] = da
            dpp_ref[rs, :] = (d * sg).astype(BF16)
            dx2 = d + lax.dot_general(da, wg_ref[...], (((1,), (1,)), ((), ())), preferred_element_type=F32)
            dy, qg, qb = _ln_bwd_rows(dx2, pre_ref[rs, :], g_ref[...])
            dy_ref[rs, :] = dy
            dybf_ref[rs, :] = dy.astype(BF16)
            pg, pb = pg + qg, pb + qb
        _accumulate(dg_ref, pg, pl.program_id(0) == 0)
        _accumulate(db_ref, pb, pl.program_id(0) == 0)

    row = lambda i: (i, 0)
    fix = lambda i: (0, 0)
    tile = pl.BlockSpec((tm, D_MODEL), row)
    vec = pl.BlockSpec((1, D_MODEL), fix)
    act = lambda dt: jax.ShapeDtypeStruct((s, D_MODEL), dt)
    return pl.pallas_call(
        body, name=name, grid=(s // tm,),
        out_shape=(act(BF16), act(BF16), act(F32), act(BF16), jax.ShapeDtypeStruct((1, D_MODEL), F32),
                   jax.ShapeDtypeStruct((1, D_MODEL), F32)),
        in_specs=[tile, tile, pl.BlockSpec((tm, PLE_DIM), row), pl.BlockSpec((D_MODEL, D_MODEL), fix),
                  pl.BlockSpec((PLE_DIM, D_MODEL), fix), tile, vec],
        out_specs=(tile, tile, tile, tile, vec, vec),
        compiler_params=_params(("arbitrary",)),
    )(dx3, x2bf, p, w_gate, w_proj, pre, g)


def _loss_head(y, target, *, tm, name):
    s = y.shape[0]

    def body(y_ref, t_ref, dy_ref, loss_ref, acc_ref):
        err = y_ref[...] - t_ref[...]
        dy_ref[...] = err * (1.0 / D_MODEL)
        part = jnp.sum(err * err, axis=0, keepdims=True)

        @pl.when(pl.program_id(0) == 0)
        def _():
            acc_ref[...] = part

        @pl.when(pl.program_id(0) > 0)
        def _():
            acc_ref[...] += part

        @pl.when(pl.program_id(0) == pl.num_programs(0) - 1)
        def _():
            tot = jnp.sum(acc_ref[...], axis=1, keepdims=True) * (0.5 / D_MODEL)
            loss_ref[...] = jnp.broadcast_to(tot, (8, LANES))

    row = lambda i: (i, 0)
    return pl.pallas_call(
        body, name=name, grid=(s // tm,),
        out_shape=(jax.ShapeDtypeStruct((s, D_MODEL), F32), jax.ShapeDtypeStruct((8, LANES), F32)),
        in_specs=[pl.BlockSpec((tm, D_MODEL), row), pl.BlockSpec((tm, D_MODEL), row)],
        out_specs=(pl.BlockSpec((tm, D_MODEL), row), pl.BlockSpec((8, LANES), lambda i: (0, 0))),
        scratch_shapes=[pltpu.VMEM((1, D_MODEL), F32)],
        compiler_params=_params(("arbitrary",)),
    )(y, target)


def _rope_tables(s):
    inv = ROPE_THETA ** (-jnp.arange(0, ROPE_DIM, 2, dtype=F32) / ROPE_DIM)
    ang = jnp.arange(s, dtype=F32)[:, None] * inv[None, :]
    cos, sin = jnp.cos(ang), jnp.sin(ang)
    ones = jnp.ones((s, HEAD_DIM - ROPE_DIM), F32)
    c_head = jnp.concatenate([cos, cos, ones], axis=1)
    s_head = jnp.concatenate([-sin, sin, 0.0 * ones], axis=1)
    return jnp.concatenate([c_head, c_head], axis=1), jnp.concatenate([s_head, s_head], axis=1)


def _rope(v, cos, sin):
    n = v.shape[1] // LANES
    width = v.shape[1]
    cos_w = jnp.tile(cos, (1, n)) if n > 1 else cos
    sin_w = jnp.tile(sin, (1, n)) if n > 1 else sin
    dim = lax.broadcasted_iota(jnp.int32, (1, width), 1) % HEAD_DIM
    partner = jnp.where(dim < ROPE_DIM // 2, pltpu.roll(v, width - ROPE_DIM // 2, 1), pltpu.roll(v, ROPE_DIM // 2, 1))
    return v * cos_w + partner * sin_w


def _unrope(dv, cos, sin):
    n = dv.shape[1] // LANES
    width = dv.shape[1]
    cos_w = jnp.tile(cos, (1, n)) if n > 1 else cos
    sin_w = jnp.tile(sin, (1, n)) if n > 1 else sin
    t = dv * sin_w
    dim = lax.broadcasted_iota(jnp.int32, (1, width), 1) % HEAD_DIM
    partner = jnp.where(dim < ROPE_DIM // 2, pltpu.roll(t, width - ROPE_DIM // 2, 1),
                        jnp.where(dim < ROPE_DIM, pltpu.roll(t, ROPE_DIM // 2, 1), 0.0))
    return dv * cos_w + partner


def _att_mask(i, nb):
    rows = GROUP * ATT_BLOCK
    r = lax.broadcasted_iota(jnp.int32, (rows, 3 * ATT_BLOCK), 0) % ATT_BLOCK
    cidx = lax.broadcasted_iota(jnp.int32, (rows, 3 * ATT_BLOCK), 1)
    rel = r + ATT_BLOCK - cidx
    ok = (rel <= ATT_BLOCK) & (rel >= -ATT_BLOCK)
    ok = ok & ((cidx >= ATT_BLOCK) | (i > 0)) & ((cidx < 2 * ATT_BLOCK) | (i < nb - 1))
    return ok


def _att_mask_t(i, nb):
    cols = GROUP * ATT_BLOCK
    cidx = lax.broadcasted_iota(jnp.int32, (3 * ATT_BLOCK, cols), 0)
    r = lax.broadcasted_iota(jnp.int32, (3 * ATT_BLOCK, cols), 1) % ATT_BLOCK
    rel = r + ATT_BLOCK - cidx
    ok = (rel <= ATT_BLOCK) & (rel >= -ATT_BLOCK)
    return ok & ((cidx >= ATT_BLOCK) | (i > 0)) & ((cidx < 2 * ATT_BLOCK) | (i < nb - 1))


def _sink_lanes(sink_ref, h):
    cols = GROUP * ATT_BLOCK
    grp = lax.broadcasted_iota(jnp.int32, (1, cols), 1) // ATT_BLOCK
    out = jnp.zeros((1, cols), F32)
    for gq in range(GROUP):
        out = jnp.where(grp == gq, sink_ref[GROUP * h + gq], out)
    return out


def _half_mask(half):
    lane = lax.broadcasted_iota(jnp.int32, (1, LANES), 1)
    return (lane // HEAD_DIM) == half


def _stack_q(q, h):
    parts = []
    for gq in range(GROUP):
        n = GROUP * h + gq
        grp = q[:, LANES * (n // 2):LANES * (n // 2 + 1)]
        grp = jnp.where(_half_mask(n % 2), grp, 0.0)
        if n % 2 != h % 2:
            grp = pltpu.roll(grp, HEAD_DIM, 1)
        parts.append(grp)
    return jnp.concatenate(parts, axis=0)


def _unstack_q(stacked, h, acc):
    for gq in range(GROUP):
        n = GROUP * h + gq
        grp = stacked[ATT_BLOCK * gq:ATT_BLOCK * (gq + 1), :]
        grp = jnp.where(_half_mask(h % 2), grp, 0.0)
        if n % 2 != h % 2:
            grp = pltpu.roll(grp, HEAD_DIM, 1)
        acc[n // 2] = grp if acc[n // 2] is None else acc[n // 2] + grp
    return acc


def _sink_rows(sink_ref, h):
    rows = GROUP * ATT_BLOCK
    grp = lax.broadcasted_iota(jnp.int32, (rows, 1), 0) // ATT_BLOCK
    out = jnp.zeros((rows, 1), F32)
    for gq in range(GROUP):
        out = jnp.where(grp == gq, sink_ref[GROUP * h + gq], out)
    return out


def _att_probs(qs, kh, sink, valid):
    s = lax.dot_general(qs, kh, (((1,), (1,)), ((), ())), preferred_element_type=F32)
    s = jnp.where(valid, s, NEG)
    m = jnp.maximum(jnp.max(s, axis=-1, keepdims=True), sink)
    p = jnp.exp(s - m)
    es = jnp.exp(sink - m)
    den = jnp.sum(p, axis=-1, keepdims=True) + es
    inv = 1.0 / den
    return p * inv, es * inv


def _att_specs(nb):
    prev = lambda i: (jnp.maximum(i - 1, 0), 0)
    cur = lambda i: (i, 0)
    nxt = lambda i: (jnp.minimum(i + 1, nb - 1), 0)
    kv = lambda f: (lambda i: (f(i)[0], 2))
    tab = [pl.BlockSpec((ATT_BLOCK, LANES), f) for f in (cur, prev, cur, nxt)]
    z = [pl.BlockSpec((ATT_BLOCK, D_MODEL), cur)] + [pl.BlockSpec((ATT_BLOCK, 2 * KV_DIM), kv(f)) for f in (prev, cur, nxt)]
    return z, tab


def _att_load(zq_ref, kp_ref, kc_ref, kn_ref, cq_ref, sq_ref, cp_ref, sp_ref, cc_ref, sc_ref, cn_ref, sn_ref):
    q = (_rope(zq_ref[...], cq_ref[...], sq_ref[...]) * (HEAD_DIM ** -0.5))
    ks, vs = [], []
    for ref, c_ref, s_ref in ((kp_ref, cp_ref, sp_ref), (kc_ref, cc_ref, sc_ref), (kn_ref, cn_ref, sn_ref)):
        kvb = ref[...]
        ks.append(_rope(kvb[:, :KV_DIM], c_ref[...], s_ref[...]))
        vs.append(kvb[:, KV_DIM:])
    return q, jnp.concatenate(ks, axis=0).astype(BF16), jnp.concatenate(vs, axis=0).astype(BF16)


def _att_fwd(z, sink, cos, sin, *, comm=None, name):
    s = z.shape[0]
    nb = s // ATT_BLOCK

    def body(zq_ref, kp_ref, kc_ref, kn_ref, cq_ref, cp_ref, cc_ref, cn_ref, sq_ref, sp_ref, sc_ref, sn_ref, sink_ref,
             o_ref):
        i = pl.program_id(0)
        q, k, v = _att_load(zq_ref, kp_ref, kc_ref, kn_ref, cq_ref, sq_ref, cp_ref, sp_ref, cc_ref, sc_ref, cn_ref, sn_ref)
        valid = _att_mask(i, nb)
        acc = [None] * (N_Q_HEADS // 2)
        for h in range(N_KV_HEADS):
            lanes = slice(LANES * (h // 2), LANES * (h // 2 + 1))
            qs = _stack_q(q, h).astype(BF16)
            prob, _ = _att_probs(qs, k[:, lanes], _sink_rows(sink_ref, h), valid)
            oh = jnp.dot(prob.astype(BF16), v[:, lanes], preferred_element_type=F32)
            acc = _unstack_q(oh, h, acc)
        o_ref[...] = jnp.concatenate(acc, axis=1).astype(BF16)

    zspecs, tab = _att_specs(nb)
    return _call(
        body, name=name, grid=(nb,),
        out_shape=(jax.ShapeDtypeStruct((s, D_MODEL), BF16),),
        in_specs=zspecs + tab + tab + [pl.BlockSpec(memory_space=pltpu.SMEM)],
        out_specs=(pl.BlockSpec((ATT_BLOCK, D_MODEL), lambda i: (i, 0)),),
        args=(z, z, z, z, cos, cos, cos, cos, sin, sin, sin, sin, sink), sem=("parallel",), comm=comm,
        edge=lambda: (pl.program_id(0) == 0, pl.program_id(0) == nb - 1))


def _att_bwd(z, do, sink, cos, sin, *, comm=None, name):
    s = z.shape[0]
    nb = s // ATT_BLOCK

    def body(zq_ref, kp_ref, kc_ref, kn_ref, cq_ref, cp_ref, cc_ref, cn_ref, sq_ref, sp_ref, sc_ref, sn_ref, sink_ref,
             do_ref, dq_ref, part_ref, dsink_ref):
        i = pl.program_id(0)
        q, k, v = _att_load(zq_ref, kp_ref, kc_ref, kn_ref, cq_ref, sq_ref, cp_ref, sp_ref, cc_ref, sc_ref, cn_ref, sn_ref)
        valid = _att_mask_t(i, nb)
        dout = do_ref[...].astype(F32)
        dq_acc = [None] * (N_Q_HEADS // 2)
        dk_acc = [None] * 2
        dv_acc = [None] * 2
        rows = []
        nt = (((1,), (1,)), ((), ()))
        for h in range(N_KV_HEADS):
            grp = h // 2
            lanes = slice(LANES * grp, LANES * (grp + 1))
            qs = _stack_q(q, h).astype(BF16)
            dos = _stack_q(dout, h).astype(BF16)
            sink = _sink_lanes(sink_ref, h)
            sc = jnp.where(valid, lax.dot_general(k[:, lanes], qs, nt, preferred_element_type=F32), NEG)
            m = jnp.maximum(jnp.max(sc, axis=0, keepdims=True), sink)
            p = jnp.exp(sc - m)
            es = jnp.exp(sink - m)
            inv = 1.0 / (jnp.sum(p, axis=0, keepdims=True) + es)
            prob = p * inv
            dprob = lax.dot_general(v[:, lanes], dos, nt, preferred_element_type=F32)
            delta = jnp.sum(prob * dprob, axis=0, keepdims=True)
            dsc = (prob * (dprob - delta)).astype(BF16)
            dsk = -(es * inv) * delta
            for gq in range(GROUP):
                tot = jnp.sum(dsk[:, ATT_BLOCK * gq:ATT_BLOCK * (gq + 1)], axis=1, keepdims=True)
                rows.append(jnp.broadcast_to(tot, (1, LANES)))
            dqs = lax.dot_general(dsc, k[:, lanes], (((0,), (0,)), ((), ())), preferred_element_type=F32)
            dq_acc = _unstack_q(dqs, h, dq_acc)
            dkh = jnp.dot(dsc, qs, preferred_element_type=F32)
            dvh = jnp.dot(prob.astype(BF16), dos, preferred_element_type=F32)
            dk_acc[grp] = dkh if dk_acc[grp] is None else dk_acc[grp] + dkh
            dv_acc[grp] = dvh if dv_acc[grp] is None else dv_acc[grp] + dvh
        dq = jnp.concatenate(dq_acc, axis=1) * (HEAD_DIM ** -0.5)
        dq_ref[...] = _unrope(dq, cq_ref[...], sq_ref[...]).astype(BF16)
        part = jnp.concatenate(dk_acc + dv_acc, axis=1)
        for wdw in range(3):
            part_ref[wdw] = part[ATT_BLOCK * wdw:ATT_BLOCK * (wdw + 1), :]
        dsink = jnp.concatenate(rows, axis=0)

        @pl.when(i == 0)
        def _():
            dsink_ref[...] = dsink

        @pl.when(i > 0)
        def _():
            dsink_ref[...] += dsink

    zspecs, tab = _att_specs(nb)
    return _call(
        body, name=name, grid=(nb,),
        out_shape=(jax.ShapeDtypeStruct((s, D_MODEL), BF16), jax.ShapeDtypeStruct((nb, 3, ATT_BLOCK, 2 * KV_DIM), F32),
                   jax.ShapeDtypeStruct((N_Q_HEADS, LANES), F32)),
        in_specs=zspecs + tab + tab + [pl.BlockSpec(memory_space=pltpu.SMEM), pl.BlockSpec((ATT_BLOCK, D_MODEL), lambda i: (i, 0))],
        out_specs=(pl.BlockSpec((ATT_BLOCK, D_MODEL), lambda i: (i, 0)),
                   pl.BlockSpec((None, 3, ATT_BLOCK, 2 * KV_DIM), lambda i: (i, 0, 0, 0)),
                   pl.BlockSpec((N_Q_HEADS, LANES), lambda i: (0, 0))),
        args=(z, z, z, z, cos, cos, cos, cos, sin, sin, sin, sin, sink, do), sem=("arbitrary",), comm=comm,
        edge=lambda: (pl.program_id(0) == 0, pl.program_id(0) == nb - 1))


def _att_bwd_kv(part, cos, sin, *, name):
    nb = part.shape[0]

    def body(pn_ref, pc_ref, pp_ref, c_ref, s_ref, o_ref):
        j = pl.program_id(0)
        tot = pc_ref[...]
        tot = tot + jnp.where(j < nb - 1, pn_ref[...], 0.0)
        tot = tot + jnp.where(j > 0, pp_ref[...], 0.0)
        dk = _unrope(tot[:, :KV_DIM], c_ref[...], s_ref[...])
        o_ref[...] = jnp.concatenate([dk, tot[:, KV_DIM:]], axis=1).astype(BF16)

    blk = (None, None, ATT_BLOCK, 2 * KV_DIM)
    return pl.pallas_call(
        body, name=name, grid=(nb,),
        out_shape=jax.ShapeDtypeStruct((nb * ATT_BLOCK, 2 * KV_DIM), BF16),
        in_specs=[pl.BlockSpec(blk, lambda j: (jnp.minimum(j + 1, nb - 1), 0, 0, 0)),
                  pl.BlockSpec(blk, lambda j: (j, 1, 0, 0)),
                  pl.BlockSpec(blk, lambda j: (jnp.maximum(j - 1, 0), 2, 0, 0)),
                  pl.BlockSpec((ATT_BLOCK, LANES), lambda j: (j, 0)), pl.BlockSpec((ATT_BLOCK, LANES), lambda j: (j, 0))],
        out_specs=pl.BlockSpec((ATT_BLOCK, 2 * KV_DIM), lambda j: (j, 0)),
        compiler_params=_params(("parallel",)),
    )(part, part, part, cos, sin)


def _bdot(a, b, dims):
    return lax.dot_general(a.astype(BF16), b.astype(BF16), (dims, ((), ())), preferred_element_type=F32)


@jax.custom_vjp
def _dot_nn(a, b):
    return _bdot(a, b, ((1,), (0,)))


@jax.custom_vjp
def _dot_nt(a, b):
    return _bdot(a, b, ((1,), (1,)))


@jax.custom_vjp
def _dot_tn(a, b):
    return _bdot(a, b, ((0,), (0,)))


_dot_nn.defvjp(lambda a, b: (_dot_nn(a, b), (a, b)), lambda r, d: (_dot_nt(d, r[1]), _dot_tn(r[0], d)))
_dot_nt.defvjp(lambda a, b: (_dot_nt(a, b), (a, b)), lambda r, d: (_dot_nn(d, r[1]), _dot_tn(d, r[0])))
_dot_tn.defvjp(lambda a, b: (_dot_tn(a, b), (a, b)), lambda r, d: (_dot_nt(r[1], d), _dot_nn(r[0], d)))


def _running_sum(v, up):
    n = v.shape[0]
    rows = lax.broadcasted_iota(jnp.int32, v.shape, 0)
    sh = 1
    while sh < n:
        if up:
            v = v + jnp.where(rows < n - sh, pltpu.roll(v, n - sh, 0), 0.0)
        else:
            v = v + jnp.where(rows >= sh, pltpu.roll(v, sh, 0), 0.0)
        sh *= 2
    return v


@jax.custom_vjp
def _sum_down(v):
    return _running_sum(v, False)


@jax.custom_vjp
def _sum_up(v):
    return _running_sum(v, True)


_sum_down.defvjp(lambda v: (_running_sum(v, False), None), lambda _, d: (_sum_up(d),))
_sum_up.defvjp(lambda v: (_running_sum(v, True), None), lambda _, d: (_sum_down(d),))

N_SUB = HG_CHUNK // HG_SUB


def _fold_blocks(v):
    out = v[:HG_CHUNK]
    for i in range(1, N_SUB):
        out = out + v[HG_CHUNK * i:HG_CHUNK * (i + 1)]
    return out


@jax.custom_vjp
def _fold(v):
    return _fold_blocks(v)


_fold.defvjp(lambda v: (_fold_blocks(v), None), lambda _, d: (jnp.concatenate([d] * N_SUB, axis=0),))


def _hg_consts(rev):
    c, sub = HG_CHUNK, HG_SUB
    rowpos = lax.broadcasted_iota(jnp.int32, (c, HG_DIM), 0)
    rr = lax.broadcasted_iota(jnp.int32, (N_SUB * c, c), 0)
    key = lax.broadcasted_iota(jnp.int32, (N_SUB * c, c), 1)
    blk, qry = rr // c, rr % c
    if rev:
        rowpos, qry, key = c - 1 - rowpos, c - 1 - qry, c - 1 - key
    keep = (key // sub == blk) & (key <= qry)
    return keep, rowpos


def _pick(b, rowpos, t):
    return jnp.sum(jnp.where(rowpos == t, b, 0.0), axis=0, keepdims=True)


def _hg_local(zq, zf, zv, lbv, consts, dots):
    dot_nn, dot_nt, dot_tn, cum, fold = dots
    keep, rowpos = consts
    sig = _sigmoid(zf)
    f = lbv + (1.0 - lbv) * sig
    g = jnp.log(f)
    k = (1.0 - lbv) * (1.0 - sig)
    q = zq * _sigmoid(zq)
    b = cum(g)
    ends = [_pick(b, rowpos, (j + 1) * HG_SUB - 1) for j in range(N_SUB)]
    b_last = ends[-1]
    b_end = b_last
    for j in range(N_SUB - 1):
        b_end = jnp.where(rowpos // HG_SUB == j, ends[j], b_end)
    kc = k * jnp.exp(b_end - b)
    qbs = [q * jnp.exp(jnp.where(rowpos >= j * HG_SUB, b - ends[j], 0.0)) for j in range(N_SUB)]
    scores = fold(jnp.where(keep, dot_nt(jnp.concatenate(qbs, axis=0), kc), 0.0))
    return dot_nn(scores, zv), q * jnp.exp(b), k * jnp.exp(b_last - b), jnp.exp(b_last)


def _hg_chunk(zq, zf, zv, lbv, st, consts, dots):
    intra, qs, kd, dec = _hg_local(zq, zf, zv, lbv, consts, dots)
    return intra + dots[1](qs, st), dec * st + dots[2](zv, kd)


def _hg_dots(diff, rev):
    if diff:
        return _dot_nn, _dot_nt, _dot_tn, (_sum_up if rev else _sum_down), _fold
    return (lambda a, b: _bdot(a, b, ((1,), (0,))), lambda a, b: _bdot(a, b, ((1,), (1,))),
            lambda a, b: _bdot(a, b, ((0,), (0,))), lambda v: _running_sum(v, rev), _fold_blocks)


def _hg_specs(ts, nch, trow):
    tile = pl.BlockSpec((ts, HG_DIM), lambda h, t: (trow(t), h))
    mats = pl.BlockSpec((None, nch, HG_DIM, HG_DIM), lambda h, t: (h, trow(t), 0, 0))
    vecs = pl.BlockSpec((None, nch, 1, HG_DIM), lambda h, t: (h, trow(t), 0, 0))
    return tile, mats, vecs


def _time_order(nch, rev):
    return range(nch - 1, -1, -1) if rev else range(nch)


def _chunk_rows(c):
    return pl.ds(c * HG_CHUNK, HG_CHUNK)


def _hg_edge(nt):
    h, t = pl.program_id(0), pl.program_id(1)
    return (h == 0) & (t == 0), (h == HG_HEADS - 1) & (t == nt - 1)


def _hg_fwd(z, lb, *, rev, ts, comm=None, name):
    s = z.shape[0]
    nt = s // ts
    nch = ts // HG_CHUNK
    fcol = HG_HEADS * (2 if rev else 1)

    def body(zq_ref, zf_ref, zv_ref, lb_ref, o_ref, st_ref, qs_ref, dec_ref, state_ref):
        @pl.when(pl.program_id(1) == 0)
        def _():
            state_ref[...] = jnp.zeros_like(state_ref)

        consts = _hg_consts(rev)
        dots = _hg_dots(False, rev)
        lbv = lb_ref[...]
        local = {}
        for c in range(nch):
            rows = _chunk_rows(c)
            zv = zv_ref[rows, :]
            intra, qs, kd, dec = _hg_local(zq_ref[rows, :], zf_ref[rows, :], zv, lbv, consts, dots)
            qs = qs.astype(BF16)
            qs_ref[rows, :] = qs
            dec_ref[c] = dec
            local[c] = (intra, qs, dec, dots[2](zv, kd))
        st = state_ref[...]
        for c in _time_order(nch, rev):
            intra, qs, dec, upd = local[c]
            st_ref[c] = st.astype(BF16)
            o_ref[_chunk_rows(c), :] = intra + _bdot(qs, st, ((1,), (1,)))
            st = dec * st + upd
        state_ref[...] = st

    trow = (lambda t: nt - 1 - t) if rev else (lambda t: t)
    col = lambda off: pl.BlockSpec((ts, HG_DIM), lambda h, t: (trow(t), off + h))
    tile, mats, vecs = _hg_specs(ts, nch, trow)
    nchunks = s // HG_CHUNK
    return _call(
        body, name=name, grid=(HG_HEADS, nt),
        out_shape=(jax.ShapeDtypeStruct((s, D_MODEL), F32),
                   jax.ShapeDtypeStruct((HG_HEADS, nchunks, HG_DIM, HG_DIM), BF16),
                   jax.ShapeDtypeStruct((s, D_MODEL), BF16),
                   jax.ShapeDtypeStruct((HG_HEADS, nchunks, 1, HG_DIM), F32)),
        in_specs=[col(0), col(fcol), col(3 * HG_HEADS), pl.BlockSpec((None, 1, HG_DIM), lambda h, t: (h, 0, 0))],
        out_specs=(tile, mats, tile, vecs), args=(z, z, z, lb),
        scratch_shapes=[pltpu.VMEM((HG_DIM, HG_DIM), F32)], sem=("parallel", "arbitrary"), comm=comm,
        edge=lambda: _hg_edge(nt))


def _hg_bwd(z, lb, states, qs, dec, dout, addq, addv, *, rev, ts, comm=None, name):
    s = z.shape[0]
    nt = s // ts
    nch = ts // HG_CHUNK
    fcol = HG_HEADS * (2 if rev else 1)
    has_add = addq is not None

    def body(*refs):
        zq_ref, zf_ref, zv_ref, lb_ref, st_ref, qs_ref, dec_ref, do_ref = refs[:8]
        aq_ref, av_ref = (refs[8], refs[9]) if has_add else (None, None)
        dq_ref, df_ref, dv_ref, dlb_ref, grad_ref = refs[-5:]

        @pl.when(pl.program_id(1) == 0)
        def _():
            grad_ref[...] = jnp.zeros_like(grad_ref)

        consts = _hg_consts(rev)
        dots = _hg_dots(True, rev)
        lbv = lb_ref[...]
        prods = {c: _bdot(do_ref[_chunk_rows(c), :], qs_ref[_chunk_rows(c), :], ((0,), (0,))) for c in range(nch)}
        gleave = {}
        gr = grad_ref[...]
        for c in reversed(_time_order(nch, rev)):
            gleave[c] = gr
            gr = dec_ref[c] * gr + prods[c]
        grad_ref[...] = gr
        dlb_blk = jnp.zeros((1, HG_DIM), F32)
        for c in range(nch):
            rows = _chunk_rows(c)
            fn = lambda a, b2, c2, d2, e2: _hg_chunk(a, b2, c2, d2, e2, consts, dots)
            _, pull = jax.vjp(fn, zq_ref[rows, :], zf_ref[rows, :], zv_ref[rows, :], lbv, st_ref[c].astype(F32))
            dq, df, dv, dlb, _ = pull((do_ref[rows, :], gleave[c]))
            if has_add:
                dq = dq + aq_ref[rows, :]
                dv = dv + av_ref[rows, :]
            dq_ref[rows, :] = dq.astype(dq_ref.dtype)
            df_ref[rows, :] = df.astype(BF16)
            dv_ref[rows, :] = dv.astype(dv_ref.dtype)
            dlb_blk = dlb_blk + dlb

        @pl.when(pl.program_id(1) == 0)
        def _():
            dlb_ref[...] = dlb_blk

        @pl.when(pl.program_id(1) > 0)
        def _():
            dlb_ref[...] += dlb_blk

    trow = (lambda t: t) if rev else (lambda t: nt - 1 - t)
    col = lambda off: pl.BlockSpec((ts, HG_DIM), lambda h, t: (trow(t), off + h))
    tile, mats, vecs = _hg_specs(ts, nch, trow)
    in_specs = [col(0), col(fcol), col(3 * HG_HEADS), pl.BlockSpec((None, 1, HG_DIM), lambda h, t: (h, 0, 0)),
                mats, tile, vecs, tile]
    args = [z, z, z, lb, states, qs, dec, dout]
    if has_add:
        in_specs += [tile, tile]
        args += [addq, addv]
    act = lambda dt: jax.ShapeDtypeStruct((s, D_MODEL), dt)
    sums = BF16 if has_add else F32
    return _call(
        body, name=name, grid=(HG_HEADS, nt),
        out_shape=(act(sums), act(BF16), act(sums), jax.ShapeDtypeStruct((HG_HEADS, 1, HG_DIM), F32)),
        in_specs=in_specs,
        out_specs=(tile, tile, tile, pl.BlockSpec((None, 1, HG_DIM), lambda h, t: (h, 0, 0))), args=tuple(args),
        scratch_shapes=[pltpu.VMEM((HG_DIM, HG_DIM), F32)], sem=("parallel", "arbitrary"), comm=comm,
        edge=lambda: _hg_edge(nt))


def _hg_post(of, ob, z, norm_g, *, tm, name):
    s = of.shape[0]

    def body(of_ref, ob_ref, gate_ref, ng_ref, y_ref):
        gn = ng_ref[...]
        for h in range(HG_HEADS):
            ln = slice(HG_DIM * h, HG_DIM * (h + 1))
            o = of_ref[:, ln] + ob_ref[:, ln]
            r = lax.rsqrt(jnp.mean(o * o, axis=-1, keepdims=True) + LN_EPS)
            gt = gate_ref[:, ln]
            y_ref[:, ln] = (o * r * gn * gt * _sigmoid(gt)).astype(BF16)

    row = lambda i: (i, 0)
    return pl.pallas_call(
        body, name=name, grid=(s // tm,),
        out_shape=jax.ShapeDtypeStruct((s, D_MODEL), BF16),
        in_specs=[pl.BlockSpec((tm, D_MODEL), row), pl.BlockSpec((tm, D_MODEL), row),
                  pl.BlockSpec((tm, D_MODEL), lambda i: (i, 4)), pl.BlockSpec((1, HG_DIM), lambda i: (0, 0))],
        out_specs=pl.BlockSpec((tm, D_MODEL), row),
        compiler_params=_params(("parallel",)),
    )(of, ob, z, norm_g)


def _hg_post_bwd(dy, of, ob, z, norm_g, *, tm, name):
    s = of.shape[0]

    def body(dy_ref, of_ref, ob_ref, gate_ref, ng_ref, do_ref, dgate_ref, dng_ref):
        gn = ng_ref[...]
        tot = jnp.zeros((1, HG_DIM), F32)
        for h in range(HG_HEADS):
            ln = slice(HG_DIM * h, HG_DIM * (h + 1))
            d = dy_ref[:, ln].astype(F32)
            o = of_ref[:, ln] + ob_ref[:, ln]
            r = lax.rsqrt(jnp.mean(o * o, axis=-1, keepdims=True) + LN_EPS)
            ohat = o * r
            gt = gate_ref[:, ln]
            sg = _sigmoid(gt)
            don = d * gt * sg
            dgate_ref[:, ln] = (d * ohat * gn * sg * (1.0 + gt * (1.0 - sg))).astype(BF16)
            tot = tot + jnp.sum(don * ohat, axis=0, keepdims=True)
            dohat = don * gn
            do_ref[:, ln] = r * (dohat - ohat * jnp.mean(dohat * ohat, axis=-1, keepdims=True))

        @pl.when(pl.program_id(0) == 0)
        def _():
            dng_ref[...] = tot

        @pl.when(pl.program_id(0) > 0)
        def _():
            dng_ref[...] += tot

    row = lambda i: (i, 0)
    return pl.pallas_call(
        body, name=name, grid=(s // tm,),
        out_shape=(jax.ShapeDtypeStruct((s, D_MODEL), F32), jax.ShapeDtypeStruct((s, D_MODEL), BF16),
                   jax.ShapeDtypeStruct((1, HG_DIM), F32)),
        in_specs=[pl.BlockSpec((tm, D_MODEL), row), pl.BlockSpec((tm, D_MODEL), row), pl.BlockSpec((tm, D_MODEL), row),
                  pl.BlockSpec((tm, D_MODEL), lambda i: (i, 4)), pl.BlockSpec((1, HG_DIM), lambda i: (0, 0))],
        out_specs=(pl.BlockSpec((tm, D_MODEL), row), pl.BlockSpec((tm, D_MODEL), row),
                   pl.BlockSpec((1, HG_DIM), lambda i: (0, 0))),
        compiler_params=_params(("arbitrary",)),
    )(dy, of, ob, z, norm_g)


def _lb_fwd(logits, *, name):
    w = logits.shape[1]

    def body(l_ref, o_ref):
        lg = l_ref[...]
        e = jnp.exp(lg - jnp.max(lg, axis=0, keepdims=True))
        sm = e / jnp.sum(e, axis=0, keepdims=True)
        o_ref[0:1, :] = sm[1:2]
        o_ref[1:2, :] = sm[1:2] + sm[2:3] + sm[3:4]

    return pl.pallas_call(body, name=name, out_shape=jax.ShapeDtypeStruct((2, w), F32))(logits)


def _lb_bwd(logits, dlb, *, name):
    w = logits.shape[1]

    def body(l_ref, d_ref, o_ref):
        lg = l_ref[...]
        e = jnp.exp(lg - jnp.max(lg, axis=0, keepdims=True))
        sm = e / jnp.sum(e, axis=0, keepdims=True)
        d1, d3 = d_ref[0:1, :], d_ref[1:2, :]
        dot = sm[1:2] * (d1 + d3) + (sm[2:3] + sm[3:4]) * d3
        o_ref[0:1, :] = -sm[0:1] * dot
        o_ref[1:2, :] = sm[1:2] * (d1 + d3 - dot)
        o_ref[2:3, :] = sm[2:3] * (d3 - dot)
        o_ref[3:4, :] = sm[3:4] * (d3 - dot)

    return pl.pallas_call(body, name=name, out_shape=jax.ShapeDtypeStruct((4, w), F32))(logits, dlb)


def _adamw(w, g, m, v, *, tr, g_off=0, name):
    rows = w.shape[0]
    parts = g.ndim == 3
    c1 = 1.0 / (1.0 - ADAM_B1 ** ADAM_STEP)
    c2 = 1.0 / (1.0 - ADAM_B2 ** ADAM_STEP)

    def body(w_ref, g_ref, m_ref, v_ref, go_ref, d_ref, mo_ref, vo_ref):
        if parts:
            gg = g_ref[0].astype(F32)
            for i in range(1, N_DEV):
                gg = gg + g_ref[i].astype(F32)
        else:
            gg = g_ref[...]
        mm = ADAM_B1 * m_ref[...] + (1.0 - ADAM_B1) * gg
        vv = ADAM_B2 * v_ref[...] + (1.0 - ADAM_B2) * (gg * gg)
        go_ref[...] = gg
        mo_ref[...] = mm
        vo_ref[...] = vv
        d_ref[...] = -ADAM_LR * ((mm * c1) / (jnp.sqrt(vv * c2) + ADAM_EPS) + ADAM_WD * w_ref[...])

    tile = pl.BlockSpec((tr, D_MODEL), lambda i: (i, 0))
    gspec = pl.BlockSpec((N_DEV, tr, D_MODEL), lambda i: (0, i + g_off // tr, 0)) if parts else tile
    out = jax.ShapeDtypeStruct((rows, D_MODEL), F32)
    return pl.pallas_call(
        body, name=name, grid=(rows // tr,),
        out_shape=(out, out, out, out),
        in_specs=[tile, gspec, tile, tile], out_specs=(tile, tile, tile, tile),
        compiler_params=_params(("parallel",)),
    )(w, g, m, v)


def _sum8(parts, *, name):
    def body(p_ref, o_ref):
        tot = p_ref[0]
        for i in range(1, N_DEV):
            tot = tot + p_ref[i]
        o_ref[...] = tot

    return pl.pallas_call(body, name=name, out_shape=jax.ShapeDtypeStruct(parts.shape[1:], parts.dtype))(parts)


def _layer_params(i):
    j = i // 2
    mix = [("att_w_qkv", j, 1), ("att_w_o", j, 0)] if i % 2 == 0 else [("hgrn_w_in", j, 1), ("hgrn_w_o", j, 0)]
    return mix + [("ffn_w_in", i, 1), ("ffn_w_out", i, 0), ("ple_w_gate", i, 0), ("ple_w_proj", i, 1)]


def _pack_local(tree, params):
    return jnp.concatenate([tree[n][j].reshape(-1, D_MODEL) for n, j, _ in params], axis=0)


def _unpack_local(packed, params, like):
    out, r = {}, 0
    for n, _, _ in params:
        shp = like[n].shape[1:]
        k = shp[0] * shp[1] // D_MODEL
        out[n] = packed[r:r + k].reshape(shp)
        r += k
    return out


def _unpack_gathered(gathered, i, like):
    out, r = {}, 0
    for n, _, ax in _layer_params(i):
        shp = like[n].shape[1:]
        k = shp[0] * shp[1] // D_MODEL
        t = gathered[:, r:r + k].reshape((N_DEV,) + shp)
        out[n] = (jnp.moveaxis(t, 0, 1).reshape(shp[0], N_DEV * shp[1]) if ax == 1
                  else t.reshape(N_DEV * shp[0], shp[1]))
        r += k
    return out


def _pack_full(grads, params, like):
    cols = []
    for n, _, ax in params:
        shp = like[n].shape[1:]
        t = (jnp.moveaxis(grads[n].reshape(shp[0], N_DEV, shp[1]), 1, 0) if ax == 1
             else grads[n].reshape(N_DEV, shp[0], shp[1]))
        cols.append(t.reshape(N_DEV, -1, D_MODEL).astype(BF16))
    return jnp.concatenate(cols, axis=1)


def _row_tile(rows):
    return max(t for t in range(16, 257, 16) if rows % t == 0)


SMALL_ROWS = 24


def _pad_row(a):
    flat = a.reshape(1, -1)
    return jnp.pad(flat, ((0, 0), (0, D_MODEL - flat.shape[1])))


def _tile(n, pref):
    return min(n, pref)


def kernel(x, p, att_w_qkv, att_sink, att_w_o, hgrn_w_in, hgrn_lb_logits, hgrn_norm_g, hgrn_w_o, ln_mix_g, ln_mix_b, ffn_w_in, ffn_w_out, ln_ffn_g, ln_ffn_b, ple_w_gate, ple_w_proj, loss_target, m_att_w_qkv, m_att_sink, m_att_w_o, m_hgrn_w_in, m_hgrn_lb_logits, m_hgrn_norm_g, m_hgrn_w_o, m_ln_mix_g, m_ln_mix_b, m_ffn_w_in, m_ffn_w_out, m_ln_ffn_g, m_ln_ffn_b, m_ple_w_gate, m_ple_w_proj, v_att_w_qkv, v_att_sink, v_att_w_o, v_hgrn_w_in, v_hgrn_lb_logits, v_hgrn_norm_g, v_hgrn_w_o, v_ln_mix_g, v_ln_mix_b, v_ffn_w_in, v_ffn_w_out, v_ln_ffn_g, v_ln_ffn_b, v_ple_w_gate, v_ple_w_proj):
    names = ["att_w_qkv", "att_sink", "att_w_o", "hgrn_w_in", "hgrn_lb_logits", "hgrn_norm_g", "hgrn_w_o", "ln_mix_g",
             "ln_mix_b", "ffn_w_in", "ffn_w_out", "ln_ffn_g", "ln_ffn_b", "ple_w_gate", "ple_w_proj"]
    w = dict(zip(names, (att_w_qkv, att_sink, att_w_o, hgrn_w_in, hgrn_lb_logits, hgrn_norm_g, hgrn_w_o, ln_mix_g,
                         ln_mix_b, ffn_w_in, ffn_w_out, ln_ffn_g, ln_ffn_b, ple_w_gate, ple_w_proj)))
    mom = dict(zip(names, (m_att_w_qkv, m_att_sink, m_att_w_o, m_hgrn_w_in, m_hgrn_lb_logits, m_hgrn_norm_g, m_hgrn_w_o,
                           m_ln_mix_g, m_ln_mix_b, m_ffn_w_in, m_ffn_w_out, m_ln_ffn_g, m_ln_ffn_b, m_ple_w_gate,
                           m_ple_w_proj)))
    var = dict(zip(names, (v_att_w_qkv, v_att_sink, v_att_w_o, v_hgrn_w_in, v_hgrn_lb_logits, v_hgrn_norm_g, v_hgrn_w_o,
                           v_ln_mix_g, v_ln_mix_b, v_ffn_w_in, v_ffn_w_out, v_ln_ffn_g, v_ln_ffn_b, v_ple_w_gate,
                           v_ple_w_proj)))
    s = x.shape[1]
    me = 4 * lax.axis_index("x") + 2 * lax.axis_index("y") + lax.axis_index("c")
    tm = _tile(s, 512)
    tbig = _tile(s, 1024)
    ts = _tile(s // 2, 1024)
    x0 = x.reshape(s, D_MODEL)
    target = loss_target.reshape(s, D_MODEL)
    pl_in = p.reshape(DEPTH, s, PLE_DIM)

    w_rows = [_pack_local(w, _layer_params(i)).astype(BF16) for i in range(DEPTH)]
    full = _unpack_gathered(_gather(w_rows[0], name="gather_weights"), 0, w)
    lb_rows = jnp.pad(hgrn_lb_logits.reshape(8, HG_DIM), ((0, 0), (0, D_MODEL - HG_DIM)))
    lb_all = _gather(lb_rows, name="gather_lb")[:, :, :HG_DIM]
    logits_full = jnp.moveaxis(lb_all, 0, 1).reshape(DEPTH, 2 * D_MODEL)
    lb = _lb_fwd(logits_full, name="lb_fwd")
    cos, sin = _rope_tables(s)

    saved = []
    xf, xb = x0, x0
    for i in range(DEPTH):
        j = i // 2
        sv = {"x": xf, "xb": xb, "w": full}
        nxt = (w_rows[i + 1], True) if i + 1 < DEPTH else None
        if i % 2 == 0:
            z = _mm(xb, full["att_w_qkv"], tm=tbig, tn=512, tk=D_MODEL, name="att_in")
            o, *more = _att_fwd(z, w["att_sink"][j], cos, sin, comm=nxt, name="att_fwd")
            w_o = full["att_w_o"]
        else:
            z = _mm(xb, full["hgrn_w_in"], tm=tbig, tn=1024, tk=D_MODEL, name="hgrn_in")
            lbl = lb[j].reshape(2, HG_HEADS, 1, HG_DIM)
            of, st_f, qs_f, dec_f, *more = _hg_fwd(z, lbl[0], rev=False, ts=ts, comm=nxt, name="hgrn_fwd")
            ob, st_b, qs_b, dec_b = _hg_fwd(z, lbl[1], rev=True, ts=ts, name="hgrn_fwd_rev")
            o = _hg_post(of, ob, z, w["hgrn_norm_g"][j].reshape(1, HG_DIM), tm=tm, name="hgrn_post")
            w_o = full["hgrn_w_o"]
            sv.update(of=of, ob=ob, st_f=st_f, st_b=st_b, lbl=lbl, qs_f=qs_f, qs_b=qs_b, dec_f=dec_f, dec_b=dec_b)
        sv.update(z=z, o=o)
        pre1, x1, x1b = _proj_ln(o, w_o, xf, w["ln_mix_g"][i:i + 1], w["ln_mix_b"][i:i + 1], tm=tm, name="mix_out_ln")
        gg, uu, act = _ffn_in(x1b, full["ffn_w_in"], tm=tm, tn=FF_TILE, name="ffn_in")
        pre2, x2b, xf, xb = _ffn_out_ple(act, full["ffn_w_out"], x1, w["ln_ffn_g"][i:i + 1], w["ln_ffn_b"][i:i + 1],
                                         pl_in[i], full["ple_w_gate"], full["ple_w_proj"], tm=tm, name="ffn_out_ple")
        sv.update(pre1=pre1, x1=x1, x1b=x1b, g=gg, u=uu, act=act, pre2=pre2, x2b=x2b)
        saved.append(sv)
        if nxt is not None:
            full = _unpack_gathered(more[0], i + 1, w)

    dx, loss_blk = _loss_head(xf, target, tm=tm, name="loss_head")
    loss = lax.psum(loss_blk[0, 0], AXES)

    small = {n: [None] * DEPTH for n in ("ln_mix_g", "ln_mix_b", "ln_ffn_g", "ln_ffn_b")}
    dlb_rows = [None] * 4
    dnorm, dsink = [None] * 2, [None] * 2
    recv_late, recv_early = [None] * DEPTH, [None] * DEPTH
    above = None
    mmw = functools.partial(_mm, ta=True, tk=_tile(s, 2048), out_dtype=BF16)
    for i in reversed(range(DEPTH)):
        j = i // 2
        sv = saved[i]
        full, gl = sv["w"], {}
        da, dpp, dy2, dy2b, small["ln_ffn_g"][i], small["ln_ffn_b"][i] = _ple_ln_bwd(
            dx, sv["x2b"], pl_in[i], full["ple_w_gate"], full["ple_w_proj"], sv["pre2"], w["ln_ffn_g"][i:i + 1],
            tm=tm, name="ple_ln_bwd")
        gl["ple_w_gate"] = mmw(sv["x2b"], da, tm=D_MODEL, tn=D_MODEL, name="dw_ple_gate")
        gl["ple_w_proj"] = mmw(pl_in[i], dpp, tm=PLE_DIM, tn=D_MODEL, name="dw_ple_proj")
        dg, du = _ffn_bwd_act(dy2b, full["ffn_w_out"], sv["g"], sv["u"], tm=_tile(s, SUB_ROWS), tn=D_FF,
                              name="ffn_bwd_act")
        gl["ffn_w_out"] = mmw(sv["act"], dy2b, tm=FF_TILE, tn=D_MODEL, name="dw_ffn_out")
        dy1, dy1b, small["ln_mix_g"][i], small["ln_mix_b"][i] = _dx_from_pieces(
            [dg, du], full["ffn_w_in"], dy2, tm=_tile(s, SUB_ROWS), ln=(sv["pre1"], w["ln_mix_g"][i:i + 1]),
            name="ffn_bwd_x_ln")
        gl["ffn_w_in"] = mmw(sv["x1b"], [dg, du], tm=D_MODEL, tn=FF_TILE, name="dw_ffn_in")
        n_out, n_inw = ("att_w_o", "att_w_qkv") if i % 2 == 0 else ("hgrn_w_o", "hgrn_w_in")
        do = _mm(dy1b, full[n_out], tm=tbig, tn=D_MODEL, tk=D_MODEL, tb=True, out_dtype=BF16, name="mix_out_bwd")
        gl[n_out] = mmw(sv["o"], dy1b, tm=D_MODEL, tn=D_MODEL, name="dw_mix_out")
        early = _pack_full(gl, _layer_params(i)[1:], w)
        comm = (early if above is None else jnp.concatenate([above, early], axis=1), False)
        if i % 2 == 0:
            dzq, part, dsk, *more = _att_bwd(sv["z"], do, w["att_sink"][j], cos, sin, comm=comm, name="att_bwd")
            dz = [dzq, _att_bwd_kv(part, cos, sin, name="att_bwd_kv")]
            dsink[j] = dsk[:, 0]
        else:
            dsum, dgate, dnorm[j] = _hg_post_bwd(do, sv["of"], sv["ob"], sv["z"], w["hgrn_norm_g"][j].reshape(1, HG_DIM),
                                                 tm=tm, name="hgrn_post_bwd")
            dq1, df1, dv1, dlb1, *more = _hg_bwd(sv["z"], sv["lbl"][0], sv["st_f"], sv["qs_f"], sv["dec_f"], dsum, None,
                                                 None, rev=False, ts=ts, comm=comm, name="hgrn_bwd")
            dq2, df2, dv2, dlb2 = _hg_bwd(sv["z"], sv["lbl"][1], sv["st_b"], sv["qs_b"], sv["dec_b"], dsum, dq1, dv1,
                                          rev=True, ts=ts, name="hgrn_bwd_rev")
            dz = [dq2, df1, df2, dv2, dgate]
            dlb_rows[2 * j] = dlb1.reshape(1, D_MODEL)
            dlb_rows[2 * j + 1] = dlb2.reshape(1, D_MODEL)
        if above is not None:
            recv_late[i + 1] = (more[0], 0)
        recv_early[i] = (more[0], 0 if above is None else above.shape[1])
        dx = _dx_from_pieces(dz, full[n_inw], dy1, tm=tm, name="mix_in_bwd")
        gl[n_inw] = mmw(sv["xb"], dz, tm=D_MODEL, tn=512, name="dw_mix_in")
        above = _pack_full(gl, _layer_params(i)[:1], w)
    grad_x = dx.reshape(x.shape)
    recv_late[0] = (_exchange(above, name="exchange_grads"), 0)

    big_out = [{n: [None] * w[n].shape[0] for n, _ in BIG} for _ in range(4)]
    for i in range(DEPTH):
        for params, (got, off) in ((_layer_params(i)[:1], recv_late[i]), (_layer_params(i)[1:], recv_early[i])):
            w_part = _pack_local(w, params)
            outs = _adamw(w_part, got, _pack_local(mom, params), _pack_local(var, params),
                          tr=_row_tile(math.gcd(w_part.shape[0], off)), g_off=off, name="adamw_big")
            for kind, packed in enumerate(outs):
                for (n, j, _), piece in zip(params, _unpack_local(packed, params, w).values()):
                    big_out[kind][n][j] = piece
    big_out = [{n: jnp.stack(v) for n, v in kind.items()} for kind in big_out]

    small_rows = jnp.concatenate(
        [jnp.concatenate(small[n], axis=0) for n in ("ln_mix_g", "ln_mix_b", "ln_ffn_g", "ln_ffn_b")] + dlb_rows
        + [_pad_row(jnp.stack(dnorm)), _pad_row(jnp.stack(dsink)), jnp.zeros((2, D_MODEL), F32)], axis=0)
    small_all = _gather(small_rows, name="gather_small")
    lbw, lbm, lbv = (t.reshape(4, 2 * HG_DIM) for t in (hgrn_lb_logits, mom["hgrn_lb_logits"], var["hgrn_lb_logits"]))
    summed = _sum8(small_all, name="sum_small")
    dlb_mine = lax.dynamic_slice_in_dim(summed[16:20].reshape(2, 2, HG_HEADS, HG_DIM), me, 1, axis=2)
    dlogits = _lb_bwd(lbw, dlb_mine.reshape(2, 2 * HG_DIM), name="lb_bwd")

    def small_pack(ln4, lbt, ng, sk):
        return jnp.concatenate([ln4[n] for n in ("ln_mix_g", "ln_mix_b", "ln_ffn_g", "ln_ffn_b")]
                               + [_pad_row(lbt), _pad_row(ng), _pad_row(sk), jnp.zeros((5, D_MODEL), F32)], axis=0)

    g_small = jnp.concatenate([summed[:16], _pad_row(dlogits), summed[20:22], jnp.zeros((5, D_MODEL), F32)], axis=0)
    souts = _adamw(small_pack(w, lbw, w["hgrn_norm_g"], w["att_sink"]), g_small,
                   small_pack(mom, lbm, mom["hgrn_norm_g"], mom["att_sink"]),
                   small_pack(var, lbv, var["hgrn_norm_g"], var["att_sink"]), tr=SMALL_ROWS, name="adamw_small")

    def small_unpack(t):
        out = {n: t[4 * k:4 * k + 4] for k, n in enumerate(("ln_mix_g", "ln_mix_b", "ln_ffn_g", "ln_ffn_b"))}
        out["hgrn_lb_logits"] = t[16].reshape(hgrn_lb_logits.shape)
        out["hgrn_norm_g"] = t[17, :2 * HG_DIM].reshape(hgrn_norm_g.shape)
        out["att_sink"] = t[18, :2 * N_Q_HEADS].reshape(att_sink.shape)
        return out

    result = [loss, grad_x]
    for big_t, small_t in zip(big_out, souts):
        merged = dict(big_t)
        merged.update(small_unpack(small_t))
        result += [merged[n] for n in names]
    return tuple(result)
```

```python
import functools
import math

import jax
import jax.numpy as jnp
from jax import lax
from jax.experimental import pallas as pl
from jax.experimental.pallas import tpu as pltpu

F32 = jnp.float32
BF16 = jnp.bfloat16

D_MODEL = 1024
DEPTH = 4
HEAD_DIM = 64
N_Q_HEADS = 16
N_KV_HEADS = 4
GROUP = 4
KV_DIM = 256
ATT_BLOCK = 128
ROPE_DIM = 16
ROPE_THETA = 500000.0
HG_HEADS = 8
HG_DIM = 128
HG_CHUNK = 64
HG_SUB = 16
D_FF = 2816
FF_TILE = 1408
SUB_ROWS = 256
PLE_DIM = 256
ALPHA = (2 * DEPTH) ** 0.25
LN_EPS = 1e-5
ADAM_LR, ADAM_B1, ADAM_B2, ADAM_EPS, ADAM_WD, ADAM_STEP = 0.001, 0.9, 0.999, 1e-08, 0.01, 10

N_DEV = 8
LANES = 128
VMEM_LIMIT = 52 * 1024 * 1024
NEG = -1e30
MESH = pl.DeviceIdType.MESH
AXES = ("x", "y", "c")

BIG = (("att_w_qkv", 2), ("att_w_o", 1), ("hgrn_w_in", 2), ("hgrn_w_o", 1), ("ffn_w_in", 2), ("ffn_w_out", 1),
       ("ple_w_gate", 1), ("ple_w_proj", 2))


def _params(sem=None, vmem=VMEM_LIMIT):
    return pltpu.CompilerParams(dimension_semantics=sem, vmem_limit_bytes=vmem)


def _sigmoid(x):
    return jax.nn.sigmoid(x)


def _direct_copies(src_ref, out_ref, send_sems, recv_sems, local_sem, gather, arrivals):
    x, y, c = lax.axis_index("x"), lax.axis_index("y"), lax.axis_index("c")
    me = 4 * x + 2 * y + c
    mine = (lambda j: src_ref) if gather else (lambda j: src_ref.at[j])
    pairs = []
    for k in range(1, N_DEV):
        px, py, pc = x ^ (k >> 2), y ^ ((k >> 1) & 1), c ^ (k & 1)
        peer = 4 * px + 2 * py + pc
        send = pltpu.make_async_remote_copy(
            src_ref=mine(peer), dst_ref=out_ref.at[me], send_sem=send_sems.at[k], recv_sem=recv_sems.at[k],
            device_id=(px, py, pc), device_id_type=MESH)
        arrival = pltpu.make_async_remote_copy(
            src_ref=mine(peer), dst_ref=out_ref.at[peer], send_sem=send_sems.at[k], recv_sem=recv_sems.at[k],
            device_id=(x, y, c), device_id_type=MESH) if arrivals else None
        pairs.append((send, arrival))
    return pltpu.make_async_copy(mine(me), out_ref.at[me], local_sem), pairs


def _direct_start(*refs, gather):
    local, pairs = _direct_copies(*refs, gather, False)
    local.start()
    for send, _ in pairs:
        send.start()


def _direct_wait(*refs, gather):
    local, pairs = _direct_copies(*refs, gather, True)
    for send, arrival in pairs:
        send.wait_send()
        arrival.wait_recv()
    local.wait()


COMM_SCRATCH = [pltpu.SemaphoreType.DMA((N_DEV,)), pltpu.SemaphoreType.DMA((N_DEV,)), pltpu.SemaphoreType.DMA]


def _exchange(src, *, gather=False, name):
    def body(*refs):
        _direct_start(*refs, gather=gather)
        _direct_wait(*refs, gather=gather)

    blk = tuple(src.shape) if gather else tuple(src.shape[1:])
    return pl.pallas_call(
        body, name=name,
        out_shape=jax.ShapeDtypeStruct((N_DEV,) + blk, src.dtype),
        in_specs=[pl.BlockSpec(memory_space=pltpu.HBM)],
        out_specs=pl.BlockSpec(memory_space=pltpu.HBM),
        scratch_shapes=COMM_SCRATCH,
    )(src)


def _call(body, *, name, grid, out_shape, in_specs, out_specs, args, scratch_shapes=(), sem, comm=None, edge=None):
    out_shape, out_specs = tuple(out_shape), tuple(out_specs)
    if comm is None:
        return pl.pallas_call(body, name=name, grid=grid, out_shape=out_shape, in_specs=list(in_specs),
                              out_specs=out_specs, scratch_shapes=list(scratch_shapes),
                              compiler_params=_params(sem))(*args)
    src, gather = comm
    n_in, n_out, n_scr = len(args), len(out_shape), len(scratch_shapes)
    blk = tuple(src.shape) if gather else tuple(src.shape[1:])
    hbm = pl.BlockSpec(memory_space=pltpu.HBM)

    def carrying(*refs):
        ins, src_ref = refs[:n_in], refs[n_in]
        outs, dst_ref = refs[n_in + 1:n_in + 1 + n_out], refs[n_in + 1 + n_out]
        own = refs[n_in + 2 + n_out:n_in + 2 + n_out + n_scr]
        comm_refs = (src_ref, dst_ref) + tuple(refs[n_in + 2 + n_out + n_scr:])
        first, last = edge()

        @pl.when(first)
        def _():
            _direct_start(*comm_refs, gather=gather)

        body(*ins, *outs, *own)

        @pl.when(last)
        def _():
            _direct_wait(*comm_refs, gather=gather)

    return pl.pallas_call(
        carrying, name=name, grid=grid,
        out_shape=out_shape + (jax.ShapeDtypeStruct((N_DEV,) + blk, src.dtype),),
        in_specs=list(in_specs) + [hbm], out_specs=out_specs + (hbm,),
        scratch_shapes=list(scratch_shapes) + COMM_SCRATCH,
        compiler_params=_params(("arbitrary",) * len(grid)),
    )(*args, src)


def _gather(src, *, name):
    def body(src_ref, out_ref, send_sems, recv_sems, local_sem):
        x, y, c = lax.axis_index("x"), lax.axis_index("y"), lax.axis_index("c")
        sibling = (x, y, 1 - c)
        chips = [(1 - x, y), (x, 1 - y), (1 - x, 1 - y)]

        def rows(px, py, pc):
            return out_ref.at[4 * px + 2 * py + pc]

        def copy(k, block, to, from_src=False):
            return pltpu.make_async_remote_copy(
                src_ref=src_ref if from_src else rows(*block), dst_ref=rows(*block), send_sem=send_sems.at[k],
                recv_sem=recv_sems.at[k], device_id=to, device_id_type=MESH)

        me = (x, y, c)
        mine = pltpu.make_async_copy(src_ref, rows(*me), local_sem)
        mine.start()
        first = [copy(0, me, sibling, from_src=True)]
        first += [copy(1 + j, me, (*chip, c), from_src=True) for j, chip in enumerate(chips)]
        for cp in first:
            cp.start()
        passed = [copy(4 + j, (*chip, c), sibling) for j, chip in enumerate(chips)]
        for j, chip in enumerate(chips):
            copy(1 + j, (*chip, c), me).wait_recv()
            passed[j].start()
        copy(0, sibling, me).wait_recv()
        for j, chip in enumerate(chips):
            copy(4 + j, (*chip, 1 - c), me).wait_recv()
        for cp in first + passed:
            cp.wait_send()
        mine.wait()

    return pl.pallas_call(
        body, name=name,
        out_shape=jax.ShapeDtypeStruct((N_DEV,) + tuple(src.shape), src.dtype),
        in_specs=[pl.BlockSpec(memory_space=pltpu.HBM)],
        out_specs=pl.BlockSpec(memory_space=pltpu.HBM),
        scratch_shapes=[pltpu.SemaphoreType.DMA((7,)), pltpu.SemaphoreType.DMA((7,)), pltpu.SemaphoreType.DMA],
    )(src)


def _mm(a, b, *, tm, tn, tk, ta=False, tb=False, out_dtype=F32, name):
    b_list = list(b) if isinstance(b, (list, tuple)) else [b]
    assert not (tb and len(b_list) > 1)
    m, kdim = (a.shape[1], a.shape[0]) if ta else a.shape
    joff = [0]
    for piece in b_list:
        joff.append(joff[-1] + (piece.shape[0] if tb else piece.shape[1]) // tn)
    nk, n = kdim // tk, joff[-1] * tn
    dims = (((0 if ta else 1,), (1 if tb else 0,)), ((), ()))

    def mine(j, p):
        return (j >= joff[p]) & (j < joff[p + 1])

    def body(*refs):
        a_ref, b_refs, o_ref = refs[0], refs[1:1 + len(b_list)], refs[1 + len(b_list)]
        acc_ref = refs[-1] if nk > 1 else None
        j, k = pl.program_id(1), pl.program_id(2)
        for p, b_ref in enumerate(b_refs):
            def step(b_ref=b_ref):
                part = lax.dot_general(a_ref[...].astype(BF16), b_ref[...].astype(BF16), dims,
                                       preferred_element_type=F32)
                if nk == 1:
                    o_ref[...] = part.astype(out_dtype)
                else:
                    _accumulate(acc_ref, part, k == 0)

            if len(b_list) == 1:
                step()
            else:
                pl.when(mine(j, p))(step)
        if nk > 1:
            @pl.when(k == nk - 1)
            def _():
                o_ref[...] = acc_ref[...].astype(out_dtype)

    def b_spec(p):
        jj = lambda j: jnp.clip(j - joff[p], 0, joff[p + 1] - joff[p] - 1)
        kk = (lambda j, k: k) if len(b_list) == 1 else (lambda j, k: jnp.where(mine(j, p), k, 0))
        return (pl.BlockSpec((tn, tk), lambda i, j, k: (jj(j), kk(j, k))) if tb
                else pl.BlockSpec((tk, tn), lambda i, j, k: (kk(j, k), jj(j))))

    a_spec = pl.BlockSpec((tk, tm), lambda i, j, k: (k, i)) if ta else pl.BlockSpec((tm, tk), lambda i, j, k: (i, k))
    return pl.pallas_call(
        body, name=name, grid=(m // tm, n // tn, nk),
        out_shape=jax.ShapeDtypeStruct((m, n), out_dtype),
        in_specs=[a_spec] + [b_spec(p) for p in range(len(b_list))],
        out_specs=pl.BlockSpec((tm, tn), lambda i, j, k: (i, j)),
        scratch_shapes=[pltpu.VMEM((tm, tn), F32)] if nk > 1 else [],
        compiler_params=_params(("parallel", "parallel", "arbitrary")),
    )(a, *b_list)


def _dx_from_pieces(pieces, w, add, *, tm, ln=None, name):
    s = pieces[0].shape[0]
    widths = [p.shape[1] for p in pieces]

    def body(*refs):
        p_refs, (w_ref, add_ref) = refs[:len(pieces)], refs[len(pieces):len(pieces) + 2]
        rest = refs[len(pieces) + 2:]
        pg = jnp.zeros((1, D_MODEL), F32)
        pb = jnp.zeros((1, D_MODEL), F32)
        for rs in _row_parts(tm):
            r = ALPHA * add_ref[rs, :]
            off = 0
            for p_ref, width in zip(p_refs, widths):
                r = r + lax.dot_general(p_ref[rs, :], w_ref[:, off:off + width], (((1,), (1,)), ((), ())),
                                        preferred_element_type=F32)
                off += width
            if ln is None:
                rest[0][rs, :] = r
            else:
                dy, qg, qb = _ln_bwd_rows(r, rest[0][rs, :], rest[1][...])
                rest[2][rs, :] = dy
                rest[3][rs, :] = dy.astype(BF16)
                pg, pb = pg + qg, pb + qb
        if ln is not None:
            _accumulate(rest[4], pg, pl.program_id(0) == 0)
            _accumulate(rest[5], pb, pl.program_id(0) == 0)

    row = lambda i: (i, 0)
    tile = pl.BlockSpec((tm, D_MODEL), row)
    vec = pl.BlockSpec((1, D_MODEL), lambda i: (0, 0))
    in_specs = ([pl.BlockSpec((tm, width), row) for width in widths]
                + [pl.BlockSpec((D_MODEL, sum(widths)), lambda i: (0, 0)), tile])
    args = list(pieces) + [w, add]
    out_shape, out_specs = jax.ShapeDtypeStruct((s, D_MODEL), F32), tile
    if ln is not None:
        in_specs += [tile, vec]
        args += list(ln)
        out_shape = (jax.ShapeDtypeStruct((s, D_MODEL), F32), jax.ShapeDtypeStruct((s, D_MODEL), BF16),
                     jax.ShapeDtypeStruct((1, D_MODEL), F32), jax.ShapeDtypeStruct((1, D_MODEL), F32))
        out_specs = (tile, tile, vec, vec)
    return pl.pallas_call(
        body, name=name, grid=(s // tm,), out_shape=out_shape, in_specs=in_specs, out_specs=out_specs,
        compiler_params=_params(("arbitrary",) if ln is not None else ("parallel",)),
    )(*args)


def _ln_bwd_rows(do, y, g):
    mu = jnp.mean(y, axis=-1, keepdims=True)
    yc = y - mu
    var = jnp.mean(yc * yc, axis=-1, keepdims=True)
    rstd = lax.rsqrt(var + LN_EPS)
    xhat = yc * rstd
    dxhat = do * g
    dy = rstd * (dxhat - jnp.mean(dxhat, axis=-1, keepdims=True) - xhat * jnp.mean(dxhat * xhat, axis=-1, keepdims=True))
    return dy, jnp.sum(do * xhat, axis=0, keepdims=True), jnp.sum(do, axis=0, keepdims=True)


def _accumulate(ref, val, first):
    @pl.when(first)
    def _():
        ref[...] = val

    @pl.when(jnp.logical_not(first))
    def _():
        ref[...] += val


def _layer_norm_rows(y, g, b):
    mu = jnp.mean(y, axis=-1, keepdims=True)
    yc = y - mu
    var = jnp.mean(yc * yc, axis=-1, keepdims=True)
    return yc * lax.rsqrt(var + LN_EPS) * g + b


def _proj_ln(a, w, res, g, b, *, tm, name):
    s, kdim = a.shape

    def body(a_ref, w_ref, res_ref, g_ref, b_ref, pre_ref, o_ref, obf_ref):
        for rs in _row_parts(tm):
            h = jnp.dot(a_ref[rs, :], w_ref[...], preferred_element_type=F32)
            pre = ALPHA * res_ref[rs, :] + h
            out = _layer_norm_rows(pre, g_ref[...], b_ref[...])
            pre_ref[rs, :] = pre
            o_ref[rs, :] = out
            obf_ref[rs, :] = out.astype(BF16)

    row = lambda i: (i, 0)
    fix = lambda i: (0, 0)
    return pl.pallas_call(
        body, name=name, grid=(s // tm,),
        out_shape=(jax.ShapeDtypeStruct((s, D_MODEL), F32), jax.ShapeDtypeStruct((s, D_MODEL), F32),
                   jax.ShapeDtypeStruct((s, D_MODEL), BF16)),
        in_specs=[pl.BlockSpec((tm, kdim), row), pl.BlockSpec((kdim, D_MODEL), fix), pl.BlockSpec((tm, D_MODEL), row),
                  pl.BlockSpec((1, D_MODEL), fix), pl.BlockSpec((1, D_MODEL), fix)],
        out_specs=(pl.BlockSpec((tm, D_MODEL), row),) * 3,
        compiler_params=_params(("parallel",)),
    )(a, w, res, g, b)


def _row_parts(tm):
    sub = min(tm, SUB_ROWS)
    return [pl.ds(r * sub, sub) for r in range(tm // sub)]


def _ffn_in(xbf, w, *, tm, tn, name):
    s = xbf.shape[0]
    nj = D_FF // tn

    def body(x_ref, wg_ref, wu_ref, g_ref, u_ref, act_ref):
        for rs in _row_parts(tm):
            xv = x_ref[rs, :]
            gg = jnp.dot(xv, wg_ref[...], preferred_element_type=F32)
            uu = jnp.dot(xv, wu_ref[...], preferred_element_type=F32)
            g_ref[rs, :] = gg.astype(BF16)
            u_ref[rs, :] = uu.astype(BF16)
            act_ref[rs, :] = (gg * _sigmoid(gg) * uu).astype(BF16)

    out = jax.ShapeDtypeStruct((s, D_FF), BF16)
    tile = pl.BlockSpec((tm, tn), lambda j, i: (i, j))
    return pl.pallas_call(
        body, name=name, grid=(nj, s // tm),
        out_shape=(out, out, out),
        in_specs=[pl.BlockSpec((tm, D_MODEL), lambda j, i: (i, 0)), pl.BlockSpec((D_MODEL, tn), lambda j, i: (0, j)),
                  pl.BlockSpec((D_MODEL, tn), lambda j, i: (0, j + nj))],
        out_specs=(tile, tile, tile),
        compiler_params=_params(("parallel", "parallel")),
    )(xbf, w, w)


def _ffn_bwd_act(dybf, w_out, g, u, *, tm, tn, name):
    s = dybf.shape[0]

    def body(dy_ref, w_ref, g_ref, u_ref, dg_ref, du_ref):
        for rs in _row_parts(tm):
            dact = lax.dot_general(dy_ref[rs, :], w_ref[...], (((1,), (1,)), ((), ())), preferred_element_type=F32)
            gg = g_ref[rs, :].astype(F32)
            uu = u_ref[rs, :].astype(F32)
            sg = _sigmoid(gg)
            dg_ref[rs, :] = (dact * uu * sg * (1.0 + gg * (1.0 - sg))).astype(BF16)
            du_ref[rs, :] = (dact * gg * sg).astype(BF16)

    out = jax.ShapeDtypeStruct((s, D_FF), BF16)
    tile = pl.BlockSpec((tm, tn), lambda j, i: (i, j))
    return pl.pallas_call(
        body, name=name, grid=(D_FF // tn, s // tm),
        out_shape=(out, out),
        in_specs=[pl.BlockSpec((tm, D_MODEL), lambda j, i: (i, 0)), pl.BlockSpec((tn, D_MODEL), lambda j, i: (j, 0)),
                  tile, tile],
        out_specs=(tile, tile),
        compiler_params=_params(("parallel", "parallel")),
    )(dybf, w_out, g, u)


def _ffn_out_ple(act, w_out, res, g, b, p, w_gate, w_proj, *, tm, name):
    s = act.shape[0]

    def body(a_ref, w_ref, res_ref, g_ref, b_ref, p_ref, wg_ref, wp_ref, pre_ref, x2bf_ref, o_ref, obf_ref):
        for rs in _row_parts(tm):
            pre = ALPHA * res_ref[rs, :] + jnp.dot(a_ref[rs, :], w_ref[...], preferred_element_type=F32)
            x2 = _layer_norm_rows(pre, g_ref[...], b_ref[...])
            x2bf = x2.astype(BF16)
            pre_ref[rs, :] = pre
            x2bf_ref[rs, :] = x2bf
            gate = jnp.dot(x2bf, wg_ref[...], preferred_element_type=F32)
            pp = jnp.dot(p_ref[rs, :].astype(BF16), wp_ref[...], preferred_element_type=F32)
            out = x2 + _sigmoid(gate) * pp
            o_ref[rs, :] = out
            obf_ref[rs, :] = out.astype(BF16)

    row = lambda i: (i, 0)
    fix = lambda i: (0, 0)
    tile = pl.BlockSpec((tm, D_MODEL), row)
    vec = pl.BlockSpec((1, D_MODEL), fix)
    act_t = lambda dt: jax.ShapeDtypeStruct((s, D_MODEL), dt)
    return pl.pallas_call(
        body, name=name, grid=(s // tm,),
        out_shape=(act_t(F32), act_t(BF16), act_t(F32), act_t(BF16)),
        in_specs=[pl.BlockSpec((tm, D_FF), row), pl.BlockSpec((D_FF, D_MODEL), fix), tile, vec, vec,
                  pl.BlockSpec((tm, PLE_DIM), row), pl.BlockSpec((D_MODEL, D_MODEL), fix),
                  pl.BlockSpec((PLE_DIM, D_MODEL), fix)],
        out_specs=(tile, tile, tile, tile),
        compiler_params=_params(("parallel",)),
    )(act, w_out, res, g, b, p, w_gate, w_proj)


def _ple_ln_bwd(dx3, x2bf, p, w_gate, w_proj, pre, g, *, tm, name):
    s = dx3.shape[0]

    def body(d_ref, xbf_ref, p_ref, wg_ref, wp_ref, pre_ref, g_ref, da_ref, dpp_ref, dy_ref, dybf_ref, dg_ref, db_ref):
        pg = jnp.zeros((1, D_MODEL), F32)
        pb = jnp.zeros((1, D_MODEL), F32)
        for rs in _row_parts(tm):
            d = d_ref[rs, :]
            a = jnp.dot(xbf_ref[rs, :], wg_ref[...], preferred_element_type=F32)
            pp = jnp.dot(p_ref[rs, :].astype(BF16), wp_ref[...], preferred_element_type=F32)
            sg = _sigmoid(a)
            da = (d * pp * sg * (1.0 - sg)).astype(BF16)
            da_ref[rs, :] = da
            dpp_ref[rs, :] = (d * sg).astype(BF16)
            dx2 = d + lax.dot_general(da, wg_ref[...], (((1,), (1,)), ((), ())), preferred_element_type=F32)
            dy, qg, qb = _ln_bwd_rows(dx2, pre_ref[rs, :], g_ref[...])
            dy_ref[rs, :] = dy
            dybf_ref[rs, :] = dy.astype(BF16)
            pg, pb = pg + qg, pb + qb
        _accumulate(dg_ref, pg, pl.program_id(0) == 0)
        _accumulate(db_ref, pb, pl.program_id(0) == 0)

    row = lambda i: (i, 0)
    fix = lambda i: (0, 0)
    tile = pl.BlockSpec((tm, D_MODEL), row)
    vec = pl.BlockSpec((1, D_MODEL), fix)
    act = lambda dt: jax.ShapeDtypeStruct((s, D_MODEL), dt)
    return pl.pallas_call(
        body, name=name, grid=(s // tm,),
        out_shape=(act(BF16), act(BF16), act(F32), act(BF16), jax.ShapeDtypeStruct((1, D_MODEL), F32),
                   jax.ShapeDtypeStruct((1, D_MODEL), F32)),
        in_specs=[tile, tile, pl.BlockSpec((tm, PLE_DIM), row), pl.BlockSpec((D_MODEL, D_MODEL), fix),
                  pl.BlockSpec((PLE_DIM, D_MODEL), fix), tile, vec],
        out_specs=(tile, tile, tile, tile, vec, vec),
        compiler_params=_params(("arbitrary",)),
    )(dx3, x2bf, p, w_gate, w_proj, pre, g)


def _loss_head(y, target, *, tm, name):
    s = y.shape[0]

    def body(y_ref, t_ref, dy_ref, loss_ref, acc_ref):
        err = y_ref[...] - t_ref[...]
        dy_ref[...] = err * (1.0 / D_MODEL)
        part = jnp.sum(err * err, axis=0, keepdims=True)

        @pl.when(pl.program_id(0) == 0)
        def _():
            acc_ref[...] = part

        @pl.when(pl.program_id(0) > 0)
        def _():
            acc_ref[...] += part

        @pl.when(pl.program_id(0) == pl.num_programs(0) - 1)
        def _():
            tot = jnp.sum(acc_ref[...], axis=1, keepdims=True) * (0.5 / D_MODEL)
            loss_ref[...] = jnp.broadcast_to(tot, (8, LANES))

    row = lambda i: (i, 0)
    return pl.pallas_call(
        body, name=name, grid=(s // tm,),
        out_shape=(jax.ShapeDtypeStruct((s, D_MODEL), F32), jax.ShapeDtypeStruct((8, LANES), F32)),
        in_specs=[pl.BlockSpec((tm, D_MODEL), row), pl.BlockSpec((tm, D_MODEL), row)],
        out_specs=(pl.BlockSpec((tm, D_MODEL), row), pl.BlockSpec((8, LANES), lambda i: (0, 0))),
        scratch_shapes=[pltpu.VMEM((1, D_MODEL), F32)],
        compiler_params=_params(("arbitrary",)),
    )(y, target)


def _rope_tables(s):
    inv = ROPE_THETA ** (-jnp.arange(0, ROPE_DIM, 2, dtype=F32) / ROPE_DIM)
    ang = jnp.arange(s, dtype=F32)[:, None] * inv[None, :]
    cos, sin = jnp.cos(ang), jnp.sin(ang)
    ones = jnp.ones((s, HEAD_DIM - ROPE_DIM), F32)
    c_head = jnp.concatenate([cos, cos, ones], axis=1)
    s_head = jnp.concatenate([-sin, sin, 0.0 * ones], axis=1)
    return jnp.concatenate([c_head, c_head], axis=1), jnp.concatenate([s_head, s_head], axis=1)


def _rope(v, cos, sin):
    n = v.shape[1] // LANES
    width = v.shape[1]
    cos_w = jnp.tile(cos, (1, n)) if n > 1 else cos
    sin_w = jnp.tile(sin, (1, n)) if n > 1 else sin
    dim = lax.broadcasted_iota(jnp.int32, (1, width), 1) % HEAD_DIM
    partner = jnp.where(dim < ROPE_DIM // 2, pltpu.roll(v, width - ROPE_DIM // 2, 1), pltpu.roll(v, ROPE_DIM // 2, 1))
    return v * cos_w + partner * sin_w


def _unrope(dv, cos, sin):
    n = dv.shape[1] // LANES
    width = dv.shape[1]
    cos_w = jnp.tile(cos, (1, n)) if n > 1 else cos
    sin_w = jnp.tile(sin, (1, n)) if n > 1 else sin
    t = dv * sin_w
    dim = lax.broadcasted_iota(jnp.int32, (1, width), 1) % HEAD_DIM
    partner = jnp.where(dim < ROPE_DIM // 2, pltpu.roll(t, width - ROPE_DIM // 2, 1),
                        jnp.where(dim < ROPE_DIM, pltpu.roll(t, ROPE_DIM // 2, 1), 0.0))
    return dv * cos_w + partner


def _att_mask(i, nb):
    rows = GROUP * ATT_BLOCK
    r = lax.broadcasted_iota(jnp.int32, (rows, 3 * ATT_BLOCK), 0) % ATT_BLOCK
    cidx = lax.broadcasted_iota(jnp.int32, (rows, 3 * ATT_BLOCK), 1)
    rel = r + ATT_BLOCK - cidx
    ok = (rel <= ATT_BLOCK) & (rel >= -ATT_BLOCK)
    ok = ok & ((cidx >= ATT_BLOCK) | (i > 0)) & ((cidx < 2 * ATT_BLOCK) | (i < nb - 1))
    return ok


def _att_mask_t(i, nb):
    cols = GROUP * ATT_BLOCK
    cidx = lax.broadcasted_iota(jnp.int32, (3 * ATT_BLOCK, cols), 0)
    r = lax.broadcasted_iota(jnp.int32, (3 * ATT_BLOCK, cols), 1) % ATT_BLOCK
    rel = r + ATT_BLOCK - cidx
    ok = (rel <= ATT_BLOCK) & (rel >= -ATT_BLOCK)
    return ok & ((cidx >= ATT_BLOCK) | (i > 0)) & ((cidx < 2 * ATT_BLOCK) | (i < nb - 1))


def _sink_lanes(sink_ref, h):
    cols = GROUP * ATT_BLOCK
    grp = lax.broadcasted_iota(jnp.int32, (1, cols), 1) // ATT_BLOCK
    out = jnp.zeros((1, cols), F32)
    for gq in range(GROUP):
        out = jnp.where(grp == gq, sink_ref[GROUP * h + gq], out)
    return out


def _half_mask(half):
    lane = lax.broadcasted_iota(jnp.int32, (1, LANES), 1)
    return (lane // HEAD_DIM) == half


def _stack_q(q, h):
    parts = []
    for gq in range(GROUP):
        n = GROUP * h + gq
        grp = q[:, LANES * (n // 2):LANES * (n // 2 + 1)]
        grp = jnp.where(_half_mask(n % 2), grp, 0.0)
        if n % 2 != h % 2:
            grp = pltpu.roll(grp, HEAD_DIM, 1)
        parts.append(grp)
    return jnp.concatenate(parts, axis=0)


def _unstack_q(stacked, h, acc):
    for gq in range(GROUP):
        n = GROUP * h + gq
        grp = stacked[ATT_BLOCK * gq:ATT_BLOCK * (gq + 1), :]
        grp = jnp.where(_half_mask(h % 2), grp, 0.0)
        if n % 2 != h % 2:
            grp = pltpu.roll(grp, HEAD_DIM, 1)
        acc[n // 2] = grp if acc[n // 2] is None else acc[n // 2] + grp
    return acc


def _sink_rows(sink_ref, h):
    rows = GROUP * ATT_BLOCK
    grp = lax.broadcasted_iota(jnp.int32, (rows, 1), 0) // ATT_BLOCK
    out = jnp.zeros((rows, 1), F32)
    for gq in range(GROUP):
        out = jnp.where(grp == gq, sink_ref[GROUP * h + gq], out)
    return out


def _att_probs(qs, kh, sink, valid):
    s = lax.dot_general(qs, kh, (((1,), (1,)), ((), ())), preferred_element_type=F32)
    s = jnp.where(valid, s, NEG)
    m = jnp.maximum(jnp.max(s, axis=-1, keepdims=True), sink)
    p = jnp.exp(s - m)
    es = jnp.exp(sink - m)
    den = jnp.sum(p, axis=-1, keepdims=True) + es
    inv = 1.0 / den
    return p * inv, es * inv


def _att_specs(nb):
    prev = lambda i: (jnp.maximum(i - 1, 0), 0)
    cur = lambda i: (i, 0)
    nxt = lambda i: (jnp.minimum(i + 1, nb - 1), 0)
    kv = lambda f: (lambda i: (f(i)[0], 2))
    tab = [pl.BlockSpec((ATT_BLOCK, LANES), f) for f in (cur, prev, cur, nxt)]
    z = [pl.BlockSpec((ATT_BLOCK, D_MODEL), cur)] + [pl.BlockSpec((ATT_BLOCK, 2 * KV_DIM), kv(f)) for f in (prev, cur, nxt)]
    return z, tab


def _att_load(zq_ref, kp_ref, kc_ref, kn_ref, cq_ref, sq_ref, cp_ref, sp_ref, cc_ref, sc_ref, cn_ref, sn_ref):
    q = (_rope(zq_ref[...], cq_ref[...], sq_ref[...]) * (HEAD_DIM ** -0.5))
    ks, vs = [], []
    for ref, c_ref, s_ref in ((kp_ref, cp_ref, sp_ref), (kc_ref, cc_ref, sc_ref), (kn_ref, cn_ref, sn_ref)):
        kvb = ref[...]
        ks.append(_rope(kvb[:, :KV_DIM], c_ref[...], s_ref[...]))
        vs.append(kvb[:, KV_DIM:])
    return q, jnp.concatenate(ks, axis=0).astype(BF16), jnp.concatenate(vs, axis=0).astype(BF16)


def _att_fwd(z, sink, cos, sin, *, comm=None, name):
    s = z.shape[0]
    nb = s // ATT_BLOCK

    def body(zq_ref, kp_ref, kc_ref, kn_ref, cq_ref, cp_ref, cc_ref, cn_ref, sq_ref, sp_ref, sc_ref, sn_ref, sink_ref,
             o_ref):
        i = pl.program_id(0)
        q, k, v = _att_load(zq_ref, kp_ref, kc_ref, kn_ref, cq_ref, sq_ref, cp_ref, sp_ref, cc_ref, sc_ref, cn_ref, sn_ref)
        valid = _att_mask(i, nb)
        acc = [None] * (N_Q_HEADS // 2)
        for h in range(N_KV_HEADS):
            lanes = slice(LANES * (h // 2), LANES * (h // 2 + 1))
            qs = _stack_q(q, h).astype(BF16)
            prob, _ = _att_probs(qs, k[:, lanes], _sink_rows(sink_ref, h), valid)
            oh = jnp.dot(prob.astype(BF16), v[:, lanes], preferred_element_type=F32)
            acc = _unstack_q(oh, h, acc)
        o_ref[...] = jnp.concatenate(acc, axis=1).astype(BF16)

    zspecs, tab = _att_specs(nb)
    return _call(
        body, name=name, grid=(nb,),
        out_shape=(jax.ShapeDtypeStruct((s, D_MODEL), BF16),),
        in_specs=zspecs + tab + tab + [pl.BlockSpec(memory_space=pltpu.SMEM)],
        out_specs=(pl.BlockSpec((ATT_BLOCK, D_MODEL), lambda i: (i, 0)),),
        args=(z, z, z, z, cos, cos, cos, cos, sin, sin, sin, sin, sink), sem=("parallel",), comm=comm,
        edge=lambda: (pl.program_id(0) == 0, pl.program_id(0) == nb - 1))


def _att_bwd(z, do, sink, cos, sin, *, comm=None, name):
    s = z.shape[0]
    nb = s // ATT_BLOCK

    def body(zq_ref, kp_ref, kc_ref, kn_ref, cq_ref, cp_ref, cc_ref, cn_ref, sq_ref, sp_ref, sc_ref, sn_ref, sink_ref,
             do_ref, dq_ref, part_ref, dsink_ref):
        i = pl.program_id(0)
        q, k, v = _att_load(zq_ref, kp_ref, kc_ref, kn_ref, cq_ref, sq_ref, cp_ref, sp_ref, cc_ref, sc_ref, cn_ref, sn_ref)
        valid = _att_mask_t(i, nb)
        dout = do_ref[...].astype(F32)
        dq_acc = [None] * (N_Q_HEADS // 2)
        dk_acc = [None] * 2
        dv_acc = [None] * 2
        rows = []
        nt = (((1,), (1,)), ((), ()))
        for h in range(N_KV_HEADS):
            grp = h // 2
            lanes = slice(LANES * grp, LANES * (grp + 1))
            qs = _stack_q(q, h).astype(BF16)
            dos = _stack_q(dout, h).astype(BF16)
            sink = _sink_lanes(sink_ref, h)
            sc = jnp.where(valid, lax.dot_general(k[:, lanes], qs, nt, preferred_element_type=F32), NEG)
            m = jnp.maximum(jnp.max(sc, axis=0, keepdims=True), sink)
            p = jnp.exp(sc - m)
            es = jnp.exp(sink - m)
            inv = 1.0 / (jnp.sum(p, axis=0, keepdims=True) + es)
            prob = p * inv
            dprob = lax.dot_general(v[:, lanes], dos, nt, preferred_element_type=F32)
            delta = jnp.sum(prob * dprob, axis=0, keepdims=True)
            dsc = (prob * (dprob - delta)).astype(BF16)
            dsk = -(es * inv) * delta
            for gq in range(GROUP):
                tot = jnp.sum(dsk[:, ATT_BLOCK * gq:ATT_BLOCK * (gq + 1)], axis=1, keepdims=True)
                rows.append(jnp.broadcast_to(tot, (1, LANES)))
            dqs = lax.dot_general(dsc, k[:, lanes], (((0,), (0,)), ((), ())), preferred_element_type=F32)
            dq_acc = _unstack_q(dqs, h, dq_acc)
            dkh = jnp.dot(dsc, qs, preferred_element_type=F32)
            dvh = jnp.dot(prob.astype(BF16), dos, preferred_element_type=F32)
            dk_acc[grp] = dkh if dk_acc[grp] is None else dk_acc[grp] + dkh
            dv_acc[grp] = dvh if dv_acc[grp] is None else dv_acc[grp] + dvh
        dq = jnp.concatenate(dq_acc, axis=1) * (HEAD_DIM ** -0.5)
        dq_ref[...] = _unrope(dq, cq_ref[...], sq_ref[...]).astype(BF16)
        part = jnp.concatenate(dk_acc + dv_acc, axis=1)
        for wdw in range(3):
            part_ref[wdw] = part[ATT_BLOCK * wdw:ATT_BLOCK * (wdw + 1), :]
        dsink = jnp.concatenate(rows, axis=0)

        @pl.when(i == 0)
        def _():
            dsink_ref[...] = dsink

        @pl.when(i > 0)
        def _():
            dsink_ref[...] += dsink

    zspecs, tab = _att_specs(nb)
    return _call(
        body, name=name, grid=(nb,),
        out_shape=(jax.ShapeDtypeStruct((s, D_MODEL), BF16), jax.ShapeDtypeStruct((nb, 3, ATT_BLOCK, 2 * KV_DIM), F32),
                   jax.ShapeDtypeStruct((N_Q_HEADS, LANES), F32)),
        in_specs=zspecs + tab + tab + [pl.BlockSpec(memory_space=pltpu.SMEM), pl.BlockSpec((ATT_BLOCK, D_MODEL), lambda i: (i, 0))],
        out_specs=(pl.BlockSpec((ATT_BLOCK, D_MODEL), lambda i: (i, 0)),
                   pl.BlockSpec((None, 3, ATT_BLOCK, 2 * KV_DIM), lambda i: (i, 0, 0, 0)),
                   pl.BlockSpec((N_Q_HEADS, LANES), lambda i: (0, 0))),
        args=(z, z, z, z, cos, cos, cos, cos, sin, sin, sin, sin, sink, do), sem=("arbitrary",), comm=comm,
        edge=lambda: (pl.program_id(0) == 0, pl.program_id(0) == nb - 1))


def _att_bwd_kv(part, cos, sin, *, name):
    nb = part.shape[0]

    def body(pn_ref, pc_ref, pp_ref, c_ref, s_ref, o_ref):
        j = pl.program_id(0)
        tot = pc_ref[...]
        tot = tot + jnp.where(j < nb - 1, pn_ref[...], 0.0)
        tot = tot + jnp.where(j > 0, pp_ref[...], 0.0)
        dk = _unrope(tot[:, :KV_DIM], c_ref[...], s_ref[...])
        o_ref[...] = jnp.concatenate([dk, tot[:, KV_DIM:]], axis=1).astype(BF16)

    blk = (None, None, ATT_BLOCK, 2 * KV_DIM)
    return pl.pallas_call(
        body, name=name, grid=(nb,),
        out_shape=jax.ShapeDtypeStruct((nb * ATT_BLOCK, 2 * KV_DIM), BF16),
        in_specs=[pl.BlockSpec(blk, lambda j: (jnp.minimum(j + 1, nb - 1), 0, 0, 0)),
                  pl.BlockSpec(blk, lambda j: (j, 1, 0, 0)),
                  pl.BlockSpec(blk, lambda j: (jnp.maximum(j - 1, 0), 2, 0, 0)),
                  pl.BlockSpec((ATT_BLOCK, LANES), lambda j: (j, 0)), pl.BlockSpec((ATT_BLOCK, LANES), lambda j: (j, 0))],
        out_specs=pl.BlockSpec((ATT_BLOCK, 2 * KV_DIM), lambda j: (j, 0)),
        compiler_params=_params(("parallel",)),
    )(part, part, part, cos, sin)


def _bdot(a, b, dims):
    return lax.dot_general(a.astype(BF16), b.astype(BF16), (dims, ((), ())), preferred_element_type=F32)


@jax.custom_vjp
def _dot_nn(a, b):
    return _bdot(a, b, ((1,), (0,)))


@jax.custom_vjp
def _dot_nt(a, b):
    return _bdot(a, b, ((1,), (1,)))


@jax.custom_vjp
def _dot_tn(a, b):
    return _bdot(a, b, ((0,), (0,)))


_dot_nn.defvjp(lambda a, b: (_dot_nn(a, b), (a, b)), lambda r, d: (_dot_nt(d, r[1]), _dot_tn(r[0], d)))
_dot_nt.defvjp(lambda a, b: (_dot_nt(a, b), (a, b)), lambda r, d: (_dot_nn(d, r[1]), _dot_tn(d, r[0])))
_dot_tn.defvjp(lambda a, b: (_dot_tn(a, b), (a, b)), lambda r, d: (_dot_nt(r[1], d), _dot_nn(r[0], d)))


def _running_sum(v, up):
    n = v.shape[0]
    rows = lax.broadcasted_iota(jnp.int32, v.shape, 0)
    sh = 1
    while sh < n:
        if up:
            v = v + jnp.where(rows < n - sh, pltpu.roll(v, n - sh, 0), 0.0)
        else:
            v = v + jnp.where(rows >= sh, pltpu.roll(v, sh, 0), 0.0)
        sh *= 2
    return v


@jax.custom_vjp
def _sum_down(v):
    return _running_sum(v, False)


@jax.custom_vjp
def _sum_up(v):
    return _running_sum(v, True)


_sum_down.defvjp(lambda v: (_running_sum(v, False), None), lambda _, d: (_sum_up(d),))
_sum_up.defvjp(lambda v: (_running_sum(v, True), None), lambda _, d: (_sum_down(d),))

N_SUB = HG_CHUNK // HG_SUB


def _fold_blocks(v):
    out = v[:HG_CHUNK]
    for i in range(1, N_SUB):
        out = out + v[HG_CHUNK * i:HG_CHUNK * (i + 1)]
    return out


@jax.custom_vjp
def _fold(v):
    return _fold_blocks(v)


_fold.defvjp(lambda v: (_fold_blocks(v), None), lambda _, d: (jnp.concatenate([d] * N_SUB, axis=0),))


def _hg_consts(rev):
    c, sub = HG_CHUNK, HG_SUB
    rowpos = lax.broadcasted_iota(jnp.int32, (c, HG_DIM), 0)
    rr = lax.broadcasted_iota(jnp.int32, (N_SUB * c, c), 0)
    key = lax.broadcasted_iota(jnp.int32, (N_SUB * c, c), 1)
    blk, qry = rr // c, rr % c
    if rev:
        rowpos, qry, key = c - 1 - rowpos, c - 1 - qry, c - 1 - key
    keep = (key // sub == blk) & (key <= qry)
    return keep, rowpos


def _pick(b, rowpos, t):
    return jnp.sum(jnp.where(rowpos == t, b, 0.0), axis=0, keepdims=True)


def _hg_local(zq, zf, zv, lbv, consts, dots):
    dot_nn, dot_nt, dot_tn, cum, fold = dots
    keep, rowpos = consts
    sig = _sigmoid(zf)
    f = lbv + (1.0 - lbv) * sig
    g = jnp.log(f)
    k = (1.0 - lbv) * (1.0 - sig)
    q = zq * _sigmoid(zq)
    b = cum(g)
    ends = [_pick(b, rowpos, (j + 1) * HG_SUB - 1) for j in range(N_SUB)]
    b_last = ends[-1]
    b_end = b_last
    for j in range(N_SUB - 1):
        b_end = jnp.where(rowpos // HG_SUB == j, ends[j], b_end)
    kc = k * jnp.exp(b_end - b)
    qbs = [q * jnp.exp(jnp.where(rowpos >= j * HG_SUB, b - ends[j], 0.0)) for j in range(N_SUB)]
    scores = fold(jnp.where(keep, dot_nt(jnp.concatenate(qbs, axis=0), kc), 0.0))
    return dot_nn(scores, zv), q * jnp.exp(b), k * jnp.exp(b_last - b), jnp.exp(b_last)


def _hg_chunk(zq, zf, zv, lbv, st, consts, dots):
    intra, qs, kd, dec = _hg_local(zq, zf, zv, lbv, consts, dots)
    return intra + dots[1](qs, st), dec * st + dots[2](zv, kd)


def _hg_dots(diff, rev):
    if diff:
        return _dot_nn, _dot_nt, _dot_tn, (_sum_up if rev else _sum_down), _fold
    return (lambda a, b: _bdot(a, b, ((1,), (0,))), lambda a, b: _bdot(a, b, ((1,), (1,))),
            lambda a, b: _bdot(a, b, ((0,), (0,))), lambda v: _running_sum(v, rev), _fold_blocks)


def _hg_specs(ts, nch, trow):
    tile = pl.BlockSpec((ts, HG_DIM), lambda h, t: (trow(t), h))
    mats = pl.BlockSpec((None, nch, HG_DIM, HG_DIM), lambda h, t: (h, trow(t), 0, 0))
    vecs = pl.BlockSpec((None, nch, 1, HG_DIM), lambda h, t: (h, trow(t), 0, 0))
    return tile, mats, vecs


def _time_order(nch, rev):
    return range(nch - 1, -1, -1) if rev else range(nch)


def _chunk_rows(c):
    return pl.ds(c * HG_CHUNK, HG_CHUNK)


def _hg_edge(nt):
    h, t = pl.program_id(0), pl.program_id(1)
    return (h == 0) & (t == 0), (h == HG_HEADS - 1) & (t == nt - 1)


def _hg_fwd(z, lb, *, rev, ts, comm=None, name):
    s = z.shape[0]
    nt = s // ts
    nch = ts // HG_CHUNK
    fcol = HG_HEADS * (2 if rev else 1)

    def body(zq_ref, zf_ref, zv_ref, lb_ref, o_ref, st_ref, qs_ref, dec_ref, state_ref):
        @pl.when(pl.program_id(1) == 0)
        def _():
            state_ref[...] = jnp.zeros_like(state_ref)

        consts = _hg_consts(rev)
        dots = _hg_dots(False, rev)
        lbv = lb_ref[...]
        local = {}
        for c in range(nch):
            rows = _chunk_rows(c)
            zv = zv_ref[rows, :]
            intra, qs, kd, dec = _hg_local(zq_ref[rows, :], zf_ref[rows, :], zv, lbv, consts, dots)
            qs = qs.astype(BF16)
            qs_ref[rows, :] = qs
            dec_ref[c] = dec
            local[c] = (intra, qs, dec, dots[2](zv, kd))
        st = state_ref[...]
        for c in _time_order(nch, rev):
            intra, qs, dec, upd = local[c]
            st_ref[c] = st.astype(BF16)
            o_ref[_chunk_rows(c), :] = intra + _bdot(qs, st, ((1,), (1,)))
            st = dec * st + upd
        state_ref[...] = st

    trow = (lambda t: nt - 1 - t) if rev else (lambda t: t)
    col = lambda off: pl.BlockSpec((ts, HG_DIM), lambda h, t: (trow(t), off + h))
    tile, mats, vecs = _hg_specs(ts, nch, trow)
    nchunks = s // HG_CHUNK
    return _call(
        body, name=name, grid=(HG_HEADS, nt),
        out_shape=(jax.ShapeDtypeStruct((s, D_MODEL), F32),
                   jax.ShapeDtypeStruct((HG_HEADS, nchunks, HG_DIM, HG_DIM), BF16),
                   jax.ShapeDtypeStruct((s, D_MODEL), BF16),
                   jax.ShapeDtypeStruct((HG_HEADS, nchunks, 1, HG_DIM), F32)),
        in_specs=[col(0), col(fcol), col(3 * HG_HEADS), pl.BlockSpec((None, 1, HG_DIM), lambda h, t: (h, 0, 0))],
        out_specs=(tile, mats, tile, vecs), args=(z, z, z, lb),
        scratch_shapes=[pltpu.VMEM((HG_DIM, HG_DIM), F32)], sem=("parallel", "arbitrary"), comm=comm,
        edge=lambda: _hg_edge(nt))


def _hg_bwd(z, lb, states, qs, dec, dout, addq, addv, *, rev, ts, comm=None, name):
    s = z.shape[0]
    nt = s // ts
    nch = ts // HG_CHUNK
    fcol = HG_HEADS * (2 if rev else 1)
    has_add = addq is not None

    def body(*refs):
        zq_ref, zf_ref, zv_ref, lb_ref, st_ref, qs_ref, dec_ref, do_ref = refs[:8]
        aq_ref, av_ref = (refs[8], refs[9]) if has_add else (None, None)
        dq_ref, df_ref, dv_ref, dlb_ref, grad_ref = refs[-5:]

        @pl.when(pl.program_id(1) == 0)
        def _():
            grad_ref[...] = jnp.zeros_like(grad_ref)

        consts = _hg_consts(rev)
        dots = _hg_dots(True, rev)
        lbv = lb_ref[...]
        prods = {c: _bdot(do_ref[_chunk_rows(c), :], qs_ref[_chunk_rows(c), :], ((0,), (0,))) for c in range(nch)}
        gleave = {}
        gr = grad_ref[...]
        for c in reversed(_time_order(nch, rev)):
            gleave[c] = gr
            gr = dec_ref[c] * gr + prods[c]
        grad_ref[...] = gr
        dlb_blk = jnp.zeros((1, HG_DIM), F32)
        for c in range(nch):
            rows = _chunk_rows(c)
            fn = lambda a, b2, c2, d2, e2: _hg_chunk(a, b2, c2, d2, e2, consts, dots)
            _, pull = jax.vjp(fn, zq_ref[rows, :], zf_ref[rows, :], zv_ref[rows, :], lbv, st_ref[c].astype(F32))
            dq, df, dv, dlb, _ = pull((do_ref[rows, :], gleave[c]))
            if has_add:
                dq = dq + aq_ref[rows, :]
                dv = dv + av_ref[rows, :]
            dq_ref[rows, :] = dq.astype(dq_ref.dtype)
            df_ref[rows, :] = df.astype(BF16)
            dv_ref[rows, :] = dv.astype(dv_ref.dtype)
            dlb_blk = dlb_blk + dlb

        @pl.when(pl.program_id(1) == 0)
        def _():
            dlb_ref[...] = dlb_blk

        @pl.when(pl.program_id(1) > 0)
        def _():
            dlb_ref[...] += dlb_blk

    trow = (lambda t: t) if rev else (lambda t: nt - 1 - t)
    col = lambda off: pl.BlockSpec((ts, HG_DIM), lambda h, t: (trow(t), off + h))
    tile, mats, vecs = _hg_specs(ts, nch, trow)
    in_specs = [col(0), col(fcol), col(3 * HG_HEADS), pl.BlockSpec((None, 1, HG_DIM), lambda h, t: (h, 0, 0)),
                mats, tile, vecs, tile]
    args = [z, z, z, lb, states, qs, dec, dout]
    if has_add:
        in_specs += [tile, tile]
        args += [addq, addv]
    act = lambda dt: jax.ShapeDtypeStruct((s, D_MODEL), dt)
    sums = BF16 if has_add else F32
    return _call(
        body, name=name, grid=(HG_HEADS, nt),
        out_shape=(act(sums), act(BF16), act(sums), jax.ShapeDtypeStruct((HG_HEADS, 1, HG_DIM), F32)),
        in_specs=in_specs,
        out_specs=(tile, tile, tile, pl.BlockSpec((None, 1, HG_DIM), lambda h, t: (h, 0, 0))), args=tuple(args),
        scratch_shapes=[pltpu.VMEM((HG_DIM, HG_DIM), F32)], sem=("parallel", "arbitrary"), comm=comm,
        edge=lambda: _hg_edge(nt))


def _hg_post(of, ob, z, norm_g, *, tm, name):
    s = of.shape[0]

    def body(of_ref, ob_ref, gate_ref, ng_ref, y_ref):
        gn = ng_ref[...]
        for h in range(HG_HEADS):
            ln = slice(HG_DIM * h, HG_DIM * (h + 1))
            o = of_ref[:, ln] + ob_ref[:, ln]
            r = lax.rsqrt(jnp.mean(o * o, axis=-1, keepdims=True) + LN_EPS)
            gt = gate_ref[:, ln]
            y_ref[:, ln] = (o * r * gn * gt * _sigmoid(gt)).astype(BF16)

    row = lambda i: (i, 0)
    return pl.pallas_call(
        body, name=name, grid=(s // tm,),
        out_shape=jax.ShapeDtypeStruct((s, D_MODEL), BF16),
        in_specs=[pl.BlockSpec((tm, D_MODEL), row), pl.BlockSpec((tm, D_MODEL), row),
                  pl.BlockSpec((tm, D_MODEL), lambda i: (i, 4)), pl.BlockSpec((1, HG_DIM), lambda i: (0, 0))],
        out_specs=pl.BlockSpec((tm, D_MODEL), row),
        compiler_params=_params(("parallel",)),
    )(of, ob, z, norm_g)


def _hg_post_bwd(dy, of, ob, z, norm_g, *, tm, name):
    s = of.shape[0]

    def body(dy_ref, of_ref, ob_ref, gate_ref, ng_ref, do_ref, dgate_ref, dng_ref):
        gn = ng_ref[...]
        tot = jnp.zeros((1, HG_DIM), F32)
        for h in range(HG_HEADS):
            ln = slice(HG_DIM * h, HG_DIM * (h + 1))
            d = dy_ref[:, ln].astype(F32)
            o = of_ref[:, ln] + ob_ref[:, ln]
            r = lax.rsqrt(jnp.mean(o * o, axis=-1, keepdims=True) + LN_EPS)
            ohat = o * r
            gt = gate_ref[:, ln]
            sg = _sigmoid(gt)
            don = d * gt * sg
            dgate_ref[:, ln] = (d * ohat * gn * sg * (1.0 + gt * (1.0 - sg))).astype(BF16)
            tot = tot + jnp.sum(don * ohat, axis=0, keepdims=True)
            dohat = don * gn
            do_ref[:, ln] = r * (dohat - ohat * jnp.mean(dohat * ohat, axis=-1, keepdims=True))

        @pl.when(pl.program_id(0) == 0)
        def _():
            dng_ref[...] = tot

        @pl.when(pl.program_id(0) > 0)
        def _():
            dng_ref[...] += tot

    row = lambda i: (i, 0)
    return pl.pallas_call(
        body, name=name, grid=(s // tm,),
        out_shape=(jax.ShapeDtypeStruct((s, D_MODEL), F32), jax.ShapeDtypeStruct((s, D_MODEL), BF16),
                   jax.ShapeDtypeStruct((1, HG_DIM), F32)),
        in_specs=[pl.BlockSpec((tm, D_MODEL), row), pl.BlockSpec((tm, D_MODEL), row), pl.BlockSpec((tm, D_MODEL), row),
                  pl.BlockSpec((tm, D_MODEL), lambda i: (i, 4)), pl.BlockSpec((1, HG_DIM), lambda i: (0, 0))],
        out_specs=(pl.BlockSpec((tm, D_MODEL), row), pl.BlockSpec((tm, D_MODEL), row),
                   pl.BlockSpec((1, HG_DIM), lambda i: (0, 0))),
        compiler_params=_params(("arbitrary",)),
    )(dy, of, ob, z, norm_g)


def _lb_fwd(logits, *, name):
    w = logits.shape[1]

    def body(l_ref, o_ref):
        lg = l_ref[...]
        e = jnp.exp(lg - jnp.max(lg, axis=0, keepdims=True))
        sm = e / jnp.sum(e, axis=0, keepdims=True)
        o_ref[0:1, :] = sm[1:2]
        o_ref[1:2, :] = sm[1:2] + sm[2:3] + sm[3:4]

    return pl.pallas_call(body, name=name, out_shape=jax.ShapeDtypeStruct((2, w), F32))(logits)


def _lb_bwd(logits, dlb, *, name):
    w = logits.shape[1]

    def body(l_ref, d_ref, o_ref):
        lg = l_ref[...]
        e = jnp.exp(lg - jnp.max(lg, axis=0, keepdims=True))
        sm = e / jnp.sum(e, axis=0, keepdims=True)
        d1, d3 = d_ref[0:1, :], d_ref[1:2, :]
        dot = sm[1:2] * (d1 + d3) + (sm[2:3] + sm[3:4]) * d3
        o_ref[0:1, :] = -sm[0:1] * dot
        o_ref[1:2, :] = sm[1:2] * (d1 + d3 - dot)
        o_ref[2:3, :] = sm[2:3] * (d3 - dot)
        o_ref[3:4, :] = sm[3:4] * (d3 - dot)

    return pl.pallas_call(body, name=name, out_shape=jax.ShapeDtypeStruct((4, w), F32))(logits, dlb)


def _adamw(w, g, m, v, *, tr, g_off=0, name):
    rows = w.shape[0]
    parts = g.ndim == 3
    c1 = 1.0 / (1.0 - ADAM_B1 ** ADAM_STEP)
    c2 = 1.0 / (1.0 - ADAM_B2 ** ADAM_STEP)

    def body(w_ref, g_ref, m_ref, v_ref, go_ref, d_ref, mo_ref, vo_ref):
        if parts:
            gg = g_ref[0].astype(F32)
            for i in range(1, N_DEV):
                gg = gg + g_ref[i].astype(F32)
        else:
            gg = g_ref[...]
        mm = ADAM_B1 * m_ref[...] + (1.0 - ADAM_B1) * gg
        vv = ADAM_B2 * v_ref[...] + (1.0 - ADAM_B2) * (gg * gg)
        go_ref[...] = gg
        mo_ref[...] = mm
        vo_ref[...] = vv
        d_ref[...] = -ADAM_LR * ((mm * c1) / (jnp.sqrt(vv * c2) + ADAM_EPS) + ADAM_WD * w_ref[...])

    tile = pl.BlockSpec((tr, D_MODEL), lambda i: (i, 0))
    gspec = pl.BlockSpec((N_DEV, tr, D_MODEL), lambda i: (0, i + g_off // tr, 0)) if parts else tile
    out = jax.ShapeDtypeStruct((rows, D_MODEL), F32)
    return pl.pallas_call(
        body, name=name, grid=(rows // tr,),
        out_shape=(out, out, out, out),
        in_specs=[tile, gspec, tile, tile], out_specs=(tile, tile, tile, tile),
        compiler_params=_params(("parallel",)),
    )(w, g, m, v)


def _sum8(parts, *, name):
    def body(p_ref, o_ref):
        tot = p_ref[0]
        for i in range(1, N_DEV):
            tot = tot + p_ref[i]
        o_ref[...] = tot

    return pl.pallas_call(body, name=name, out_shape=jax.ShapeDtypeStruct(parts.shape[1:], parts.dtype))(parts)


def _layer_params(i):
    j = i // 2
    mix = [("att_w_qkv", j, 1), ("att_w_o", j, 0)] if i % 2 == 0 else [("hgrn_w_in", j, 1), ("hgrn_w_o", j, 0)]
    return mix + [("ffn_w_in", i, 1), ("ffn_w_out", i, 0), ("ple_w_gate", i, 0), ("ple_w_proj", i, 1)]


def _pack_local(tree, params):
    return jnp.concatenate([tree[n][j].reshape(-1, D_MODEL) for n, j, _ in params], axis=0)


def _unpack_local(packed, params, like):
    out, r = {}, 0
    for n, _, _ in params:
        shp = like[n].shape[1:]
        k = shp[0] * shp[1] // D_MODEL
        out[n] = packed[r:r + k].reshape(shp)
        r += k
    return out


def _unpack_gathered(gathered, i, like):
    out, r = {}, 0
    for n, _, ax in _layer_params(i):
        shp = like[n].shape[1:]
        k = shp[0] * shp[1] // D_MODEL
        t = gathered[:, r:r + k].reshape((N_DEV,) + shp)
        out[n] = (jnp.moveaxis(t, 0, 1).reshape(shp[0], N_DEV * shp[1]) if ax == 1
                  else t.reshape(N_DEV * shp[0], shp[1]))
        r += k
    return out


def _pack_full(grads, params, like):
    cols = []
    for n, _, ax in params:
        shp = like[n].shape[1:]
        t = (jnp.moveaxis(grads[n].reshape(shp[0], N_DEV, shp[1]), 1, 0) if ax == 1
             else grads[n].reshape(N_DEV, shp[0], shp[1]))
        cols.append(t.reshape(N_DEV, -1, D_MODEL).astype(BF16))
    return jnp.concatenate(cols, axis=1)


def _row_tile(rows):
    return max(t for t in range(16, 257, 16) if rows % t == 0)


SMALL_ROWS = 24


def _pad_row(a):
    flat = a.reshape(1, -1)
    return jnp.pad(flat, ((0, 0), (0, D_MODEL - flat.shape[1])))


def _tile(n, pref):
    return min(n, pref)


def kernel(x, p, att_w_qkv, att_sink, att_w_o, hgrn_w_in, hgrn_lb_logits, hgrn_norm_g, hgrn_w_o, ln_mix_g, ln_mix_b, ffn_w_in, ffn_w_out, ln_ffn_g, ln_ffn_b, ple_w_gate, ple_w_proj, loss_target, m_att_w_qkv, m_att_sink, m_att_w_o, m_hgrn_w_in, m_hgrn_lb_logits, m_hgrn_norm_g, m_hgrn_w_o, m_ln_mix_g, m_ln_mix_b, m_ffn_w_in, m_ffn_w_out, m_ln_ffn_g, m_ln_ffn_b, m_ple_w_gate, m_ple_w_proj, v_att_w_qkv, v_att_sink, v_att_w_o, v_hgrn_w_in, v_hgrn_lb_logits, v_hgrn_norm_g, v_hgrn_w_o, v_ln_mix_g, v_ln_mix_b, v_ffn_w_in, v_ffn_w_out, v_ln_ffn_g, v_ln_ffn_b, v_ple_w_gate, v_ple_w_proj):
    names = ["att_w_qkv", "att_sink", "att_w_o", "hgrn_w_in", "hgrn_lb_logits", "hgrn_norm_g", "hgrn_w_o", "ln_mix_g",
             "ln_mix_b", "ffn_w_in", "ffn_w_out", "ln_ffn_g", "ln_ffn_b", "ple_w_gate", "ple_w_proj"]
    w = dict(zip(names, (att_w_qkv, att_sink, att_w_o, hgrn_w_in, hgrn_lb_logits, hgrn_norm_g, hgrn_w_o, ln_mix_g,
                         ln_mix_b, ffn_w_in, ffn_w_out, ln_ffn_g, ln_ffn_b, ple_w_gate, ple_w_proj)))
    mom = dict(zip(names, (m_att_w_qkv, m_att_sink, m_att_w_o, m_hgrn_w_in, m_hgrn_lb_logits, m_hgrn_norm_g, m_hgrn_w_o,
                           m_ln_mix_g, m_ln_mix_b, m_ffn_w_in, m_ffn_w_out, m_ln_ffn_g, m_ln_ffn_b, m_ple_w_gate,
                           m_ple_w_proj)))
    var = dict(zip(names, (v_att_w_qkv, v_att_sink, v_att_w_o, v_hgrn_w_in, v_hgrn_lb_logits, v_hgrn_norm_g, v_hgrn_w_o,
                           v_ln_mix_g, v_ln_mix_b, v_ffn_w_in, v_ffn_w_out, v_ln_ffn_g, v_ln_ffn_b, v_ple_w_gate,
                           v_ple_w_proj)))
    s = x.shape[1]
    me = 4 * lax.axis_index("x") + 2 * lax.axis_index("y") + lax.axis_index("c")
    tm = _tile(s, 512)
    tbig = _tile(s, 1024)
    ts = _tile(s // 2, 2048)
    x0 = x.reshape(s, D_MODEL)
    target = loss_target.reshape(s, D_MODEL)
    pl_in = p.reshape(DEPTH, s, PLE_DIM)

    w_rows = [_pack_local(w, _layer_params(i)).astype(BF16) for i in range(DEPTH)]
    full = _unpack_gathered(_gather(w_rows[0], name="gather_weights"), 0, w)
    lb_rows = jnp.pad(hgrn_lb_logits.reshape(8, HG_DIM), ((0, 0), (0, D_MODEL - HG_DIM)))
    lb_all = _gather(lb_rows, name="gather_lb")[:, :, :HG_DIM]
    logits_full = jnp.moveaxis(lb_all, 0, 1).reshape(DEPTH, 2 * D_MODEL)
    lb = _lb_fwd(logits_full, name="lb_fwd")
    cos, sin = _rope_tables(s)

    saved = []
    xf, xb = x0, x0
    for i in range(DEPTH):
        j = i // 2
        sv = {"x": xf, "xb": xb, "w": full}
        nxt = (w_rows[i + 1], True) if i + 1 < DEPTH else None
        if i % 2 == 0:
            z = _mm(xb, full["att_w_qkv"], tm=tbig, tn=512, tk=D_MODEL, name="att_in")
            o, *more = _att_fwd(z, w["att_sink"][j], cos, sin, comm=nxt, name="att_fwd")
            w_o = full["att_w_o"]
        else:
            z = _mm(xb, full["hgrn_w_in"], tm=tbig, tn=1024, tk=D_MODEL, name="hgrn_in")
            lbl = lb[j].reshape(2, HG_HEADS, 1, HG_DIM)
            of, st_f, qs_f, dec_f, *more = _hg_fwd(z, lbl[0], rev=False, ts=ts, comm=nxt, name="hgrn_fwd")
            ob, st_b, qs_b, dec_b = _hg_fwd(z, lbl[1], rev=True, ts=ts, name="hgrn_fwd_rev")
            o = _hg_post(of, ob, z, w["hgrn_norm_g"][j].reshape(1, HG_DIM), tm=tm, name="hgrn_post")
            w_o = full["hgrn_w_o"]
            sv.update(of=of, ob=ob, st_f=st_f, st_b=st_b, lbl=lbl, qs_f=qs_f, qs_b=qs_b, dec_f=dec_f, dec_b=dec_b)
        sv.update(z=z, o=o)
        pre1, x1, x1b = _proj_ln(o, w_o, xf, w["ln_mix_g"][i:i + 1], w["ln_mix_b"][i:i + 1], tm=tm, name="mix_out_ln")
        gg, uu, act = _ffn_in(x1b, full["ffn_w_in"], tm=tm, tn=FF_TILE, name="ffn_in")
        pre2, x2b, xf, xb = _ffn_out_ple(act, full["ffn_w_out"], x1, w["ln_ffn_g"][i:i + 1], w["ln_ffn_b"][i:i + 1],
                                         pl_in[i], full["ple_w_gate"], full["ple_w_proj"], tm=tm, name="ffn_out_ple")
        sv.update(pre1=pre1, x1=x1, x1b=x1b, g=gg, u=uu, act=act, pre2=pre2, x2b=x2b)
        saved.append(sv)
        if nxt is not None:
            full = _unpack_gathered(more[0], i + 1, w)

    dx, loss_blk = _loss_head(xf, target, tm=tm, name="loss_head")
    loss = lax.psum(loss_blk[0, 0], AXES)

    small = {n: [None] * DEPTH for n in ("ln_mix_g", "ln_mix_b", "ln_ffn_g", "ln_ffn_b")}
    dlb_rows = [None] * 4
    dnorm, dsink = [None] * 2, [None] * 2
    recv_late, recv_early = [None] * DEPTH, [None] * DEPTH
    above = None
    mmw = functools.partial(_mm, ta=True, tk=_tile(s, 2048), out_dtype=BF16)
    for i in reversed(range(DEPTH)):
        j = i // 2
        sv = saved[i]
        full, gl = sv["w"], {}
        da, dpp, dy2, dy2b, small["ln_ffn_g"][i], small["ln_ffn_b"][i] = _ple_ln_bwd(
            dx, sv["x2b"], pl_in[i], full["ple_w_gate"], full["ple_w_proj"], sv["pre2"], w["ln_ffn_g"][i:i + 1],
            tm=tm, name="ple_ln_bwd")
        gl["ple_w_gate"] = mmw(sv["x2b"], da, tm=D_MODEL, tn=D_MODEL, name="dw_ple_gate")
        gl["ple_w_proj"] = mmw(pl_in[i], dpp, tm=PLE_DIM, tn=D_MODEL, name="dw_ple_proj")
        dg, du = _ffn_bwd_act(dy2b, full["ffn_w_out"], sv["g"], sv["u"], tm=_tile(s, SUB_ROWS), tn=D_FF,
                              name="ffn_bwd_act")
        gl["ffn_w_out"] = mmw(sv["act"], dy2b, tm=FF_TILE, tn=D_MODEL, name="dw_ffn_out")
        dy1, dy1b, small["ln_mix_g"][i], small["ln_mix_b"][i] = _dx_from_pieces(
            [dg, du], full["ffn_w_in"], dy2, tm=_tile(s, SUB_ROWS), ln=(sv["pre1"], w["ln_mix_g"][i:i + 1]),
            name="ffn_bwd_x_ln")
        gl["ffn_w_in"] = mmw(sv["x1b"], [dg, du], tm=D_MODEL, tn=FF_TILE, name="dw_ffn_in")
        n_out, n_inw = ("att_w_o", "att_w_qkv") if i % 2 == 0 else ("hgrn_w_o", "hgrn_w_in")
        do = _mm(dy1b, full[n_out], tm=tbig, tn=D_MODEL, tk=D_MODEL, tb=True, out_dtype=BF16, name="mix_out_bwd")
        gl[n_out] = mmw(sv["o"], dy1b, tm=D_MODEL, tn=D_MODEL, name="dw_mix_out")
        early = _pack_full(gl, _layer_params(i)[1:], w)
        comm = (early if above is None else jnp.concatenate([above, early], axis=1), False)
        if i % 2 == 0:
            dzq, part, dsk, *more = _att_bwd(sv["z"], do, w["att_sink"][j], cos, sin, comm=comm, name="att_bwd")
            dz = [dzq, _att_bwd_kv(part, cos, sin, name="att_bwd_kv")]
            dsink[j] = dsk[:, 0]
        else:
            dsum, dgate, dnorm[j] = _hg_post_bwd(do, sv["of"], sv["ob"], sv["z"], w["hgrn_norm_g"][j].reshape(1, HG_DIM),
                                                 tm=tm, name="hgrn_post_bwd")
            dq1, df1, dv1, dlb1, *more = _hg_bwd(sv["z"], sv["lbl"][0], sv["st_f"], sv["qs_f"], sv["dec_f"], dsum, None,
                                                 None, rev=False, ts=ts, comm=comm, name="hgrn_bwd")
            dq2, df2, dv2, dlb2 = _hg_bwd(sv["z"], sv["lbl"][1], sv["st_b"], sv["qs_b"], sv["dec_b"], dsum, dq1, dv1,
                                          rev=True, ts=ts, name="hgrn_bwd_rev")
            dz = [dq2, df1, df2, dv2, dgate]
            dlb_rows[2 * j] = dlb1.reshape(1, D_MODEL)
            dlb_rows[2 * j + 1] = dlb2.reshape(1, D_MODEL)
        if above is not None:
            recv_late[i + 1] = (more[0], 0)
        recv_early[i] = (more[0], 0 if above is None else above.shape[1])
        dx = _dx_from_pieces(dz, full[n_inw], dy1, tm=tm, name="mix_in_bwd")
        gl[n_inw] = mmw(sv["xb"], dz, tm=D_MODEL, tn=512, name="dw_mix_in")
        above = _pack_full(gl, _layer_params(i)[:1], w)
    grad_x = dx.reshape(x.shape)
    recv_late[0] = (_exchange(above, name="exchange_grads"), 0)

    big_out = [{n: [None] * w[n].shape[0] for n, _ in BIG} for _ in range(4)]
    for i in range(DEPTH):
        for params, (got, off) in ((_layer_params(i)[:1], recv_late[i]), (_layer_params(i)[1:], recv_early[i])):
            w_part = _pack_local(w, params)
            outs = _adamw(w_part, got, _pack_local(mom, params), _pack_local(var, params),
                          tr=_row_tile(math.gcd(w_part.shape[0], off)), g_off=off, name="adamw_big")
            for kind, packed in enumerate(outs):
                for (n, j, _), piece in zip(params, _unpack_local(packed, params, w).values()):
                    big_out[kind][n][j] = piece
    big_out = [{n: jnp.stack(v) for n, v in kind.items()} for kind in big_out]

    small_rows = jnp.concatenate(
        [jnp.concatenate(small[n], axis=0) for n in ("ln_mix_g", "ln_mix_b", "ln_ffn_g", "ln_ffn_b")] + dlb_rows
        + [_pad_row(jnp.stack(dnorm)), _pad_row(jnp.stack(dsink)), jnp.zeros((2, D_MODEL), F32)], axis=0)
    small_all = _gather(small_rows, name="gather_small")
    lbw, lbm, lbv = (t.reshape(4, 2 * HG_DIM) for t in (hgrn_lb_logits, mom["hgrn_lb_logits"], var["hgrn_lb_logits"]))
    summed = _sum8(small_all, name="sum_small")
    dlb_mine = lax.dynamic_slice_in_dim(summed[16:20].reshape(2, 2, HG_HEADS, HG_DIM), me, 1, axis=2)
    dlogits = _lb_bwd(lbw, dlb_mine.reshape(2, 2 * HG_DIM), name="lb_bwd")

    def small_pack(ln4, lbt, ng, sk):
        return jnp.concatenate([ln4[n] for n in ("ln_mix_g", "ln_mix_b", "ln_ffn_g", "ln_ffn_b")]
                               + [_pad_row(lbt), _pad_row(ng), _pad_row(sk), jnp.zeros((5, D_MODEL), F32)], axis=0)

    g_small = jnp.concatenate([summed[:16], _pad_row(dlogits), summed[20:22], jnp.zeros((5, D_MODEL), F32)], axis=0)
    souts = _adamw(small_pack(w, lbw, w["hgrn_norm_g"], w["att_sink"]), g_small,
                   small_pack(mom, lbm, mom["hgrn_norm_g"], mom["att_sink"]),
                   small_pack(var, lbv, var["hgrn_norm_g"], var["att_sink"]), tr=SMALL_ROWS, name="adamw_small")

    def small_unpack(t):
        out = {n: t[4 * k:4 * k + 4] for k, n in enumerate(("ln_mix_g", "ln_mix_b", "ln_ffn_g", "ln_ffn_b"))}
        out["hgrn_lb_logits"] = t[16].reshape(hgrn_lb_logits.shape)
        out["hgrn_norm_g"] = t[17, :2 * HG_DIM].reshape(hgrn_norm_g.shape)
        out["att_sink"] = t[18, :2 * N_Q_HEADS].reshape(att_sink.shape)
        return out

    result = [loss, grad_x]
    for big_t, small_t in zip(big_out, souts):
        merged = dict(big_t)
        merged.update(small_unpack(small_t))
        result += [merged[n] for n in names]
    return tuple(result)
```

```python
import functools
import math

import jax
import jax.numpy as jnp
from jax import lax
from jax.experimental import pallas as pl
from jax.experimental.pallas import tpu as pltpu

F32 = jnp.float32
BF16 = jnp.bfloat16

D_MODEL = 1024
DEPTH = 4
HEAD_DIM = 64
N_Q_HEADS = 16
N_KV_HEADS = 4
GROUP = 4
KV_DIM = 256
ATT_BLOCK = 128
ROPE_DIM = 16
ROPE_THETA = 500000.0
HG_HEADS = 8
HG_DIM = 128
HG_CHUNK = 64
HG_SUB = 16
D_FF = 2816
FF_TILE = 1408
SUB_ROWS = 256
PLE_DIM = 256
ALPHA = (2 * DEPTH) ** 0.25
LN_EPS = 1e-5
ADAM_LR, ADAM_B1, ADAM_B2, ADAM_EPS, ADAM_WD, ADAM_STEP = 0.001, 0.9, 0.999, 1e-08, 0.01, 10

N_DEV = 8
LANES = 128
VMEM_LIMIT = 52 * 1024 * 1024
NEG = -1e30
MESH = pl.DeviceIdType.MESH
AXES = ("x", "y", "c")

BIG = (("att_w_qkv", 2), ("att_w_o", 1), ("hgrn_w_in", 2), ("hgrn_w_o", 1), ("ffn_w_in", 2), ("ffn_w_out", 1),
       ("ple_w_gate", 1), ("ple_w_proj", 2))


def _params(sem=None, vmem=VMEM_LIMIT):
    return pltpu.CompilerParams(dimension_semantics=sem, vmem_limit_bytes=vmem)


def _sigmoid(x):
    return jax.nn.sigmoid(x)


def _direct_copies(src_ref, out_ref, send_sems, recv_sems, local_sem, gather, arrivals):
    x, y, c = lax.axis_index("x"), lax.axis_index("y"), lax.axis_index("c")
    me = 4 * x + 2 * y + c
    mine = (lambda j: src_ref) if gather else (lambda j: src_ref.at[j])
    pairs = []
    for k in range(1, N_DEV):
        px, py, pc = x ^ (k >> 2), y ^ ((k >> 1) & 1), c ^ (k & 1)
        peer = 4 * px + 2 * py + pc
        send = pltpu.make_async_remote_copy(
            src_ref=mine(peer), dst_ref=out_ref.at[me], send_sem=send_sems.at[k], recv_sem=recv_sems.at[k],
            device_id=(px, py, pc), device_id_type=MESH)
        arrival = pltpu.make_async_remote_copy(
            src_ref=mine(peer), dst_ref=out_ref.at[peer], send_sem=send_sems.at[k], recv_sem=recv_sems.at[k],
            device_id=(x, y, c), device_id_type=MESH) if arrivals else None
        pairs.append((send, arrival))
    return pltpu.make_async_copy(mine(me), out_ref.at[me], local_sem), pairs


def _direct_start(*refs, gather):
    local, pairs = _direct_copies(*refs, gather, False)
    local.start()
    for send, _ in pairs:
        send.start()


def _direct_wait(*refs, gather):
    local, pairs = _direct_copies(*refs, gather, True)
    for send, arrival in pairs:
        send.wait_send()
        arrival.wait_recv()
    local.wait()


COMM_SCRATCH = [pltpu.SemaphoreType.DMA((N_DEV,)), pltpu.SemaphoreType.DMA((N_DEV,)), pltpu.SemaphoreType.DMA]


def _exchange(src, *, gather=False, name):
    def body(*refs):
        _direct_start(*refs, gather=gather)
        _direct_wait(*refs, gather=gather)

    blk = tuple(src.shape) if gather else tuple(src.shape[1:])
    return pl.pallas_call(
        body, name=name,
        out_shape=jax.ShapeDtypeStruct((N_DEV,) + blk, src.dtype),
        in_specs=[pl.BlockSpec(memory_space=pltpu.HBM)],
        out_specs=pl.BlockSpec(memory_space=pltpu.HBM),
        scratch_shapes=COMM_SCRATCH,
    )(src)


def _call(body, *, name, grid, out_shape, in_specs, out_specs, args, scratch_shapes=(), sem, comm=None, edge=None):
    out_shape, out_specs = tuple(out_shape), tuple(out_specs)
    if comm is None:
        return pl.pallas_call(body, name=name, grid=grid, out_shape=out_shape, in_specs=list(in_specs),
                              out_specs=out_specs, scratch_shapes=list(scratch_shapes),
                              compiler_params=_params(sem))(*args)
    src, gather = comm
    n_in, n_out, n_scr = len(args), len(out_shape), len(scratch_shapes)
    blk = tuple(src.shape) if gather else tuple(src.shape[1:])
    hbm = pl.BlockSpec(memory_space=pltpu.HBM)

    def carrying(*refs):
        ins, src_ref = refs[:n_in], refs[n_in]
        outs, dst_ref = refs[n_in + 1:n_in + 1 + n_out], refs[n_in + 1 + n_out]
        own = refs[n_in + 2 + n_out:n_in + 2 + n_out + n_scr]
        comm_refs = (src_ref, dst_ref) + tuple(refs[n_in + 2 + n_out + n_scr:])
        first, last = edge()

        @pl.when(first)
        def _():
            _direct_start(*comm_refs, gather=gather)

        body(*ins, *outs, *own)

        @pl.when(last)
        def _():
            _direct_wait(*comm_refs, gather=gather)

    return pl.pallas_call(
        carrying, name=name, grid=grid,
        out_shape=out_shape + (jax.ShapeDtypeStruct((N_DEV,) + blk, src.dtype),),
        in_specs=list(in_specs) + [hbm], out_specs=out_specs + (hbm,),
        scratch_shapes=list(scratch_shapes) + COMM_SCRATCH,
        compiler_params=_params(("arbitrary",) * len(grid)),
    )(*args, src)


def _gather(src, *, name):
    def body(src_ref, out_ref, send_sems, recv_sems, local_sem):
        x, y, c = lax.axis_index("x"), lax.axis_index("y"), lax.axis_index("c")
        sibling = (x, y, 1 - c)
        chips = [(1 - x, y), (x, 1 - y), (1 - x, 1 - y)]

        def rows(px, py, pc):
            return out_ref.at[4 * px + 2 * py + pc]

        def copy(k, block, to, from_src=False):
            return pltpu.make_async_remote_copy(
                src_ref=src_ref if from_src else rows(*block), dst_ref=rows(*block), send_sem=send_sems.at[k],
                recv_sem=recv_sems.at[k], device_id=to, device_id_type=MESH)

        me = (x, y, c)
        mine = pltpu.make_async_copy(src_ref, rows(*me), local_sem)
        mine.start()
        first = [copy(0, me, sibling, from_src=True)]
        first += [copy(1 + j, me, (*chip, c), from_src=True) for j, chip in enumerate(chips)]
        for cp in first:
            cp.start()
        passed = [copy(4 + j, (*chip, c), sibling) for j, chip in enumerate(chips)]
        for j, chip in enumerate(chips):
            copy(1 + j, (*chip, c), me).wait_recv()
            passed[j].start()
        copy(0, sibling, me).wait_recv()
        for j, chip in enumerate(chips):
            copy(4 + j, (*chip, 1 - c), me).wait_recv()
        for cp in first + passed:
            cp.wait_send()
        mine.wait()

    return pl.pallas_call(
        body, name=name,
        out_shape=jax.ShapeDtypeStruct((N_DEV,) + tuple(src.shape), src.dtype),
        in_specs=[pl.BlockSpec(memory_space=pltpu.HBM)],
        out_specs=pl.BlockSpec(memory_space=pltpu.HBM),
        scratch_shapes=[pltpu.SemaphoreType.DMA((7,)), pltpu.SemaphoreType.DMA((7,)), pltpu.SemaphoreType.DMA],
    )(src)


def _mm(a, b, *, tm, tn, tk, ta=False, tb=False, out_dtype=F32, name):
    b_list = list(b) if isinstance(b, (list, tuple)) else [b]
    assert not (tb and len(b_list) > 1)
    m, kdim = (a.shape[1], a.shape[0]) if ta else a.shape
    joff = [0]
    for piece in b_list:
        joff.append(joff[-1] + (piece.shape[0] if tb else piece.shape[1]) // tn)
    nk, n = kdim // tk, joff[-1] * tn
    dims = (((0 if ta else 1,), (1 if tb else 0,)), ((), ()))

    def mine(j, p):
        return (j >= joff[p]) & (j < joff[p + 1])

    def body(*refs):
        a_ref, b_refs, o_ref = refs[0], refs[1:1 + len(b_list)], refs[1 + len(b_list)]
        acc_ref = refs[-1] if nk > 1 else None
        j, k = pl.program_id(1), pl.program_id(2)
        for p, b_ref in enumerate(b_refs):
            def step(b_ref=b_ref):
                part = lax.dot_general(a_ref[...].astype(BF16), b_ref[...].astype(BF16), dims,
                                       preferred_element_type=F32)
                if nk == 1:
                    o_ref[...] = part.astype(out_dtype)
                else:
                    _accumulate(acc_ref, part, k == 0)

            if len(b_list) == 1:
                step()
            else:
                pl.when(mine(j, p))(step)
        if nk > 1:
            @pl.when(k == nk - 1)
            def _():
                o_ref[...] = acc_ref[...].astype(out_dtype)

    def b_spec(p):
        jj = lambda j: jnp.clip(j - joff[p], 0, joff[p + 1] - joff[p] - 1)
        kk = (lambda j, k: k) if len(b_list) == 1 else (lambda j, k: jnp.where(mine(j, p), k, 0))
        return (pl.BlockSpec((tn, tk), lambda i, j, k: (jj(j), kk(j, k))) if tb
                else pl.BlockSpec((tk, tn), lambda i, j, k: (kk(j, k), jj(j))))

    a_spec = pl.BlockSpec((tk, tm), lambda i, j, k: (k, i)) if ta else pl.BlockSpec((tm, tk), lambda i, j, k: (i, k))
    return pl.pallas_call(
        body, name=name, grid=(m // tm, n // tn, nk),
        out_shape=jax.ShapeDtypeStruct((m, n), out_dtype),
        in_specs=[a_spec] + [b_spec(p) for p in range(len(b_list))],
        out_specs=pl.BlockSpec((tm, tn), lambda i, j, k: (i, j)),
        scratch_shapes=[pltpu.VMEM((tm, tn), F32)] if nk > 1 else [],
        compiler_params=_params(("parallel", "parallel", "arbitrary")),
    )(a, *b_list)


def _dx_from_pieces(pieces, w, add, *, tm, ln=None, name):
    s = pieces[0].shape[0]
    widths = [p.shape[1] for p in pieces]

    def body(*refs):
        p_refs, (w_ref, add_ref) = refs[:len(pieces)], refs[len(pieces):len(pieces) + 2]
        rest = refs[len(pieces) + 2:]
        pg = jnp.zeros((1, D_MODEL), F32)
        pb = jnp.zeros((1, D_MODEL), F32)
        for rs in _row_parts(tm):
            r = ALPHA * add_ref[rs, :]
            off = 0
            for p_ref, width in zip(p_refs, widths):
                r = r + lax.dot_general(p_ref[rs, :], w_ref[:, off:off + width], (((1,), (1,)), ((), ())),
                                        preferred_element_type=F32)
                off += width
            if ln is None:
                rest[0][rs, :] = r
            else:
                dy, qg, qb = _ln_bwd_rows(r, rest[0][rs, :], rest[1][...])
                rest[2][rs, :] = dy
                rest[3][rs, :] = dy.astype(BF16)
                pg, pb = pg + qg, pb + qb
        if ln is not None:
            _accumulate(rest[4], pg, pl.program_id(0) == 0)
            _accumulate(rest[5], pb, pl.program_id(0) == 0)

    row = lambda i: (i, 0)
    tile = pl.BlockSpec((tm, D_MODEL), row)
    vec = pl.BlockSpec((1, D_MODEL), lambda i: (0, 0))
    in_specs = ([pl.BlockSpec((tm, width), row) for width in widths]
                + [pl.BlockSpec((D_MODEL, sum(widths)), lambda i: (0, 0)), tile])
    args = list(pieces) + [w, add]
    out_shape, out_specs = jax.ShapeDtypeStruct((s, D_MODEL), F32), tile
    if ln is not None:
        in_specs += [tile, vec]
        args += list(ln)
        out_shape = (jax.ShapeDtypeStruct((s, D_MODEL), F32), jax.ShapeDtypeStruct((s, D_MODEL), BF16),
                     jax.ShapeDtypeStruct((1, D_MODEL), F32), jax.ShapeDtypeStruct((1, D_MODEL), F32))
        out_specs = (tile, tile, vec, vec)
    return pl.pallas_call(
        body, name=name, grid=(s // tm,), out_shape=out_shape, in_specs=in_specs, out_specs=out_specs,
        compiler_params=_params(("arbitrary",) if ln is not None else ("parallel",)),
    )(*args)


def _ln_bwd_rows(do, y, g):
    mu = jnp.mean(y, axis=-1, keepdims=True)
    yc = y - mu
    var = jnp.mean(yc * yc, axis=-1, keepdims=True)
    rstd = lax.rsqrt(var + LN_EPS)
    xhat = yc * rstd
    dxhat = do * g
    dy = rstd * (dxhat - jnp.mean(dxhat, axis=-1, keepdims=True) - xhat * jnp.mean(dxhat * xhat, axis=-1, keepdims=True))
    return dy, jnp.sum(do * xhat, axis=0, keepdims=True), jnp.sum(do, axis=0, keepdims=True)


def _accumulate(ref, val, first):
    @pl.when(first)
    def _():
        ref[...] = val

    @pl.when(jnp.logical_not(first))
    def _():
        ref[...] += val


def _layer_norm_rows(y, g, b):
    mu = jnp.mean(y, axis=-1, keepdims=True)
    yc = y - mu
    var = jnp.mean(yc * yc, axis=-1, keepdims=True)
    return yc * lax.rsqrt(var + LN_EPS) * g + b


def _proj_ln(a, w, res, g, b, *, tm, name):
    s, kdim = a.shape

    def body(a_ref, w_ref, res_ref, g_ref, b_ref, pre_ref, obf_ref):
        for rs in _row_parts(tm):
            h = jnp.dot(a_ref[rs, :], w_ref[...], preferred_element_type=F32)
            pre = ALPHA * res_ref[rs, :] + h
            pre_ref[rs, :] = pre
            obf_ref[rs, :] = _layer_norm_rows(pre, g_ref[...], b_ref[...]).astype(BF16)

    row = lambda i: (i, 0)
    fix = lambda i: (0, 0)
    return pl.pallas_call(
        body, name=name, grid=(s // tm,),
        out_shape=(jax.ShapeDtypeStruct((s, D_MODEL), F32), jax.ShapeDtypeStruct((s, D_MODEL), BF16)),
        in_specs=[pl.BlockSpec((tm, kdim), row), pl.BlockSpec((kdim, D_MODEL), fix), pl.BlockSpec((tm, D_MODEL), row),
                  pl.BlockSpec((1, D_MODEL), fix), pl.BlockSpec((1, D_MODEL), fix)],
        out_specs=(pl.BlockSpec((tm, D_MODEL), row),) * 2,
        compiler_params=_params(("parallel",)),
    )(a, w, res, g, b)


def _row_parts(tm):
    sub = min(tm, SUB_ROWS)
    return [pl.ds(r * sub, sub) for r in range(tm // sub)]


def _ffn_in(xbf, w, *, tm, tn, name):
    s = xbf.shape[0]
    nj = D_FF // tn

    def body(x_ref, wg_ref, wu_ref, g_ref, u_ref, act_ref):
        for rs in _row_parts(tm):
            xv = x_ref[rs, :]
            gg = jnp.dot(xv, wg_ref[...], preferred_element_type=F32)
            uu = jnp.dot(xv, wu_ref[...], preferred_element_type=F32)
            g_ref[rs, :] = gg.astype(BF16)
            u_ref[rs, :] = uu.astype(BF16)
            act_ref[rs, :] = (gg * _sigmoid(gg) * uu).astype(BF16)

    out = jax.ShapeDtypeStruct((s, D_FF), BF16)
    tile = pl.BlockSpec((tm, tn), lambda j, i: (i, j))
    return pl.pallas_call(
        body, name=name, grid=(nj, s // tm),
        out_shape=(out, out, out),
        in_specs=[pl.BlockSpec((tm, D_MODEL), lambda j, i: (i, 0)), pl.BlockSpec((D_MODEL, tn), lambda j, i: (0, j)),
                  pl.BlockSpec((D_MODEL, tn), lambda j, i: (0, j + nj))],
        out_specs=(tile, tile, tile),
        compiler_params=_params(("parallel", "parallel")),
    )(xbf, w, w)


def _ffn_bwd_act(dybf, w_out, g, u, *, tm, tn, name):
    s = dybf.shape[0]

    def body(dy_ref, w_ref, g_ref, u_ref, dg_ref, du_ref):
        for rs in _row_parts(tm):
            dact = lax.dot_general(dy_ref[rs, :], w_ref[...], (((1,), (1,)), ((), ())), preferred_element_type=F32)
            gg = g_ref[rs, :].astype(F32)
            uu = u_ref[rs, :].astype(F32)
            sg = _sigmoid(gg)
            dg_ref[rs, :] = (dact * uu * sg * (1.0 + gg * (1.0 - sg))).astype(BF16)
            du_ref[rs, :] = (dact * gg * sg).astype(BF16)

    out = jax.ShapeDtypeStruct((s, D_FF), BF16)
    tile = pl.BlockSpec((tm, tn), lambda j, i: (i, j))
    return pl.pallas_call(
        body, name=name, grid=(D_FF // tn, s // tm),
        out_shape=(out, out),
        in_specs=[pl.BlockSpec((tm, D_MODEL), lambda j, i: (i, 0)), pl.BlockSpec((tn, D_MODEL), lambda j, i: (j, 0)),
                  tile, tile],
        out_specs=(tile, tile),
        compiler_params=_params(("parallel", "parallel")),
    )(dybf, w_out, g, u)


def _ffn_out_ple(act, w_out, res_pre, res_g, res_b, g, b, p, w_gate, w_proj, *, tm, name):
    s = act.shape[0]

    def body(a_ref, w_ref, res_ref, rg_ref, rb_ref, g_ref, b_ref, p_ref, wg_ref, wp_ref, pre_ref, x2bf_ref, o_ref,
             obf_ref):
        for rs in _row_parts(tm):
            res = _layer_norm_rows(res_ref[rs, :], rg_ref[...], rb_ref[...])
            pre = ALPHA * res + jnp.dot(a_ref[rs, :], w_ref[...], preferred_element_type=F32)
            x2 = _layer_norm_rows(pre, g_ref[...], b_ref[...])
            x2bf = x2.astype(BF16)
            pre_ref[rs, :] = pre
            x2bf_ref[rs, :] = x2bf
            gate = jnp.dot(x2bf, wg_ref[...], preferred_element_type=F32)
            pp = jnp.dot(p_ref[rs, :].astype(BF16), wp_ref[...], preferred_element_type=F32)
            out = x2 + _sigmoid(gate) * pp
            o_ref[rs, :] = out
            obf_ref[rs, :] = out.astype(BF16)

    row = lambda i: (i, 0)
    fix = lambda i: (0, 0)
    tile = pl.BlockSpec((tm, D_MODEL), row)
    vec = pl.BlockSpec((1, D_MODEL), fix)
    act_t = lambda dt: jax.ShapeDtypeStruct((s, D_MODEL), dt)
    return pl.pallas_call(
        body, name=name, grid=(s // tm,),
        out_shape=(act_t(F32), act_t(BF16), act_t(F32), act_t(BF16)),
        in_specs=[pl.BlockSpec((tm, D_FF), row), pl.BlockSpec((D_FF, D_MODEL), fix), tile, vec, vec, vec, vec,
                  pl.BlockSpec((tm, PLE_DIM), row), pl.BlockSpec((D_MODEL, D_MODEL), fix),
                  pl.BlockSpec((PLE_DIM, D_MODEL), fix)],
        out_specs=(tile, tile, tile, tile),
        compiler_params=_params(("parallel",)),
    )(act, w_out, res_pre, res_g, res_b, g, b, p, w_gate, w_proj)


def _ple_ln_bwd(dx3, x2bf, p, w_gate, w_proj, pre, g, *, tm, name):
    s = dx3.shape[0]

    def body(d_ref, xbf_ref, p_ref, wg_ref, wp_ref, pre_ref, g_ref, da_ref, dpp_ref, dy_ref, dybf_ref, dg_ref, db_ref):
        pg = jnp.zeros((1, D_MODEL), F32)
        pb = jnp.zeros((1, D_MODEL), F32)
        for rs in _row_parts(tm):
            d = d_ref[rs, :]
            a = jnp.dot(xbf_ref[rs, :], wg_ref[...], preferred_element_type=F32)
            pp = jnp.dot(p_ref[rs, :].astype(BF16), wp_ref[...], preferred_element_type=F32)
            sg = _sigmoid(a)
            da = (d * pp * sg * (1.0 - sg)).astype(BF16)
            da_ref[rs, :] = da
            dpp_ref[rs, :] = (d * sg).astype(BF16)
            dx2 = d + lax.dot_general(da, wg_ref[...], (((1,), (1,)), ((), ())), preferred_element_type=F32)
            dy, qg, qb = _ln_bwd_rows(dx2, pre_ref[rs, :], g_ref[...])
            dy_ref[rs, :] = dy
            dybf_ref[rs, :] = dy.astype(BF16)
            pg, pb = pg + qg, pb + qb
        _accumulate(dg_ref, pg, pl.program_id(0) == 0)
        _accumulate(db_ref, pb, pl.program_id(0) == 0)

    row = lambda i: (i, 0)
    fix = lambda i: (0, 0)
    tile = pl.BlockSpec((tm, D_MODEL), row)
    vec = pl.BlockSpec((1, D_MODEL), fix)
    act = lambda dt: jax.ShapeDtypeStruct((s, D_MODEL), dt)
    return pl.pallas_call(
        body, name=name, grid=(s // tm,),
        out_shape=(act(BF16), act(BF16), act(F32), act(BF16), jax.ShapeDtypeStruct((1, D_MODEL), F32),
                   jax.ShapeDtypeStruct((1, D_MODEL), F32)),
        in_specs=[tile, tile, pl.BlockSpec((tm, PLE_DIM), row), pl.BlockSpec((D_MODEL, D_MODEL), fix),
                  pl.BlockSpec((PLE_DIM, D_MODEL), fix), tile, vec],
        out_specs=(tile, tile, tile, tile, vec, vec),
        compiler_params=_params(("arbitrary",)),
    )(dx3, x2bf, p, w_gate, w_proj, pre, g)


def _loss_head(y, target, *, tm, name):
    s = y.shape[0]

    def body(y_ref, t_ref, dy_ref, loss_ref, acc_ref):
        err = y_ref[...] - t_ref[...]
        dy_ref[...] = err * (1.0 / D_MODEL)
        part = jnp.sum(err * err, axis=0, keepdims=True)

        @pl.when(pl.program_id(0) == 0)
        def _():
            acc_ref[...] = part

        @pl.when(pl.program_id(0) > 0)
        def _():
            acc_ref[...] += part

        @pl.when(pl.program_id(0) == pl.num_programs(0) - 1)
        def _():
            tot = jnp.sum(acc_ref[...], axis=1, keepdims=True) * (0.5 / D_MODEL)
            loss_ref[...] = jnp.broadcast_to(tot, (8, LANES))

    row = lambda i: (i, 0)
    return pl.pallas_call(
        body, name=name, grid=(s // tm,),
        out_shape=(jax.ShapeDtypeStruct((s, D_MODEL), F32), jax.ShapeDtypeStruct((8, LANES), F32)),
        in_specs=[pl.BlockSpec((tm, D_MODEL), row), pl.BlockSpec((tm, D_MODEL), row)],
        out_specs=(pl.BlockSpec((tm, D_MODEL), row), pl.BlockSpec((8, LANES), lambda i: (0, 0))),
        scratch_shapes=[pltpu.VMEM((1, D_MODEL), F32)],
        compiler_params=_params(("arbitrary",)),
    )(y, target)


def _rope_tables(s):
    inv = ROPE_THETA ** (-jnp.arange(0, ROPE_DIM, 2, dtype=F32) / ROPE_DIM)
    ang = jnp.arange(s, dtype=F32)[:, None] * inv[None, :]
    cos, sin = jnp.cos(ang), jnp.sin(ang)
    ones = jnp.ones((s, HEAD_DIM - ROPE_DIM), F32)
    c_head = jnp.concatenate([cos, cos, ones], axis=1)
    s_head = jnp.concatenate([-sin, sin, 0.0 * ones], axis=1)
    return jnp.concatenate([c_head, c_head], axis=1), jnp.concatenate([s_head, s_head], axis=1)


def _rope(v, cos, sin):
    n = v.shape[1] // LANES
    width = v.shape[1]
    cos_w = jnp.tile(cos, (1, n)) if n > 1 else cos
    sin_w = jnp.tile(sin, (1, n)) if n > 1 else sin
    dim = lax.broadcasted_iota(jnp.int32, (1, width), 1) % HEAD_DIM
    partner = jnp.where(dim < ROPE_DIM // 2, pltpu.roll(v, width - ROPE_DIM // 2, 1), pltpu.roll(v, ROPE_DIM // 2, 1))
    return v * cos_w + partner * sin_w


def _unrope(dv, cos, sin):
    n = dv.shape[1] // LANES
    width = dv.shape[1]
    cos_w = jnp.tile(cos, (1, n)) if n > 1 else cos
    sin_w = jnp.tile(sin, (1, n)) if n > 1 else sin
    t = dv * sin_w
    dim = lax.broadcasted_iota(jnp.int32, (1, width), 1) % HEAD_DIM
    partner = jnp.where(dim < ROPE_DIM // 2, pltpu.roll(t, width - ROPE_DIM // 2, 1),
                        jnp.where(dim < ROPE_DIM, pltpu.roll(t, ROPE_DIM // 2, 1), 0.0))
    return dv * cos_w + partner


def _att_mask(i, nb):
    rows = GROUP * ATT_BLOCK
    r = lax.broadcasted_iota(jnp.int32, (rows, 3 * ATT_BLOCK), 0) % ATT_BLOCK
    cidx = lax.broadcasted_iota(jnp.int32, (rows, 3 * ATT_BLOCK), 1)
    rel = r + ATT_BLOCK - cidx
    ok = (rel <= ATT_BLOCK) & (rel >= -ATT_BLOCK)
    ok = ok & ((cidx >= ATT_BLOCK) | (i > 0)) & ((cidx < 2 * ATT_BLOCK) | (i < nb - 1))
    return ok


def _att_mask_t(i, nb):
    cols = GROUP * ATT_BLOCK
    cidx = lax.broadcasted_iota(jnp.int32, (3 * ATT_BLOCK, cols), 0)
    r = lax.broadcasted_iota(jnp.int32, (3 * ATT_BLOCK, cols), 1) % ATT_BLOCK
    rel = r + ATT_BLOCK - cidx
    ok = (rel <= ATT_BLOCK) & (rel >= -ATT_BLOCK)
    return ok & ((cidx >= ATT_BLOCK) | (i > 0)) & ((cidx < 2 * ATT_BLOCK) | (i < nb - 1))


def _sink_lanes(sink_ref, h):
    cols = GROUP * ATT_BLOCK
    grp = lax.broadcasted_iota(jnp.int32, (1, cols), 1) // ATT_BLOCK
    out = jnp.zeros((1, cols), F32)
    for gq in range(GROUP):
        out = jnp.where(grp == gq, sink_ref[GROUP * h + gq], out)
    return out


def _half_mask(half):
    lane = lax.broadcasted_iota(jnp.int32, (1, LANES), 1)
    return (lane // HEAD_DIM) == half


def _stack_q(q, h):
    parts = []
    for gq in range(GROUP):
        n = GROUP * h + gq
        grp = q[:, LANES * (n // 2):LANES * (n // 2 + 1)]
        grp = jnp.where(_half_mask(n % 2), grp, 0.0)
        if n % 2 != h % 2:
            grp = pltpu.roll(grp, HEAD_DIM, 1)
        parts.append(grp)
    return jnp.concatenate(parts, axis=0)


def _unstack_q(stacked, h, acc):
    for gq in range(GROUP):
        n = GROUP * h + gq
        grp = stacked[ATT_BLOCK * gq:ATT_BLOCK * (gq + 1), :]
        grp = jnp.where(_half_mask(h % 2), grp, 0.0)
        if n % 2 != h % 2:
            grp = pltpu.roll(grp, HEAD_DIM, 1)
        acc[n // 2] = grp if acc[n // 2] is None else acc[n // 2] + grp
    return acc


def _sink_rows(sink_ref, h):
    rows = GROUP * ATT_BLOCK
    grp = lax.broadcasted_iota(jnp.int32, (rows, 1), 0) // ATT_BLOCK
    out = jnp.zeros((rows, 1), F32)
    for gq in range(GROUP):
        out = jnp.where(grp == gq, sink_ref[GROUP * h + gq], out)
    return out


def _att_probs(qs, kh, sink, valid):
    s = lax.dot_general(qs, kh, (((1,), (1,)), ((), ())), preferred_element_type=F32)
    s = jnp.where(valid, s, NEG)
    m = jnp.maximum(jnp.max(s, axis=-1, keepdims=True), sink)
    p = jnp.exp(s - m)
    es = jnp.exp(sink - m)
    den = jnp.sum(p, axis=-1, keepdims=True) + es
    inv = 1.0 / den
    return p * inv, es * inv


ATT_STEP = 2


def _att_specs(nb):
    rows = ATT_STEP * ATT_BLOCK
    prev = lambda i: (jnp.maximum(ATT_STEP * i - 1, 0), 0)
    cur = lambda i: (i, 0)
    nxt = lambda i: (jnp.minimum(ATT_STEP * (i + 1), nb - 1), 0)
    kv = lambda f: (lambda i: (f(i)[0], 2))
    shapes = ((rows, cur), (ATT_BLOCK, prev), (rows, cur), (ATT_BLOCK, nxt))
    tab = [pl.BlockSpec((r, LANES), f) for r, f in shapes]
    z = [pl.BlockSpec((rows, D_MODEL), cur)] + [pl.BlockSpec((r, 2 * KV_DIM), kv(f)) for r, f in shapes[1:]]
    return z, tab


def _att_load(zq_ref, kp_ref, kc_ref, kn_ref, cq_ref, sq_ref, cp_ref, sp_ref, cc_ref, sc_ref, cn_ref, sn_ref):
    q = (_rope(zq_ref[...], cq_ref[...], sq_ref[...]) * (HEAD_DIM ** -0.5))
    ks, vs = [], []
    for ref, c_ref, s_ref in ((kp_ref, cp_ref, sp_ref), (kc_ref, cc_ref, sc_ref), (kn_ref, cn_ref, sn_ref)):
        kvb = ref[...]
        ks.append(_rope(kvb[:, :KV_DIM], c_ref[...], s_ref[...]))
        vs.append(kvb[:, KV_DIM:])
    return q, jnp.concatenate(ks, axis=0).astype(BF16), jnp.concatenate(vs, axis=0).astype(BF16)


def _att_rows(sub):
    return (slice(ATT_BLOCK * sub, ATT_BLOCK * (sub + 1)), slice(ATT_BLOCK * sub, ATT_BLOCK * (sub + 3)))


def _att_fwd(z, sink, cos, sin, *, comm=None, name):
    s = z.shape[0]
    nb = s // ATT_BLOCK
    steps = nb // ATT_STEP

    def body(zq_ref, kp_ref, kc_ref, kn_ref, cq_ref, cp_ref, cc_ref, cn_ref, sq_ref, sp_ref, sc_ref, sn_ref, sink_ref,
             o_ref):
        i = pl.program_id(0)
        q, k, v = _att_load(zq_ref, kp_ref, kc_ref, kn_ref, cq_ref, sq_ref, cp_ref, sp_ref, cc_ref, sc_ref, cn_ref, sn_ref)
        for sub in range(ATT_STEP):
            qrows, krows = _att_rows(sub)
            valid = _att_mask(ATT_STEP * i + sub, nb)
            acc = [None] * (N_Q_HEADS // 2)
            for h in range(N_KV_HEADS):
                lanes = slice(LANES * (h // 2), LANES * (h // 2 + 1))
                qs = _stack_q(q[qrows], h).astype(BF16)
                prob, _ = _att_probs(qs, k[krows, lanes], _sink_rows(sink_ref, h), valid)
                oh = jnp.dot(prob.astype(BF16), v[krows, lanes], preferred_element_type=F32)
                acc = _unstack_q(oh, h, acc)
            o_ref[qrows, :] = jnp.concatenate(acc, axis=1).astype(BF16)

    zspecs, tab = _att_specs(nb)
    return _call(
        body, name=name, grid=(steps,),
        out_shape=(jax.ShapeDtypeStruct((s, D_MODEL), BF16),),
        in_specs=zspecs + tab + tab + [pl.BlockSpec(memory_space=pltpu.SMEM)],
        out_specs=(pl.BlockSpec((ATT_STEP * ATT_BLOCK, D_MODEL), lambda i: (i, 0)),),
        args=(z, z, z, z, cos, cos, cos, cos, sin, sin, sin, sin, sink), sem=("parallel",), comm=comm,
        edge=lambda: (pl.program_id(0) == 0, pl.program_id(0) == steps - 1))


def _att_bwd(z, do, sink, cos, sin, *, comm=None, name):
    s = z.shape[0]
    nb = s // ATT_BLOCK
    steps = nb // ATT_STEP

    def body(zq_ref, kp_ref, kc_ref, kn_ref, cq_ref, cp_ref, cc_ref, cn_ref, sq_ref, sp_ref, sc_ref, sn_ref, sink_ref,
             do_ref, dq_ref, part_ref, dsink_ref):
        i = pl.program_id(0)
        q, k, v = _att_load(zq_ref, kp_ref, kc_ref, kn_ref, cq_ref, sq_ref, cp_ref, sp_ref, cc_ref, sc_ref, cn_ref, sn_ref)
        dsink = None
        nt = (((1,), (1,)), ((), ()))
        for sub in range(ATT_STEP):
            qrows, krows = _att_rows(sub)
            valid = _att_mask_t(ATT_STEP * i + sub, nb)
            dout = do_ref[qrows, :].astype(F32)
            dq_acc = [None] * (N_Q_HEADS // 2)
            dk_acc = [None] * 2
            dv_acc = [None] * 2
            rows = []
            for h in range(N_KV_HEADS):
                grp = h // 2
                lanes = slice(LANES * grp, LANES * (grp + 1))
                kh, vh = k[krows, lanes], v[krows, lanes]
                qs = _stack_q(q[qrows], h).astype(BF16)
                dos = _stack_q(dout, h).astype(BF16)
                sink = _sink_lanes(sink_ref, h)
                sc = jnp.where(valid, lax.dot_general(kh, qs, nt, preferred_element_type=F32), NEG)
                m = jnp.maximum(jnp.max(sc, axis=0, keepdims=True), sink)
                p = jnp.exp(sc - m)
                es = jnp.exp(sink - m)
                inv = 1.0 / (jnp.sum(p, axis=0, keepdims=True) + es)
                prob = p * inv
                dprob = lax.dot_general(vh, dos, nt, preferred_element_type=F32)
                delta = jnp.sum(prob * dprob, axis=0, keepdims=True)
                dsc = (prob * (dprob - delta)).astype(BF16)
                dsk = -(es * inv) * delta
                for gq in range(GROUP):
                    tot = jnp.sum(dsk[:, ATT_BLOCK * gq:ATT_BLOCK * (gq + 1)], axis=1, keepdims=True)
                    rows.append(jnp.broadcast_to(tot, (1, LANES)))
                dqs = lax.dot_general(dsc, kh, (((0,), (0,)), ((), ())), preferred_element_type=F32)
                dq_acc = _unstack_q(dqs, h, dq_acc)
                dkh = jnp.dot(dsc, qs, preferred_element_type=F32)
                dvh = jnp.dot(prob.astype(BF16), dos, preferred_element_type=F32)
                dk_acc[grp] = dkh if dk_acc[grp] is None else dk_acc[grp] + dkh
                dv_acc[grp] = dvh if dv_acc[grp] is None else dv_acc[grp] + dvh
            dq = jnp.concatenate(dq_acc, axis=1) * (HEAD_DIM ** -0.5)
            dq_ref[qrows, :] = _unrope(dq, cq_ref[qrows, :], sq_ref[qrows, :]).astype(BF16)
            part = jnp.concatenate(dk_acc + dv_acc, axis=1)
            for wdw in range(3):
                part_ref[sub, wdw] = part[ATT_BLOCK * wdw:ATT_BLOCK * (wdw + 1), :]
            mine = jnp.concatenate(rows, axis=0)
            dsink = mine if dsink is None else dsink + mine
        _accumulate(dsink_ref, dsink, i == 0)

    zspecs, tab = _att_specs(nb)
    group = pl.BlockSpec((ATT_STEP * ATT_BLOCK, D_MODEL), lambda i: (i, 0))
    return _call(
        body, name=name, grid=(steps,),
        out_shape=(jax.ShapeDtypeStruct((s, D_MODEL), BF16), jax.ShapeDtypeStruct((nb, 3, ATT_BLOCK, 2 * KV_DIM), F32),
                   jax.ShapeDtypeStruct((N_Q_HEADS, LANES), F32)),
        in_specs=zspecs + tab + tab + [pl.BlockSpec(memory_space=pltpu.SMEM), group],
        out_specs=(group, pl.BlockSpec((ATT_STEP, 3, ATT_BLOCK, 2 * KV_DIM), lambda i: (i, 0, 0, 0)),
                   pl.BlockSpec((N_Q_HEADS, LANES), lambda i: (0, 0))),
        args=(z, z, z, z, cos, cos, cos, cos, sin, sin, sin, sin, sink, do), sem=("arbitrary",), comm=comm,
        edge=lambda: (pl.program_id(0) == 0, pl.program_id(0) == steps - 1))


def _att_bwd_kv(part, cos, sin, *, name):
    nb = part.shape[0]

    def body(pn_ref, pc_ref, pp_ref, c_ref, s_ref, o_ref):
        j = pl.program_id(0)
        tot = pc_ref[...]
        tot = tot + jnp.where(j < nb - 1, pn_ref[...], 0.0)
        tot = tot + jnp.where(j > 0, pp_ref[...], 0.0)
        dk = _unrope(tot[:, :KV_DIM], c_ref[...], s_ref[...])
        o_ref[...] = jnp.concatenate([dk, tot[:, KV_DIM:]], axis=1).astype(BF16)

    blk = (None, None, ATT_BLOCK, 2 * KV_DIM)
    return pl.pallas_call(
        body, name=name, grid=(nb,),
        out_shape=jax.ShapeDtypeStruct((nb * ATT_BLOCK, 2 * KV_DIM), BF16),
        in_specs=[pl.BlockSpec(blk, lambda j: (jnp.minimum(j + 1, nb - 1), 0, 0, 0)),
                  pl.BlockSpec(blk, lambda j: (j, 1, 0, 0)),
                  pl.BlockSpec(blk, lambda j: (jnp.maximum(j - 1, 0), 2, 0, 0)),
                  pl.BlockSpec((ATT_BLOCK, LANES), lambda j: (j, 0)), pl.BlockSpec((ATT_BLOCK, LANES), lambda j: (j, 0))],
        out_specs=pl.BlockSpec((ATT_BLOCK, 2 * KV_DIM), lambda j: (j, 0)),
        compiler_params=_params(("parallel",)),
    )(part, part, part, cos, sin)


def _bdot(a, b, dims):
    return lax.dot_general(a.astype(BF16), b.astype(BF16), (dims, ((), ())), preferred_element_type=F32)


@jax.custom_vjp
def _dot_nn(a, b):
    return _bdot(a, b, ((1,), (0,)))


@jax.custom_vjp
def _dot_nt(a, b):
    return _bdot(a, b, ((1,), (1,)))


@jax.custom_vjp
def _dot_tn(a, b):
    return _bdot(a, b, ((0,), (0,)))


_dot_nn.defvjp(lambda a, b: (_dot_nn(a, b), (a, b)), lambda r, d: (_dot_nt(d, r[1]), _dot_tn(r[0], d)))
_dot_nt.defvjp(lambda a, b: (_dot_nt(a, b), (a, b)), lambda r, d: (_dot_nn(d, r[1]), _dot_tn(d, r[0])))
_dot_tn.defvjp(lambda a, b: (_dot_tn(a, b), (a, b)), lambda r, d: (_dot_nt(r[1], d), _dot_nn(r[0], d)))


def _running_sum(v, up):
    n = v.shape[0]
    rows = lax.broadcasted_iota(jnp.int32, v.shape, 0)
    sh = 1
    while sh < n:
        if up:
            v = v + jnp.where(rows < n - sh, pltpu.roll(v, n - sh, 0), 0.0)
        else:
            v = v + jnp.where(rows >= sh, pltpu.roll(v, sh, 0), 0.0)
        sh *= 2
    return v


@jax.custom_vjp
def _sum_down(v):
    return _running_sum(v, False)


@jax.custom_vjp
def _sum_up(v):
    return _running_sum(v, True)


_sum_down.defvjp(lambda v: (_running_sum(v, False), None), lambda _, d: (_sum_up(d),))
_sum_up.defvjp(lambda v: (_running_sum(v, True), None), lambda _, d: (_sum_down(d),))

N_SUB = HG_CHUNK // HG_SUB


def _fold_blocks(v):
    out = v[:HG_CHUNK]
    for i in range(1, N_SUB):
        out = out + v[HG_CHUNK * i:HG_CHUNK * (i + 1)]
    return out


@jax.custom_vjp
def _fold(v):
    return _fold_blocks(v)


_fold.defvjp(lambda v: (_fold_blocks(v), None), lambda _, d: (jnp.concatenate([d] * N_SUB, axis=0),))


def _hg_consts(rev):
    c, sub = HG_CHUNK, HG_SUB
    rowpos = lax.broadcasted_iota(jnp.int32, (c, HG_DIM), 0)
    rr = lax.broadcasted_iota(jnp.int32, (N_SUB * c, c), 0)
    key = lax.broadcasted_iota(jnp.int32, (N_SUB * c, c), 1)
    blk, qry = rr // c, rr % c
    if rev:
        rowpos, qry, key = c - 1 - rowpos, c - 1 - qry, c - 1 - key
    keep = (key // sub == blk) & (key <= qry)
    return keep, rowpos


def _pick(b, rowpos, t):
    return jnp.sum(jnp.where(rowpos == t, b, 0.0), axis=0, keepdims=True)


def _hg_local(zq, zf, zv, lbv, consts, dots):
    dot_nn, dot_nt, dot_tn, cum, fold = dots
    keep, rowpos = consts
    sig = _sigmoid(zf)
    f = lbv + (1.0 - lbv) * sig
    g = jnp.log(f)
    k = (1.0 - lbv) * (1.0 - sig)
    q = zq * _sigmoid(zq)
    b = cum(g)
    ends = [_pick(b, rowpos, (j + 1) * HG_SUB - 1) for j in range(N_SUB)]
    b_last = ends[-1]
    b_end = b_last
    for j in range(N_SUB - 1):
        b_end = jnp.where(rowpos // HG_SUB == j, ends[j], b_end)
    kc = k * jnp.exp(b_end - b)
    qbs = [q * jnp.exp(jnp.where(rowpos >= j * HG_SUB, b - ends[j], 0.0)) for j in range(N_SUB)]
    scores = fold(jnp.where(keep, dot_nt(jnp.concatenate(qbs, axis=0), kc), 0.0))
    return dot_nn(scores, zv), q * jnp.exp(b), k * jnp.exp(b_last - b), jnp.exp(b_last)


def _hg_chunk(zq, zf, zv, lbv, st, consts, dots):
    intra, qs, kd, dec = _hg_local(zq, zf, zv, lbv, consts, dots)
    return intra + dots[1](qs, st), dec * st + dots[2](zv, kd)


def _hg_dots(diff, rev):
    if diff:
        return _dot_nn, _dot_nt, _dot_tn, (_sum_up if rev else _sum_down), _fold
    return (lambda a, b: _bdot(a, b, ((1,), (0,))), lambda a, b: _bdot(a, b, ((1,), (1,))),
            lambda a, b: _bdot(a, b, ((0,), (0,))), lambda v: _running_sum(v, rev), _fold_blocks)


def _hg_specs(ts, nch, trow):
    tile = pl.BlockSpec((ts, HG_DIM), lambda h, t: (trow(t), h))
    mats = pl.BlockSpec((None, nch, HG_DIM, HG_DIM), lambda h, t: (h, trow(t), 0, 0))
    vecs = pl.BlockSpec((None, nch, 1, HG_DIM), lambda h, t: (h, trow(t), 0, 0))
    return tile, mats, vecs


def _time_order(nch, rev):
    return range(nch - 1, -1, -1) if rev else range(nch)


def _chunk_rows(c):
    return pl.ds(c * HG_CHUNK, HG_CHUNK)


def _hg_edge(nt):
    h, t = pl.program_id(0), pl.program_id(1)
    return (h == 0) & (t == 0), (h == HG_HEADS - 1) & (t == nt - 1)


def _hg_fwd(z, lb, *, rev, ts, comm=None, name):
    s = z.shape[0]
    nt = s // ts
    nch = ts // HG_CHUNK
    fcol = HG_HEADS * (2 if rev else 1)

    def body(zq_ref, zf_ref, zv_ref, lb_ref, o_ref, st_ref, qs_ref, dec_ref, state_ref):
        @pl.when(pl.program_id(1) == 0)
        def _():
            state_ref[...] = jnp.zeros_like(state_ref)

        consts = _hg_consts(rev)
        dots = _hg_dots(False, rev)
        lbv = lb_ref[...]
        local = {}
        for c in range(nch):
            rows = _chunk_rows(c)
            zv = zv_ref[rows, :]
            intra, qs, kd, dec = _hg_local(zq_ref[rows, :], zf_ref[rows, :], zv, lbv, consts, dots)
            qs = qs.astype(BF16)
            qs_ref[rows, :] = qs
            dec_ref[c] = dec
            local[c] = (intra, qs, dec, dots[2](zv, kd))
        st = state_ref[...]
        for c in _time_order(nch, rev):
            intra, qs, dec, upd = local[c]
            st_ref[c] = st.astype(BF16)
            o_ref[_chunk_rows(c), :] = intra + _bdot(qs, st, ((1,), (1,)))
            st = dec * st + upd
        state_ref[...] = st

    trow = (lambda t: nt - 1 - t) if rev else (lambda t: t)
    col = lambda off: pl.BlockSpec((ts, HG_DIM), lambda h, t: (trow(t), off + h))
    tile, mats, vecs = _hg_specs(ts, nch, trow)
    nchunks = s // HG_CHUNK
    return _call(
        body, name=name, grid=(HG_HEADS, nt),
        out_shape=(jax.ShapeDtypeStruct((s, D_MODEL), F32),
                   jax.ShapeDtypeStruct((HG_HEADS, nchunks, HG_DIM, HG_DIM), BF16),
                   jax.ShapeDtypeStruct((s, D_MODEL), BF16),
                   jax.ShapeDtypeStruct((HG_HEADS, nchunks, 1, HG_DIM), F32)),
        in_specs=[col(0), col(fcol), col(3 * HG_HEADS), pl.BlockSpec((None, 1, HG_DIM), lambda h, t: (h, 0, 0))],
        out_specs=(tile, mats, tile, vecs), args=(z, z, z, lb),
        scratch_shapes=[pltpu.VMEM((HG_DIM, HG_DIM), F32)], sem=("parallel", "arbitrary"), comm=comm,
        edge=lambda: _hg_edge(nt))


def _hg_bwd(z, lb, states, qs, dec, dout, addq, addv, *, rev, ts, comm=None, name):
    s = z.shape[0]
    nt = s // ts
    nch = ts // HG_CHUNK
    fcol = HG_HEADS * (2 if rev else 1)
    has_add = addq is not None

    def body(*refs):
        zq_ref, zf_ref, zv_ref, lb_ref, st_ref, qs_ref, dec_ref, do_ref = refs[:8]
        aq_ref, av_ref = (refs[8], refs[9]) if has_add else (None, None)
        dq_ref, df_ref, dv_ref, dlb_ref, grad_ref = refs[-5:]

        @pl.when(pl.program_id(1) == 0)
        def _():
            grad_ref[...] = jnp.zeros_like(grad_ref)

        consts = _hg_consts(rev)
        dots = _hg_dots(True, rev)
        lbv = lb_ref[...]
        prods = {c: _bdot(do_ref[_chunk_rows(c), :], qs_ref[_chunk_rows(c), :], ((0,), (0,))) for c in range(nch)}
        gleave = {}
        gr = grad_ref[...]
        for c in reversed(_time_order(nch, rev)):
            gleave[c] = gr
            gr = dec_ref[c] * gr + prods[c]
        grad_ref[...] = gr
        dlb_blk = jnp.zeros((1, HG_DIM), F32)
        for c in range(nch):
            rows = _chunk_rows(c)
            fn = lambda a, b2, c2, d2, e2: _hg_chunk(a, b2, c2, d2, e2, consts, dots)
            _, pull = jax.vjp(fn, zq_ref[rows, :], zf_ref[rows, :], zv_ref[rows, :], lbv, st_ref[c].astype(F32))
            dq, df, dv, dlb, _ = pull((do_ref[rows, :], gleave[c]))
            if has_add:
                dq = dq + aq_ref[rows, :]
                dv = dv + av_ref[rows, :]
            dq_ref[rows, :] = dq.astype(dq_ref.dtype)
            df_ref[rows, :] = df.astype(BF16)
            dv_ref[rows, :] = dv.astype(dv_ref.dtype)
            dlb_blk = dlb_blk + dlb

        @pl.when(pl.program_id(1) == 0)
        def _():
            dlb_ref[...] = dlb_blk

        @pl.when(pl.program_id(1) > 0)
        def _():
            dlb_ref[...] += dlb_blk

    trow = (lambda t: t) if rev else (lambda t: nt - 1 - t)
    col = lambda off: pl.BlockSpec((ts, HG_DIM), lambda h, t: (trow(t), off + h))
    tile, mats, vecs = _hg_specs(ts, nch, trow)
    in_specs = [col(0), col(fcol), col(3 * HG_HEADS), pl.BlockSpec((None, 1, HG_DIM), lambda h, t: (h, 0, 0)),
                mats, tile, vecs, tile]
    args = [z, z, z, lb, states, qs, dec, dout]
    if has_add:
        in_specs += [tile, tile]
        args += [addq, addv]
    act = lambda dt: jax.ShapeDtypeStruct((s, D_MODEL), dt)
    sums = BF16 if has_add else F32
    return _call(
        body, name=name, grid=(HG_HEADS, nt),
        out_shape=(act(sums), act(BF16), act(sums), jax.ShapeDtypeStruct((HG_HEADS, 1, HG_DIM), F32)),
        in_specs=in_specs,
        out_specs=(tile, tile, tile, pl.BlockSpec((None, 1, HG_DIM), lambda h, t: (h, 0, 0))), args=tuple(args),
        scratch_shapes=[pltpu.VMEM((HG_DIM, HG_DIM), F32)], sem=("parallel", "arbitrary"), comm=comm,
        edge=lambda: _hg_edge(nt))


def _hg_post(of, ob, z, norm_g, *, tm, name):
    s = of.shape[0]

    def body(of_ref, ob_ref, gate_ref, ng_ref, y_ref):
        gn = ng_ref[...]
        for h in range(HG_HEADS):
            ln = slice(HG_DIM * h, HG_DIM * (h + 1))
            o = of_ref[:, ln] + ob_ref[:, ln]
            r = lax.rsqrt(jnp.mean(o * o, axis=-1, keepdims=True) + LN_EPS)
            gt = gate_ref[:, ln]
            y_ref[:, ln] = (o * r * gn * gt * _sigmoid(gt)).astype(BF16)

    row = lambda i: (i, 0)
    return pl.pallas_call(
        body, name=name, grid=(s // tm,),
        out_shape=jax.ShapeDtypeStruct((s, D_MODEL), BF16),
        in_specs=[pl.BlockSpec((tm, D_MODEL), row), pl.BlockSpec((tm, D_MODEL), row),
                  pl.BlockSpec((tm, D_MODEL), lambda i: (i, 4)), pl.BlockSpec((1, HG_DIM), lambda i: (0, 0))],
        out_specs=pl.BlockSpec((tm, D_MODEL), row),
        compiler_params=_params(("parallel",)),
    )(of, ob, z, norm_g)


def _hg_post_bwd(dy, of, ob, z, norm_g, *, tm, name):
    s = of.shape[0]

    def body(dy_ref, of_ref, ob_ref, gate_ref, ng_ref, do_ref, dgate_ref, dng_ref):
        gn = ng_ref[...]
        tot = jnp.zeros((1, HG_DIM), F32)
        for h in range(HG_HEADS):
            ln = slice(HG_DIM * h, HG_DIM * (h + 1))
            d = dy_ref[:, ln].astype(F32)
            o = of_ref[:, ln] + ob_ref[:, ln]
            r = lax.rsqrt(jnp.mean(o * o, axis=-1, keepdims=True) + LN_EPS)
            ohat = o * r
            gt = gate_ref[:, ln]
            sg = _sigmoid(gt)
            don = d * gt * sg
            dgate_ref[:, ln] = (d * ohat * gn * sg * (1.0 + gt * (1.0 - sg))).astype(BF16)
            tot = tot + jnp.sum(don * ohat, axis=0, keepdims=True)
            dohat = don * gn
            do_ref[:, ln] = r * (dohat - ohat * jnp.mean(dohat * ohat, axis=-1, keepdims=True))

        @pl.when(pl.program_id(0) == 0)
        def _():
            dng_ref[...] = tot

        @pl.when(pl.program_id(0) > 0)
        def _():
            dng_ref[...] += tot

    row = lambda i: (i, 0)
    return pl.pallas_call(
        body, name=name, grid=(s // tm,),
        out_shape=(jax.ShapeDtypeStruct((s, D_MODEL), F32), jax.ShapeDtypeStruct((s, D_MODEL), BF16),
                   jax.ShapeDtypeStruct((1, HG_DIM), F32)),
        in_specs=[pl.BlockSpec((tm, D_MODEL), row), pl.BlockSpec((tm, D_MODEL), row), pl.BlockSpec((tm, D_MODEL), row),
                  pl.BlockSpec((tm, D_MODEL), lambda i: (i, 4)), pl.BlockSpec((1, HG_DIM), lambda i: (0, 0))],
        out_specs=(pl.BlockSpec((tm, D_MODEL), row), pl.BlockSpec((tm, D_MODEL), row),
                   pl.BlockSpec((1, HG_DIM), lambda i: (0, 0))),
        compiler_params=_params(("arbitrary",)),
    )(dy, of, ob, z, norm_g)


def _lb_fwd(logits, *, name):
    w = logits.shape[1]

    def body(l_ref, o_ref):
        lg = l_ref[...]
        e = jnp.exp(lg - jnp.max(lg, axis=0, keepdims=True))
        sm = e / jnp.sum(e, axis=0, keepdims=True)
        o_ref[0:1, :] = sm[1:2]
        o_ref[1:2, :] = sm[1:2] + sm[2:3] + sm[3:4]

    return pl.pallas_call(body, name=name, out_shape=jax.ShapeDtypeStruct((2, w), F32))(logits)


def _lb_bwd(logits, dlb, *, name):
    w = logits.shape[1]

    def body(l_ref, d_ref, o_ref):
        lg = l_ref[...]
        e = jnp.exp(lg - jnp.max(lg, axis=0, keepdims=True))
        sm = e / jnp.sum(e, axis=0, keepdims=True)
        d1, d3 = d_ref[0:1, :], d_ref[1:2, :]
        dot = sm[1:2] * (d1 + d3) + (sm[2:3] + sm[3:4]) * d3
        o_ref[0:1, :] = -sm[0:1] * dot
        o_ref[1:2, :] = sm[1:2] * (d1 + d3 - dot)
        o_ref[2:3, :] = sm[2:3] * (d3 - dot)
        o_ref[3:4, :] = sm[3:4] * (d3 - dot)

    return pl.pallas_call(body, name=name, out_shape=jax.ShapeDtypeStruct((4, w), F32))(logits, dlb)


def _adamw(w, g, m, v, *, tr, g_off=0, name):
    rows = w.shape[0]
    parts = g.ndim == 3
    c1 = 1.0 / (1.0 - ADAM_B1 ** ADAM_STEP)
    c2 = 1.0 / (1.0 - ADAM_B2 ** ADAM_STEP)

    def body(w_ref, g_ref, m_ref, v_ref, go_ref, d_ref, mo_ref, vo_ref):
        if parts:
            gg = g_ref[0].astype(F32)
            for i in range(1, N_DEV):
                gg = gg + g_ref[i].astype(F32)
        else:
            gg = g_ref[...]
        mm = ADAM_B1 * m_ref[...] + (1.0 - ADAM_B1) * gg
        vv = ADAM_B2 * v_ref[...] + (1.0 - ADAM_B2) * (gg * gg)
        go_ref[...] = gg
        mo_ref[...] = mm
        vo_ref[...] = vv
        d_ref[...] = -ADAM_LR * ((mm * c1) / (jnp.sqrt(vv * c2) + ADAM_EPS) + ADAM_WD * w_ref[...])

    tile = pl.BlockSpec((tr, D_MODEL), lambda i: (i, 0))
    gspec = pl.BlockSpec((N_DEV, tr, D_MODEL), lambda i: (0, i + g_off // tr, 0)) if parts else tile
    out = jax.ShapeDtypeStruct((rows, D_MODEL), F32)
    return pl.pallas_call(
        body, name=name, grid=(rows // tr,),
        out_shape=(out, out, out, out),
        in_specs=[tile, gspec, tile, tile], out_specs=(tile, tile, tile, tile),
        compiler_params=_params(("parallel",)),
    )(w, g, m, v)


def _sum8(parts, *, name):
    def body(p_ref, o_ref):
        tot = p_ref[0]
        for i in range(1, N_DEV):
            tot = tot + p_ref[i]
        o_ref[...] = tot

    return pl.pallas_call(body, name=name, out_shape=jax.ShapeDtypeStruct(parts.shape[1:], parts.dtype))(parts)


def _layer_params(i):
    j = i // 2
    mix = [("att_w_qkv", j, 1), ("att_w_o", j, 0)] if i % 2 == 0 else [("hgrn_w_in", j, 1), ("hgrn_w_o", j, 0)]
    return mix + [("ffn_w_in", i, 1), ("ffn_w_out", i, 0), ("ple_w_gate", i, 0), ("ple_w_proj", i, 1)]


def _pack_local(tree, params):
    return jnp.concatenate([tree[n][j].reshape(-1, D_MODEL) for n, j, _ in params], axis=0)


def _unpack_local(packed, params, like):
    out, r = {}, 0
    for n, _, _ in params:
        shp = like[n].shape[1:]
        k = shp[0] * shp[1] // D_MODEL
        out[n] = packed[r:r + k].reshape(shp)
        r += k
    return out


def _unpack_gathered(gathered, i, like):
    out, r = {}, 0
    for n, _, ax in _layer_params(i):
        shp = like[n].shape[1:]
        k = shp[0] * shp[1] // D_MODEL
        t = gathered[:, r:r + k].reshape((N_DEV,) + shp)
        out[n] = (jnp.moveaxis(t, 0, 1).reshape(shp[0], N_DEV * shp[1]) if ax == 1
                  else t.reshape(N_DEV * shp[0], shp[1]))
        r += k
    return out


def _pack_full(grads, params, like):
    cols = []
    for n, _, ax in params:
        shp = like[n].shape[1:]
        t = (jnp.moveaxis(grads[n].reshape(shp[0], N_DEV, shp[1]), 1, 0) if ax == 1
             else grads[n].reshape(N_DEV, shp[0], shp[1]))
        cols.append(t.reshape(N_DEV, -1, D_MODEL).astype(BF16))
    return jnp.concatenate(cols, axis=1)


def _row_tile(rows):
    return max(t for t in range(16, 257, 16) if rows % t == 0)


SMALL_ROWS = 24


def _pad_row(a):
    flat = a.reshape(1, -1)
    return jnp.pad(flat, ((0, 0), (0, D_MODEL - flat.shape[1])))


def _tile(n, pref):
    return min(n, pref)


def kernel(x, p, att_w_qkv, att_sink, att_w_o, hgrn_w_in, hgrn_lb_logits, hgrn_norm_g, hgrn_w_o, ln_mix_g, ln_mix_b, ffn_w_in, ffn_w_out, ln_ffn_g, ln_ffn_b, ple_w_gate, ple_w_proj, loss_target, m_att_w_qkv, m_att_sink, m_att_w_o, m_hgrn_w_in, m_hgrn_lb_logits, m_hgrn_norm_g, m_hgrn_w_o, m_ln_mix_g, m_ln_mix_b, m_ffn_w_in, m_ffn_w_out, m_ln_ffn_g, m_ln_ffn_b, m_ple_w_gate, m_ple_w_proj, v_att_w_qkv, v_att_sink, v_att_w_o, v_hgrn_w_in, v_hgrn_lb_logits, v_hgrn_norm_g, v_hgrn_w_o, v_ln_mix_g, v_ln_mix_b, v_ffn_w_in, v_ffn_w_out, v_ln_ffn_g, v_ln_ffn_b, v_ple_w_gate, v_ple_w_proj):
    names = ["att_w_qkv", "att_sink", "att_w_o", "hgrn_w_in", "hgrn_lb_logits", "hgrn_norm_g", "hgrn_w_o", "ln_mix_g",
             "ln_mix_b", "ffn_w_in", "ffn_w_out", "ln_ffn_g", "ln_ffn_b", "ple_w_gate", "ple_w_proj"]
    w = dict(zip(names, (att_w_qkv, att_sink, att_w_o, hgrn_w_in, hgrn_lb_logits, hgrn_norm_g, hgrn_w_o, ln_mix_g,
                         ln_mix_b, ffn_w_in, ffn_w_out, ln_ffn_g, ln_ffn_b, ple_w_gate, ple_w_proj)))
    mom = dict(zip(names, (m_att_w_qkv, m_att_sink, m_att_w_o, m_hgrn_w_in, m_hgrn_lb_logits, m_hgrn_norm_g, m_hgrn_w_o,
                           m_ln_mix_g, m_ln_mix_b, m_ffn_w_in, m_ffn_w_out, m_ln_ffn_g, m_ln_ffn_b, m_ple_w_gate,
                           m_ple_w_proj)))
    var = dict(zip(names, (v_att_w_qkv, v_att_sink, v_att_w_o, v_hgrn_w_in, v_hgrn_lb_logits, v_hgrn_norm_g, v_hgrn_w_o,
                           v_ln_mix_g, v_ln_mix_b, v_ffn_w_in, v_ffn_w_out, v_ln_ffn_g, v_ln_ffn_b, v_ple_w_gate,
                           v_ple_w_proj)))
    s = x.shape[1]
    me = 4 * lax.axis_index("x") + 2 * lax.axis_index("y") + lax.axis_index("c")
    tm = _tile(s, 512)
    tbig = _tile(s, 1024)
    ts = _tile(s // 2, 2048)
    x0 = x.reshape(s, D_MODEL)
    target = loss_target.reshape(s, D_MODEL)
    pl_in = p.reshape(DEPTH, s, PLE_DIM)

    w_rows = [_pack_local(w, _layer_params(i)).astype(BF16) for i in range(DEPTH)]
    full = _unpack_gathered(_gather(w_rows[0], name="gather_weights"), 0, w)
    lb_rows = jnp.pad(hgrn_lb_logits.reshape(8, HG_DIM), ((0, 0), (0, D_MODEL - HG_DIM)))
    lb_all = _gather(lb_rows, name="gather_lb")[:, :, :HG_DIM]
    logits_full = jnp.moveaxis(lb_all, 0, 1).reshape(DEPTH, 2 * D_MODEL)
    lb = _lb_fwd(logits_full, name="lb_fwd")
    cos, sin = _rope_tables(s)

    saved = []
    xf, xb = x0, x0
    for i in range(DEPTH):
        j = i // 2
        sv = {"x": xf, "xb": xb, "w": full}
        nxt = (w_rows[i + 1], True) if i + 1 < DEPTH else None
        if i % 2 == 0:
            z = _mm(xb, full["att_w_qkv"], tm=tbig, tn=512, tk=D_MODEL, name="att_in")
            o, *more = _att_fwd(z, w["att_sink"][j], cos, sin, comm=nxt, name="att_fwd")
            w_o = full["att_w_o"]
        else:
            z = _mm(xb, full["hgrn_w_in"], tm=tbig, tn=1024, tk=D_MODEL, name="hgrn_in")
            lbl = lb[j].reshape(2, HG_HEADS, 1, HG_DIM)
            of, st_f, qs_f, dec_f, *more = _hg_fwd(z, lbl[0], rev=False, ts=ts, comm=nxt, name="hgrn_fwd")
            ob, st_b, qs_b, dec_b = _hg_fwd(z, lbl[1], rev=True, ts=ts, name="hgrn_fwd_rev")
            o = _hg_post(of, ob, z, w["hgrn_norm_g"][j].reshape(1, HG_DIM), tm=tm, name="hgrn_post")
            w_o = full["hgrn_w_o"]
            sv.update(of=of, ob=ob, st_f=st_f, st_b=st_b, lbl=lbl, qs_f=qs_f, qs_b=qs_b, dec_f=dec_f, dec_b=dec_b)
        sv.update(z=z, o=o)
        g1, b1 = w["ln_mix_g"][i:i + 1], w["ln_mix_b"][i:i + 1]
        pre1, x1b = _proj_ln(o, w_o, xf, g1, b1, tm=tm, name="mix_out_ln")
        gg, uu, act = _ffn_in(x1b, full["ffn_w_in"], tm=tm, tn=FF_TILE, name="ffn_in")
        pre2, x2b, xf, xb = _ffn_out_ple(act, full["ffn_w_out"], pre1, g1, b1, w["ln_ffn_g"][i:i + 1],
                                         w["ln_ffn_b"][i:i + 1], pl_in[i], full["ple_w_gate"], full["ple_w_proj"], tm=tm,
                                         name="ffn_out_ple")
        sv.update(pre1=pre1, x1b=x1b, g=gg, u=uu, act=act, pre2=pre2, x2b=x2b)
        saved.append(sv)
        if nxt is not None:
            full = _unpack_gathered(more[0], i + 1, w)

    dx, loss_blk = _loss_head(xf, target, tm=tm, name="loss_head")
    loss = lax.psum(loss_blk[0, 0], AXES)

    small = {n: [None] * DEPTH for n in ("ln_mix_g", "ln_mix_b", "ln_ffn_g", "ln_ffn_b")}
    dlb_rows = [None] * 4
    dnorm, dsink = [None] * 2, [None] * 2
    recv_late, recv_early = [None] * DEPTH, [None] * DEPTH
    above = None
    mmw = functools.partial(_mm, ta=True, tk=_tile(s, 2048), out_dtype=BF16)
    for i in reversed(range(DEPTH)):
        j = i // 2
        sv = saved[i]
        full, gl = sv["w"], {}
        da, dpp, dy2, dy2b, small["ln_ffn_g"][i], small["ln_ffn_b"][i] = _ple_ln_bwd(
            dx, sv["x2b"], pl_in[i], full["ple_w_gate"], full["ple_w_proj"], sv["pre2"], w["ln_ffn_g"][i:i + 1],
            tm=tm, name="ple_ln_bwd")
        gl["ple_w_gate"] = mmw(sv["x2b"], da, tm=D_MODEL, tn=D_MODEL, name="dw_ple_gate")
        gl["ple_w_proj"] = mmw(pl_in[i], dpp, tm=PLE_DIM, tn=D_MODEL, name="dw_ple_proj")
        dg, du = _ffn_bwd_act(dy2b, full["ffn_w_out"], sv["g"], sv["u"], tm=_tile(s, SUB_ROWS), tn=D_FF,
                              name="ffn_bwd_act")
        gl["ffn_w_out"] = mmw(sv["act"], dy2b, tm=FF_TILE, tn=D_MODEL, name="dw_ffn_out")
        dy1, dy1b, small["ln_mix_g"][i], small["ln_mix_b"][i] = _dx_from_pieces(
            [dg, du], full["ffn_w_in"], dy2, tm=_tile(s, SUB_ROWS), ln=(sv["pre1"], w["ln_mix_g"][i:i + 1]),
            name="ffn_bwd_x_ln")
        gl["ffn_w_in"] = mmw(sv["x1b"], [dg, du], tm=D_MODEL, tn=FF_TILE, name="dw_ffn_in")
        n_out, n_inw = ("att_w_o", "att_w_qkv") if i % 2 == 0 else ("hgrn_w_o", "hgrn_w_in")
        do = _mm(dy1b, full[n_out], tm=tbig, tn=D_MODEL, tk=D_MODEL, tb=True, out_dtype=BF16, name="mix_out_bwd")
        gl[n_out] = mmw(sv["o"], dy1b, tm=D_MODEL, tn=D_MODEL, name="dw_mix_out")
        early = _pack_full(gl, _layer_params(i)[1:], w)
        comm = (early if above is None else jnp.concatenate([above, early], axis=1), False)
        if i % 2 == 0:
            dzq, part, dsk, *more = _att_bwd(sv["z"], do, w["att_sink"][j], cos, sin, comm=comm, name="att_bwd")
            dz = [dzq, _att_bwd_kv(part, cos, sin, name="att_bwd_kv")]
            dsink[j] = dsk[:, 0]
        else:
            dsum, dgate, dnorm[j] = _hg_post_bwd(do, sv["of"], sv["ob"], sv["z"], w["hgrn_norm_g"][j].reshape(1, HG_DIM),
                                                 tm=tm, name="hgrn_post_bwd")
            dq1, df1, dv1, dlb1, *more = _hg_bwd(sv["z"], sv["lbl"][0], sv["st_f"], sv["qs_f"], sv["dec_f"], dsum, None,
                                                 None, rev=False, ts=ts, comm=comm, name="hgrn_bwd")
            dq2, df2, dv2, dlb2 = _hg_bwd(sv["z"], sv["lbl"][1], sv["st_b"], sv["qs_b"], sv["dec_b"], dsum, dq1, dv1,
                                          rev=True, ts=ts, name="hgrn_bwd_rev")
            dz = [dq2, df1, df2, dv2, dgate]
            dlb_rows[2 * j] = dlb1.reshape(1, D_MODEL)
            dlb_rows[2 * j + 1] = dlb2.reshape(1, D_MODEL)
        if above is not None:
            recv_late[i + 1] = (more[0], 0)
        recv_early[i] = (more[0], 0 if above is None else above.shape[1])
        dx = _dx_from_pieces(dz, full[n_inw], dy1, tm=tm, name="mix_in_bwd")
        gl[n_inw] = mmw(sv["xb"], dz, tm=D_MODEL, tn=512, name="dw_mix_in")
        above = _pack_full(gl, _layer_params(i)[:1], w)
    grad_x = dx.reshape(x.shape)
    recv_late[0] = (_exchange(above, name="exchange_grads"), 0)

    big_out = [{n: [None] * w[n].shape[0] for n, _ in BIG} for _ in range(4)]
    for i in range(DEPTH):
        for params, (got, off) in ((_layer_params(i)[:1], recv_late[i]), (_layer_params(i)[1:], recv_early[i])):
            w_part = _pack_local(w, params)
            outs = _adamw(w_part, got, _pack_local(mom, params), _pack_local(var, params),
                          tr=_row_tile(math.gcd(w_part.shape[0], off)), g_off=off, name="adamw_big")
            for kind, packed in enumerate(outs):
                for (n, j, _), piece in zip(params, _unpack_local(packed, params, w).values()):
                    big_out[kind][n][j] = piece
    big_out = [{n: jnp.stack(v) for n, v in kind.items()} for kind in big_out]

    small_rows = jnp.concatenate(
        [jnp.concatenate(small[n], axis=0) for n in ("ln_mix_g", "ln_mix_b", "ln_ffn_g", "ln_ffn_b")] + dlb_rows
        + [_pad_row(jnp.stack(dnorm)), _pad_row(jnp.stack(dsink)), jnp.zeros((2, D_MODEL), F32)], axis=0)
    small_all = _gather(small_rows, name="gather_small")
    lbw, lbm, lbv = (t.reshape(4, 2 * HG_DIM) for t in (hgrn_lb_logits, mom["hgrn_lb_logits"], var["hgrn_lb_logits"]))
    summed = _sum8(small_all, name="sum_small")
    dlb_mine = lax.dynamic_slice_in_dim(summed[16:20].reshape(2, 2, HG_HEADS, HG_DIM), me, 1, axis=2)
    dlogits = _lb_bwd(lbw, dlb_mine.reshape(2, 2 * HG_DIM), name="lb_bwd")

    def small_pack(ln4, lbt, ng, sk):
        return jnp.concatenate([ln4[n] for n in ("ln_mix_g", "ln_mix_b", "ln_ffn_g", "ln_ffn_b")]
                               + [_pad_row(lbt), _pad_row(ng), _pad_row(sk), jnp.zeros((5, D_MODEL), F32)], axis=0)

    g_small = jnp.concatenate([summed[:16], _pad_row(dlogits), summed[20:22], jnp.zeros((5, D_MODEL), F32)], axis=0)
    souts = _adamw(small_pack(w, lbw, w["hgrn_norm_g"], w["att_sink"]), g_small,
                   small_pack(mom, lbm, mom["hgrn_norm_g"], mom["att_sink"]),
                   small_pack(var, lbv, var["hgrn_norm_g"], var["att_sink"]), tr=SMALL_ROWS, name="adamw_small")

    def small_unpack(t):
        out = {n: t[4 * k:4 * k + 4] for k, n in enumerate(("ln_mix_g", "ln_mix_b", "ln_ffn_g", "ln_ffn_b"))}
        out["hgrn_lb_logits"] = t[16].reshape(hgrn_lb_logits.shape)
        out["hgrn_norm_g"] = t[17, :2 * HG_DIM].reshape(hgrn_norm_g.shape)
        out["att_sink"] = t[18, :2 * N_Q_HEADS].reshape(att_sink.shape)
        return out

    result = [loss, grad_x]
    for big_t, small_t in zip(big_out, souts):
        merged = dict(big_t)
        merged.update(small_unpack(small_t))
        result += [merged[n] for n in names]
    return tuple(result)
```

```python
import functools
import math

import jax
import jax.numpy as jnp
from jax import lax
from jax.experimental import pallas as pl
from jax.experimental.pallas import tpu as pltpu

F32 = jnp.float32
BF16 = jnp.bfloat16

D_MODEL = 1024
DEPTH = 4
HEAD_DIM = 64
N_Q_HEADS = 16
N_KV_HEADS = 4
GROUP = 4
KV_DIM = 256
ATT_BLOCK = 128
ROPE_DIM = 16
ROPE_THETA = 500000.0
HG_HEADS = 8
HG_DIM = 128
HG_CHUNK = 64
HG_SUB = 16
D_FF = 2816
FF_TILE = 1408
SUB_ROWS = 256
PLE_DIM = 256
ALPHA = (2 * DEPTH) ** 0.25
LN_EPS = 1e-5
ADAM_LR, ADAM_B1, ADAM_B2, ADAM_EPS, ADAM_WD, ADAM_STEP = 0.001, 0.9, 0.999, 1e-08, 0.01, 10

N_DEV = 8
LANES = 128
VMEM_LIMIT = 52 * 1024 * 1024
NEG = -1e30
MESH = pl.DeviceIdType.MESH
AXES = ("x", "y", "c")

BIG = (("att_w_qkv", 2), ("att_w_o", 1), ("hgrn_w_in", 2), ("hgrn_w_o", 1), ("ffn_w_in", 2), ("ffn_w_out", 1),
       ("ple_w_gate", 1), ("ple_w_proj", 2))


def _params(sem=None, vmem=VMEM_LIMIT):
    return pltpu.CompilerParams(dimension_semantics=sem, vmem_limit_bytes=vmem)


def _sigmoid(x):
    return jax.nn.sigmoid(x)


def _direct_copies(src_ref, out_ref, send_sems, recv_sems, local_sem, gather, arrivals):
    x, y, c = lax.axis_index("x"), lax.axis_index("y"), lax.axis_index("c")
    me = 4 * x + 2 * y + c
    mine = (lambda j: src_ref) if gather else (lambda j: src_ref.at[j])
    pairs = []
    for k in range(1, N_DEV):
        px, py, pc = x ^ (k >> 2), y ^ ((k >> 1) & 1), c ^ (k & 1)
        peer = 4 * px + 2 * py + pc
        send = pltpu.make_async_remote_copy(
            src_ref=mine(peer), dst_ref=out_ref.at[me], send_sem=send_sems.at[k], recv_sem=recv_sems.at[k],
            device_id=(px, py, pc), device_id_type=MESH)
        arrival = pltpu.make_async_remote_copy(
            src_ref=mine(peer), dst_ref=out_ref.at[peer], send_sem=send_sems.at[k], recv_sem=recv_sems.at[k],
            device_id=(x, y, c), device_id_type=MESH) if arrivals else None
        pairs.append((send, arrival))
    return pltpu.make_async_copy(mine(me), out_ref.at[me], local_sem), pairs


def _direct_start(*refs, gather):
    local, pairs = _direct_copies(*refs, gather, False)
    local.start()
    for send, _ in pairs:
        send.start()


def _direct_wait(*refs, gather):
    local, pairs = _direct_copies(*refs, gather, True)
    for send, arrival in pairs:
        send.wait_send()
        arrival.wait_recv()
    local.wait()


COMM_SCRATCH = [pltpu.SemaphoreType.DMA((N_DEV,)), pltpu.SemaphoreType.DMA((N_DEV,)), pltpu.SemaphoreType.DMA]


def _exchange(src, *, gather=False, name):
    def body(*refs):
        _direct_start(*refs, gather=gather)
        _direct_wait(*refs, gather=gather)

    blk = tuple(src.shape) if gather else tuple(src.shape[1:])
    return pl.pallas_call(
        body, name=name,
        out_shape=jax.ShapeDtypeStruct((N_DEV,) + blk, src.dtype),
        in_specs=[pl.BlockSpec(memory_space=pltpu.HBM)],
        out_specs=pl.BlockSpec(memory_space=pltpu.HBM),
        scratch_shapes=COMM_SCRATCH,
    )(src)


def _call(body, *, name, grid, out_shape, in_specs, out_specs, args, scratch_shapes=(), sem, comm=None, edge=None):
    out_shape, out_specs = tuple(out_shape), tuple(out_specs)
    if comm is None:
        return pl.pallas_call(body, name=name, grid=grid, out_shape=out_shape, in_specs=list(in_specs),
                              out_specs=out_specs, scratch_shapes=list(scratch_shapes),
                              compiler_params=_params(sem))(*args)
    src, gather = comm
    n_in, n_out, n_scr = len(args), len(out_shape), len(scratch_shapes)
    blk = tuple(src.shape) if gather else tuple(src.shape[1:])
    hbm = pl.BlockSpec(memory_space=pltpu.HBM)

    def carrying(*refs):
        ins, src_ref = refs[:n_in], refs[n_in]
        outs, dst_ref = refs[n_in + 1:n_in + 1 + n_out], refs[n_in + 1 + n_out]
        own = refs[n_in + 2 + n_out:n_in + 2 + n_out + n_scr]
        comm_refs = (src_ref, dst_ref) + tuple(refs[n_in + 2 + n_out + n_scr:])
        first, last = edge()

        @pl.when(first)
        def _():
            _direct_start(*comm_refs, gather=gather)

        body(*ins, *outs, *own)

        @pl.when(last)
        def _():
            _direct_wait(*comm_refs, gather=gather)

    return pl.pallas_call(
        carrying, name=name, grid=grid,
        out_shape=out_shape + (jax.ShapeDtypeStruct((N_DEV,) + blk, src.dtype),),
        in_specs=list(in_specs) + [hbm], out_specs=out_specs + (hbm,),
        scratch_shapes=list(scratch_shapes) + COMM_SCRATCH,
        compiler_params=_params(("arbitrary",) * len(grid)),
    )(*args, src)


def _gather(src, *, name):
    def body(src_ref, out_ref, send_sems, recv_sems, local_sem):
        x, y, c = lax.axis_index("x"), lax.axis_index("y"), lax.axis_index("c")
        sibling = (x, y, 1 - c)
        chips = [(1 - x, y), (x, 1 - y), (1 - x, 1 - y)]

        def rows(px, py, pc):
            return out_ref.at[4 * px + 2 * py + pc]

        def copy(k, block, to, from_src=False):
            return pltpu.make_async_remote_copy(
                src_ref=src_ref if from_src else rows(*block), dst_ref=rows(*block), send_sem=send_sems.at[k],
                recv_sem=recv_sems.at[k], device_id=to, device_id_type=MESH)

        me = (x, y, c)
        mine = pltpu.make_async_copy(src_ref, rows(*me), local_sem)
        mine.start()
        first = [copy(0, me, sibling, from_src=True)]
        first += [copy(1 + j, me, (*chip, c), from_src=True) for j, chip in enumerate(chips)]
        for cp in first:
            cp.start()
        passed = [copy(4 + j, (*chip, c), sibling) for j, chip in enumerate(chips)]
        for j, chip in enumerate(chips):
            copy(1 + j, (*chip, c), me).wait_recv()
            passed[j].start()
        copy(0, sibling, me).wait_recv()
        for j, chip in enumerate(chips):
            copy(4 + j, (*chip, 1 - c), me).wait_recv()
        for cp in first + passed:
            cp.wait_send()
        mine.wait()

    return pl.pallas_call(
        body, name=name,
        out_shape=jax.ShapeDtypeStruct((N_DEV,) + tuple(src.shape), src.dtype),
        in_specs=[pl.BlockSpec(memory_space=pltpu.HBM)],
        out_specs=pl.BlockSpec(memory_space=pltpu.HBM),
        scratch_shapes=[pltpu.SemaphoreType.DMA((7,)), pltpu.SemaphoreType.DMA((7,)), pltpu.SemaphoreType.DMA],
    )(src)


def _mm(a, b, *, tm, tn, tk, ta=False, tb=False, out_dtype=F32, name):
    b_list = list(b) if isinstance(b, (list, tuple)) else [b]
    assert not (tb and len(b_list) > 1)
    m, kdim = (a.shape[1], a.shape[0]) if ta else a.shape
    joff = [0]
    for piece in b_list:
        joff.append(joff[-1] + (piece.shape[0] if tb else piece.shape[1]) // tn)
    nk, n = kdim // tk, joff[-1] * tn
    dims = (((0 if ta else 1,), (1 if tb else 0,)), ((), ()))

    def mine(j, p):
        return (j >= joff[p]) & (j < joff[p + 1])

    def body(*refs):
        a_ref, b_refs, o_ref = refs[0], refs[1:1 + len(b_list)], refs[1 + len(b_list)]
        acc_ref = refs[-1] if nk > 1 else None
        j, k = pl.program_id(1), pl.program_id(2)
        for p, b_ref in enumerate(b_refs):
            def step(b_ref=b_ref):
                part = lax.dot_general(a_ref[...].astype(BF16), b_ref[...].astype(BF16), dims,
                                       preferred_element_type=F32)
                if nk == 1:
                    o_ref[...] = part.astype(out_dtype)
                else:
                    _accumulate(acc_ref, part, k == 0)

            if len(b_list) == 1:
                step()
            else:
                pl.when(mine(j, p))(step)
        if nk > 1:
            @pl.when(k == nk - 1)
            def _():
                o_ref[...] = acc_ref[...].astype(out_dtype)

    def b_spec(p):
        jj = lambda j: jnp.clip(j - joff[p], 0, joff[p + 1] - joff[p] - 1)
        kk = (lambda j, k: k) if len(b_list) == 1 else (lambda j, k: jnp.where(mine(j, p), k, 0))
        return (pl.BlockSpec((tn, tk), lambda i, j, k: (jj(j), kk(j, k))) if tb
                else pl.BlockSpec((tk, tn), lambda i, j, k: (kk(j, k), jj(j))))

    a_spec = pl.BlockSpec((tk, tm), lambda i, j, k: (k, i)) if ta else pl.BlockSpec((tm, tk), lambda i, j, k: (i, k))
    return pl.pallas_call(
        body, name=name, grid=(m // tm, n // tn, nk),
        out_shape=jax.ShapeDtypeStruct((m, n), out_dtype),
        in_specs=[a_spec] + [b_spec(p) for p in range(len(b_list))],
        out_specs=pl.BlockSpec((tm, tn), lambda i, j, k: (i, j)),
        scratch_shapes=[pltpu.VMEM((tm, tn), F32)] if nk > 1 else [],
        compiler_params=_params(("parallel", "parallel", "arbitrary")),
    )(a, *b_list)


def _dx_from_pieces(pieces, w, add, *, tm, ln=None, name):
    s = pieces[0].shape[0]
    widths = [p.shape[1] for p in pieces]

    def body(*refs):
        p_refs, (w_ref, add_ref) = refs[:len(pieces)], refs[len(pieces):len(pieces) + 2]
        rest = refs[len(pieces) + 2:]
        pg = jnp.zeros((1, D_MODEL), F32)
        pb = jnp.zeros((1, D_MODEL), F32)
        for rs in _row_parts(tm):
            r = ALPHA * add_ref[rs, :]
            off = 0
            for p_ref, width in zip(p_refs, widths):
                r = r + lax.dot_general(p_ref[rs, :], w_ref[:, off:off + width], (((1,), (1,)), ((), ())),
                                        preferred_element_type=F32)
                off += width
            if ln is None:
                rest[0][rs, :] = r
            else:
                dy, qg, qb = _ln_bwd_rows(r, rest[0][rs, :], rest[1][...])
                rest[2][rs, :] = dy
                rest[3][rs, :] = dy.astype(BF16)
                pg, pb = pg + qg, pb + qb
        if ln is not None:
            _accumulate(rest[4], pg, pl.program_id(0) == 0)
            _accumulate(rest[5], pb, pl.program_id(0) == 0)

    row = lambda i: (i, 0)
    tile = pl.BlockSpec((tm, D_MODEL), row)
    vec = pl.BlockSpec((1, D_MODEL), lambda i: (0, 0))
    in_specs = ([pl.BlockSpec((tm, width), row) for width in widths]
                + [pl.BlockSpec((D_MODEL, sum(widths)), lambda i: (0, 0)), tile])
    args = list(pieces) + [w, add]
    out_shape, out_specs = jax.ShapeDtypeStruct((s, D_MODEL), F32), tile
    if ln is not None:
        in_specs += [tile, vec]
        args += list(ln)
        out_shape = (jax.ShapeDtypeStruct((s, D_MODEL), F32), jax.ShapeDtypeStruct((s, D_MODEL), BF16),
                     jax.ShapeDtypeStruct((1, D_MODEL), F32), jax.ShapeDtypeStruct((1, D_MODEL), F32))
        out_specs = (tile, tile, vec, vec)
    return pl.pallas_call(
        body, name=name, grid=(s // tm,), out_shape=out_shape, in_specs=in_specs, out_specs=out_specs,
        compiler_params=_params(("arbitrary",) if ln is not None else ("parallel",)),
    )(*args)


def _ln_bwd_rows(do, y, g):
    mu = jnp.mean(y, axis=-1, keepdims=True)
    yc = y - mu
    var = jnp.mean(yc * yc, axis=-1, keepdims=True)
    rstd = lax.rsqrt(var + LN_EPS)
    xhat = yc * rstd
    dxhat = do * g
    dy = rstd * (dxhat - jnp.mean(dxhat, axis=-1, keepdims=True) - xhat * jnp.mean(dxhat * xhat, axis=-1, keepdims=True))
    return dy, jnp.sum(do * xhat, axis=0, keepdims=True), jnp.sum(do, axis=0, keepdims=True)


def _accumulate(ref, val, first):
    @pl.when(first)
    def _():
        ref[...] = val

    @pl.when(jnp.logical_not(first))
    def _():
        ref[...] += val


def _layer_norm_rows(y, g, b):
    mu = jnp.mean(y, axis=-1, keepdims=True)
    yc = y - mu
    var = jnp.mean(yc * yc, axis=-1, keepdims=True)
    return yc * lax.rsqrt(var + LN_EPS) * g + b


def _proj_ln(a, w, res, g, b, *, tm, name):
    s, kdim = a.shape

    def body(a_ref, w_ref, res_ref, g_ref, b_ref, pre_ref, obf_ref):
        for rs in _row_parts(tm):
            h = jnp.dot(a_ref[rs, :], w_ref[...], preferred_element_type=F32)
            pre = ALPHA * res_ref[rs, :] + h
            pre_ref[rs, :] = pre
            obf_ref[rs, :] = _layer_norm_rows(pre, g_ref[...], b_ref[...]).astype(BF16)

    row = lambda i: (i, 0)
    fix = lambda i: (0, 0)
    return pl.pallas_call(
        body, name=name, grid=(s // tm,),
        out_shape=(jax.ShapeDtypeStruct((s, D_MODEL), F32), jax.ShapeDtypeStruct((s, D_MODEL), BF16)),
        in_specs=[pl.BlockSpec((tm, kdim), row), pl.BlockSpec((kdim, D_MODEL), fix), pl.BlockSpec((tm, D_MODEL), row),
                  pl.BlockSpec((1, D_MODEL), fix), pl.BlockSpec((1, D_MODEL), fix)],
        out_specs=(pl.BlockSpec((tm, D_MODEL), row),) * 2,
        compiler_params=_params(("parallel",)),
    )(a, w, res, g, b)


def _row_parts(tm):
    sub = min(tm, SUB_ROWS)
    return [pl.ds(r * sub, sub) for r in range(tm // sub)]


def _ffn_in(xbf, w, *, tm, tn, name):
    s = xbf.shape[0]
    nj = D_FF // tn

    def body(x_ref, wg_ref, wu_ref, g_ref, u_ref, act_ref):
        for rs in _row_parts(tm):
            xv = x_ref[rs, :]
            gg = jnp.dot(xv, wg_ref[...], preferred_element_type=F32)
            uu = jnp.dot(xv, wu_ref[...], preferred_element_type=F32)
            g_ref[rs, :] = gg.astype(BF16)
            u_ref[rs, :] = uu.astype(BF16)
            act_ref[rs, :] = (gg * _sigmoid(gg) * uu).astype(BF16)

    out = jax.ShapeDtypeStruct((s, D_FF), BF16)
    tile = pl.BlockSpec((tm, tn), lambda j, i: (i, j))
    return pl.pallas_call(
        body, name=name, grid=(nj, s // tm),
        out_shape=(out, out, out),
        in_specs=[pl.BlockSpec((tm, D_MODEL), lambda j, i: (i, 0)), pl.BlockSpec((D_MODEL, tn), lambda j, i: (0, j)),
                  pl.BlockSpec((D_MODEL, tn), lambda j, i: (0, j + nj))],
        out_specs=(tile, tile, tile),
        compiler_params=_params(("parallel", "parallel")),
    )(xbf, w, w)


def _ffn_bwd_act(dybf, w_out, g, u, *, tm, tn, name):
    s = dybf.shape[0]

    def body(dy_ref, w_ref, g_ref, u_ref, dg_ref, du_ref):
        for rs in _row_parts(tm):
            dact = lax.dot_general(dy_ref[rs, :], w_ref[...], (((1,), (1,)), ((), ())), preferred_element_type=F32)
            gg = g_ref[rs, :].astype(F32)
            uu = u_ref[rs, :].astype(F32)
            sg = _sigmoid(gg)
            dg_ref[rs, :] = (dact * uu * sg * (1.0 + gg * (1.0 - sg))).astype(BF16)
            du_ref[rs, :] = (dact * gg * sg).astype(BF16)

    out = jax.ShapeDtypeStruct((s, D_FF), BF16)
    tile = pl.BlockSpec((tm, tn), lambda j, i: (i, j))
    return pl.pallas_call(
        body, name=name, grid=(D_FF // tn, s // tm),
        out_shape=(out, out),
        in_specs=[pl.BlockSpec((tm, D_MODEL), lambda j, i: (i, 0)), pl.BlockSpec((tn, D_MODEL), lambda j, i: (j, 0)),
                  tile, tile],
        out_specs=(tile, tile),
        compiler_params=_params(("parallel", "parallel")),
    )(dybf, w_out, g, u)


def _ffn_out_ple(act, w_out, res_pre, res_g, res_b, g, b, p, w_gate, w_proj, *, tm, name):
    s = act.shape[0]

    def body(a_ref, w_ref, res_ref, rg_ref, rb_ref, g_ref, b_ref, p_ref, wg_ref, wp_ref, pre_ref, x2bf_ref, o_ref,
             obf_ref):
        for rs in _row_parts(tm):
            res = _layer_norm_rows(res_ref[rs, :], rg_ref[...], rb_ref[...])
            pre = ALPHA * res + jnp.dot(a_ref[rs, :], w_ref[...], preferred_element_type=F32)
            x2 = _layer_norm_rows(pre, g_ref[...], b_ref[...])
            x2bf = x2.astype(BF16)
            pre_ref[rs, :] = pre
            x2bf_ref[rs, :] = x2bf
            gate = jnp.dot(x2bf, wg_ref[...], preferred_element_type=F32)
            pp = jnp.dot(p_ref[rs, :].astype(BF16), wp_ref[...], preferred_element_type=F32)
            out = x2 + _sigmoid(gate) * pp
            o_ref[rs, :] = out
            obf_ref[rs, :] = out.astype(BF16)

    row = lambda i: (i, 0)
    fix = lambda i: (0, 0)
    tile = pl.BlockSpec((tm, D_MODEL), row)
    vec = pl.BlockSpec((1, D_MODEL), fix)
    act_t = lambda dt: jax.ShapeDtypeStruct((s, D_MODEL), dt)
    return pl.pallas_call(
        body, name=name, grid=(s // tm,),
        out_shape=(act_t(F32), act_t(BF16), act_t(F32), act_t(BF16)),
        in_specs=[pl.BlockSpec((tm, D_FF), row), pl.BlockSpec((D_FF, D_MODEL), fix), tile, vec, vec, vec, vec,
                  pl.BlockSpec((tm, PLE_DIM), row), pl.BlockSpec((D_MODEL, D_MODEL), fix),
                  pl.BlockSpec((PLE_DIM, D_MODEL), fix)],
        out_specs=(tile, tile, tile, tile),
        compiler_params=_params(("parallel",)),
    )(act, w_out, res_pre, res_g, res_b, g, b, p, w_gate, w_proj)


def _ple_ln_bwd(dx3, x2bf, p, w_gate, w_proj, pre, g, *, tm, name):
    s = dx3.shape[0]

    def body(d_ref, xbf_ref, p_ref, wg_ref, wp_ref, pre_ref, g_ref, da_ref, dpp_ref, dy_ref, dybf_ref, dg_ref, db_ref):
        pg = jnp.zeros((1, D_MODEL), F32)
        pb = jnp.zeros((1, D_MODEL), F32)
        for rs in _row_parts(tm):
            d = d_ref[rs, :]
            a = jnp.dot(xbf_ref[rs, :], wg_ref[...], preferred_element_type=F32)
            pp = jnp.dot(p_ref[rs, :].astype(BF16), wp_ref[...], preferred_element_type=F32)
            sg = _sigmoid(a)
            da = (d * pp * sg * (1.0 - sg)).astype(BF16)
            da_ref[rs, :] = da
            dpp_ref[rs, :] = (d * sg).astype(BF16)
            dx2 = d + lax.dot_general(da, wg_ref[...], (((1,), (1,)), ((), ())), preferred_element_type=F32)
            dy, qg, qb = _ln_bwd_rows(dx2, pre_ref[rs, :], g_ref[...])
            dy_ref[rs, :] = dy
            dybf_ref[rs, :] = dy.astype(BF16)
            pg, pb = pg + qg, pb + qb
        _accumulate(dg_ref, pg, pl.program_id(0) == 0)
        _accumulate(db_ref, pb, pl.program_id(0) == 0)

    row = lambda i: (i, 0)
    fix = lambda i: (0, 0)
    tile = pl.BlockSpec((tm, D_MODEL), row)
    vec = pl.BlockSpec((1, D_MODEL), fix)
    act = lambda dt: jax.ShapeDtypeStruct((s, D_MODEL), dt)
    return pl.pallas_call(
        body, name=name, grid=(s // tm,),
        out_shape=(act(BF16), act(BF16), act(F32), act(BF16), jax.ShapeDtypeStruct((1, D_MODEL), F32),
                   jax.ShapeDtypeStruct((1, D_MODEL), F32)),
        in_specs=[tile, tile, pl.BlockSpec((tm, PLE_DIM), row), pl.BlockSpec((D_MODEL, D_MODEL), fix),
                  pl.BlockSpec((PLE_DIM, D_MODEL), fix), tile, vec],
        out_specs=(tile, tile, tile, tile, vec, vec),
        compiler_params=_params(("arbitrary",)),
    )(dx3, x2bf, p, w_gate, w_proj, pre, g)


def _loss_head(y, target, *, tm, name):
    s = y.shape[0]

    def body(y_ref, t_ref, dy_ref, loss_ref, acc_ref):
        err = y_ref[...] - t_ref[...]
        dy_ref[...] = err * (1.0 / D_MODEL)
        part = jnp.sum(err * err, axis=0, keepdims=True)

        @pl.when(pl.program_id(0) == 0)
        def _():
            acc_ref[...] = part

        @pl.when(pl.program_id(0) > 0)
        def _():
            acc_ref[...] += part

        @pl.when(pl.program_id(0) == pl.num_programs(0) - 1)
        def _():
            tot = jnp.sum(acc_ref[...], axis=1, keepdims=True) * (0.5 / D_MODEL)
            loss_ref[...] = jnp.broadcast_to(tot, (8, LANES))

    row = lambda i: (i, 0)
    return pl.pallas_call(
        body, name=name, grid=(s // tm,),
        out_shape=(jax.ShapeDtypeStruct((s, D_MODEL), F32), jax.ShapeDtypeStruct((8, LANES), F32)),
        in_specs=[pl.BlockSpec((tm, D_MODEL), row), pl.BlockSpec((tm, D_MODEL), row)],
        out_specs=(pl.BlockSpec((tm, D_MODEL), row), pl.BlockSpec((8, LANES), lambda i: (0, 0))),
        scratch_shapes=[pltpu.VMEM((1, D_MODEL), F32)],
        compiler_params=_params(("arbitrary",)),
    )(y, target)


def _rope_tables(s):
    inv = ROPE_THETA ** (-jnp.arange(0, ROPE_DIM, 2, dtype=F32) / ROPE_DIM)
    ang = jnp.arange(s, dtype=F32)[:, None] * inv[None, :]
    cos, sin = jnp.cos(ang), jnp.sin(ang)
    ones = jnp.ones((s, HEAD_DIM - ROPE_DIM), F32)
    c_head = jnp.concatenate([cos, cos, ones], axis=1)
    s_head = jnp.concatenate([-sin, sin, 0.0 * ones], axis=1)
    return jnp.concatenate([c_head, c_head], axis=1), jnp.concatenate([s_head, s_head], axis=1)


def _rope(v, cos, sin):
    n = v.shape[1] // LANES
    width = v.shape[1]
    cos_w = jnp.tile(cos, (1, n)) if n > 1 else cos
    sin_w = jnp.tile(sin, (1, n)) if n > 1 else sin
    dim = lax.broadcasted_iota(jnp.int32, (1, width), 1) % HEAD_DIM
    partner = jnp.where(dim < ROPE_DIM // 2, pltpu.roll(v, width - ROPE_DIM // 2, 1), pltpu.roll(v, ROPE_DIM // 2, 1))
    return v * cos_w + partner * sin_w


def _unrope(dv, cos, sin):
    n = dv.shape[1] // LANES
    width = dv.shape[1]
    cos_w = jnp.tile(cos, (1, n)) if n > 1 else cos
    sin_w = jnp.tile(sin, (1, n)) if n > 1 else sin
    t = dv * sin_w
    dim = lax.broadcasted_iota(jnp.int32, (1, width), 1) % HEAD_DIM
    partner = jnp.where(dim < ROPE_DIM // 2, pltpu.roll(t, width - ROPE_DIM // 2, 1),
                        jnp.where(dim < ROPE_DIM, pltpu.roll(t, ROPE_DIM // 2, 1), 0.0))
    return dv * cos_w + partner


def _att_mask(i, nb):
    rows = GROUP * ATT_BLOCK
    r = lax.broadcasted_iota(jnp.int32, (rows, 3 * ATT_BLOCK), 0) % ATT_BLOCK
    cidx = lax.broadcasted_iota(jnp.int32, (rows, 3 * ATT_BLOCK), 1)
    rel = r + ATT_BLOCK - cidx
    ok = (rel <= ATT_BLOCK) & (rel >= -ATT_BLOCK)
    ok = ok & ((cidx >= ATT_BLOCK) | (i > 0)) & ((cidx < 2 * ATT_BLOCK) | (i < nb - 1))
    return ok


def _att_mask_t(i, nb):
    cols = GROUP * ATT_BLOCK
    cidx = lax.broadcasted_iota(jnp.int32, (3 * ATT_BLOCK, cols), 0)
    r = lax.broadcasted_iota(jnp.int32, (3 * ATT_BLOCK, cols), 1) % ATT_BLOCK
    rel = r + ATT_BLOCK - cidx
    ok = (rel <= ATT_BLOCK) & (rel >= -ATT_BLOCK)
    return ok & ((cidx >= ATT_BLOCK) | (i > 0)) & ((cidx < 2 * ATT_BLOCK) | (i < nb - 1))


def _sink_lanes(sink_ref, h):
    cols = GROUP * ATT_BLOCK
    grp = lax.broadcasted_iota(jnp.int32, (1, cols), 1) // ATT_BLOCK
    out = jnp.zeros((1, cols), F32)
    for gq in range(GROUP):
        out = jnp.where(grp == gq, sink_ref[GROUP * h + gq], out)
    return out


def _half_mask(half):
    lane = lax.broadcasted_iota(jnp.int32, (1, LANES), 1)
    return (lane // HEAD_DIM) == half


def _stack_q(q, h):
    parts = []
    for gq in range(GROUP):
        n = GROUP * h + gq
        grp = q[:, LANES * (n // 2):LANES * (n // 2 + 1)]
        grp = jnp.where(_half_mask(n % 2), grp, 0.0)
        if n % 2 != h % 2:
            grp = pltpu.roll(grp, HEAD_DIM, 1)
        parts.append(grp)
    return jnp.concatenate(parts, axis=0)


def _unstack_q(stacked, h, acc):
    for gq in range(GROUP):
        n = GROUP * h + gq
        grp = stacked[ATT_BLOCK * gq:ATT_BLOCK * (gq + 1), :]
        grp = jnp.where(_half_mask(h % 2), grp, 0.0)
        if n % 2 != h % 2:
            grp = pltpu.roll(grp, HEAD_DIM, 1)
        acc[n // 2] = grp if acc[n // 2] is None else acc[n // 2] + grp
    return acc


def _sink_rows(sink_ref, h):
    rows = GROUP * ATT_BLOCK
    grp = lax.broadcasted_iota(jnp.int32, (rows, 1), 0) // ATT_BLOCK
    out = jnp.zeros((rows, 1), F32)
    for gq in range(GROUP):
        out = jnp.where(grp == gq, sink_ref[GROUP * h + gq], out)
    return out


def _att_probs(qs, kh, sink, valid):
    s = lax.dot_general(qs, kh, (((1,), (1,)), ((), ())), preferred_element_type=F32)
    s = jnp.where(valid, s, NEG)
    m = jnp.maximum(jnp.max(s, axis=-1, keepdims=True), sink)
    p = jnp.exp(s - m)
    es = jnp.exp(sink - m)
    den = jnp.sum(p, axis=-1, keepdims=True) + es
    inv = 1.0 / den
    return p * inv, es * inv


ATT_STEP = 4


def _att_specs(nb):
    rows = ATT_STEP * ATT_BLOCK
    prev = lambda i: (jnp.maximum(ATT_STEP * i - 1, 0), 0)
    cur = lambda i: (i, 0)
    nxt = lambda i: (jnp.minimum(ATT_STEP * (i + 1), nb - 1), 0)
    kv = lambda f: (lambda i: (f(i)[0], 2))
    shapes = ((rows, cur), (ATT_BLOCK, prev), (rows, cur), (ATT_BLOCK, nxt))
    tab = [pl.BlockSpec((r, LANES), f) for r, f in shapes]
    z = [pl.BlockSpec((rows, D_MODEL), cur)] + [pl.BlockSpec((r, 2 * KV_DIM), kv(f)) for r, f in shapes[1:]]
    return z, tab


def _att_load(zq_ref, kp_ref, kc_ref, kn_ref, cq_ref, sq_ref, cp_ref, sp_ref, cc_ref, sc_ref, cn_ref, sn_ref):
    q = (_rope(zq_ref[...], cq_ref[...], sq_ref[...]) * (HEAD_DIM ** -0.5))
    ks, vs = [], []
    for ref, c_ref, s_ref in ((kp_ref, cp_ref, sp_ref), (kc_ref, cc_ref, sc_ref), (kn_ref, cn_ref, sn_ref)):
        kvb = ref[...]
        ks.append(_rope(kvb[:, :KV_DIM], c_ref[...], s_ref[...]))
        vs.append(kvb[:, KV_DIM:])
    return q, jnp.concatenate(ks, axis=0).astype(BF16), jnp.concatenate(vs, axis=0).astype(BF16)


def _att_rows(sub):
    return (slice(ATT_BLOCK * sub, ATT_BLOCK * (sub + 1)), slice(ATT_BLOCK * sub, ATT_BLOCK * (sub + 3)))


def _att_fwd(z, sink, cos, sin, *, comm=None, name):
    s = z.shape[0]
    nb = s // ATT_BLOCK
    steps = nb // ATT_STEP

    def body(zq_ref, kp_ref, kc_ref, kn_ref, cq_ref, cp_ref, cc_ref, cn_ref, sq_ref, sp_ref, sc_ref, sn_ref, sink_ref,
             o_ref):
        i = pl.program_id(0)
        q, k, v = _att_load(zq_ref, kp_ref, kc_ref, kn_ref, cq_ref, sq_ref, cp_ref, sp_ref, cc_ref, sc_ref, cn_ref, sn_ref)
        for sub in range(ATT_STEP):
            qrows, krows = _att_rows(sub)
            valid = _att_mask(ATT_STEP * i + sub, nb)
            acc = [None] * (N_Q_HEADS // 2)
            for h in range(N_KV_HEADS):
                lanes = slice(LANES * (h // 2), LANES * (h // 2 + 1))
                qs = _stack_q(q[qrows], h).astype(BF16)
                prob, _ = _att_probs(qs, k[krows, lanes], _sink_rows(sink_ref, h), valid)
                oh = jnp.dot(prob.astype(BF16), v[krows, lanes], preferred_element_type=F32)
                acc = _unstack_q(oh, h, acc)
            o_ref[qrows, :] = jnp.concatenate(acc, axis=1).astype(BF16)

    zspecs, tab = _att_specs(nb)
    return _call(
        body, name=name, grid=(steps,),
        out_shape=(jax.ShapeDtypeStruct((s, D_MODEL), BF16),),
        in_specs=zspecs + tab + tab + [pl.BlockSpec(memory_space=pltpu.SMEM)],
        out_specs=(pl.BlockSpec((ATT_STEP * ATT_BLOCK, D_MODEL), lambda i: (i, 0)),),
        args=(z, z, z, z, cos, cos, cos, cos, sin, sin, sin, sin, sink), sem=("parallel",), comm=comm,
        edge=lambda: (pl.program_id(0) == 0, pl.program_id(0) == steps - 1))


def _att_bwd(z, do, sink, cos, sin, *, comm=None, name):
    s = z.shape[0]
    nb = s // ATT_BLOCK
    steps = nb // ATT_STEP

    def body(zq_ref, kp_ref, kc_ref, kn_ref, cq_ref, cp_ref, cc_ref, cn_ref, sq_ref, sp_ref, sc_ref, sn_ref, sink_ref,
             do_ref, dq_ref, part_ref, dsink_ref):
        i = pl.program_id(0)
        q, k, v = _att_load(zq_ref, kp_ref, kc_ref, kn_ref, cq_ref, sq_ref, cp_ref, sp_ref, cc_ref, sc_ref, cn_ref, sn_ref)
        dsink = None
        nt = (((1,), (1,)), ((), ()))
        for sub in range(ATT_STEP):
            qrows, krows = _att_rows(sub)
            valid = _att_mask_t(ATT_STEP * i + sub, nb)
            dout = do_ref[qrows, :].astype(F32)
            dq_acc = [None] * (N_Q_HEADS // 2)
            dk_acc = [None] * 2
            dv_acc = [None] * 2
            rows = []
            for h in range(N_KV_HEADS):
                grp = h // 2
                lanes = slice(LANES * grp, LANES * (grp + 1))
                kh, vh = k[krows, lanes], v[krows, lanes]
                qs = _stack_q(q[qrows], h).astype(BF16)
                dos = _stack_q(dout, h).astype(BF16)
                sink = _sink_lanes(sink_ref, h)
                sc = jnp.where(valid, lax.dot_general(kh, qs, nt, preferred_element_type=F32), NEG)
                m = jnp.maximum(jnp.max(sc, axis=0, keepdims=True), sink)
                p = jnp.exp(sc - m)
                es = jnp.exp(sink - m)
                inv = 1.0 / (jnp.sum(p, axis=0, keepdims=True) + es)
                prob = p * inv
                dprob = lax.dot_general(vh, dos, nt, preferred_element_type=F32)
                delta = jnp.sum(prob * dprob, axis=0, keepdims=True)
                dsc = (prob * (dprob - delta)).astype(BF16)
                dsk = -(es * inv) * delta
                for gq in range(GROUP):
                    tot = jnp.sum(dsk[:, ATT_BLOCK * gq:ATT_BLOCK * (gq + 1)], axis=1, keepdims=True)
                    rows.append(jnp.broadcast_to(tot, (1, LANES)))
                dqs = lax.dot_general(dsc, kh, (((0,), (0,)), ((), ())), preferred_element_type=F32)
                dq_acc = _unstack_q(dqs, h, dq_acc)
                dkh = jnp.dot(dsc, qs, preferred_element_type=F32)
                dvh = jnp.dot(prob.astype(BF16), dos, preferred_element_type=F32)
                dk_acc[grp] = dkh if dk_acc[grp] is None else dk_acc[grp] + dkh
                dv_acc[grp] = dvh if dv_acc[grp] is None else dv_acc[grp] + dvh
            dq = jnp.concatenate(dq_acc, axis=1) * (HEAD_DIM ** -0.5)
            dq_ref[qrows, :] = _unrope(dq, cq_ref[qrows, :], sq_ref[qrows, :]).astype(BF16)
            part = jnp.concatenate(dk_acc + dv_acc, axis=1)
            for wdw in range(3):
                part_ref[sub, wdw] = part[ATT_BLOCK * wdw:ATT_BLOCK * (wdw + 1), :]
            mine = jnp.concatenate(rows, axis=0)
            dsink = mine if dsink is None else dsink + mine
        _accumulate(dsink_ref, dsink, i == 0)

    zspecs, tab = _att_specs(nb)
    group = pl.BlockSpec((ATT_STEP * ATT_BLOCK, D_MODEL), lambda i: (i, 0))
    return _call(
        body, name=name, grid=(steps,),
        out_shape=(jax.ShapeDtypeStruct((s, D_MODEL), BF16), jax.ShapeDtypeStruct((nb, 3, ATT_BLOCK, 2 * KV_DIM), F32),
                   jax.ShapeDtypeStruct((N_Q_HEADS, LANES), F32)),
        in_specs=zspecs + tab + tab + [pl.BlockSpec(memory_space=pltpu.SMEM), group],
        out_specs=(group, pl.BlockSpec((ATT_STEP, 3, ATT_BLOCK, 2 * KV_DIM), lambda i: (i, 0, 0, 0)),
                   pl.BlockSpec((N_Q_HEADS, LANES), lambda i: (0, 0))),
        args=(z, z, z, z, cos, cos, cos, cos, sin, sin, sin, sin, sink, do), sem=("arbitrary",), comm=comm,
        edge=lambda: (pl.program_id(0) == 0, pl.program_id(0) == steps - 1))


def _att_bwd_kv(part, cos, sin, *, name):
    nb = part.shape[0]

    def body(pn_ref, pc_ref, pp_ref, c_ref, s_ref, o_ref):
        j = pl.program_id(0)
        tot = pc_ref[...]
        tot = tot + jnp.where(j < nb - 1, pn_ref[...], 0.0)
        tot = tot + jnp.where(j > 0, pp_ref[...], 0.0)
        dk = _unrope(tot[:, :KV_DIM], c_ref[...], s_ref[...])
        o_ref[...] = jnp.concatenate([dk, tot[:, KV_DIM:]], axis=1).astype(BF16)

    blk = (None, None, ATT_BLOCK, 2 * KV_DIM)
    return pl.pallas_call(
        body, name=name, grid=(nb,),
        out_shape=jax.ShapeDtypeStruct((nb * ATT_BLOCK, 2 * KV_DIM), BF16),
        in_specs=[pl.BlockSpec(blk, lambda j: (jnp.minimum(j + 1, nb - 1), 0, 0, 0)),
                  pl.BlockSpec(blk, lambda j: (j, 1, 0, 0)),
                  pl.BlockSpec(blk, lambda j: (jnp.maximum(j - 1, 0), 2, 0, 0)),
                  pl.BlockSpec((ATT_BLOCK, LANES), lambda j: (j, 0)), pl.BlockSpec((ATT_BLOCK, LANES), lambda j: (j, 0))],
        out_specs=pl.BlockSpec((ATT_BLOCK, 2 * KV_DIM), lambda j: (j, 0)),
        compiler_params=_params(("parallel",)),
    )(part, part, part, cos, sin)


def _bdot(a, b, dims):
    return lax.dot_general(a.astype(BF16), b.astype(BF16), (dims, ((), ())), preferred_element_type=F32)


@jax.custom_vjp
def _dot_nn(a, b):
    return _bdot(a, b, ((1,), (0,)))


@jax.custom_vjp
def _dot_nt(a, b):
    return _bdot(a, b, ((1,), (1,)))


@jax.custom_vjp
def _dot_tn(a, b):
    return _bdot(a, b, ((0,), (0,)))


_dot_nn.defvjp(lambda a, b: (_dot_nn(a, b), (a, b)), lambda r, d: (_dot_nt(d, r[1]), _dot_tn(r[0], d)))
_dot_nt.defvjp(lambda a, b: (_dot_nt(a, b), (a, b)), lambda r, d: (_dot_nn(d, r[1]), _dot_tn(d, r[0])))
_dot_tn.defvjp(lambda a, b: (_dot_tn(a, b), (a, b)), lambda r, d: (_dot_nt(r[1], d), _dot_nn(r[0], d)))


def _running_sum(v, up):
    n = v.shape[0]
    rows = lax.broadcasted_iota(jnp.int32, v.shape, 0)
    sh = 1
    while sh < n:
        if up:
            v = v + jnp.where(rows < n - sh, pltpu.roll(v, n - sh, 0), 0.0)
        else:
            v = v + jnp.where(rows >= sh, pltpu.roll(v, sh, 0), 0.0)
        sh *= 2
    return v


@jax.custom_vjp
def _sum_down(v):
    return _running_sum(v, False)


@jax.custom_vjp
def _sum_up(v):
    return _running_sum(v, True)


_sum_down.defvjp(lambda v: (_running_sum(v, False), None), lambda _, d: (_sum_up(d),))
_sum_up.defvjp(lambda v: (_running_sum(v, True), None), lambda _, d: (_sum_down(d),))

N_SUB = HG_CHUNK // HG_SUB


def _fold_blocks(v):
    out = v[:HG_CHUNK]
    for i in range(1, N_SUB):
        out = out + v[HG_CHUNK * i:HG_CHUNK * (i + 1)]
    return out


@jax.custom_vjp
def _fold(v):
    return _fold_blocks(v)


_fold.defvjp(lambda v: (_fold_blocks(v), None), lambda _, d: (jnp.concatenate([d] * N_SUB, axis=0),))


def _hg_consts(rev):
    c, sub = HG_CHUNK, HG_SUB
    rowpos = lax.broadcasted_iota(jnp.int32, (c, HG_DIM), 0)
    rr = lax.broadcasted_iota(jnp.int32, (N_SUB * c, c), 0)
    key = lax.broadcasted_iota(jnp.int32, (N_SUB * c, c), 1)
    blk, qry = rr // c, rr % c
    if rev:
        rowpos, qry, key = c - 1 - rowpos, c - 1 - qry, c - 1 - key
    keep = (key // sub == blk) & (key <= qry)
    return keep, rowpos


def _pick(b, rowpos, t):
    return jnp.sum(jnp.where(rowpos == t, b, 0.0), axis=0, keepdims=True)


def _hg_local(zq, zf, zv, lbv, consts, dots):
    dot_nn, dot_nt, dot_tn, cum, fold = dots
    keep, rowpos = consts
    sig = _sigmoid(zf)
    f = lbv + (1.0 - lbv) * sig
    g = jnp.log(f)
    k = (1.0 - lbv) * (1.0 - sig)
    q = zq * _sigmoid(zq)
    b = cum(g)
    ends = [_pick(b, rowpos, (j + 1) * HG_SUB - 1) for j in range(N_SUB)]
    b_last = ends[-1]
    b_end = b_last
    for j in range(N_SUB - 1):
        b_end = jnp.where(rowpos // HG_SUB == j, ends[j], b_end)
    kc = k * jnp.exp(b_end - b)
    qbs = [q * jnp.exp(jnp.where(rowpos >= j * HG_SUB, b - ends[j], 0.0)) for j in range(N_SUB)]
    scores = fold(jnp.where(keep, dot_nt(jnp.concatenate(qbs, axis=0), kc), 0.0))
    return dot_nn(scores, zv), q * jnp.exp(b), k * jnp.exp(b_last - b), jnp.exp(b_last)


def _hg_chunk(zq, zf, zv, lbv, st, consts, dots):
    intra, qs, kd, dec = _hg_local(zq, zf, zv, lbv, consts, dots)
    return intra + dots[1](qs, st), dec * st + dots[2](zv, kd)


def _hg_dots(diff, rev):
    if diff:
        return _dot_nn, _dot_nt, _dot_tn, (_sum_up if rev else _sum_down), _fold
    return (lambda a, b: _bdot(a, b, ((1,), (0,))), lambda a, b: _bdot(a, b, ((1,), (1,))),
            lambda a, b: _bdot(a, b, ((0,), (0,))), lambda v: _running_sum(v, rev), _fold_blocks)


def _hg_specs(ts, nch, trow):
    tile = pl.BlockSpec((ts, HG_DIM), lambda h, t: (trow(t), h))
    mats = pl.BlockSpec((None, nch, HG_DIM, HG_DIM), lambda h, t: (h, trow(t), 0, 0))
    vecs = pl.BlockSpec((None, nch, 1, HG_DIM), lambda h, t: (h, trow(t), 0, 0))
    return tile, mats, vecs


def _time_order(nch, rev):
    return range(nch - 1, -1, -1) if rev else range(nch)


def _chunk_rows(c):
    return pl.ds(c * HG_CHUNK, HG_CHUNK)


def _hg_edge(nt):
    h, t = pl.program_id(0), pl.program_id(1)
    return (h == 0) & (t == 0), (h == HG_HEADS - 1) & (t == nt - 1)


def _hg_fwd(z, lb, *, rev, ts, comm=None, name):
    s = z.shape[0]
    nt = s // ts
    nch = ts // HG_CHUNK
    fcol = HG_HEADS * (2 if rev else 1)

    def body(zq_ref, zf_ref, zv_ref, lb_ref, o_ref, st_ref, qs_ref, dec_ref, state_ref):
        @pl.when(pl.program_id(1) == 0)
        def _():
            state_ref[...] = jnp.zeros_like(state_ref)

        consts = _hg_consts(rev)
        dots = _hg_dots(False, rev)
        lbv = lb_ref[...]
        local = {}
        for c in range(nch):
            rows = _chunk_rows(c)
            zv = zv_ref[rows, :]
            intra, qs, kd, dec = _hg_local(zq_ref[rows, :], zf_ref[rows, :], zv, lbv, consts, dots)
            qs = qs.astype(BF16)
            qs_ref[rows, :] = qs
            dec_ref[c] = dec
            local[c] = (intra, qs, dec, dots[2](zv, kd))
        st = state_ref[...]
        for c in _time_order(nch, rev):
            intra, qs, dec, upd = local[c]
            st_ref[c] = st.astype(BF16)
            o_ref[_chunk_rows(c), :] = intra + _bdot(qs, st, ((1,), (1,)))
            st = dec * st + upd
        state_ref[...] = st

    trow = (lambda t: nt - 1 - t) if rev else (lambda t: t)
    col = lambda off: pl.BlockSpec((ts, HG_DIM), lambda h, t: (trow(t), off + h))
    tile, mats, vecs = _hg_specs(ts, nch, trow)
    nchunks = s // HG_CHUNK
    return _call(
        body, name=name, grid=(HG_HEADS, nt),
        out_shape=(jax.ShapeDtypeStruct((s, D_MODEL), F32),
                   jax.ShapeDtypeStruct((HG_HEADS, nchunks, HG_DIM, HG_DIM), BF16),
                   jax.ShapeDtypeStruct((s, D_MODEL), BF16),
                   jax.ShapeDtypeStruct((HG_HEADS, nchunks, 1, HG_DIM), F32)),
        in_specs=[col(0), col(fcol), col(3 * HG_HEADS), pl.BlockSpec((None, 1, HG_DIM), lambda h, t: (h, 0, 0))],
        out_specs=(tile, mats, tile, vecs), args=(z, z, z, lb),
        scratch_shapes=[pltpu.VMEM((HG_DIM, HG_DIM), F32)], sem=("parallel", "arbitrary"), comm=comm,
        edge=lambda: _hg_edge(nt))


def _hg_bwd(z, lb, states, qs, dec, dout, addq, addv, *, rev, ts, comm=None, name):
    s = z.shape[0]
    nt = s // ts
    nch = ts // HG_CHUNK
    fcol = HG_HEADS * (2 if rev else 1)
    has_add = addq is not None

    def body(*refs):
        zq_ref, zf_ref, zv_ref, lb_ref, st_ref, qs_ref, dec_ref, do_ref = refs[:8]
        aq_ref, av_ref = (refs[8], refs[9]) if has_add else (None, None)
        dq_ref, df_ref, dv_ref, dlb_ref, grad_ref = refs[-5:]

        @pl.when(pl.program_id(1) == 0)
        def _():
            grad_ref[...] = jnp.zeros_like(grad_ref)

        consts = _hg_consts(rev)
        dots = _hg_dots(True, rev)
        lbv = lb_ref[...]
        prods = {c: _bdot(do_ref[_chunk_rows(c), :], qs_ref[_chunk_rows(c), :], ((0,), (0,))) for c in range(nch)}
        gleave = {}
        gr = grad_ref[...]
        for c in reversed(_time_order(nch, rev)):
            gleave[c] = gr
            gr = dec_ref[c] * gr + prods[c]
        grad_ref[...] = gr
        dlb_blk = jnp.zeros((1, HG_DIM), F32)
        for c in range(nch):
            rows = _chunk_rows(c)
            fn = lambda a, b2, c2, d2, e2: _hg_chunk(a, b2, c2, d2, e2, consts, dots)
            _, pull = jax.vjp(fn, zq_ref[rows, :], zf_ref[rows, :], zv_ref[rows, :], lbv, st_ref[c].astype(F32))
            dq, df, dv, dlb, _ = pull((do_ref[rows, :], gleave[c]))
            if has_add:
                dq = dq + aq_ref[rows, :]
                dv = dv + av_ref[rows, :]
            dq_ref[rows, :] = dq.astype(dq_ref.dtype)
            df_ref[rows, :] = df.astype(BF16)
            dv_ref[rows, :] = dv.astype(dv_ref.dtype)
            dlb_blk = dlb_blk + dlb

        @pl.when(pl.program_id(1) == 0)
        def _():
            dlb_ref[...] = dlb_blk

        @pl.when(pl.program_id(1) > 0)
        def _():
            dlb_ref[...] += dlb_blk

    trow = (lambda t: t) if rev else (lambda t: nt - 1 - t)
    col = lambda off: pl.BlockSpec((ts, HG_DIM), lambda h, t: (trow(t), off + h))
    tile, mats, vecs = _hg_specs(ts, nch, trow)
    in_specs = [col(0), col(fcol), col(3 * HG_HEADS), pl.BlockSpec((None, 1, HG_DIM), lambda h, t: (h, 0, 0)),
                mats, tile, vecs, tile]
    args = [z, z, z, lb, states, qs, dec, dout]
    if has_add:
        in_specs += [tile, tile]
        args += [addq, addv]
    act = lambda dt: jax.ShapeDtypeStruct((s, D_MODEL), dt)
    sums = BF16 if has_add else F32
    return _call(
        body, name=name, grid=(HG_HEADS, nt),
        out_shape=(act(sums), act(BF16), act(sums), jax.ShapeDtypeStruct((HG_HEADS, 1, HG_DIM), F32)),
        in_specs=in_specs,
        out_specs=(tile, tile, tile, pl.BlockSpec((None, 1, HG_DIM), lambda h, t: (h, 0, 0))), args=tuple(args),
        scratch_shapes=[pltpu.VMEM((HG_DIM, HG_DIM), F32)], sem=("parallel", "arbitrary"), comm=comm,
        edge=lambda: _hg_edge(nt))


def _hg_post(of, ob, z, norm_g, *, tm, name):
    s = of.shape[0]

    def body(of_ref, ob_ref, gate_ref, ng_ref, y_ref):
        gn = ng_ref[...]
        for h in range(HG_HEADS):
            ln = slice(HG_DIM * h, HG_DIM * (h + 1))
            o = of_ref[:, ln] + ob_ref[:, ln]
            r = lax.rsqrt(jnp.mean(o * o, axis=-1, keepdims=True) + LN_EPS)
            gt = gate_ref[:, ln]
            y_ref[:, ln] = (o * r * gn * gt * _sigmoid(gt)).astype(BF16)

    row = lambda i: (i, 0)
    return pl.pallas_call(
        body, name=name, grid=(s // tm,),
        out_shape=jax.ShapeDtypeStruct((s, D_MODEL), BF16),
        in_specs=[pl.BlockSpec((tm, D_MODEL), row), pl.BlockSpec((tm, D_MODEL), row),
                  pl.BlockSpec((tm, D_MODEL), lambda i: (i, 4)), pl.BlockSpec((1, HG_DIM), lambda i: (0, 0))],
        out_specs=pl.BlockSpec((tm, D_MODEL), row),
        compiler_params=_params(("parallel",)),
    )(of, ob, z, norm_g)


def _hg_post_bwd(dy, of, ob, z, norm_g, *, tm, name):
    s = of.shape[0]

    def body(dy_ref, of_ref, ob_ref, gate_ref, ng_ref, do_ref, dgate_ref, dng_ref):
        gn = ng_ref[...]
        tot = jnp.zeros((1, HG_DIM), F32)
        for h in range(HG_HEADS):
            ln = slice(HG_DIM * h, HG_DIM * (h + 1))
            d = dy_ref[:, ln].astype(F32)
            o = of_ref[:, ln] + ob_ref[:, ln]
            r = lax.rsqrt(jnp.mean(o * o, axis=-1, keepdims=True) + LN_EPS)
            ohat = o * r
            gt = gate_ref[:, ln]
            sg = _sigmoid(gt)
            don = d * gt * sg
            dgate_ref[:, ln] = (d * ohat * gn * sg * (1.0 + gt * (1.0 - sg))).astype(BF16)
            tot = tot + jnp.sum(don * ohat, axis=0, keepdims=True)
            dohat = don * gn
            do_ref[:, ln] = r * (dohat - ohat * jnp.mean(dohat * ohat, axis=-1, keepdims=True))

        @pl.when(pl.program_id(0) == 0)
        def _():
            dng_ref[...] = tot

        @pl.when(pl.program_id(0) > 0)
        def _():
            dng_ref[...] += tot

    row = lambda i: (i, 0)
    return pl.pallas_call(
        body, name=name, grid=(s // tm,),
        out_shape=(jax.ShapeDtypeStruct((s, D_MODEL), F32), jax.ShapeDtypeStruct((s, D_MODEL), BF16),
                   jax.ShapeDtypeStruct((1, HG_DIM), F32)),
        in_specs=[pl.BlockSpec((tm, D_MODEL), row), pl.BlockSpec((tm, D_MODEL), row), pl.BlockSpec((tm, D_MODEL), row),
                  pl.BlockSpec((tm, D_MODEL), lambda i: (i, 4)), pl.BlockSpec((1, HG_DIM), lambda i: (0, 0))],
        out_specs=(pl.BlockSpec((tm, D_MODEL), row), pl.BlockSpec((tm, D_MODEL), row),
                   pl.BlockSpec((1, HG_DIM), lambda i: (0, 0))),
        compiler_params=_params(("arbitrary",)),
    )(dy, of, ob, z, norm_g)


def _lb_fwd(logits, *, name):
    w = logits.shape[1]

    def body(l_ref, o_ref):
        lg = l_ref[...]
        e = jnp.exp(lg - jnp.max(lg, axis=0, keepdims=True))
        sm = e / jnp.sum(e, axis=0, keepdims=True)
        o_ref[0:1, :] = sm[1:2]
        o_ref[1:2, :] = sm[1:2] + sm[2:3] + sm[3:4]

    return pl.pallas_call(body, name=name, out_shape=jax.ShapeDtypeStruct((2, w), F32))(logits)


def _lb_bwd(logits, dlb, *, name):
    w = logits.shape[1]

    def body(l_ref, d_ref, o_ref):
        lg = l_ref[...]
        e = jnp.exp(lg - jnp.max(lg, axis=0, keepdims=True))
        sm = e / jnp.sum(e, axis=0, keepdims=True)
        d1, d3 = d_ref[0:1, :], d_ref[1:2, :]
        dot = sm[1:2] * (d1 + d3) + (sm[2:3] + sm[3:4]) * d3
        o_ref[0:1, :] = -sm[0:1] * dot
        o_ref[1:2, :] = sm[1:2] * (d1 + d3 - dot)
        o_ref[2:3, :] = sm[2:3] * (d3 - dot)
        o_ref[3:4, :] = sm[3:4] * (d3 - dot)

    return pl.pallas_call(body, name=name, out_shape=jax.ShapeDtypeStruct((4, w), F32))(logits, dlb)


def _adamw(w, g, m, v, *, tr, g_off=0, name):
    rows = w.shape[0]
    parts = g.ndim == 3
    c1 = 1.0 / (1.0 - ADAM_B1 ** ADAM_STEP)
    c2 = 1.0 / (1.0 - ADAM_B2 ** ADAM_STEP)

    def body(w_ref, g_ref, m_ref, v_ref, go_ref, d_ref, mo_ref, vo_ref):
        if parts:
            gg = g_ref[0].astype(F32)
            for i in range(1, N_DEV):
                gg = gg + g_ref[i].astype(F32)
        else:
            gg = g_ref[...]
        mm = ADAM_B1 * m_ref[...] + (1.0 - ADAM_B1) * gg
        vv = ADAM_B2 * v_ref[...] + (1.0 - ADAM_B2) * (gg * gg)
        go_ref[...] = gg
        mo_ref[...] = mm
        vo_ref[...] = vv
        d_ref[...] = -ADAM_LR * ((mm * c1) / (jnp.sqrt(vv * c2) + ADAM_EPS) + ADAM_WD * w_ref[...])

    tile = pl.BlockSpec((tr, D_MODEL), lambda i: (i, 0))
    gspec = pl.BlockSpec((N_DEV, tr, D_MODEL), lambda i: (0, i + g_off // tr, 0)) if parts else tile
    out = jax.ShapeDtypeStruct((rows, D_MODEL), F32)
    return pl.pallas_call(
        body, name=name, grid=(rows // tr,),
        out_shape=(out, out, out, out),
        in_specs=[tile, gspec, tile, tile], out_specs=(tile, tile, tile, tile),
        compiler_params=_params(("parallel",)),
    )(w, g, m, v)


def _sum8(parts, *, name):
    def body(p_ref, o_ref):
        tot = p_ref[0]
        for i in range(1, N_DEV):
            tot = tot + p_ref[i]
        o_ref[...] = tot

    return pl.pallas_call(body, name=name, out_shape=jax.ShapeDtypeStruct(parts.shape[1:], parts.dtype))(parts)


def _layer_params(i):
    j = i // 2
    mix = [("att_w_qkv", j, 1), ("att_w_o", j, 0)] if i % 2 == 0 else [("hgrn_w_in", j, 1), ("hgrn_w_o", j, 0)]
    return mix + [("ffn_w_in", i, 1), ("ffn_w_out", i, 0), ("ple_w_gate", i, 0), ("ple_w_proj", i, 1)]


def _pack_local(tree, params):
    return jnp.concatenate([tree[n][j].reshape(-1, D_MODEL) for n, j, _ in params], axis=0)


def _unpack_local(packed, params, like):
    out, r = {}, 0
    for n, _, _ in params:
        shp = like[n].shape[1:]
        k = shp[0] * shp[1] // D_MODEL
        out[n] = packed[r:r + k].reshape(shp)
        r += k
    return out


def _unpack_gathered(gathered, i, like):
    out, r = {}, 0
    for n, _, ax in _layer_params(i):
        shp = like[n].shape[1:]
        k = shp[0] * shp[1] // D_MODEL
        t = gathered[:, r:r + k].reshape((N_DEV,) + shp)
        out[n] = (jnp.moveaxis(t, 0, 1).reshape(shp[0], N_DEV * shp[1]) if ax == 1
                  else t.reshape(N_DEV * shp[0], shp[1]))
        r += k
    return out


def _pack_full(grads, params, like):
    cols = []
    for n, _, ax in params:
        shp = like[n].shape[1:]
        t = (jnp.moveaxis(grads[n].reshape(shp[0], N_DEV, shp[1]), 1, 0) if ax == 1
             else grads[n].reshape(N_DEV, shp[0], shp[1]))
        cols.append(t.reshape(N_DEV, -1, D_MODEL).astype(BF16))
    return jnp.concatenate(cols, axis=1)


def _row_tile(rows):
    return max(t for t in range(16, 257, 16) if rows % t == 0)


SMALL_ROWS = 24


def _pad_row(a):
    flat = a.reshape(1, -1)
    return jnp.pad(flat, ((0, 0), (0, D_MODEL - flat.shape[1])))


def _tile(n, pref):
    return min(n, pref)


def kernel(x, p, att_w_qkv, att_sink, att_w_o, hgrn_w_in, hgrn_lb_logits, hgrn_norm_g, hgrn_w_o, ln_mix_g, ln_mix_b, ffn_w_in, ffn_w_out, ln_ffn_g, ln_ffn_b, ple_w_gate, ple_w_proj, loss_target, m_att_w_qkv, m_att_sink, m_att_w_o, m_hgrn_w_in, m_hgrn_lb_logits, m_hgrn_norm_g, m_hgrn_w_o, m_ln_mix_g, m_ln_mix_b, m_ffn_w_in, m_ffn_w_out, m_ln_ffn_g, m_ln_ffn_b, m_ple_w_gate, m_ple_w_proj, v_att_w_qkv, v_att_sink, v_att_w_o, v_hgrn_w_in, v_hgrn_lb_logits, v_hgrn_norm_g, v_hgrn_w_o, v_ln_mix_g, v_ln_mix_b, v_ffn_w_in, v_ffn_w_out, v_ln_ffn_g, v_ln_ffn_b, v_ple_w_gate, v_ple_w_proj):
    names = ["att_w_qkv", "att_sink", "att_w_o", "hgrn_w_in", "hgrn_lb_logits", "hgrn_norm_g", "hgrn_w_o", "ln_mix_g",
             "ln_mix_b", "ffn_w_in", "ffn_w_out", "ln_ffn_g", "ln_ffn_b", "ple_w_gate", "ple_w_proj"]
    w = dict(zip(names, (att_w_qkv, att_sink, att_w_o, hgrn_w_in, hgrn_lb_logits, hgrn_norm_g, hgrn_w_o, ln_mix_g,
                         ln_mix_b, ffn_w_in, ffn_w_out, ln_ffn_g, ln_ffn_b, ple_w_gate, ple_w_proj)))
    mom = dict(zip(names, (m_att_w_qkv, m_att_sink, m_att_w_o, m_hgrn_w_in, m_hgrn_lb_logits, m_hgrn_norm_g, m_hgrn_w_o,
                           m_ln_mix_g, m_ln_mix_b, m_ffn_w_in, m_ffn_w_out, m_ln_ffn_g, m_ln_ffn_b, m_ple_w_gate,
                           m_ple_w_proj)))
    var = dict(zip(names, (v_att_w_qkv, v_att_sink, v_att_w_o, v_hgrn_w_in, v_hgrn_lb_logits, v_hgrn_norm_g, v_hgrn_w_o,
                           v_ln_mix_g, v_ln_mix_b, v_ffn_w_in, v_ffn_w_out, v_ln_ffn_g, v_ln_ffn_b, v_ple_w_gate,
                           v_ple_w_proj)))
    s = x.shape[1]
    me = 4 * lax.axis_index("x") + 2 * lax.axis_index("y") + lax.axis_index("c")
    tm = _tile(s, 512)
    tbig = _tile(s, 1024)
    ts = _tile(s // 2, 2048)
    x0 = x.reshape(s, D_MODEL)
    target = loss_target.reshape(s, D_MODEL)
    pl_in = p.reshape(DEPTH, s, PLE_DIM)

    w_rows = [_pack_local(w, _layer_params(i)).astype(BF16) for i in range(DEPTH)]
    full = _unpack_gathered(_gather(w_rows[0], name="gather_weights"), 0, w)
    lb_rows = jnp.pad(hgrn_lb_logits.reshape(8, HG_DIM), ((0, 0), (0, D_MODEL - HG_DIM)))
    lb_all = _gather(lb_rows, name="gather_lb")[:, :, :HG_DIM]
    logits_full = jnp.moveaxis(lb_all, 0, 1).reshape(DEPTH, 2 * D_MODEL)
    lb = _lb_fwd(logits_full, name="lb_fwd")
    cos, sin = _rope_tables(s)

    saved = []
    xf, xb = x0, x0
    for i in range(DEPTH):
        j = i // 2
        sv = {"x": xf, "xb": xb, "w": full}
        nxt = (w_rows[i + 1], True) if i + 1 < DEPTH else None
        if i % 2 == 0:
            z = _mm(xb, full["att_w_qkv"], tm=tbig, tn=512, tk=D_MODEL, name="att_in")
            o, *more = _att_fwd(z, w["att_sink"][j], cos, sin, comm=nxt, name="att_fwd")
            w_o = full["att_w_o"]
        else:
            z = _mm(xb, full["hgrn_w_in"], tm=tbig, tn=1024, tk=D_MODEL, name="hgrn_in")
            lbl = lb[j].reshape(2, HG_HEADS, 1, HG_DIM)
            of, st_f, qs_f, dec_f, *more = _hg_fwd(z, lbl[0], rev=False, ts=ts, comm=nxt, name="hgrn_fwd")
            ob, st_b, qs_b, dec_b = _hg_fwd(z, lbl[1], rev=True, ts=ts, name="hgrn_fwd_rev")
            o = _hg_post(of, ob, z, w["hgrn_norm_g"][j].reshape(1, HG_DIM), tm=tm, name="hgrn_post")
            w_o = full["hgrn_w_o"]
            sv.update(of=of, ob=ob, st_f=st_f, st_b=st_b, lbl=lbl, qs_f=qs_f, qs_b=qs_b, dec_f=dec_f, dec_b=dec_b)
        sv.update(z=z, o=o)
        g1, b1 = w["ln_mix_g"][i:i + 1], w["ln_mix_b"][i:i + 1]
        pre1, x1b = _proj_ln(o, w_o, xf, g1, b1, tm=tm, name="mix_out_ln")
        gg, uu, act = _ffn_in(x1b, full["ffn_w_in"], tm=tm, tn=FF_TILE, name="ffn_in")
        pre2, x2b, xf, xb = _ffn_out_ple(act, full["ffn_w_out"], pre1, g1, b1, w["ln_ffn_g"][i:i + 1],
                                         w["ln_ffn_b"][i:i + 1], pl_in[i], full["ple_w_gate"], full["ple_w_proj"], tm=tm,
                                         name="ffn_out_ple")
        sv.update(pre1=pre1, x1b=x1b, g=gg, u=uu, act=act, pre2=pre2, x2b=x2b)
        saved.append(sv)
        if nxt is not None:
            full = _unpack_gathered(more[0], i + 1, w)

    dx, loss_blk = _loss_head(xf, target, tm=tm, name="loss_head")
    loss = lax.psum(loss_blk[0, 0], AXES)

    small = {n: [None] * DEPTH for n in ("ln_mix_g", "ln_mix_b", "ln_ffn_g", "ln_ffn_b")}
    dlb_rows = [None] * 4
    dnorm, dsink = [None] * 2, [None] * 2
    recv_late, recv_early = [None] * DEPTH, [None] * DEPTH
    above = None
    mmw = functools.partial(_mm, ta=True, tk=_tile(s, 2048), out_dtype=BF16)
    for i in reversed(range(DEPTH)):
        j = i // 2
        sv = saved[i]
        full, gl = sv["w"], {}
        da, dpp, dy2, dy2b, small["ln_ffn_g"][i], small["ln_ffn_b"][i] = _ple_ln_bwd(
            dx, sv["x2b"], pl_in[i], full["ple_w_gate"], full["ple_w_proj"], sv["pre2"], w["ln_ffn_g"][i:i + 1],
            tm=tm, name="ple_ln_bwd")
        gl["ple_w_gate"] = mmw(sv["x2b"], da, tm=D_MODEL, tn=D_MODEL, name="dw_ple_gate")
        gl["ple_w_proj"] = mmw(pl_in[i], dpp, tm=PLE_DIM, tn=D_MODEL, name="dw_ple_proj")
        dg, du = _ffn_bwd_act(dy2b, full["ffn_w_out"], sv["g"], sv["u"], tm=_tile(s, SUB_ROWS), tn=D_FF,
                              name="ffn_bwd_act")
        gl["ffn_w_out"] = mmw(sv["act"], dy2b, tm=FF_TILE, tn=D_MODEL, name="dw_ffn_out")
        dy1, dy1b, small["ln_mix_g"][i], small["ln_mix_b"][i] = _dx_from_pieces(
            [dg, du], full["ffn_w_in"], dy2, tm=_tile(s, SUB_ROWS), ln=(sv["pre1"], w["ln_mix_g"][i:i + 1]),
            name="ffn_bwd_x_ln")
        gl["ffn_w_in"] = mmw(sv["x1b"], [dg, du], tm=D_MODEL, tn=FF_TILE, name="dw_ffn_in")
        n_out, n_inw = ("att_w_o", "att_w_qkv") if i % 2 == 0 else ("hgrn_w_o", "hgrn_w_in")
        do = _mm(dy1b, full[n_out], tm=tbig, tn=D_MODEL, tk=D_MODEL, tb=True, out_dtype=BF16, name="mix_out_bwd")
        gl[n_out] = mmw(sv["o"], dy1b, tm=D_MODEL, tn=D_MODEL, name="dw_mix_out")
        early = _pack_full(gl, _layer_params(i)[1:], w)
        comm = (early if above is None else jnp.concatenate([above, early], axis=1), False)
        if i % 2 == 0:
            dzq, part, dsk, *more = _att_bwd(sv["z"], do, w["att_sink"][j], cos, sin, comm=comm, name="att_bwd")
            dz = [dzq, _att_bwd_kv(part, cos, sin, name="att_bwd_kv")]
            dsink[j] = dsk[:, 0]
        else:
            dsum, dgate, dnorm[j] = _hg_post_bwd(do, sv["of"], sv["ob"], sv["z"], w["hgrn_norm_g"][j].reshape(1, HG_DIM),
                                                 tm=tm, name="hgrn_post_bwd")
            dq1, df1, dv1, dlb1, *more = _hg_bwd(sv["z"], sv["lbl"][0], sv["st_f"], sv["qs_f"], sv["dec_f"], dsum, None,
                                                 None, rev=False, ts=ts, comm=comm, name="hgrn_bwd")
            dq2, df2, dv2, dlb2 = _hg_bwd(sv["z"], sv["lbl"][1], sv["st_b"], sv["qs_b"], sv["dec_b"], dsum, dq1, dv1,
                                          rev=True, ts=ts, name="hgrn_bwd_rev")
            dz = [dq2, df1, df2, dv2, dgate]
            dlb_rows[2 * j] = dlb1.reshape(1, D_MODEL)
            dlb_rows[2 * j + 1] = dlb2.reshape(1, D_MODEL)
        if above is not None:
            recv_late[i + 1] = (more[0], 0)
        recv_early[i] = (more[0], 0 if above is None else above.shape[1])
        dx = _dx_from_pieces(dz, full[n_inw], dy1, tm=tm, name="mix_in_bwd")
        gl[n_inw] = mmw(sv["xb"], dz, tm=D_MODEL, tn=512, name="dw_mix_in")
        above = _pack_full(gl, _layer_params(i)[:1], w)
    grad_x = dx.reshape(x.shape)
    recv_late[0] = (_exchange(above, name="exchange_grads"), 0)

    big_out = [{n: [None] * w[n].shape[0] for n, _ in BIG} for _ in range(4)]
    for i in range(DEPTH):
        for params, (got, off) in ((_layer_params(i)[:1], recv_late[i]), (_layer_params(i)[1:], recv_early[i])):
            w_part = _pack_local(w, params)
            outs = _adamw(w_part, got, _pack_local(mom, params), _pack_local(var, params),
                          tr=_row_tile(math.gcd(w_part.shape[0], off)), g_off=off, name="adamw_big")
            for kind, packed in enumerate(outs):
                for (n, j, _), piece in zip(params, _unpack_local(packed, params, w).values()):
                    big_out[kind][n][j] = piece
    big_out = [{n: jnp.stack(v) for n, v in kind.items()} for kind in big_out]

    small_rows = jnp.concatenate(
        [jnp.concatenate(small[n], axis=0) for n in ("ln_mix_g", "ln_mix_b", "ln_ffn_g", "ln_ffn_b")] + dlb_rows
        + [_pad_row(jnp.stack(dnorm)), _pad_row(jnp.stack(dsink)), jnp.zeros((2, D_MODEL), F32)], axis=0)
    small_all = _gather(small_rows, name="gather_small")
    lbw, lbm, lbv = (t.reshape(4, 2 * HG_DIM) for t in (hgrn_lb_logits, mom["hgrn_lb_logits"], var["hgrn_lb_logits"]))
    summed = _sum8(small_all, name="sum_small")
    dlb_mine = lax.dynamic_slice_in_dim(summed[16:20].reshape(2, 2, HG_HEADS, HG_DIM), me, 1, axis=2)
    dlogits = _lb_bwd(lbw, dlb_mine.reshape(2, 2 * HG_DIM), name="lb_bwd")

    def small_pack(ln4, lbt, ng, sk):
        return jnp.concatenate([ln4[n] for n in ("ln_mix_g", "ln_mix_b", "ln_ffn_g", "ln_ffn_b")]
                               + [_pad_row(lbt), _pad_row(ng), _pad_row(sk), jnp.zeros((5, D_MODEL), F32)], axis=0)

    g_small = jnp.concatenate([summed[:16], _pad_row(dlogits), summed[20:22], jnp.zeros((5, D_MODEL), F32)], axis=0)
    souts = _adamw(small_pack(w, lbw, w["hgrn_norm_g"], w["att_sink"]), g_small,
                   small_pack(mom, lbm, mom["hgrn_norm_g"], mom["att_sink"]),
                   small_pack(var, lbv, var["hgrn_norm_g"], var["att_sink"]), tr=SMALL_ROWS, name="adamw_small")

    def small_unpack(t):
        out = {n: t[4 * k:4 * k + 4] for k, n in enumerate(("ln_mix_g", "ln_mix_b", "ln_ffn_g", "ln_ffn_b"))}
        out["hgrn_lb_logits"] = t[16].reshape(hgrn_lb_logits.shape)
        out["hgrn_norm_g"] = t[17, :2 * HG_DIM].reshape(hgrn_norm_g.shape)
        out["att_sink"] = t[18, :2 * N_Q_HEADS].reshape(att_sink.shape)
        return out

    result = [loss, grad_x]
    for big_t, small_t in zip(big_out, souts):
        merged = dict(big_t)
        merged.update(small_unpack(small_t))
        result += [merged[n] for n in names]
    return tuple(result)
```

```python
import functools
import math

import jax
import jax.numpy as jnp
from jax import lax
from jax.experimental import pallas as pl
from jax.experimental.pallas import tpu as pltpu

F32 = jnp.float32
BF16 = jnp.bfloat16

D_MODEL = 1024
DEPTH = 4
HEAD_DIM = 64
N_Q_HEADS = 16
N_KV_HEADS = 4
GROUP = 4
KV_DIM = 256
ATT_BLOCK = 128
ROPE_DIM = 16
ROPE_THETA = 500000.0
HG_HEADS = 8
HG_DIM = 128
HG_CHUNK = 64
HG_SUB = 16
D_FF = 2816
FF_TILE = 1408
SUB_ROWS = 256
PLE_DIM = 256
ALPHA = (2 * DEPTH) ** 0.25
LN_EPS = 1e-5
ADAM_LR, ADAM_B1, ADAM_B2, ADAM_EPS, ADAM_WD, ADAM_STEP = 0.001, 0.9, 0.999, 1e-08, 0.01, 10

N_DEV = 8
LANES = 128
VMEM_LIMIT = 52 * 1024 * 1024
NEG = -1e30
MESH = pl.DeviceIdType.MESH
AXES = ("x", "y", "c")

BIG = (("att_w_qkv", 2), ("att_w_o", 1), ("hgrn_w_in", 2), ("hgrn_w_o", 1), ("ffn_w_in", 2), ("ffn_w_out", 1),
       ("ple_w_gate", 1), ("ple_w_proj", 2))


def _params(sem=None, vmem=VMEM_LIMIT):
    return pltpu.CompilerParams(dimension_semantics=sem, vmem_limit_bytes=vmem)


def _sigmoid(x):
    return jax.nn.sigmoid(x)


def _direct_copies(src_ref, out_ref, send_sems, recv_sems, local_sem, gather, arrivals):
    x, y, c = lax.axis_index("x"), lax.axis_index("y"), lax.axis_index("c")
    me = 4 * x + 2 * y + c
    mine = (lambda j: src_ref) if gather else (lambda j: src_ref.at[j])
    pairs = []
    for k in range(1, N_DEV):
        px, py, pc = x ^ (k >> 2), y ^ ((k >> 1) & 1), c ^ (k & 1)
        peer = 4 * px + 2 * py + pc
        send = pltpu.make_async_remote_copy(
            src_ref=mine(peer), dst_ref=out_ref.at[me], send_sem=send_sems.at[k], recv_sem=recv_sems.at[k],
            device_id=(px, py, pc), device_id_type=MESH)
        arrival = pltpu.make_async_remote_copy(
            src_ref=mine(peer), dst_ref=out_ref.at[peer], send_sem=send_sems.at[k], recv_sem=recv_sems.at[k],
            device_id=(x, y, c), device_id_type=MESH) if arrivals else None
        pairs.append((send, arrival))
    return pltpu.make_async_copy(mine(me), out_ref.at[me], local_sem), pairs


def _direct_start(*refs, gather):
    local, pairs = _direct_copies(*refs, gather, False)
    local.start()
    for send, _ in pairs:
        send.start()


def _direct_wait(*refs, gather):
    local, pairs = _direct_copies(*refs, gather, True)
    for send, arrival in pairs:
        send.wait_send()
        arrival.wait_recv()
    local.wait()


COMM_SCRATCH = [pltpu.SemaphoreType.DMA((N_DEV,)), pltpu.SemaphoreType.DMA((N_DEV,)), pltpu.SemaphoreType.DMA]


def _exchange(src, *, gather=False, name):
    def body(*refs):
        _direct_start(*refs, gather=gather)
        _direct_wait(*refs, gather=gather)

    blk = tuple(src.shape) if gather else tuple(src.shape[1:])
    return pl.pallas_call(
        body, name=name,
        out_shape=jax.ShapeDtypeStruct((N_DEV,) + blk, src.dtype),
        in_specs=[pl.BlockSpec(memory_space=pltpu.HBM)],
        out_specs=pl.BlockSpec(memory_space=pltpu.HBM),
        scratch_shapes=COMM_SCRATCH,
    )(src)


def _call(body, *, name, grid, out_shape, in_specs, out_specs, args, scratch_shapes=(), sem, comm=None, edge=None):
    out_shape, out_specs = tuple(out_shape), tuple(out_specs)
    if comm is None:
        return pl.pallas_call(body, name=name, grid=grid, out_shape=out_shape, in_specs=list(in_specs),
                              out_specs=out_specs, scratch_shapes=list(scratch_shapes),
                              compiler_params=_params(sem))(*args)
    src, gather = comm
    n_in, n_out, n_scr = len(args), len(out_shape), len(scratch_shapes)
    blk = tuple(src.shape) if gather else tuple(src.shape[1:])
    hbm = pl.BlockSpec(memory_space=pltpu.HBM)

    def carrying(*refs):
        ins, src_ref = refs[:n_in], refs[n_in]
        outs, dst_ref = refs[n_in + 1:n_in + 1 + n_out], refs[n_in + 1 + n_out]
        own = refs[n_in + 2 + n_out:n_in + 2 + n_out + n_scr]
        comm_refs = (src_ref, dst_ref) + tuple(refs[n_in + 2 + n_out + n_scr:])
        first, last = edge()

        @pl.when(first)
        def _():
            _direct_start(*comm_refs, gather=gather)

        body(*ins, *outs, *own)

        @pl.when(last)
        def _():
            _direct_wait(*comm_refs, gather=gather)

    return pl.pallas_call(
        carrying, name=name, grid=grid,
        out_shape=out_shape + (jax.ShapeDtypeStruct((N_DEV,) + blk, src.dtype),),
        in_specs=list(in_specs) + [hbm], out_specs=out_specs + (hbm,),
        scratch_shapes=list(scratch_shapes) + COMM_SCRATCH,
        compiler_params=_params(("arbitrary",) * len(grid)),
    )(*args, src)


def _gather(src, *, name):
    def body(src_ref, out_ref, send_sems, recv_sems, local_sem):
        x, y, c = lax.axis_index("x"), lax.axis_index("y"), lax.axis_index("c")
        sibling = (x, y, 1 - c)
        chips = [(1 - x, y), (x, 1 - y), (1 - x, 1 - y)]

        def rows(px, py, pc):
            return out_ref.at[4 * px + 2 * py + pc]

        def copy(k, block, to, from_src=False):
            return pltpu.make_async_remote_copy(
                src_ref=src_ref if from_src else rows(*block), dst_ref=rows(*block), send_sem=send_sems.at[k],
                recv_sem=recv_sems.at[k], device_id=to, device_id_type=MESH)

        me = (x, y, c)
        mine = pltpu.make_async_copy(src_ref, rows(*me), local_sem)
        mine.start()
        first = [copy(0, me, sibling, from_src=True)]
        first += [copy(1 + j, me, (*chip, c), from_src=True) for j, chip in enumerate(chips)]
        for cp in first:
            cp.start()
        passed = [copy(4 + j, (*chip, c), sibling) for j, chip in enumerate(chips)]
        for j, chip in enumerate(chips):
            copy(1 + j, (*chip, c), me).wait_recv()
            passed[j].start()
        copy(0, sibling, me).wait_recv()
        for j, chip in enumerate(chips):
            copy(4 + j, (*chip, 1 - c), me).wait_recv()
        for cp in first + passed:
            cp.wait_send()
        mine.wait()

    return pl.pallas_call(
        body, name=name,
        out_shape=jax.ShapeDtypeStruct((N_DEV,) + tuple(src.shape), src.dtype),
        in_specs=[pl.BlockSpec(memory_space=pltpu.HBM)],
        out_specs=pl.BlockSpec(memory_space=pltpu.HBM),
        scratch_shapes=[pltpu.SemaphoreType.DMA((7,)), pltpu.SemaphoreType.DMA((7,)), pltpu.SemaphoreType.DMA],
    )(src)


def _mm(a, b, *, tm, tn, tk, ta=False, tb=False, out_dtype=F32, name):
    b_list = list(b) if isinstance(b, (list, tuple)) else [b]
    assert not (tb and len(b_list) > 1)
    m, kdim = (a.shape[1], a.shape[0]) if ta else a.shape
    joff = [0]
    for piece in b_list:
        joff.append(joff[-1] + (piece.shape[0] if tb else piece.shape[1]) // tn)
    nk, n = kdim // tk, joff[-1] * tn
    dims = (((0 if ta else 1,), (1 if tb else 0,)), ((), ()))

    def mine(j, p):
        return (j >= joff[p]) & (j < joff[p + 1])

    def body(*refs):
        a_ref, b_refs, o_ref = refs[0], refs[1:1 + len(b_list)], refs[1 + len(b_list)]
        acc_ref = refs[-1] if nk > 1 else None
        j, k = pl.program_id(1), pl.program_id(2)
        for p, b_ref in enumerate(b_refs):
            def step(b_ref=b_ref):
                part = lax.dot_general(a_ref[...].astype(BF16), b_ref[...].astype(BF16), dims,
                                       preferred_element_type=F32)
                if nk == 1:
                    o_ref[...] = part.astype(out_dtype)
                else:
                    _accumulate(acc_ref, part, k == 0)

            if len(b_list) == 1:
                step()
            else:
                pl.when(mine(j, p))(step)
        if nk > 1:
            @pl.when(k == nk - 1)
            def _():
                o_ref[...] = acc_ref[...].astype(out_dtype)

    def b_spec(p):
        jj = lambda j: jnp.clip(j - joff[p], 0, joff[p + 1] - joff[p] - 1)
        kk = (lambda j, k: k) if len(b_list) == 1 else (lambda j, k: jnp.where(mine(j, p), k, 0))
        return (pl.BlockSpec((tn, tk), lambda i, j, k: (jj(j), kk(j, k))) if tb
                else pl.BlockSpec((tk, tn), lambda i, j, k: (kk(j, k), jj(j))))

    a_spec = pl.BlockSpec((tk, tm), lambda i, j, k: (k, i)) if ta else pl.BlockSpec((tm, tk), lambda i, j, k: (i, k))
    return pl.pallas_call(
        body, name=name, grid=(m // tm, n // tn, nk),
        out_shape=jax.ShapeDtypeStruct((m, n), out_dtype),
        in_specs=[a_spec] + [b_spec(p) for p in range(len(b_list))],
        out_specs=pl.BlockSpec((tm, tn), lambda i, j, k: (i, j)),
        scratch_shapes=[pltpu.VMEM((tm, tn), F32)] if nk > 1 else [],
        compiler_params=_params(("parallel", "parallel", "arbitrary")),
    )(a, *b_list)


def _dx_from_pieces(pieces, w, add, *, tm, ln=None, name):
    s = pieces[0].shape[0]
    widths = [p.shape[1] for p in pieces]

    def body(*refs):
        p_refs, (w_ref, add_ref) = refs[:len(pieces)], refs[len(pieces):len(pieces) + 2]
        rest = refs[len(pieces) + 2:]
        pg = jnp.zeros((1, D_MODEL), F32)
        pb = jnp.zeros((1, D_MODEL), F32)
        for rs in _row_parts(tm):
            r = ALPHA * add_ref[rs, :]
            off = 0
            for p_ref, width in zip(p_refs, widths):
                r = r + lax.dot_general(p_ref[rs, :], w_ref[:, off:off + width], (((1,), (1,)), ((), ())),
                                        preferred_element_type=F32)
                off += width
            if ln is None:
                rest[0][rs, :] = r
            else:
                dy, qg, qb = _ln_bwd_rows(r, rest[0][rs, :], rest[1][...])
                rest[2][rs, :] = dy
                rest[3][rs, :] = dy.astype(BF16)
                pg, pb = pg + qg, pb + qb
        if ln is not None:
            _accumulate(rest[4], pg, pl.program_id(0) == 0)
            _accumulate(rest[5], pb, pl.program_id(0) == 0)

    row = lambda i: (i, 0)
    tile = pl.BlockSpec((tm, D_MODEL), row)
    vec = pl.BlockSpec((1, D_MODEL), lambda i: (0, 0))
    in_specs = ([pl.BlockSpec((tm, width), row) for width in widths]
                + [pl.BlockSpec((D_MODEL, sum(widths)), lambda i: (0, 0)), tile])
    args = list(pieces) + [w, add]
    out_shape, out_specs = jax.ShapeDtypeStruct((s, D_MODEL), F32), tile
    if ln is not None:
        in_specs += [tile, vec]
        args += list(ln)
        out_shape = (jax.ShapeDtypeStruct((s, D_MODEL), F32), jax.ShapeDtypeStruct((s, D_MODEL), BF16),
                     jax.ShapeDtypeStruct((1, D_MODEL), F32), jax.ShapeDtypeStruct((1, D_MODEL), F32))
        out_specs = (tile, tile, vec, vec)
    return pl.pallas_call(
        body, name=name, grid=(s // tm,), out_shape=out_shape, in_specs=in_specs, out_specs=out_specs,
        compiler_params=_params(("arbitrary",) if ln is not None else ("parallel",)),
    )(*args)


def _ln_bwd_rows(do, y, g):
    mu = jnp.mean(y, axis=-1, keepdims=True)
    yc = y - mu
    var = jnp.mean(yc * yc, axis=-1, keepdims=True)
    rstd = lax.rsqrt(var + LN_EPS)
    xhat = yc * rstd
    dxhat = do * g
    dy = rstd * (dxhat - jnp.mean(dxhat, axis=-1, keepdims=True) - xhat * jnp.mean(dxhat * xhat, axis=-1, keepdims=True))
    return dy, jnp.sum(do * xhat, axis=0, keepdims=True), jnp.sum(do, axis=0, keepdims=True)


def _accumulate(ref, val, first):
    @pl.when(first)
    def _():
        ref[...] = val

    @pl.when(jnp.logical_not(first))
    def _():
        ref[...] += val


def _layer_norm_rows(y, g, b):
    mu = jnp.mean(y, axis=-1, keepdims=True)
    yc = y - mu
    var = jnp.mean(yc * yc, axis=-1, keepdims=True)
    return yc * lax.rsqrt(var + LN_EPS) * g + b


def _proj_ln(a, w, res, g, b, *, tm, name):
    s, kdim = a.shape

    def body(a_ref, w_ref, res_ref, g_ref, b_ref, pre_ref, obf_ref):
        for rs in _row_parts(tm):
            h = jnp.dot(a_ref[rs, :], w_ref[...], preferred_element_type=F32)
            pre = ALPHA * res_ref[rs, :] + h
            pre_ref[rs, :] = pre
            obf_ref[rs, :] = _layer_norm_rows(pre, g_ref[...], b_ref[...]).astype(BF16)

    row = lambda i: (i, 0)
    fix = lambda i: (0, 0)
    return pl.pallas_call(
        body, name=name, grid=(s // tm,),
        out_shape=(jax.ShapeDtypeStruct((s, D_MODEL), F32), jax.ShapeDtypeStruct((s, D_MODEL), BF16)),
        in_specs=[pl.BlockSpec((tm, kdim), row), pl.BlockSpec((kdim, D_MODEL), fix), pl.BlockSpec((tm, D_MODEL), row),
                  pl.BlockSpec((1, D_MODEL), fix), pl.BlockSpec((1, D_MODEL), fix)],
        out_specs=(pl.BlockSpec((tm, D_MODEL), row),) * 2,
        compiler_params=_params(("parallel",)),
    )(a, w, res, g, b)


def _row_parts(tm):
    sub = min(tm, SUB_ROWS)
    return [pl.ds(r * sub, sub) for r in range(tm // sub)]


def _ffn_in(xbf, w, *, tm, tn, comm=None, name):
    s = xbf.shape[0]
    nj = D_FF // tn
    ni = s // tm

    def body(x_ref, wg_ref, wu_ref, g_ref, u_ref, act_ref):
        for rs in _row_parts(tm):
            xv = x_ref[rs, :]
            gg = jnp.dot(xv, wg_ref[...], preferred_element_type=F32)
            uu = jnp.dot(xv, wu_ref[...], preferred_element_type=F32)
            g_ref[rs, :] = gg.astype(BF16)
            u_ref[rs, :] = uu.astype(BF16)
            act_ref[rs, :] = (gg * _sigmoid(gg) * uu).astype(BF16)

    out = jax.ShapeDtypeStruct((s, D_FF), BF16)
    tile = pl.BlockSpec((tm, tn), lambda j, i: (i, j))
    return _call(
        body, name=name, grid=(nj, ni),
        out_shape=(out, out, out),
        in_specs=[pl.BlockSpec((tm, D_MODEL), lambda j, i: (i, 0)), pl.BlockSpec((D_MODEL, tn), lambda j, i: (0, j)),
                  pl.BlockSpec((D_MODEL, tn), lambda j, i: (0, j + nj))],
        out_specs=(tile, tile, tile), args=(xbf, w, w), sem=("parallel", "parallel"), comm=comm,
        edge=lambda: ((pl.program_id(0) == 0) & (pl.program_id(1) == 0),
                      (pl.program_id(0) == nj - 1) & (pl.program_id(1) == ni - 1)))


def _ffn_bwd_act(dybf, w_out, g, u, *, tm, tn, name):
    s = dybf.shape[0]

    def body(dy_ref, w_ref, g_ref, u_ref, dg_ref, du_ref):
        for rs in _row_parts(tm):
            dact = lax.dot_general(dy_ref[rs, :], w_ref[...], (((1,), (1,)), ((), ())), preferred_element_type=F32)
            gg = g_ref[rs, :].astype(F32)
            uu = u_ref[rs, :].astype(F32)
            sg = _sigmoid(gg)
            dg_ref[rs, :] = (dact * uu * sg * (1.0 + gg * (1.0 - sg))).astype(BF16)
            du_ref[rs, :] = (dact * gg * sg).astype(BF16)

    out = jax.ShapeDtypeStruct((s, D_FF), BF16)
    tile = pl.BlockSpec((tm, tn), lambda j, i: (i, j))
    return pl.pallas_call(
        body, name=name, grid=(D_FF // tn, s // tm),
        out_shape=(out, out),
        in_specs=[pl.BlockSpec((tm, D_MODEL), lambda j, i: (i, 0)), pl.BlockSpec((tn, D_MODEL), lambda j, i: (j, 0)),
                  tile, tile],
        out_specs=(tile, tile),
        compiler_params=_params(("parallel", "parallel")),
    )(dybf, w_out, g, u)


def _ffn_out_ple(act, w_out, res_pre, res_g, res_b, g, b, p, w_gate, w_proj, *, tm, name):
    s = act.shape[0]

    def body(a_ref, w_ref, res_ref, rg_ref, rb_ref, g_ref, b_ref, p_ref, wg_ref, wp_ref, pre_ref, x2bf_ref, o_ref,
             obf_ref):
        for rs in _row_parts(tm):
            res = _layer_norm_rows(res_ref[rs, :], rg_ref[...], rb_ref[...])
            pre = ALPHA * res + jnp.dot(a_ref[rs, :], w_ref[...], preferred_element_type=F32)
            x2 = _layer_norm_rows(pre, g_ref[...], b_ref[...])
            x2bf = x2.astype(BF16)
            pre_ref[rs, :] = pre
            x2bf_ref[rs, :] = x2bf
            gate = jnp.dot(x2bf, wg_ref[...], preferred_element_type=F32)
            pp = jnp.dot(p_ref[rs, :].astype(BF16), wp_ref[...], preferred_element_type=F32)
            out = x2 + _sigmoid(gate) * pp
            o_ref[rs, :] = out
            obf_ref[rs, :] = out.astype(BF16)

    row = lambda i: (i, 0)
    fix = lambda i: (0, 0)
    tile = pl.BlockSpec((tm, D_MODEL), row)
    vec = pl.BlockSpec((1, D_MODEL), fix)
    act_t = lambda dt: jax.ShapeDtypeStruct((s, D_MODEL), dt)
    return pl.pallas_call(
        body, name=name, grid=(s // tm,),
        out_shape=(act_t(F32), act_t(BF16), act_t(F32), act_t(BF16)),
        in_specs=[pl.BlockSpec((tm, D_FF), row), pl.BlockSpec((D_FF, D_MODEL), fix), tile, vec, vec, vec, vec,
                  pl.BlockSpec((tm, PLE_DIM), row), pl.BlockSpec((D_MODEL, D_MODEL), fix),
                  pl.BlockSpec((PLE_DIM, D_MODEL), fix)],
        out_specs=(tile, tile, tile, tile),
        compiler_params=_params(("parallel",)),
    )(act, w_out, res_pre, res_g, res_b, g, b, p, w_gate, w_proj)


def _ple_ln_bwd(dx3, x2bf, p, w_gate, w_proj, pre, g, *, tm, name):
    s = dx3.shape[0]

    def body(d_ref, xbf_ref, p_ref, wg_ref, wp_ref, pre_ref, g_ref, da_ref, dpp_ref, dy_ref, dybf_ref, dg_ref, db_ref):
        pg = jnp.zeros((1, D_MODEL), F32)
        pb = jnp.zeros((1, D_MODEL), F32)
        for rs in _row_parts(tm):
            d = d_ref[rs, :]
            a = jnp.dot(xbf_ref[rs, :], wg_ref[...], preferred_element_type=F32)
            pp = jnp.dot(p_ref[rs, :].astype(BF16), wp_ref[...], preferred_element_type=F32)
            sg = _sigmoid(a)
            da = (d * pp * sg * (1.0 - sg)).astype(BF16)
            da_ref[rs, :] = da
            dpp_ref[rs, :] = (d * sg).astype(BF16)
            dx2 = d + lax.dot_general(da, wg_ref[...], (((1,), (1,)), ((), ())), preferred_element_type=F32)
            dy, qg, qb = _ln_bwd_rows(dx2, pre_ref[rs, :], g_ref[...])
            dy_ref[rs, :] = dy
            dybf_ref[rs, :] = dy.astype(BF16)
            pg, pb = pg + qg, pb + qb
        _accumulate(dg_ref, pg, pl.program_id(0) == 0)
        _accumulate(db_ref, pb, pl.program_id(0) == 0)

    row = lambda i: (i, 0)
    fix = lambda i: (0, 0)
    tile = pl.BlockSpec((tm, D_MODEL), row)
    vec = pl.BlockSpec((1, D_MODEL), fix)
    act = lambda dt: jax.ShapeDtypeStruct((s, D_MODEL), dt)
    return pl.pallas_call(
        body, name=name, grid=(s // tm,),
        out_shape=(act(BF16), act(BF16), act(F32), act(BF16), jax.ShapeDtypeStruct((1, D_MODEL), F32),
                   jax.ShapeDtypeStruct((1, D_MODEL), F32)),
        in_specs=[tile, tile, pl.BlockSpec((tm, PLE_DIM), row), pl.BlockSpec((D_MODEL, D_MODEL), fix),
                  pl.BlockSpec((PLE_DIM, D_MODEL), fix), tile, vec],
        out_specs=(tile, tile, tile, tile, vec, vec),
        compiler_params=_params(("arbitrary",)),
    )(dx3, x2bf, p, w_gate, w_proj, pre, g)


def _loss_head(y, target, *, tm, name):
    s = y.shape[0]

    def body(y_ref, t_ref, dy_ref, loss_ref, acc_ref):
        err = y_ref[...] - t_ref[...]
        dy_ref[...] = err * (1.0 / D_MODEL)
        part = jnp.sum(err * err, axis=0, keepdims=True)

        @pl.when(pl.program_id(0) == 0)
        def _():
            acc_ref[...] = part

        @pl.when(pl.program_id(0) > 0)
        def _():
            acc_ref[...] += part

        @pl.when(pl.program_id(0) == pl.num_programs(0) - 1)
        def _():
            tot = jnp.sum(acc_ref[...], axis=1, keepdims=True) * (0.5 / D_MODEL)
            loss_ref[...] = jnp.broadcast_to(tot, (8, LANES))

    row = lambda i: (i, 0)
    return pl.pallas_call(
        body, name=name, grid=(s // tm,),
        out_shape=(jax.ShapeDtypeStruct((s, D_MODEL), F32), jax.ShapeDtypeStruct((8, LANES), F32)),
        in_specs=[pl.BlockSpec((tm, D_MODEL), row), pl.BlockSpec((tm, D_MODEL), row)],
        out_specs=(pl.BlockSpec((tm, D_MODEL), row), pl.BlockSpec((8, LANES), lambda i: (0, 0))),
        scratch_shapes=[pltpu.VMEM((1, D_MODEL), F32)],
        compiler_params=_params(("arbitrary",)),
    )(y, target)


def _rope_tables(s):
    inv = ROPE_THETA ** (-jnp.arange(0, ROPE_DIM, 2, dtype=F32) / ROPE_DIM)
    ang = jnp.arange(s, dtype=F32)[:, None] * inv[None, :]
    cos, sin = jnp.cos(ang), jnp.sin(ang)
    ones = jnp.ones((s, HEAD_DIM - ROPE_DIM), F32)
    c_head = jnp.concatenate([cos, cos, ones], axis=1)
    s_head = jnp.concatenate([-sin, sin, 0.0 * ones], axis=1)
    return jnp.concatenate([c_head, c_head], axis=1), jnp.concatenate([s_head, s_head], axis=1)


def _rope(v, cos, sin):
    n = v.shape[1] // LANES
    width = v.shape[1]
    cos_w = jnp.tile(cos, (1, n)) if n > 1 else cos
    sin_w = jnp.tile(sin, (1, n)) if n > 1 else sin
    dim = lax.broadcasted_iota(jnp.int32, (1, width), 1) % HEAD_DIM
    partner = jnp.where(dim < ROPE_DIM // 2, pltpu.roll(v, width - ROPE_DIM // 2, 1), pltpu.roll(v, ROPE_DIM // 2, 1))
    return v * cos_w + partner * sin_w


def _unrope(dv, cos, sin):
    n = dv.shape[1] // LANES
    width = dv.shape[1]
    cos_w = jnp.tile(cos, (1, n)) if n > 1 else cos
    sin_w = jnp.tile(sin, (1, n)) if n > 1 else sin
    t = dv * sin_w
    dim = lax.broadcasted_iota(jnp.int32, (1, width), 1) % HEAD_DIM
    partner = jnp.where(dim < ROPE_DIM // 2, pltpu.roll(t, width - ROPE_DIM // 2, 1),
                        jnp.where(dim < ROPE_DIM, pltpu.roll(t, ROPE_DIM // 2, 1), 0.0))
    return dv * cos_w + partner


def _att_mask(i, nb):
    rows = GROUP * ATT_BLOCK
    r = lax.broadcasted_iota(jnp.int32, (rows, 3 * ATT_BLOCK), 0) % ATT_BLOCK
    cidx = lax.broadcasted_iota(jnp.int32, (rows, 3 * ATT_BLOCK), 1)
    rel = r + ATT_BLOCK - cidx
    ok = (rel <= ATT_BLOCK) & (rel >= -ATT_BLOCK)
    ok = ok & ((cidx >= ATT_BLOCK) | (i > 0)) & ((cidx < 2 * ATT_BLOCK) | (i < nb - 1))
    return ok


def _att_mask_t(i, nb):
    cols = GROUP * ATT_BLOCK
    cidx = lax.broadcasted_iota(jnp.int32, (3 * ATT_BLOCK, cols), 0)
    r = lax.broadcasted_iota(jnp.int32, (3 * ATT_BLOCK, cols), 1) % ATT_BLOCK
    rel = r + ATT_BLOCK - cidx
    ok = (rel <= ATT_BLOCK) & (rel >= -ATT_BLOCK)
    return ok & ((cidx >= ATT_BLOCK) | (i > 0)) & ((cidx < 2 * ATT_BLOCK) | (i < nb - 1))


def _sink_lanes(sink_ref, h):
    cols = GROUP * ATT_BLOCK
    grp = lax.broadcasted_iota(jnp.int32, (1, cols), 1) // ATT_BLOCK
    out = jnp.zeros((1, cols), F32)
    for gq in range(GROUP):
        out = jnp.where(grp == gq, sink_ref[GROUP * h + gq], out)
    return out


def _half_mask(half):
    lane = lax.broadcasted_iota(jnp.int32, (1, LANES), 1)
    return (lane // HEAD_DIM) == half


def _stack_q(q, h):
    parts = []
    for gq in range(GROUP):
        n = GROUP * h + gq
        grp = q[:, LANES * (n // 2):LANES * (n // 2 + 1)]
        grp = jnp.where(_half_mask(n % 2), grp, 0.0)
        if n % 2 != h % 2:
            grp = pltpu.roll(grp, HEAD_DIM, 1)
        parts.append(grp)
    return jnp.concatenate(parts, axis=0)


def _unstack_q(stacked, h, acc):
    for gq in range(GROUP):
        n = GROUP * h + gq
        grp = stacked[ATT_BLOCK * gq:ATT_BLOCK * (gq + 1), :]
        grp = jnp.where(_half_mask(h % 2), grp, 0.0)
        if n % 2 != h % 2:
            grp = pltpu.roll(grp, HEAD_DIM, 1)
        acc[n // 2] = grp if acc[n // 2] is None else acc[n // 2] + grp
    return acc


def _sink_rows(sink_ref, h):
    rows = GROUP * ATT_BLOCK
    grp = lax.broadcasted_iota(jnp.int32, (rows, 1), 0) // ATT_BLOCK
    out = jnp.zeros((rows, 1), F32)
    for gq in range(GROUP):
        out = jnp.where(grp == gq, sink_ref[GROUP * h + gq], out)
    return out


def _att_probs(qs, kh, sink, valid):
    s = lax.dot_general(qs, kh, (((1,), (1,)), ((), ())), preferred_element_type=F32)
    s = jnp.where(valid, s, NEG)
    m = jnp.maximum(jnp.max(s, axis=-1, keepdims=True), sink)
    p = jnp.exp(s - m)
    es = jnp.exp(sink - m)
    den = jnp.sum(p, axis=-1, keepdims=True) + es
    inv = 1.0 / den
    return p * inv, es * inv


ATT_STEP = 4


def _att_specs(nb):
    rows = ATT_STEP * ATT_BLOCK
    prev = lambda i: (jnp.maximum(ATT_STEP * i - 1, 0), 0)
    cur = lambda i: (i, 0)
    nxt = lambda i: (jnp.minimum(ATT_STEP * (i + 1), nb - 1), 0)
    kv = lambda f: (lambda i: (f(i)[0], 2))
    shapes = ((rows, cur), (ATT_BLOCK, prev), (rows, cur), (ATT_BLOCK, nxt))
    tab = [pl.BlockSpec((r, LANES), f) for r, f in shapes]
    z = [pl.BlockSpec((rows, D_MODEL), cur)] + [pl.BlockSpec((r, 2 * KV_DIM), kv(f)) for r, f in shapes[1:]]
    return z, tab


def _att_load(zq_ref, kp_ref, kc_ref, kn_ref, cq_ref, sq_ref, cp_ref, sp_ref, cc_ref, sc_ref, cn_ref, sn_ref):
    q = (_rope(zq_ref[...], cq_ref[...], sq_ref[...]) * (HEAD_DIM ** -0.5))
    ks, vs = [], []
    for ref, c_ref, s_ref in ((kp_ref, cp_ref, sp_ref), (kc_ref, cc_ref, sc_ref), (kn_ref, cn_ref, sn_ref)):
        kvb = ref[...]
        ks.append(_rope(kvb[:, :KV_DIM], c_ref[...], s_ref[...]))
        vs.append(kvb[:, KV_DIM:])
    return q, jnp.concatenate(ks, axis=0).astype(BF16), jnp.concatenate(vs, axis=0).astype(BF16)


def _att_rows(sub):
    return (slice(ATT_BLOCK * sub, ATT_BLOCK * (sub + 1)), slice(ATT_BLOCK * sub, ATT_BLOCK * (sub + 3)))


def _att_fwd(z, sink, cos, sin, *, comm=None, name):
    s = z.shape[0]
    nb = s // ATT_BLOCK
    steps = nb // ATT_STEP

    def body(zq_ref, kp_ref, kc_ref, kn_ref, cq_ref, cp_ref, cc_ref, cn_ref, sq_ref, sp_ref, sc_ref, sn_ref, sink_ref,
             o_ref):
        i = pl.program_id(0)
        q, k, v = _att_load(zq_ref, kp_ref, kc_ref, kn_ref, cq_ref, sq_ref, cp_ref, sp_ref, cc_ref, sc_ref, cn_ref, sn_ref)
        for sub in range(ATT_STEP):
            qrows, krows = _att_rows(sub)
            valid = _att_mask(ATT_STEP * i + sub, nb)
            acc = [None] * (N_Q_HEADS // 2)
            for h in range(N_KV_HEADS):
                lanes = slice(LANES * (h // 2), LANES * (h // 2 + 1))
                qs = _stack_q(q[qrows], h).astype(BF16)
                prob, _ = _att_probs(qs, k[krows, lanes], _sink_rows(sink_ref, h), valid)
                oh = jnp.dot(prob.astype(BF16), v[krows, lanes], preferred_element_type=F32)
                acc = _unstack_q(oh, h, acc)
            o_ref[qrows, :] = jnp.concatenate(acc, axis=1).astype(BF16)

    zspecs, tab = _att_specs(nb)
    return _call(
        body, name=name, grid=(steps,),
        out_shape=(jax.ShapeDtypeStruct((s, D_MODEL), BF16),),
        in_specs=zspecs + tab + tab + [pl.BlockSpec(memory_space=pltpu.SMEM)],
        out_specs=(pl.BlockSpec((ATT_STEP * ATT_BLOCK, D_MODEL), lambda i: (i, 0)),),
        args=(z, z, z, z, cos, cos, cos, cos, sin, sin, sin, sin, sink), sem=("parallel",), comm=comm,
        edge=lambda: (pl.program_id(0) == 0, pl.program_id(0) == steps - 1))


def _att_bwd(z, do, sink, cos, sin, *, comm=None, name):
    s = z.shape[0]
    nb = s // ATT_BLOCK
    steps = nb // ATT_STEP

    def body(zq_ref, kp_ref, kc_ref, kn_ref, cq_ref, cp_ref, cc_ref, cn_ref, sq_ref, sp_ref, sc_ref, sn_ref, sink_ref,
             do_ref, dq_ref, part_ref, dsink_ref):
        i = pl.program_id(0)
        q, k, v = _att_load(zq_ref, kp_ref, kc_ref, kn_ref, cq_ref, sq_ref, cp_ref, sp_ref, cc_ref, sc_ref, cn_ref, sn_ref)
        dsink = None
        nt = (((1,), (1,)), ((), ()))
        for sub in range(ATT_STEP):
            qrows, krows = _att_rows(sub)
            valid = _att_mask_t(ATT_STEP * i + sub, nb)
            dout = do_ref[qrows, :].astype(F32)
            dq_acc = [None] * (N_Q_HEADS // 2)
            dk_acc = [None] * 2
            dv_acc = [None] * 2
            rows = []
            for h in range(N_KV_HEADS):
                grp = h // 2
                lanes = slice(LANES * grp, LANES * (grp + 1))
                kh, vh = k[krows, lanes], v[krows, lanes]
                qs = _stack_q(q[qrows], h).astype(BF16)
                dos = _stack_q(dout, h).astype(BF16)
                sink = _sink_lanes(sink_ref, h)
                sc = jnp.where(valid, lax.dot_general(kh, qs, nt, preferred_element_type=F32), NEG)
                m = jnp.maximum(jnp.max(sc, axis=0, keepdims=True), sink)
                p = jnp.exp(sc - m)
                es = jnp.exp(sink - m)
                inv = 1.0 / (jnp.sum(p, axis=0, keepdims=True) + es)
                prob = p * inv
                dprob = lax.dot_general(vh, dos, nt, preferred_element_type=F32)
                delta = jnp.sum(prob * dprob, axis=0, keepdims=True)
                dsc = (prob * (dprob - delta)).astype(BF16)
                dsk = -(es * inv) * delta
                for gq in range(GROUP):
                    tot = jnp.sum(dsk[:, ATT_BLOCK * gq:ATT_BLOCK * (gq + 1)], axis=1, keepdims=True)
                    rows.append(jnp.broadcast_to(tot, (1, LANES)))
                dqs = lax.dot_general(dsc, kh, (((0,), (0,)), ((), ())), preferred_element_type=F32)
                dq_acc = _unstack_q(dqs, h, dq_acc)
                dkh = jnp.dot(dsc, qs, preferred_element_type=F32)
                dvh = jnp.dot(prob.astype(BF16), dos, preferred_element_type=F32)
                dk_acc[grp] = dkh if dk_acc[grp] is None else dk_acc[grp] + dkh
                dv_acc[grp] = dvh if dv_acc[grp] is None else dv_acc[grp] + dvh
            dq = jnp.concatenate(dq_acc, axis=1) * (HEAD_DIM ** -0.5)
            dq_ref[qrows, :] = _unrope(dq, cq_ref[qrows, :], sq_ref[qrows, :]).astype(BF16)
            part = jnp.concatenate(dk_acc + dv_acc, axis=1)
            for wdw in range(3):
                part_ref[sub, wdw] = part[ATT_BLOCK * wdw:ATT_BLOCK * (wdw + 1), :]
            mine = jnp.concatenate(rows, axis=0)
            dsink = mine if dsink is None else dsink + mine
        _accumulate(dsink_ref, dsink, i == 0)

    zspecs, tab = _att_specs(nb)
    group = pl.BlockSpec((ATT_STEP * ATT_BLOCK, D_MODEL), lambda i: (i, 0))
    return _call(
        body, name=name, grid=(steps,),
        out_shape=(jax.ShapeDtypeStruct((s, D_MODEL), BF16), jax.ShapeDtypeStruct((nb, 3, ATT_BLOCK, 2 * KV_DIM), F32),
                   jax.ShapeDtypeStruct((N_Q_HEADS, LANES), F32)),
        in_specs=zspecs + tab + tab + [pl.BlockSpec(memory_space=pltpu.SMEM), group],
        out_specs=(group, pl.BlockSpec((ATT_STEP, 3, ATT_BLOCK, 2 * KV_DIM), lambda i: (i, 0, 0, 0)),
                   pl.BlockSpec((N_Q_HEADS, LANES), lambda i: (0, 0))),
        args=(z, z, z, z, cos, cos, cos, cos, sin, sin, sin, sin, sink, do), sem=("arbitrary",), comm=comm,
        edge=lambda: (pl.program_id(0) == 0, pl.program_id(0) == steps - 1))


def _att_bwd_kv(part, cos, sin, *, name):
    nb = part.shape[0]

    def body(pn_ref, pc_ref, pp_ref, c_ref, s_ref, o_ref):
        j = pl.program_id(0)
        tot = pc_ref[...]
        tot = tot + jnp.where(j < nb - 1, pn_ref[...], 0.0)
        tot = tot + jnp.where(j > 0, pp_ref[...], 0.0)
        dk = _unrope(tot[:, :KV_DIM], c_ref[...], s_ref[...])
        o_ref[...] = jnp.concatenate([dk, tot[:, KV_DIM:]], axis=1).astype(BF16)

    blk = (None, None, ATT_BLOCK, 2 * KV_DIM)
    return pl.pallas_call(
        body, name=name, grid=(nb,),
        out_shape=jax.ShapeDtypeStruct((nb * ATT_BLOCK, 2 * KV_DIM), BF16),
        in_specs=[pl.BlockSpec(blk, lambda j: (jnp.minimum(j + 1, nb - 1), 0, 0, 0)),
                  pl.BlockSpec(blk, lambda j: (j, 1, 0, 0)),
                  pl.BlockSpec(blk, lambda j: (jnp.maximum(j - 1, 0), 2, 0, 0)),
                  pl.BlockSpec((ATT_BLOCK, LANES), lambda j: (j, 0)), pl.BlockSpec((ATT_BLOCK, LANES), lambda j: (j, 0))],
        out_specs=pl.BlockSpec((ATT_BLOCK, 2 * KV_DIM), lambda j: (j, 0)),
        compiler_params=_params(("parallel",)),
    )(part, part, part, cos, sin)


def _bdot(a, b, dims):
    return lax.dot_general(a.astype(BF16), b.astype(BF16), (dims, ((), ())), preferred_element_type=F32)


@jax.custom_vjp
def _dot_nn(a, b):
    return _bdot(a, b, ((1,), (0,)))


@jax.custom_vjp
def _dot_nt(a, b):
    return _bdot(a, b, ((1,), (1,)))


@jax.custom_vjp
def _dot_tn(a, b):
    return _bdot(a, b, ((0,), (0,)))


_dot_nn.defvjp(lambda a, b: (_dot_nn(a, b), (a, b)), lambda r, d: (_dot_nt(d, r[1]), _dot_tn(r[0], d)))
_dot_nt.defvjp(lambda a, b: (_dot_nt(a, b), (a, b)), lambda r, d: (_dot_nn(d, r[1]), _dot_tn(d, r[0])))
_dot_tn.defvjp(lambda a, b: (_dot_tn(a, b), (a, b)), lambda r, d: (_dot_nt(r[1], d), _dot_nn(r[0], d)))


def _running_sum(v, up):
    n = v.shape[0]
    rows = lax.broadcasted_iota(jnp.int32, v.shape, 0)
    sh = 1
    while sh < n:
        if up:
            v = v + jnp.where(rows < n - sh, pltpu.roll(v, n - sh, 0), 0.0)
        else:
            v = v + jnp.where(rows >= sh, pltpu.roll(v, sh, 0), 0.0)
        sh *= 2
    return v


@jax.custom_vjp
def _sum_down(v):
    return _running_sum(v, False)


@jax.custom_vjp
def _sum_up(v):
    return _running_sum(v, True)


_sum_down.defvjp(lambda v: (_running_sum(v, False), None), lambda _, d: (_sum_up(d),))
_sum_up.defvjp(lambda v: (_running_sum(v, True), None), lambda _, d: (_sum_down(d),))

N_SUB = HG_CHUNK // HG_SUB


def _fold_blocks(v):
    out = v[:HG_CHUNK]
    for i in range(1, N_SUB):
        out = out + v[HG_CHUNK * i:HG_CHUNK * (i + 1)]
    return out


@jax.custom_vjp
def _fold(v):
    return _fold_blocks(v)


_fold.defvjp(lambda v: (_fold_blocks(v), None), lambda _, d: (jnp.concatenate([d] * N_SUB, axis=0),))


def _hg_consts(rev):
    c, sub = HG_CHUNK, HG_SUB
    rowpos = lax.broadcasted_iota(jnp.int32, (c, HG_DIM), 0)
    rr = lax.broadcasted_iota(jnp.int32, (N_SUB * c, c), 0)
    key = lax.broadcasted_iota(jnp.int32, (N_SUB * c, c), 1)
    blk, qry = rr // c, rr % c
    if rev:
        rowpos, qry, key = c - 1 - rowpos, c - 1 - qry, c - 1 - key
    keep = (key // sub == blk) & (key <= qry)
    return keep, rowpos


def _pick(b, rowpos, t):
    return jnp.sum(jnp.where(rowpos == t, b, 0.0), axis=0, keepdims=True)


def _hg_local(zq, zf, zv, lbv, consts, dots):
    dot_nn, dot_nt, dot_tn, cum, fold = dots
    keep, rowpos = consts
    sig = _sigmoid(zf)
    f = lbv + (1.0 - lbv) * sig
    g = jnp.log(f)
    k = (1.0 - lbv) * (1.0 - sig)
    q = zq * _sigmoid(zq)
    b = cum(g)
    ends = [_pick(b, rowpos, (j + 1) * HG_SUB - 1) for j in range(N_SUB)]
    b_last = ends[-1]
    b_end = b_last
    for j in range(N_SUB - 1):
        b_end = jnp.where(rowpos // HG_SUB == j, ends[j], b_end)
    kc = k * jnp.exp(b_end - b)
    qbs = [q * jnp.exp(jnp.where(rowpos >= j * HG_SUB, b - ends[j], 0.0)) for j in range(N_SUB)]
    scores = fold(jnp.where(keep, dot_nt(jnp.concatenate(qbs, axis=0), kc), 0.0))
    return dot_nn(scores, zv), q * jnp.exp(b), k * jnp.exp(b_last - b), jnp.exp(b_last)


def _hg_chunk(zq, zf, zv, lbv, st, consts, dots):
    intra, qs, kd, dec = _hg_local(zq, zf, zv, lbv, consts, dots)
    return intra + dots[1](qs, st), dec * st + dots[2](zv, kd)


def _hg_dots(diff, rev):
    if diff:
        return _dot_nn, _dot_nt, _dot_tn, (_sum_up if rev else _sum_down), _fold
    return (lambda a, b: _bdot(a, b, ((1,), (0,))), lambda a, b: _bdot(a, b, ((1,), (1,))),
            lambda a, b: _bdot(a, b, ((0,), (0,))), lambda v: _running_sum(v, rev), _fold_blocks)


def _hg_specs(ts, nch, trow):
    tile = pl.BlockSpec((ts, HG_DIM), lambda h, t: (trow(t), h))
    mats = pl.BlockSpec((None, nch, HG_DIM, HG_DIM), lambda h, t: (h, trow(t), 0, 0))
    vecs = pl.BlockSpec((None, nch, 1, HG_DIM), lambda h, t: (h, trow(t), 0, 0))
    return tile, mats, vecs


def _time_order(nch, rev):
    return range(nch - 1, -1, -1) if rev else range(nch)


def _chunk_rows(c):
    return pl.ds(c * HG_CHUNK, HG_CHUNK)


def _hg_edge(nt):
    h, t = pl.program_id(0), pl.program_id(1)
    return (h == 0) & (t == 0), (h == HG_HEADS - 1) & (t == nt - 1)


def _hg_fwd(z, lb, *, rev, ts, comm=None, name):
    s = z.shape[0]
    nt = s // ts
    nch = ts // HG_CHUNK
    fcol = HG_HEADS * (2 if rev else 1)

    def body(zq_ref, zf_ref, zv_ref, lb_ref, o_ref, st_ref, qs_ref, dec_ref, state_ref):
        @pl.when(pl.program_id(1) == 0)
        def _():
            state_ref[...] = jnp.zeros_like(state_ref)

        consts = _hg_consts(rev)
        dots = _hg_dots(False, rev)
        lbv = lb_ref[...]
        local = {}
        for c in range(nch):
            rows = _chunk_rows(c)
            zv = zv_ref[rows, :]
            intra, qs, kd, dec = _hg_local(zq_ref[rows, :], zf_ref[rows, :], zv, lbv, consts, dots)
            qs = qs.astype(BF16)
            qs_ref[rows, :] = qs
            dec_ref[c] = dec
            local[c] = (intra, qs, dec, dots[2](zv, kd))
        st = state_ref[...]
        for c in _time_order(nch, rev):
            intra, qs, dec, upd = local[c]
            st_ref[c] = st.astype(BF16)
            o_ref[_chunk_rows(c), :] = intra + _bdot(qs, st, ((1,), (1,)))
            st = dec * st + upd
        state_ref[...] = st

    trow = (lambda t: nt - 1 - t) if rev else (lambda t: t)
    col = lambda off: pl.BlockSpec((ts, HG_DIM), lambda h, t: (trow(t), off + h))
    tile, mats, vecs = _hg_specs(ts, nch, trow)
    nchunks = s // HG_CHUNK
    return _call(
        body, name=name, grid=(HG_HEADS, nt),
        out_shape=(jax.ShapeDtypeStruct((s, D_MODEL), F32),
                   jax.ShapeDtypeStruct((HG_HEADS, nchunks, HG_DIM, HG_DIM), BF16),
                   jax.ShapeDtypeStruct((s, D_MODEL), BF16),
                   jax.ShapeDtypeStruct((HG_HEADS, nchunks, 1, HG_DIM), F32)),
        in_specs=[col(0), col(fcol), col(3 * HG_HEADS), pl.BlockSpec((None, 1, HG_DIM), lambda h, t: (h, 0, 0))],
        out_specs=(tile, mats, tile, vecs), args=(z, z, z, lb),
        scratch_shapes=[pltpu.VMEM((HG_DIM, HG_DIM), F32)], sem=("parallel", "arbitrary"), comm=comm,
        edge=lambda: _hg_edge(nt))


def _hg_bwd(z, lb, states, qs, dec, dout, addq, addv, *, rev, ts, comm=None, name):
    s = z.shape[0]
    nt = s // ts
    nch = ts // HG_CHUNK
    fcol = HG_HEADS * (2 if rev else 1)
    has_add = addq is not None

    def body(*refs):
        zq_ref, zf_ref, zv_ref, lb_ref, st_ref, qs_ref, dec_ref, do_ref = refs[:8]
        aq_ref, av_ref = (refs[8], refs[9]) if has_add else (None, None)
        dq_ref, df_ref, dv_ref, dlb_ref, grad_ref = refs[-5:]

        @pl.when(pl.program_id(1) == 0)
        def _():
            grad_ref[...] = jnp.zeros_like(grad_ref)

        consts = _hg_consts(rev)
        dots = _hg_dots(True, rev)
        lbv = lb_ref[...]
        prods = {c: _bdot(do_ref[_chunk_rows(c), :], qs_ref[_chunk_rows(c), :], ((0,), (0,))) for c in range(nch)}
        gleave = {}
        gr = grad_ref[...]
        for c in reversed(_time_order(nch, rev)):
            gleave[c] = gr
            gr = dec_ref[c] * gr + prods[c]
        grad_ref[...] = gr
        dlb_blk = jnp.zeros((1, HG_DIM), F32)
        for c in range(nch):
            rows = _chunk_rows(c)
            fn = lambda a, b2, c2, d2, e2: _hg_chunk(a, b2, c2, d2, e2, consts, dots)
            _, pull = jax.vjp(fn, zq_ref[rows, :], zf_ref[rows, :], zv_ref[rows, :], lbv, st_ref[c].astype(F32))
            dq, df, dv, dlb, _ = pull((do_ref[rows, :], gleave[c]))
            if has_add:
                dq = dq + aq_ref[rows, :]
                dv = dv + av_ref[rows, :]
            dq_ref[rows, :] = dq.astype(dq_ref.dtype)
            df_ref[rows, :] = df.astype(BF16)
            dv_ref[rows, :] = dv.astype(dv_ref.dtype)
            dlb_blk = dlb_blk + dlb

        @pl.when(pl.program_id(1) == 0)
        def _():
            dlb_ref[...] = dlb_blk

        @pl.when(pl.program_id(1) > 0)
        def _():
            dlb_ref[...] += dlb_blk

    trow = (lambda t: t) if rev else (lambda t: nt - 1 - t)
    col = lambda off: pl.BlockSpec((ts, HG_DIM), lambda h, t: (trow(t), off + h))
    tile, mats, vecs = _hg_specs(ts, nch, trow)
    in_specs = [col(0), col(fcol), col(3 * HG_HEADS), pl.BlockSpec((None, 1, HG_DIM), lambda h, t: (h, 0, 0)),
                mats, tile, vecs, tile]
    args = [z, z, z, lb, states, qs, dec, dout]
    if has_add:
        in_specs += [tile, tile]
        args += [addq, addv]
    act = lambda dt: jax.ShapeDtypeStruct((s, D_MODEL), dt)
    sums = BF16 if has_add else F32
    return _call(
        body, name=name, grid=(HG_HEADS, nt),
        out_shape=(act(sums), act(BF16), act(sums), jax.ShapeDtypeStruct((HG_HEADS, 1, HG_DIM), F32)),
        in_specs=in_specs,
        out_specs=(tile, tile, tile, pl.BlockSpec((None, 1, HG_DIM), lambda h, t: (h, 0, 0))), args=tuple(args),
        scratch_shapes=[pltpu.VMEM((HG_DIM, HG_DIM), F32)], sem=("parallel", "arbitrary"), comm=comm,
        edge=lambda: _hg_edge(nt))


def _hg_post(of, ob, z, norm_g, *, tm, name):
    s = of.shape[0]

    def body(of_ref, ob_ref, gate_ref, ng_ref, y_ref):
        gn = ng_ref[...]
        for h in range(HG_HEADS):
            ln = slice(HG_DIM * h, HG_DIM * (h + 1))
            o = of_ref[:, ln] + ob_ref[:, ln]
            r = lax.rsqrt(jnp.mean(o * o, axis=-1, keepdims=True) + LN_EPS)
            gt = gate_ref[:, ln]
            y_ref[:, ln] = (o * r * gn * gt * _sigmoid(gt)).astype(BF16)

    row = lambda i: (i, 0)
    return pl.pallas_call(
        body, name=name, grid=(s // tm,),
        out_shape=jax.ShapeDtypeStruct((s, D_MODEL), BF16),
        in_specs=[pl.BlockSpec((tm, D_MODEL), row), pl.BlockSpec((tm, D_MODEL), row),
                  pl.BlockSpec((tm, D_MODEL), lambda i: (i, 4)), pl.BlockSpec((1, HG_DIM), lambda i: (0, 0))],
        out_specs=pl.BlockSpec((tm, D_MODEL), row),
        compiler_params=_params(("parallel",)),
    )(of, ob, z, norm_g)


def _hg_post_bwd(dy, of, ob, z, norm_g, *, tm, name):
    s = of.shape[0]

    def body(dy_ref, of_ref, ob_ref, gate_ref, ng_ref, do_ref, dgate_ref, dng_ref):
        gn = ng_ref[...]
        tot = jnp.zeros((1, HG_DIM), F32)
        for h in range(HG_HEADS):
            ln = slice(HG_DIM * h, HG_DIM * (h + 1))
            d = dy_ref[:, ln].astype(F32)
            o = of_ref[:, ln] + ob_ref[:, ln]
            r = lax.rsqrt(jnp.mean(o * o, axis=-1, keepdims=True) + LN_EPS)
            ohat = o * r
            gt = gate_ref[:, ln]
            sg = _sigmoid(gt)
            don = d * gt * sg
            dgate_ref[:, ln] = (d * ohat * gn * sg * (1.0 + gt * (1.0 - sg))).astype(BF16)
            tot = tot + jnp.sum(don * ohat, axis=0, keepdims=True)
            dohat = don * gn
            do_ref[:, ln] = r * (dohat - ohat * jnp.mean(dohat * ohat, axis=-1, keepdims=True))

        @pl.when(pl.program_id(0) == 0)
        def _():
            dng_ref[...] = tot

        @pl.when(pl.program_id(0) > 0)
        def _():
            dng_ref[...] += tot

    row = lambda i: (i, 0)
    return pl.pallas_call(
        body, name=name, grid=(s // tm,),
        out_shape=(jax.ShapeDtypeStruct((s, D_MODEL), F32), jax.ShapeDtypeStruct((s, D_MODEL), BF16),
                   jax.ShapeDtypeStruct((1, HG_DIM), F32)),
        in_specs=[pl.BlockSpec((tm, D_MODEL), row), pl.BlockSpec((tm, D_MODEL), row), pl.BlockSpec((tm, D_MODEL), row),
                  pl.BlockSpec((tm, D_MODEL), lambda i: (i, 4)), pl.BlockSpec((1, HG_DIM), lambda i: (0, 0))],
        out_specs=(pl.BlockSpec((tm, D_MODEL), row), pl.BlockSpec((tm, D_MODEL), row),
                   pl.BlockSpec((1, HG_DIM), lambda i: (0, 0))),
        compiler_params=_params(("arbitrary",)),
    )(dy, of, ob, z, norm_g)


def _lb_fwd(logits, *, name):
    w = logits.shape[1]

    def body(l_ref, o_ref):
        lg = l_ref[...]
        e = jnp.exp(lg - jnp.max(lg, axis=0, keepdims=True))
        sm = e / jnp.sum(e, axis=0, keepdims=True)
        o_ref[0:1, :] = sm[1:2]
        o_ref[1:2, :] = sm[1:2] + sm[2:3] + sm[3:4]

    return pl.pallas_call(body, name=name, out_shape=jax.ShapeDtypeStruct((2, w), F32))(logits)


def _lb_bwd(logits, dlb, *, name):
    w = logits.shape[1]

    def body(l_ref, d_ref, o_ref):
        lg = l_ref[...]
        e = jnp.exp(lg - jnp.max(lg, axis=0, keepdims=True))
        sm = e / jnp.sum(e, axis=0, keepdims=True)
        d1, d3 = d_ref[0:1, :], d_ref[1:2, :]
        dot = sm[1:2] * (d1 + d3) + (sm[2:3] + sm[3:4]) * d3
        o_ref[0:1, :] = -sm[0:1] * dot
        o_ref[1:2, :] = sm[1:2] * (d1 + d3 - dot)
        o_ref[2:3, :] = sm[2:3] * (d3 - dot)
        o_ref[3:4, :] = sm[3:4] * (d3 - dot)

    return pl.pallas_call(body, name=name, out_shape=jax.ShapeDtypeStruct((4, w), F32))(logits, dlb)


def _adamw(w, g, m, v, *, tr, g_off=0, name):
    rows = w.shape[0]
    parts = g.ndim == 3
    c1 = 1.0 / (1.0 - ADAM_B1 ** ADAM_STEP)
    c2 = 1.0 / (1.0 - ADAM_B2 ** ADAM_STEP)

    def body(w_ref, g_ref, m_ref, v_ref, go_ref, d_ref, mo_ref, vo_ref):
        if parts:
            gg = g_ref[0].astype(F32)
            for i in range(1, N_DEV):
                gg = gg + g_ref[i].astype(F32)
        else:
            gg = g_ref[...]
        mm = ADAM_B1 * m_ref[...] + (1.0 - ADAM_B1) * gg
        vv = ADAM_B2 * v_ref[...] + (1.0 - ADAM_B2) * (gg * gg)
        go_ref[...] = gg
        mo_ref[...] = mm
        vo_ref[...] = vv
        d_ref[...] = -ADAM_LR * ((mm * c1) / (jnp.sqrt(vv * c2) + ADAM_EPS) + ADAM_WD * w_ref[...])

    tile = pl.BlockSpec((tr, D_MODEL), lambda i: (i, 0))
    gspec = pl.BlockSpec((N_DEV, tr, D_MODEL), lambda i: (0, i + g_off // tr, 0)) if parts else tile
    out = jax.ShapeDtypeStruct((rows, D_MODEL), F32)
    return pl.pallas_call(
        body, name=name, grid=(rows // tr,),
        out_shape=(out, out, out, out),
        in_specs=[tile, gspec, tile, tile], out_specs=(tile, tile, tile, tile),
        compiler_params=_params(("parallel",)),
    )(w, g, m, v)


def _sum8(parts, *, name):
    def body(p_ref, o_ref):
        tot = p_ref[0]
        for i in range(1, N_DEV):
            tot = tot + p_ref[i]
        o_ref[...] = tot

    return pl.pallas_call(body, name=name, out_shape=jax.ShapeDtypeStruct(parts.shape[1:], parts.dtype))(parts)


def _layer_params(i):
    j = i // 2
    mix = [("att_w_qkv", j, 1), ("att_w_o", j, 0)] if i % 2 == 0 else [("hgrn_w_in", j, 1), ("hgrn_w_o", j, 0)]
    return mix + [("ffn_w_in", i, 1), ("ffn_w_out", i, 0), ("ple_w_gate", i, 0), ("ple_w_proj", i, 1)]


def _pack_local(tree, params):
    return jnp.concatenate([tree[n][j].reshape(-1, D_MODEL) for n, j, _ in params], axis=0)


def _unpack_local(packed, params, like):
    out, r = {}, 0
    for n, _, _ in params:
        shp = like[n].shape[1:]
        k = shp[0] * shp[1] // D_MODEL
        out[n] = packed[r:r + k].reshape(shp)
        r += k
    return out


def _unpack_gathered(gathered, params, like):
    out, r = {}, 0
    for n, _, ax in params:
        shp = like[n].shape[1:]
        k = shp[0] * shp[1] // D_MODEL
        t = gathered[:, r:r + k].reshape((N_DEV,) + shp)
        out[n] = (jnp.moveaxis(t, 0, 1).reshape(shp[0], N_DEV * shp[1]) if ax == 1
                  else t.reshape(N_DEV * shp[0], shp[1]))
        r += k
    return out


def _pack_full(grads, params, like):
    cols = []
    for n, _, ax in params:
        shp = like[n].shape[1:]
        t = (jnp.moveaxis(grads[n].reshape(shp[0], N_DEV, shp[1]), 1, 0) if ax == 1
             else grads[n].reshape(N_DEV, shp[0], shp[1]))
        cols.append(t.reshape(N_DEV, -1, D_MODEL).astype(BF16))
    return jnp.concatenate(cols, axis=1)


def _row_tile(rows):
    return max(t for t in range(16, 257, 16) if rows % t == 0)


SMALL_ROWS = 24


def _pad_row(a):
    flat = a.reshape(1, -1)
    return jnp.pad(flat, ((0, 0), (0, D_MODEL - flat.shape[1])))


def _tile(n, pref):
    return min(n, pref)


def kernel(x, p, att_w_qkv, att_sink, att_w_o, hgrn_w_in, hgrn_lb_logits, hgrn_norm_g, hgrn_w_o, ln_mix_g, ln_mix_b, ffn_w_in, ffn_w_out, ln_ffn_g, ln_ffn_b, ple_w_gate, ple_w_proj, loss_target, m_att_w_qkv, m_att_sink, m_att_w_o, m_hgrn_w_in, m_hgrn_lb_logits, m_hgrn_norm_g, m_hgrn_w_o, m_ln_mix_g, m_ln_mix_b, m_ffn_w_in, m_ffn_w_out, m_ln_ffn_g, m_ln_ffn_b, m_ple_w_gate, m_ple_w_proj, v_att_w_qkv, v_att_sink, v_att_w_o, v_hgrn_w_in, v_hgrn_lb_logits, v_hgrn_norm_g, v_hgrn_w_o, v_ln_mix_g, v_ln_mix_b, v_ffn_w_in, v_ffn_w_out, v_ln_ffn_g, v_ln_ffn_b, v_ple_w_gate, v_ple_w_proj):
    names = ["att_w_qkv", "att_sink", "att_w_o", "hgrn_w_in", "hgrn_lb_logits", "hgrn_norm_g", "hgrn_w_o", "ln_mix_g",
             "ln_mix_b", "ffn_w_in", "ffn_w_out", "ln_ffn_g", "ln_ffn_b", "ple_w_gate", "ple_w_proj"]
    w = dict(zip(names, (att_w_qkv, att_sink, att_w_o, hgrn_w_in, hgrn_lb_logits, hgrn_norm_g, hgrn_w_o, ln_mix_g,
                         ln_mix_b, ffn_w_in, ffn_w_out, ln_ffn_g, ln_ffn_b, ple_w_gate, ple_w_proj)))
    mom = dict(zip(names, (m_att_w_qkv, m_att_sink, m_att_w_o, m_hgrn_w_in, m_hgrn_lb_logits, m_hgrn_norm_g, m_hgrn_w_o,
                           m_ln_mix_g, m_ln_mix_b, m_ffn_w_in, m_ffn_w_out, m_ln_ffn_g, m_ln_ffn_b, m_ple_w_gate,
                           m_ple_w_proj)))
    var = dict(zip(names, (v_att_w_qkv, v_att_sink, v_att_w_o, v_hgrn_w_in, v_hgrn_lb_logits, v_hgrn_norm_g, v_hgrn_w_o,
                           v_ln_mix_g, v_ln_mix_b, v_ffn_w_in, v_ffn_w_out, v_ln_ffn_g, v_ln_ffn_b, v_ple_w_gate,
                           v_ple_w_proj)))
    s = x.shape[1]
    me = 4 * lax.axis_index("x") + 2 * lax.axis_index("y") + lax.axis_index("c")
    tm = _tile(s, 512)
    tbig = _tile(s, 1024)
    ts = _tile(s // 2, 2048)
    x0 = x.reshape(s, D_MODEL)
    target = loss_target.reshape(s, D_MODEL)
    pl_in = p.reshape(DEPTH, s, PLE_DIM)

    n_first = 3
    full = _unpack_gathered(_gather(_pack_local(w, _layer_params(0)).astype(BF16), name="gather_weights"),
                            _layer_params(0), w)
    lb_rows = jnp.pad(hgrn_lb_logits.reshape(8, HG_DIM), ((0, 0), (0, D_MODEL - HG_DIM)))
    lb_all = _gather(lb_rows, name="gather_lb")[:, :, :HG_DIM]
    logits_full = jnp.moveaxis(lb_all, 0, 1).reshape(DEPTH, 2 * D_MODEL)
    lb = _lb_fwd(logits_full, name="lb_fwd")
    cos, sin = _rope_tables(s)

    saved = []
    xf, xb = x0, x0
    for i in range(DEPTH):
        j = i // 2
        sv = {"xb": xb, "w": full}
        nxt = nxt2 = None
        if i + 1 < DEPTH:
            ahead = _layer_params(i + 1)
            nxt = (_pack_local(w, ahead[:n_first]).astype(BF16), True)
            nxt2 = (_pack_local(w, ahead[n_first:]).astype(BF16), True)
        if i % 2 == 0:
            z = _mm(xb, full["att_w_qkv"], tm=tbig, tn=512, tk=D_MODEL, name="att_in")
            o, *more = _att_fwd(z, w["att_sink"][j], cos, sin, comm=nxt, name="att_fwd")
            w_o = full["att_w_o"]
        else:
            z = _mm(xb, full["hgrn_w_in"], tm=tbig, tn=1024, tk=D_MODEL, name="hgrn_in")
            lbl = lb[j].reshape(2, HG_HEADS, 1, HG_DIM)
            of, st_f, qs_f, dec_f, *more = _hg_fwd(z, lbl[0], rev=False, ts=ts, comm=nxt, name="hgrn_fwd")
            ob, st_b, qs_b, dec_b = _hg_fwd(z, lbl[1], rev=True, ts=ts, name="hgrn_fwd_rev")
            o = _hg_post(of, ob, z, w["hgrn_norm_g"][j].reshape(1, HG_DIM), tm=tm, name="hgrn_post")
            w_o = full["hgrn_w_o"]
            sv.update(of=of, ob=ob, st_f=st_f, st_b=st_b, lbl=lbl, qs_f=qs_f, qs_b=qs_b, dec_f=dec_f, dec_b=dec_b)
        sv.update(z=z, o=o)
        g1, b1 = w["ln_mix_g"][i:i + 1], w["ln_mix_b"][i:i + 1]
        pre1, x1b = _proj_ln(o, w_o, xf, g1, b1, tm=tm, name="mix_out_ln")
        gg, uu, act, *more2 = _ffn_in(x1b, full["ffn_w_in"], tm=tm, tn=FF_TILE, comm=nxt2, name="ffn_in")
        pre2, x2b, xf, xb = _ffn_out_ple(act, full["ffn_w_out"], pre1, g1, b1, w["ln_ffn_g"][i:i + 1],
                                         w["ln_ffn_b"][i:i + 1], pl_in[i], full["ple_w_gate"], full["ple_w_proj"], tm=tm,
                                         name="ffn_out_ple")
        sv.update(pre1=pre1, x1b=x1b, g=gg, u=uu, act=act, pre2=pre2, x2b=x2b)
        saved.append(sv)
        if nxt is not None:
            full = _unpack_gathered(more[0], ahead[:n_first], w)
            full.update(_unpack_gathered(more2[0], ahead[n_first:], w))

    dx, loss_blk = _loss_head(xf, target, tm=tm, name="loss_head")
    loss = lax.psum(loss_blk[0, 0], AXES)

    small = {n: [None] * DEPTH for n in ("ln_mix_g", "ln_mix_b", "ln_ffn_g", "ln_ffn_b")}
    dlb_rows = [None] * 4
    dnorm, dsink = [None] * 2, [None] * 2
    recv_late, recv_early = [None] * DEPTH, [None] * DEPTH
    above = None
    mmw = functools.partial(_mm, ta=True, tk=_tile(s, 2048), out_dtype=BF16)
    for i in reversed(range(DEPTH)):
        j = i // 2
        sv = saved[i]
        full, gl = sv["w"], {}
        da, dpp, dy2, dy2b, small["ln_ffn_g"][i], small["ln_ffn_b"][i] = _ple_ln_bwd(
            dx, sv["x2b"], pl_in[i], full["ple_w_gate"], full["ple_w_proj"], sv["pre2"], w["ln_ffn_g"][i:i + 1],
            tm=tm, name="ple_ln_bwd")
        gl["ple_w_gate"] = mmw(sv["x2b"], da, tm=D_MODEL, tn=D_MODEL, name="dw_ple_gate")
        gl["ple_w_proj"] = mmw(pl_in[i], dpp, tm=PLE_DIM, tn=D_MODEL, name="dw_ple_proj")
        dg, du = _ffn_bwd_act(dy2b, full["ffn_w_out"], sv["g"], sv["u"], tm=_tile(s, SUB_ROWS), tn=D_FF,
                              name="ffn_bwd_act")
        gl["ffn_w_out"] = mmw(sv["act"], dy2b, tm=FF_TILE, tn=D_MODEL, name="dw_ffn_out")
        dy1, dy1b, small["ln_mix_g"][i], small["ln_mix_b"][i] = _dx_from_pieces(
            [dg, du], full["ffn_w_in"], dy2, tm=_tile(s, SUB_ROWS), ln=(sv["pre1"], w["ln_mix_g"][i:i + 1]),
            name="ffn_bwd_x_ln")
        gl["ffn_w_in"] = mmw(sv["x1b"], [dg, du], tm=D_MODEL, tn=FF_TILE, name="dw_ffn_in")
        n_out, n_inw = ("att_w_o", "att_w_qkv") if i % 2 == 0 else ("hgrn_w_o", "hgrn_w_in")
        do = _mm(dy1b, full[n_out], tm=tbig, tn=D_MODEL, tk=D_MODEL, tb=True, out_dtype=BF16, name="mix_out_bwd")
        gl[n_out] = mmw(sv["o"], dy1b, tm=D_MODEL, tn=D_MODEL, name="dw_mix_out")
        early = _pack_full(gl, _layer_params(i)[1:], w)
        comm = (early if above is None else jnp.concatenate([above, early], axis=1), False)
        if i % 2 == 0:
            dzq, part, dsk, *more = _att_bwd(sv["z"], do, w["att_sink"][j], cos, sin, comm=comm, name="att_bwd")
            dz = [dzq, _att_bwd_kv(part, cos, sin, name="att_bwd_kv")]
            dsink[j] = dsk[:, 0]
        else:
            dsum, dgate, dnorm[j] = _hg_post_bwd(do, sv["of"], sv["ob"], sv["z"], w["hgrn_norm_g"][j].reshape(1, HG_DIM),
                                                 tm=tm, name="hgrn_post_bwd")
            dq1, df1, dv1, dlb1, *more = _hg_bwd(sv["z"], sv["lbl"][0], sv["st_f"], sv["qs_f"], sv["dec_f"], dsum, None,
                                                 None, rev=False, ts=ts, comm=comm, name="hgrn_bwd")
            dq2, df2, dv2, dlb2 = _hg_bwd(sv["z"], sv["lbl"][1], sv["st_b"], sv["qs_b"], sv["dec_b"], dsum, dq1, dv1,
                                          rev=True, ts=ts, name="hgrn_bwd_rev")
            dz = [dq2, df1, df2, dv2, dgate]
            dlb_rows[2 * j] = dlb1.reshape(1, D_MODEL)
            dlb_rows[2 * j + 1] = dlb2.reshape(1, D_MODEL)
        if above is not None:
            recv_late[i + 1] = (more[0], 0)
        recv_early[i] = (more[0], 0 if above is None else above.shape[1])
        dx = _dx_from_pieces(dz, full[n_inw], dy1, tm=tm, name="mix_in_bwd")
        gl[n_inw] = mmw(sv["xb"], dz, tm=D_MODEL, tn=512, name="dw_mix_in")
        above = _pack_full(gl, _layer_params(i)[:1], w)
    grad_x = dx.reshape(x.shape)
    recv_late[0] = (_exchange(above, name="exchange_grads"), 0)

    big_out = [{n: [None] * w[n].shape[0] for n, _ in BIG} for _ in range(4)]
    for i in range(DEPTH):
        for params, (got, off) in ((_layer_params(i)[:1], recv_late[i]), (_layer_params(i)[1:], recv_early[i])):
            w_part = _pack_local(w, params)
            outs = _adamw(w_part, got, _pack_local(mom, params), _pack_local(var, params),
                          tr=_row_tile(math.gcd(w_part.shape[0], off)), g_off=off, name="adamw_big")
            for kind, packed in enumerate(outs):
                for (n, j, _), piece in zip(params, _unpack_local(packed, params, w).values()):
                    big_out[kind][n][j] = piece
    big_out = [{n: jnp.stack(v) for n, v in kind.items()} for kind in big_out]

    small_rows = jnp.concatenate(
        [jnp.concatenate(small[n], axis=0) for n in ("ln_mix_g", "ln_mix_b", "ln_ffn_g", "ln_ffn_b")] + dlb_rows
        + [_pad_row(jnp.stack(dnorm)), _pad_row(jnp.stack(dsink)), jnp.zeros((2, D_MODEL), F32)], axis=0)
    small_all = _gather(small_rows, name="gather_small")
    lbw, lbm, lbv = (t.reshape(4, 2 * HG_DIM) for t in (hgrn_lb_logits, mom["hgrn_lb_logits"], var["hgrn_lb_logits"]))
    summed = _sum8(small_all, name="sum_small")
    dlb_mine = lax.dynamic_slice_in_dim(summed[16:20].reshape(2, 2, HG_HEADS, HG_DIM), me, 1, axis=2)
    dlogits = _lb_bwd(lbw, dlb_mine.reshape(2, 2 * HG_DIM), name="lb_bwd")

    def small_pack(ln4, lbt, ng, sk):
        return jnp.concatenate([ln4[n] for n in ("ln_mix_g", "ln_mix_b", "ln_ffn_g", "ln_ffn_b")]
                               + [_pad_row(lbt), _pad_row(ng), _pad_row(sk), jnp.zeros((5, D_MODEL), F32)], axis=0)

    g_small = jnp.concatenate([summed[:16], _pad_row(dlogits), summed[20:22], jnp.zeros((5, D_MODEL), F32)], axis=0)
    souts = _adamw(small_pack(w, lbw, w["hgrn_norm_g"], w["att_sink"]), g_small,
                   small_pack(mom, lbm, mom["hgrn_norm_g"], mom["att_sink"]),
                   small_pack(var, lbv, var["hgrn_norm_g"], var["att_sink"]), tr=SMALL_ROWS, name="adamw_small")

    def small_unpack(t):
        out = {n: t[4 * k:4 * k + 4] for k, n in enumerate(("ln_mix_g", "ln_mix_b", "ln_ffn_g", "ln_ffn_b"))}
        out["hgrn_lb_logits"] = t[16].reshape(hgrn_lb_logits.shape)
        out["hgrn_norm_g"] = t[17, :2 * HG_DIM].reshape(hgrn_norm_g.shape)
        out["att_sink"] = t[18, :2 * N_Q_HEADS].reshape(att_sink.shape)
        return out

    result = [loss, grad_x]
    for big_t, small_t in zip(big_out, souts):
        merged = dict(big_t)
        merged.update(small_unpack(small_t))
        result += [merged[n] for n in names]
    return tuple(result)
```

```python
import functools
import math

import jax
import jax.numpy as jnp
from jax import lax
from jax.experimental import pallas as pl
from jax.experimental.pallas import tpu as pltpu

F32 = jnp.float32
BF16 = jnp.bfloat16

D_MODEL = 1024
DEPTH = 4
HEAD_DIM = 64
N_Q_HEADS = 16
N_KV_HEADS = 4
GROUP = 4
KV_DIM = 256
ATT_BLOCK = 128
ROPE_DIM = 16
ROPE_THETA = 500000.0
HG_HEADS = 8
HG_DIM = 128
HG_CHUNK = 64
HG_SUB = 16
D_FF = 2816
FF_TILE = 1408
SUB_ROWS = 256
PLE_DIM = 256
ALPHA = (2 * DEPTH) ** 0.25
LN_EPS = 1e-5
ADAM_LR, ADAM_B1, ADAM_B2, ADAM_EPS, ADAM_WD, ADAM_STEP = 0.001, 0.9, 0.999, 1e-08, 0.01, 10

N_DEV = 8
LANES = 128
VMEM_LIMIT = 52 * 1024 * 1024
NEG = -1e30
MESH = pl.DeviceIdType.MESH
AXES = ("x", "y", "c")

BIG = (("att_w_qkv", 2), ("att_w_o", 1), ("hgrn_w_in", 2), ("hgrn_w_o", 1), ("ffn_w_in", 2), ("ffn_w_out", 1),
       ("ple_w_gate", 1), ("ple_w_proj", 2))


def _params(sem=None, vmem=VMEM_LIMIT):
    return pltpu.CompilerParams(dimension_semantics=sem, vmem_limit_bytes=vmem)


def _sigmoid(x):
    return jax.nn.sigmoid(x)


def _direct_copies(src_ref, out_ref, send_sems, recv_sems, local_sem, gather, arrivals):
    x, y, c = lax.axis_index("x"), lax.axis_index("y"), lax.axis_index("c")
    me = 4 * x + 2 * y + c
    mine = (lambda j: src_ref) if gather else (lambda j: src_ref.at[j])
    pairs = []
    for k in range(1, N_DEV):
        px, py, pc = x ^ (k >> 2), y ^ ((k >> 1) & 1), c ^ (k & 1)
        peer = 4 * px + 2 * py + pc
        send = pltpu.make_async_remote_copy(
            src_ref=mine(peer), dst_ref=out_ref.at[me], send_sem=send_sems.at[k], recv_sem=recv_sems.at[k],
            device_id=(px, py, pc), device_id_type=MESH)
        arrival = pltpu.make_async_remote_copy(
            src_ref=mine(peer), dst_ref=out_ref.at[peer], send_sem=send_sems.at[k], recv_sem=recv_sems.at[k],
            device_id=(x, y, c), device_id_type=MESH) if arrivals else None
        pairs.append((send, arrival))
    return pltpu.make_async_copy(mine(me), out_ref.at[me], local_sem), pairs


def _direct_start(*refs, gather):
    local, pairs = _direct_copies(*refs, gather, False)
    local.start()
    for send, _ in pairs:
        send.start()


def _direct_wait(*refs, gather):
    local, pairs = _direct_copies(*refs, gather, True)
    for send, arrival in pairs:
        send.wait_send()
        arrival.wait_recv()
    local.wait()


COMM_SCRATCH = [pltpu.SemaphoreType.DMA((N_DEV,)), pltpu.SemaphoreType.DMA((N_DEV,)), pltpu.SemaphoreType.DMA]


def _exchange(src, *, gather=False, name):
    def body(*refs):
        _direct_start(*refs, gather=gather)
        _direct_wait(*refs, gather=gather)

    blk = tuple(src.shape) if gather else tuple(src.shape[1:])
    return pl.pallas_call(
        body, name=name,
        out_shape=jax.ShapeDtypeStruct((N_DEV,) + blk, src.dtype),
        in_specs=[pl.BlockSpec(memory_space=pltpu.HBM)],
        out_specs=pl.BlockSpec(memory_space=pltpu.HBM),
        scratch_shapes=COMM_SCRATCH,
    )(src)


def _call(body, *, name, grid, out_shape, in_specs, out_specs, args, scratch_shapes=(), sem, comm=None, edge=None):
    out_shape, out_specs = tuple(out_shape), tuple(out_specs)
    if comm is None:
        return pl.pallas_call(body, name=name, grid=grid, out_shape=out_shape, in_specs=list(in_specs),
                              out_specs=out_specs, scratch_shapes=list(scratch_shapes),
                              compiler_params=_params(sem))(*args)
    src, gather = comm
    n_in, n_out, n_scr = len(args), len(out_shape), len(scratch_shapes)
    blk = tuple(src.shape) if gather else tuple(src.shape[1:])
    hbm = pl.BlockSpec(memory_space=pltpu.HBM)

    def carrying(*refs):
        ins, src_ref = refs[:n_in], refs[n_in]
        outs, dst_ref = refs[n_in + 1:n_in + 1 + n_out], refs[n_in + 1 + n_out]
        own = refs[n_in + 2 + n_out:n_in + 2 + n_out + n_scr]
        comm_refs = (src_ref, dst_ref) + tuple(refs[n_in + 2 + n_out + n_scr:])
        first, last = edge()

        @pl.when(first)
        def _():
            _direct_start(*comm_refs, gather=gather)

        body(*ins, *outs, *own)

        @pl.when(last)
        def _():
            _direct_wait(*comm_refs, gather=gather)

    return pl.pallas_call(
        carrying, name=name, grid=grid,
        out_shape=out_shape + (jax.ShapeDtypeStruct((N_DEV,) + blk, src.dtype),),
        in_specs=list(in_specs) + [hbm], out_specs=out_specs + (hbm,),
        scratch_shapes=list(scratch_shapes) + COMM_SCRATCH,
        compiler_params=_params(("arbitrary",) * len(grid)),
    )(*args, src)


def _gather(src, *, name):
    def body(src_ref, out_ref, send_sems, recv_sems, local_sem):
        x, y, c = lax.axis_index("x"), lax.axis_index("y"), lax.axis_index("c")
        sibling = (x, y, 1 - c)
        chips = [(1 - x, y), (x, 1 - y), (1 - x, 1 - y)]

        def rows(px, py, pc):
            return out_ref.at[4 * px + 2 * py + pc]

        def copy(k, block, to, from_src=False):
            return pltpu.make_async_remote_copy(
                src_ref=src_ref if from_src else rows(*block), dst_ref=rows(*block), send_sem=send_sems.at[k],
                recv_sem=recv_sems.at[k], device_id=to, device_id_type=MESH)

        me = (x, y, c)
        mine = pltpu.make_async_copy(src_ref, rows(*me), local_sem)
        mine.start()
        first = [copy(0, me, sibling, from_src=True)]
        first += [copy(1 + j, me, (*chip, c), from_src=True) for j, chip in enumerate(chips)]
        for cp in first:
            cp.start()
        passed = [copy(4 + j, (*chip, c), sibling) for j, chip in enumerate(chips)]
        for j, chip in enumerate(chips):
            copy(1 + j, (*chip, c), me).wait_recv()
            passed[j].start()
        copy(0, sibling, me).wait_recv()
        for j, chip in enumerate(chips):
            copy(4 + j, (*chip, 1 - c), me).wait_recv()
        for cp in first + passed:
            cp.wait_send()
        mine.wait()

    return pl.pallas_call(
        body, name=name,
        out_shape=jax.ShapeDtypeStruct((N_DEV,) + tuple(src.shape), src.dtype),
        in_specs=[pl.BlockSpec(memory_space=pltpu.HBM)],
        out_specs=pl.BlockSpec(memory_space=pltpu.HBM),
        scratch_shapes=[pltpu.SemaphoreType.DMA((7,)), pltpu.SemaphoreType.DMA((7,)), pltpu.SemaphoreType.DMA],
    )(src)


def _mm(a, b, *, tm, tn, tk, ta=False, tb=False, out_dtype=F32, name):
    b_list = list(b) if isinstance(b, (list, tuple)) else [b]
    assert not (tb and len(b_list) > 1)
    m, kdim = (a.shape[1], a.shape[0]) if ta else a.shape
    joff = [0]
    for piece in b_list:
        joff.append(joff[-1] + (piece.shape[0] if tb else piece.shape[1]) // tn)
    nk, n = kdim // tk, joff[-1] * tn
    dims = (((0 if ta else 1,), (1 if tb else 0,)), ((), ()))

    def mine(j, p):
        return (j >= joff[p]) & (j < joff[p + 1])

    def body(*refs):
        a_ref, b_refs, o_ref = refs[0], refs[1:1 + len(b_list)], refs[1 + len(b_list)]
        acc_ref = refs[-1] if nk > 1 else None
        j, k = pl.program_id(1), pl.program_id(2)
        for p, b_ref in enumerate(b_refs):
            def step(b_ref=b_ref):
                part = lax.dot_general(a_ref[...].astype(BF16), b_ref[...].astype(BF16), dims,
                                       preferred_element_type=F32)
                if nk == 1:
                    o_ref[...] = part.astype(out_dtype)
                else:
                    _accumulate(acc_ref, part, k == 0)

            if len(b_list) == 1:
                step()
            else:
                pl.when(mine(j, p))(step)
        if nk > 1:
            @pl.when(k == nk - 1)
            def _():
                o_ref[...] = acc_ref[...].astype(out_dtype)

    def b_spec(p):
        jj = lambda j: jnp.clip(j - joff[p], 0, joff[p + 1] - joff[p] - 1)
        kk = (lambda j, k: k) if len(b_list) == 1 else (lambda j, k: jnp.where(mine(j, p), k, 0))
        return (pl.BlockSpec((tn, tk), lambda i, j, k: (jj(j), kk(j, k))) if tb
                else pl.BlockSpec((tk, tn), lambda i, j, k: (kk(j, k), jj(j))))

    a_spec = pl.BlockSpec((tk, tm), lambda i, j, k: (k, i)) if ta else pl.BlockSpec((tm, tk), lambda i, j, k: (i, k))
    return pl.pallas_call(
        body, name=name, grid=(m // tm, n // tn, nk),
        out_shape=jax.ShapeDtypeStruct((m, n), out_dtype),
        in_specs=[a_spec] + [b_spec(p) for p in range(len(b_list))],
        out_specs=pl.BlockSpec((tm, tn), lambda i, j, k: (i, j)),
        scratch_shapes=[pltpu.VMEM((tm, tn), F32)] if nk > 1 else [],
        compiler_params=_params(("parallel", "parallel", "arbitrary")),
    )(a, *b_list)


def _dx_from_pieces(pieces, w, add, *, tm, ln=None, name):
    s = pieces[0].shape[0]
    widths = [p.shape[1] for p in pieces]

    def body(*refs):
        p_refs, (w_ref, add_ref) = refs[:len(pieces)], refs[len(pieces):len(pieces) + 2]
        rest = refs[len(pieces) + 2:]
        pg = jnp.zeros((1, D_MODEL), F32)
        pb = jnp.zeros((1, D_MODEL), F32)
        for rs in _row_parts(tm):
            r = ALPHA * add_ref[rs, :]
            off = 0
            for p_ref, width in zip(p_refs, widths):
                r = r + lax.dot_general(p_ref[rs, :], w_ref[:, off:off + width], (((1,), (1,)), ((), ())),
                                        preferred_element_type=F32)
                off += width
            if ln is None:
                rest[0][rs, :] = r
            else:
                dy, qg, qb = _ln_bwd_rows(r, rest[0][rs, :], rest[1][...])
                rest[2][rs, :] = dy
                rest[3][rs, :] = dy.astype(BF16)
                pg, pb = pg + qg, pb + qb
        if ln is not None:
            _accumulate(rest[4], pg, pl.program_id(0) == 0)
            _accumulate(rest[5], pb, pl.program_id(0) == 0)

    row = lambda i: (i, 0)
    tile = pl.BlockSpec((tm, D_MODEL), row)
    vec = pl.BlockSpec((1, D_MODEL), lambda i: (0, 0))
    in_specs = ([pl.BlockSpec((tm, width), row) for width in widths]
                + [pl.BlockSpec((D_MODEL, sum(widths)), lambda i: (0, 0)), tile])
    args = list(pieces) + [w, add]
    out_shape, out_specs = jax.ShapeDtypeStruct((s, D_MODEL), F32), tile
    if ln is not None:
        in_specs += [tile, vec]
        args += list(ln)
        out_shape = (jax.ShapeDtypeStruct((s, D_MODEL), F32), jax.ShapeDtypeStruct((s, D_MODEL), BF16),
                     jax.ShapeDtypeStruct((1, D_MODEL), F32), jax.ShapeDtypeStruct((1, D_MODEL), F32))
        out_specs = (tile, tile, vec, vec)
    return pl.pallas_call(
        body, name=name, grid=(s // tm,), out_shape=out_shape, in_specs=in_specs, out_specs=out_specs,
        compiler_params=_params(("arbitrary",) if ln is not None else ("parallel",)),
    )(*args)


def _ln_bwd_rows(do, y, g):
    mu = jnp.mean(y, axis=-1, keepdims=True)
    yc = y - mu
    var = jnp.mean(yc * yc, axis=-1, keepdims=True)
    rstd = lax.rsqrt(var + LN_EPS)
    xhat = yc * rstd
    dxhat = do * g
    dy = rstd * (dxhat - jnp.mean(dxhat, axis=-1, keepdims=True) - xhat * jnp.mean(dxhat * xhat, axis=-1, keepdims=True))
    return dy, jnp.sum(do * xhat, axis=0, keepdims=True), jnp.sum(do, axis=0, keepdims=True)


def _accumulate(ref, val, first):
    @pl.when(first)
    def _():
        ref[...] = val

    @pl.when(jnp.logical_not(first))
    def _():
        ref[...] += val


def _layer_norm_rows(y, g, b):
    mu = jnp.mean(y, axis=-1, keepdims=True)
    yc = y - mu
    var = jnp.mean(yc * yc, axis=-1, keepdims=True)
    return yc * lax.rsqrt(var + LN_EPS) * g + b


def _proj_ln(a, w, res, g, b, *, tm, name):
    s, kdim = a.shape

    def body(a_ref, w_ref, res_ref, g_ref, b_ref, pre_ref, obf_ref):
        for rs in _row_parts(tm):
            h = jnp.dot(a_ref[rs, :], w_ref[...], preferred_element_type=F32)
            pre = ALPHA * res_ref[rs, :] + h
            pre_ref[rs, :] = pre
            obf_ref[rs, :] = _layer_norm_rows(pre, g_ref[...], b_ref[...]).astype(BF16)

    row = lambda i: (i, 0)
    fix = lambda i: (0, 0)
    return pl.pallas_call(
        body, name=name, grid=(s // tm,),
        out_shape=(jax.ShapeDtypeStruct((s, D_MODEL), F32), jax.ShapeDtypeStruct((s, D_MODEL), BF16)),
        in_specs=[pl.BlockSpec((tm, kdim), row), pl.BlockSpec((kdim, D_MODEL), fix), pl.BlockSpec((tm, D_MODEL), row),
                  pl.BlockSpec((1, D_MODEL), fix), pl.BlockSpec((1, D_MODEL), fix)],
        out_specs=(pl.BlockSpec((tm, D_MODEL), row),) * 2,
        compiler_params=_params(("parallel",)),
    )(a, w, res, g, b)


def _row_parts(tm):
    sub = min(tm, SUB_ROWS)
    return [pl.ds(r * sub, sub) for r in range(tm // sub)]


def _ffn_in(xbf, w, *, tm, tn, comm=None, name):
    s = xbf.shape[0]
    nj = D_FF // tn
    ni = s // tm

    def body(x_ref, wg_ref, wu_ref, g_ref, u_ref, act_ref):
        for rs in _row_parts(tm):
            xv = x_ref[rs, :]
            gg = jnp.dot(xv, wg_ref[...], preferred_element_type=F32)
            uu = jnp.dot(xv, wu_ref[...], preferred_element_type=F32)
            g_ref[rs, :] = gg.astype(BF16)
            u_ref[rs, :] = uu.astype(BF16)
            act_ref[rs, :] = (gg * _sigmoid(gg) * uu).astype(BF16)

    out = jax.ShapeDtypeStruct((s, D_FF), BF16)
    tile = pl.BlockSpec((tm, tn), lambda j, i: (i, j))
    return _call(
        body, name=name, grid=(nj, ni),
        out_shape=(out, out, out),
        in_specs=[pl.BlockSpec((tm, D_MODEL), lambda j, i: (i, 0)), pl.BlockSpec((D_MODEL, tn), lambda j, i: (0, j)),
                  pl.BlockSpec((D_MODEL, tn), lambda j, i: (0, j + nj))],
        out_specs=(tile, tile, tile), args=(xbf, w, w), sem=("parallel", "parallel"), comm=comm,
        edge=lambda: ((pl.program_id(0) == 0) & (pl.program_id(1) == 0),
                      (pl.program_id(0) == nj - 1) & (pl.program_id(1) == ni - 1)))


def _ffn_bwd_act(dybf, w_out, g, u, *, tm, tn, name):
    s = dybf.shape[0]

    def body(dy_ref, w_ref, g_ref, u_ref, dg_ref, du_ref):
        for rs in _row_parts(tm):
            dact = lax.dot_general(dy_ref[rs, :], w_ref[...], (((1,), (1,)), ((), ())), preferred_element_type=F32)
            gg = g_ref[rs, :].astype(F32)
            uu = u_ref[rs, :].astype(F32)
            sg = _sigmoid(gg)
            dg_ref[rs, :] = (dact * uu * sg * (1.0 + gg * (1.0 - sg))).astype(BF16)
            du_ref[rs, :] = (dact * gg * sg).astype(BF16)

    out = jax.ShapeDtypeStruct((s, D_FF), BF16)
    tile = pl.BlockSpec((tm, tn), lambda j, i: (i, j))
    return pl.pallas_call(
        body, name=name, grid=(D_FF // tn, s // tm),
        out_shape=(out, out),
        in_specs=[pl.BlockSpec((tm, D_MODEL), lambda j, i: (i, 0)), pl.BlockSpec((tn, D_MODEL), lambda j, i: (j, 0)),
                  tile, tile],
        out_specs=(tile, tile),
        compiler_params=_params(("parallel", "parallel")),
    )(dybf, w_out, g, u)


def _ffn_out_ple(act, w_out, res_pre, res_g, res_b, g, b, p, w_gate, w_proj, *, tm, name):
    s = act.shape[0]

    def body(a_ref, w_ref, res_ref, rg_ref, rb_ref, g_ref, b_ref, p_ref, wg_ref, wp_ref, pre_ref, x2bf_ref, o_ref,
             obf_ref):
        for rs in _row_parts(tm):
            res = _layer_norm_rows(res_ref[rs, :], rg_ref[...], rb_ref[...])
            pre = ALPHA * res + jnp.dot(a_ref[rs, :], w_ref[...], preferred_element_type=F32)
            x2 = _layer_norm_rows(pre, g_ref[...], b_ref[...])
            x2bf = x2.astype(BF16)
            pre_ref[rs, :] = pre
            x2bf_ref[rs, :] = x2bf
            gate = jnp.dot(x2bf, wg_ref[...], preferred_element_type=F32)
            pp = jnp.dot(p_ref[rs, :].astype(BF16), wp_ref[...], preferred_element_type=F32)
            out = x2 + _sigmoid(gate) * pp
            o_ref[rs, :] = out
            obf_ref[rs, :] = out.astype(BF16)

    row = lambda i: (i, 0)
    fix = lambda i: (0, 0)
    tile = pl.BlockSpec((tm, D_MODEL), row)
    vec = pl.BlockSpec((1, D_MODEL), fix)
    act_t = lambda dt: jax.ShapeDtypeStruct((s, D_MODEL), dt)
    return pl.pallas_call(
        body, name=name, grid=(s // tm,),
        out_shape=(act_t(F32), act_t(BF16), act_t(F32), act_t(BF16)),
        in_specs=[pl.BlockSpec((tm, D_FF), row), pl.BlockSpec((D_FF, D_MODEL), fix), tile, vec, vec, vec, vec,
                  pl.BlockSpec((tm, PLE_DIM), row), pl.BlockSpec((D_MODEL, D_MODEL), fix),
                  pl.BlockSpec((PLE_DIM, D_MODEL), fix)],
        out_specs=(tile, tile, tile, tile),
        compiler_params=_params(("parallel",)),
    )(act, w_out, res_pre, res_g, res_b, g, b, p, w_gate, w_proj)


def _ple_ln_bwd(dx3, x2bf, p, w_gate, w_proj, pre, g, *, tm, name):
    s = dx3.shape[0]

    def body(d_ref, xbf_ref, p_ref, wg_ref, wp_ref, pre_ref, g_ref, da_ref, dpp_ref, dy_ref, dybf_ref, dg_ref, db_ref):
        pg = jnp.zeros((1, D_MODEL), F32)
        pb = jnp.zeros((1, D_MODEL), F32)
        for rs in _row_parts(tm):
            d = d_ref[rs, :]
            a = jnp.dot(xbf_ref[rs, :], wg_ref[...], preferred_element_type=F32)
            pp = jnp.dot(p_ref[rs, :].astype(BF16), wp_ref[...], preferred_element_type=F32)
            sg = _sigmoid(a)
            da = (d * pp * sg * (1.0 - sg)).astype(BF16)
            da_ref[rs, :] = da
            dpp_ref[rs, :] = (d * sg).astype(BF16)
            dx2 = d + lax.dot_general(da, wg_ref[...], (((1,), (1,)), ((), ())), preferred_element_type=F32)
            dy, qg, qb = _ln_bwd_rows(dx2, pre_ref[rs, :], g_ref[...])
            dy_ref[rs, :] = dy
            dybf_ref[rs, :] = dy.astype(BF16)
            pg, pb = pg + qg, pb + qb
        _accumulate(dg_ref, pg, pl.program_id(0) == 0)
        _accumulate(db_ref, pb, pl.program_id(0) == 0)

    row = lambda i: (i, 0)
    fix = lambda i: (0, 0)
    tile = pl.BlockSpec((tm, D_MODEL), row)
    vec = pl.BlockSpec((1, D_MODEL), fix)
    act = lambda dt: jax.ShapeDtypeStruct((s, D_MODEL), dt)
    return pl.pallas_call(
        body, name=name, grid=(s // tm,),
        out_shape=(act(BF16), act(BF16), act(F32), act(BF16), jax.ShapeDtypeStruct((1, D_MODEL), F32),
                   jax.ShapeDtypeStruct((1, D_MODEL), F32)),
        in_specs=[tile, tile, pl.BlockSpec((tm, PLE_DIM), row), pl.BlockSpec((D_MODEL, D_MODEL), fix),
                  pl.BlockSpec((PLE_DIM, D_MODEL), fix), tile, vec],
        out_specs=(tile, tile, tile, tile, vec, vec),
        compiler_params=_params(("arbitrary",)),
    )(dx3, x2bf, p, w_gate, w_proj, pre, g)


def _loss_head(y, target, *, tm, name):
    s = y.shape[0]

    def body(y_ref, t_ref, dy_ref, loss_ref, acc_ref):
        err = y_ref[...] - t_ref[...]
        dy_ref[...] = err * (1.0 / D_MODEL)
        part = jnp.sum(err * err, axis=0, keepdims=True)

        @pl.when(pl.program_id(0) == 0)
        def _():
            acc_ref[...] = part

        @pl.when(pl.program_id(0) > 0)
        def _():
            acc_ref[...] += part

        @pl.when(pl.program_id(0) == pl.num_programs(0) - 1)
        def _():
            tot = jnp.sum(acc_ref[...], axis=1, keepdims=True) * (0.5 / D_MODEL)
            loss_ref[...] = jnp.broadcast_to(tot, (8, LANES))

    row = lambda i: (i, 0)
    return pl.pallas_call(
        body, name=name, grid=(s // tm,),
        out_shape=(jax.ShapeDtypeStruct((s, D_MODEL), F32), jax.ShapeDtypeStruct((8, LANES), F32)),
        in_specs=[pl.BlockSpec((tm, D_MODEL), row), pl.BlockSpec((tm, D_MODEL), row)],
        out_specs=(pl.BlockSpec((tm, D_MODEL), row), pl.BlockSpec((8, LANES), lambda i: (0, 0))),
        scratch_shapes=[pltpu.VMEM((1, D_MODEL), F32)],
        compiler_params=_params(("arbitrary",)),
    )(y, target)


def _rope_tables(s):
    inv = ROPE_THETA ** (-jnp.arange(0, ROPE_DIM, 2, dtype=F32) / ROPE_DIM)
    ang = jnp.arange(s, dtype=F32)[:, None] * inv[None, :]
    cos, sin = jnp.cos(ang), jnp.sin(ang)
    ones = jnp.ones((s, HEAD_DIM - ROPE_DIM), F32)
    c_head = jnp.concatenate([cos, cos, ones], axis=1)
    s_head = jnp.concatenate([-sin, sin, 0.0 * ones], axis=1)
    return jnp.concatenate([c_head, c_head], axis=1), jnp.concatenate([s_head, s_head], axis=1)


def _rope(v, cos, sin):
    n = v.shape[1] // LANES
    width = v.shape[1]
    cos_w = jnp.tile(cos, (1, n)) if n > 1 else cos
    sin_w = jnp.tile(sin, (1, n)) if n > 1 else sin
    dim = lax.broadcasted_iota(jnp.int32, (1, width), 1) % HEAD_DIM
    partner = jnp.where(dim < ROPE_DIM // 2, pltpu.roll(v, width - ROPE_DIM // 2, 1), pltpu.roll(v, ROPE_DIM // 2, 1))
    return v * cos_w + partner * sin_w


def _unrope(dv, cos, sin):
    n = dv.shape[1] // LANES
    width = dv.shape[1]
    cos_w = jnp.tile(cos, (1, n)) if n > 1 else cos
    sin_w = jnp.tile(sin, (1, n)) if n > 1 else sin
    t = dv * sin_w
    dim = lax.broadcasted_iota(jnp.int32, (1, width), 1) % HEAD_DIM
    partner = jnp.where(dim < ROPE_DIM // 2, pltpu.roll(t, width - ROPE_DIM // 2, 1),
                        jnp.where(dim < ROPE_DIM, pltpu.roll(t, ROPE_DIM // 2, 1), 0.0))
    return dv * cos_w + partner


def _att_mask(i, nb):
    rows = GROUP * ATT_BLOCK
    r = lax.broadcasted_iota(jnp.int32, (rows, 3 * ATT_BLOCK), 0) % ATT_BLOCK
    cidx = lax.broadcasted_iota(jnp.int32, (rows, 3 * ATT_BLOCK), 1)
    rel = r + ATT_BLOCK - cidx
    ok = (rel <= ATT_BLOCK) & (rel >= -ATT_BLOCK)
    ok = ok & ((cidx >= ATT_BLOCK) | (i > 0)) & ((cidx < 2 * ATT_BLOCK) | (i < nb - 1))
    return ok


def _att_mask_t(i, nb):
    cols = GROUP * ATT_BLOCK
    cidx = lax.broadcasted_iota(jnp.int32, (3 * ATT_BLOCK, cols), 0)
    r = lax.broadcasted_iota(jnp.int32, (3 * ATT_BLOCK, cols), 1) % ATT_BLOCK
    rel = r + ATT_BLOCK - cidx
    ok = (rel <= ATT_BLOCK) & (rel >= -ATT_BLOCK)
    return ok & ((cidx >= ATT_BLOCK) | (i > 0)) & ((cidx < 2 * ATT_BLOCK) | (i < nb - 1))


def _sink_lanes(sink_ref, h):
    cols = GROUP * ATT_BLOCK
    grp = lax.broadcasted_iota(jnp.int32, (1, cols), 1) // ATT_BLOCK
    out = jnp.zeros((1, cols), F32)
    for gq in range(GROUP):
        out = jnp.where(grp == gq, sink_ref[GROUP * h + gq], out)
    return out


def _half_mask(half):
    lane = lax.broadcasted_iota(jnp.int32, (1, LANES), 1)
    return (lane // HEAD_DIM) == half


def _stack_q(q, h):
    parts = []
    for gq in range(GROUP):
        n = GROUP * h + gq
        grp = q[:, LANES * (n // 2):LANES * (n // 2 + 1)]
        grp = jnp.where(_half_mask(n % 2), grp, 0.0)
        if n % 2 != h % 2:
            grp = pltpu.roll(grp, HEAD_DIM, 1)
        parts.append(grp)
    return jnp.concatenate(parts, axis=0)


def _unstack_q(stacked, h, acc):
    for gq in range(GROUP):
        n = GROUP * h + gq
        grp = stacked[ATT_BLOCK * gq:ATT_BLOCK * (gq + 1), :]
        grp = jnp.where(_half_mask(h % 2), grp, 0.0)
        if n % 2 != h % 2:
            grp = pltpu.roll(grp, HEAD_DIM, 1)
        acc[n // 2] = grp if acc[n // 2] is None else acc[n // 2] + grp
    return acc


def _sink_rows(sink_ref, h):
    rows = GROUP * ATT_BLOCK
    grp = lax.broadcasted_iota(jnp.int32, (rows, 1), 0) // ATT_BLOCK
    out = jnp.zeros((rows, 1), F32)
    for gq in range(GROUP):
        out = jnp.where(grp == gq, sink_ref[GROUP * h + gq], out)
    return out


def _att_probs(qs, kh, sink, valid):
    s = lax.dot_general(qs, kh, (((1,), (1,)), ((), ())), preferred_element_type=F32)
    s = jnp.where(valid, s, NEG)
    m = jnp.maximum(jnp.max(s, axis=-1, keepdims=True), sink)
    p = jnp.exp(s - m)
    es = jnp.exp(sink - m)
    den = jnp.sum(p, axis=-1, keepdims=True) + es
    inv = 1.0 / den
    return p * inv, es * inv


ATT_STEP_FWD = 2
ATT_STEP_BWD = 4


def _att_specs(nb, step):
    rows = step * ATT_BLOCK
    prev = lambda i: (jnp.maximum(step * i - 1, 0), 0)
    cur = lambda i: (i, 0)
    nxt = lambda i: (jnp.minimum(step * (i + 1), nb - 1), 0)
    kv = lambda f: (lambda i: (f(i)[0], 2))
    shapes = ((rows, cur), (ATT_BLOCK, prev), (rows, cur), (ATT_BLOCK, nxt))
    tab = [pl.BlockSpec((r, LANES), f) for r, f in shapes]
    z = [pl.BlockSpec((rows, D_MODEL), cur)] + [pl.BlockSpec((r, 2 * KV_DIM), kv(f)) for r, f in shapes[1:]]
    return z, tab


def _att_load(zq_ref, kp_ref, kc_ref, kn_ref, cq_ref, sq_ref, cp_ref, sp_ref, cc_ref, sc_ref, cn_ref, sn_ref):
    q = (_rope(zq_ref[...], cq_ref[...], sq_ref[...]) * (HEAD_DIM ** -0.5))
    ks, vs = [], []
    for ref, c_ref, s_ref in ((kp_ref, cp_ref, sp_ref), (kc_ref, cc_ref, sc_ref), (kn_ref, cn_ref, sn_ref)):
        kvb = ref[...]
        ks.append(_rope(kvb[:, :KV_DIM], c_ref[...], s_ref[...]))
        vs.append(kvb[:, KV_DIM:])
    return q, jnp.concatenate(ks, axis=0).astype(BF16), jnp.concatenate(vs, axis=0).astype(BF16)


def _att_rows(sub):
    return (slice(ATT_BLOCK * sub, ATT_BLOCK * (sub + 1)), slice(ATT_BLOCK * sub, ATT_BLOCK * (sub + 3)))


def _att_fwd(z, sink, cos, sin, *, comm=None, name):
    s = z.shape[0]
    nb = s // ATT_BLOCK
    group = min(ATT_STEP_FWD, nb)
    steps = nb // group

    def body(zq_ref, kp_ref, kc_ref, kn_ref, cq_ref, cp_ref, cc_ref, cn_ref, sq_ref, sp_ref, sc_ref, sn_ref, sink_ref,
             o_ref):
        i = pl.program_id(0)
        q, k, v = _att_load(zq_ref, kp_ref, kc_ref, kn_ref, cq_ref, sq_ref, cp_ref, sp_ref, cc_ref, sc_ref, cn_ref, sn_ref)
        for sub in range(group):
            qrows, krows = _att_rows(sub)
            valid = _att_mask(group * i + sub, nb)
            acc = [None] * (N_Q_HEADS // 2)
            for h in range(N_KV_HEADS):
                lanes = slice(LANES * (h // 2), LANES * (h // 2 + 1))
                qs = _stack_q(q[qrows], h).astype(BF16)
                prob, _ = _att_probs(qs, k[krows, lanes], _sink_rows(sink_ref, h), valid)
                oh = jnp.dot(prob.astype(BF16), v[krows, lanes], preferred_element_type=F32)
                acc = _unstack_q(oh, h, acc)
            o_ref[qrows, :] = jnp.concatenate(acc, axis=1).astype(BF16)

    zspecs, tab = _att_specs(nb, group)
    return _call(
        body, name=name, grid=(steps,),
        out_shape=(jax.ShapeDtypeStruct((s, D_MODEL), BF16),),
        in_specs=zspecs + tab + tab + [pl.BlockSpec(memory_space=pltpu.SMEM)],
        out_specs=(pl.BlockSpec((group * ATT_BLOCK, D_MODEL), lambda i: (i, 0)),),
        args=(z, z, z, z, cos, cos, cos, cos, sin, sin, sin, sin, sink), sem=("parallel",), comm=comm,
        edge=lambda: (pl.program_id(0) == 0, pl.program_id(0) == steps - 1))


def _att_bwd(z, do, sink, cos, sin, *, comm=None, name):
    s = z.shape[0]
    nb = s // ATT_BLOCK
    group = min(ATT_STEP_BWD, nb)
    steps = nb // group

    def body(zq_ref, kp_ref, kc_ref, kn_ref, cq_ref, cp_ref, cc_ref, cn_ref, sq_ref, sp_ref, sc_ref, sn_ref, sink_ref,
             do_ref, dq_ref, part_ref, dsink_ref):
        i = pl.program_id(0)
        q, k, v = _att_load(zq_ref, kp_ref, kc_ref, kn_ref, cq_ref, sq_ref, cp_ref, sp_ref, cc_ref, sc_ref, cn_ref, sn_ref)
        dsink = None
        nt = (((1,), (1,)), ((), ()))
        for sub in range(group):
            qrows, krows = _att_rows(sub)
            valid = _att_mask_t(group * i + sub, nb)
            dout = do_ref[qrows, :].astype(F32)
            dq_acc = [None] * (N_Q_HEADS // 2)
            dk_acc = [None] * 2
            dv_acc = [None] * 2
            rows = []
            for h in range(N_KV_HEADS):
                grp = h // 2
                lanes = slice(LANES * grp, LANES * (grp + 1))
                kh, vh = k[krows, lanes], v[krows, lanes]
                qs = _stack_q(q[qrows], h).astype(BF16)
                dos = _stack_q(dout, h).astype(BF16)
                sink = _sink_lanes(sink_ref, h)
                sc = jnp.where(valid, lax.dot_general(kh, qs, nt, preferred_element_type=F32), NEG)
                m = jnp.maximum(jnp.max(sc, axis=0, keepdims=True), sink)
                p = jnp.exp(sc - m)
                es = jnp.exp(sink - m)
                inv = 1.0 / (jnp.sum(p, axis=0, keepdims=True) + es)
                prob = p * inv
                dprob = lax.dot_general(vh, dos, nt, preferred_element_type=F32)
                delta = jnp.sum(prob * dprob, axis=0, keepdims=True)
                dsc = (prob * (dprob - delta)).astype(BF16)
                dsk = -(es * inv) * delta
                for gq in range(GROUP):
                    tot = jnp.sum(dsk[:, ATT_BLOCK * gq:ATT_BLOCK * (gq + 1)], axis=1, keepdims=True)
                    rows.append(jnp.broadcast_to(tot, (1, LANES)))
                dqs = lax.dot_general(dsc, kh, (((0,), (0,)), ((), ())), preferred_element_type=F32)
                dq_acc = _unstack_q(dqs, h, dq_acc)
                dkh = jnp.dot(dsc, qs, preferred_element_type=F32)
                dvh = jnp.dot(prob.astype(BF16), dos, preferred_element_type=F32)
                dk_acc[grp] = dkh if dk_acc[grp] is None else dk_acc[grp] + dkh
                dv_acc[grp] = dvh if dv_acc[grp] is None else dv_acc[grp] + dvh
            dq = jnp.concatenate(dq_acc, axis=1) * (HEAD_DIM ** -0.5)
            dq_ref[qrows, :] = _unrope(dq, cq_ref[qrows, :], sq_ref[qrows, :]).astype(BF16)
            part = jnp.concatenate(dk_acc + dv_acc, axis=1)
            for wdw in range(3):
                part_ref[sub, wdw] = part[ATT_BLOCK * wdw:ATT_BLOCK * (wdw + 1), :]
            mine = jnp.concatenate(rows, axis=0)
            dsink = mine if dsink is None else dsink + mine
        _accumulate(dsink_ref, dsink, i == 0)

    zspecs, tab = _att_specs(nb, group)
    tile = pl.BlockSpec((group * ATT_BLOCK, D_MODEL), lambda i: (i, 0))
    return _call(
        body, name=name, grid=(steps,),
        out_shape=(jax.ShapeDtypeStruct((s, D_MODEL), BF16), jax.ShapeDtypeStruct((nb, 3, ATT_BLOCK, 2 * KV_DIM), F32),
                   jax.ShapeDtypeStruct((N_Q_HEADS, LANES), F32)),
        in_specs=zspecs + tab + tab + [pl.BlockSpec(memory_space=pltpu.SMEM), tile],
        out_specs=(tile, pl.BlockSpec((group, 3, ATT_BLOCK, 2 * KV_DIM), lambda i: (i, 0, 0, 0)),
                   pl.BlockSpec((N_Q_HEADS, LANES), lambda i: (0, 0))),
        args=(z, z, z, z, cos, cos, cos, cos, sin, sin, sin, sin, sink, do), sem=("arbitrary",), comm=comm,
        edge=lambda: (pl.program_id(0) == 0, pl.program_id(0) == steps - 1))


def _att_bwd_kv(part, cos, sin, *, name):
    nb = part.shape[0]

    def body(pn_ref, pc_ref, pp_ref, c_ref, s_ref, o_ref):
        j = pl.program_id(0)
        tot = pc_ref[...]
        tot = tot + jnp.where(j < nb - 1, pn_ref[...], 0.0)
        tot = tot + jnp.where(j > 0, pp_ref[...], 0.0)
        dk = _unrope(tot[:, :KV_DIM], c_ref[...], s_ref[...])
        o_ref[...] = jnp.concatenate([dk, tot[:, KV_DIM:]], axis=1).astype(BF16)

    blk = (None, None, ATT_BLOCK, 2 * KV_DIM)
    return pl.pallas_call(
        body, name=name, grid=(nb,),
        out_shape=jax.ShapeDtypeStruct((nb * ATT_BLOCK, 2 * KV_DIM), BF16),
        in_specs=[pl.BlockSpec(blk, lambda j: (jnp.minimum(j + 1, nb - 1), 0, 0, 0)),
                  pl.BlockSpec(blk, lambda j: (j, 1, 0, 0)),
                  pl.BlockSpec(blk, lambda j: (jnp.maximum(j - 1, 0), 2, 0, 0)),
                  pl.BlockSpec((ATT_BLOCK, LANES), lambda j: (j, 0)), pl.BlockSpec((ATT_BLOCK, LANES), lambda j: (j, 0))],
        out_specs=pl.BlockSpec((ATT_BLOCK, 2 * KV_DIM), lambda j: (j, 0)),
        compiler_params=_params(("parallel",)),
    )(part, part, part, cos, sin)


def _bdot(a, b, dims):
    return lax.dot_general(a.astype(BF16), b.astype(BF16), (dims, ((), ())), preferred_element_type=F32)


@jax.custom_vjp
def _dot_nn(a, b):
    return _bdot(a, b, ((1,), (0,)))


@jax.custom_vjp
def _dot_nt(a, b):
    return _bdot(a, b, ((1,), (1,)))


@jax.custom_vjp
def _dot_tn(a, b):
    return _bdot(a, b, ((0,), (0,)))


_dot_nn.defvjp(lambda a, b: (_dot_nn(a, b), (a, b)), lambda r, d: (_dot_nt(d, r[1]), _dot_tn(r[0], d)))
_dot_nt.defvjp(lambda a, b: (_dot_nt(a, b), (a, b)), lambda r, d: (_dot_nn(d, r[1]), _dot_tn(d, r[0])))
_dot_tn.defvjp(lambda a, b: (_dot_tn(a, b), (a, b)), lambda r, d: (_dot_nt(r[1], d), _dot_nn(r[0], d)))


def _running_sum(v, up):
    n = v.shape[0]
    rows = lax.broadcasted_iota(jnp.int32, v.shape, 0)
    sh = 1
    while sh < n:
        if up:
            v = v + jnp.where(rows < n - sh, pltpu.roll(v, n - sh, 0), 0.0)
        else:
            v = v + jnp.where(rows >= sh, pltpu.roll(v, sh, 0), 0.0)
        sh *= 2
    return v


@jax.custom_vjp
def _sum_down(v):
    return _running_sum(v, False)


@jax.custom_vjp
def _sum_up(v):
    return _running_sum(v, True)


_sum_down.defvjp(lambda v: (_running_sum(v, False), None), lambda _, d: (_sum_up(d),))
_sum_up.defvjp(lambda v: (_running_sum(v, True), None), lambda _, d: (_sum_down(d),))

N_SUB = HG_CHUNK // HG_SUB


def _fold_blocks(v):
    out = v[:HG_CHUNK]
    for i in range(1, N_SUB):
        out = out + v[HG_CHUNK * i:HG_CHUNK * (i + 1)]
    return out


@jax.custom_vjp
def _fold(v):
    return _fold_blocks(v)


_fold.defvjp(lambda v: (_fold_blocks(v), None), lambda _, d: (jnp.concatenate([d] * N_SUB, axis=0),))


def _hg_consts(rev):
    c, sub = HG_CHUNK, HG_SUB
    rowpos = lax.broadcasted_iota(jnp.int32, (c, HG_DIM), 0)
    rr = lax.broadcasted_iota(jnp.int32, (N_SUB * c, c), 0)
    key = lax.broadcasted_iota(jnp.int32, (N_SUB * c, c), 1)
    blk, qry = rr // c, rr % c
    if rev:
        rowpos, qry, key = c - 1 - rowpos, c - 1 - qry, c - 1 - key
    keep = (key // sub == blk) & (key <= qry)
    return keep, rowpos


def _pick(b, rowpos, t):
    return jnp.sum(jnp.where(rowpos == t, b, 0.0), axis=0, keepdims=True)


def _hg_local(zq, zf, zv, lbv, consts, dots):
    dot_nn, dot_nt, dot_tn, cum, fold = dots
    keep, rowpos = consts
    sig = _sigmoid(zf)
    f = lbv + (1.0 - lbv) * sig
    g = jnp.log(f)
    k = (1.0 - lbv) * (1.0 - sig)
    q = zq * _sigmoid(zq)
    b = cum(g)
    ends = [_pick(b, rowpos, (j + 1) * HG_SUB - 1) for j in range(N_SUB)]
    b_last = ends[-1]
    b_end = b_last
    for j in range(N_SUB - 1):
        b_end = jnp.where(rowpos // HG_SUB == j, ends[j], b_end)
    kc = k * jnp.exp(b_end - b)
    qbs = [q * jnp.exp(jnp.where(rowpos >= j * HG_SUB, b - ends[j], 0.0)) for j in range(N_SUB)]
    scores = fold(jnp.where(keep, dot_nt(jnp.concatenate(qbs, axis=0), kc), 0.0))
    return dot_nn(scores, zv), q * jnp.exp(b), k * jnp.exp(b_last - b), jnp.exp(b_last)


def _hg_chunk(zq, zf, zv, lbv, st, consts, dots):
    intra, qs, kd, dec = _hg_local(zq, zf, zv, lbv, consts, dots)
    return intra + dots[1](qs, st), dec * st + dots[2](zv, kd)


def _hg_dots(diff, rev):
    if diff:
        return _dot_nn, _dot_nt, _dot_tn, (_sum_up if rev else _sum_down), _fold
    return (lambda a, b: _bdot(a, b, ((1,), (0,))), lambda a, b: _bdot(a, b, ((1,), (1,))),
            lambda a, b: _bdot(a, b, ((0,), (0,))), lambda v: _running_sum(v, rev), _fold_blocks)


def _hg_specs(ts, nch, trow):
    tile = pl.BlockSpec((ts, HG_DIM), lambda h, t: (trow(t), h))
    mats = pl.BlockSpec((None, nch, HG_DIM, HG_DIM), lambda h, t: (h, trow(t), 0, 0))
    vecs = pl.BlockSpec((None, nch, 1, HG_DIM), lambda h, t: (h, trow(t), 0, 0))
    return tile, mats, vecs


def _time_order(nch, rev):
    return range(nch - 1, -1, -1) if rev else range(nch)


def _chunk_rows(c):
    return pl.ds(c * HG_CHUNK, HG_CHUNK)


def _hg_edge(nt):
    h, t = pl.program_id(0), pl.program_id(1)
    return (h == 0) & (t == 0), (h == HG_HEADS - 1) & (t == nt - 1)


def _hg_fwd(z, lb, *, rev, ts, comm=None, name):
    s = z.shape[0]
    nt = s // ts
    nch = ts // HG_CHUNK
    fcol = HG_HEADS * (2 if rev else 1)

    def body(zq_ref, zf_ref, zv_ref, lb_ref, o_ref, st_ref, qs_ref, dec_ref, state_ref):
        @pl.when(pl.program_id(1) == 0)
        def _():
            state_ref[...] = jnp.zeros_like(state_ref)

        consts = _hg_consts(rev)
        dots = _hg_dots(False, rev)
        lbv = lb_ref[...]
        local = {}
        for c in range(nch):
            rows = _chunk_rows(c)
            zv = zv_ref[rows, :]
            intra, qs, kd, dec = _hg_local(zq_ref[rows, :], zf_ref[rows, :], zv, lbv, consts, dots)
            qs = qs.astype(BF16)
            qs_ref[rows, :] = qs
            dec_ref[c] = dec
            local[c] = (intra, qs, dec, dots[2](zv, kd))
        st = state_ref[...]
        for c in _time_order(nch, rev):
            intra, qs, dec, upd = local[c]
            st_ref[c] = st.astype(BF16)
            o_ref[_chunk_rows(c), :] = intra + _bdot(qs, st, ((1,), (1,)))
            st = dec * st + upd
        state_ref[...] = st

    trow = (lambda t: nt - 1 - t) if rev else (lambda t: t)
    col = lambda off: pl.BlockSpec((ts, HG_DIM), lambda h, t: (trow(t), off + h))
    tile, mats, vecs = _hg_specs(ts, nch, trow)
    nchunks = s // HG_CHUNK
    return _call(
        body, name=name, grid=(HG_HEADS, nt),
        out_shape=(jax.ShapeDtypeStruct((s, D_MODEL), F32),
                   jax.ShapeDtypeStruct((HG_HEADS, nchunks, HG_DIM, HG_DIM), BF16),
                   jax.ShapeDtypeStruct((s, D_MODEL), BF16),
                   jax.ShapeDtypeStruct((HG_HEADS, nchunks, 1, HG_DIM), F32)),
        in_specs=[col(0), col(fcol), col(3 * HG_HEADS), pl.BlockSpec((None, 1, HG_DIM), lambda h, t: (h, 0, 0))],
        out_specs=(tile, mats, tile, vecs), args=(z, z, z, lb),
        scratch_shapes=[pltpu.VMEM((HG_DIM, HG_DIM), F32)], sem=("parallel", "arbitrary"), comm=comm,
        edge=lambda: _hg_edge(nt))


def _hg_bwd(z, lb, states, qs, dec, dout, addq, addv, *, rev, ts, comm=None, name):
    s = z.shape[0]
    nt = s // ts
    nch = ts // HG_CHUNK
    fcol = HG_HEADS * (2 if rev else 1)
    has_add = addq is not None

    def body(*refs):
        zq_ref, zf_ref, zv_ref, lb_ref, st_ref, qs_ref, dec_ref, do_ref = refs[:8]
        aq_ref, av_ref = (refs[8], refs[9]) if has_add else (None, None)
        dq_ref, df_ref, dv_ref, dlb_ref, grad_ref = refs[-5:]

        @pl.when(pl.program_id(1) == 0)
        def _():
            grad_ref[...] = jnp.zeros_like(grad_ref)

        consts = _hg_consts(rev)
        dots = _hg_dots(True, rev)
        lbv = lb_ref[...]
        prods = {c: _bdot(do_ref[_chunk_rows(c), :], qs_ref[_chunk_rows(c), :], ((0,), (0,))) for c in range(nch)}
        gleave = {}
        gr = grad_ref[...]
        for c in reversed(_time_order(nch, rev)):
            gleave[c] = gr
            gr = dec_ref[c] * gr + prods[c]
        grad_ref[...] = gr
        dlb_blk = jnp.zeros((1, HG_DIM), F32)
        for c in range(nch):
            rows = _chunk_rows(c)
            fn = lambda a, b2, c2, d2, e2: _hg_chunk(a, b2, c2, d2, e2, consts, dots)
            _, pull = jax.vjp(fn, zq_ref[rows, :], zf_ref[rows, :], zv_ref[rows, :], lbv, st_ref[c].astype(F32))
            dq, df, dv, dlb, _ = pull((do_ref[rows, :], gleave[c]))
            if has_add:
                dq = dq + aq_ref[rows, :]
                dv = dv + av_ref[rows, :]
            dq_ref[rows, :] = dq.astype(dq_ref.dtype)
            df_ref[rows, :] = df.astype(BF16)
            dv_ref[rows, :] = dv.astype(dv_ref.dtype)
            dlb_blk = dlb_blk + dlb

        @pl.when(pl.program_id(1) == 0)
        def _():
            dlb_ref[...] = dlb_blk

        @pl.when(pl.program_id(1) > 0)
        def _():
            dlb_ref[...] += dlb_blk

    trow = (lambda t: t) if rev else (lambda t: nt - 1 - t)
    col = lambda off: pl.BlockSpec((ts, HG_DIM), lambda h, t: (trow(t), off + h))
    tile, mats, vecs = _hg_specs(ts, nch, trow)
    in_specs = [col(0), col(fcol), col(3 * HG_HEADS), pl.BlockSpec((None, 1, HG_DIM), lambda h, t: (h, 0, 0)),
                mats, tile, vecs, tile]
    args = [z, z, z, lb, states, qs, dec, dout]
    if has_add:
        in_specs += [tile, tile]
        args += [addq, addv]
    act = lambda dt: jax.ShapeDtypeStruct((s, D_MODEL), dt)
    sums = BF16 if has_add else F32
    return _call(
        body, name=name, grid=(HG_HEADS, nt),
        out_shape=(act(sums), act(BF16), act(sums), jax.ShapeDtypeStruct((HG_HEADS, 1, HG_DIM), F32)),
        in_specs=in_specs,
        out_specs=(tile, tile, tile, pl.BlockSpec((None, 1, HG_DIM), lambda h, t: (h, 0, 0))), args=tuple(args),
        scratch_shapes=[pltpu.VMEM((HG_DIM, HG_DIM), F32)], sem=("parallel", "arbitrary"), comm=comm,
        edge=lambda: _hg_edge(nt))


def _hg_post(of, ob, z, norm_g, *, tm, name):
    s = of.shape[0]

    def body(of_ref, ob_ref, gate_ref, ng_ref, y_ref):
        gn = ng_ref[...]
        for h in range(HG_HEADS):
            ln = slice(HG_DIM * h, HG_DIM * (h + 1))
            o = of_ref[:, ln] + ob_ref[:, ln]
            r = lax.rsqrt(jnp.mean(o * o, axis=-1, keepdims=True) + LN_EPS)
            gt = gate_ref[:, ln]
            y_ref[:, ln] = (o * r * gn * gt * _sigmoid(gt)).astype(BF16)

    row = lambda i: (i, 0)
    return pl.pallas_call(
        body, name=name, grid=(s // tm,),
        out_shape=jax.ShapeDtypeStruct((s, D_MODEL), BF16),
        in_specs=[pl.BlockSpec((tm, D_MODEL), row), pl.BlockSpec((tm, D_MODEL), row),
                  pl.BlockSpec((tm, D_MODEL), lambda i: (i, 4)), pl.BlockSpec((1, HG_DIM), lambda i: (0, 0))],
        out_specs=pl.BlockSpec((tm, D_MODEL), row),
        compiler_params=_params(("parallel",)),
    )(of, ob, z, norm_g)


def _hg_post_bwd(dy, of, ob, z, norm_g, *, tm, name):
    s = of.shape[0]

    def body(dy_ref, of_ref, ob_ref, gate_ref, ng_ref, do_ref, dgate_ref, dng_ref):
        gn = ng_ref[...]
        tot = jnp.zeros((1, HG_DIM), F32)
        for h in range(HG_HEADS):
            ln = slice(HG_DIM * h, HG_DIM * (h + 1))
            d = dy_ref[:, ln].astype(F32)
            o = of_ref[:, ln] + ob_ref[:, ln]
            r = lax.rsqrt(jnp.mean(o * o, axis=-1, keepdims=True) + LN_EPS)
            ohat = o * r
            gt = gate_ref[:, ln]
            sg = _sigmoid(gt)
            don = d * gt * sg
            dgate_ref[:, ln] = (d * ohat * gn * sg * (1.0 + gt * (1.0 - sg))).astype(BF16)
            tot = tot + jnp.sum(don * ohat, axis=0, keepdims=True)
            dohat = don * gn
            do_ref[:, ln] = r * (dohat - ohat * jnp.mean(dohat * ohat, axis=-1, keepdims=True))

        @pl.when(pl.program_id(0) == 0)
        def _():
            dng_ref[...] = tot

        @pl.when(pl.program_id(0) > 0)
        def _():
            dng_ref[...] += tot

    row = lambda i: (i, 0)
    return pl.pallas_call(
        body, name=name, grid=(s // tm,),
        out_shape=(jax.ShapeDtypeStruct((s, D_MODEL), F32), jax.ShapeDtypeStruct((s, D_MODEL), BF16),
                   jax.ShapeDtypeStruct((1, HG_DIM), F32)),
        in_specs=[pl.BlockSpec((tm, D_MODEL), row), pl.BlockSpec((tm, D_MODEL), row), pl.BlockSpec((tm, D_MODEL), row),
                  pl.BlockSpec((tm, D_MODEL), lambda i: (i, 4)), pl.BlockSpec((1, HG_DIM), lambda i: (0, 0))],
        out_specs=(pl.BlockSpec((tm, D_MODEL), row), pl.BlockSpec((tm, D_MODEL), row),
                   pl.BlockSpec((1, HG_DIM), lambda i: (0, 0))),
        compiler_params=_params(("arbitrary",)),
    )(dy, of, ob, z, norm_g)


def _lb_fwd(logits, *, name):
    w = logits.shape[1]

    def body(l_ref, o_ref):
        lg = l_ref[...]
        e = jnp.exp(lg - jnp.max(lg, axis=0, keepdims=True))
        sm = e / jnp.sum(e, axis=0, keepdims=True)
        o_ref[0:1, :] = sm[1:2]
        o_ref[1:2, :] = sm[1:2] + sm[2:3] + sm[3:4]

    return pl.pallas_call(body, name=name, out_shape=jax.ShapeDtypeStruct((2, w), F32))(logits)


def _lb_bwd(logits, dlb, *, name):
    w = logits.shape[1]

    def body(l_ref, d_ref, o_ref):
        lg = l_ref[...]
        e = jnp.exp(lg - jnp.max(lg, axis=0, keepdims=True))
        sm = e / jnp.sum(e, axis=0, keepdims=True)
        d1, d3 = d_ref[0:1, :], d_ref[1:2, :]
        dot = sm[1:2] * (d1 + d3) + (sm[2:3] + sm[3:4]) * d3
        o_ref[0:1, :] = -sm[0:1] * dot
        o_ref[1:2, :] = sm[1:2] * (d1 + d3 - dot)
        o_ref[2:3, :] = sm[2:3] * (d3 - dot)
        o_ref[3:4, :] = sm[3:4] * (d3 - dot)

    return pl.pallas_call(body, name=name, out_shape=jax.ShapeDtypeStruct((4, w), F32))(logits, dlb)


def _adamw(w, g, m, v, *, tr, g_off=0, name):
    rows = w.shape[0]
    parts = g.ndim == 3
    c1 = 1.0 / (1.0 - ADAM_B1 ** ADAM_STEP)
    c2 = 1.0 / (1.0 - ADAM_B2 ** ADAM_STEP)

    def body(w_ref, g_ref, m_ref, v_ref, go_ref, d_ref, mo_ref, vo_ref):
        if parts:
            gg = g_ref[0].astype(F32)
            for i in range(1, N_DEV):
                gg = gg + g_ref[i].astype(F32)
        else:
            gg = g_ref[...]
        mm = ADAM_B1 * m_ref[...] + (1.0 - ADAM_B1) * gg
        vv = ADAM_B2 * v_ref[...] + (1.0 - ADAM_B2) * (gg * gg)
        go_ref[...] = gg
        mo_ref[...] = mm
        vo_ref[...] = vv
        d_ref[...] = -ADAM_LR * ((mm * c1) / (jnp.sqrt(vv * c2) + ADAM_EPS) + ADAM_WD * w_ref[...])

    tile = pl.BlockSpec((tr, D_MODEL), lambda i: (i, 0))
    gspec = pl.BlockSpec((N_DEV, tr, D_MODEL), lambda i: (0, i + g_off // tr, 0)) if parts else tile
    out = jax.ShapeDtypeStruct((rows, D_MODEL), F32)
    return pl.pallas_call(
        body, name=name, grid=(rows // tr,),
        out_shape=(out, out, out, out),
        in_specs=[tile, gspec, tile, tile], out_specs=(tile, tile, tile, tile),
        compiler_params=_params(("parallel",)),
    )(w, g, m, v)


def _sum8(parts, *, name):
    def body(p_ref, o_ref):
        tot = p_ref[0]
        for i in range(1, N_DEV):
            tot = tot + p_ref[i]
        o_ref[...] = tot

    return pl.pallas_call(body, name=name, out_shape=jax.ShapeDtypeStruct(parts.shape[1:], parts.dtype))(parts)


def _layer_params(i):
    j = i // 2
    mix = [("att_w_qkv", j, 1), ("att_w_o", j, 0)] if i % 2 == 0 else [("hgrn_w_in", j, 1), ("hgrn_w_o", j, 0)]
    return mix + [("ffn_w_in", i, 1), ("ffn_w_out", i, 0), ("ple_w_gate", i, 0), ("ple_w_proj", i, 1)]


def _pack_local(tree, params):
    return jnp.concatenate([tree[n][j].reshape(-1, D_MODEL) for n, j, _ in params], axis=0)


def _unpack_local(packed, params, like):
    out, r = {}, 0
    for n, _, _ in params:
        shp = like[n].shape[1:]
        k = shp[0] * shp[1] // D_MODEL
        out[n] = packed[r:r + k].reshape(shp)
        r += k
    return out


def _unpack_gathered(gathered, params, like):
    out, r = {}, 0
    for n, _, ax in params:
        shp = like[n].shape[1:]
        k = shp[0] * shp[1] // D_MODEL
        t = gathered[:, r:r + k].reshape((N_DEV,) + shp)
        out[n] = (jnp.moveaxis(t, 0, 1).reshape(shp[0], N_DEV * shp[1]) if ax == 1
                  else t.reshape(N_DEV * shp[0], shp[1]))
        r += k
    return out


def _pack_full(grads, params, like):
    cols = []
    for n, _, ax in params:
        shp = like[n].shape[1:]
        t = (jnp.moveaxis(grads[n].reshape(shp[0], N_DEV, shp[1]), 1, 0) if ax == 1
             else grads[n].reshape(N_DEV, shp[0], shp[1]))
        cols.append(t.reshape(N_DEV, -1, D_MODEL).astype(BF16))
    return jnp.concatenate(cols, axis=1)


def _row_tile(rows):
    return max(t for t in range(16, 257, 16) if rows % t == 0)


SMALL_ROWS = 24


def _pad_row(a):
    flat = a.reshape(1, -1)
    return jnp.pad(flat, ((0, 0), (0, D_MODEL - flat.shape[1])))


def _tile(n, pref):
    return min(n, pref)


def kernel(x, p, att_w_qkv, att_sink, att_w_o, hgrn_w_in, hgrn_lb_logits, hgrn_norm_g, hgrn_w_o, ln_mix_g, ln_mix_b, ffn_w_in, ffn_w_out, ln_ffn_g, ln_ffn_b, ple_w_gate, ple_w_proj, loss_target, m_att_w_qkv, m_att_sink, m_att_w_o, m_hgrn_w_in, m_hgrn_lb_logits, m_hgrn_norm_g, m_hgrn_w_o, m_ln_mix_g, m_ln_mix_b, m_ffn_w_in, m_ffn_w_out, m_ln_ffn_g, m_ln_ffn_b, m_ple_w_gate, m_ple_w_proj, v_att_w_qkv, v_att_sink, v_att_w_o, v_hgrn_w_in, v_hgrn_lb_logits, v_hgrn_norm_g, v_hgrn_w_o, v_ln_mix_g, v_ln_mix_b, v_ffn_w_in, v_ffn_w_out, v_ln_ffn_g, v_ln_ffn_b, v_ple_w_gate, v_ple_w_proj):
    names = ["att_w_qkv", "att_sink", "att_w_o", "hgrn_w_in", "hgrn_lb_logits", "hgrn_norm_g", "hgrn_w_o", "ln_mix_g",
             "ln_mix_b", "ffn_w_in", "ffn_w_out", "ln_ffn_g", "ln_ffn_b", "ple_w_gate", "ple_w_proj"]
    w = dict(zip(names, (att_w_qkv, att_sink, att_w_o, hgrn_w_in, hgrn_lb_logits, hgrn_norm_g, hgrn_w_o, ln_mix_g,
                         ln_mix_b, ffn_w_in, ffn_w_out, ln_ffn_g, ln_ffn_b, ple_w_gate, ple_w_proj)))
    mom = dict(zip(names, (m_att_w_qkv, m_att_sink, m_att_w_o, m_hgrn_w_in, m_hgrn_lb_logits, m_hgrn_norm_g, m_hgrn_w_o,
                           m_ln_mix_g, m_ln_mix_b, m_ffn_w_in, m_ffn_w_out, m_ln_ffn_g, m_ln_ffn_b, m_ple_w_gate,
                           m_ple_w_proj)))
    var = dict(zip(names, (v_att_w_qkv, v_att_sink, v_att_w_o, v_hgrn_w_in, v_hgrn_lb_logits, v_hgrn_norm_g, v_hgrn_w_o,
                           v_ln_mix_g, v_ln_mix_b, v_ffn_w_in, v_ffn_w_out, v_ln_ffn_g, v_ln_ffn_b, v_ple_w_gate,
                           v_ple_w_proj)))
    s = x.shape[1]
    me = 4 * lax.axis_index("x") + 2 * lax.axis_index("y") + lax.axis_index("c")
    tm = _tile(s, 512)
    tbig = _tile(s, 1024)
    ts = _tile(s // 2, 2048)
    x0 = x.reshape(s, D_MODEL)
    target = loss_target.reshape(s, D_MODEL)
    pl_in = p.reshape(DEPTH, s, PLE_DIM)

    full =_unpack_gathered(_gather(_pack_local(w, _layer_params(0)).astype(BF16), name="gather_weights"),
                            _layer_params(0), w)
    lb_rows = jnp.pad(hgrn_lb_logits.reshape(8, HG_DIM), ((0, 0), (0, D_MODEL - HG_DIM)))
    lb_all = _gather(lb_rows, name="gather_lb")[:, :, :HG_DIM]
    logits_full = jnp.moveaxis(lb_all, 0, 1).reshape(DEPTH, 2 * D_MODEL)
    lb = _lb_fwd(logits_full, name="lb_fwd")
    cos, sin = _rope_tables(s)

    saved = []
    xf, xb = x0, x0
    for i in range(DEPTH):
        j = i // 2
        sv = {"xb": xb, "w": full}
        nxt = nxt2 = None
        if i + 1 < DEPTH:
            ahead = _layer_params(i + 1)
            n_first = len(ahead) if i % 2 == 0 else 3
            nxt = (_pack_local(w, ahead[:n_first]).astype(BF16), True)
            if n_first < len(ahead):
                nxt2 = (_pack_local(w, ahead[n_first:]).astype(BF16), True)
        if i % 2 == 0:
            z = _mm(xb, full["att_w_qkv"], tm=tbig, tn=512, tk=D_MODEL, name="att_in")
            o, *more = _att_fwd(z, w["att_sink"][j], cos, sin, comm=nxt, name="att_fwd")
            w_o = full["att_w_o"]
        else:
            z = _mm(xb, full["hgrn_w_in"], tm=tbig, tn=1024, tk=D_MODEL, name="hgrn_in")
            lbl = lb[j].reshape(2, HG_HEADS, 1, HG_DIM)
            of, st_f, qs_f, dec_f, *more = _hg_fwd(z, lbl[0], rev=False, ts=ts, comm=nxt, name="hgrn_fwd")
            ob, st_b, qs_b, dec_b = _hg_fwd(z, lbl[1], rev=True, ts=ts, name="hgrn_fwd_rev")
            o = _hg_post(of, ob, z, w["hgrn_norm_g"][j].reshape(1, HG_DIM), tm=tm, name="hgrn_post")
            w_o = full["hgrn_w_o"]
            sv.update(of=of, ob=ob, st_f=st_f, st_b=st_b, lbl=lbl, qs_f=qs_f, qs_b=qs_b, dec_f=dec_f, dec_b=dec_b)
        sv.update(z=z, o=o)
        g1, b1 = w["ln_mix_g"][i:i + 1], w["ln_mix_b"][i:i + 1]
        pre1, x1b = _proj_ln(o, w_o, xf, g1, b1, tm=tm, name="mix_out_ln")
        gg, uu, act, *more2 = _ffn_in(x1b, full["ffn_w_in"], tm=tm, tn=FF_TILE, comm=nxt2, name="ffn_in")
        pre2, x2b, xf, xb = _ffn_out_ple(act, full["ffn_w_out"], pre1, g1, b1, w["ln_ffn_g"][i:i + 1],
                                         w["ln_ffn_b"][i:i + 1], pl_in[i], full["ple_w_gate"], full["ple_w_proj"], tm=tm,
                                         name="ffn_out_ple")
        sv.update(pre1=pre1, x1b=x1b, g=gg, u=uu, act=act, pre2=pre2, x2b=x2b)
        saved.append(sv)
        if nxt is not None:
            full = _unpack_gathered(more[0], ahead[:n_first], w)
        if nxt2 is not None:
            full.update(_unpack_gathered(more2[0], ahead[n_first:], w))

    dx, loss_blk = _loss_head(xf, target, tm=tm, name="loss_head")
    loss = lax.psum(loss_blk[0, 0], AXES)

    small = {n: [None] * DEPTH for n in ("ln_mix_g", "ln_mix_b", "ln_ffn_g", "ln_ffn_b")}
    dlb_rows = [None] * 4
    dnorm, dsink = [None] * 2, [None] * 2
    recv_late, recv_early = [None] * DEPTH, [None] * DEPTH
    above = None
    mmw = functools.partial(_mm, ta=True, tk=_tile(s, 2048), out_dtype=BF16)
    for i in reversed(range(DEPTH)):
        j = i // 2
        sv = saved[i]
        full, gl = sv["w"], {}
        da, dpp, dy2, dy2b, small["ln_ffn_g"][i], small["ln_ffn_b"][i] = _ple_ln_bwd(
            dx, sv["x2b"], pl_in[i], full["ple_w_gate"], full["ple_w_proj"], sv["pre2"], w["ln_ffn_g"][i:i + 1],
            tm=tm, name="ple_ln_bwd")
        gl["ple_w_gate"] = mmw(sv["x2b"], da, tm=D_MODEL, tn=D_MODEL, name="dw_ple_gate")
        gl["ple_w_proj"] = mmw(pl_in[i], dpp, tm=PLE_DIM, tn=D_MODEL, name="dw_ple_proj")
        dg, du = _ffn_bwd_act(dy2b, full["ffn_w_out"], sv["g"], sv["u"], tm=_tile(s, SUB_ROWS), tn=D_FF,
                              name="ffn_bwd_act")
        gl["ffn_w_out"] = mmw(sv["act"], dy2b, tm=FF_TILE, tn=D_MODEL, name="dw_ffn_out")
        dy1, dy1b, small["ln_mix_g"][i], small["ln_mix_b"][i] = _dx_from_pieces(
            [dg, du], full["ffn_w_in"], dy2, tm=_tile(s, SUB_ROWS), ln=(sv["pre1"], w["ln_mix_g"][i:i + 1]),
            name="ffn_bwd_x_ln")
        gl["ffn_w_in"] = mmw(sv["x1b"], [dg, du], tm=D_MODEL, tn=FF_TILE, name="dw_ffn_in")
        n_out, n_inw = ("att_w_o", "att_w_qkv") if i % 2 == 0 else ("hgrn_w_o", "hgrn_w_in")
        do = _mm(dy1b, full[n_out], tm=tbig, tn=D_MODEL, tk=D_MODEL, tb=True, out_dtype=BF16, name="mix_out_bwd")
        gl[n_out] = mmw(sv["o"], dy1b, tm=D_MODEL, tn=D_MODEL, name="dw_mix_out")
        early = _pack_full(gl, _layer_params(i)[1:], w)
        comm = (early if above is None else jnp.concatenate([above, early], axis=1), False)
        if i % 2 == 0:
            dzq, part, dsk, *more = _att_bwd(sv["z"], do, w["att_sink"][j], cos, sin, comm=comm, name="att_bwd")
            dz = [dzq, _att_bwd_kv(part, cos, sin, name="att_bwd_kv")]
            dsink[j] = dsk[:, 0]
        else:
            dsum, dgate, dnorm[j] = _hg_post_bwd(do, sv["of"], sv["ob"], sv["z"], w["hgrn_norm_g"][j].reshape(1, HG_DIM),
                                                 tm=tm, name="hgrn_post_bwd")
            dq1, df1, dv1, dlb1, *more = _hg_bwd(sv["z"], sv["lbl"][0], sv["st_f"], sv["qs_f"], sv["dec_f"], dsum, None,
                                                 None, rev=False, ts=ts, comm=comm, name="hgrn_bwd")
            dq2, df2, dv2, dlb2 = _hg_bwd(sv["z"], sv["lbl"][1], sv["st_b"], sv["qs_b"], sv["dec_b"], dsum, dq1, dv1,
                                          rev=True, ts=ts, name="hgrn_bwd_rev")
            dz = [dq2, df1, df2, dv2, dgate]
            dlb_rows[2 * j] = dlb1.reshape(1, D_MODEL)
            dlb_rows[2 * j + 1] = dlb2.reshape(1, D_MODEL)
        if above is not None:
            recv_late[i + 1] = (more[0], 0)
        recv_early[i] = (more[0], 0 if above is None else above.shape[1])
        dx = _dx_from_pieces(dz, full[n_inw], dy1, tm=tm, name="mix_in_bwd")
        gl[n_inw] = mmw(sv["xb"], dz, tm=D_MODEL, tn=512, name="dw_mix_in")
        above = _pack_full(gl, _layer_params(i)[:1], w)
    grad_x = dx.reshape(x.shape)
    recv_late[0] = (_exchange(above, name="exchange_grads"), 0)

    big_out = [{n: [None] * w[n].shape[0] for n, _ in BIG} for _ in range(4)]
    for i in range(DEPTH):
        for params, (got, off) in ((_layer_params(i)[:1], recv_late[i]), (_layer_params(i)[1:], recv_early[i])):
            w_part = _pack_local(w, params)
            outs = _adamw(w_part, got, _pack_local(mom, params), _pack_local(var, params),
                          tr=_row_tile(math.gcd(w_part.shape[0], off)), g_off=off, name="adamw_big")
            for kind, packed in enumerate(outs):
                for (n, j, _), piece in zip(params, _unpack_local(packed, params, w).values()):
                    big_out[kind][n][j] = piece
    big_out = [{n: jnp.stack(v) for n, v in kind.items()} for kind in big_out]

    small_rows = jnp.concatenate(
        [jnp.concatenate(small[n], axis=0) for n in ("ln_mix_g", "ln_mix_b", "ln_ffn_g", "ln_ffn_b")] + dlb_rows
        + [_pad_row(jnp.stack(dnorm)), _pad_row(jnp.stack(dsink)), jnp.zeros((2, D_MODEL), F32)], axis=0)
    small_all = _gather(small_rows, name="gather_small")
    lbw, lbm, lbv = (t.reshape(4, 2 * HG_DIM) for t in (hgrn_lb_logits, mom["hgrn_lb_logits"], var["hgrn_lb_logits"]))
    summed = _sum8(small_all, name="sum_small")
    dlb_mine = lax.dynamic_slice_in_dim(summed[16:20].reshape(2, 2, HG_HEADS, HG_DIM), me, 1, axis=2)
    dlogits = _lb_bwd(lbw, dlb_mine.reshape(2, 2 * HG_DIM), name="lb_bwd")

    def small_pack(ln4, lbt, ng, sk):
        return jnp.concatenate([ln4[n] for n in ("ln_mix_g", "ln_mix_b", "ln_ffn_g", "ln_ffn_b")]
                               + [_pad_row(lbt), _pad_row(ng), _pad_row(sk), jnp.zeros((5, D_MODEL), F32)], axis=0)

    g_small = jnp.concatenate([summed[:16], _pad_row(dlogits), summed[20:22], jnp.zeros((5, D_MODEL), F32)], axis=0)
    souts = _adamw(small_pack(w, lbw, w["hgrn_norm_g"], w["att_sink"]), g_small,
                   small_pack(mom, lbm, mom["hgrn_norm_g"], mom["att_sink"]),
                   small_pack(var, lbv, var["hgrn_norm_g"], var["att_sink"]), tr=SMALL_ROWS, name="adamw_small")

    def small_unpack(t):
        out = {n: t[4 * k:4 * k + 4] for k, n in enumerate(("ln_mix_g", "ln_mix_b", "ln_ffn_g", "ln_ffn_b"))}
        out["hgrn_lb_logits"] = t[16].reshape(hgrn_lb_logits.shape)
        out["hgrn_norm_g"] = t[17, :2 * HG_DIM].reshape(hgrn_norm_g.shape)
        out["att_sink"] = t[18, :2 * N_Q_HEADS].reshape(att_sink.shape)
        return out

    result = [loss, grad_x]
    for big_t, small_t in zip(big_out, souts):
        merged = dict(big_t)
        merged.update(small_unpack(small_t))
        result += [merged[n] for n in names]
    return tuple(result)
```

```python
import functools
import math

import jax
import jax.numpy as jnp
from jax import lax
from jax.experimental import pallas as pl
from jax.experimental.pallas import tpu as pltpu

F32 = jnp.float32
BF16 = jnp.bfloat16

D_MODEL = 1024
DEPTH = 4
HEAD_DIM = 64
N_Q_HEADS = 16
N_KV_HEADS = 4
GROUP = 4
KV_DIM = 256
ATT_BLOCK = 128
ROPE_DIM = 16
ROPE_THETA = 500000.0
HG_HEADS = 8
HG_DIM = 128
HG_CHUNK = 64
HG_SUB = 16
D_FF = 2816
FF_TILE = 1408
SUB_ROWS = 256
PLE_DIM = 256
ALPHA = (2 * DEPTH) ** 0.25
LN_EPS = 1e-5
ADAM_LR, ADAM_B1, ADAM_B2, ADAM_EPS, ADAM_WD, ADAM_STEP = 0.001, 0.9, 0.999, 1e-08, 0.01, 10

N_DEV = 8
LANES = 128
VMEM_LIMIT = 52 * 1024 * 1024
NEG = -1e30
MESH = pl.DeviceIdType.MESH
AXES = ("x", "y", "c")

BIG = (("att_w_qkv", 2), ("att_w_o", 1), ("hgrn_w_in", 2), ("hgrn_w_o", 1), ("ffn_w_in", 2), ("ffn_w_out", 1),
       ("ple_w_gate", 1), ("ple_w_proj", 2))


def _params(sem=None, vmem=VMEM_LIMIT):
    return pltpu.CompilerParams(dimension_semantics=sem, vmem_limit_bytes=vmem)


def _sigmoid(x):
    return jax.nn.sigmoid(x)


def _direct_copies(src_ref, out_ref, send_sems, recv_sems, local_sem, gather, arrivals):
    x, y, c = lax.axis_index("x"), lax.axis_index("y"), lax.axis_index("c")
    me = 4 * x + 2 * y + c
    mine = (lambda j: src_ref) if gather else (lambda j: src_ref.at[j])
    pairs = []
    for k in range(1, N_DEV):
        px, py, pc = x ^ (k >> 2), y ^ ((k >> 1) & 1), c ^ (k & 1)
        peer = 4 * px + 2 * py + pc
        send = pltpu.make_async_remote_copy(
            src_ref=mine(peer), dst_ref=out_ref.at[me], send_sem=send_sems.at[k], recv_sem=recv_sems.at[k],
            device_id=(px, py, pc), device_id_type=MESH)
        arrival = pltpu.make_async_remote_copy(
            src_ref=mine(peer), dst_ref=out_ref.at[peer], send_sem=send_sems.at[k], recv_sem=recv_sems.at[k],
            device_id=(x, y, c), device_id_type=MESH) if arrivals else None
        pairs.append((send, arrival))
    return pltpu.make_async_copy(mine(me), out_ref.at[me], local_sem), pairs


def _direct_start(*refs, gather):
    local, pairs = _direct_copies(*refs, gather, False)
    local.start()
    for send, _ in pairs:
        send.start()


def _direct_wait(*refs, gather):
    local, pairs = _direct_copies(*refs, gather, True)
    for send, arrival in pairs:
        send.wait_send()
        arrival.wait_recv()
    local.wait()


COMM_SCRATCH = [pltpu.SemaphoreType.DMA((N_DEV,)), pltpu.SemaphoreType.DMA((N_DEV,)), pltpu.SemaphoreType.DMA]


def _exchange(src, *, gather=False, name):
    def body(*refs):
        _direct_start(*refs, gather=gather)
        _direct_wait(*refs, gather=gather)

    blk = tuple(src.shape) if gather else tuple(src.shape[1:])
    return pl.pallas_call(
        body, name=name,
        out_shape=jax.ShapeDtypeStruct((N_DEV,) + blk, src.dtype),
        in_specs=[pl.BlockSpec(memory_space=pltpu.HBM)],
        out_specs=pl.BlockSpec(memory_space=pltpu.HBM),
        scratch_shapes=COMM_SCRATCH,
    )(src)


def _call(body, *, name, grid, out_shape, in_specs, out_specs, args, scratch_shapes=(), sem, comm=None, edge=None):
    out_shape, out_specs = tuple(out_shape), tuple(out_specs)
    if comm is None:
        return pl.pallas_call(body, name=name, grid=grid, out_shape=out_shape, in_specs=list(in_specs),
                              out_specs=out_specs, scratch_shapes=list(scratch_shapes),
                              compiler_params=_params(sem))(*args)
    src, gather = comm
    n_in, n_out, n_scr = len(args), len(out_shape), len(scratch_shapes)
    blk = tuple(src.shape) if gather else tuple(src.shape[1:])
    hbm = pl.BlockSpec(memory_space=pltpu.HBM)

    def carrying(*refs):
        ins, src_ref = refs[:n_in], refs[n_in]
        outs, dst_ref = refs[n_in + 1:n_in + 1 + n_out], refs[n_in + 1 + n_out]
        own = refs[n_in + 2 + n_out:n_in + 2 + n_out + n_scr]
        comm_refs = (src_ref, dst_ref) + tuple(refs[n_in + 2 + n_out + n_scr:])
        first, last = edge()

        @pl.when(first)
        def _():
            _direct_start(*comm_refs, gather=gather)

        body(*ins, *outs, *own)

        @pl.when(last)
        def _():
            _direct_wait(*comm_refs, gather=gather)

    return pl.pallas_call(
        carrying, name=name, grid=grid,
        out_shape=out_shape + (jax.ShapeDtypeStruct((N_DEV,) + blk, src.dtype),),
        in_specs=list(in_specs) + [hbm], out_specs=out_specs + (hbm,),
        scratch_shapes=list(scratch_shapes) + COMM_SCRATCH,
        compiler_params=_params(("arbitrary",) * len(grid)),
    )(*args, src)


def _gather(src, *, name):
    def body(src_ref, out_ref, send_sems, recv_sems, local_sem):
        x, y, c = lax.axis_index("x"), lax.axis_index("y"), lax.axis_index("c")
        sibling = (x, y, 1 - c)
        chips = [(1 - x, y), (x, 1 - y), (1 - x, 1 - y)]

        def rows(px, py, pc):
            return out_ref.at[4 * px + 2 * py + pc]

        def copy(k, block, to, from_src=False):
            return pltpu.make_async_remote_copy(
                src_ref=src_ref if from_src else rows(*block), dst_ref=rows(*block), send_sem=send_sems.at[k],
                recv_sem=recv_sems.at[k], device_id=to, device_id_type=MESH)

        me = (x, y, c)
        mine = pltpu.make_async_copy(src_ref, rows(*me), local_sem)
        mine.start()
        first = [copy(0, me, sibling, from_src=True)]
        first += [copy(1 + j, me, (*chip, c), from_src=True) for j, chip in enumerate(chips)]
        for cp in first:
            cp.start()
        passed = [copy(4 + j, (*chip, c), sibling) for j, chip in enumerate(chips)]
        for j, chip in enumerate(chips):
            copy(1 + j, (*chip, c), me).wait_recv()
            passed[j].start()
        copy(0, sibling, me).wait_recv()
        for j, chip in enumerate(chips):
            copy(4 + j, (*chip, 1 - c), me).wait_recv()
        for cp in first + passed:
            cp.wait_send()
        mine.wait()

    return pl.pallas_call(
        body, name=name,
        out_shape=jax.ShapeDtypeStruct((N_DEV,) + tuple(src.shape), src.dtype),
        in_specs=[pl.BlockSpec(memory_space=pltpu.HBM)],
        out_specs=pl.BlockSpec(memory_space=pltpu.HBM),
        scratch_shapes=[pltpu.SemaphoreType.DMA((7,)), pltpu.SemaphoreType.DMA((7,)), pltpu.SemaphoreType.DMA],
    )(src)


def _mm(a, b, *, tm, tn, tk, ta=False, tb=False, out_dtype=F32, name):
    b_list = list(b) if isinstance(b, (list, tuple)) else [b]
    assert not (tb and len(b_list) > 1)
    m, kdim = (a.shape[1], a.shape[0]) if ta else a.shape
    joff = [0]
    for piece in b_list:
        joff.append(joff[-1] + (piece.shape[0] if tb else piece.shape[1]) // tn)
    nk, n = kdim // tk, joff[-1] * tn
    dims = (((0 if ta else 1,), (1 if tb else 0,)), ((), ()))

    def mine(j, p):
        return (j >= joff[p]) & (j < joff[p + 1])

    def body(*refs):
        a_ref, b_refs, o_ref = refs[0], refs[1:1 + len(b_list)], refs[1 + len(b_list)]
        acc_ref = refs[-1] if nk > 1 else None
        j, k = pl.program_id(1), pl.program_id(2)
        for p, b_ref in enumerate(b_refs):
            def step(b_ref=b_ref):
                part = lax.dot_general(a_ref[...].astype(BF16), b_ref[...].astype(BF16), dims,
                                       preferred_element_type=F32)
                if nk == 1:
                    o_ref[...] = part.astype(out_dtype)
                else:
                    _accumulate(acc_ref, part, k == 0)

            if len(b_list) == 1:
                step()
            else:
                pl.when(mine(j, p))(step)
        if nk > 1:
            @pl.when(k == nk - 1)
            def _():
                o_ref[...] = acc_ref[...].astype(out_dtype)

    def b_spec(p):
        jj = lambda j: jnp.clip(j - joff[p], 0, joff[p + 1] - joff[p] - 1)
        kk = (lambda j, k: k) if len(b_list) == 1 else (lambda j, k: jnp.where(mine(j, p), k, 0))
        return (pl.BlockSpec((tn, tk), lambda i, j, k: (jj(j), kk(j, k))) if tb
                else pl.BlockSpec((tk, tn), lambda i, j, k: (kk(j, k), jj(j))))

    a_spec = pl.BlockSpec((tk, tm), lambda i, j, k: (k, i)) if ta else pl.BlockSpec((tm, tk), lambda i, j, k: (i, k))
    return pl.pallas_call(
        body, name=name, grid=(m // tm, n // tn, nk),
        out_shape=jax.ShapeDtypeStruct((m, n), out_dtype),
        in_specs=[a_spec] + [b_spec(p) for p in range(len(b_list))],
        out_specs=pl.BlockSpec((tm, tn), lambda i, j, k: (i, j)),
        scratch_shapes=[pltpu.VMEM((tm, tn), F32)] if nk > 1 else [],
        compiler_params=_params(("parallel", "parallel", "arbitrary")),
    )(a, *b_list)


def _dx_from_pieces(pieces, w, add, *, tm, ln=None, name):
    s = pieces[0].shape[0]
    widths = [p.shape[1] for p in pieces]

    def body(*refs):
        p_refs, (w_ref, add_ref) = refs[:len(pieces)], refs[len(pieces):len(pieces) + 2]
        rest = refs[len(pieces) + 2:]
        pg = jnp.zeros((1, D_MODEL), F32)
        pb = jnp.zeros((1, D_MODEL), F32)
        for rs in _row_parts(tm):
            r = ALPHA * add_ref[rs, :]
            off = 0
            for p_ref, width in zip(p_refs, widths):
                r = r + lax.dot_general(p_ref[rs, :], w_ref[:, off:off + width], (((1,), (1,)), ((), ())),
                                        preferred_element_type=F32)
                off += width
            if ln is None:
                rest[0][rs, :] = r
            else:
                dy, qg, qb = _ln_bwd_rows(r, rest[0][rs, :], rest[1][...])
                rest[2][rs, :] = dy
                rest[3][rs, :] = dy.astype(BF16)
                pg, pb = pg + qg, pb + qb
        if ln is not None:
            _accumulate(rest[4], pg, pl.program_id(0) == 0)
            _accumulate(rest[5], pb, pl.program_id(0) == 0)

    row = lambda i: (i, 0)
    tile = pl.BlockSpec((tm, D_MODEL), row)
    vec = pl.BlockSpec((1, D_MODEL), lambda i: (0, 0))
    in_specs = ([pl.BlockSpec((tm, width), row) for width in widths]
                + [pl.BlockSpec((D_MODEL, sum(widths)), lambda i: (0, 0)), tile])
    args = list(pieces) + [w, add]
    out_shape, out_specs = jax.ShapeDtypeStruct((s, D_MODEL), F32), tile
    if ln is not None:
        in_specs += [tile, vec]
        args += list(ln)
        out_shape = (jax.ShapeDtypeStruct((s, D_MODEL), F32), jax.ShapeDtypeStruct((s, D_MODEL), BF16),
                     jax.ShapeDtypeStruct((1, D_MODEL), F32), jax.ShapeDtypeStruct((1, D_MODEL), F32))
        out_specs = (tile, tile, vec, vec)
    return pl.pallas_call(
        body, name=name, grid=(s // tm,), out_shape=out_shape, in_specs=in_specs, out_specs=out_specs,
        compiler_params=_params(("arbitrary",) if ln is not None else ("parallel",)),
    )(*args)


def _ln_bwd_rows(do, y, g):
    mu = jnp.mean(y, axis=-1, keepdims=True)
    yc = y - mu
    var = jnp.mean(yc * yc, axis=-1, keepdims=True)
    rstd = lax.rsqrt(var + LN_EPS)
    xhat = yc * rstd
    dxhat = do * g
    dy = rstd * (dxhat - jnp.mean(dxhat, axis=-1, keepdims=True) - xhat * jnp.mean(dxhat * xhat, axis=-1, keepdims=True))
    return dy, jnp.sum(do * xhat, axis=0, keepdims=True), jnp.sum(do, axis=0, keepdims=True)


def _accumulate(ref, val, first):
    @pl.when(first)
    def _():
        ref[...] = val

    @pl.when(jnp.logical_not(first))
    def _():
        ref[...] += val


def _layer_norm_rows(y, g, b):
    mu = jnp.mean(y, axis=-1, keepdims=True)
    yc = y - mu
    var = jnp.mean(yc * yc, axis=-1, keepdims=True)
    return yc * lax.rsqrt(var + LN_EPS) * g + b


def _proj_ln(a, w, res, g, b, *, tm, name):
    s, kdim = a.shape

    def body(a_ref, w_ref, res_ref, g_ref, b_ref, pre_ref, obf_ref):
        for rs in _row_parts(tm):
            h = jnp.dot(a_ref[rs, :], w_ref[...], preferred_element_type=F32)
            pre = ALPHA * res_ref[rs, :] + h
            pre_ref[rs, :] = pre
            obf_ref[rs, :] = _layer_norm_rows(pre, g_ref[...], b_ref[...]).astype(BF16)

    row = lambda i: (i, 0)
    fix = lambda i: (0, 0)
    return pl.pallas_call(
        body, name=name, grid=(s // tm,),
        out_shape=(jax.ShapeDtypeStruct((s, D_MODEL), F32), jax.ShapeDtypeStruct((s, D_MODEL), BF16)),
        in_specs=[pl.BlockSpec((tm, kdim), row), pl.BlockSpec((kdim, D_MODEL), fix), pl.BlockSpec((tm, D_MODEL), row),
                  pl.BlockSpec((1, D_MODEL), fix), pl.BlockSpec((1, D_MODEL), fix)],
        out_specs=(pl.BlockSpec((tm, D_MODEL), row),) * 2,
        compiler_params=_params(("parallel",)),
    )(a, w, res, g, b)


def _row_parts(tm):
    sub = min(tm, SUB_ROWS)
    return [pl.ds(r * sub, sub) for r in range(tm // sub)]


def _ffn_in(xbf, w, *, tm, tn, comm=None, name):
    s = xbf.shape[0]
    nj = D_FF // tn
    ni = s // tm

    def body(x_ref, wg_ref, wu_ref, g_ref, u_ref, act_ref):
        for rs in _row_parts(tm):
            xv = x_ref[rs, :]
            gg = jnp.dot(xv, wg_ref[...], preferred_element_type=F32)
            uu = jnp.dot(xv, wu_ref[...], preferred_element_type=F32)
            g_ref[rs, :] = gg.astype(BF16)
            u_ref[rs, :] = uu.astype(BF16)
            act_ref[rs, :] = (gg * _sigmoid(gg) * uu).astype(BF16)

    out = jax.ShapeDtypeStruct((s, D_FF), BF16)
    tile = pl.BlockSpec((tm, tn), lambda j, i: (i, j))
    return _call(
        body, name=name, grid=(nj, ni),
        out_shape=(out, out, out),
        in_specs=[pl.BlockSpec((tm, D_MODEL), lambda j, i: (i, 0)), pl.BlockSpec((D_MODEL, tn), lambda j, i: (0, j)),
                  pl.BlockSpec((D_MODEL, tn), lambda j, i: (0, j + nj))],
        out_specs=(tile, tile, tile), args=(xbf, w, w), sem=("parallel", "parallel"), comm=comm,
        edge=lambda: ((pl.program_id(0) == 0) & (pl.program_id(1) == 0),
                      (pl.program_id(0) == nj - 1) & (pl.program_id(1) == ni - 1)))


def _ffn_bwd_act(dybf, w_out, g, u, *, tm, tn, name):
    s = dybf.shape[0]

    def body(dy_ref, w_ref, g_ref, u_ref, dg_ref, du_ref):
        for rs in _row_parts(tm):
            dact = lax.dot_general(dy_ref[rs, :], w_ref[...], (((1,), (1,)), ((), ())), preferred_element_type=F32)
            gg = g_ref[rs, :].astype(F32)
            uu = u_ref[rs, :].astype(F32)
            sg = _sigmoid(gg)
            dg_ref[rs, :] = (dact * uu * sg * (1.0 + gg * (1.0 - sg))).astype(BF16)
            du_ref[rs, :] = (dact * gg * sg).astype(BF16)

    out = jax.ShapeDtypeStruct((s, D_FF), BF16)
    tile = pl.BlockSpec((tm, tn), lambda j, i: (i, j))
    return pl.pallas_call(
        body, name=name, grid=(D_FF // tn, s // tm),
        out_shape=(out, out),
        in_specs=[pl.BlockSpec((tm, D_MODEL), lambda j, i: (i, 0)), pl.BlockSpec((tn, D_MODEL), lambda j, i: (j, 0)),
                  tile, tile],
        out_specs=(tile, tile),
        compiler_params=_params(("parallel", "parallel")),
    )(dybf, w_out, g, u)


def _ffn_out_ple(act, w_out, res_pre, res_g, res_b, g, b, p, w_gate, w_proj, *, tm, name):
    s = act.shape[0]

    def body(a_ref, w_ref, res_ref, rg_ref, rb_ref, g_ref, b_ref, p_ref, wg_ref, wp_ref, pre_ref, x2bf_ref, o_ref,
             obf_ref):
        for rs in _row_parts(tm):
            res = _layer_norm_rows(res_ref[rs, :], rg_ref[...], rb_ref[...])
            pre = ALPHA * res + jnp.dot(a_ref[rs, :], w_ref[...], preferred_element_type=F32)
            x2 = _layer_norm_rows(pre, g_ref[...], b_ref[...])
            x2bf = x2.astype(BF16)
            pre_ref[rs, :] = pre
            x2bf_ref[rs, :] = x2bf
            gate = jnp.dot(x2bf, wg_ref[...], preferred_element_type=F32)
            pp = jnp.dot(p_ref[rs, :].astype(BF16), wp_ref[...], preferred_element_type=F32)
            out = x2 + _sigmoid(gate) * pp
            o_ref[rs, :] = out
            obf_ref[rs, :] = out.astype(BF16)

    row = lambda i: (i, 0)
    fix = lambda i: (0, 0)
    tile = pl.BlockSpec((tm, D_MODEL), row)
    vec = pl.BlockSpec((1, D_MODEL), fix)
    act_t = lambda dt: jax.ShapeDtypeStruct((s, D_MODEL), dt)
    return pl.pallas_call(
        body, name=name, grid=(s // tm,),
        out_shape=(act_t(F32), act_t(BF16), act_t(F32), act_t(BF16)),
        in_specs=[pl.BlockSpec((tm, D_FF), row), pl.BlockSpec((D_FF, D_MODEL), fix), tile, vec, vec, vec, vec,
                  pl.BlockSpec((tm, PLE_DIM), row), pl.BlockSpec((D_MODEL, D_MODEL), fix),
                  pl.BlockSpec((PLE_DIM, D_MODEL), fix)],
        out_specs=(tile, tile, tile, tile),
        compiler_params=_params(("parallel",)),
    )(act, w_out, res_pre, res_g, res_b, g, b, p, w_gate, w_proj)


def _ple_ln_bwd(dx3, x2bf, p, w_gate, w_proj, pre, g, *, tm, name):
    s = dx3.shape[0]

    def body(d_ref, xbf_ref, p_ref, wg_ref, wp_ref, pre_ref, g_ref, da_ref, dpp_ref, dy_ref, dybf_ref, dg_ref, db_ref):
        pg = jnp.zeros((1, D_MODEL), F32)
        pb = jnp.zeros((1, D_MODEL), F32)
        for rs in _row_parts(tm):
            d = d_ref[rs, :]
            a = jnp.dot(xbf_ref[rs, :], wg_ref[...], preferred_element_type=F32)
            pp = jnp.dot(p_ref[rs, :].astype(BF16), wp_ref[...], preferred_element_type=F32)
            sg = _sigmoid(a)
            da = (d * pp * sg * (1.0 - sg)).astype(BF16)
            da_ref[rs, :] = da
            dpp_ref[rs, :] = (d * sg).astype(BF16)
            dx2 = d + lax.dot_general(da, wg_ref[...], (((1,), (1,)), ((), ())), preferred_element_type=F32)
            dy, qg, qb = _ln_bwd_rows(dx2, pre_ref[rs, :], g_ref[...])
            dy_ref[rs, :] = dy
            dybf_ref[rs, :] = dy.astype(BF16)
            pg, pb = pg + qg, pb + qb
        _accumulate(dg_ref, pg, pl.program_id(0) == 0)
        _accumulate(db_ref, pb, pl.program_id(0) == 0)

    row = lambda i: (i, 0)
    fix = lambda i: (0, 0)
    tile = pl.BlockSpec((tm, D_MODEL), row)
    vec = pl.BlockSpec((1, D_MODEL), fix)
    act = lambda dt: jax.ShapeDtypeStruct((s, D_MODEL), dt)
    return pl.pallas_call(
        body, name=name, grid=(s // tm,),
        out_shape=(act(BF16), act(BF16), act(F32), act(BF16), jax.ShapeDtypeStruct((1, D_MODEL), F32),
                   jax.ShapeDtypeStruct((1, D_MODEL), F32)),
        in_specs=[tile, tile, pl.BlockSpec((tm, PLE_DIM), row), pl.BlockSpec((D_MODEL, D_MODEL), fix),
                  pl.BlockSpec((PLE_DIM, D_MODEL), fix), tile, vec],
        out_specs=(tile, tile, tile, tile, vec, vec),
        compiler_params=_params(("arbitrary",)),
    )(dx3, x2bf, p, w_gate, w_proj, pre, g)


def _loss_head(y, target, *, tm, name):
    s = y.shape[0]

    def body(y_ref, t_ref, dy_ref, loss_ref, acc_ref):
        err = y_ref[...] - t_ref[...]
        dy_ref[...] = err * (1.0 / D_MODEL)
        part = jnp.sum(err * err, axis=0, keepdims=True)

        @pl.when(pl.program_id(0) == 0)
        def _():
            acc_ref[...] = part

        @pl.when(pl.program_id(0) > 0)
        def _():
            acc_ref[...] += part

        @pl.when(pl.program_id(0) == pl.num_programs(0) - 1)
        def _():
            tot = jnp.sum(acc_ref[...], axis=1, keepdims=True) * (0.5 / D_MODEL)
            loss_ref[...] = jnp.broadcast_to(tot, (8, LANES))

    row = lambda i: (i, 0)
    return pl.pallas_call(
        body, name=name, grid=(s // tm,),
        out_shape=(jax.ShapeDtypeStruct((s, D_MODEL), F32), jax.ShapeDtypeStruct((8, LANES), F32)),
        in_specs=[pl.BlockSpec((tm, D_MODEL), row), pl.BlockSpec((tm, D_MODEL), row)],
        out_specs=(pl.BlockSpec((tm, D_MODEL), row), pl.BlockSpec((8, LANES), lambda i: (0, 0))),
        scratch_shapes=[pltpu.VMEM((1, D_MODEL), F32)],
        compiler_params=_params(("arbitrary",)),
    )(y, target)


def _rope_tables(s):
    inv = ROPE_THETA ** (-jnp.arange(0, ROPE_DIM, 2, dtype=F32) / ROPE_DIM)
    ang = jnp.arange(s, dtype=F32)[:, None] * inv[None, :]
    cos, sin = jnp.cos(ang), jnp.sin(ang)
    ones = jnp.ones((s, HEAD_DIM - ROPE_DIM), F32)
    c_head = jnp.concatenate([cos, cos, ones], axis=1)
    s_head = jnp.concatenate([-sin, sin, 0.0 * ones], axis=1)
    return jnp.concatenate([c_head, c_head], axis=1), jnp.concatenate([s_head, s_head], axis=1)


def _rope(v, cos, sin):
    n = v.shape[1] // LANES
    width = v.shape[1]
    cos_w = jnp.tile(cos, (1, n)) if n > 1 else cos
    sin_w = jnp.tile(sin, (1, n)) if n > 1 else sin
    dim = lax.broadcasted_iota(jnp.int32, (1, width), 1) % HEAD_DIM
    partner = jnp.where(dim < ROPE_DIM // 2, pltpu.roll(v, width - ROPE_DIM // 2, 1), pltpu.roll(v, ROPE_DIM // 2, 1))
    return v * cos_w + partner * sin_w


def _unrope(dv, cos, sin):
    n = dv.shape[1] // LANES
    width = dv.shape[1]
    cos_w = jnp.tile(cos, (1, n)) if n > 1 else cos
    sin_w = jnp.tile(sin, (1, n)) if n > 1 else sin
    t = dv * sin_w
    dim = lax.broadcasted_iota(jnp.int32, (1, width), 1) % HEAD_DIM
    partner = jnp.where(dim < ROPE_DIM // 2, pltpu.roll(t, width - ROPE_DIM // 2, 1),
                        jnp.where(dim < ROPE_DIM, pltpu.roll(t, ROPE_DIM // 2, 1), 0.0))
    return dv * cos_w + partner


def _att_mask(i, nb):
    rows = GROUP * ATT_BLOCK
    r = lax.broadcasted_iota(jnp.int32, (rows, 3 * ATT_BLOCK), 0) % ATT_BLOCK
    cidx = lax.broadcasted_iota(jnp.int32, (rows, 3 * ATT_BLOCK), 1)
    rel = r + ATT_BLOCK - cidx
    ok = (rel <= ATT_BLOCK) & (rel >= -ATT_BLOCK)
    ok = ok & ((cidx >= ATT_BLOCK) | (i > 0)) & ((cidx < 2 * ATT_BLOCK) | (i < nb - 1))
    return ok


def _att_mask_t(i, nb):
    cols = GROUP * ATT_BLOCK
    cidx = lax.broadcasted_iota(jnp.int32, (3 * ATT_BLOCK, cols), 0)
    r = lax.broadcasted_iota(jnp.int32, (3 * ATT_BLOCK, cols), 1) % ATT_BLOCK
    rel = r + ATT_BLOCK - cidx
    ok = (rel <= ATT_BLOCK) & (rel >= -ATT_BLOCK)
    return ok & ((cidx >= ATT_BLOCK) | (i > 0)) & ((cidx < 2 * ATT_BLOCK) | (i < nb - 1))


def _sink_lanes(sink_ref, h):
    cols = GROUP * ATT_BLOCK
    grp = lax.broadcasted_iota(jnp.int32, (1, cols), 1) // ATT_BLOCK
    out = jnp.zeros((1, cols), F32)
    for gq in range(GROUP):
        out = jnp.where(grp == gq, sink_ref[GROUP * h + gq], out)
    return out


def _half_mask(half):
    lane = lax.broadcasted_iota(jnp.int32, (1, LANES), 1)
    return (lane // HEAD_DIM) == half


def _stack_q(q, h):
    parts = []
    for gq in range(GROUP):
        n = GROUP * h + gq
        grp = q[:, LANES * (n // 2):LANES * (n // 2 + 1)]
        grp = jnp.where(_half_mask(n % 2), grp, 0.0)
        if n % 2 != h % 2:
            grp = pltpu.roll(grp, HEAD_DIM, 1)
        parts.append(grp)
    return jnp.concatenate(parts, axis=0)


def _unstack_q(stacked, h, acc):
    for gq in range(GROUP):
        n = GROUP * h + gq
        grp = stacked[ATT_BLOCK * gq:ATT_BLOCK * (gq + 1), :]
        grp = jnp.where(_half_mask(h % 2), grp, 0.0)
        if n % 2 != h % 2:
            grp = pltpu.roll(grp, HEAD_DIM, 1)
        acc[n // 2] = grp if acc[n // 2] is None else acc[n // 2] + grp
    return acc


def _sink_rows(sink_ref, h):
    rows = GROUP * ATT_BLOCK
    grp = lax.broadcasted_iota(jnp.int32, (rows, 1), 0) // ATT_BLOCK
    out = jnp.zeros((rows, 1), F32)
    for gq in range(GROUP):
        out = jnp.where(grp == gq, sink_ref[GROUP * h + gq], out)
    return out


def _att_probs(qs, kh, sink, valid):
    s = lax.dot_general(qs, kh, (((1,), (1,)), ((), ())), preferred_element_type=F32)
    s = jnp.where(valid, s, NEG)
    m = jnp.maximum(jnp.max(s, axis=-1, keepdims=True), sink)
    p = jnp.exp(s - m)
    es = jnp.exp(sink - m)
    den = jnp.sum(p, axis=-1, keepdims=True) + es
    inv = 1.0 / den
    return p * inv, es * inv


ATT_STEP_FWD = 2
ATT_STEP_BWD = 4


def _att_specs(nb, step):
    rows = step * ATT_BLOCK
    prev = lambda i: (jnp.maximum(step * i - 1, 0), 0)
    cur = lambda i: (i, 0)
    nxt = lambda i: (jnp.minimum(step * (i + 1), nb - 1), 0)
    kv = lambda f: (lambda i: (f(i)[0], 2))
    shapes = ((rows, cur), (ATT_BLOCK, prev), (rows, cur), (ATT_BLOCK, nxt))
    tab = [pl.BlockSpec((r, LANES), f) for r, f in shapes]
    z = [pl.BlockSpec((rows, D_MODEL), cur)] + [pl.BlockSpec((r, 2 * KV_DIM), kv(f)) for r, f in shapes[1:]]
    return z, tab


def _att_load(zq_ref, kp_ref, kc_ref, kn_ref, cq_ref, sq_ref, cp_ref, sp_ref, cc_ref, sc_ref, cn_ref, sn_ref):
    q = (_rope(zq_ref[...], cq_ref[...], sq_ref[...]) * (HEAD_DIM ** -0.5))
    ks, vs = [], []
    for ref, c_ref, s_ref in ((kp_ref, cp_ref, sp_ref), (kc_ref, cc_ref, sc_ref), (kn_ref, cn_ref, sn_ref)):
        kvb = ref[...]
        ks.append(_rope(kvb[:, :KV_DIM], c_ref[...], s_ref[...]))
        vs.append(kvb[:, KV_DIM:])
    return q, jnp.concatenate(ks, axis=0).astype(BF16), jnp.concatenate(vs, axis=0).astype(BF16)


def _att_rows(sub):
    return (slice(ATT_BLOCK * sub, ATT_BLOCK * (sub + 1)), slice(ATT_BLOCK * sub, ATT_BLOCK * (sub + 3)))


def _att_fwd(z, sink, cos, sin, *, comm=None, name):
    s = z.shape[0]
    nb = s // ATT_BLOCK
    group = min(ATT_STEP_FWD, nb)
    steps = nb // group

    def body(zq_ref, kp_ref, kc_ref, kn_ref, cq_ref, cp_ref, cc_ref, cn_ref, sq_ref, sp_ref, sc_ref, sn_ref, sink_ref,
             o_ref):
        i = pl.program_id(0)
        q, k, v = _att_load(zq_ref, kp_ref, kc_ref, kn_ref, cq_ref, sq_ref, cp_ref, sp_ref, cc_ref, sc_ref, cn_ref, sn_ref)
        for sub in range(group):
            qrows, krows = _att_rows(sub)
            valid = _att_mask(group * i + sub, nb)
            acc = [None] * (N_Q_HEADS // 2)
            for h in range(N_KV_HEADS):
                lanes = slice(LANES * (h // 2), LANES * (h // 2 + 1))
                qs = _stack_q(q[qrows], h).astype(BF16)
                prob, _ = _att_probs(qs, k[krows, lanes], _sink_rows(sink_ref, h), valid)
                oh = jnp.dot(prob.astype(BF16), v[krows, lanes], preferred_element_type=F32)
                acc = _unstack_q(oh, h, acc)
            o_ref[qrows, :] = jnp.concatenate(acc, axis=1).astype(BF16)

    zspecs, tab = _att_specs(nb, group)
    return _call(
        body, name=name, grid=(steps,),
        out_shape=(jax.ShapeDtypeStruct((s, D_MODEL), BF16),),
        in_specs=zspecs + tab + tab + [pl.BlockSpec(memory_space=pltpu.SMEM)],
        out_specs=(pl.BlockSpec((group * ATT_BLOCK, D_MODEL), lambda i: (i, 0)),),
        args=(z, z, z, z, cos, cos, cos, cos, sin, sin, sin, sin, sink), sem=("parallel",), comm=comm,
        edge=lambda: (pl.program_id(0) == 0, pl.program_id(0) == steps - 1))


def _att_bwd(z, do, sink, cos, sin, *, comm=None, name):
    s = z.shape[0]
    nb = s // ATT_BLOCK
    group = min(ATT_STEP_BWD, nb)
    steps = nb // group

    def body(zq_ref, kp_ref, kc_ref, kn_ref, cq_ref, cp_ref, cc_ref, cn_ref, sq_ref, sp_ref, sc_ref, sn_ref, sink_ref,
             do_ref, dq_ref, part_ref, dsink_ref):
        i = pl.program_id(0)
        q, k, v = _att_load(zq_ref, kp_ref, kc_ref, kn_ref, cq_ref, sq_ref, cp_ref, sp_ref, cc_ref, sc_ref, cn_ref, sn_ref)
        dsink = None
        nt = (((1,), (1,)), ((), ()))
        for sub in range(group):
            qrows, krows = _att_rows(sub)
            valid = _att_mask_t(group * i + sub, nb)
            dout = do_ref[qrows, :].astype(F32)
            dq_acc = [None] * (N_Q_HEADS // 2)
            dk_acc = [None] * 2
            dv_acc = [None] * 2
            rows = []
            for h in range(N_KV_HEADS):
                grp = h // 2
                lanes = slice(LANES * grp, LANES * (grp + 1))
                kh, vh = k[krows, lanes], v[krows, lanes]
                qs = _stack_q(q[qrows], h).astype(BF16)
                dos = _stack_q(dout, h).astype(BF16)
                sink = _sink_lanes(sink_ref, h)
                sc = jnp.where(valid, lax.dot_general(kh, qs, nt, preferred_element_type=F32), NEG)
                m = jnp.maximum(jnp.max(sc, axis=0, keepdims=True), sink)
                p = jnp.exp(sc - m)
                es = jnp.exp(sink - m)
                inv = 1.0 / (jnp.sum(p, axis=0, keepdims=True) + es)
                prob = p * inv
                dprob = lax.dot_general(vh, dos, nt, preferred_element_type=F32)
                delta = jnp.sum(prob * dprob, axis=0, keepdims=True)
                dsc = (prob * (dprob - delta)).astype(BF16)
                dsk = -(es * inv) * delta
                for gq in range(GROUP):
                    tot = jnp.sum(dsk[:, ATT_BLOCK * gq:ATT_BLOCK * (gq + 1)], axis=1, keepdims=True)
                    rows.append(jnp.broadcast_to(tot, (1, LANES)))
                dqs = lax.dot_general(dsc, kh, (((0,), (0,)), ((), ())), preferred_element_type=F32)
                dq_acc = _unstack_q(dqs, h, dq_acc)
                dkh = jnp.dot(dsc, qs, preferred_element_type=F32)
                dvh = jnp.dot(prob.astype(BF16), dos, preferred_element_type=F32)
                dk_acc[grp] = dkh if dk_acc[grp] is None else dk_acc[grp] + dkh
                dv_acc[grp] = dvh if dv_acc[grp] is None else dv_acc[grp] + dvh
            dq = jnp.concatenate(dq_acc, axis=1) * (HEAD_DIM ** -0.5)
            dq_ref[qrows, :] = _unrope(dq, cq_ref[qrows, :], sq_ref[qrows, :]).astype(BF16)
            part = jnp.concatenate(dk_acc + dv_acc, axis=1)
            for wdw in range(3):
                part_ref[sub, wdw] = part[ATT_BLOCK * wdw:ATT_BLOCK * (wdw + 1), :]
            mine = jnp.concatenate(rows, axis=0)
            dsink = mine if dsink is None else dsink + mine
        _accumulate(dsink_ref, dsink, i == 0)

    zspecs, tab = _att_specs(nb, group)
    tile = pl.BlockSpec((group * ATT_BLOCK, D_MODEL), lambda i: (i, 0))
    return _call(
        body, name=name, grid=(steps,),
        out_shape=(jax.ShapeDtypeStruct((s, D_MODEL), BF16), jax.ShapeDtypeStruct((nb, 3, ATT_BLOCK, 2 * KV_DIM), F32),
                   jax.ShapeDtypeStruct((N_Q_HEADS, LANES), F32)),
        in_specs=zspecs + tab + tab + [pl.BlockSpec(memory_space=pltpu.SMEM), tile],
        out_specs=(tile, pl.BlockSpec((group, 3, ATT_BLOCK, 2 * KV_DIM), lambda i: (i, 0, 0, 0)),
                   pl.BlockSpec((N_Q_HEADS, LANES), lambda i: (0, 0))),
        args=(z, z, z, z, cos, cos, cos, cos, sin, sin, sin, sin, sink, do), sem=("arbitrary",), comm=comm,
        edge=lambda: (pl.program_id(0) == 0, pl.program_id(0) == steps - 1))


def _att_bwd_kv(part, cos, sin, *, name):
    nb = part.shape[0]

    def body(pn_ref, pc_ref, pp_ref, c_ref, s_ref, o_ref):
        j = pl.program_id(0)
        tot = pc_ref[...]
        tot = tot + jnp.where(j < nb - 1, pn_ref[...], 0.0)
        tot = tot + jnp.where(j > 0, pp_ref[...], 0.0)
        dk = _unrope(tot[:, :KV_DIM], c_ref[...], s_ref[...])
        o_ref[...] = jnp.concatenate([dk, tot[:, KV_DIM:]], axis=1).astype(BF16)

    blk = (None, None, ATT_BLOCK, 2 * KV_DIM)
    return pl.pallas_call(
        body, name=name, grid=(nb,),
        out_shape=jax.ShapeDtypeStruct((nb * ATT_BLOCK, 2 * KV_DIM), BF16),
        in_specs=[pl.BlockSpec(blk, lambda j: (jnp.minimum(j + 1, nb - 1), 0, 0, 0)),
                  pl.BlockSpec(blk, lambda j: (j, 1, 0, 0)),
                  pl.BlockSpec(blk, lambda j: (jnp.maximum(j - 1, 0), 2, 0, 0)),
                  pl.BlockSpec((ATT_BLOCK, LANES), lambda j: (j, 0)), pl.BlockSpec((ATT_BLOCK, LANES), lambda j: (j, 0))],
        out_specs=pl.BlockSpec((ATT_BLOCK, 2 * KV_DIM), lambda j: (j, 0)),
        compiler_params=_params(("parallel",)),
    )(part, part, part, cos, sin)


def _bdot(a, b, dims):
    return lax.dot_general(a.astype(BF16), b.astype(BF16), (dims, ((), ())), preferred_element_type=F32)


@jax.custom_vjp
def _dot_nn(a, b):
    return _bdot(a, b, ((1,), (0,)))


@jax.custom_vjp
def _dot_nt(a, b):
    return _bdot(a, b, ((1,), (1,)))


@jax.custom_vjp
def _dot_tn(a, b):
    return _bdot(a, b, ((0,), (0,)))


_dot_nn.defvjp(lambda a, b: (_dot_nn(a, b), (a, b)), lambda r, d: (_dot_nt(d, r[1]), _dot_tn(r[0], d)))
_dot_nt.defvjp(lambda a, b: (_dot_nt(a, b), (a, b)), lambda r, d: (_dot_nn(d, r[1]), _dot_tn(d, r[0])))
_dot_tn.defvjp(lambda a, b: (_dot_tn(a, b), (a, b)), lambda r, d: (_dot_nt(r[1], d), _dot_nn(r[0], d)))


def _running_sum(v, up):
    n = v.shape[0]
    rows = lax.broadcasted_iota(jnp.int32, v.shape, 0)
    sh = 1
    while sh < n:
        if up:
            v = v + jnp.where(rows < n - sh, pltpu.roll(v, n - sh, 0), 0.0)
        else:
            v = v + jnp.where(rows >= sh, pltpu.roll(v, sh, 0), 0.0)
        sh *= 2
    return v


@jax.custom_vjp
def _sum_down(v):
    return _running_sum(v, False)


@jax.custom_vjp
def _sum_up(v):
    return _running_sum(v, True)


_sum_down.defvjp(lambda v: (_running_sum(v, False), None), lambda _, d: (_sum_up(d),))
_sum_up.defvjp(lambda v: (_running_sum(v, True), None), lambda _, d: (_sum_down(d),))

N_SUB = HG_CHUNK // HG_SUB


def _fold_blocks(v):
    out = v[:HG_CHUNK]
    for i in range(1, N_SUB):
        out = out + v[HG_CHUNK * i:HG_CHUNK * (i + 1)]
    return out


@jax.custom_vjp
def _fold(v):
    return _fold_blocks(v)


_fold.defvjp(lambda v: (_fold_blocks(v), None), lambda _, d: (jnp.concatenate([d] * N_SUB, axis=0),))


def _hg_consts(rev):
    c, sub = HG_CHUNK, HG_SUB
    rowpos = lax.broadcasted_iota(jnp.int32, (c, HG_DIM), 0)
    rr = lax.broadcasted_iota(jnp.int32, (N_SUB * c, c), 0)
    key = lax.broadcasted_iota(jnp.int32, (N_SUB * c, c), 1)
    blk, qry = rr // c, rr % c
    if rev:
        rowpos, qry, key = c - 1 - rowpos, c - 1 - qry, c - 1 - key
    keep = (key // sub == blk) & (key <= qry)
    return keep, rowpos


def _pick(b, rowpos, t):
    return jnp.sum(jnp.where(rowpos == t, b, 0.0), axis=0, keepdims=True)


def _hg_local(zq, zf, zv, lbv, consts, dots):
    dot_nn, dot_nt, dot_tn, cum, fold = dots
    keep, rowpos = consts
    sig = _sigmoid(zf)
    f = lbv + (1.0 - lbv) * sig
    g = jnp.log(f)
    k = (1.0 - lbv) * (1.0 - sig)
    q = zq * _sigmoid(zq)
    b = cum(g)
    ends = [_pick(b, rowpos, (j + 1) * HG_SUB - 1) for j in range(N_SUB)]
    b_last = ends[-1]
    b_end = b_last
    for j in range(N_SUB - 1):
        b_end = jnp.where(rowpos // HG_SUB == j, ends[j], b_end)
    kc = k * jnp.exp(b_end - b)
    qbs = [q * jnp.exp(jnp.where(rowpos >= j * HG_SUB, b - ends[j], 0.0)) for j in range(N_SUB)]
    scores = fold(jnp.where(keep, dot_nt(jnp.concatenate(qbs, axis=0), kc), 0.0))
    return dot_nn(scores, zv), q * jnp.exp(b), k * jnp.exp(b_last - b), jnp.exp(b_last)


def _hg_chunk(zq, zf, zv, lbv, st, consts, dots):
    intra, qs, kd, dec = _hg_local(zq, zf, zv, lbv, consts, dots)
    return intra + dots[1](qs, st), dec * st + dots[2](zv, kd)


def _hg_dots(diff, rev):
    if diff:
        return _dot_nn, _dot_nt, _dot_tn, (_sum_up if rev else _sum_down), _fold
    return (lambda a, b: _bdot(a, b, ((1,), (0,))), lambda a, b: _bdot(a, b, ((1,), (1,))),
            lambda a, b: _bdot(a, b, ((0,), (0,))), lambda v: _running_sum(v, rev), _fold_blocks)


def _hg_specs(ts, nch, trow):
    tile = pl.BlockSpec((ts, HG_DIM), lambda h, t: (trow(t), h))
    mats = pl.BlockSpec((None, nch, HG_DIM, HG_DIM), lambda h, t: (h, trow(t), 0, 0))
    vecs = pl.BlockSpec((None, nch, 1, HG_DIM), lambda h, t: (h, trow(t), 0, 0))
    return tile, mats, vecs


def _time_order(nch, rev):
    return range(nch - 1, -1, -1) if rev else range(nch)


def _chunk_rows(c):
    return pl.ds(c * HG_CHUNK, HG_CHUNK)


def _hg_edge(nt):
    h, t = pl.program_id(0), pl.program_id(1)
    return (h == 0) & (t == 0), (h == HG_HEADS - 1) & (t == nt - 1)


def _hg_fwd(z, lb, *, rev, ts, comm=None, name):
    s = z.shape[0]
    nt = s // ts
    nch = ts // HG_CHUNK
    fcol = HG_HEADS * (2 if rev else 1)

    def body(zq_ref, zf_ref, zv_ref, lb_ref, o_ref, st_ref, qs_ref, dec_ref, state_ref):
        @pl.when(pl.program_id(1) == 0)
        def _():
            state_ref[...] = jnp.zeros_like(state_ref)

        consts = _hg_consts(rev)
        dots = _hg_dots(False, rev)
        lbv = lb_ref[...]
        local = {}
        for c in range(nch):
            rows = _chunk_rows(c)
            zv = zv_ref[rows, :]
            intra, qs, kd, dec = _hg_local(zq_ref[rows, :], zf_ref[rows, :], zv, lbv, consts, dots)
            qs = qs.astype(BF16)
            qs_ref[rows, :] = qs
            dec_ref[c] = dec
            local[c] = (intra, qs, dec, dots[2](zv, kd))
        st = state_ref[...]
        for c in _time_order(nch, rev):
            intra, qs, dec, upd = local[c]
            st_ref[c] = st.astype(BF16)
            o_ref[_chunk_rows(c), :] = intra + _bdot(qs, st, ((1,), (1,)))
            st = dec * st + upd
        state_ref[...] = st

    trow = (lambda t: nt - 1 - t) if rev else (lambda t: t)
    col = lambda off: pl.BlockSpec((ts, HG_DIM), lambda h, t: (trow(t), off + h))
    tile, mats, vecs = _hg_specs(ts, nch, trow)
    nchunks = s // HG_CHUNK
    return _call(
        body, name=name, grid=(HG_HEADS, nt),
        out_shape=(jax.ShapeDtypeStruct((s, D_MODEL), F32),
                   jax.ShapeDtypeStruct((HG_HEADS, nchunks, HG_DIM, HG_DIM), BF16),
                   jax.ShapeDtypeStruct((s, D_MODEL), BF16),
                   jax.ShapeDtypeStruct((HG_HEADS, nchunks, 1, HG_DIM), F32)),
        in_specs=[col(0), col(fcol), col(3 * HG_HEADS), pl.BlockSpec((None, 1, HG_DIM), lambda h, t: (h, 0, 0))],
        out_specs=(tile, mats, tile, vecs), args=(z, z, z, lb),
        scratch_shapes=[pltpu.VMEM((HG_DIM, HG_DIM), F32)], sem=("parallel", "arbitrary"), comm=comm,
        edge=lambda: _hg_edge(nt))


def _hg_bwd(z, lb, states, qs, dec, dout, addq, addv, *, rev, ts, comm=None, name):
    s = z.shape[0]
    nt = s // ts
    nch = ts // HG_CHUNK
    fcol = HG_HEADS * (2 if rev else 1)
    has_add = addq is not None

    def body(*refs):
        zq_ref, zf_ref, zv_ref, lb_ref, st_ref, qs_ref, dec_ref, do_ref = refs[:8]
        aq_ref, av_ref = (refs[8], refs[9]) if has_add else (None, None)
        dq_ref, df_ref, dv_ref, dlb_ref, grad_ref = refs[-5:]

        @pl.when(pl.program_id(1) == 0)
        def _():
            grad_ref[...] = jnp.zeros_like(grad_ref)

        consts = _hg_consts(rev)
        dots = _hg_dots(True, rev)
        lbv = lb_ref[...]
        prods = {c: _bdot(do_ref[_chunk_rows(c), :], qs_ref[_chunk_rows(c), :], ((0,), (0,))) for c in range(nch)}
        gleave = {}
        gr = grad_ref[...]
        for c in reversed(_time_order(nch, rev)):
            gleave[c] = gr
            gr = dec_ref[c] * gr + prods[c]
        grad_ref[...] = gr
        dlb_blk = jnp.zeros((1, HG_DIM), F32)
        for c in range(nch):
            rows = _chunk_rows(c)
            fn = lambda a, b2, c2, d2, e2: _hg_chunk(a, b2, c2, d2, e2, consts, dots)
            _, pull = jax.vjp(fn, zq_ref[rows, :], zf_ref[rows, :], zv_ref[rows, :], lbv, st_ref[c].astype(F32))
            dq, df, dv, dlb, _ = pull((do_ref[rows, :], gleave[c]))
            if has_add:
                dq = dq + aq_ref[rows, :].astype(F32)
                dv = dv + av_ref[rows, :].astype(F32)
            dq_ref[rows, :] = dq.astype(dq_ref.dtype)
            df_ref[rows, :] = df.astype(BF16)
            dv_ref[rows, :] = dv.astype(dv_ref.dtype)
            dlb_blk = dlb_blk + dlb

        @pl.when(pl.program_id(1) == 0)
        def _():
            dlb_ref[...] = dlb_blk

        @pl.when(pl.program_id(1) > 0)
        def _():
            dlb_ref[...] += dlb_blk

    trow = (lambda t: t) if rev else (lambda t: nt - 1 - t)
    col = lambda off: pl.BlockSpec((ts, HG_DIM), lambda h, t: (trow(t), off + h))
    tile, mats, vecs = _hg_specs(ts, nch, trow)
    in_specs = [col(0), col(fcol), col(3 * HG_HEADS), pl.BlockSpec((None, 1, HG_DIM), lambda h, t: (h, 0, 0)),
                mats, tile, vecs, tile]
    args = [z, z, z, lb, states, qs, dec, dout]
    if has_add:
        in_specs += [tile, tile]
        args += [addq, addv]
    act = lambda dt: jax.ShapeDtypeStruct((s, D_MODEL), dt)
    sums = BF16
    return _call(
        body, name=name, grid=(HG_HEADS, nt),
        out_shape=(act(sums), act(BF16), act(sums), jax.ShapeDtypeStruct((HG_HEADS, 1, HG_DIM), F32)),
        in_specs=in_specs,
        out_specs=(tile, tile, tile, pl.BlockSpec((None, 1, HG_DIM), lambda h, t: (h, 0, 0))), args=tuple(args),
        scratch_shapes=[pltpu.VMEM((HG_DIM, HG_DIM), F32)], sem=("parallel", "arbitrary"), comm=comm,
        edge=lambda: _hg_edge(nt))


def _hg_post(of, ob, z, norm_g, *, tm, name):
    s = of.shape[0]

    def body(of_ref, ob_ref, gate_ref, ng_ref, y_ref):
        gn = ng_ref[...]
        for h in range(HG_HEADS):
            ln = slice(HG_DIM * h, HG_DIM * (h + 1))
            o = of_ref[:, ln] + ob_ref[:, ln]
            r = lax.rsqrt(jnp.mean(o * o, axis=-1, keepdims=True) + LN_EPS)
            gt = gate_ref[:, ln]
            y_ref[:, ln] = (o * r * gn * gt * _sigmoid(gt)).astype(BF16)

    row = lambda i: (i, 0)
    return pl.pallas_call(
        body, name=name, grid=(s // tm,),
        out_shape=jax.ShapeDtypeStruct((s, D_MODEL), BF16),
        in_specs=[pl.BlockSpec((tm, D_MODEL), row), pl.BlockSpec((tm, D_MODEL), row),
                  pl.BlockSpec((tm, D_MODEL), lambda i: (i, 4)), pl.BlockSpec((1, HG_DIM), lambda i: (0, 0))],
        out_specs=pl.BlockSpec((tm, D_MODEL), row),
        compiler_params=_params(("parallel",)),
    )(of, ob, z, norm_g)


def _hg_post_bwd(dy, of, ob, z, norm_g, *, tm, name):
    s = of.shape[0]

    def body(dy_ref, of_ref, ob_ref, gate_ref, ng_ref, do_ref, dgate_ref, dng_ref):
        gn = ng_ref[...]
        tot = jnp.zeros((1, HG_DIM), F32)
        for h in range(HG_HEADS):
            ln = slice(HG_DIM * h, HG_DIM * (h + 1))
            d = dy_ref[:, ln].astype(F32)
            o = of_ref[:, ln] + ob_ref[:, ln]
            r = lax.rsqrt(jnp.mean(o * o, axis=-1, keepdims=True) + LN_EPS)
            ohat = o * r
            gt = gate_ref[:, ln]
            sg = _sigmoid(gt)
            don = d * gt * sg
            dgate_ref[:, ln] = (d * ohat * gn * sg * (1.0 + gt * (1.0 - sg))).astype(BF16)
            tot = tot + jnp.sum(don * ohat, axis=0, keepdims=True)
            dohat = don * gn
            do_ref[:, ln] = r * (dohat - ohat * jnp.mean(dohat * ohat, axis=-1, keepdims=True))

        @pl.when(pl.program_id(0) == 0)
        def _():
            dng_ref[...] = tot

        @pl.when(pl.program_id(0) > 0)
        def _():
            dng_ref[...] += tot

    row = lambda i: (i, 0)
    return pl.pallas_call(
        body, name=name, grid=(s // tm,),
        out_shape=(jax.ShapeDtypeStruct((s, D_MODEL), F32), jax.ShapeDtypeStruct((s, D_MODEL), BF16),
                   jax.ShapeDtypeStruct((1, HG_DIM), F32)),
        in_specs=[pl.BlockSpec((tm, D_MODEL), row), pl.BlockSpec((tm, D_MODEL), row), pl.BlockSpec((tm, D_MODEL), row),
                  pl.BlockSpec((tm, D_MODEL), lambda i: (i, 4)), pl.BlockSpec((1, HG_DIM), lambda i: (0, 0))],
        out_specs=(pl.BlockSpec((tm, D_MODEL), row), pl.BlockSpec((tm, D_MODEL), row),
                   pl.BlockSpec((1, HG_DIM), lambda i: (0, 0))),
        compiler_params=_params(("arbitrary",)),
    )(dy, of, ob, z, norm_g)


def _lb_fwd(logits, *, name):
    w = logits.shape[1]

    def body(l_ref, o_ref):
        lg = l_ref[...]
        e = jnp.exp(lg - jnp.max(lg, axis=0, keepdims=True))
        sm = e / jnp.sum(e, axis=0, keepdims=True)
        o_ref[0:1, :] = sm[1:2]
        o_ref[1:2, :] = sm[1:2] + sm[2:3] + sm[3:4]

    return pl.pallas_call(body, name=name, out_shape=jax.ShapeDtypeStruct((2, w), F32))(logits)


def _lb_bwd(logits, dlb, *, name):
    w = logits.shape[1]

    def body(l_ref, d_ref, o_ref):
        lg = l_ref[...]
        e = jnp.exp(lg - jnp.max(lg, axis=0, keepdims=True))
        sm = e / jnp.sum(e, axis=0, keepdims=True)
        d1, d3 = d_ref[0:1, :], d_ref[1:2, :]
        dot = sm[1:2] * (d1 + d3) + (sm[2:3] + sm[3:4]) * d3
        o_ref[0:1, :] = -sm[0:1] * dot
        o_ref[1:2, :] = sm[1:2] * (d1 + d3 - dot)
        o_ref[2:3, :] = sm[2:3] * (d3 - dot)
        o_ref[3:4, :] = sm[3:4] * (d3 - dot)

    return pl.pallas_call(body, name=name, out_shape=jax.ShapeDtypeStruct((4, w), F32))(logits, dlb)


def _adamw(w, g, m, v, *, tr, g_off=0, name):
    rows = w.shape[0]
    parts = g.ndim == 3
    c1 = 1.0 / (1.0 - ADAM_B1 ** ADAM_STEP)
    c2 = 1.0 / (1.0 - ADAM_B2 ** ADAM_STEP)

    def body(w_ref, g_ref, m_ref, v_ref, go_ref, d_ref, mo_ref, vo_ref):
        if parts:
            gg = g_ref[0].astype(F32)
            for i in range(1, N_DEV):
                gg = gg + g_ref[i].astype(F32)
        else:
            gg = g_ref[...]
        mm = ADAM_B1 * m_ref[...] + (1.0 - ADAM_B1) * gg
        vv = ADAM_B2 * v_ref[...] + (1.0 - ADAM_B2) * (gg * gg)
        go_ref[...] = gg
        mo_ref[...] = mm
        vo_ref[...] = vv
        d_ref[...] = -ADAM_LR * ((mm * c1) / (jnp.sqrt(vv * c2) + ADAM_EPS) + ADAM_WD * w_ref[...])

    tile = pl.BlockSpec((tr, D_MODEL), lambda i: (i, 0))
    gspec = pl.BlockSpec((N_DEV, tr, D_MODEL), lambda i: (0, i + g_off // tr, 0)) if parts else tile
    out = jax.ShapeDtypeStruct((rows, D_MODEL), F32)
    return pl.pallas_call(
        body, name=name, grid=(rows // tr,),
        out_shape=(out, out, out, out),
        in_specs=[tile, gspec, tile, tile], out_specs=(tile, tile, tile, tile),
        compiler_params=_params(("parallel",)),
    )(w, g, m, v)


def _sum8(parts, *, name):
    def body(p_ref, o_ref):
        tot = p_ref[0]
        for i in range(1, N_DEV):
            tot = tot + p_ref[i]
        o_ref[...] = tot

    return pl.pallas_call(body, name=name, out_shape=jax.ShapeDtypeStruct(parts.shape[1:], parts.dtype))(parts)


def _layer_params(i):
    j = i // 2
    mix = [("att_w_qkv", j, 1), ("att_w_o", j, 0)] if i % 2 == 0 else [("hgrn_w_in", j, 1), ("hgrn_w_o", j, 0)]
    return mix + [("ffn_w_in", i, 1), ("ffn_w_out", i, 0), ("ple_w_gate", i, 0), ("ple_w_proj", i, 1)]


def _pack_local(tree, params):
    return jnp.concatenate([tree[n][j].reshape(-1, D_MODEL) for n, j, _ in params], axis=0)


def _unpack_local(packed, params, like):
    out, r = {}, 0
    for n, _, _ in params:
        shp = like[n].shape[1:]
        k = shp[0] * shp[1] // D_MODEL
        out[n] = packed[r:r + k].reshape(shp)
        r += k
    return out


def _unpack_gathered(gathered, params, like):
    out, r = {}, 0
    for n, _, ax in params:
        shp = like[n].shape[1:]
        k = shp[0] * shp[1] // D_MODEL
        t = gathered[:, r:r + k].reshape((N_DEV,) + shp)
        out[n] = (jnp.moveaxis(t, 0, 1).reshape(shp[0], N_DEV * shp[1]) if ax == 1
                  else t.reshape(N_DEV * shp[0], shp[1]))
        r += k
    return out


def _pack_full(grads, params, like):
    cols = []
    for n, _, ax in params:
        shp = like[n].shape[1:]
        t = (jnp.moveaxis(grads[n].reshape(shp[0], N_DEV, shp[1]), 1, 0) if ax == 1
             else grads[n].reshape(N_DEV, shp[0], shp[1]))
        cols.append(t.reshape(N_DEV, -1, D_MODEL).astype(BF16))
    return jnp.concatenate(cols, axis=1)


def _row_tile(rows):
    return max(t for t in range(16, 257, 16) if rows % t == 0)


SMALL_ROWS = 24


def _pad_row(a):
    flat = a.reshape(1, -1)
    return jnp.pad(flat, ((0, 0), (0, D_MODEL - flat.shape[1])))


def _tile(n, pref):
    return min(n, pref)


def kernel(x, p, att_w_qkv, att_sink, att_w_o, hgrn_w_in, hgrn_lb_logits, hgrn_norm_g, hgrn_w_o, ln_mix_g, ln_mix_b, ffn_w_in, ffn_w_out, ln_ffn_g, ln_ffn_b, ple_w_gate, ple_w_proj, loss_target, m_att_w_qkv, m_att_sink, m_att_w_o, m_hgrn_w_in, m_hgrn_lb_logits, m_hgrn_norm_g, m_hgrn_w_o, m_ln_mix_g, m_ln_mix_b, m_ffn_w_in, m_ffn_w_out, m_ln_ffn_g, m_ln_ffn_b, m_ple_w_gate, m_ple_w_proj, v_att_w_qkv, v_att_sink, v_att_w_o, v_hgrn_w_in, v_hgrn_lb_logits, v_hgrn_norm_g, v_hgrn_w_o, v_ln_mix_g, v_ln_mix_b, v_ffn_w_in, v_ffn_w_out, v_ln_ffn_g, v_ln_ffn_b, v_ple_w_gate, v_ple_w_proj):
    names = ["att_w_qkv", "att_sink", "att_w_o", "hgrn_w_in", "hgrn_lb_logits", "hgrn_norm_g", "hgrn_w_o", "ln_mix_g",
             "ln_mix_b", "ffn_w_in", "ffn_w_out", "ln_ffn_g", "ln_ffn_b", "ple_w_gate", "ple_w_proj"]
    w = dict(zip(names, (att_w_qkv, att_sink, att_w_o, hgrn_w_in, hgrn_lb_logits, hgrn_norm_g, hgrn_w_o, ln_mix_g,
                         ln_mix_b, ffn_w_in, ffn_w_out, ln_ffn_g, ln_ffn_b, ple_w_gate, ple_w_proj)))
    mom = dict(zip(names, (m_att_w_qkv, m_att_sink, m_att_w_o, m_hgrn_w_in, m_hgrn_lb_logits, m_hgrn_norm_g, m_hgrn_w_o,
                           m_ln_mix_g, m_ln_mix_b, m_ffn_w_in, m_ffn_w_out, m_ln_ffn_g, m_ln_ffn_b, m_ple_w_gate,
                           m_ple_w_proj)))
    var = dict(zip(names, (v_att_w_qkv, v_att_sink, v_att_w_o, v_hgrn_w_in, v_hgrn_lb_logits, v_hgrn_norm_g, v_hgrn_w_o,
                           v_ln_mix_g, v_ln_mix_b, v_ffn_w_in, v_ffn_w_out, v_ln_ffn_g, v_ln_ffn_b, v_ple_w_gate,
                           v_ple_w_proj)))
    s = x.shape[1]
    me = 4 * lax.axis_index("x") + 2 * lax.axis_index("y") + lax.axis_index("c")
    tm = _tile(s, 512)
    tbig = _tile(s, 1024)
    ts = _tile(s // 2, 2048)
    x0 = x.reshape(s, D_MODEL)
    target = loss_target.reshape(s, D_MODEL)
    pl_in = p.reshape(DEPTH, s, PLE_DIM)

    full =_unpack_gathered(_gather(_pack_local(w, _layer_params(0)).astype(BF16), name="gather_weights"),
                            _layer_params(0), w)
    lb_rows = jnp.pad(hgrn_lb_logits.reshape(8, HG_DIM), ((0, 0), (0, D_MODEL - HG_DIM)))
    lb_all = _gather(lb_rows, name="gather_lb")[:, :, :HG_DIM]
    logits_full = jnp.moveaxis(lb_all, 0, 1).reshape(DEPTH, 2 * D_MODEL)
    lb = _lb_fwd(logits_full, name="lb_fwd")
    cos, sin = _rope_tables(s)

    saved = []
    xf, xb = x0, x0
    for i in range(DEPTH):
        j = i // 2
        sv = {"xb": xb, "w": full}
        nxt = nxt2 = None
        if i + 1 < DEPTH:
            ahead = _layer_params(i + 1)
            n_first = len(ahead) if i % 2 == 0 else 3
            nxt = (_pack_local(w, ahead[:n_first]).astype(BF16), True)
            if n_first < len(ahead):
                nxt2 = (_pack_local(w, ahead[n_first:]).astype(BF16), True)
        if i % 2 == 0:
            z = _mm(xb, full["att_w_qkv"], tm=tbig, tn=512, tk=D_MODEL, name="att_in")
            o, *more = _att_fwd(z, w["att_sink"][j], cos, sin, comm=nxt, name="att_fwd")
            w_o = full["att_w_o"]
        else:
            z = _mm(xb, full["hgrn_w_in"], tm=tbig, tn=1024, tk=D_MODEL, name="hgrn_in")
            lbl = lb[j].reshape(2, HG_HEADS, 1, HG_DIM)
            of, st_f, qs_f, dec_f, *more = _hg_fwd(z, lbl[0], rev=False, ts=ts, comm=nxt, name="hgrn_fwd")
            ob, st_b, qs_b, dec_b = _hg_fwd(z, lbl[1], rev=True, ts=ts, name="hgrn_fwd_rev")
            o = _hg_post(of, ob, z, w["hgrn_norm_g"][j].reshape(1, HG_DIM), tm=tm, name="hgrn_post")
            w_o = full["hgrn_w_o"]
            sv.update(of=of, ob=ob, st_f=st_f, st_b=st_b, lbl=lbl, qs_f=qs_f, qs_b=qs_b, dec_f=dec_f, dec_b=dec_b)
        sv.update(z=z, o=o)
        g1, b1 = w["ln_mix_g"][i:i + 1], w["ln_mix_b"][i:i + 1]
        pre1, x1b = _proj_ln(o, w_o, xf, g1, b1, tm=tm, name="mix_out_ln")
        gg, uu, act, *more2 = _ffn_in(x1b, full["ffn_w_in"], tm=tm, tn=FF_TILE, comm=nxt2, name="ffn_in")
        pre2, x2b, xf, xb = _ffn_out_ple(act, full["ffn_w_out"], pre1, g1, b1, w["ln_ffn_g"][i:i + 1],
                                         w["ln_ffn_b"][i:i + 1], pl_in[i], full["ple_w_gate"], full["ple_w_proj"], tm=tm,
                                         name="ffn_out_ple")
        sv.update(pre1=pre1, x1b=x1b, g=gg, u=uu, act=act, pre2=pre2, x2b=x2b)
        saved.append(sv)
        if nxt is not None:
            full = _unpack_gathered(more[0], ahead[:n_first], w)
        if nxt2 is not None:
            full.update(_unpack_gathered(more2[0], ahead[n_first:], w))

    dx, loss_blk = _loss_head(xf, target, tm=tm, name="loss_head")
    loss = lax.psum(loss_blk[0, 0], AXES)

    small = {n: [None] * DEPTH for n in ("ln_mix_g", "ln_mix_b", "ln_ffn_g", "ln_ffn_b")}
    dlb_rows = [None] * 4
    dnorm, dsink = [None] * 2, [None] * 2
    recv_late, recv_early = [None] * DEPTH, [None] * DEPTH
    above = None
    mmw = functools.partial(_mm, ta=True, tk=_tile(s, 2048), out_dtype=BF16)
    for i in reversed(range(DEPTH)):
        j = i // 2
        sv = saved[i]
        full, gl = sv["w"], {}
        da, dpp, dy2, dy2b, small["ln_ffn_g"][i], small["ln_ffn_b"][i] = _ple_ln_bwd(
            dx, sv["x2b"], pl_in[i], full["ple_w_gate"], full["ple_w_proj"], sv["pre2"], w["ln_ffn_g"][i:i + 1],
            tm=tm, name="ple_ln_bwd")
        gl["ple_w_gate"] = mmw(sv["x2b"], da, tm=D_MODEL, tn=D_MODEL, name="dw_ple_gate")
        gl["ple_w_proj"] = mmw(pl_in[i], dpp, tm=PLE_DIM, tn=D_MODEL, name="dw_ple_proj")
        dg, du = _ffn_bwd_act(dy2b, full["ffn_w_out"], sv["g"], sv["u"], tm=_tile(s, SUB_ROWS), tn=D_FF,
                              name="ffn_bwd_act")
        gl["ffn_w_out"] = mmw(sv["act"], dy2b, tm=FF_TILE, tn=D_MODEL, name="dw_ffn_out")
        dy1, dy1b, small["ln_mix_g"][i], small["ln_mix_b"][i] = _dx_from_pieces(
            [dg, du], full["ffn_w_in"], dy2, tm=_tile(s, SUB_ROWS), ln=(sv["pre1"], w["ln_mix_g"][i:i + 1]),
            name="ffn_bwd_x_ln")
        gl["ffn_w_in"] = mmw(sv["x1b"], [dg, du], tm=D_MODEL, tn=FF_TILE, name="dw_ffn_in")
        n_out, n_inw = ("att_w_o", "att_w_qkv") if i % 2 == 0 else ("hgrn_w_o", "hgrn_w_in")
        do = _mm(dy1b, full[n_out], tm=tbig, tn=D_MODEL, tk=D_MODEL, tb=True, out_dtype=BF16, name="mix_out_bwd")
        gl[n_out] = mmw(sv["o"], dy1b, tm=D_MODEL, tn=D_MODEL, name="dw_mix_out")
        early = _pack_full(gl, _layer_params(i)[1:], w)
        comm = (early if above is None else jnp.concatenate([above, early], axis=1), False)
        if i % 2 == 0:
            dzq, part, dsk, *more = _att_bwd(sv["z"], do, w["att_sink"][j], cos, sin, comm=comm, name="att_bwd")
            dz = [dzq, _att_bwd_kv(part, cos, sin, name="att_bwd_kv")]
            dsink[j] = dsk[:, 0]
        else:
            dsum, dgate, dnorm[j] = _hg_post_bwd(do, sv["of"], sv["ob"], sv["z"], w["hgrn_norm_g"][j].reshape(1, HG_DIM),
                                                 tm=tm, name="hgrn_post_bwd")
            dq1, df1, dv1, dlb1, *more = _hg_bwd(sv["z"], sv["lbl"][0], sv["st_f"], sv["qs_f"], sv["dec_f"], dsum, None,
                                                 None, rev=False, ts=ts, comm=comm, name="hgrn_bwd")
            dq2, df2, dv2, dlb2 = _hg_bwd(sv["z"], sv["lbl"][1], sv["st_b"], sv["qs_b"], sv["dec_b"], dsum, dq1, dv1,
                                          rev=True, ts=ts, name="hgrn_bwd_rev")
            dz = [dq2, df1, df2, dv2, dgate]
            dlb_rows[2 * j] = dlb1.reshape(1, D_MODEL)
            dlb_rows[2 * j + 1] = dlb2.reshape(1, D_MODEL)
        if above is not None:
            recv_late[i + 1] = (more[0], 0)
        recv_early[i] = (more[0], 0 if above is None else above.shape[1])
        dx = _dx_from_pieces(dz, full[n_inw], dy1, tm=tm, name="mix_in_bwd")
        gl[n_inw] = mmw(sv["xb"], dz, tm=D_MODEL, tn=512, name="dw_mix_in")
        above = _pack_full(gl, _layer_params(i)[:1], w)
    grad_x = dx.reshape(x.shape)
    recv_late[0] = (_exchange(above, name="exchange_grads"), 0)

    big_out = [{n: [None] * w[n].shape[0] for n, _ in BIG} for _ in range(4)]
    for i in range(DEPTH):
        for params, (got, off) in ((_layer_params(i)[:1], recv_late[i]), (_layer_params(i)[1:], recv_early[i])):
            w_part = _pack_local(w, params)
            outs = _adamw(w_part, got, _pack_local(mom, params), _pack_local(var, params),
                          tr=_row_tile(math.gcd(w_part.shape[0], off)), g_off=off, name="adamw_big")
            for kind, packed in enumerate(outs):
                for (n, j, _), piece in zip(params, _unpack_local(packed, params, w).values()):
                    big_out[kind][n][j] = piece
    big_out = [{n: jnp.stack(v) for n, v in kind.items()} for kind in big_out]

    small_rows = jnp.concatenate(
        [jnp.concatenate(small[n], axis=0) for n in ("ln_mix_g", "ln_mix_b", "ln_ffn_g", "ln_ffn_b")] + dlb_rows
        + [_pad_row(jnp.stack(dnorm)), _pad_row(jnp.stack(dsink)), jnp.zeros((2, D_MODEL), F32)], axis=0)
    small_all = _gather(small_rows, name="gather_small")
    lbw, lbm, lbv = (t.reshape(4, 2 * HG_DIM) for t in (hgrn_lb_logits, mom["hgrn_lb_logits"], var["hgrn_lb_logits"]))
    summed = _sum8(small_all, name="sum_small")
    dlb_mine = lax.dynamic_slice_in_dim(summed[16:20].reshape(2, 2, HG_HEADS, HG_DIM), me, 1, axis=2)
    dlogits = _lb_bwd(lbw, dlb_mine.reshape(2, 2 * HG_DIM), name="lb_bwd")

    def small_pack(ln4, lbt, ng, sk):
        return jnp.concatenate([ln4[n] for n in ("ln_mix_g", "ln_mix_b", "ln_ffn_g", "ln_ffn_b")]
                               + [_pad_row(lbt), _pad_row(ng), _pad_row(sk), jnp.zeros((5, D_MODEL), F32)], axis=0)

    g_small = jnp.concatenate([summed[:16], _pad_row(dlogits), summed[20:22], jnp.zeros((5, D_MODEL), F32)], axis=0)
    souts = _adamw(small_pack(w, lbw, w["hgrn_norm_g"], w["att_sink"]), g_small,
                   small_pack(mom, lbm, mom["hgrn_norm_g"], mom["att_sink"]),
                   small_pack(var, lbv, var["hgrn_norm_g"], var["att_sink"]), tr=SMALL_ROWS, name="adamw_small")

    def small_unpack(t):
        out = {n: t[4 * k:4 * k + 4] for k, n in enumerate(("ln_mix_g", "ln_mix_b", "ln_ffn_g", "ln_ffn_b"))}
        out["hgrn_lb_logits"] = t[16].reshape(hgrn_lb_logits.shape)
        out["hgrn_norm_g"] = t[17, :2 * HG_DIM].reshape(hgrn_norm_g.shape)
        out["att_sink"] = t[18, :2 * N_Q_HEADS].reshape(att_sink.shape)
        return out

    result = [loss, grad_x]
    for big_t, small_t in zip(big_out, souts):
        merged = dict(big_t)
        merged.update(small_unpack(small_t))
        result += [merged[n] for n in names]
    return tuple(result)
```

```python
import functools
import math

import jax
import jax.numpy as jnp
from jax import lax
from jax.experimental import pallas as pl
from jax.experimental.pallas import tpu as pltpu

F32 = jnp.float32
BF16 = jnp.bfloat16

D_MODEL = 1024
DEPTH = 4
HEAD_DIM = 64
N_Q_HEADS = 16
N_KV_HEADS = 4
GROUP = 4
KV_DIM = 256
ATT_BLOCK = 128
ROPE_DIM = 16
ROPE_THETA = 500000.0
HG_HEADS = 8
HG_DIM = 128
HG_CHUNK = 64
HG_SUB = 16
D_FF = 2816
FF_TILE = 1408
SUB_ROWS = 256
PLE_DIM = 256
ALPHA = (2 * DEPTH) ** 0.25
LN_EPS = 1e-5
ADAM_LR, ADAM_B1, ADAM_B2, ADAM_EPS, ADAM_WD, ADAM_STEP = 0.001, 0.9, 0.999, 1e-08, 0.01, 10

N_DEV = 8
LANES = 128
VMEM_LIMIT = 52 * 1024 * 1024
NEG = -1e30
MESH = pl.DeviceIdType.MESH
AXES = ("x", "y", "c")

BIG = (("att_w_qkv", 2), ("att_w_o", 1), ("hgrn_w_in", 2), ("hgrn_w_o", 1), ("ffn_w_in", 2), ("ffn_w_out", 1),
       ("ple_w_gate", 1), ("ple_w_proj", 2))


def _params(sem=None, vmem=VMEM_LIMIT):
    return pltpu.CompilerParams(dimension_semantics=sem, vmem_limit_bytes=vmem)


def _sigmoid(x):
    return jax.nn.sigmoid(x)


def _direct_copies(src_ref, out_ref, send_sems, recv_sems, local_sem, gather, arrivals):
    x, y, c = lax.axis_index("x"), lax.axis_index("y"), lax.axis_index("c")
    me = 4 * x + 2 * y + c
    mine = (lambda j: src_ref) if gather else (lambda j: src_ref.at[j])
    pairs = []
    for k in range(1, N_DEV):
        px, py, pc = x ^ (k >> 2), y ^ ((k >> 1) & 1), c ^ (k & 1)
        peer = 4 * px + 2 * py + pc
        send = pltpu.make_async_remote_copy(
            src_ref=mine(peer), dst_ref=out_ref.at[me], send_sem=send_sems.at[k], recv_sem=recv_sems.at[k],
            device_id=(px, py, pc), device_id_type=MESH)
        arrival = pltpu.make_async_remote_copy(
            src_ref=mine(peer), dst_ref=out_ref.at[peer], send_sem=send_sems.at[k], recv_sem=recv_sems.at[k],
            device_id=(x, y, c), device_id_type=MESH) if arrivals else None
        pairs.append((send, arrival))
    return pltpu.make_async_copy(mine(me), out_ref.at[me], local_sem), pairs


def _direct_start(*refs, gather):
    local, pairs = _direct_copies(*refs, gather, False)
    local.start()
    for send, _ in pairs:
        send.start()


def _direct_wait(*refs, gather):
    local, pairs = _direct_copies(*refs, gather, True)
    for send, arrival in pairs:
        send.wait_send()
        arrival.wait_recv()
    local.wait()


COMM_SCRATCH = [pltpu.SemaphoreType.DMA((N_DEV,)), pltpu.SemaphoreType.DMA((N_DEV,)), pltpu.SemaphoreType.DMA]


def _exchange(src, *, gather=False, name):
    def body(*refs):
        _direct_start(*refs, gather=gather)
        _direct_wait(*refs, gather=gather)

    blk = tuple(src.shape) if gather else tuple(src.shape[1:])
    return pl.pallas_call(
        body, name=name,
        out_shape=jax.ShapeDtypeStruct((N_DEV,) + blk, src.dtype),
        in_specs=[pl.BlockSpec(memory_space=pltpu.HBM)],
        out_specs=pl.BlockSpec(memory_space=pltpu.HBM),
        scratch_shapes=COMM_SCRATCH,
    )(src)


def _call(body, *, name, grid, out_shape, in_specs, out_specs, args, scratch_shapes=(), sem, comm=None, edge=None):
    out_shape, out_specs = tuple(out_shape), tuple(out_specs)
    if comm is None:
        return pl.pallas_call(body, name=name, grid=grid, out_shape=out_shape, in_specs=list(in_specs),
                              out_specs=out_specs, scratch_shapes=list(scratch_shapes),
                              compiler_params=_params(sem))(*args)
    src, gather = comm
    n_in, n_out, n_scr = len(args), len(out_shape), len(scratch_shapes)
    blk = tuple(src.shape) if gather else tuple(src.shape[1:])
    hbm = pl.BlockSpec(memory_space=pltpu.HBM)

    def carrying(*refs):
        ins, src_ref = refs[:n_in], refs[n_in]
        outs, dst_ref = refs[n_in + 1:n_in + 1 + n_out], refs[n_in + 1 + n_out]
        own = refs[n_in + 2 + n_out:n_in + 2 + n_out + n_scr]
        comm_refs = (src_ref, dst_ref) + tuple(refs[n_in + 2 + n_out + n_scr:])
        first, last = edge()

        @pl.when(first)
        def _():
            _direct_start(*comm_refs, gather=gather)

        body(*ins, *outs, *own)

        @pl.when(last)
        def _():
            _direct_wait(*comm_refs, gather=gather)

    return pl.pallas_call(
        carrying, name=name, grid=grid,
        out_shape=out_shape + (jax.ShapeDtypeStruct((N_DEV,) + blk, src.dtype),),
        in_specs=list(in_specs) + [hbm], out_specs=out_specs + (hbm,),
        scratch_shapes=list(scratch_shapes) + COMM_SCRATCH,
        compiler_params=_params(("arbitrary",) * len(grid)),
    )(*args, src)


def _gather(src, *, name):
    def body(src_ref, out_ref, send_sems, recv_sems, local_sem):
        x, y, c = lax.axis_index("x"), lax.axis_index("y"), lax.axis_index("c")
        sibling = (x, y, 1 - c)
        chips = [(1 - x, y), (x, 1 - y), (1 - x, 1 - y)]

        def rows(px, py, pc):
            return out_ref.at[4 * px + 2 * py + pc]

        def copy(k, block, to, from_src=False):
            return pltpu.make_async_remote_copy(
                src_ref=src_ref if from_src else rows(*block), dst_ref=rows(*block), send_sem=send_sems.at[k],
                recv_sem=recv_sems.at[k], device_id=to, device_id_type=MESH)

        me = (x, y, c)
        mine = pltpu.make_async_copy(src_ref, rows(*me), local_sem)
        mine.start()
        first = [copy(0, me, sibling, from_src=True)]
        first += [copy(1 + j, me, (*chip, c), from_src=True) for j, chip in enumerate(chips)]
        for cp in first:
            cp.start()
        passed = [copy(4 + j, (*chip, c), sibling) for j, chip in enumerate(chips)]
        for j, chip in enumerate(chips):
            copy(1 + j, (*chip, c), me).wait_recv()
            passed[j].start()
        copy(0, sibling, me).wait_recv()
        for j, chip in enumerate(chips):
            copy(4 + j, (*chip, 1 - c), me).wait_recv()
        for cp in first + passed:
            cp.wait_send()
        mine.wait()

    return pl.pallas_call(
        body, name=name,
        out_shape=jax.ShapeDtypeStruct((N_DEV,) + tuple(src.shape), src.dtype),
        in_specs=[pl.BlockSpec(memory_space=pltpu.HBM)],
        out_specs=pl.BlockSpec(memory_space=pltpu.HBM),
        scratch_shapes=[pltpu.SemaphoreType.DMA((7,)), pltpu.SemaphoreType.DMA((7,)), pltpu.SemaphoreType.DMA],
    )(src)


def _mm(a, b, *, tm, tn, tk, ta=False, tb=False, out_dtype=F32, name):
    b_list = list(b) if isinstance(b, (list, tuple)) else [b]
    assert not (tb and len(b_list) > 1)
    m, kdim = (a.shape[1], a.shape[0]) if ta else a.shape
    joff = [0]
    for piece in b_list:
        joff.append(joff[-1] + (piece.shape[0] if tb else piece.shape[1]) // tn)
    nk, n = kdim // tk, joff[-1] * tn
    dims = (((0 if ta else 1,), (1 if tb else 0,)), ((), ()))

    def mine(j, p):
        return (j >= joff[p]) & (j < joff[p + 1])

    def body(*refs):
        a_ref, b_refs, o_ref = refs[0], refs[1:1 + len(b_list)], refs[1 + len(b_list)]
        acc_ref = refs[-1] if nk > 1 else None
        j, k = pl.program_id(1), pl.program_id(2)
        for p, b_ref in enumerate(b_refs):
            def step(b_ref=b_ref):
                part = lax.dot_general(a_ref[...].astype(BF16), b_ref[...].astype(BF16), dims,
                                       preferred_element_type=F32)
                if nk == 1:
                    o_ref[...] = part.astype(out_dtype)
                else:
                    _accumulate(acc_ref, part, k == 0)

            if len(b_list) == 1:
                step()
            else:
                pl.when(mine(j, p))(step)
        if nk > 1:
            @pl.when(k == nk - 1)
            def _():
                o_ref[...] = acc_ref[...].astype(out_dtype)

    def b_spec(p):
        jj = lambda j: jnp.clip(j - joff[p], 0, joff[p + 1] - joff[p] - 1)
        kk = (lambda j, k: k) if len(b_list) == 1 else (lambda j, k: jnp.where(mine(j, p), k, 0))
        return (pl.BlockSpec((tn, tk), lambda i, j, k: (jj(j), kk(j, k))) if tb
                else pl.BlockSpec((tk, tn), lambda i, j, k: (kk(j, k), jj(j))))

    a_spec = pl.BlockSpec((tk, tm), lambda i, j, k: (k, i)) if ta else pl.BlockSpec((tm, tk), lambda i, j, k: (i, k))
    return pl.pallas_call(
        body, name=name, grid=(m // tm, n // tn, nk),
        out_shape=jax.ShapeDtypeStruct((m, n), out_dtype),
        in_specs=[a_spec] + [b_spec(p) for p in range(len(b_list))],
        out_specs=pl.BlockSpec((tm, tn), lambda i, j, k: (i, j)),
        scratch_shapes=[pltpu.VMEM((tm, tn), F32)] if nk > 1 else [],
        compiler_params=_params(("parallel", "parallel", "arbitrary")),
    )(a, *b_list)


def _dx_from_pieces(pieces, w, add, *, tm, ln=None, name):
    s = pieces[0].shape[0]
    widths = [p.shape[1] for p in pieces]

    def body(*refs):
        p_refs, (w_ref, add_ref) = refs[:len(pieces)], refs[len(pieces):len(pieces) + 2]
        rest = refs[len(pieces) + 2:]
        pg = jnp.zeros((1, D_MODEL), F32)
        pb = jnp.zeros((1, D_MODEL), F32)
        for rs in _row_parts(tm):
            r = ALPHA * add_ref[rs, :]
            off = 0
            for p_ref, width in zip(p_refs, widths):
                r = r + lax.dot_general(p_ref[rs, :], w_ref[:, off:off + width], (((1,), (1,)), ((), ())),
                                        preferred_element_type=F32)
                off += width
            if ln is None:
                rest[0][rs, :] = r
            else:
                dy, qg, qb = _ln_bwd_rows(r, rest[0][rs, :], rest[1][...])
                rest[2][rs, :] = dy
                rest[3][rs, :] = dy.astype(BF16)
                pg, pb = pg + qg, pb + qb
        if ln is not None:
            _accumulate(rest[4], pg, pl.program_id(0) == 0)
            _accumulate(rest[5], pb, pl.program_id(0) == 0)

    row = lambda i: (i, 0)
    tile = pl.BlockSpec((tm, D_MODEL), row)
    vec = pl.BlockSpec((1, D_MODEL), lambda i: (0, 0))
    in_specs = ([pl.BlockSpec((tm, width), row) for width in widths]
                + [pl.BlockSpec((D_MODEL, sum(widths)), lambda i: (0, 0)), tile])
    args = list(pieces) + [w, add]
    out_shape, out_specs = jax.ShapeDtypeStruct((s, D_MODEL), F32), tile
    if ln is not None:
        in_specs += [tile, vec]
        args += list(ln)
        out_shape = (jax.ShapeDtypeStruct((s, D_MODEL), F32), jax.ShapeDtypeStruct((s, D_MODEL), BF16),
                     jax.ShapeDtypeStruct((1, D_MODEL), F32), jax.ShapeDtypeStruct((1, D_MODEL), F32))
        out_specs = (tile, tile, vec, vec)
    return pl.pallas_call(
        body, name=name, grid=(s // tm,), out_shape=out_shape, in_specs=in_specs, out_specs=out_specs,
        compiler_params=_params(("arbitrary",) if ln is not None else ("parallel",)),
    )(*args)


def _ln_bwd_rows(do, y, g):
    mu = jnp.mean(y, axis=-1, keepdims=True)
    yc = y - mu
    var = jnp.mean(yc * yc, axis=-1, keepdims=True)
    rstd = lax.rsqrt(var + LN_EPS)
    xhat = yc * rstd
    dxhat = do * g
    dy = rstd * (dxhat - jnp.mean(dxhat, axis=-1, keepdims=True) - xhat * jnp.mean(dxhat * xhat, axis=-1, keepdims=True))
    return dy, jnp.sum(do * xhat, axis=0, keepdims=True), jnp.sum(do, axis=0, keepdims=True)


def _accumulate(ref, val, first):
    @pl.when(first)
    def _():
        ref[...] = val

    @pl.when(jnp.logical_not(first))
    def _():
        ref[...] += val


def _layer_norm_rows(y, g, b):
    mu = jnp.mean(y, axis=-1, keepdims=True)
    yc = y - mu
    var = jnp.mean(yc * yc, axis=-1, keepdims=True)
    return yc * lax.rsqrt(var + LN_EPS) * g + b


def _proj_ln(a, w, res, g, b, *, tm, name):
    s, kdim = a.shape

    def body(a_ref, w_ref, res_ref, g_ref, b_ref, pre_ref, obf_ref):
        for rs in _row_parts(tm):
            h = jnp.dot(a_ref[rs, :], w_ref[...], preferred_element_type=F32)
            pre = ALPHA * res_ref[rs, :] + h
            pre_ref[rs, :] = pre
            obf_ref[rs, :] = _layer_norm_rows(pre, g_ref[...], b_ref[...]).astype(BF16)

    row = lambda i: (i, 0)
    fix = lambda i: (0, 0)
    return pl.pallas_call(
        body, name=name, grid=(s // tm,),
        out_shape=(jax.ShapeDtypeStruct((s, D_MODEL), F32), jax.ShapeDtypeStruct((s, D_MODEL), BF16)),
        in_specs=[pl.BlockSpec((tm, kdim), row), pl.BlockSpec((kdim, D_MODEL), fix), pl.BlockSpec((tm, D_MODEL), row),
                  pl.BlockSpec((1, D_MODEL), fix), pl.BlockSpec((1, D_MODEL), fix)],
        out_specs=(pl.BlockSpec((tm, D_MODEL), row),) * 2,
        compiler_params=_params(("parallel",)),
    )(a, w, res, g, b)


def _row_parts(tm):
    sub = min(tm, SUB_ROWS)
    return [pl.ds(r * sub, sub) for r in range(tm // sub)]


def _ffn_in(xbf, w, *, tm, tn, comm=None, name):
    s = xbf.shape[0]
    nj = D_FF // tn
    ni = s // tm

    def body(x_ref, wg_ref, wu_ref, g_ref, u_ref, act_ref):
        for rs in _row_parts(tm):
            xv = x_ref[rs, :]
            gg = jnp.dot(xv, wg_ref[...], preferred_element_type=F32)
            uu = jnp.dot(xv, wu_ref[...], preferred_element_type=F32)
            g_ref[rs, :] = gg.astype(BF16)
            u_ref[rs, :] = uu.astype(BF16)
            act_ref[rs, :] = (gg * _sigmoid(gg) * uu).astype(BF16)

    out = jax.ShapeDtypeStruct((s, D_FF), BF16)
    tile = pl.BlockSpec((tm, tn), lambda j, i: (i, j))
    return _call(
        body, name=name, grid=(nj, ni),
        out_shape=(out, out, out),
        in_specs=[pl.BlockSpec((tm, D_MODEL), lambda j, i: (i, 0)), pl.BlockSpec((D_MODEL, tn), lambda j, i: (0, j)),
                  pl.BlockSpec((D_MODEL, tn), lambda j, i: (0, j + nj))],
        out_specs=(tile, tile, tile), args=(xbf, w, w), sem=("parallel", "parallel"), comm=comm,
        edge=lambda: ((pl.program_id(0) == 0) & (pl.program_id(1) == 0),
                      (pl.program_id(0) == nj - 1) & (pl.program_id(1) == ni - 1)))


def _ffn_bwd_act(dybf, w_out, g, u, *, tm, tn, name):
    s = dybf.shape[0]

    def body(dy_ref, w_ref, g_ref, u_ref, dg_ref, du_ref):
        for rs in _row_parts(tm):
            dact = lax.dot_general(dy_ref[rs, :], w_ref[...], (((1,), (1,)), ((), ())), preferred_element_type=F32)
            gg = g_ref[rs, :].astype(F32)
            uu = u_ref[rs, :].astype(F32)
            sg = _sigmoid(gg)
            dg_ref[rs, :] = (dact * uu * sg * (1.0 + gg * (1.0 - sg))).astype(BF16)
            du_ref[rs, :] = (dact * gg * sg).astype(BF16)

    out = jax.ShapeDtypeStruct((s, D_FF), BF16)
    tile = pl.BlockSpec((tm, tn), lambda j, i: (i, j))
    return pl.pallas_call(
        body, name=name, grid=(D_FF // tn, s // tm),
        out_shape=(out, out),
        in_specs=[pl.BlockSpec((tm, D_MODEL), lambda j, i: (i, 0)), pl.BlockSpec((tn, D_MODEL), lambda j, i: (j, 0)),
                  tile, tile],
        out_specs=(tile, tile),
        compiler_params=_params(("parallel", "parallel")),
    )(dybf, w_out, g, u)


def _ffn_out_ple(act, w_out, res_pre, res_g, res_b, g, b, p, w_gate, w_proj, *, tm, name):
    s = act.shape[0]

    def body(a_ref, w_ref, res_ref, rg_ref, rb_ref, g_ref, b_ref, p_ref, wg_ref, wp_ref, pre_ref, x2bf_ref, o_ref,
             obf_ref):
        for rs in _row_parts(tm):
            res = _layer_norm_rows(res_ref[rs, :], rg_ref[...], rb_ref[...])
            pre = ALPHA * res + jnp.dot(a_ref[rs, :], w_ref[...], preferred_element_type=F32)
            x2 = _layer_norm_rows(pre, g_ref[...], b_ref[...])
            x2bf = x2.astype(BF16)
            pre_ref[rs, :] = pre
            x2bf_ref[rs, :] = x2bf
            gate = jnp.dot(x2bf, wg_ref[...], preferred_element_type=F32)
            pp = jnp.dot(p_ref[rs, :].astype(BF16), wp_ref[...], preferred_element_type=F32)
            out = x2 + _sigmoid(gate) * pp
            o_ref[rs, :] = out
            obf_ref[rs, :] = out.astype(BF16)

    row = lambda i: (i, 0)
    fix = lambda i: (0, 0)
    tile = pl.BlockSpec((tm, D_MODEL), row)
    vec = pl.BlockSpec((1, D_MODEL), fix)
    act_t = lambda dt: jax.ShapeDtypeStruct((s, D_MODEL), dt)
    return pl.pallas_call(
        body, name=name, grid=(s // tm,),
        out_shape=(act_t(F32), act_t(BF16), act_t(F32), act_t(BF16)),
        in_specs=[pl.BlockSpec((tm, D_FF), row), pl.BlockSpec((D_FF, D_MODEL), fix), tile, vec, vec, vec, vec,
                  pl.BlockSpec((tm, PLE_DIM), row), pl.BlockSpec((D_MODEL, D_MODEL), fix),
                  pl.BlockSpec((PLE_DIM, D_MODEL), fix)],
        out_specs=(tile, tile, tile, tile),
        compiler_params=_params(("parallel",)),
    )(act, w_out, res_pre, res_g, res_b, g, b, p, w_gate, w_proj)


def _ple_ln_bwd(dx3, x2bf, p, w_gate, w_proj, pre, g, *, tm, name):
    s = dx3.shape[0]

    def body(d_ref, xbf_ref, p_ref, wg_ref, wp_ref, pre_ref, g_ref, da_ref, dpp_ref, dy_ref, dybf_ref, dg_ref, db_ref):
        pg = jnp.zeros((1, D_MODEL), F32)
        pb = jnp.zeros((1, D_MODEL), F32)
        for rs in _row_parts(tm):
            d = d_ref[rs, :]
            a = jnp.dot(xbf_ref[rs, :], wg_ref[...], preferred_element_type=F32)
            pp = jnp.dot(p_ref[rs, :].astype(BF16), wp_ref[...], preferred_element_type=F32)
            sg = _sigmoid(a)
            da = (d * pp * sg * (1.0 - sg)).astype(BF16)
            da_ref[rs, :] = da
            dpp_ref[rs, :] = (d * sg).astype(BF16)
            dx2 = d + lax.dot_general(da, wg_ref[...], (((1,), (1,)), ((), ())), preferred_element_type=F32)
            dy, qg, qb = _ln_bwd_rows(dx2, pre_ref[rs, :], g_ref[...])
            dy_ref[rs, :] = dy
            dybf_ref[rs, :] = dy.astype(BF16)
            pg, pb = pg + qg, pb + qb
        _accumulate(dg_ref, pg, pl.program_id(0) == 0)
        _accumulate(db_ref, pb, pl.program_id(0) == 0)

    row = lambda i: (i, 0)
    fix = lambda i: (0, 0)
    tile = pl.BlockSpec((tm, D_MODEL), row)
    vec = pl.BlockSpec((1, D_MODEL), fix)
    act = lambda dt: jax.ShapeDtypeStruct((s, D_MODEL), dt)
    return pl.pallas_call(
        body, name=name, grid=(s // tm,),
        out_shape=(act(BF16), act(BF16), act(F32), act(BF16), jax.ShapeDtypeStruct((1, D_MODEL), F32),
                   jax.ShapeDtypeStruct((1, D_MODEL), F32)),
        in_specs=[tile, tile, pl.BlockSpec((tm, PLE_DIM), row), pl.BlockSpec((D_MODEL, D_MODEL), fix),
                  pl.BlockSpec((PLE_DIM, D_MODEL), fix), tile, vec],
        out_specs=(tile, tile, tile, tile, vec, vec),
        compiler_params=_params(("arbitrary",)),
    )(dx3, x2bf, p, w_gate, w_proj, pre, g)


def _loss_head(y, target, *, tm, name):
    s = y.shape[0]

    def body(y_ref, t_ref, dy_ref, loss_ref, acc_ref):
        err = y_ref[...] - t_ref[...]
        dy_ref[...] = err * (1.0 / D_MODEL)
        part = jnp.sum(err * err, axis=0, keepdims=True)

        @pl.when(pl.program_id(0) == 0)
        def _():
            acc_ref[...] = part

        @pl.when(pl.program_id(0) > 0)
        def _():
            acc_ref[...] += part

        @pl.when(pl.program_id(0) == pl.num_programs(0) - 1)
        def _():
            tot = jnp.sum(acc_ref[...], axis=1, keepdims=True) * (0.5 / D_MODEL)
            loss_ref[...] = jnp.broadcast_to(tot, (8, LANES))

    row = lambda i: (i, 0)
    return pl.pallas_call(
        body, name=name, grid=(s // tm,),
        out_shape=(jax.ShapeDtypeStruct((s, D_MODEL), F32), jax.ShapeDtypeStruct((8, LANES), F32)),
        in_specs=[pl.BlockSpec((tm, D_MODEL), row), pl.BlockSpec((tm, D_MODEL), row)],
        out_specs=(pl.BlockSpec((tm, D_MODEL), row), pl.BlockSpec((8, LANES), lambda i: (0, 0))),
        scratch_shapes=[pltpu.VMEM((1, D_MODEL), F32)],
        compiler_params=_params(("arbitrary",)),
    )(y, target)


def _rope_tables(s):
    inv = ROPE_THETA ** (-jnp.arange(0, ROPE_DIM, 2, dtype=F32) / ROPE_DIM)
    ang = jnp.arange(s, dtype=F32)[:, None] * inv[None, :]
    cos, sin = jnp.cos(ang), jnp.sin(ang)
    ones = jnp.ones((s, HEAD_DIM - ROPE_DIM), F32)
    c_head = jnp.concatenate([cos, cos, ones], axis=1)
    s_head = jnp.concatenate([-sin, sin, 0.0 * ones], axis=1)
    return jnp.concatenate([c_head, c_head], axis=1), jnp.concatenate([s_head, s_head], axis=1)


def _rope(v, cos, sin):
    n = v.shape[1] // LANES
    width = v.shape[1]
    cos_w = jnp.tile(cos, (1, n)) if n > 1 else cos
    sin_w = jnp.tile(sin, (1, n)) if n > 1 else sin
    dim = lax.broadcasted_iota(jnp.int32, (1, width), 1) % HEAD_DIM
    partner = jnp.where(dim < ROPE_DIM // 2, pltpu.roll(v, width - ROPE_DIM // 2, 1), pltpu.roll(v, ROPE_DIM // 2, 1))
    return v * cos_w + partner * sin_w


def _unrope(dv, cos, sin):
    n = dv.shape[1] // LANES
    width = dv.shape[1]
    cos_w = jnp.tile(cos, (1, n)) if n > 1 else cos
    sin_w = jnp.tile(sin, (1, n)) if n > 1 else sin
    t = dv * sin_w
    dim = lax.broadcasted_iota(jnp.int32, (1, width), 1) % HEAD_DIM
    partner = jnp.where(dim < ROPE_DIM // 2, pltpu.roll(t, width - ROPE_DIM // 2, 1),
                        jnp.where(dim < ROPE_DIM, pltpu.roll(t, ROPE_DIM // 2, 1), 0.0))
    return dv * cos_w + partner


def _att_mask(i, nb):
    rows = GROUP * ATT_BLOCK
    r = lax.broadcasted_iota(jnp.int32, (rows, 3 * ATT_BLOCK), 0) % ATT_BLOCK
    cidx = lax.broadcasted_iota(jnp.int32, (rows, 3 * ATT_BLOCK), 1)
    rel = r + ATT_BLOCK - cidx
    ok = (rel <= ATT_BLOCK) & (rel >= -ATT_BLOCK)
    ok = ok & ((cidx >= ATT_BLOCK) | (i > 0)) & ((cidx < 2 * ATT_BLOCK) | (i < nb - 1))
    return ok


def _att_mask_t(i, nb):
    cols = GROUP * ATT_BLOCK
    cidx = lax.broadcasted_iota(jnp.int32, (3 * ATT_BLOCK, cols), 0)
    r = lax.broadcasted_iota(jnp.int32, (3 * ATT_BLOCK, cols), 1) % ATT_BLOCK
    rel = r + ATT_BLOCK - cidx
    ok = (rel <= ATT_BLOCK) & (rel >= -ATT_BLOCK)
    return ok & ((cidx >= ATT_BLOCK) | (i > 0)) & ((cidx < 2 * ATT_BLOCK) | (i < nb - 1))


def _sink_lanes(sink_ref, h):
    cols = GROUP * ATT_BLOCK
    grp = lax.broadcasted_iota(jnp.int32, (1, cols), 1) // ATT_BLOCK
    out = jnp.zeros((1, cols), F32)
    for gq in range(GROUP):
        out = jnp.where(grp == gq, sink_ref[GROUP * h + gq], out)
    return out


def _half_mask(half):
    lane = lax.broadcasted_iota(jnp.int32, (1, LANES), 1)
    return (lane // HEAD_DIM) == half


def _stack_q(q, h):
    parts = []
    for gq in range(GROUP):
        n = GROUP * h + gq
        grp = q[:, LANES * (n // 2):LANES * (n // 2 + 1)]
        grp = jnp.where(_half_mask(n % 2), grp, 0.0)
        if n % 2 != h % 2:
            grp = pltpu.roll(grp, HEAD_DIM, 1)
        parts.append(grp)
    return jnp.concatenate(parts, axis=0)


def _unstack_q(stacked, h, acc):
    for gq in range(GROUP):
        n = GROUP * h + gq
        grp = stacked[ATT_BLOCK * gq:ATT_BLOCK * (gq + 1), :]
        grp = jnp.where(_half_mask(h % 2), grp, 0.0)
        if n % 2 != h % 2:
            grp = pltpu.roll(grp, HEAD_DIM, 1)
        acc[n // 2] = grp if acc[n // 2] is None else acc[n // 2] + grp
    return acc


def _sink_rows(sink_ref, h):
    rows = GROUP * ATT_BLOCK
    grp = lax.broadcasted_iota(jnp.int32, (rows, 1), 0) // ATT_BLOCK
    out = jnp.zeros((rows, 1), F32)
    for gq in range(GROUP):
        out = jnp.where(grp == gq, sink_ref[GROUP * h + gq], out)
    return out


def _att_probs(qs, kh, sink, valid):
    s = lax.dot_general(qs, kh, (((1,), (1,)), ((), ())), preferred_element_type=F32)
    s = jnp.where(valid, s, NEG)
    m = jnp.maximum(jnp.max(s, axis=-1, keepdims=True), sink)
    p = jnp.exp(s - m)
    es = jnp.exp(sink - m)
    den = jnp.sum(p, axis=-1, keepdims=True) + es
    inv = 1.0 / den
    return p * inv, es * inv


ATT_STEP_FWD = 2
ATT_STEP_BWD = 4


def _att_specs(nb, step):
    rows = step * ATT_BLOCK
    prev = lambda i: (jnp.maximum(step * i - 1, 0), 0)
    cur = lambda i: (i, 0)
    nxt = lambda i: (jnp.minimum(step * (i + 1), nb - 1), 0)
    kv = lambda f: (lambda i: (f(i)[0], 2))
    shapes = ((rows, cur), (ATT_BLOCK, prev), (rows, cur), (ATT_BLOCK, nxt))
    tab = [pl.BlockSpec((r, LANES), f) for r, f in shapes]
    z = [pl.BlockSpec((rows, D_MODEL), cur)] + [pl.BlockSpec((r, 2 * KV_DIM), kv(f)) for r, f in shapes[1:]]
    return z, tab


def _att_load(zq_ref, kp_ref, kc_ref, kn_ref, cq_ref, sq_ref, cp_ref, sp_ref, cc_ref, sc_ref, cn_ref, sn_ref):
    q = (_rope(zq_ref[...], cq_ref[...], sq_ref[...]) * (HEAD_DIM ** -0.5))
    ks, vs = [], []
    for ref, c_ref, s_ref in ((kp_ref, cp_ref, sp_ref), (kc_ref, cc_ref, sc_ref), (kn_ref, cn_ref, sn_ref)):
        kvb = ref[...]
        ks.append(_rope(kvb[:, :KV_DIM], c_ref[...], s_ref[...]))
        vs.append(kvb[:, KV_DIM:])
    return q, jnp.concatenate(ks, axis=0).astype(BF16), jnp.concatenate(vs, axis=0).astype(BF16)


def _att_rows(sub):
    return (slice(ATT_BLOCK * sub, ATT_BLOCK * (sub + 1)), slice(ATT_BLOCK * sub, ATT_BLOCK * (sub + 3)))


def _att_fwd(z, sink, cos, sin, *, comm=None, name):
    s = z.shape[0]
    nb = s // ATT_BLOCK
    group = min(ATT_STEP_FWD, nb)
    steps = nb // group

    def body(zq_ref, kp_ref, kc_ref, kn_ref, cq_ref, cp_ref, cc_ref, cn_ref, sq_ref, sp_ref, sc_ref, sn_ref, sink_ref,
             o_ref):
        i = pl.program_id(0)
        q, k, v = _att_load(zq_ref, kp_ref, kc_ref, kn_ref, cq_ref, sq_ref, cp_ref, sp_ref, cc_ref, sc_ref, cn_ref, sn_ref)
        for sub in range(group):
            qrows, krows = _att_rows(sub)
            valid = _att_mask(group * i + sub, nb)
            acc = [None] * (N_Q_HEADS // 2)
            for h in range(N_KV_HEADS):
                lanes = slice(LANES * (h // 2), LANES * (h // 2 + 1))
                qs = _stack_q(q[qrows], h).astype(BF16)
                prob, _ = _att_probs(qs, k[krows, lanes], _sink_rows(sink_ref, h), valid)
                oh = jnp.dot(prob.astype(BF16), v[krows, lanes], preferred_element_type=F32)
                acc = _unstack_q(oh, h, acc)
            o_ref[qrows, :] = jnp.concatenate(acc, axis=1).astype(BF16)

    zspecs, tab = _att_specs(nb, group)
    return _call(
        body, name=name, grid=(steps,),
        out_shape=(jax.ShapeDtypeStruct((s, D_MODEL), BF16),),
        in_specs=zspecs + tab + tab + [pl.BlockSpec(memory_space=pltpu.SMEM)],
        out_specs=(pl.BlockSpec((group * ATT_BLOCK, D_MODEL), lambda i: (i, 0)),),
        args=(z, z, z, z, cos, cos, cos, cos, sin, sin, sin, sin, sink), sem=("parallel",), comm=comm,
        edge=lambda: (pl.program_id(0) == 0, pl.program_id(0) == steps - 1))


def _att_bwd(z, do, sink, cos, sin, *, comm=None, name):
    s = z.shape[0]
    nb = s // ATT_BLOCK
    group = min(ATT_STEP_BWD, nb)
    steps = nb // group

    def body(zq_ref, kp_ref, kc_ref, kn_ref, cq_ref, cp_ref, cc_ref, cn_ref, sq_ref, sp_ref, sc_ref, sn_ref, sink_ref,
             do_ref, dq_ref, part_ref, dsink_ref):
        i = pl.program_id(0)
        q, k, v = _att_load(zq_ref, kp_ref, kc_ref, kn_ref, cq_ref, sq_ref, cp_ref, sp_ref, cc_ref, sc_ref, cn_ref, sn_ref)
        dsink = None
        window = [None] * (group + 2)
        nt = (((1,), (1,)), ((), ()))
        for sub in range(group):
            qrows, krows = _att_rows(sub)
            valid = _att_mask_t(group * i + sub, nb)
            dout = do_ref[qrows, :].astype(F32)
            dq_acc = [None] * (N_Q_HEADS // 2)
            dk_acc = [None] * 2
            dv_acc = [None] * 2
            rows = []
            for h in range(N_KV_HEADS):
                grp = h // 2
                lanes = slice(LANES * grp, LANES * (grp + 1))
                kh, vh = k[krows, lanes], v[krows, lanes]
                qs = _stack_q(q[qrows], h).astype(BF16)
                dos = _stack_q(dout, h).astype(BF16)
                sink = _sink_lanes(sink_ref, h)
                sc = jnp.where(valid, lax.dot_general(kh, qs, nt, preferred_element_type=F32), NEG)
                m = jnp.maximum(jnp.max(sc, axis=0, keepdims=True), sink)
                p = jnp.exp(sc - m)
                es = jnp.exp(sink - m)
                inv = 1.0 / (jnp.sum(p, axis=0, keepdims=True) + es)
                prob = p * inv
                dprob = lax.dot_general(vh, dos, nt, preferred_element_type=F32)
                delta = jnp.sum(prob * dprob, axis=0, keepdims=True)
                dsc = (prob * (dprob - delta)).astype(BF16)
                dsk = -(es * inv) * delta
                for gq in range(GROUP):
                    tot = jnp.sum(dsk[:, ATT_BLOCK * gq:ATT_BLOCK * (gq + 1)], axis=1, keepdims=True)
                    rows.append(jnp.broadcast_to(tot, (1, LANES)))
                dqs = lax.dot_general(dsc, kh, (((0,), (0,)), ((), ())), preferred_element_type=F32)
                dq_acc = _unstack_q(dqs, h, dq_acc)
                dkh = jnp.dot(dsc, qs, preferred_element_type=F32)
                dvh = jnp.dot(prob.astype(BF16), dos, preferred_element_type=F32)
                dk_acc[grp] = dkh if dk_acc[grp] is None else dk_acc[grp] + dkh
                dv_acc[grp] = dvh if dv_acc[grp] is None else dv_acc[grp] + dvh
            dq = jnp.concatenate(dq_acc, axis=1) * (HEAD_DIM ** -0.5)
            dq_ref[qrows, :] = _unrope(dq, cq_ref[qrows, :], sq_ref[qrows, :]).astype(BF16)
            part = jnp.concatenate(dk_acc + dv_acc, axis=1)
            for wdw in range(3):
                piece = part[ATT_BLOCK * wdw:ATT_BLOCK * (wdw + 1), :]
                window[sub + wdw] = piece if window[sub + wdw] is None else window[sub + wdw] + piece
            mine = jnp.concatenate(rows, axis=0)
            dsink = mine if dsink is None else dsink + mine
        for slot, piece in enumerate(window):
            part_ref[slot] = piece
        _accumulate(dsink_ref, dsink, i == 0)

    zspecs, tab = _att_specs(nb, group)
    tile = pl.BlockSpec((group * ATT_BLOCK, D_MODEL), lambda i: (i, 0))
    return _call(
        body, name=name, grid=(steps,),
        out_shape=(jax.ShapeDtypeStruct((s, D_MODEL), BF16),
                   jax.ShapeDtypeStruct((steps, group + 2, ATT_BLOCK, 2 * KV_DIM), F32),
                   jax.ShapeDtypeStruct((N_Q_HEADS, LANES), F32)),
        in_specs=zspecs + tab + tab + [pl.BlockSpec(memory_space=pltpu.SMEM), tile],
        out_specs=(tile, pl.BlockSpec((None, group + 2, ATT_BLOCK, 2 * KV_DIM), lambda i: (i, 0, 0, 0)),
                   pl.BlockSpec((N_Q_HEADS, LANES), lambda i: (0, 0))),
        args=(z, z, z, z, cos, cos, cos, cos, sin, sin, sin, sin, sink, do), sem=("arbitrary",), comm=comm,
        edge=lambda: (pl.program_id(0) == 0, pl.program_id(0) == steps - 1))


def _att_bwd_kv(part, cos, sin, *, name):
    steps, group = part.shape[0], part.shape[1] - 2
    rows = group * ATT_BLOCK

    def body(own_ref, before_ref, after_ref, c_ref, s_ref, o_ref):
        g = pl.program_id(0)
        for r in range(group):
            tot = own_ref[r + 1]
            if r == 0:
                tot = tot + jnp.where(g > 0, before_ref[...], 0.0)
            if r == group - 1:
                tot = tot + jnp.where(g < steps - 1, after_ref[...], 0.0)
            rs = slice(ATT_BLOCK * r, ATT_BLOCK * (r + 1))
            dk = _unrope(tot[:, :KV_DIM], c_ref[rs, :], s_ref[rs, :])
            o_ref[rs, :] = jnp.concatenate([dk, tot[:, KV_DIM:]], axis=1).astype(BF16)

    one = (None, None, ATT_BLOCK, 2 * KV_DIM)
    return pl.pallas_call(
        body, name=name, grid=(steps,),
        out_shape=jax.ShapeDtypeStruct((steps * rows, 2 * KV_DIM), BF16),
        in_specs=[pl.BlockSpec((None, group + 2, ATT_BLOCK, 2 * KV_DIM), lambda g: (g, 0, 0, 0)),
                  pl.BlockSpec(one, lambda g: (jnp.maximum(g - 1, 0), group + 1, 0, 0)),
                  pl.BlockSpec(one, lambda g: (jnp.minimum(g + 1, steps - 1), 0, 0, 0)),
                  pl.BlockSpec((rows, LANES), lambda g: (g, 0)), pl.BlockSpec((rows, LANES), lambda g: (g, 0))],
        out_specs=pl.BlockSpec((rows, 2 * KV_DIM), lambda g: (g, 0)),
        compiler_params=_params(("parallel",)),
    )(part, part, part, cos, sin)


def _bdot(a, b, dims):
    return lax.dot_general(a.astype(BF16), b.astype(BF16), (dims, ((), ())), preferred_element_type=F32)


@jax.custom_vjp
def _dot_nn(a, b):
    return _bdot(a, b, ((1,), (0,)))


@jax.custom_vjp
def _dot_nt(a, b):
    return _bdot(a, b, ((1,), (1,)))


@jax.custom_vjp
def _dot_tn(a, b):
    return _bdot(a, b, ((0,), (0,)))


_dot_nn.defvjp(lambda a, b: (_dot_nn(a, b), (a, b)), lambda r, d: (_dot_nt(d, r[1]), _dot_tn(r[0], d)))
_dot_nt.defvjp(lambda a, b: (_dot_nt(a, b), (a, b)), lambda r, d: (_dot_nn(d, r[1]), _dot_tn(d, r[0])))
_dot_tn.defvjp(lambda a, b: (_dot_tn(a, b), (a, b)), lambda r, d: (_dot_nt(r[1], d), _dot_nn(r[0], d)))


def _running_sum(v, up):
    n = v.shape[0]
    rows = lax.broadcasted_iota(jnp.int32, v.shape, 0)
    sh = 1
    while sh < n:
        if up:
            v = v + jnp.where(rows < n - sh, pltpu.roll(v, n - sh, 0), 0.0)
        else:
            v = v + jnp.where(rows >= sh, pltpu.roll(v, sh, 0), 0.0)
        sh *= 2
    return v


@jax.custom_vjp
def _sum_down(v):
    return _running_sum(v, False)


@jax.custom_vjp
def _sum_up(v):
    return _running_sum(v, True)


_sum_down.defvjp(lambda v: (_running_sum(v, False), None), lambda _, d: (_sum_up(d),))
_sum_up.defvjp(lambda v: (_running_sum(v, True), None), lambda _, d: (_sum_down(d),))

N_SUB = HG_CHUNK // HG_SUB


def _fold_blocks(v):
    out = v[:HG_CHUNK]
    for i in range(1, N_SUB):
        out = out + v[HG_CHUNK * i:HG_CHUNK * (i + 1)]
    return out


@jax.custom_vjp
def _fold(v):
    return _fold_blocks(v)


_fold.defvjp(lambda v: (_fold_blocks(v), None), lambda _, d: (jnp.concatenate([d] * N_SUB, axis=0),))


def _hg_consts(rev):
    c, sub = HG_CHUNK, HG_SUB
    rowpos = lax.broadcasted_iota(jnp.int32, (c, HG_DIM), 0)
    rr = lax.broadcasted_iota(jnp.int32, (N_SUB * c, c), 0)
    key = lax.broadcasted_iota(jnp.int32, (N_SUB * c, c), 1)
    blk, qry = rr // c, rr % c
    if rev:
        rowpos, qry, key = c - 1 - rowpos, c - 1 - qry, c - 1 - key
    keep = (key // sub == blk) & (key <= qry)
    return keep, rowpos


def _pick(b, rowpos, t):
    return jnp.sum(jnp.where(rowpos == t, b, 0.0), axis=0, keepdims=True)


def _hg_local(zq, zf, zv, lbv, consts, dots):
    dot_nn, dot_nt, dot_tn, cum, fold = dots
    keep, rowpos = consts
    sig = _sigmoid(zf)
    f = lbv + (1.0 - lbv) * sig
    g = jnp.log(f)
    k = (1.0 - lbv) * (1.0 - sig)
    q = zq * _sigmoid(zq)
    b = cum(g)
    ends = [_pick(b, rowpos, (j + 1) * HG_SUB - 1) for j in range(N_SUB)]
    b_last = ends[-1]
    b_end = b_last
    for j in range(N_SUB - 1):
        b_end = jnp.where(rowpos // HG_SUB == j, ends[j], b_end)
    kc = k * jnp.exp(b_end - b)
    qbs = [q * jnp.exp(jnp.where(rowpos >= j * HG_SUB, b - ends[j], 0.0)) for j in range(N_SUB)]
    scores = fold(jnp.where(keep, dot_nt(jnp.concatenate(qbs, axis=0), kc), 0.0))
    return dot_nn(scores, zv), q * jnp.exp(b), k * jnp.exp(b_last - b), jnp.exp(b_last)


def _hg_chunk(zq, zf, zv, lbv, st, consts, dots):
    intra, qs, kd, dec = _hg_local(zq, zf, zv, lbv, consts, dots)
    return intra + dots[1](qs, st), dec * st + dots[2](zv, kd)


def _hg_dots(diff, rev):
    if diff:
        return _dot_nn, _dot_nt, _dot_tn, (_sum_up if rev else _sum_down), _fold
    return (lambda a, b: _bdot(a, b, ((1,), (0,))), lambda a, b: _bdot(a, b, ((1,), (1,))),
            lambda a, b: _bdot(a, b, ((0,), (0,))), lambda v: _running_sum(v, rev), _fold_blocks)


def _hg_specs(ts, nch, trow):
    tile = pl.BlockSpec((ts, HG_DIM), lambda h, t: (trow(t), h))
    mats = pl.BlockSpec((None, nch, HG_DIM, HG_DIM), lambda h, t: (h, trow(t), 0, 0))
    vecs = pl.BlockSpec((None, nch, 1, HG_DIM), lambda h, t: (h, trow(t), 0, 0))
    return tile, mats, vecs


def _time_order(nch, rev):
    return range(nch - 1, -1, -1) if rev else range(nch)


def _chunk_rows(c):
    return pl.ds(c * HG_CHUNK, HG_CHUNK)


def _hg_edge(nt):
    h, t = pl.program_id(0), pl.program_id(1)
    return (h == 0) & (t == 0), (h == HG_HEADS - 1) & (t == nt - 1)


def _hg_fwd(z, lb, *, rev, ts, comm=None, name):
    s = z.shape[0]
    nt = s // ts
    nch = ts // HG_CHUNK
    fcol = HG_HEADS * (2 if rev else 1)

    def body(zq_ref, zf_ref, zv_ref, lb_ref, o_ref, st_ref, qs_ref, dec_ref, state_ref):
        @pl.when(pl.program_id(1) == 0)
        def _():
            state_ref[...] = jnp.zeros_like(state_ref)

        consts = _hg_consts(rev)
        dots = _hg_dots(False, rev)
        lbv = lb_ref[...]
        local = {}
        for c in range(nch):
            rows = _chunk_rows(c)
            zv = zv_ref[rows, :]
            intra, qs, kd, dec = _hg_local(zq_ref[rows, :], zf_ref[rows, :], zv, lbv, consts, dots)
            qs = qs.astype(BF16)
            qs_ref[rows, :] = qs
            dec_ref[c] = dec
            local[c] = (intra, qs, dec, dots[2](zv, kd))
        st = state_ref[...]
        for c in _time_order(nch, rev):
            intra, qs, dec, upd = local[c]
            st_ref[c] = st.astype(BF16)
            o_ref[_chunk_rows(c), :] = intra + _bdot(qs, st, ((1,), (1,)))
            st = dec * st + upd
        state_ref[...] = st

    trow = (lambda t: nt - 1 - t) if rev else (lambda t: t)
    col = lambda off: pl.BlockSpec((ts, HG_DIM), lambda h, t: (trow(t), off + h))
    tile, mats, vecs = _hg_specs(ts, nch, trow)
    nchunks = s // HG_CHUNK
    return _call(
        body, name=name, grid=(HG_HEADS, nt),
        out_shape=(jax.ShapeDtypeStruct((s, D_MODEL), F32),
                   jax.ShapeDtypeStruct((HG_HEADS, nchunks, HG_DIM, HG_DIM), BF16),
                   jax.ShapeDtypeStruct((s, D_MODEL), BF16),
                   jax.ShapeDtypeStruct((HG_HEADS, nchunks, 1, HG_DIM), F32)),
        in_specs=[col(0), col(fcol), col(3 * HG_HEADS), pl.BlockSpec((None, 1, HG_DIM), lambda h, t: (h, 0, 0))],
        out_specs=(tile, mats, tile, vecs), args=(z, z, z, lb),
        scratch_shapes=[pltpu.VMEM((HG_DIM, HG_DIM), F32)], sem=("parallel", "arbitrary"), comm=comm,
        edge=lambda: _hg_edge(nt))


def _hg_bwd(z, lb, states, qs, dec, dout, addq, addv, *, rev, ts, comm=None, name):
    s = z.shape[0]
    nt = s // ts
    nch = ts // HG_CHUNK
    fcol = HG_HEADS * (2 if rev else 1)
    has_add = addq is not None

    def body(*refs):
        zq_ref, zf_ref, zv_ref, lb_ref, st_ref, qs_ref, dec_ref, do_ref = refs[:8]
        aq_ref, av_ref = (refs[8], refs[9]) if has_add else (None, None)
        dq_ref, df_ref, dv_ref, dlb_ref, grad_ref = refs[-5:]

        @pl.when(pl.program_id(1) == 0)
        def _():
            grad_ref[...] = jnp.zeros_like(grad_ref)

        consts = _hg_consts(rev)
        dots = _hg_dots(True, rev)
        lbv = lb_ref[...]
        prods = {c: _bdot(do_ref[_chunk_rows(c), :], qs_ref[_chunk_rows(c), :], ((0,), (0,))) for c in range(nch)}
        gleave = {}
        gr = grad_ref[...]
        for c in reversed(_time_order(nch, rev)):
            gleave[c] = gr
            gr = dec_ref[c] * gr + prods[c]
        grad_ref[...] = gr
        dlb_blk = jnp.zeros((1, HG_DIM), F32)
        for c in range(nch):
            rows = _chunk_rows(c)
            fn = lambda a, b2, c2, d2, e2: _hg_chunk(a, b2, c2, d2, e2, consts, dots)
            _, pull = jax.vjp(fn, zq_ref[rows, :], zf_ref[rows, :], zv_ref[rows, :], lbv, st_ref[c].astype(F32))
            dq, df, dv, dlb, _ = pull((do_ref[rows, :], gleave[c]))
            if has_add:
                dq = dq + aq_ref[rows, :]
                dv = dv + av_ref[rows, :]
            dq_ref[rows, :] = dq.astype(dq_ref.dtype)
            df_ref[rows, :] = df.astype(BF16)
            dv_ref[rows, :] = dv.astype(dv_ref.dtype)
            dlb_blk = dlb_blk + dlb

        @pl.when(pl.program_id(1) == 0)
        def _():
            dlb_ref[...] = dlb_blk

        @pl.when(pl.program_id(1) > 0)
        def _():
            dlb_ref[...] += dlb_blk

    trow = (lambda t: t) if rev else (lambda t: nt - 1 - t)
    col = lambda off: pl.BlockSpec((ts, HG_DIM), lambda h, t: (trow(t), off + h))
    tile, mats, vecs = _hg_specs(ts, nch, trow)
    in_specs = [col(0), col(fcol), col(3 * HG_HEADS), pl.BlockSpec((None, 1, HG_DIM), lambda h, t: (h, 0, 0)),
                mats, tile, vecs, tile]
    args = [z, z, z, lb, states, qs, dec, dout]
    if has_add:
        in_specs += [tile, tile]
        args += [addq, addv]
    act = lambda dt: jax.ShapeDtypeStruct((s, D_MODEL), dt)
    sums = BF16 if has_add else F32
    return _call(
        body, name=name, grid=(HG_HEADS, nt),
        out_shape=(act(sums), act(BF16), act(sums), jax.ShapeDtypeStruct((HG_HEADS, 1, HG_DIM), F32)),
        in_specs=in_specs,
        out_specs=(tile, tile, tile, pl.BlockSpec((None, 1, HG_DIM), lambda h, t: (h, 0, 0))), args=tuple(args),
        scratch_shapes=[pltpu.VMEM((HG_DIM, HG_DIM), F32)], sem=("parallel", "arbitrary"), comm=comm,
        edge=lambda: _hg_edge(nt))


def _hg_post(of, ob, z, norm_g, *, tm, name):
    s = of.shape[0]

    def body(of_ref, ob_ref, gate_ref, ng_ref, y_ref):
        gn = ng_ref[...]
        for h in range(HG_HEADS):
            ln = slice(HG_DIM * h, HG_DIM * (h + 1))
            o = of_ref[:, ln] + ob_ref[:, ln]
            r = lax.rsqrt(jnp.mean(o * o, axis=-1, keepdims=True) + LN_EPS)
            gt = gate_ref[:, ln]
            y_ref[:, ln] = (o * r * gn * gt * _sigmoid(gt)).astype(BF16)

    row = lambda i: (i, 0)
    return pl.pallas_call(
        body, name=name, grid=(s // tm,),
        out_shape=jax.ShapeDtypeStruct((s, D_MODEL), BF16),
        in_specs=[pl.BlockSpec((tm, D_MODEL), row), pl.BlockSpec((tm, D_MODEL), row),
                  pl.BlockSpec((tm, D_MODEL), lambda i: (i, 4)), pl.BlockSpec((1, HG_DIM), lambda i: (0, 0))],
        out_specs=pl.BlockSpec((tm, D_MODEL), row),
        compiler_params=_params(("parallel",)),
    )(of, ob, z, norm_g)


def _hg_post_bwd(dy, of, ob, z, norm_g, *, tm, name):
    s = of.shape[0]

    def body(dy_ref, of_ref, ob_ref, gate_ref, ng_ref, do_ref, dgate_ref, dng_ref):
        gn = ng_ref[...]
        tot = jnp.zeros((1, HG_DIM), F32)
        for h in range(HG_HEADS):
            ln = slice(HG_DIM * h, HG_DIM * (h + 1))
            d = dy_ref[:, ln].astype(F32)
            o = of_ref[:, ln] + ob_ref[:, ln]
            r = lax.rsqrt(jnp.mean(o * o, axis=-1, keepdims=True) + LN_EPS)
            ohat = o * r
            gt = gate_ref[:, ln]
            sg = _sigmoid(gt)
            don = d * gt * sg
            dgate_ref[:, ln] = (d * ohat * gn * sg * (1.0 + gt * (1.0 - sg))).astype(BF16)
            tot = tot + jnp.sum(don * ohat, axis=0, keepdims=True)
            dohat = don * gn
            do_ref[:, ln] = r * (dohat - ohat * jnp.mean(dohat * ohat, axis=-1, keepdims=True))

        @pl.when(pl.program_id(0) == 0)
        def _():
            dng_ref[...] = tot

        @pl.when(pl.program_id(0) > 0)
        def _():
            dng_ref[...] += tot

    row = lambda i: (i, 0)
    return pl.pallas_call(
        body, name=name, grid=(s // tm,),
        out_shape=(jax.ShapeDtypeStruct((s, D_MODEL), F32), jax.ShapeDtypeStruct((s, D_MODEL), BF16),
                   jax.ShapeDtypeStruct((1, HG_DIM), F32)),
        in_specs=[pl.BlockSpec((tm, D_MODEL), row), pl.BlockSpec((tm, D_MODEL), row), pl.BlockSpec((tm, D_MODEL), row),
                  pl.BlockSpec((tm, D_MODEL), lambda i: (i, 4)), pl.BlockSpec((1, HG_DIM), lambda i: (0, 0))],
        out_specs=(pl.BlockSpec((tm, D_MODEL), row), pl.BlockSpec((tm, D_MODEL), row),
                   pl.BlockSpec((1, HG_DIM), lambda i: (0, 0))),
        compiler_params=_params(("arbitrary",)),
    )(dy, of, ob, z, norm_g)


def _lb_fwd(logits, *, name):
    w = logits.shape[1]

    def body(l_ref, o_ref):
        lg = l_ref[...]
        e = jnp.exp(lg - jnp.max(lg, axis=0, keepdims=True))
        sm = e / jnp.sum(e, axis=0, keepdims=True)
        o_ref[0:1, :] = sm[1:2]
        o_ref[1:2, :] = sm[1:2] + sm[2:3] + sm[3:4]

    return pl.pallas_call(body, name=name, out_shape=jax.ShapeDtypeStruct((2, w), F32))(logits)


def _lb_bwd(logits, dlb, *, name):
    w = logits.shape[1]

    def body(l_ref, d_ref, o_ref):
        lg = l_ref[...]
        e = jnp.exp(lg - jnp.max(lg, axis=0, keepdims=True))
        sm = e / jnp.sum(e, axis=0, keepdims=True)
        d1, d3 = d_ref[0:1, :], d_ref[1:2, :]
        dot = sm[1:2] * (d1 + d3) + (sm[2:3] + sm[3:4]) * d3
        o_ref[0:1, :] = -sm[0:1] * dot
        o_ref[1:2, :] = sm[1:2] * (d1 + d3 - dot)
        o_ref[2:3, :] = sm[2:3] * (d3 - dot)
        o_ref[3:4, :] = sm[3:4] * (d3 - dot)

    return pl.pallas_call(body, name=name, out_shape=jax.ShapeDtypeStruct((4, w), F32))(logits, dlb)


def _adamw(w, g, m, v, *, tr, g_off=0, name):
    rows = w.shape[0]
    parts = g.ndim == 3
    c1 = 1.0 / (1.0 - ADAM_B1 ** ADAM_STEP)
    c2 = 1.0 / (1.0 - ADAM_B2 ** ADAM_STEP)

    def body(w_ref, g_ref, m_ref, v_ref, go_ref, d_ref, mo_ref, vo_ref):
        if parts:
            gg = g_ref[0].astype(F32)
            for i in range(1, N_DEV):
                gg = gg + g_ref[i].astype(F32)
        else:
            gg = g_ref[...]
        mm = ADAM_B1 * m_ref[...] + (1.0 - ADAM_B1) * gg
        vv = ADAM_B2 * v_ref[...] + (1.0 - ADAM_B2) * (gg * gg)
        go_ref[...] = gg
        mo_ref[...] = mm
        vo_ref[...] = vv
        d_ref[...] = -ADAM_LR * ((mm * c1) / (jnp.sqrt(vv * c2) + ADAM_EPS) + ADAM_WD * w_ref[...])

    tile = pl.BlockSpec((tr, D_MODEL), lambda i: (i, 0))
    gspec = pl.BlockSpec((N_DEV, tr, D_MODEL), lambda i: (0, i + g_off // tr, 0)) if parts else tile
    out = jax.ShapeDtypeStruct((rows, D_MODEL), F32)
    return pl.pallas_call(
        body, name=name, grid=(rows // tr,),
        out_shape=(out, out, out, out),
        in_specs=[tile, gspec, tile, tile], out_specs=(tile, tile, tile, tile),
        compiler_params=_params(("parallel",)),
    )(w, g, m, v)


def _sum8(parts, *, name):
    def body(p_ref, o_ref):
        tot = p_ref[0]
        for i in range(1, N_DEV):
            tot = tot + p_ref[i]
        o_ref[...] = tot

    return pl.pallas_call(body, name=name, out_shape=jax.ShapeDtypeStruct(parts.shape[1:], parts.dtype))(parts)


def _layer_params(i):
    j = i // 2
    mix = [("att_w_qkv", j, 1), ("att_w_o", j, 0)] if i % 2 == 0 else [("hgrn_w_in", j, 1), ("hgrn_w_o", j, 0)]
    return mix + [("ffn_w_in", i, 1), ("ffn_w_out", i, 0), ("ple_w_gate", i, 0), ("ple_w_proj", i, 1)]


def _pack_local(tree, params):
    return jnp.concatenate([tree[n][j].reshape(-1, D_MODEL) for n, j, _ in params], axis=0)


def _unpack_local(packed, params, like):
    out, r = {}, 0
    for n, _, _ in params:
        shp = like[n].shape[1:]
        k = shp[0] * shp[1] // D_MODEL
        out[n] = packed[r:r + k].reshape(shp)
        r += k
    return out


def _unpack_gathered(gathered, params, like):
    out, r = {}, 0
    for n, _, ax in params:
        shp = like[n].shape[1:]
        k = shp[0] * shp[1] // D_MODEL
        t = gathered[:, r:r + k].reshape((N_DEV,) + shp)
        out[n] = (jnp.moveaxis(t, 0, 1).reshape(shp[0], N_DEV * shp[1]) if ax == 1
                  else t.reshape(N_DEV * shp[0], shp[1]))
        r += k
    return out


def _pack_full(grads, params, like):
    cols = []
    for n, _, ax in params:
        shp = like[n].shape[1:]
        t = (jnp.moveaxis(grads[n].reshape(shp[0], N_DEV, shp[1]), 1, 0) if ax == 1
             else grads[n].reshape(N_DEV, shp[0], shp[1]))
        cols.append(t.reshape(N_DEV, -1, D_MODEL).astype(BF16))
    return jnp.concatenate(cols, axis=1)


def _row_tile(rows):
    return max(t for t in range(16, 257, 16) if rows % t == 0)


SMALL_ROWS = 24


def _pad_row(a):
    flat = a.reshape(1, -1)
    return jnp.pad(flat, ((0, 0), (0, D_MODEL - flat.shape[1])))


def _tile(n, pref):
    return min(n, pref)


def kernel(x, p, att_w_qkv, att_sink, att_w_o, hgrn_w_in, hgrn_lb_logits, hgrn_norm_g, hgrn_w_o, ln_mix_g, ln_mix_b, ffn_w_in, ffn_w_out, ln_ffn_g, ln_ffn_b, ple_w_gate, ple_w_proj, loss_target, m_att_w_qkv, m_att_sink, m_att_w_o, m_hgrn_w_in, m_hgrn_lb_logits, m_hgrn_norm_g, m_hgrn_w_o, m_ln_mix_g, m_ln_mix_b, m_ffn_w_in, m_ffn_w_out, m_ln_ffn_g, m_ln_ffn_b, m_ple_w_gate, m_ple_w_proj, v_att_w_qkv, v_att_sink, v_att_w_o, v_hgrn_w_in, v_hgrn_lb_logits, v_hgrn_norm_g, v_hgrn_w_o, v_ln_mix_g, v_ln_mix_b, v_ffn_w_in, v_ffn_w_out, v_ln_ffn_g, v_ln_ffn_b, v_ple_w_gate, v_ple_w_proj):
    names = ["att_w_qkv", "att_sink", "att_w_o", "hgrn_w_in", "hgrn_lb_logits", "hgrn_norm_g", "hgrn_w_o", "ln_mix_g",
             "ln_mix_b", "ffn_w_in", "ffn_w_out", "ln_ffn_g", "ln_ffn_b", "ple_w_gate", "ple_w_proj"]
    w = dict(zip(names, (att_w_qkv, att_sink, att_w_o, hgrn_w_in, hgrn_lb_logits, hgrn_norm_g, hgrn_w_o, ln_mix_g,
                         ln_mix_b, ffn_w_in, ffn_w_out, ln_ffn_g, ln_ffn_b, ple_w_gate, ple_w_proj)))
    mom = dict(zip(names, (m_att_w_qkv, m_att_sink, m_att_w_o, m_hgrn_w_in, m_hgrn_lb_logits, m_hgrn_norm_g, m_hgrn_w_o,
                           m_ln_mix_g, m_ln_mix_b, m_ffn_w_in, m_ffn_w_out, m_ln_ffn_g, m_ln_ffn_b, m_ple_w_gate,
                           m_ple_w_proj)))
    var = dict(zip(names, (v_att_w_qkv, v_att_sink, v_att_w_o, v_hgrn_w_in, v_hgrn_lb_logits, v_hgrn_norm_g, v_hgrn_w_o,
                           v_ln_mix_g, v_ln_mix_b, v_ffn_w_in, v_ffn_w_out, v_ln_ffn_g, v_ln_ffn_b, v_ple_w_gate,
                           v_ple_w_proj)))
    s = x.shape[1]
    me = 4 * lax.axis_index("x") + 2 * lax.axis_index("y") + lax.axis_index("c")
    tm = _tile(s, 512)
    tbig = _tile(s, 1024)
    ts = _tile(s // 2, 2048)
    x0 = x.reshape(s, D_MODEL)
    target = loss_target.reshape(s, D_MODEL)
    pl_in = p.reshape(DEPTH, s, PLE_DIM)

    full =_unpack_gathered(_gather(_pack_local(w, _layer_params(0)).astype(BF16), name="gather_weights"),
                            _layer_params(0), w)
    lb_rows = jnp.pad(hgrn_lb_logits.reshape(8, HG_DIM), ((0, 0), (0, D_MODEL - HG_DIM)))
    lb_all = _gather(lb_rows, name="gather_lb")[:, :, :HG_DIM]
    logits_full = jnp.moveaxis(lb_all, 0, 1).reshape(DEPTH, 2 * D_MODEL)
    lb = _lb_fwd(logits_full, name="lb_fwd")
    cos, sin = _rope_tables(s)

    saved = []
    xf, xb = x0, x0
    for i in range(DEPTH):
        j = i // 2
        sv = {"xb": xb, "w": full}
        nxt = nxt2 = None
        if i + 1 < DEPTH:
            ahead = _layer_params(i + 1)
            n_first = len(ahead) if i % 2 == 0 else 3
            nxt = (_pack_local(w, ahead[:n_first]).astype(BF16), True)
            if n_first < len(ahead):
                nxt2 = (_pack_local(w, ahead[n_first:]).astype(BF16), True)
        if i % 2 == 0:
            z = _mm(xb, full["att_w_qkv"], tm=tbig, tn=512, tk=D_MODEL, name="att_in")
            o, *more = _att_fwd(z, w["att_sink"][j], cos, sin, comm=nxt, name="att_fwd")
            w_o = full["att_w_o"]
        else:
            z = _mm(xb, full["hgrn_w_in"], tm=tbig, tn=1024, tk=D_MODEL, name="hgrn_in")
            lbl = lb[j].reshape(2, HG_HEADS, 1, HG_DIM)
            of, st_f, qs_f, dec_f, *more = _hg_fwd(z, lbl[0], rev=False, ts=ts, comm=nxt, name="hgrn_fwd")
            ob, st_b, qs_b, dec_b = _hg_fwd(z, lbl[1], rev=True, ts=ts, name="hgrn_fwd_rev")
            o = _hg_post(of, ob, z, w["hgrn_norm_g"][j].reshape(1, HG_DIM), tm=tm, name="hgrn_post")
            w_o = full["hgrn_w_o"]
            sv.update(of=of, ob=ob, st_f=st_f, st_b=st_b, lbl=lbl, qs_f=qs_f, qs_b=qs_b, dec_f=dec_f, dec_b=dec_b)
        sv.update(z=z, o=o)
        g1, b1 = w["ln_mix_g"][i:i + 1], w["ln_mix_b"][i:i + 1]
        pre1, x1b = _proj_ln(o, w_o, xf, g1, b1, tm=tm, name="mix_out_ln")
        gg, uu, act, *more2 = _ffn_in(x1b, full["ffn_w_in"], tm=tm, tn=FF_TILE, comm=nxt2, name="ffn_in")
        pre2, x2b, xf, xb = _ffn_out_ple(act, full["ffn_w_out"], pre1, g1, b1, w["ln_ffn_g"][i:i + 1],
                                         w["ln_ffn_b"][i:i + 1], pl_in[i], full["ple_w_gate"], full["ple_w_proj"], tm=tm,
                                         name="ffn_out_ple")
        sv.update(pre1=pre1, x1b=x1b, g=gg, u=uu, act=act, pre2=pre2, x2b=x2b)
        saved.append(sv)
        if nxt is not None:
            full = _unpack_gathered(more[0], ahead[:n_first], w)
        if nxt2 is not None:
            full.update(_unpack_gathered(more2[0], ahead[n_first:], w))

    dx, loss_blk = _loss_head(xf, target, tm=tm, name="loss_head")
    loss = lax.psum(loss_blk[0, 0], AXES)

    small = {n: [None] * DEPTH for n in ("ln_mix_g", "ln_mix_b", "ln_ffn_g", "ln_ffn_b")}
    dlb_rows = [None] * 4
    dnorm, dsink = [None] * 2, [None] * 2
    recv_late, recv_early = [None] * DEPTH, [None] * DEPTH
    above = None
    mmw = functools.partial(_mm, ta=True, tk=_tile(s, 2048), out_dtype=BF16)
    for i in reversed(range(DEPTH)):
        j = i // 2
        sv = saved[i]
        full, gl = sv["w"], {}
        da, dpp, dy2, dy2b, small["ln_ffn_g"][i], small["ln_ffn_b"][i] = _ple_ln_bwd(
            dx, sv["x2b"], pl_in[i], full["ple_w_gate"], full["ple_w_proj"], sv["pre2"], w["ln_ffn_g"][i:i + 1],
            tm=tm, name="ple_ln_bwd")
        gl["ple_w_gate"] = mmw(sv["x2b"], da, tm=D_MODEL, tn=D_MODEL, name="dw_ple_gate")
        gl["ple_w_proj"] = mmw(pl_in[i], dpp, tm=PLE_DIM, tn=D_MODEL, name="dw_ple_proj")
        dg, du = _ffn_bwd_act(dy2b, full["ffn_w_out"], sv["g"], sv["u"], tm=_tile(s, SUB_ROWS), tn=D_FF,
                              name="ffn_bwd_act")
        gl["ffn_w_out"] = mmw(sv["act"], dy2b, tm=FF_TILE, tn=D_MODEL, name="dw_ffn_out")
        dy1, dy1b, small["ln_mix_g"][i], small["ln_mix_b"][i] = _dx_from_pieces(
            [dg, du], full["ffn_w_in"], dy2, tm=_tile(s, SUB_ROWS), ln=(sv["pre1"], w["ln_mix_g"][i:i + 1]),
            name="ffn_bwd_x_ln")
        gl["ffn_w_in"] = mmw(sv["x1b"], [dg, du], tm=D_MODEL, tn=FF_TILE, name="dw_ffn_in")
        n_out, n_inw = ("att_w_o", "att_w_qkv") if i % 2 == 0 else ("hgrn_w_o", "hgrn_w_in")
        do = _mm(dy1b, full[n_out], tm=tbig, tn=D_MODEL, tk=D_MODEL, tb=True, out_dtype=BF16, name="mix_out_bwd")
        gl[n_out] = mmw(sv["o"], dy1b, tm=D_MODEL, tn=D_MODEL, name="dw_mix_out")
        early = _pack_full(gl, _layer_params(i)[1:], w)
        comm = (early if above is None else jnp.concatenate([above, early], axis=1), False)
        if i % 2 == 0:
            dzq, part, dsk, *more = _att_bwd(sv["z"], do, w["att_sink"][j], cos, sin, comm=comm, name="att_bwd")
            dz = [dzq, _att_bwd_kv(part, cos, sin, name="att_bwd_kv")]
            dsink[j] = dsk[:, 0]
        else:
            dsum, dgate, dnorm[j] = _hg_post_bwd(do, sv["of"], sv["ob"], sv["z"], w["hgrn_norm_g"][j].reshape(1, HG_DIM),
                                                 tm=tm, name="hgrn_post_bwd")
            dq1, df1, dv1, dlb1, *more = _hg_bwd(sv["z"], sv["lbl"][0], sv["st_f"], sv["qs_f"], sv["dec_f"], dsum, None,
                                                 None, rev=False, ts=ts, comm=comm, name="hgrn_bwd")
            dq2, df2, dv2, dlb2 = _hg_bwd(sv["z"], sv["lbl"][1], sv["st_b"], sv["qs_b"], sv["dec_b"], dsum, dq1, dv1,
                                          rev=True, ts=ts, name="hgrn_bwd_rev")
            dz = [dq2, df1, df2, dv2, dgate]
            dlb_rows[2 * j] = dlb1.reshape(1, D_MODEL)
            dlb_rows[2 * j + 1] = dlb2.reshape(1, D_MODEL)
        if above is not None:
            recv_late[i + 1] = (more[0], 0)
        recv_early[i] = (more[0], 0 if above is None else above.shape[1])
        dx = _dx_from_pieces(dz, full[n_inw], dy1, tm=tm, name="mix_in_bwd")
        gl[n_inw] = mmw(sv["xb"], dz, tm=D_MODEL, tn=512, name="dw_mix_in")
        above = _pack_full(gl, _layer_params(i)[:1], w)
    grad_x = dx.reshape(x.shape)
    recv_late[0] = (_exchange(above, name="exchange_grads"), 0)

    big_out = [{n: [None] * w[n].shape[0] for n, _ in BIG} for _ in range(4)]
    for i in range(DEPTH):
        for params, (got, off) in ((_layer_params(i)[:1], recv_late[i]), (_layer_params(i)[1:], recv_early[i])):
            w_part = _pack_local(w, params)
            outs = _adamw(w_part, got, _pack_local(mom, params), _pack_local(var, params),
                          tr=_row_tile(math.gcd(w_part.shape[0], off)), g_off=off, name="adamw_big")
            for kind, packed in enumerate(outs):
                for (n, j, _), piece in zip(params, _unpack_local(packed, params, w).values()):
                    big_out[kind][n][j] = piece
    big_out = [{n: jnp.stack(v) for n, v in kind.items()} for kind in big_out]

    small_rows = jnp.concatenate(
        [jnp.concatenate(small[n], axis=0) for n in ("ln_mix_g", "ln_mix_b", "ln_ffn_g", "ln_ffn_b")] + dlb_rows
        + [_pad_row(jnp.stack(dnorm)), _pad_row(jnp.stack(dsink)), jnp.zeros((2, D_MODEL), F32)], axis=0)
    small_all = _gather(small_rows, name="gather_small")
    lbw, lbm, lbv = (t.reshape(4, 2 * HG_DIM) for t in (hgrn_lb_logits, mom["hgrn_lb_logits"], var["hgrn_lb_logits"]))
    summed = _sum8(small_all, name="sum_small")
    dlb_mine = lax.dynamic_slice_in_dim(summed[16:20].reshape(2, 2, HG_HEADS, HG_DIM), me, 1, axis=2)
    dlogits = _lb_bwd(lbw, dlb_mine.reshape(2, 2 * HG_DIM), name="lb_bwd")

    def small_pack(ln4, lbt, ng, sk):
        return jnp.concatenate([ln4[n] for n in ("ln_mix_g", "ln_mix_b", "ln_ffn_g", "ln_ffn_b")]
                               + [_pad_row(lbt), _pad_row(ng), _pad_row(sk), jnp.zeros((5, D_MODEL), F32)], axis=0)

    g_small = jnp.concatenate([summed[:16], _pad_row(dlogits), summed[20:22], jnp.zeros((5, D_MODEL), F32)], axis=0)
    souts = _adamw(small_pack(w, lbw, w["hgrn_norm_g"], w["att_sink"]), g_small,
                   small_pack(mom, lbm, mom["hgrn_norm_g"], mom["att_sink"]),
                   small_pack(var, lbv, var["hgrn_norm_g"], var["att_sink"]), tr=SMALL_ROWS, name="adamw_small")

    def small_unpack(t):
        out = {n: t[4 * k:4 * k + 4] for k, n in enumerate(("ln_mix_g", "ln_mix_b", "ln_ffn_g", "ln_ffn_b"))}
        out["hgrn_lb_logits"] = t[16].reshape(hgrn_lb_logits.shape)
        out["hgrn_norm_g"] = t[17, :2 * HG_DIM].reshape(hgrn_norm_g.shape)
        out["att_sink"] = t[18, :2 * N_Q_HEADS].reshape(att_sink.shape)
        return out

    result = [loss, grad_x]
    for big_t, small_t in zip(big_out, souts):
        merged = dict(big_t)
        merged.update(small_unpack(small_t))
        result += [merged[n] for n in names]
    return tuple(result)
```
